```python
import math
import jax, jax.numpy as jnp
from jax import lax
import numpy as np

D_MODEL = 1024
BATCH = 4
SEQ = 4096
DEPTH = 2

A_HEADS = 8
A_HEAD_DIM = 64
A_WIDTH = A_HEADS * A_HEAD_DIM
DILATED_CONFIGS = ((128, 1), (512, 4), (2048, 16))
A_BLOCK = 128
A_SPAN = A_BLOCK * max(d for _, d in DILATED_CONFIGS)
B_WIDTH = 512
B_GROUP = 16
B_GROUPS = B_WIDTH // B_GROUP
B_STATE = 64
AB_WIDTH = A_WIDTH + B_WIDTH
AB_IN = 3 * A_WIDTH + B_WIDTH + AB_WIDTH
C_HEADS = 8
C_DK = 128
C_DV = 128
C_CONV = 4
C_CHUNK = 64
C_WIDTH = C_HEADS * C_DV
QKV_WIDTH = 2 * C_HEADS * C_DK + C_WIDTH
C_IN = QKV_WIDTH + C_WIDTH + 2 * C_HEADS
REL_BUCKETS = 32
REL_MAX_DIST = 2048
N_EVEN = (DEPTH + 1) // 2
N_ODD = DEPTH // 2
EPS = 1e-6

kernel_name = "hybrid_dilated_s5_gdn_block"


def rms_norm(x, g):
    xf = x.astype(jnp.float32)
    y = xf * lax.rsqrt(jnp.mean(xf * xf, axis=-1, keepdims=True) + EPS)
    return (y * g.astype(jnp.float32)).astype(x.dtype)


def t5_bucket_np(dist):
    dist = np.maximum(dist, 0)
    max_exact = REL_BUCKETS // 2
    large = max_exact + (np.log(np.maximum(dist, 1) / max_exact)
                         / math.log(REL_MAX_DIST / max_exact) * (REL_BUCKETS - max_exact)).astype(np.int32)
    large = np.minimum(large, REL_BUCKETS - 1)
    return np.where(dist < max_exact, dist, large).astype(np.int32)


def dilated_attention(q, k, v, rel_bias):
    bsz, s_len = q.shape[:2]
    sp = -(-s_len // A_SPAN) * A_SPAN
    pad = ((0, 0), (0, sp - s_len), (0, 0), (0, 0))
    q, k, v = jnp.pad(q, pad), jnp.pad(k, pad), jnp.pad(v, pad)
    scale = A_HEAD_DIM ** -0.5
    qi = np.arange(A_BLOCK)[:, None]
    kj = np.arange(2 * A_BLOCK)[None, :]
    rel = qi + A_BLOCK - kj
    neg = jnp.finfo(jnp.float32).min
    outs, lses = [], []
    for window, dil in DILATED_CONFIGS:
        n_keys = window // dil
        nb = sp // dil // A_BLOCK
        qb, kb, vb = (t.reshape(bsz, nb, A_BLOCK, dil, A_HEADS, A_HEAD_DIM) for t in (q, k, v))
        prev = lambda t: jnp.pad(t[:, :-1], ((0, 0), (1, 0), (0, 0), (0, 0), (0, 0), (0, 0)))
        kw = jnp.concatenate([prev(kb), kb], axis=2)
        vw = jnp.concatenate([prev(vb), vb], axis=2)
        bias = jnp.transpose(rel_bias[t5_bucket_np(rel * dil)], (2, 0, 1)).astype(jnp.float32)
        band = (rel >= 0) & (rel <= n_keys)
        first = band & (kj >= A_BLOCK)
        blk_mask = np.concatenate([first[None], np.broadcast_to(band, (nb - 1,) + band.shape)], 0)
        s = jnp.einsum('bnqrhd,bnkrhd->bnrhqk', qb, kw, preferred_element_type=jnp.float32) * scale
        s = jnp.where(blk_mask[None, :, None, None], s + bias[None, None, None], neg)
        m = jnp.max(s, axis=-1, keepdims=True)
        p = jnp.exp(s - m)
        den = jnp.sum(p, axis=-1)
        o = jnp.einsum('bnrhqk,bnkrhd->bnqrhd', p, vw.astype(jnp.float32))
        den_t = jnp.transpose(den, (0, 1, 4, 2, 3))
        lse = jnp.transpose(m[..., 0], (0, 1, 4, 2, 3)) + jnp.log(den_t)
        outs.append((o / den_t[..., None]).reshape(bsz, sp, A_HEADS, A_HEAD_DIM))
        lses.append(lse.reshape(bsz, sp, A_HEADS))
    w = jax.nn.softmax(jnp.stack(lses, 0), axis=0)
    out = jnp.einsum('cbsh,cbshd->bshd', w, jnp.stack(outs, 0))
    return out[:, :s_len].reshape(bsz, s_len, A_WIDTH)


def _complex_affine_combine(e1, e2):
    a1r, a1i, b1r, b1i = e1
    a2r, a2i, b2r, b2i = e2
    return (a2r * a1r - a2i * a1i, a2r * a1i + a2i * a1r,
            a2r * b1r - a2i * b1i + b2r, a2r * b1i + a2i * b1r + b2i)


def s5_layer(u, a_re, a_im, log_dt, b_re, b_im, c_re, c_im, d_skip, glu_w, glu_b):
    bsz, s_len = u.shape[:2]
    uf = u.astype(jnp.float32)
    ug = uf.reshape(bsz, s_len, B_GROUPS, B_GROUP)
    dt = jnp.exp(log_dt.astype(jnp.float32))[:, None]
    ar, ai = a_re.astype(jnp.float32), a_im.astype(jnp.float32)
    mag = jnp.exp(dt * ar)
    abar_r, abar_i = mag * jnp.cos(dt * ai), mag * jnp.sin(dt * ai)
    den = ar * ar + ai * ai
    fr = ((abar_r - 1.0) * ar + abar_i * ai) / den
    fi = (abar_i * ar - (abar_r - 1.0) * ai) / den
    br, bi = b_re.astype(jnp.float32), b_im.astype(jnp.float32)
    bbar_r = fr[..., None] * br - fi[..., None] * bi
    bbar_i = fr[..., None] * bi + fi[..., None] * br
    bu_r = jnp.einsum('bsgm,gpm->bsgp', ug, bbar_r)
    bu_i = jnp.einsum('bsgm,gpm->bsgp', ug, bbar_i)
    a_r = jnp.broadcast_to(abar_r, bu_r.shape)
    a_i = jnp.broadcast_to(abar_i, bu_i.shape)
    _, _, x_r, x_i = lax.associative_scan(_complex_affine_combine, (a_r, a_i, bu_r, bu_i), axis=1)
    y = (jnp.einsum('gmp,bsgp->bsgm', c_re.astype(jnp.float32), x_r)
         - jnp.einsum('gmp,bsgp->bsgm', c_im.astype(jnp.float32), x_i))
    y = jax.nn.gelu(y.reshape(bsz, s_len, B_WIDTH) + d_skip.astype(jnp.float32) * uf)
    return (y * jax.nn.sigmoid(y @ glu_w.astype(jnp.float32) + glu_b.astype(jnp.float32))).astype(u.dtype)


def ab_mixer(h, w_in, w_out, rel_bias, a_re, a_im, log_dt, b_re, b_im, c_re, c_im, d_skip, glu_w, glu_b):
    bsz, s_len, _ = h.shape
    z = h @ w_in
    q, k, v, u, gate = jnp.split(z, [A_WIDTH, 2 * A_WIDTH, 3 * A_WIDTH, 3 * A_WIDTH + B_WIDTH], axis=-1)
    hs = (bsz, s_len, A_HEADS, A_HEAD_DIM)
    o_a = dilated_attention(q.reshape(hs), k.reshape(hs), v.reshape(hs), rel_bias).astype(h.dtype)
    o_b = s5_layer(u, a_re, a_im, log_dt, b_re, b_im, c_re, c_im, d_skip, glu_w, glu_b)
    o = jnp.concatenate([o_a, o_b], axis=-1) * jax.nn.silu(gate)
    return o @ w_out


def l2_normalize(t):
    return t * lax.rsqrt(jnp.sum(t * t, axis=-1, keepdims=True) + EPS)


def gdn_mixer(h, w_in, conv_w, a_log, dt_bias, norm_g, w_out):
    bsz, s_len, _ = h.shape
    z = h @ w_in
    qkv, gate, beta_raw, a_raw = jnp.split(z, [QKV_WIDTH, QKV_WIDTH + C_WIDTH, QKV_WIDTH + C_WIDTH + C_HEADS], axis=-1)
    qkv = lax.conv_general_dilated(qkv, conv_w[:, None, :].astype(qkv.dtype), (1,), [(C_CONV - 1, 0)],
                                   dimension_numbers=('NWC', 'WIO', 'NWC'), feature_group_count=QKV_WIDTH)
    qkv = jax.nn.silu(qkv).astype(jnp.float32)
    q, k, v = jnp.split(qkv, [C_HEADS * C_DK, 2 * C_HEADS * C_DK], axis=-1)
    q = l2_normalize(q.reshape(bsz, s_len, C_HEADS, C_DK)) * (C_DK ** -0.5)
    k = l2_normalize(k.reshape(bsz, s_len, C_HEADS, C_DK))
    v = v.reshape(bsz, s_len, C_HEADS, C_DV)
    beta = jax.nn.sigmoid(beta_raw.astype(jnp.float32))
    g = -jnp.exp(a_log.astype(jnp.float32)) * jax.nn.softplus(a_raw.astype(jnp.float32) + dt_bias.astype(jnp.float32))
    n_chunks = s_len // C_CHUNK

    def chunks(t):
        return jnp.moveaxis(t.reshape(bsz, n_chunks, C_CHUNK, C_HEADS, *t.shape[3:]), 3, 1)

    qc, kc, vc, bc = chunks(q), chunks(k), chunks(v), chunks(beta)
    gc = jnp.cumsum(chunks(g), axis=-1)
    idx = np.arange(C_CHUNK)
    tril = idx[:, None] >= idx[None, :]
    strict = idx[:, None] > idx[None, :]
    decay = jnp.exp(jnp.where(tril, gc[..., :, None] - gc[..., None, :], -jnp.inf))
    kb = kc * bc[..., None]
    lower = jnp.where(strict, jnp.einsum('bhnid,bhnjd->bhnij', kb, kc) * decay, 0.0)
    tri = lower + jnp.eye(C_CHUNK, dtype=lower.dtype)
    u_c = lax.linalg.triangular_solve(tri, vc * bc[..., None], left_side=True, lower=True, unit_diagonal=True)
    w_c = lax.linalg.triangular_solve(tri, kb * jnp.exp(gc)[..., None], left_side=True, lower=True, unit_diagonal=True)
    aqk = jnp.einsum('bhnid,bhnjd->bhnij', qc, kc) * decay
    qg = qc * jnp.exp(gc)[..., None]
    g_last = gc[..., -1]
    kd = kc * jnp.exp(g_last[..., None] - gc)[..., None]
    xs = tuple(jnp.moveaxis(t, 2, 0) for t in (w_c, u_c, qg, kd, aqk, jnp.exp(g_last)))

    def step(state, xs_c):
        w_i, u_i, qg_i, kd_i, aqk_i, dec_i = xs_c
        v_new = u_i - jnp.einsum('bhik,bhkv->bhiv', w_i, state)
        o_i = jnp.einsum('bhik,bhkv->bhiv', qg_i, state) + jnp.einsum('bhij,bhjv->bhiv', aqk_i, v_new)
        state = state * dec_i[..., None, None] + jnp.einsum('bhik,bhiv->bhkv', kd_i, v_new)
        return state, o_i

    s0 = jnp.zeros((bsz, C_HEADS, C_DK, C_DV), jnp.float32)
    _, o = lax.scan(step, s0, xs)
    o = jnp.transpose(o, (1, 0, 3, 2, 4)).reshape(bsz, s_len, C_HEADS, C_DV)
    o = rms_norm(o, norm_g).reshape(bsz, s_len, C_WIDTH).astype(h.dtype) * jax.nn.silu(gate)
    return o @ w_out


def setup_inputs(seed: int = 0) -> dict:
    key = jax.random.key(seed)
    ks = iter(jax.random.split(key, 32))
    nrm = lambda shape, s: jax.random.normal(next(ks), shape, jnp.float32) * s
    uni = lambda shape, lo, hi: jax.random.uniform(next(ks), shape, jnp.float32, lo, hi)
    n_idx = jnp.arange(B_STATE, dtype=jnp.float32)
    dt_c = uni((N_ODD, C_HEADS), 0.001, 0.1)
    return {
        "x": nrm((BATCH, SEQ, D_MODEL), 1.0),
        "c": nrm((BATCH, D_MODEL), 1.0),
        "ada_w": nrm((DEPTH, D_MODEL, 3 * D_MODEL), 0.3 * D_MODEL ** -0.5),
        "ada_b": nrm((DEPTH, 3 * D_MODEL), 0.02),
        "pre_g": 1.0 + nrm((DEPTH, D_MODEL), 0.02),
        "post_g": 1.0 + nrm((DEPTH, D_MODEL), 0.02),
        "rel_bias": nrm((REL_BUCKETS, A_HEADS), 0.5),
        "ab_w_in": nrm((N_EVEN, D_MODEL, AB_IN), D_MODEL ** -0.5),
        "ab_w_out": nrm((N_EVEN, AB_WIDTH, D_MODEL), AB_WIDTH ** -0.5),
        "s5_a_re": -0.5 + nrm((N_EVEN, B_GROUPS, B_STATE), 0.01),
        "s5_a_im": math.pi * n_idx + nrm((N_EVEN, B_GROUPS, B_STATE), 0.01),
        "s5_log_dt": uni((N_EVEN, B_GROUPS), math.log(0.001), math.log(0.1)),
        "s5_b_re": nrm((N_EVEN, B_GROUPS, B_STATE, B_GROUP), (2 * B_GROUP) ** -0.5),
        "s5_b_im": nrm((N_EVEN, B_GROUPS, B_STATE, B_GROUP), (2 * B_GROUP) ** -0.5),
        "s5_c_re": nrm((N_EVEN, B_GROUPS, B_GROUP, B_STATE), (2 * B_STATE) ** -0.5),
        "s5_c_im": nrm((N_EVEN, B_GROUPS, B_GROUP, B_STATE), (2 * B_STATE) ** -0.5),
        "s5_d": nrm((N_EVEN, B_WIDTH), 1.0),
        "s5_glu_w": nrm((N_EVEN, B_WIDTH, B_WIDTH), B_WIDTH ** -0.5),
        "s5_glu_b": nrm((N_EVEN, B_WIDTH), 0.02),
        "gdn_w_in": nrm((N_ODD, D_MODEL, C_IN), D_MODEL ** -0.5),
        "gdn_conv": nrm((N_ODD, C_CONV, QKV_WIDTH), C_CONV ** -0.5),
        "gdn_a_log": jnp.log(uni((N_ODD, C_HEADS), 1.0, 16.0)),
        "gdn_dt_bias": dt_c + jnp.log(-jnp.expm1(-dt_c)),
        "gdn_norm_g": 1.0 + nrm((N_ODD, C_DV), 0.02),
        "gdn_w_out": nrm((N_ODD, C_WIDTH, D_MODEL), C_WIDTH ** -0.5),
    }


def reference(x, c, ada_w, ada_b, pre_g, post_g, rel_bias, ab_w_in, ab_w_out,
              s5_a_re, s5_a_im, s5_log_dt, s5_b_re, s5_b_im, s5_c_re, s5_c_im, s5_d, s5_glu_w, s5_glu_b,
              gdn_w_in, gdn_conv, gdn_a_log, gdn_dt_bias, gdn_norm_g, gdn_w_out):
    c_act = jax.nn.silu(c)
    for layer in range(DEPTH):
        mod = c_act @ ada_w[layer] + ada_b[layer]
        shift, scale, gate = jnp.split(mod, 3, axis=-1)
        h = rms_norm(x, pre_g[layer]) * (1.0 + scale[:, None]) + shift[:, None]
        j = layer // 2
        if layer % 2 == 0:
            y = ab_mixer(h, ab_w_in[j], ab_w_out[j], rel_bias, s5_a_re[j], s5_a_im[j], s5_log_dt[j],
                         s5_b_re[j], s5_b_im[j], s5_c_re[j], s5_c_im[j], s5_d[j], s5_glu_w[j], s5_glu_b[j])
        else:
            y = gdn_mixer(h, gdn_w_in[j], gdn_conv[j], gdn_a_log[j], gdn_dt_bias[j], gdn_norm_g[j], gdn_w_out[j])
        x = x + gate[:, None] * rms_norm(y, post_g[layer])
    return x
```

```python
import functools
import math

import numpy as np
import jax
import jax.numpy as jnp
from jax import lax
from jax.experimental import pallas as pl
from jax.experimental.pallas import tpu as pltpu

F32 = jnp.float32
BF16 = jnp.bfloat16
HI = lax.Precision.HIGHEST

EPS = 1e-6
A_HEADS = 8
A_HEAD_DIM = 64
A_WIDTH = A_HEADS * A_HEAD_DIM
A_BLOCK = 128
DILATED_CONFIGS = ((128, 1), (512, 4), (2048, 16))
B_GROUP = 16
B_STATE = 64
REL_BUCKETS = 32
REL_MAX_DIST = 2048
C_HEADS = 8
C_DK = 128
C_DV = 128
C_CONV = 4
MASK_NEG = -1e30

LANES = 128
SUBLANES = 8
VMEM_LIMIT = 48 * 1024 * 1024
ROW_TILE = 512
COL_CHUNK = 512
S5_J = 32
S5_LW = 512
GDN_STEP = 128
GDN_CHUNK = 64

_NT = (((1,), (1,)), ((), ()))


def _cparams(sem):
    return pltpu.CompilerParams(dimension_semantics=sem, vmem_limit_bytes=VMEM_LIMIT)


def _sigmoid(x):
    return 1.0 / (1.0 + jnp.exp(-x))


def _mod_kernel(c_ref, w_ref, b_ref, o_ref):
    c = c_ref[...]
    ca = c * _sigmoid(c)
    o_ref[0] = jnp.dot(ca, w_ref[0], preferred_element_type=F32, precision=HI) + b_ref[0]


def _adaln_mod(c, ada_w, ada_b):
    depth, d, d3 = ada_w.shape
    bsz = c.shape[0]
    return pl.pallas_call(
        _mod_kernel,
        out_shape=jax.ShapeDtypeStruct((depth, bsz, d3), F32),
        grid=(depth, d3 // d),
        in_specs=[pl.BlockSpec((bsz, d), lambda l, j: (0, 0)),
                  pl.BlockSpec((1, d, d), lambda l, j: (l, 0, j)),
                  pl.BlockSpec((1, 1, d), lambda l, j: (l, 0, j))],
        out_specs=pl.BlockSpec((1, bsz, d), lambda l, j: (l, 0, j)),
        compiler_params=_cparams(("arbitrary", "arbitrary")),
        name="adaln_mod",
    )(c, ada_w, ada_b.reshape(depth, 1, d3))


def _in_proj_kernel(x_ref, mod_ref, g_ref, w_ref, *out_refs, splits, d_model):
    x = x_ref[...]
    ms = jnp.mean(x * x, axis=-1, keepdims=True)
    y = x * lax.rsqrt(ms + EPS) * g_ref[...]
    shift = mod_ref[0, :, 0:d_model]
    scale = mod_ref[0, :, d_model:2 * d_model]
    h = (y * (1.0 + scale) + shift).astype(BF16)
    for (c0, width, mult), o_ref in zip(splits, out_refs):
        for cc in range(0, width, COL_CHUNK):
            cw = min(COL_CHUNK, width - cc)
            acc = jnp.dot(h, w_ref[:, c0 + cc:c0 + cc + cw], preferred_element_type=F32)
            if mult != 1.0:
                acc = acc * mult
            o_ref[:, cc:cc + cw] = acc.astype(o_ref.dtype)


def _in_proj(x2, mod3, gain, w_bf16, splits, out_dtypes, seq):
    t, d = x2.shape
    tiles_per_seq = seq // ROW_TILE
    n_w = w_bf16.shape[1]
    return pl.pallas_call(
        functools.partial(_in_proj_kernel, splits=splits, d_model=d),
        out_shape=[jax.ShapeDtypeStruct((t, wd), dt) for (_, wd, _), dt in zip(splits, out_dtypes)],
        grid=(t // ROW_TILE,),
        in_specs=[pl.BlockSpec((ROW_TILE, d), lambda i: (i, 0)),
                  pl.BlockSpec((1, 1, 3 * d), lambda i: (i // tiles_per_seq, 0, 0)),
                  pl.BlockSpec((1, d), lambda i: (0, 0)),
                  pl.BlockSpec((d, n_w), lambda i: (0, 0))],
        out_specs=[pl.BlockSpec((ROW_TILE, wd), lambda i: (i, 0)) for (_, wd, _) in splits],
        compiler_params=_cparams(("arbitrary",)),
        name="in_proj",
    )(x2, mod3, gain.reshape(1, d), w_bf16)


def _t5_bucket(dist):
    dist = np.maximum(dist, 0)
    max_exact = REL_BUCKETS // 2
    large = max_exact + (np.log(np.maximum(dist, 1) / max_exact)
                         / math.log(REL_MAX_DIST / max_exact) * (REL_BUCKETS - max_exact)).astype(np.int32)
    large = np.minimum(large, REL_BUCKETS - 1)
    return np.where(dist < max_exact, dist, large).astype(np.int32)


def _attn_bias_table(rel_bias, window, dil):
    qi = np.arange(A_BLOCK)[:, None]
    kj = np.arange(2 * A_BLOCK)[None, :]
    rel = qi + A_BLOCK - kj
    n_keys = window // dil
    bias = jnp.transpose(rel_bias[_t5_bucket(rel * dil)], (2, 0, 1)).astype(F32)
    band = (rel >= 0) & (rel <= n_keys)
    first = band & (kj >= A_BLOCK)
    return jnp.stack([jnp.where(first[None], bias, MASK_NEG), jnp.where(band[None], bias, MASK_NEG)])


def _attn_kernel(q_ref, kp_ref, kc_ref, vp_ref, vc_ref, bias_ref, o_ref, lse_ref):
    q = q_ref[0]
    kcat = jnp.concatenate([kp_ref[0], kc_ref[0]], axis=0)
    vcat = jnp.concatenate([vp_ref[0], vc_ref[0]], axis=0)
    lane = lax.broadcasted_iota(jnp.int32, (A_BLOCK, LANES), 1)
    low = lane < A_HEAD_DIM
    lse_tile = jnp.zeros((A_BLOCK, LANES), F32)
    zero = jnp.zeros((A_BLOCK, LANES), BF16)
    for hp in range(A_HEADS // 2):
        sl = slice(hp * LANES, (hp + 1) * LANES)
        qp, kp, vp = q[:, sl], kcat[:, sl], vcat[:, sl]
        outs = []
        for e in range(2):
            h = 2 * hp + e
            qm = jnp.where(low if e == 0 else jnp.logical_not(low), qp, zero)
            s = lax.dot_general(qm, kp, _NT, preferred_element_type=F32) + bias_ref[0, h]
            m = jnp.max(s, axis=-1, keepdims=True)
            p = jnp.exp(s - m)
            l = jnp.sum(p, axis=-1, keepdims=True)
            o = jnp.dot(p.astype(BF16), vp, preferred_element_type=F32)
            outs.append(o / l)
            lse_tile = jnp.where(lane == h, m + jnp.log(l), lse_tile)
        o_ref[0, :, sl] = jnp.where(low, outs[0], outs[1])
    lse_ref[0] = lse_tile


def _dilated_attention_one(q, k, v, bias_tab, bsz, seq, dil):
    nb = seq // dil // A_BLOCK
    w = A_WIDTH
    view = lambda a: a.reshape(bsz, seq // dil, dil * a.shape[-1])
    cur = lambda b, r, n: (b, n, r)
    prev = lambda b, r, n: (b, jnp.maximum(n - 1, 0), r)
    blk = pl.BlockSpec((1, A_BLOCK, w), cur)
    blk_prev = pl.BlockSpec((1, A_BLOCK, w), prev)
    o, lse = pl.pallas_call(
        _attn_kernel,
        out_shape=[jax.ShapeDtypeStruct((bsz, seq // dil, dil * w), F32),
                   jax.ShapeDtypeStruct((bsz, seq // dil, dil * LANES), F32)],
        grid=(bsz, dil, nb),
        in_specs=[blk, blk_prev, blk, blk_prev, blk,
                  pl.BlockSpec((1, A_HEADS, A_BLOCK, 2 * A_BLOCK), lambda b, r, n: (jnp.minimum(n, 1), 0, 0, 0))],
        out_specs=[pl.BlockSpec((1, A_BLOCK, w), cur), pl.BlockSpec((1, A_BLOCK, LANES), cur)],
        compiler_params=_cparams(("arbitrary", "arbitrary", "arbitrary")),
        name=f"dilated_attn_d{dil}",
    )(view(q), view(k), view(k), view(v), view(v), bias_tab)
    return o.reshape(bsz * seq, w), lse.reshape(bsz * seq, LANES)


def _gelu_tanh(x):
    c = math.sqrt(2.0 / math.pi)
    return x * (0.5 * (1.0 + jnp.tanh(c * (x + 0.044715 * (x * x * x)))))


def _s5_kernel(u_ref, pm_ref, pmt_ref, bd_ref, pw_ref, cd_ref, dsk_ref, gw_ref, gb_ref, o_ref, x_scr, st_scr, *,
               n_state):
    n = n_state
    jn = S5_J

    @pl.when(pl.program_id(1) == 0)
    def _():
        st_scr[...] = jnp.zeros_like(st_scr)

    u = u_ref[...]
    u_perm = jnp.dot(pm_ref[...], u.astype(BF16), preferred_element_type=F32).astype(BF16)
    x_scr[...] = jnp.dot(u_perm, bd_ref[...], preferred_element_type=F32)

    for c0 in range(0, n, S5_LW):
        re = slice(c0, c0 + S5_LW)
        im = slice(n + c0, n + c0 + S5_LW)
        a1r, a1i = pw_ref[0:1, re], pw_ref[0:1, im]
        xr = jnp.zeros((SUBLANES, S5_LW), F32)
        xi = jnp.zeros((SUBLANES, S5_LW), F32)
        for j in range(jn):
            rows = slice(SUBLANES * j, SUBLANES * (j + 1))
            nr = a1r * xr - a1i * xi + x_scr[rows, re]
            ni = a1r * xi + a1i * xr + x_scr[rows, im]
            xr, xi = nr, ni
            x_scr[rows, re] = xr
            x_scr[rows, im] = xi
        ajr, aji = pw_ref[jn - 1:jn, re], pw_ref[jn - 1:jn, im]
        pr, pi = st_scr[0:1, re], st_scr[0:1, im]
        cin_r, cin_i = [], []
        for s in range(SUBLANES):
            cin_r.append(pr)
            cin_i.append(pi)
            nr = xr[s:s + 1] + ajr * pr - aji * pi
            ni = xi[s:s + 1] + ajr * pi + aji * pr
            pr, pi = nr, ni
        st_scr[0:1, re] = pr
        st_scr[0:1, im] = pi
        cr = jnp.concatenate(cin_r, axis=0)
        ci = jnp.concatenate(cin_i, axis=0)
        for j in range(jn):
            rows = slice(SUBLANES * j, SUBLANES * (j + 1))
            pjr, pji = pw_ref[j:j + 1, re], pw_ref[j:j + 1, im]
            x_scr[rows, re] = x_scr[rows, re] + (pjr * cr - pji * ci)
            x_scr[rows, im] = x_scr[rows, im] + (pjr * ci + pji * cr)

    y_perm = jnp.dot(x_scr[...].astype(BF16), cd_ref[...], preferred_element_type=F32)
    y_hi = y_perm.astype(BF16)
    y_lo = (y_perm - y_hi.astype(F32)).astype(BF16)
    y = (jnp.dot(pmt_ref[...], y_hi, preferred_element_type=F32)
         + jnp.dot(pmt_ref[...], y_lo, preferred_element_type=F32))
    y = _gelu_tanh(y + dsk_ref[...] * u)
    z = jnp.dot(y.astype(BF16), gw_ref[...], preferred_element_type=F32) + gb_ref[...]
    o_ref[...] = y * _sigmoid(z)


def _s5_tables(a_re, a_im, log_dt, b_re, b_im, c_re, c_im):
    g, p = a_re.shape
    dt = jnp.exp(log_dt.astype(F32))[:, None]
    ar, ai = a_re.astype(F32), a_im.astype(F32)
    mag = jnp.exp(dt * ar)
    abar_r, abar_i = mag * jnp.cos(dt * ai), mag * jnp.sin(dt * ai)
    den = ar * ar + ai * ai
    fr = ((abar_r - 1.0) * ar + abar_i * ai) / den
    fi = (abar_i * ar - (abar_r - 1.0) * ai) / den
    br, bi = b_re.astype(F32), b_im.astype(F32)
    bbar_r = fr[..., None] * br - fi[..., None] * bi
    bbar_i = fr[..., None] * bi + fi[..., None] * br
    eye = jnp.eye(g, dtype=F32)
    m = br.shape[-1]
    dense_b = lambda t: jnp.einsum('gpm,gh->gmhp', t, eye).reshape(g * m, g * p)
    bd = jnp.concatenate([dense_b(bbar_r), dense_b(bbar_i)], axis=1)
    dense_c = lambda t: jnp.einsum('gmp,gh->gphm', t, eye).reshape(g * p, g * m)
    cd = jnp.concatenate([dense_c(c_re.astype(F32)), -dense_c(c_im.astype(F32))], axis=0)
    kk = jnp.arange(1, S5_J + 1, dtype=F32)[:, None, None]
    pmag = jnp.exp(kk * (dt * ar)[None])
    pw_r = (pmag * jnp.cos(kk * (dt * ai)[None])).reshape(S5_J, g * p)
    pw_i = (pmag * jnp.sin(kk * (dt * ai)[None])).reshape(S5_J, g * p)
    pw = jnp.concatenate([pw_r, pw_i], axis=1)
    return bd.astype(BF16), pw, cd.astype(BF16)


def _s5_layer(u, bsz, seq, a_re, a_im, log_dt, b_re, b_im, c_re, c_im, d_skip, glu_w, glu_b):
    t, width = u.shape
    n = a_re.shape[0] * a_re.shape[1]
    bd, pw, cd = _s5_tables(a_re, a_im, log_dt, b_re, b_im, c_re, c_im)
    tile = SUBLANES * S5_J
    tiles_per_seq = seq // tile
    const = lambda b, i: (0, 0)
    src = (np.arange(tile) % SUBLANES) * S5_J + np.arange(tile) // SUBLANES
    perm = (src[:, None] == np.arange(tile)[None, :]).astype(np.float32)
    return pl.pallas_call(
        functools.partial(_s5_kernel, n_state=n),
        out_shape=jax.ShapeDtypeStruct((t, width), F32),
        grid=(bsz, tiles_per_seq),
        in_specs=[pl.BlockSpec((tile, width), lambda b, i: (b * tiles_per_seq + i, 0)),
                  pl.BlockSpec((tile, tile), const),
                  pl.BlockSpec((tile, tile), const),
                  pl.BlockSpec((width, 2 * n), const),
                  pl.BlockSpec((S5_J, 2 * n), const),
                  pl.BlockSpec((2 * n, width), const),
                  pl.BlockSpec((1, width), const),
                  pl.BlockSpec((width, width), const),
                  pl.BlockSpec((1, width), const)],
        out_specs=pl.BlockSpec((tile, width), lambda b, i: (b * tiles_per_seq + i, 0)),
        scratch_shapes=[pltpu.VMEM((tile, 2 * n), F32), pltpu.VMEM((SUBLANES, 2 * n), F32)],
        compiler_params=_cparams(("arbitrary", "arbitrary")),
        name="s5_layer",
    )(u, jnp.asarray(perm, BF16), jnp.asarray(perm.T, BF16), bd, pw, cd,
      d_skip.reshape(1, width).astype(F32), glu_w.astype(BF16),
      glu_b.reshape(1, width).astype(F32))


def _finish(y, x_ref, mod_ref, pg_ref, out_ref, d_model):
    ms = jnp.mean(y * y, axis=-1, keepdims=True)
    yn = y * lax.rsqrt(ms + EPS) * pg_ref[...]
    gate_mod = mod_ref[0, :, 2 * d_model:3 * d_model]
    out_ref[...] = x_ref[...] + gate_mod * yn


def _out_ab_kernel(o0_ref, o1_ref, o2_ref, l0_ref, l1_ref, l2_ref, ob_ref, gate_ref, e_ref, w_ref,
                   x_ref, mod_ref, pg_ref, out_ref, *, d_model):
    l0, l1, l2 = l0_ref[...], l1_ref[...], l2_ref[...]
    mx = jnp.maximum(jnp.maximum(l0, l1), l2)
    e0, e1, e2 = jnp.exp(l0 - mx), jnp.exp(l1 - mx), jnp.exp(l2 - mx)
    den = e0 + e1 + e2
    expand = lambda wgt: jnp.dot(wgt / den, e_ref[...], preferred_element_type=F32, precision=HI)
    o_a = expand(e0) * o0_ref[...] + expand(e1) * o1_ref[...] + expand(e2) * o2_ref[...]
    gate = gate_ref[...].astype(F32)
    sg = gate * _sigmoid(gate)
    aw = o_a.shape[-1]
    y = jnp.dot((o_a * sg[:, :aw]).astype(BF16), w_ref[0:aw, :], preferred_element_type=F32)
    y = y + jnp.dot((ob_ref[...] * sg[:, aw:]).astype(BF16), w_ref[aw:, :], preferred_element_type=F32)
    _finish(y, x_ref, mod_ref, pg_ref, out_ref, d_model)


def _out_c_kernel(o_ref, gate_ref, w_ref, x_ref, mod_ref, pg_ref, out_ref, *, d_model):
    gate = gate_ref[...].astype(F32)
    o = o_ref[...].astype(F32) * (gate * _sigmoid(gate))
    y = jnp.dot(o.astype(BF16), w_ref[...], preferred_element_type=F32)
    _finish(y, x_ref, mod_ref, pg_ref, out_ref, d_model)


def _out_proj(kern, row_inputs, const_inputs, w_bf16, x2, mod3, post_g, seq, name):
    t, d = x2.shape
    tiles_per_seq = seq // ROW_TILE
    row_spec = lambda a: pl.BlockSpec((ROW_TILE, a.shape[1]), lambda i: (i, 0))
    const_spec = lambda a: pl.BlockSpec(a.shape, lambda i: (0, 0))
    return pl.pallas_call(
        functools.partial(kern, d_model=d),
        out_shape=jax.ShapeDtypeStruct((t, d), F32),
        grid=(t // ROW_TILE,),
        in_specs=([row_spec(a) for a in row_inputs] + [const_spec(a) for a in const_inputs]
                  + [const_spec(w_bf16), row_spec(x2),
                     pl.BlockSpec((1, 1, 3 * d), lambda i: (i // tiles_per_seq, 0, 0)),
                     pl.BlockSpec((1, d), lambda i: (0, 0))]),
        out_specs=pl.BlockSpec((ROW_TILE, d), lambda i: (i, 0)),
        compiler_params=_cparams(("arbitrary",)),
        name=name,
    )(*row_inputs, *const_inputs, w_bf16, x2, mod3, post_g.reshape(1, d))


def _gdn_prep_kernel(x_ref, halo_ref, br_ref, cw_ref, alog_ref, dtb_ref,
                     w_ref, u_ref, qg_ref, kdt_ref, aqk_ref, dec_ref, xs_scr):
    n = GDN_STEP
    first = pl.program_id(1) == 0
    halo = halo_ref[...]
    xs_scr[0:SUBLANES, :] = jnp.where(first, jnp.zeros_like(halo), halo)
    xs_scr[SUBLANES:SUBLANES + n, :] = x_ref[...]
    cw = cw_ref[...]
    conv = xs_scr[SUBLANES:SUBLANES + n, :] * cw[C_CONV - 1:C_CONV]
    for j in range(C_CONV - 1):
        off = SUBLANES - (C_CONV - 1) + j
        conv = conv + xs_scr[off:off + n, :] * cw[j:j + 1]
    act = conv * _sigmoid(conv)

    row = lax.broadcasted_iota(jnp.int32, (n, n), 0)
    col = lax.broadcasted_iota(jnp.int32, (n, n), 1)
    same = (row // GDN_CHUNK) == (col // GDN_CHUNK)
    tril = jnp.logical_and(same, row >= col)
    strict = jnp.logical_and(same, row > col)
    eye = (row == col).astype(F32)

    br = br_ref[...]
    beta_all = _sigmoid(br)
    xg = br + dtb_ref[...]
    softplus = jnp.maximum(xg, 0.0) + jnp.log(1.0 + jnp.exp(-jnp.abs(xg)))
    g_all = -jnp.exp(alog_ref[...]) * softplus
    gc_all = jnp.dot(tril.astype(F32), g_all, preferred_element_type=F32, precision=HI)
    gc_t = gc_all.T
    first_chunk = lax.broadcasted_iota(jnp.int32, (n, 1), 0) < GDN_CHUNK
    dec_row = lax.broadcasted_iota(jnp.int32, (SUBLANES, LANES), 0)

    for h in range(C_HEADS):
        sl = slice(h * LANES, (h + 1) * LANES)
        gcol = gc_all[:, C_HEADS + h:C_HEADS + h + 1]
        grow = gc_t[C_HEADS + h:C_HEADS + h + 1, :]
        decay = jnp.where(tril, jnp.exp(jnp.where(tril, gcol - grow, 0.0)), 0.0)
        beta = beta_all[:, h:h + 1]
        q = act[:, h * C_DK:(h + 1) * C_DK]
        k = act[:, C_HEADS * C_DK + h * C_DK:C_HEADS * C_DK + (h + 1) * C_DK]
        v = act[:, 2 * C_HEADS * C_DK + h * C_DV:2 * C_HEADS * C_DK + (h + 1) * C_DV]
        q = q * lax.rsqrt(jnp.sum(q * q, axis=-1, keepdims=True) + EPS) * (C_DK ** -0.5)
        k = k * lax.rsqrt(jnp.sum(k * k, axis=-1, keepdims=True) + EPS)
        kb = k * beta
        k16 = k.astype(BF16)
        a = jnp.where(strict, lax.dot_general(kb.astype(BF16), k16, _NT, preferred_element_type=F32) * decay, 0.0)
        aqk = lax.dot_general(q.astype(BF16), k16, _NT, preferred_element_type=F32) * decay
        inv = eye - a
        ak = a
        for _ in range(5):
            ak = jnp.dot(ak, ak, preferred_element_type=F32, precision=HI)
            inv = inv + jnp.dot(inv, ak, preferred_element_type=F32, precision=HI)
        eg = jnp.exp(gcol)
        rhs = jnp.concatenate([v * beta, kb * eg], axis=1)
        uw = jnp.dot(inv, rhs, preferred_element_type=F32, precision=HI)
        g_last = jnp.where(first_chunk, gcol[GDN_CHUNK - 1:GDN_CHUNK], gcol[n - 1:n])
        kd = k * jnp.exp(g_last - gcol)
        u_ref[:, sl] = uw[:, 0:C_DV]
        w_ref[:, sl] = uw[:, C_DV:].astype(BF16)
        qg_ref[:, sl] = (q * eg).astype(BF16)
        kdt_ref[:, sl] = kd.T.astype(BF16)
        aqk_ref[:, sl] = aqk.astype(BF16)
        dec_ref[0, :, sl] = jnp.where(dec_row < SUBLANES // 2, jnp.exp(gcol[GDN_CHUNK - 1:GDN_CHUNK]),
                                      jnp.exp(gcol[n - 1:n]))


def _gdn_rec_kernel(w_ref, u_ref, qg_ref, kdt_ref, aqk_ref, dec_ref, ng_ref, o_ref, s_scr):
    @pl.when(pl.program_id(1) == 0)
    def _():
        s_scr[...] = jnp.zeros_like(s_scr)

    half = SUBLANES // 2
    zeros = jnp.zeros((GDN_CHUNK, C_DV), BF16)
    for h in range(C_HEADS):
        sl = slice(h * LANES, (h + 1) * LANES)
        kdt = kdt_ref[:, sl]
        state = s_scr[h]
        for j in range(GDN_STEP // GDN_CHUNK):
            rows = slice(j * GDN_CHUNK, (j + 1) * GDN_CHUNK)
            wq = jnp.concatenate([w_ref[rows, sl], qg_ref[rows, sl]], axis=0)
            ws = jnp.dot(wq, state.astype(BF16), preferred_element_type=F32)
            v_new = (u_ref[rows, sl] - ws[0:GDN_CHUNK]).astype(BF16)
            v_pad = jnp.concatenate([v_new, zeros] if j == 0 else [zeros, v_new], axis=0)
            o = ws[GDN_CHUNK:] + jnp.dot(aqk_ref[rows, sl], v_pad, preferred_element_type=F32)
            dec = dec_ref[0, j * half:j * half + 1, sl]
            state = state * dec + jnp.dot(kdt, v_pad, preferred_element_type=F32)
            ms = jnp.mean(o * o, axis=-1, keepdims=True)
            o_ref[rows, sl] = o * lax.rsqrt(ms + EPS) * ng_ref[...]
        s_scr[h] = state


def _gdn_core(qkv_pre, br, bsz, seq, conv_w, a_log, dt_bias, norm_g):
    t, qkv_w = qkv_pre.shape
    hw = C_HEADS * C_DV
    steps_per_seq = seq // GDN_STEP
    halo_blocks = GDN_STEP // SUBLANES
    pad_row = lambda vec: jnp.zeros((1, LANES), F32).at[0, C_HEADS:2 * C_HEADS].set(vec.astype(F32))
    tok = lambda b, i: (b * steps_per_seq + i, 0)
    const = lambda b, i: (0, 0)
    tok_spec = pl.BlockSpec((GDN_STEP, hw), tok)
    dec_spec = pl.BlockSpec((1, SUBLANES, hw), lambda b, i: (b * steps_per_seq + i, 0, 0))
    w, u, qg, kdt, aqk, dec = pl.pallas_call(
        _gdn_prep_kernel,
        out_shape=[jax.ShapeDtypeStruct((t, hw), BF16), jax.ShapeDtypeStruct((t, hw), F32),
                   jax.ShapeDtypeStruct((t, hw), BF16), jax.ShapeDtypeStruct((t, hw), BF16),
                   jax.ShapeDtypeStruct((t, hw), BF16),
                   jax.ShapeDtypeStruct((t // GDN_STEP, SUBLANES, hw), F32)],
        grid=(bsz, steps_per_seq),
        in_specs=[pl.BlockSpec((GDN_STEP, qkv_w), tok),
                  pl.BlockSpec((SUBLANES, qkv_w),
                               lambda b, i: (jnp.maximum((b * steps_per_seq + i) * halo_blocks - 1, 0), 0)),
                  pl.BlockSpec((GDN_STEP, LANES), tok),
                  pl.BlockSpec((C_CONV, qkv_w), const),
                  pl.BlockSpec((1, LANES), const),
                  pl.BlockSpec((1, LANES), const)],
        out_specs=[tok_spec, tok_spec, tok_spec, tok_spec, tok_spec, dec_spec],
        scratch_shapes=[pltpu.VMEM((SUBLANES + GDN_STEP, qkv_w), F32)],
        compiler_params=_cparams(("arbitrary", "arbitrary")),
        name="gdn_prep",
    )(qkv_pre, qkv_pre, br, conv_w.astype(F32), pad_row(a_log), pad_row(dt_bias))
    return pl.pallas_call(
        _gdn_rec_kernel,
        out_shape=jax.ShapeDtypeStruct((t, hw), F32),
        grid=(bsz, steps_per_seq),
        in_specs=[tok_spec, tok_spec, tok_spec, tok_spec, tok_spec, dec_spec,
                  pl.BlockSpec((1, C_DV), const)],
        out_specs=tok_spec,
        scratch_shapes=[pltpu.VMEM((C_HEADS, C_DK, C_DV), F32)],
        compiler_params=_cparams(("arbitrary", "arbitrary")),
        name="gdn_recurrence",
    )(w, u, qg, kdt, aqk, dec, norm_g.reshape(1, C_DV).astype(F32))


def kernel(x, c, ada_w, ada_b, pre_g, post_g, rel_bias, ab_w_in, ab_w_out, s5_a_re, s5_a_im, s5_log_dt, s5_b_re, s5_b_im, s5_c_re, s5_c_im, s5_d, s5_glu_w, s5_glu_b, gdn_w_in, gdn_conv, gdn_a_log, gdn_dt_bias, gdn_norm_g, gdn_w_out):
    bsz, seq, d = x.shape
    depth = ada_w.shape[0]
    assert seq % (A_BLOCK * max(dl for _, dl in DILATED_CONFIGS)) == 0 and seq % ROW_TILE == 0
    x2 = x.reshape(bsz * seq, d)
    mod = _adaln_mod(c, ada_w, ada_b)
    b_width = s5_d.shape[-1]
    head_expand = jnp.asarray(
        (np.arange(LANES)[:, None] == (np.arange(A_WIDTH)[None, :] // A_HEAD_DIM)).astype(np.float32))

    for layer in range(depth):
        j = layer // 2
        mod3 = mod[layer].reshape(bsz, 1, 3 * d)
        if layer % 2 == 0:
            splits = ((0, A_WIDTH, A_HEAD_DIM ** -0.5), (A_WIDTH, A_WIDTH, 1.0), (2 * A_WIDTH, A_WIDTH, 1.0),
                      (3 * A_WIDTH, b_width, 1.0), (3 * A_WIDTH + b_width, A_WIDTH + b_width, 1.0))
            q, k, v, u, gate = _in_proj(x2, mod3, pre_g[layer], ab_w_in[j].astype(BF16), splits,
                                        (BF16, BF16, BF16, F32, BF16), seq)
            outs, lses = [], []
            for window, dil in DILATED_CONFIGS:
                o_c, lse_c = _dilated_attention_one(q, k, v, _attn_bias_table(rel_bias, window, dil), bsz, seq, dil)
                outs.append(o_c)
                lses.append(lse_c)
            o_b = _s5_layer(u, bsz, seq, s5_a_re[j], s5_a_im[j], s5_log_dt[j], s5_b_re[j], s5_b_im[j],
                            s5_c_re[j], s5_c_im[j], s5_d[j], s5_glu_w[j], s5_glu_b[j])
            x2 = _out_proj(_out_ab_kernel, outs + lses + [o_b, gate], [head_expand], ab_w_out[j].astype(BF16),
                           x2, mod3, post_g[layer], seq, "out_proj_ab")
        else:
            qkv_w = 2 * C_HEADS * C_DK + C_HEADS * C_DV
            gate_w = C_HEADS * C_DV
            w_in = gdn_w_in[j]
            w_pad = jnp.concatenate(
                [w_in, jnp.zeros((d, LANES - (w_in.shape[1] - qkv_w - gate_w)), w_in.dtype)], axis=1).astype(BF16)
            splits = ((0, qkv_w, 1.0), (qkv_w, gate_w, 1.0), (qkv_w + gate_w, LANES, 1.0))
            qkv_pre, gate, br = _in_proj(x2, mod3, pre_g[layer], w_pad, splits, (F32, BF16, F32), seq)
            o = _gdn_core(qkv_pre, br, bsz, seq, gdn_conv[j], gdn_a_log[j], gdn_dt_bias[j], gdn_norm_g[j])
            x2 = _out_proj(_out_c_kernel, [o, gate], [], gdn_w_out[j].astype(BF16),
                           x2, mod3, post_g[layer], seq, "out_proj_c")
    return x2.reshape(bsz, seq, d)
```

```python
import functools
import math

import numpy as np
import jax
import jax.numpy as jnp
from jax import lax
from jax.experimental import pallas as pl
from jax.experimental.pallas import tpu as pltpu

F32 = jnp.float32
BF16 = jnp.bfloat16
HI = lax.Precision.HIGHEST

EPS = 1e-6
A_HEADS = 8
A_HEAD_DIM = 64
A_WIDTH = A_HEADS * A_HEAD_DIM
A_BLOCK = 128
DILATED_CONFIGS = ((128, 1), (512, 4), (2048, 16))
B_GROUP = 16
B_STATE = 64
REL_BUCKETS = 32
REL_MAX_DIST = 2048
C_HEADS = 8
C_DK = 128
C_DV = 128
C_CONV = 4
MASK_NEG = -1e30

LANES = 128
SUBLANES = 8
VMEM_LIMIT = 48 * 1024 * 1024
ROW_TILE = 512
COL_CHUNK = 512
S5_J = 32
S5_LW = 512
GDN_STEP = 128
GDN_CHUNK = 64

_NT = (((1,), (1,)), ((), ()))


def _cparams(sem):
    return pltpu.CompilerParams(dimension_semantics=sem, vmem_limit_bytes=VMEM_LIMIT)


def _sigmoid(x):
    return 1.0 / (1.0 + jnp.exp(-x))


def _mod_kernel(c_ref, w_ref, b_ref, o_ref):
    c = c_ref[...]
    ca = c * _sigmoid(c)
    o_ref[0] = jnp.dot(ca, w_ref[0], preferred_element_type=F32, precision=HI) + b_ref[0]


def _adaln_mod(c, ada_w, ada_b):
    depth, d, d3 = ada_w.shape
    bsz = c.shape[0]
    return pl.pallas_call(
        _mod_kernel,
        out_shape=jax.ShapeDtypeStruct((depth, bsz, d3), F32),
        grid=(depth, d3 // d),
        in_specs=[pl.BlockSpec((bsz, d), lambda l, j: (0, 0)),
                  pl.BlockSpec((1, d, d), lambda l, j: (l, 0, j)),
                  pl.BlockSpec((1, 1, d), lambda l, j: (l, 0, j))],
        out_specs=pl.BlockSpec((1, bsz, d), lambda l, j: (l, 0, j)),
        compiler_params=_cparams(("arbitrary", "arbitrary")),
        name="adaln_mod",
    )(c, ada_w, ada_b.reshape(depth, 1, d3))


def _in_proj_kernel(x_ref, mod_ref, g_ref, w_ref, *out_refs, splits, d_model):
    x = x_ref[...]
    ms = jnp.mean(x * x, axis=-1, keepdims=True)
    y = x * lax.rsqrt(ms + EPS) * g_ref[...]
    shift = mod_ref[0, :, 0:d_model]
    scale = mod_ref[0, :, d_model:2 * d_model]
    h = (y * (1.0 + scale) + shift).astype(BF16)
    for (c0, width, mult), o_ref in zip(splits, out_refs):
        for cc in range(0, width, COL_CHUNK):
            cw = min(COL_CHUNK, width - cc)
            acc = jnp.dot(h, w_ref[:, c0 + cc:c0 + cc + cw], preferred_element_type=F32)
            if mult != 1.0:
                acc = acc * mult
            o_ref[:, cc:cc + cw] = acc.astype(o_ref.dtype)


def _in_proj(x2, mod3, gain, w_bf16, splits, out_dtypes, seq):
    t, d = x2.shape
    tiles_per_seq = seq // ROW_TILE
    n_w = w_bf16.shape[1]
    return pl.pallas_call(
        functools.partial(_in_proj_kernel, splits=splits, d_model=d),
        out_shape=[jax.ShapeDtypeStruct((t, wd), dt) for (_, wd, _), dt in zip(splits, out_dtypes)],
        grid=(t // ROW_TILE,),
        in_specs=[pl.BlockSpec((ROW_TILE, d), lambda i: (i, 0)),
                  pl.BlockSpec((1, 1, 3 * d), lambda i: (i // tiles_per_seq, 0, 0)),
                  pl.BlockSpec((1, d), lambda i: (0, 0)),
                  pl.BlockSpec((d, n_w), lambda i: (0, 0))],
        out_specs=[pl.BlockSpec((ROW_TILE, wd), lambda i: (i, 0)) for (_, wd, _) in splits],
        compiler_params=_cparams(("arbitrary",)),
        name="in_proj",
    )(x2, mod3, gain.reshape(1, d), w_bf16)


def _t5_bucket(dist):
    dist = np.maximum(dist, 0)
    max_exact = REL_BUCKETS // 2
    large = max_exact + (np.log(np.maximum(dist, 1) / max_exact)
                         / math.log(REL_MAX_DIST / max_exact) * (REL_BUCKETS - max_exact)).astype(np.int32)
    large = np.minimum(large, REL_BUCKETS - 1)
    return np.where(dist < max_exact, dist, large).astype(np.int32)


def _bias_kernel(rb_ref, bucket_ref, mask_ref, o_ref):
    def body(r, carry):
        rows = pl.ds(pl.multiple_of(r * SUBLANES, SUBLANES), SUBLANES)
        bk = bucket_ref[0, rows, :]
        accs = [jnp.zeros(bk.shape, F32) for _ in range(A_HEADS)]
        for b in range(REL_BUCKETS):
            eq = bk == b
            accs = [jnp.where(eq, rb_ref[b, h], acc) for h, acc in enumerate(accs)]
        for f in range(2):
            keep = mask_ref[0, f, rows, :] != 0
            for h in range(A_HEADS):
                o_ref[0, f, h, rows, :] = jnp.where(keep, accs[h], MASK_NEG)
        return carry
    lax.fori_loop(0, A_BLOCK // SUBLANES, body, 0)


def _attn_bias_tables(rel_bias):
    qi = np.arange(A_BLOCK)[:, None]
    kj = np.arange(2 * A_BLOCK)[None, :]
    rel = qi + A_BLOCK - kj
    buckets, masks = [], []
    for window, dil in DILATED_CONFIGS:
        band = (rel >= 0) & (rel <= window // dil)
        buckets.append(_t5_bucket(rel * dil))
        masks.append(np.stack([band & (kj >= A_BLOCK), band]).astype(np.int32))
    n_cfg = len(DILATED_CONFIGS)
    return pl.pallas_call(
        _bias_kernel,
        out_shape=jax.ShapeDtypeStruct((n_cfg, 2, A_HEADS, A_BLOCK, 2 * A_BLOCK), F32),
        grid=(n_cfg,),
        in_specs=[pl.BlockSpec(memory_space=pltpu.SMEM),
                  pl.BlockSpec((1, A_BLOCK, 2 * A_BLOCK), lambda c: (c, 0, 0)),
                  pl.BlockSpec((1, 2, A_BLOCK, 2 * A_BLOCK), lambda c: (c, 0, 0, 0))],
        out_specs=pl.BlockSpec((1, 2, A_HEADS, A_BLOCK, 2 * A_BLOCK), lambda c: (c, 0, 0, 0, 0)),
        compiler_params=_cparams(("arbitrary",)),
        name="attn_bias",
    )(rel_bias.astype(F32), jnp.asarray(np.stack(buckets)), jnp.asarray(np.stack(masks)))


def _attn_kernel(q_ref, kp_ref, kc_ref, vp_ref, vc_ref, bias_ref, o_ref, lse_ref):
    q = q_ref[0]
    kcat = jnp.concatenate([kp_ref[0], kc_ref[0]], axis=0)
    vcat = jnp.concatenate([vp_ref[0], vc_ref[0]], axis=0)
    lane = lax.broadcasted_iota(jnp.int32, (A_BLOCK, LANES), 1)
    low = lane < A_HEAD_DIM
    lse_tile = jnp.zeros((A_BLOCK, LANES), F32)
    zero = jnp.zeros((A_BLOCK, LANES), BF16)
    for hp in range(A_HEADS // 2):
        sl = slice(hp * LANES, (hp + 1) * LANES)
        qp, kp, vp = q[:, sl], kcat[:, sl], vcat[:, sl]
        outs = []
        for e in range(2):
            h = 2 * hp + e
            qm = jnp.where(low if e == 0 else jnp.logical_not(low), qp, zero)
            s = lax.dot_general(qm, kp, _NT, preferred_element_type=F32) + bias_ref[0, h]
            m = jnp.max(s, axis=-1, keepdims=True)
            p = jnp.exp(s - m)
            l = jnp.sum(p, axis=-1, keepdims=True)
            o = jnp.dot(p.astype(BF16), vp, preferred_element_type=F32)
            outs.append(o / l)
            lse_tile = jnp.where(lane == h, m + jnp.log(l), lse_tile)
        o_ref[0, :, sl] = jnp.where(low, outs[0], outs[1])
    lse_ref[0] = lse_tile


def _dilated_attention_one(q, k, v, bias_tab, bsz, seq, dil):
    nb = seq // dil // A_BLOCK
    w = A_WIDTH
    view = lambda a: a.reshape(bsz, seq // dil, dil * a.shape[-1])
    cur = lambda b, r, n: (b, n, r)
    prev = lambda b, r, n: (b, jnp.maximum(n - 1, 0), r)
    blk = pl.BlockSpec((1, A_BLOCK, w), cur)
    blk_prev = pl.BlockSpec((1, A_BLOCK, w), prev)
    o, lse = pl.pallas_call(
        _attn_kernel,
        out_shape=[jax.ShapeDtypeStruct((bsz, seq // dil, dil * w), F32),
                   jax.ShapeDtypeStruct((bsz, seq // dil, dil * LANES), F32)],
        grid=(bsz, dil, nb),
        in_specs=[blk, blk_prev, blk, blk_prev, blk,
                  pl.BlockSpec((1, A_HEADS, A_BLOCK, 2 * A_BLOCK), lambda b, r, n: (jnp.minimum(n, 1), 0, 0, 0))],
        out_specs=[pl.BlockSpec((1, A_BLOCK, w), cur), pl.BlockSpec((1, A_BLOCK, LANES), cur)],
        compiler_params=_cparams(("arbitrary", "arbitrary", "arbitrary")),
        name=f"dilated_attn_d{dil}",
    )(view(q), view(k), view(k), view(v), view(v), bias_tab)
    return o.reshape(bsz * seq, w), lse.reshape(bsz * seq, LANES)


def _gelu_tanh(x):
    c = math.sqrt(2.0 / math.pi)
    return x * (0.5 * (1.0 + jnp.tanh(c * (x + 0.044715 * (x * x * x)))))


def _s5_kernel(u_ref, pm_ref, pmt_ref, bd_ref, pw_ref, cd_ref, dsk_ref, gw_ref, gb_ref, o_ref, x_scr, st_scr, *,
               n_state):
    n = n_state
    jn = S5_J

    @pl.when(pl.program_id(1) == 0)
    def _():
        st_scr[...] = jnp.zeros_like(st_scr)

    u = u_ref[...]
    u_perm = jnp.dot(pm_ref[...], u.astype(BF16), preferred_element_type=F32).astype(BF16)
    x_scr[...] = jnp.dot(u_perm, bd_ref[...], preferred_element_type=F32)

    for c0 in range(0, n, S5_LW):
        re = slice(c0, c0 + S5_LW)
        im = slice(n + c0, n + c0 + S5_LW)
        a1r, a1i = pw_ref[0:1, re], pw_ref[0:1, im]
        xr = jnp.zeros((SUBLANES, S5_LW), F32)
        xi = jnp.zeros((SUBLANES, S5_LW), F32)
        for j in range(jn):
            rows = slice(SUBLANES * j, SUBLANES * (j + 1))
            nr = a1r * xr - a1i * xi + x_scr[rows, re]
            ni = a1r * xi + a1i * xr + x_scr[rows, im]
            xr, xi = nr, ni
            x_scr[rows, re] = xr
            x_scr[rows, im] = xi
        ajr, aji = pw_ref[jn - 1:jn, re], pw_ref[jn - 1:jn, im]
        pr, pi = st_scr[0:1, re], st_scr[0:1, im]
        cin_r, cin_i = [], []
        for s in range(SUBLANES):
            cin_r.append(pr)
            cin_i.append(pi)
            nr = xr[s:s + 1] + ajr * pr - aji * pi
            ni = xi[s:s + 1] + ajr * pi + aji * pr
            pr, pi = nr, ni
        st_scr[0:1, re] = pr
        st_scr[0:1, im] = pi
        cr = jnp.concatenate(cin_r, axis=0)
        ci = jnp.concatenate(cin_i, axis=0)
        for j in range(jn):
            rows = slice(SUBLANES * j, SUBLANES * (j + 1))
            pjr, pji = pw_ref[j:j + 1, re], pw_ref[j:j + 1, im]
            x_scr[rows, re] = x_scr[rows, re] + (pjr * cr - pji * ci)
            x_scr[rows, im] = x_scr[rows, im] + (pjr * ci + pji * cr)

    y_perm = jnp.dot(x_scr[...].astype(BF16), cd_ref[...], preferred_element_type=F32)
    y_hi = y_perm.astype(BF16)
    y_lo = (y_perm - y_hi.astype(F32)).astype(BF16)
    y = (jnp.dot(pmt_ref[...], y_hi, preferred_element_type=F32)
         + jnp.dot(pmt_ref[...], y_lo, preferred_element_type=F32))
    y = _gelu_tanh(y + dsk_ref[...] * u)
    z = jnp.dot(y.astype(BF16), gw_ref[...], preferred_element_type=F32) + gb_ref[...]
    o_ref[...] = y * _sigmoid(z)


def _s5_tables(a_re, a_im, log_dt, b_re, b_im, c_re, c_im):
    g, p = a_re.shape
    dt = jnp.exp(log_dt.astype(F32))[:, None]
    ar, ai = a_re.astype(F32), a_im.astype(F32)
    mag = jnp.exp(dt * ar)
    abar_r, abar_i = mag * jnp.cos(dt * ai), mag * jnp.sin(dt * ai)
    den = ar * ar + ai * ai
    fr = ((abar_r - 1.0) * ar + abar_i * ai) / den
    fi = (abar_i * ar - (abar_r - 1.0) * ai) / den
    br, bi = b_re.astype(F32), b_im.astype(F32)
    bbar_r = fr[..., None] * br - fi[..., None] * bi
    bbar_i = fr[..., None] * bi + fi[..., None] * br
    eye = jnp.eye(g, dtype=F32)
    m = br.shape[-1]
    dense_b = lambda t: jnp.einsum('gpm,gh->gmhp', t, eye).reshape(g * m, g * p)
    bd = jnp.concatenate([dense_b(bbar_r), dense_b(bbar_i)], axis=1)
    dense_c = lambda t: jnp.einsum('gmp,gh->gphm', t, eye).reshape(g * p, g * m)
    cd = jnp.concatenate([dense_c(c_re.astype(F32)), -dense_c(c_im.astype(F32))], axis=0)
    kk = jnp.arange(1, S5_J + 1, dtype=F32)[:, None, None]
    pmag = jnp.exp(kk * (dt * ar)[None])
    pw_r = (pmag * jnp.cos(kk * (dt * ai)[None])).reshape(S5_J, g * p)
    pw_i = (pmag * jnp.sin(kk * (dt * ai)[None])).reshape(S5_J, g * p)
    pw = jnp.concatenate([pw_r, pw_i], axis=1)
    return bd.astype(BF16), pw, cd.astype(BF16)


def _s5_layer(u, bsz, seq, a_re, a_im, log_dt, b_re, b_im, c_re, c_im, d_skip, glu_w, glu_b):
    t, width = u.shape
    n = a_re.shape[0] * a_re.shape[1]
    bd, pw, cd = _s5_tables(a_re, a_im, log_dt, b_re, b_im, c_re, c_im)
    tile = SUBLANES * S5_J
    tiles_per_seq = seq // tile
    const = lambda b, i: (0, 0)
    src = (np.arange(tile) % SUBLANES) * S5_J + np.arange(tile) // SUBLANES
    perm = (src[:, None] == np.arange(tile)[None, :]).astype(np.float32)
    return pl.pallas_call(
        functools.partial(_s5_kernel, n_state=n),
        out_shape=jax.ShapeDtypeStruct((t, width), F32),
        grid=(bsz, tiles_per_seq),
        in_specs=[pl.BlockSpec((tile, width), lambda b, i: (b * tiles_per_seq + i, 0)),
                  pl.BlockSpec((tile, tile), const),
                  pl.BlockSpec((tile, tile), const),
                  pl.BlockSpec((width, 2 * n), const),
                  pl.BlockSpec((S5_J, 2 * n), const),
                  pl.BlockSpec((2 * n, width), const),
                  pl.BlockSpec((1, width), const),
                  pl.BlockSpec((width, width), const),
                  pl.BlockSpec((1, width), const)],
        out_specs=pl.BlockSpec((tile, width), lambda b, i: (b * tiles_per_seq + i, 0)),
        scratch_shapes=[pltpu.VMEM((tile, 2 * n), F32), pltpu.VMEM((SUBLANES, 2 * n), F32)],
        compiler_params=_cparams(("arbitrary", "arbitrary")),
        name="s5_layer",
    )(u, jnp.asarray(perm, BF16), jnp.asarray(perm.T, BF16), bd, pw, cd,
      d_skip.reshape(1, width).astype(F32), glu_w.astype(BF16),
      glu_b.reshape(1, width).astype(F32))


def _finish(y, x_ref, mod_ref, pg_ref, out_ref, d_model):
    ms = jnp.mean(y * y, axis=-1, keepdims=True)
    yn = y * lax.rsqrt(ms + EPS) * pg_ref[...]
    gate_mod = mod_ref[0, :, 2 * d_model:3 * d_model]
    out_ref[...] = x_ref[...] + gate_mod * yn


def _out_ab_kernel(o0_ref, o1_ref, o2_ref, l0_ref, l1_ref, l2_ref, ob_ref, gate_ref, e_ref, w_ref,
                   x_ref, mod_ref, pg_ref, out_ref, *, d_model):
    l0, l1, l2 = l0_ref[...], l1_ref[...], l2_ref[...]
    mx = jnp.maximum(jnp.maximum(l0, l1), l2)
    e0, e1, e2 = jnp.exp(l0 - mx), jnp.exp(l1 - mx), jnp.exp(l2 - mx)
    den = e0 + e1 + e2
    expand = lambda wgt: jnp.dot(wgt / den, e_ref[...], preferred_element_type=F32, precision=HI)
    o_a = expand(e0) * o0_ref[...] + expand(e1) * o1_ref[...] + expand(e2) * o2_ref[...]
    gate = gate_ref[...].astype(F32)
    sg = gate * _sigmoid(gate)
    aw = o_a.shape[-1]
    y = jnp.dot((o_a * sg[:, :aw]).astype(BF16), w_ref[0:aw, :], preferred_element_type=F32)
    y = y + jnp.dot((ob_ref[...] * sg[:, aw:]).astype(BF16), w_ref[aw:, :], preferred_element_type=F32)
    _finish(y, x_ref, mod_ref, pg_ref, out_ref, d_model)


def _out_c_kernel(o_ref, gate_ref, w_ref, x_ref, mod_ref, pg_ref, out_ref, *, d_model):
    gate = gate_ref[...].astype(F32)
    o = o_ref[...].astype(F32) * (gate * _sigmoid(gate))
    y = jnp.dot(o.astype(BF16), w_ref[...], preferred_element_type=F32)
    _finish(y, x_ref, mod_ref, pg_ref, out_ref, d_model)


def _out_proj(kern, row_inputs, const_inputs, w_bf16, x2, mod3, post_g, seq, name):
    t, d = x2.shape
    tiles_per_seq = seq // ROW_TILE
    row_spec = lambda a: pl.BlockSpec((ROW_TILE, a.shape[1]), lambda i: (i, 0))
    const_spec = lambda a: pl.BlockSpec(a.shape, lambda i: (0, 0))
    return pl.pallas_call(
        functools.partial(kern, d_model=d),
        out_shape=jax.ShapeDtypeStruct((t, d), F32),
        grid=(t // ROW_TILE,),
        in_specs=([row_spec(a) for a in row_inputs] + [const_spec(a) for a in const_inputs]
                  + [const_spec(w_bf16), row_spec(x2),
                     pl.BlockSpec((1, 1, 3 * d), lambda i: (i // tiles_per_seq, 0, 0)),
                     pl.BlockSpec((1, d), lambda i: (0, 0))]),
        out_specs=pl.BlockSpec((ROW_TILE, d), lambda i: (i, 0)),
        compiler_params=_cparams(("arbitrary",)),
        name=name,
    )(*row_inputs, *const_inputs, w_bf16, x2, mod3, post_g.reshape(1, d))


def _gdn_prep_kernel(x_ref, halo_ref, br_ref, cw_ref, alog_ref, dtb_ref,
                     w_ref, u_ref, qg_ref, kdt_ref, aqk_ref, dec_ref, xs_scr):
    n = GDN_STEP
    first = pl.program_id(1) == 0
    halo = halo_ref[...]
    xs_scr[0:SUBLANES, :] = jnp.where(first, jnp.zeros_like(halo), halo)
    xs_scr[SUBLANES:SUBLANES + n, :] = x_ref[...]
    cw = cw_ref[...]
    conv = xs_scr[SUBLANES:SUBLANES + n, :] * cw[C_CONV - 1:C_CONV]
    for j in range(C_CONV - 1):
        off = SUBLANES - (C_CONV - 1) + j
        conv = conv + xs_scr[off:off + n, :] * cw[j:j + 1]
    act = conv * _sigmoid(conv)

    row = lax.broadcasted_iota(jnp.int32, (n, n), 0)
    col = lax.broadcasted_iota(jnp.int32, (n, n), 1)
    same = (row // GDN_CHUNK) == (col // GDN_CHUNK)
    tril = jnp.logical_and(same, row >= col)
    strict = jnp.logical_and(same, row > col)
    eye = (row == col).astype(F32)

    br = br_ref[...]
    beta_all = _sigmoid(br)
    xg = br + dtb_ref[...]
    softplus = jnp.maximum(xg, 0.0) + jnp.log(1.0 + jnp.exp(-jnp.abs(xg)))
    g_all = -jnp.exp(alog_ref[...]) * softplus
    gc_all = jnp.dot(tril.astype(F32), g_all, preferred_element_type=F32, precision=HI)
    gc_t = gc_all.T
    first_chunk = lax.broadcasted_iota(jnp.int32, (n, 1), 0) < GDN_CHUNK
    dec_row = lax.broadcasted_iota(jnp.int32, (SUBLANES, LANES), 0)

    for h in range(C_HEADS):
        sl = slice(h * LANES, (h + 1) * LANES)
        gcol = gc_all[:, C_HEADS + h:C_HEADS + h + 1]
        grow = gc_t[C_HEADS + h:C_HEADS + h + 1, :]
        decay = jnp.where(tril, jnp.exp(jnp.where(tril, gcol - grow, 0.0)), 0.0)
        beta = beta_all[:, h:h + 1]
        q = act[:, h * C_DK:(h + 1) * C_DK]
        k = act[:, C_HEADS * C_DK + h * C_DK:C_HEADS * C_DK + (h + 1) * C_DK]
        v = act[:, 2 * C_HEADS * C_DK + h * C_DV:2 * C_HEADS * C_DK + (h + 1) * C_DV]
        q = q * lax.rsqrt(jnp.sum(q * q, axis=-1, keepdims=True) + EPS) * (C_DK ** -0.5)
        k = k * lax.rsqrt(jnp.sum(k * k, axis=-1, keepdims=True) + EPS)
        kb = k * beta
        k16 = k.astype(BF16)
        a = jnp.where(strict, lax.dot_general(kb.astype(BF16), k16, _NT, preferred_element_type=F32) * decay, 0.0)
        aqk = lax.dot_general(q.astype(BF16), k16, _NT, preferred_element_type=F32) * decay
        inv = eye - a
        ak = a
        for _ in range(5):
            ak16 = ak.astype(BF16)
            ak = jnp.dot(ak16, ak16, preferred_element_type=F32)
            inv = inv + jnp.dot(inv.astype(BF16), ak.astype(BF16), preferred_element_type=F32)
        eg = jnp.exp(gcol)
        rhs = jnp.concatenate([v * beta, kb * eg], axis=1)
        uw = jnp.dot(inv.astype(BF16), rhs.astype(BF16), preferred_element_type=F32)
        g_last = jnp.where(first_chunk, gcol[GDN_CHUNK - 1:GDN_CHUNK], gcol[n - 1:n])
        kd = k * jnp.exp(g_last - gcol)
        u_ref[:, sl] = uw[:, 0:C_DV]
        w_ref[:, sl] = uw[:, C_DV:].astype(BF16)
        qg_ref[:, sl] = (q * eg).astype(BF16)
        kdt_ref[:, sl] = kd.T.astype(BF16)
        aqk_ref[:, sl] = aqk.astype(BF16)
        dec_ref[0, :, sl] = jnp.where(dec_row < SUBLANES // 2, jnp.exp(gcol[GDN_CHUNK - 1:GDN_CHUNK]),
                                      jnp.exp(gcol[n - 1:n]))


def _gdn_rec_kernel(w_ref, u_ref, qg_ref, kdt_ref, aqk_ref, dec_ref, ng_ref, o_ref, s_scr):
    @pl.when(pl.program_id(1) == 0)
    def _():
        s_scr[...] = jnp.zeros_like(s_scr)

    half = SUBLANES // 2
    zeros = jnp.zeros((GDN_CHUNK, C_DV), BF16)
    for h in range(C_HEADS):
        sl = slice(h * LANES, (h + 1) * LANES)
        kdt = kdt_ref[:, sl]
        state = s_scr[h]
        for j in range(GDN_STEP // GDN_CHUNK):
            rows = slice(j * GDN_CHUNK, (j + 1) * GDN_CHUNK)
            wq = jnp.concatenate([w_ref[rows, sl], qg_ref[rows, sl]], axis=0)
            ws = jnp.dot(wq, state.astype(BF16), preferred_element_type=F32)
            v_new = (u_ref[rows, sl] - ws[0:GDN_CHUNK]).astype(BF16)
            v_pad = jnp.concatenate([v_new, zeros] if j == 0 else [zeros, v_new], axis=0)
            o = ws[GDN_CHUNK:] + jnp.dot(aqk_ref[rows, sl], v_pad, preferred_element_type=F32)
            dec = dec_ref[0, j * half:j * half + 1, sl]
            state = state * dec + jnp.dot(kdt, v_pad, preferred_element_type=F32)
            ms = jnp.mean(o * o, axis=-1, keepdims=True)
            o_ref[rows, sl] = o * lax.rsqrt(ms + EPS) * ng_ref[...]
        s_scr[h] = state


def _gdn_core(qkv_pre, br, bsz, seq, conv_w, a_log, dt_bias, norm_g):
    t, qkv_w = qkv_pre.shape
    hw = C_HEADS * C_DV
    steps_per_seq = seq // GDN_STEP
    halo_blocks = GDN_STEP // SUBLANES
    pad_row = lambda vec: jnp.zeros((1, LANES), F32).at[0, C_HEADS:2 * C_HEADS].set(vec.astype(F32))
    tok = lambda b, i: (b * steps_per_seq + i, 0)
    const = lambda b, i: (0, 0)
    tok_spec = pl.BlockSpec((GDN_STEP, hw), tok)
    dec_spec = pl.BlockSpec((1, SUBLANES, hw), lambda b, i: (b * steps_per_seq + i, 0, 0))
    w, u, qg, kdt, aqk, dec = pl.pallas_call(
        _gdn_prep_kernel,
        out_shape=[jax.ShapeDtypeStruct((t, hw), BF16), jax.ShapeDtypeStruct((t, hw), F32),
                   jax.ShapeDtypeStruct((t, hw), BF16), jax.ShapeDtypeStruct((t, hw), BF16),
                   jax.ShapeDtypeStruct((t, hw), BF16),
                   jax.ShapeDtypeStruct((t // GDN_STEP, SUBLANES, hw), F32)],
        grid=(bsz, steps_per_seq),
        in_specs=[pl.BlockSpec((GDN_STEP, qkv_w), tok),
                  pl.BlockSpec((SUBLANES, qkv_w),
                               lambda b, i: (jnp.maximum((b * steps_per_seq + i) * halo_blocks - 1, 0), 0)),
                  pl.BlockSpec((GDN_STEP, LANES), tok),
                  pl.BlockSpec((C_CONV, qkv_w), const),
                  pl.BlockSpec((1, LANES), const),
                  pl.BlockSpec((1, LANES), const)],
        out_specs=[tok_spec, tok_spec, tok_spec, tok_spec, tok_spec, dec_spec],
        scratch_shapes=[pltpu.VMEM((SUBLANES + GDN_STEP, qkv_w), F32)],
        compiler_params=_cparams(("arbitrary", "arbitrary")),
        name="gdn_prep",
    )(qkv_pre, qkv_pre, br, conv_w.astype(F32), pad_row(a_log), pad_row(dt_bias))
    return pl.pallas_call(
        _gdn_rec_kernel,
        out_shape=jax.ShapeDtypeStruct((t, hw), F32),
        grid=(bsz, steps_per_seq),
        in_specs=[tok_spec, tok_spec, tok_spec, tok_spec, tok_spec, dec_spec,
                  pl.BlockSpec((1, C_DV), const)],
        out_specs=tok_spec,
        scratch_shapes=[pltpu.VMEM((C_HEADS, C_DK, C_DV), F32)],
        compiler_params=_cparams(("arbitrary", "arbitrary")),
        name="gdn_recurrence",
    )(w, u, qg, kdt, aqk, dec, norm_g.reshape(1, C_DV).astype(F32))


def kernel(x, c, ada_w, ada_b, pre_g, post_g, rel_bias, ab_w_in, ab_w_out, s5_a_re, s5_a_im, s5_log_dt, s5_b_re, s5_b_im, s5_c_re, s5_c_im, s5_d, s5_glu_w, s5_glu_b, gdn_w_in, gdn_conv, gdn_a_log, gdn_dt_bias, gdn_norm_g, gdn_w_out):
    bsz, seq, d = x.shape
    depth = ada_w.shape[0]
    assert seq % (A_BLOCK * max(dl for _, dl in DILATED_CONFIGS)) == 0 and seq % ROW_TILE == 0
    x2 = x.reshape(bsz * seq, d)
    mod = _adaln_mod(c, ada_w, ada_b)
    b_width = s5_d.shape[-1]
    head_expand = jnp.asarray(
        (np.arange(LANES)[:, None] == (np.arange(A_WIDTH)[None, :] // A_HEAD_DIM)).astype(np.float32))

    for layer in range(depth):
        j = layer // 2
        mod3 = mod[layer].reshape(bsz, 1, 3 * d)
        if layer % 2 == 0:
            splits = ((0, A_WIDTH, A_HEAD_DIM ** -0.5), (A_WIDTH, A_WIDTH, 1.0), (2 * A_WIDTH, A_WIDTH, 1.0),
                      (3 * A_WIDTH, b_width, 1.0), (3 * A_WIDTH + b_width, A_WIDTH + b_width, 1.0))
            q, k, v, u, gate = _in_proj(x2, mod3, pre_g[layer], ab_w_in[j].astype(BF16), splits,
                                        (BF16, BF16, BF16, F32, BF16), seq)
            outs, lses = [], []
            bias_tabs = _attn_bias_tables(rel_bias)
            for cfg, (_, dil) in enumerate(DILATED_CONFIGS):
                o_c, lse_c = _dilated_attention_one(q, k, v, bias_tabs[cfg], bsz, seq, dil)
                outs.append(o_c)
                lses.append(lse_c)
            o_b = _s5_layer(u, bsz, seq, s5_a_re[j], s5_a_im[j], s5_log_dt[j], s5_b_re[j], s5_b_im[j],
                            s5_c_re[j], s5_c_im[j], s5_d[j], s5_glu_w[j], s5_glu_b[j])
            x2 = _out_proj(_out_ab_kernel, outs + lses + [o_b, gate], [head_expand], ab_w_out[j].astype(BF16),
                           x2, mod3, post_g[layer], seq, "out_proj_ab")
        else:
            qkv_w = 2 * C_HEADS * C_DK + C_HEADS * C_DV
            gate_w = C_HEADS * C_DV
            w_in = gdn_w_in[j]
            w_pad = jnp.concatenate(
                [w_in, jnp.zeros((d, LANES - (w_in.shape[1] - qkv_w - gate_w)), w_in.dtype)], axis=1).astype(BF16)
            splits = ((0, qkv_w, 1.0), (qkv_w, gate_w, 1.0), (qkv_w + gate_w, LANES, 1.0))
            qkv_pre, gate, br = _in_proj(x2, mod3, pre_g[layer], w_pad, splits, (F32, BF16, F32), seq)
            o = _gdn_core(qkv_pre, br, bsz, seq, gdn_conv[j], gdn_a_log[j], gdn_dt_bias[j], gdn_norm_g[j])
            x2 = _out_proj(_out_c_kernel, [o, gate], [], gdn_w_out[j].astype(BF16),
                           x2, mod3, post_g[layer], seq, "out_proj_c")
    return x2.reshape(bsz, seq, d)
```

```python
import functools
import math

import numpy as np
import jax
import jax.numpy as jnp
from jax import lax
from jax.experimental import pallas as pl
from jax.experimental.pallas import tpu as pltpu

F32 = jnp.float32
BF16 = jnp.bfloat16
HI = lax.Precision.HIGHEST

EPS = 1e-6
A_HEADS = 8
A_HEAD_DIM = 64
A_WIDTH = A_HEADS * A_HEAD_DIM
A_BLOCK = 128
DILATED_CONFIGS = ((128, 1), (512, 4), (2048, 16))
B_GROUP = 16
B_STATE = 64
REL_BUCKETS = 32
REL_MAX_DIST = 2048
C_HEADS = 8
C_DK = 128
C_DV = 128
C_CONV = 4
MASK_NEG = -1e30

LANES = 128
SUBLANES = 8
VMEM_LIMIT = 48 * 1024 * 1024
ROW_TILE = 512
COL_CHUNK = 512
S5_J = 32
S5_LW = 512
GDN_STEP = 128
GDN_CHUNK = 64

_NT = (((1,), (1,)), ((), ()))


def _cparams(sem):
    return pltpu.CompilerParams(dimension_semantics=sem, vmem_limit_bytes=VMEM_LIMIT)


def _sigmoid(x):
    return 1.0 / (1.0 + jnp.exp(-x))


def _mod_kernel(c_ref, w_ref, b_ref, o_ref):
    c = c_ref[...]
    ca = c * _sigmoid(c)
    o_ref[0] = jnp.dot(ca, w_ref[0], preferred_element_type=F32, precision=HI) + b_ref[0]


def _adaln_mod(c, ada_w, ada_b):
    depth, d, d3 = ada_w.shape
    bsz = c.shape[0]
    return pl.pallas_call(
        _mod_kernel,
        out_shape=jax.ShapeDtypeStruct((depth, bsz, d3), F32),
        grid=(depth, d3 // d),
        in_specs=[pl.BlockSpec((bsz, d), lambda l, j: (0, 0)),
                  pl.BlockSpec((1, d, d), lambda l, j: (l, 0, j)),
                  pl.BlockSpec((1, 1, d), lambda l, j: (l, 0, j))],
        out_specs=pl.BlockSpec((1, bsz, d), lambda l, j: (l, 0, j)),
        compiler_params=_cparams(("arbitrary", "arbitrary")),
        name="adaln_mod",
    )(c, ada_w, ada_b.reshape(depth, 1, d3))


def _in_proj_kernel(x_ref, mod_ref, g_ref, w_ref, *out_refs, splits, d_model):
    x = x_ref[...]
    ms = jnp.mean(x * x, axis=-1, keepdims=True)
    y = x * lax.rsqrt(ms + EPS) * g_ref[...]
    shift = mod_ref[0, :, 0:d_model]
    scale = mod_ref[0, :, d_model:2 * d_model]
    h = (y * (1.0 + scale) + shift).astype(BF16)
    for (c0, width, mult), o_ref in zip(splits, out_refs):
        for cc in range(0, width, COL_CHUNK):
            cw = min(COL_CHUNK, width - cc)
            acc = jnp.dot(h, w_ref[:, c0 + cc:c0 + cc + cw], preferred_element_type=F32)
            if mult != 1.0:
                acc = acc * mult
            o_ref[:, cc:cc + cw] = acc.astype(o_ref.dtype)


def _in_proj(x2, mod3, gain, w_bf16, splits, out_dtypes, seq):
    t, d = x2.shape
    tiles_per_seq = seq // ROW_TILE
    n_w = w_bf16.shape[1]
    return pl.pallas_call(
        functools.partial(_in_proj_kernel, splits=splits, d_model=d),
        out_shape=[jax.ShapeDtypeStruct((t, wd), dt) for (_, wd, _), dt in zip(splits, out_dtypes)],
        grid=(t // ROW_TILE,),
        in_specs=[pl.BlockSpec((ROW_TILE, d), lambda i: (i, 0)),
                  pl.BlockSpec((1, 1, 3 * d), lambda i: (i // tiles_per_seq, 0, 0)),
                  pl.BlockSpec((1, d), lambda i: (0, 0)),
                  pl.BlockSpec((d, n_w), lambda i: (0, 0))],
        out_specs=[pl.BlockSpec((ROW_TILE, wd), lambda i: (i, 0)) for (_, wd, _) in splits],
        compiler_params=_cparams(("arbitrary",)),
        name="in_proj",
    )(x2, mod3, gain.reshape(1, d), w_bf16)


def _t5_bucket(dist):
    dist = np.maximum(dist, 0)
    max_exact = REL_BUCKETS // 2
    large = max_exact + (np.log(np.maximum(dist, 1) / max_exact)
                         / math.log(REL_MAX_DIST / max_exact) * (REL_BUCKETS - max_exact)).astype(np.int32)
    large = np.minimum(large, REL_BUCKETS - 1)
    return np.where(dist < max_exact, dist, large).astype(np.int32)


def _bias_kernel(rb_ref, bucket_ref, mask_ref, o_ref):
    def body(r, carry):
        rows = pl.ds(pl.multiple_of(r * SUBLANES, SUBLANES), SUBLANES)
        bk = bucket_ref[0, rows, :]
        accs = [jnp.zeros(bk.shape, F32) for _ in range(A_HEADS)]
        for b in range(REL_BUCKETS):
            eq = bk == b
            accs = [jnp.where(eq, rb_ref[b, h], acc) for h, acc in enumerate(accs)]
        for f in range(2):
            keep = mask_ref[0, f, rows, :] != 0
            for h in range(A_HEADS):
                o_ref[0, f, h, rows, :] = jnp.where(keep, accs[h], MASK_NEG)
        return carry
    lax.fori_loop(0, A_BLOCK // SUBLANES, body, 0)


def _attn_bias_tables(rel_bias):
    qi = np.arange(A_BLOCK)[:, None]
    kj = np.arange(2 * A_BLOCK)[None, :]
    rel = qi + A_BLOCK - kj
    buckets, masks = [], []
    for window, dil in DILATED_CONFIGS:
        band = (rel >= 0) & (rel <= window // dil)
        buckets.append(_t5_bucket(rel * dil))
        masks.append(np.stack([band & (kj >= A_BLOCK), band]).astype(np.int32))
    n_cfg = len(DILATED_CONFIGS)
    return pl.pallas_call(
        _bias_kernel,
        out_shape=jax.ShapeDtypeStruct((n_cfg, 2, A_HEADS, A_BLOCK, 2 * A_BLOCK), F32),
        grid=(n_cfg,),
        in_specs=[pl.BlockSpec(memory_space=pltpu.SMEM),
                  pl.BlockSpec((1, A_BLOCK, 2 * A_BLOCK), lambda c: (c, 0, 0)),
                  pl.BlockSpec((1, 2, A_BLOCK, 2 * A_BLOCK), lambda c: (c, 0, 0, 0))],
        out_specs=pl.BlockSpec((1, 2, A_HEADS, A_BLOCK, 2 * A_BLOCK), lambda c: (c, 0, 0, 0, 0)),
        compiler_params=_cparams(("arbitrary",)),
        name="attn_bias",
    )(rel_bias.astype(F32), jnp.asarray(np.stack(buckets)), jnp.asarray(np.stack(masks)))


def _attn_kernel(q_ref, kp_ref, kc_ref, vp_ref, vc_ref, bias_ref, o_ref, lse_ref):
    q = q_ref[0]
    kcat = jnp.concatenate([kp_ref[0], kc_ref[0]], axis=0)
    vcat = jnp.concatenate([vp_ref[0], vc_ref[0]], axis=0)
    lane = lax.broadcasted_iota(jnp.int32, (A_BLOCK, LANES), 1)
    low = lane < A_HEAD_DIM
    lse_tile = jnp.zeros((A_BLOCK, LANES), F32)
    zero = jnp.zeros((A_BLOCK, LANES), BF16)
    for hp in range(A_HEADS // 2):
        sl = slice(hp * LANES, (hp + 1) * LANES)
        qp, kp, vp = q[:, sl], kcat[:, sl], vcat[:, sl]
        outs = []
        for e in range(2):
            h = 2 * hp + e
            qm = jnp.where(low if e == 0 else jnp.logical_not(low), qp, zero)
            s = lax.dot_general(qm, kp, _NT, preferred_element_type=F32) + bias_ref[0, h]
            m = jnp.max(s, axis=-1, keepdims=True)
            p = jnp.exp(s - m)
            l = jnp.sum(p, axis=-1, keepdims=True)
            o = jnp.dot(p.astype(BF16), vp, preferred_element_type=F32)
            outs.append(o / l)
            lse_tile = jnp.where(lane == h, m + jnp.log(l), lse_tile)
        o_ref[0, :, sl] = jnp.where(low, outs[0], outs[1])
    lse_ref[0] = lse_tile


def _dilated_attention_one(q, k, v, bias_tab, bsz, seq, dil):
    nb = seq // dil // A_BLOCK
    w = A_WIDTH
    view = lambda a: a.reshape(bsz, seq // dil, dil * a.shape[-1])
    cur = lambda b, r, n: (b, n, r)
    prev = lambda b, r, n: (b, jnp.maximum(n - 1, 0), r)
    blk = pl.BlockSpec((1, A_BLOCK, w), cur)
    blk_prev = pl.BlockSpec((1, A_BLOCK, w), prev)
    o, lse = pl.pallas_call(
        _attn_kernel,
        out_shape=[jax.ShapeDtypeStruct((bsz, seq // dil, dil * w), F32),
                   jax.ShapeDtypeStruct((bsz, seq // dil, dil * LANES), F32)],
        grid=(bsz, dil, nb),
        in_specs=[blk, blk_prev, blk, blk_prev, blk,
                  pl.BlockSpec((1, A_HEADS, A_BLOCK, 2 * A_BLOCK), lambda b, r, n: (jnp.minimum(n, 1), 0, 0, 0))],
        out_specs=[pl.BlockSpec((1, A_BLOCK, w), cur), pl.BlockSpec((1, A_BLOCK, LANES), cur)],
        compiler_params=_cparams(("arbitrary", "arbitrary", "arbitrary")),
        name=f"dilated_attn_d{dil}",
    )(view(q), view(k), view(k), view(v), view(v), bias_tab)
    return o.reshape(bsz * seq, w), lse.reshape(bsz * seq, LANES)


def _gelu_tanh(x):
    c = math.sqrt(2.0 / math.pi)
    return x * (0.5 * (1.0 + jnp.tanh(c * (x + 0.044715 * (x * x * x)))))


def _s5_kernel(u_ref, pm_ref, pmt_ref, bd_ref, pw_ref, cd_ref, dsk_ref, gw_ref, gb_ref, o_ref, x_scr, st_scr, *,
               n_state):
    n = n_state
    jn = S5_J

    @pl.when(pl.program_id(1) == 0)
    def _():
        st_scr[...] = jnp.zeros_like(st_scr)

    u = u_ref[...]
    u_perm = jnp.dot(pm_ref[...], u.astype(BF16), preferred_element_type=F32).astype(BF16)
    x_scr[...] = jnp.dot(u_perm, bd_ref[...], preferred_element_type=F32)

    for c0 in range(0, n, S5_LW):
        re = slice(c0, c0 + S5_LW)
        im = slice(n + c0, n + c0 + S5_LW)
        a1r, a1i = pw_ref[0:1, re], pw_ref[0:1, im]
        xr = jnp.zeros((SUBLANES, S5_LW), F32)
        xi = jnp.zeros((SUBLANES, S5_LW), F32)
        for j in range(jn):
            rows = slice(SUBLANES * j, SUBLANES * (j + 1))
            nr = a1r * xr - a1i * xi + x_scr[rows, re]
            ni = a1r * xi + a1i * xr + x_scr[rows, im]
            xr, xi = nr, ni
            x_scr[rows, re] = xr
            x_scr[rows, im] = xi
        ajr, aji = pw_ref[jn - 1:jn, re], pw_ref[jn - 1:jn, im]
        pr, pi = st_scr[0:1, re], st_scr[0:1, im]
        cin_r, cin_i = [], []
        for s in range(SUBLANES):
            cin_r.append(pr)
            cin_i.append(pi)
            nr = xr[s:s + 1] + ajr * pr - aji * pi
            ni = xi[s:s + 1] + ajr * pi + aji * pr
            pr, pi = nr, ni
        st_scr[0:1, re] = pr
        st_scr[0:1, im] = pi
        cr = jnp.concatenate(cin_r, axis=0)
        ci = jnp.concatenate(cin_i, axis=0)
        for j in range(jn):
            rows = slice(SUBLANES * j, SUBLANES * (j + 1))
            pjr, pji = pw_ref[j:j + 1, re], pw_ref[j:j + 1, im]
            x_scr[rows, re] = x_scr[rows, re] + (pjr * cr - pji * ci)
            x_scr[rows, im] = x_scr[rows, im] + (pjr * ci + pji * cr)

    y_perm = jnp.dot(x_scr[...].astype(BF16), cd_ref[...], preferred_element_type=F32)
    y_hi = y_perm.astype(BF16)
    y_lo = (y_perm - y_hi.astype(F32)).astype(BF16)
    y = (jnp.dot(pmt_ref[...], y_hi, preferred_element_type=F32)
         + jnp.dot(pmt_ref[...], y_lo, preferred_element_type=F32))
    y = _gelu_tanh(y + dsk_ref[...] * u)
    z = jnp.dot(y.astype(BF16), gw_ref[...], preferred_element_type=F32) + gb_ref[...]
    o_ref[...] = y * _sigmoid(z)


def _s5_tables(a_re, a_im, log_dt, b_re, b_im, c_re, c_im):
    g, p = a_re.shape
    dt = jnp.exp(log_dt.astype(F32))[:, None]
    ar, ai = a_re.astype(F32), a_im.astype(F32)
    mag = jnp.exp(dt * ar)
    abar_r, abar_i = mag * jnp.cos(dt * ai), mag * jnp.sin(dt * ai)
    den = ar * ar + ai * ai
    fr = ((abar_r - 1.0) * ar + abar_i * ai) / den
    fi = (abar_i * ar - (abar_r - 1.0) * ai) / den
    br, bi = b_re.astype(F32), b_im.astype(F32)
    bbar_r = fr[..., None] * br - fi[..., None] * bi
    bbar_i = fr[..., None] * bi + fi[..., None] * br
    eye = jnp.eye(g, dtype=F32)
    m = br.shape[-1]
    dense_b = lambda t: jnp.einsum('gpm,gh->gmhp', t, eye).reshape(g * m, g * p)
    bd = jnp.concatenate([dense_b(bbar_r), dense_b(bbar_i)], axis=1)
    dense_c = lambda t: jnp.einsum('gmp,gh->gphm', t, eye).reshape(g * p, g * m)
    cd = jnp.concatenate([dense_c(c_re.astype(F32)), -dense_c(c_im.astype(F32))], axis=0)
    kk = jnp.arange(1, S5_J + 1, dtype=F32)[:, None, None]
    pmag = jnp.exp(kk * (dt * ar)[None])
    pw_r = (pmag * jnp.cos(kk * (dt * ai)[None])).reshape(S5_J, g * p)
    pw_i = (pmag * jnp.sin(kk * (dt * ai)[None])).reshape(S5_J, g * p)
    pw = jnp.concatenate([pw_r, pw_i], axis=1)
    return bd.astype(BF16), pw, cd.astype(BF16)


def _s5_layer(u, bsz, seq, a_re, a_im, log_dt, b_re, b_im, c_re, c_im, d_skip, glu_w, glu_b):
    t, width = u.shape
    n = a_re.shape[0] * a_re.shape[1]
    bd, pw, cd = _s5_tables(a_re, a_im, log_dt, b_re, b_im, c_re, c_im)
    tile = SUBLANES * S5_J
    tiles_per_seq = seq // tile
    const = lambda b, i: (0, 0)
    src = (np.arange(tile) % SUBLANES) * S5_J + np.arange(tile) // SUBLANES
    perm = (src[:, None] == np.arange(tile)[None, :]).astype(np.float32)
    return pl.pallas_call(
        functools.partial(_s5_kernel, n_state=n),
        out_shape=jax.ShapeDtypeStruct((t, width), F32),
        grid=(bsz, tiles_per_seq),
        in_specs=[pl.BlockSpec((tile, width), lambda b, i: (b * tiles_per_seq + i, 0)),
                  pl.BlockSpec((tile, tile), const),
                  pl.BlockSpec((tile, tile), const),
                  pl.BlockSpec((width, 2 * n), const),
                  pl.BlockSpec((S5_J, 2 * n), const),
                  pl.BlockSpec((2 * n, width), const),
                  pl.BlockSpec((1, width), const),
                  pl.BlockSpec((width, width), const),
                  pl.BlockSpec((1, width), const)],
        out_specs=pl.BlockSpec((tile, width), lambda b, i: (b * tiles_per_seq + i, 0)),
        scratch_shapes=[pltpu.VMEM((tile, 2 * n), F32), pltpu.VMEM((SUBLANES, 2 * n), F32)],
        compiler_params=_cparams(("arbitrary", "arbitrary")),
        name="s5_layer",
    )(u, jnp.asarray(perm, BF16), jnp.asarray(perm.T, BF16), bd, pw, cd,
      d_skip.reshape(1, width).astype(F32), glu_w.astype(BF16),
      glu_b.reshape(1, width).astype(F32))


def _finish(y, x_ref, mod_ref, pg_ref, out_ref, d_model):
    ms = jnp.mean(y * y, axis=-1, keepdims=True)
    yn = y * lax.rsqrt(ms + EPS) * pg_ref[...]
    gate_mod = mod_ref[0, :, 2 * d_model:3 * d_model]
    out_ref[...] = x_ref[...] + gate_mod * yn


def _out_ab_kernel(o0_ref, o1_ref, o2_ref, l0_ref, l1_ref, l2_ref, ob_ref, gate_ref, e_ref, w_ref,
                   x_ref, mod_ref, pg_ref, out_ref, *, d_model):
    l0, l1, l2 = l0_ref[...], l1_ref[...], l2_ref[...]
    mx = jnp.maximum(jnp.maximum(l0, l1), l2)
    e0, e1, e2 = jnp.exp(l0 - mx), jnp.exp(l1 - mx), jnp.exp(l2 - mx)
    den = e0 + e1 + e2
    expand = lambda wgt: jnp.dot(wgt / den, e_ref[...], preferred_element_type=F32, precision=HI)
    o_a = expand(e0) * o0_ref[...] + expand(e1) * o1_ref[...] + expand(e2) * o2_ref[...]
    gate = gate_ref[...].astype(F32)
    sg = gate * _sigmoid(gate)
    aw = o_a.shape[-1]
    y = jnp.dot((o_a * sg[:, :aw]).astype(BF16), w_ref[0:aw, :], preferred_element_type=F32)
    y = y + jnp.dot((ob_ref[...] * sg[:, aw:]).astype(BF16), w_ref[aw:, :], preferred_element_type=F32)
    _finish(y, x_ref, mod_ref, pg_ref, out_ref, d_model)


def _out_c_kernel(o_ref, gate_ref, w_ref, x_ref, mod_ref, pg_ref, out_ref, *, d_model):
    gate = gate_ref[...].astype(F32)
    o = o_ref[...].astype(F32) * (gate * _sigmoid(gate))
    y = jnp.dot(o.astype(BF16), w_ref[...], preferred_element_type=F32)
    _finish(y, x_ref, mod_ref, pg_ref, out_ref, d_model)


def _out_proj(kern, row_inputs, const_inputs, w_bf16, x2, mod3, post_g, seq, name):
    t, d = x2.shape
    tiles_per_seq = seq // ROW_TILE
    row_spec = lambda a: pl.BlockSpec((ROW_TILE, a.shape[1]), lambda i: (i, 0))
    const_spec = lambda a: pl.BlockSpec(a.shape, lambda i: (0, 0))
    return pl.pallas_call(
        functools.partial(kern, d_model=d),
        out_shape=jax.ShapeDtypeStruct((t, d), F32),
        grid=(t // ROW_TILE,),
        in_specs=([row_spec(a) for a in row_inputs] + [const_spec(a) for a in const_inputs]
                  + [const_spec(w_bf16), row_spec(x2),
                     pl.BlockSpec((1, 1, 3 * d), lambda i: (i // tiles_per_seq, 0, 0)),
                     pl.BlockSpec((1, d), lambda i: (0, 0))]),
        out_specs=pl.BlockSpec((ROW_TILE, d), lambda i: (i, 0)),
        compiler_params=_cparams(("arbitrary",)),
        name=name,
    )(*row_inputs, *const_inputs, w_bf16, x2, mod3, post_g.reshape(1, d))


def _gdn_prep_kernel(x_ref, halo_ref, br_ref, cw_ref, alog_ref, dtb_ref,
                     w_ref, u_ref, qg_ref, kdt_ref, aqk_ref, dec_ref, xs_scr):
    n = GDN_STEP
    first = pl.program_id(1) == 0
    halo = halo_ref[...]
    xs_scr[0:SUBLANES, :] = jnp.where(first, jnp.zeros_like(halo), halo)
    xs_scr[SUBLANES:SUBLANES + n, :] = x_ref[...]
    cw = cw_ref[...]
    conv = xs_scr[SUBLANES:SUBLANES + n, :] * cw[C_CONV - 1:C_CONV]
    for j in range(C_CONV - 1):
        off = SUBLANES - (C_CONV - 1) + j
        conv = conv + xs_scr[off:off + n, :] * cw[j:j + 1]
    act = conv * _sigmoid(conv)

    row = lax.broadcasted_iota(jnp.int32, (n, n), 0)
    col = lax.broadcasted_iota(jnp.int32, (n, n), 1)
    same = (row // GDN_CHUNK) == (col // GDN_CHUNK)
    tril = jnp.logical_and(same, row >= col)
    strict = jnp.logical_and(same, row > col)
    eye = (row == col).astype(F32)

    br = br_ref[...]
    beta_all = _sigmoid(br)
    xg = br + dtb_ref[...]
    softplus = jnp.maximum(xg, 0.0) + jnp.log(1.0 + jnp.exp(-jnp.abs(xg)))
    g_all = -jnp.exp(alog_ref[...]) * softplus
    gc_all = jnp.dot(tril.astype(F32), g_all, preferred_element_type=F32, precision=HI)
    gc_t = gc_all.T
    first_chunk = lax.broadcasted_iota(jnp.int32, (n, 1), 0) < GDN_CHUNK
    dec_row = lax.broadcasted_iota(jnp.int32, (SUBLANES, LANES), 0)

    heads = range(C_HEADS)
    lanes = [slice(h * LANES, (h + 1) * LANES) for h in heads]
    gcols = [gc_all[:, C_HEADS + h:C_HEADS + h + 1] for h in heads]
    egs = [jnp.exp(g) for g in gcols]
    kbs, amats, invs = [], [], []
    for h in heads:
        grow = gc_t[C_HEADS + h:C_HEADS + h + 1, :]
        decay = jnp.where(tril, jnp.exp(jnp.where(tril, gcols[h] - grow, 0.0)), 0.0)
        q = act[:, h * C_DK:(h + 1) * C_DK]
        k = act[:, C_HEADS * C_DK + h * C_DK:C_HEADS * C_DK + (h + 1) * C_DK]
        q = q * lax.rsqrt(jnp.sum(q * q, axis=-1, keepdims=True) + EPS) * (C_DK ** -0.5)
        k = k * lax.rsqrt(jnp.sum(k * k, axis=-1, keepdims=True) + EPS)
        kb = k * beta_all[:, h:h + 1]
        k16 = k.astype(BF16)
        a = jnp.where(strict, lax.dot_general(kb.astype(BF16), k16, _NT, preferred_element_type=F32) * decay, 0.0)
        aqk = lax.dot_general(q.astype(BF16), k16, _NT, preferred_element_type=F32) * decay
        aqk_ref[:, lanes[h]] = aqk.astype(BF16)
        qg_ref[:, lanes[h]] = (q * egs[h]).astype(BF16)
        g_last = jnp.where(first_chunk, gcols[h][GDN_CHUNK - 1:GDN_CHUNK], gcols[h][n - 1:n])
        kdt_ref[:, lanes[h]] = (k * jnp.exp(g_last - gcols[h])).T.astype(BF16)
        dec_ref[0, :, lanes[h]] = jnp.where(dec_row < SUBLANES // 2, egs[h][GDN_CHUNK - 1:GDN_CHUNK],
                                             egs[h][n - 1:n])
        kbs.append(kb)
        amats.append(a)
        invs.append(eye - a)
    pows = []
    for h in heads:
        a16 = amats[h].astype(BF16)
        pows.append(jnp.dot(a16, a16, preferred_element_type=F32))
    for _ in range(4):
        for h in heads:
            p16 = pows[h].astype(BF16)
            both = jnp.dot(jnp.concatenate([invs[h].astype(BF16), p16], axis=0), p16, preferred_element_type=F32)
            invs[h] = invs[h] + both[0:n]
            pows[h] = both[n:]
    for h in heads:
        invs[h] = invs[h] + jnp.dot(invs[h].astype(BF16), pows[h].astype(BF16), preferred_element_type=F32)
    for h in heads:
        v = act[:, 2 * C_HEADS * C_DK + h * C_DV:2 * C_HEADS * C_DK + (h + 1) * C_DV]
        rhs = jnp.concatenate([v * beta_all[:, h:h + 1], kbs[h] * egs[h]], axis=1)
        uw = jnp.dot(invs[h].astype(BF16), rhs.astype(BF16), preferred_element_type=F32)
        u_ref[:, lanes[h]] = uw[:, 0:C_DV]
        w_ref[:, lanes[h]] = uw[:, C_DV:].astype(BF16)


def _gdn_rec_kernel(w_ref, u_ref, qg_ref, kdt_ref, aqk_ref, dec_ref, ng_ref, o_ref, s_scr):
    @pl.when(pl.program_id(1) == 0)
    def _():
        s_scr[...] = jnp.zeros_like(s_scr)

    half = SUBLANES // 2
    zeros = jnp.zeros((GDN_CHUNK, C_DV), BF16)
    heads = range(C_HEADS)
    lanes = [slice(h * LANES, (h + 1) * LANES) for h in heads]
    states = [s_scr[h] for h in heads]
    for j in range(GDN_STEP // GDN_CHUNK):
        rows = slice(j * GDN_CHUNK, (j + 1) * GDN_CHUNK)
        wss = []
        for h in heads:
            wq = jnp.concatenate([w_ref[rows, lanes[h]], qg_ref[rows, lanes[h]]], axis=0)
            wss.append(jnp.dot(wq, states[h].astype(BF16), preferred_element_type=F32))
        for h in heads:
            v_new = (u_ref[rows, lanes[h]] - wss[h][0:GDN_CHUNK]).astype(BF16)
            v_pad = jnp.concatenate([v_new, zeros] if j == 0 else [zeros, v_new], axis=0)
            o = wss[h][GDN_CHUNK:] + jnp.dot(aqk_ref[rows, lanes[h]], v_pad, preferred_element_type=F32)
            dec = dec_ref[0, j * half:j * half + 1, lanes[h]]
            states[h] = states[h] * dec + jnp.dot(kdt_ref[:, lanes[h]], v_pad, preferred_element_type=F32)
            ms = jnp.mean(o * o, axis=-1, keepdims=True)
            o_ref[rows, lanes[h]] = o * lax.rsqrt(ms + EPS) * ng_ref[...]
    for h in heads:
        s_scr[h] = states[h]


def _gdn_core(qkv_pre, br, bsz, seq, conv_w, a_log, dt_bias, norm_g):
    t, qkv_w = qkv_pre.shape
    hw = C_HEADS * C_DV
    steps_per_seq = seq // GDN_STEP
    halo_blocks = GDN_STEP // SUBLANES
    pad_row = lambda vec: jnp.zeros((1, LANES), F32).at[0, C_HEADS:2 * C_HEADS].set(vec.astype(F32))
    tok = lambda b, i: (b * steps_per_seq + i, 0)
    const = lambda b, i: (0, 0)
    tok_spec = pl.BlockSpec((GDN_STEP, hw), tok)
    dec_spec = pl.BlockSpec((1, SUBLANES, hw), lambda b, i: (b * steps_per_seq + i, 0, 0))
    w, u, qg, kdt, aqk, dec = pl.pallas_call(
        _gdn_prep_kernel,
        out_shape=[jax.ShapeDtypeStruct((t, hw), BF16), jax.ShapeDtypeStruct((t, hw), F32),
                   jax.ShapeDtypeStruct((t, hw), BF16), jax.ShapeDtypeStruct((t, hw), BF16),
                   jax.ShapeDtypeStruct((t, hw), BF16),
                   jax.ShapeDtypeStruct((t // GDN_STEP, SUBLANES, hw), F32)],
        grid=(bsz, steps_per_seq),
        in_specs=[pl.BlockSpec((GDN_STEP, qkv_w), tok),
                  pl.BlockSpec((SUBLANES, qkv_w),
                               lambda b, i: (jnp.maximum((b * steps_per_seq + i) * halo_blocks - 1, 0), 0)),
                  pl.BlockSpec((GDN_STEP, LANES), tok),
                  pl.BlockSpec((C_CONV, qkv_w), const),
                  pl.BlockSpec((1, LANES), const),
                  pl.BlockSpec((1, LANES), const)],
        out_specs=[tok_spec, tok_spec, tok_spec, tok_spec, tok_spec, dec_spec],
        scratch_shapes=[pltpu.VMEM((SUBLANES + GDN_STEP, qkv_w), F32)],
        compiler_params=_cparams(("arbitrary", "arbitrary")),
        name="gdn_prep",
    )(qkv_pre, qkv_pre, br, conv_w.astype(F32), pad_row(a_log), pad_row(dt_bias))
    return pl.pallas_call(
        _gdn_rec_kernel,
        out_shape=jax.ShapeDtypeStruct((t, hw), F32),
        grid=(bsz, steps_per_seq),
        in_specs=[tok_spec, tok_spec, tok_spec, tok_spec, tok_spec, dec_spec,
                  pl.BlockSpec((1, C_DV), const)],
        out_specs=tok_spec,
        scratch_shapes=[pltpu.VMEM((C_HEADS, C_DK, C_DV), F32)],
        compiler_params=_cparams(("arbitrary", "arbitrary")),
        name="gdn_recurrence",
    )(w, u, qg, kdt, aqk, dec, norm_g.reshape(1, C_DV).astype(F32))


def kernel(x, c, ada_w, ada_b, pre_g, post_g, rel_bias, ab_w_in, ab_w_out, s5_a_re, s5_a_im, s5_log_dt, s5_b_re, s5_b_im, s5_c_re, s5_c_im, s5_d, s5_glu_w, s5_glu_b, gdn_w_in, gdn_conv, gdn_a_log, gdn_dt_bias, gdn_norm_g, gdn_w_out):
    bsz, seq, d = x.shape
    depth = ada_w.shape[0]
    assert seq % (A_BLOCK * max(dl for _, dl in DILATED_CONFIGS)) == 0 and seq % ROW_TILE == 0
    x2 = x.reshape(bsz * seq, d)
    mod = _adaln_mod(c, ada_w, ada_b)
    b_width = s5_d.shape[-1]
    head_expand = jnp.asarray(
        (np.arange(LANES)[:, None] == (np.arange(A_WIDTH)[None, :] // A_HEAD_DIM)).astype(np.float32))

    for layer in range(depth):
        j = layer // 2
        mod3 = mod[layer].reshape(bsz, 1, 3 * d)
        if layer % 2 == 0:
            splits = ((0, A_WIDTH, A_HEAD_DIM ** -0.5), (A_WIDTH, A_WIDTH, 1.0), (2 * A_WIDTH, A_WIDTH, 1.0),
                      (3 * A_WIDTH, b_width, 1.0), (3 * A_WIDTH + b_width, A_WIDTH + b_width, 1.0))
            q, k, v, u, gate = _in_proj(x2, mod3, pre_g[layer], ab_w_in[j].astype(BF16), splits,
                                        (BF16, BF16, BF16, F32, BF16), seq)
            outs, lses = [], []
            bias_tabs = _attn_bias_tables(rel_bias)
            for cfg, (_, dil) in enumerate(DILATED_CONFIGS):
                o_c, lse_c = _dilated_attention_one(q, k, v, bias_tabs[cfg], bsz, seq, dil)
                outs.append(o_c)
                lses.append(lse_c)
            o_b = _s5_layer(u, bsz, seq, s5_a_re[j], s5_a_im[j], s5_log_dt[j], s5_b_re[j], s5_b_im[j],
                            s5_c_re[j], s5_c_im[j], s5_d[j], s5_glu_w[j], s5_glu_b[j])
            x2 = _out_proj(_out_ab_kernel, outs + lses + [o_b, gate], [head_expand], ab_w_out[j].astype(BF16),
                           x2, mod3, post_g[layer], seq, "out_proj_ab")
        else:
            qkv_w = 2 * C_HEADS * C_DK + C_HEADS * C_DV
            gate_w = C_HEADS * C_DV
            w_in = gdn_w_in[j]
            w_pad = jnp.concatenate(
                [w_in, jnp.zeros((d, LANES - (w_in.shape[1] - qkv_w - gate_w)), w_in.dtype)], axis=1).astype(BF16)
            splits = ((0, qkv_w, 1.0), (qkv_w, gate_w, 1.0), (qkv_w + gate_w, LANES, 1.0))
            qkv_pre, gate, br = _in_proj(x2, mod3, pre_g[layer], w_pad, splits, (F32, BF16, F32), seq)
            o = _gdn_core(qkv_pre, br, bsz, seq, gdn_conv[j], gdn_a_log[j], gdn_dt_bias[j], gdn_norm_g[j])
            x2 = _out_proj(_out_c_kernel, [o, gate], [], gdn_w_out[j].astype(BF16),
                           x2, mod3, post_g[layer], seq, "out_proj_c")
    return x2.reshape(bsz, seq, d)
```

```python
import functools
import math

import numpy as np
import jax
import jax.numpy as jnp
from jax import lax
from jax.experimental import pallas as pl
from jax.experimental.pallas import tpu as pltpu

F32 = jnp.float32
BF16 = jnp.bfloat16
HI = lax.Precision.HIGHEST

EPS = 1e-6
A_HEADS = 8
A_HEAD_DIM = 64
A_WIDTH = A_HEADS * A_HEAD_DIM
A_BLOCK = 128
DILATED_CONFIGS = ((128, 1), (512, 4), (2048, 16))
B_GROUP = 16
B_STATE = 64
REL_BUCKETS = 32
REL_MAX_DIST = 2048
C_HEADS = 8
C_DK = 128
C_DV = 128
C_CONV = 4
MASK_NEG = -1e30

LANES = 128
SUBLANES = 8
VMEM_LIMIT = 48 * 1024 * 1024
ROW_TILE = 512
COL_CHUNK = 512
SPAN_RES = max(dl for _, dl in DILATED_CONFIGS)
SPAN = A_BLOCK * SPAN_RES
SPAN_RUN = ROW_TILE // SPAN_RES
S5_J = 32
S5_LW = 512
GDN_STEP = 128
GDN_CHUNK = 64

_NT = (((1,), (1,)), ((), ()))


def _cparams(sem):
    return pltpu.CompilerParams(dimension_semantics=sem, vmem_limit_bytes=VMEM_LIMIT)


def _sigmoid(x):
    return 1.0 / (1.0 + jnp.exp(-x))


def _mod_kernel(c_ref, w_ref, b_ref, o_ref):
    c = c_ref[...]
    ca = c * _sigmoid(c)
    o_ref[0] = jnp.dot(ca, w_ref[0], preferred_element_type=F32, precision=HI) + b_ref[0]


def _adaln_mod(c, ada_w, ada_b):
    depth, d, d3 = ada_w.shape
    bsz = c.shape[0]
    return pl.pallas_call(
        _mod_kernel,
        out_shape=jax.ShapeDtypeStruct((depth, bsz, d3), F32),
        grid=(depth, d3 // d),
        in_specs=[pl.BlockSpec((bsz, d), lambda l, j: (0, 0)),
                  pl.BlockSpec((1, d, d), lambda l, j: (l, 0, j)),
                  pl.BlockSpec((1, 1, d), lambda l, j: (l, 0, j))],
        out_specs=pl.BlockSpec((1, bsz, d), lambda l, j: (l, 0, j)),
        compiler_params=_cparams(("arbitrary", "arbitrary")),
        name="adaln_mod",
    )(c, ada_w, ada_b.reshape(depth, 1, d3))


def _span_perm():
    rho = np.arange(ROW_TILE)
    nat = SPAN_RES * (rho % SPAN_RUN) + rho // SPAN_RUN
    return (nat[:, None] == np.arange(ROW_TILE)[None, :]).astype(np.float32)


def _in_proj_kernel(x_ref, mod_ref, g_ref, w_ref, *rest, splits, d_model, n_span):
    pm_ref = rest[0] if n_span else None
    out_refs = rest[1:] if n_span else rest
    span_refs = out_refs[len(splits):]
    x = x_ref[...]
    ms = jnp.mean(x * x, axis=-1, keepdims=True)
    y = x * lax.rsqrt(ms + EPS) * g_ref[...]
    shift = mod_ref[0, :, 0:d_model]
    scale = mod_ref[0, :, d_model:2 * d_model]
    h = (y * (1.0 + scale) + shift).astype(BF16)
    for idx, ((c0, width, mult), o_ref) in enumerate(zip(splits, out_refs)):
        for cc in range(0, width, COL_CHUNK):
            cw = min(COL_CHUNK, width - cc)
            acc = jnp.dot(h, w_ref[:, c0 + cc:c0 + cc + cw], preferred_element_type=F32)
            if mult != 1.0:
                acc = acc * mult
            val = acc.astype(o_ref.dtype)
            o_ref[:, cc:cc + cw] = val
            if idx < n_span:
                moved = jnp.dot(pm_ref[...], val, preferred_element_type=F32).astype(BF16)
                span_refs[idx][:, :, cc:cc + cw] = moved.reshape(SPAN_RES, SPAN_RUN, cw)


def _in_proj(x2, mod3, gain, w_bf16, splits, out_dtypes, seq, n_span=0):
    t, d = x2.shape
    tiles_per_seq = seq // ROW_TILE
    tiles_per_span = SPAN // ROW_TILE
    n_w = w_bf16.shape[1]
    row = lambda wd: pl.BlockSpec((ROW_TILE, wd), lambda i: (i, 0))
    span_shape = lambda wd: jax.ShapeDtypeStruct((t // SPAN, SPAN_RES, A_BLOCK, wd), BF16)
    span_spec = lambda wd: pl.BlockSpec((None, SPAN_RES, SPAN_RUN, wd),
                                        lambda i: (i // tiles_per_span, 0, i % tiles_per_span, 0))
    perm_in = [jnp.asarray(_span_perm(), BF16)] if n_span else []
    perm_spec = [pl.BlockSpec((ROW_TILE, ROW_TILE), lambda i: (0, 0))] if n_span else []
    return pl.pallas_call(
        functools.partial(_in_proj_kernel, splits=splits, d_model=d, n_span=n_span),
        out_shape=([jax.ShapeDtypeStruct((t, wd), dt) for (_, wd, _), dt in zip(splits, out_dtypes)]
                   + [span_shape(wd) for (_, wd, _) in splits[:n_span]]),
        grid=(t // ROW_TILE,),
        in_specs=[row(d),
                  pl.BlockSpec((1, 1, 3 * d), lambda i: (i // tiles_per_seq, 0, 0)),
                  pl.BlockSpec((1, d), lambda i: (0, 0)),
                  pl.BlockSpec((d, n_w), lambda i: (0, 0))] + perm_spec,
        out_specs=[row(wd) for (_, wd, _) in splits] + [span_spec(wd) for (_, wd, _) in splits[:n_span]],
        compiler_params=_cparams(("arbitrary",)),
        name="in_proj",
    )(x2, mod3, gain.reshape(1, d), w_bf16, *perm_in)


def _t5_bucket(dist):
    dist = np.maximum(dist, 0)
    max_exact = REL_BUCKETS // 2
    large = max_exact + (np.log(np.maximum(dist, 1) / max_exact)
                         / math.log(REL_MAX_DIST / max_exact) * (REL_BUCKETS - max_exact)).astype(np.int32)
    large = np.minimum(large, REL_BUCKETS - 1)
    return np.where(dist < max_exact, dist, large).astype(np.int32)


def _bias_kernel(rb_ref, bucket_ref, mask_ref, o_ref):
    def body(r, carry):
        rows = pl.ds(pl.multiple_of(r * SUBLANES, SUBLANES), SUBLANES)
        bk = bucket_ref[0, rows, :]
        accs = [jnp.zeros(bk.shape, F32) for _ in range(A_HEADS)]
        for b in range(REL_BUCKETS):
            eq = bk == b
            accs = [jnp.where(eq, rb_ref[b, h], acc) for h, acc in enumerate(accs)]
        for f in range(2):
            keep = mask_ref[0, f, rows, :] != 0
            for h in range(A_HEADS):
                o_ref[0, f, h, rows, :] = jnp.where(keep, accs[h], MASK_NEG)
        return carry
    lax.fori_loop(0, A_BLOCK // SUBLANES, body, 0)


def _attn_bias_tables(rel_bias):
    qi = np.arange(A_BLOCK)[:, None]
    kj = np.arange(2 * A_BLOCK)[None, :]
    rel = qi + A_BLOCK - kj
    buckets, masks = [], []
    for window, dil in DILATED_CONFIGS:
        band = (rel >= 0) & (rel <= window // dil)
        bucket = _t5_bucket(rel * dil)
        mask = np.stack([band & (kj >= A_BLOCK), band]).astype(np.int32)
        if dil > 1:
            runs = SPAN_RES // dil
            run = A_BLOCK // runs
            rho = np.arange(A_BLOCK)
            sub = runs * (rho % run) + rho // run
            keys = np.concatenate([sub, A_BLOCK + sub])
            bucket = bucket[sub][:, keys]
            mask = mask[:, sub][:, :, keys]
        buckets.append(bucket)
        masks.append(mask)
    n_cfg = len(DILATED_CONFIGS)
    return pl.pallas_call(
        _bias_kernel,
        out_shape=jax.ShapeDtypeStruct((n_cfg, 2, A_HEADS, A_BLOCK, 2 * A_BLOCK), F32),
        grid=(n_cfg,),
        in_specs=[pl.BlockSpec(memory_space=pltpu.SMEM),
                  pl.BlockSpec((1, A_BLOCK, 2 * A_BLOCK), lambda c: (c, 0, 0)),
                  pl.BlockSpec((1, 2, A_BLOCK, 2 * A_BLOCK), lambda c: (c, 0, 0, 0))],
        out_specs=pl.BlockSpec((1, 2, A_HEADS, A_BLOCK, 2 * A_BLOCK), lambda c: (c, 0, 0, 0, 0)),
        compiler_params=_cparams(("arbitrary",)),
        name="attn_bias",
    )(rel_bias.astype(F32), jnp.asarray(np.stack(buckets)), jnp.asarray(np.stack(masks)))


def _attn_kernel(q_ref, kp_ref, kc_ref, vp_ref, vc_ref, bias_ref, o_ref, lse_ref):
    rows = lambda ref: ref[...].reshape(A_BLOCK, ref.shape[-1])
    q = rows(q_ref)
    kcat = jnp.concatenate([rows(kp_ref), rows(kc_ref)], axis=0)
    vcat = jnp.concatenate([rows(vp_ref), rows(vc_ref)], axis=0)
    lane = lax.broadcasted_iota(jnp.int32, (A_BLOCK, LANES), 1)
    low = lane < A_HEAD_DIM
    lse_tile = jnp.zeros((A_BLOCK, LANES), F32)
    zero = jnp.zeros((A_BLOCK, LANES), BF16)
    heads = range(A_HEADS)
    pair = [slice((h // 2) * LANES, (h // 2 + 1) * LANES) for h in heads]
    scores = []
    for h in heads:
        qm = jnp.where(low if h % 2 == 0 else jnp.logical_not(low), q[:, pair[h]], zero)
        scores.append(lax.dot_general(qm, kcat[:, pair[h]], _NT, preferred_element_type=F32) + bias_ref[0, h])
    probs, inv_l = [], []
    for h in heads:
        m = jnp.max(scores[h], axis=-1, keepdims=True)
        p = jnp.exp(scores[h] - m)
        l = jnp.sum(p, axis=-1, keepdims=True)
        probs.append(p.astype(BF16))
        inv_l.append(1.0 / l)
        lse_tile = jnp.where(lane == h, m + jnp.log(l), lse_tile)
    outs = [jnp.dot(probs[h], vcat[:, pair[h]], preferred_element_type=F32) * inv_l[h] for h in heads]
    o = jnp.concatenate([jnp.where(low, outs[2 * hp], outs[2 * hp + 1]) for hp in range(A_HEADS // 2)], axis=1)
    o_ref[...] = o.astype(o_ref.dtype).reshape(o_ref.shape)
    lse_ref[...] = lse_tile.reshape(lse_ref.shape)


def _dilated_attention_one(q, k, v, bias_tab, bsz, seq, dil):
    nb = seq // dil // A_BLOCK
    w = A_WIDTH
    if dil == 1:
        view = lambda a: a.reshape(bsz, seq, a.shape[-1])
        block = lambda wd: (None, A_BLOCK, wd)
        cur = lambda b, r, n: (b, n, 0)
        prev = lambda b, r, n: (b, jnp.maximum(n - 1, 0), 0)
        out_shape = lambda wd, dt: jax.ShapeDtypeStruct((bsz, seq, wd), dt)
    else:
        runs = SPAN_RES // dil
        run = A_BLOCK // runs
        spans_per_seq = seq // SPAN
        view = lambda a: a.reshape(a.shape[0], runs, dil, A_BLOCK, a.shape[-1])
        block = lambda wd: (None, runs, None, run, wd)
        cur = lambda b, r, n: (b * spans_per_seq + n // runs, 0, r, n % runs, 0)
        prev = lambda b, r, n: cur(b, r, jnp.maximum(n - 1, 0))
        out_shape = lambda wd, dt: jax.ShapeDtypeStruct((bsz * spans_per_seq, runs, dil, A_BLOCK, wd), dt)
    blk = pl.BlockSpec(block(w), cur)
    blk_prev = pl.BlockSpec(block(w), prev)
    o, lse = pl.pallas_call(
        _attn_kernel,
        out_shape=[out_shape(w, BF16), out_shape(LANES, F32)],
        grid=(bsz, dil, nb),
        in_specs=[blk, blk_prev, blk, blk_prev, blk,
                  pl.BlockSpec((1, A_HEADS, A_BLOCK, 2 * A_BLOCK), lambda b, r, n: (jnp.minimum(n, 1), 0, 0, 0))],
        out_specs=[pl.BlockSpec(block(w), cur), pl.BlockSpec(block(LANES), cur)],
        compiler_params=_cparams(("arbitrary", "arbitrary", "arbitrary")),
        name=f"dilated_attn_d{dil}",
    )(view(q), view(k), view(k), view(v), view(v), bias_tab)
    if dil == 1:
        return o.reshape(bsz * seq, w), lse.reshape(bsz * seq, LANES)
    return o.reshape(o.shape[0], SPAN_RES, A_BLOCK, w), lse.reshape(o.shape[0], SPAN_RES, A_BLOCK, LANES)


def _gelu_tanh(x):
    c = math.sqrt(2.0 / math.pi)
    return x * (0.5 * (1.0 + jnp.tanh(c * (x + 0.044715 * (x * x * x)))))


def _s5_kernel(u_ref, pm_ref, pmt_ref, bd_ref, pw_ref, cd_ref, dsk_ref, gw_ref, gb_ref, o_ref, x_scr, st_scr, *,
               n_state):
    n = n_state
    jn = S5_J

    @pl.when(pl.program_id(1) == 0)
    def _():
        st_scr[...] = jnp.zeros_like(st_scr)

    u = u_ref[...]
    u_perm = jnp.dot(pm_ref[...], u.astype(BF16), preferred_element_type=F32).astype(BF16)
    x_scr[...] = jnp.dot(u_perm, bd_ref[...], preferred_element_type=F32)

    for c0 in range(0, n, S5_LW):
        re = slice(c0, c0 + S5_LW)
        im = slice(n + c0, n + c0 + S5_LW)
        a1r, a1i = pw_ref[0:1, re], pw_ref[0:1, im]
        xr = jnp.zeros((SUBLANES, S5_LW), F32)
        xi = jnp.zeros((SUBLANES, S5_LW), F32)
        for j in range(jn):
            rows = slice(SUBLANES * j, SUBLANES * (j + 1))
            nr = a1r * xr - a1i * xi + x_scr[rows, re]
            ni = a1r * xi + a1i * xr + x_scr[rows, im]
            xr, xi = nr, ni
            x_scr[rows, re] = xr
            x_scr[rows, im] = xi
        ajr, aji = pw_ref[jn - 1:jn, re], pw_ref[jn - 1:jn, im]
        pr, pi = st_scr[0:1, re], st_scr[0:1, im]
        cin_r, cin_i = [], []
        for s in range(SUBLANES):
            cin_r.append(pr)
            cin_i.append(pi)
            nr = xr[s:s + 1] + ajr * pr - aji * pi
            ni = xi[s:s + 1] + ajr * pi + aji * pr
            pr, pi = nr, ni
        st_scr[0:1, re] = pr
        st_scr[0:1, im] = pi
        cr = jnp.concatenate(cin_r, axis=0)
        ci = jnp.concatenate(cin_i, axis=0)
        for j in range(jn):
            rows = slice(SUBLANES * j, SUBLANES * (j + 1))
            pjr, pji = pw_ref[j:j + 1, re], pw_ref[j:j + 1, im]
            x_scr[rows, re] = x_scr[rows, re] + (pjr * cr - pji * ci)
            x_scr[rows, im] = x_scr[rows, im] + (pjr * ci + pji * cr)

    y_perm = jnp.dot(x_scr[...].astype(BF16), cd_ref[...], preferred_element_type=F32)
    y_hi = y_perm.astype(BF16)
    y_lo = (y_perm - y_hi.astype(F32)).astype(BF16)
    y = (jnp.dot(pmt_ref[...], y_hi, preferred_element_type=F32)
         + jnp.dot(pmt_ref[...], y_lo, preferred_element_type=F32))
    y = _gelu_tanh(y + dsk_ref[...] * u)
    z = jnp.dot(y.astype(BF16), gw_ref[...], preferred_element_type=F32) + gb_ref[...]
    o_ref[...] = y * _sigmoid(z)


def _s5_tables(a_re, a_im, log_dt, b_re, b_im, c_re, c_im):
    g, p = a_re.shape
    dt = jnp.exp(log_dt.astype(F32))[:, None]
    ar, ai = a_re.astype(F32), a_im.astype(F32)
    mag = jnp.exp(dt * ar)
    abar_r, abar_i = mag * jnp.cos(dt * ai), mag * jnp.sin(dt * ai)
    den = ar * ar + ai * ai
    fr = ((abar_r - 1.0) * ar + abar_i * ai) / den
    fi = (abar_i * ar - (abar_r - 1.0) * ai) / den
    br, bi = b_re.astype(F32), b_im.astype(F32)
    bbar_r = fr[..., None] * br - fi[..., None] * bi
    bbar_i = fr[..., None] * bi + fi[..., None] * br
    eye = jnp.eye(g, dtype=F32)
    m = br.shape[-1]
    dense_b = lambda t: jnp.einsum('gpm,gh->gmhp', t, eye).reshape(g * m, g * p)
    bd = jnp.concatenate([dense_b(bbar_r), dense_b(bbar_i)], axis=1)
    dense_c = lambda t: jnp.einsum('gmp,gh->gphm', t, eye).reshape(g * p, g * m)
    cd = jnp.concatenate([dense_c(c_re.astype(F32)), -dense_c(c_im.astype(F32))], axis=0)
    kk = jnp.arange(1, S5_J + 1, dtype=F32)[:, None, None]
    pmag = jnp.exp(kk * (dt * ar)[None])
    pw_r = (pmag * jnp.cos(kk * (dt * ai)[None])).reshape(S5_J, g * p)
    pw_i = (pmag * jnp.sin(kk * (dt * ai)[None])).reshape(S5_J, g * p)
    pw = jnp.concatenate([pw_r, pw_i], axis=1)
    return bd.astype(BF16), pw, cd.astype(BF16)


def _s5_layer(u, bsz, seq, a_re, a_im, log_dt, b_re, b_im, c_re, c_im, d_skip, glu_w, glu_b):
    t, width = u.shape
    n = a_re.shape[0] * a_re.shape[1]
    bd, pw, cd = _s5_tables(a_re, a_im, log_dt, b_re, b_im, c_re, c_im)
    tile = SUBLANES * S5_J
    tiles_per_seq = seq // tile
    const = lambda b, i: (0, 0)
    src = (np.arange(tile) % SUBLANES) * S5_J + np.arange(tile) // SUBLANES
    perm = (src[:, None] == np.arange(tile)[None, :]).astype(np.float32)
    return pl.pallas_call(
        functools.partial(_s5_kernel, n_state=n),
        out_shape=jax.ShapeDtypeStruct((t, width), F32),
        grid=(bsz, tiles_per_seq),
        in_specs=[pl.BlockSpec((tile, width), lambda b, i: (b * tiles_per_seq + i, 0)),
                  pl.BlockSpec((tile, tile), const),
                  pl.BlockSpec((tile, tile), const),
                  pl.BlockSpec((width, 2 * n), const),
                  pl.BlockSpec((S5_J, 2 * n), const),
                  pl.BlockSpec((2 * n, width), const),
                  pl.BlockSpec((1, width), const),
                  pl.BlockSpec((width, width), const),
                  pl.BlockSpec((1, width), const)],
        out_specs=pl.BlockSpec((tile, width), lambda b, i: (b * tiles_per_seq + i, 0)),
        scratch_shapes=[pltpu.VMEM((tile, 2 * n), F32), pltpu.VMEM((SUBLANES, 2 * n), F32)],
        compiler_params=_cparams(("arbitrary", "arbitrary")),
        name="s5_layer",
    )(u, jnp.asarray(perm, BF16), jnp.asarray(perm.T, BF16), bd, pw, cd,
      d_skip.reshape(1, width).astype(F32), glu_w.astype(BF16),
      glu_b.reshape(1, width).astype(F32))


def _finish(y, x_ref, mod_ref, pg_ref, out_ref, d_model):
    ms = jnp.mean(y * y, axis=-1, keepdims=True)
    yn = y * lax.rsqrt(ms + EPS) * pg_ref[...]
    gate_mod = mod_ref[0, :, 2 * d_model:3 * d_model]
    out_ref[...] = x_ref[...] + gate_mod * yn


def _split3(x):
    hi = x.astype(BF16)
    r1 = x - hi.astype(F32)
    mid = r1.astype(BF16)
    return hi, mid, (r1 - mid.astype(F32)).astype(BF16)


def _out_ab_kernel(o0_ref, l0_ref, o1_ref, o2_ref, l1_ref, l2_ref, ob_ref, gate_ref, e_ref, un_ref, w_ref,
                   x_ref, mod_ref, pg_ref, out_ref, *, d_model):
    un = un_ref[...]
    flat = lambda ref: ref[...].reshape(ROW_TILE, ref.shape[-1])
    unperm_bf16 = lambda ref: jnp.dot(un, flat(ref), preferred_element_type=F32)
    unperm_f32 = lambda ref: sum(jnp.dot(un, part, preferred_element_type=F32) for part in _split3(flat(ref)))
    o0, o1, o2 = o0_ref[...].astype(F32), unperm_bf16(o1_ref), unperm_bf16(o2_ref)
    l0, l1, l2 = l0_ref[...], unperm_f32(l1_ref), unperm_f32(l2_ref)
    mx = jnp.maximum(jnp.maximum(l0, l1), l2)
    e0, e1, e2 = jnp.exp(l0 - mx), jnp.exp(l1 - mx), jnp.exp(l2 - mx)
    inv_den = 1.0 / (e0 + e1 + e2)
    expand = lambda wgt: sum(jnp.dot(part, e_ref[...], preferred_element_type=F32) for part in _split3(wgt))
    o_a = expand(e0 * inv_den) * o0 + expand(e1 * inv_den) * o1 + expand(e2 * inv_den) * o2
    gate = gate_ref[...].astype(F32)
    sg = gate * _sigmoid(gate)
    aw = o_a.shape[-1]
    y = jnp.dot((o_a * sg[:, :aw]).astype(BF16), w_ref[0:aw, :], preferred_element_type=F32)
    y = y + jnp.dot((ob_ref[...] * sg[:, aw:]).astype(BF16), w_ref[aw:, :], preferred_element_type=F32)
    _finish(y, x_ref, mod_ref, pg_ref, out_ref, d_model)


def _out_c_kernel(o_ref, gate_ref, w_ref, x_ref, mod_ref, pg_ref, out_ref, *, d_model):
    gate = gate_ref[...].astype(F32)
    o = o_ref[...].astype(F32) * (gate * _sigmoid(gate))
    y = jnp.dot(o.astype(BF16), w_ref[...], preferred_element_type=F32)
    _finish(y, x_ref, mod_ref, pg_ref, out_ref, d_model)


def _out_proj(kern, row_inputs, const_inputs, w_bf16, x2, mod3, post_g, seq, name):
    t, d = x2.shape
    tiles_per_seq = seq // ROW_TILE
    tiles_per_span = SPAN // ROW_TILE

    def row_spec(a):
        if a.ndim == 2:
            return pl.BlockSpec((ROW_TILE, a.shape[1]), lambda i: (i, 0))
        return pl.BlockSpec((None, SPAN_RES, SPAN_RUN, a.shape[-1]),
                            lambda i: (i // tiles_per_span, 0, i % tiles_per_span, 0))

    const_spec = lambda a: pl.BlockSpec(a.shape, lambda i: (0, 0))
    return pl.pallas_call(
        functools.partial(kern, d_model=d),
        out_shape=jax.ShapeDtypeStruct((t, d), F32),
        grid=(t // ROW_TILE,),
        in_specs=([row_spec(a) for a in row_inputs] + [const_spec(a) for a in const_inputs]
                  + [const_spec(w_bf16), row_spec(x2),
                     pl.BlockSpec((1, 1, 3 * d), lambda i: (i // tiles_per_seq, 0, 0)),
                     pl.BlockSpec((1, d), lambda i: (0, 0))]),
        out_specs=pl.BlockSpec((ROW_TILE, d), lambda i: (i, 0)),
        compiler_params=_cparams(("arbitrary",)),
        name=name,
    )(*row_inputs, *const_inputs, w_bf16, x2, mod3, post_g.reshape(1, d))


def _gdn_prep_kernel(x_ref, halo_ref, br_ref, cw_ref, alog_ref, dtb_ref,
                     w_ref, u_ref, qg_ref, kdt_ref, aqk_ref, dec_ref, xs_scr):
    n = GDN_STEP
    first = pl.program_id(1) == 0
    halo = halo_ref[...]
    xs_scr[0:SUBLANES, :] = jnp.where(first, jnp.zeros_like(halo), halo)
    xs_scr[SUBLANES:SUBLANES + n, :] = x_ref[...]
    cw = cw_ref[...]
    conv = xs_scr[SUBLANES:SUBLANES + n, :] * cw[C_CONV - 1:C_CONV]
    for j in range(C_CONV - 1):
        off = SUBLANES - (C_CONV - 1) + j
        conv = conv + xs_scr[off:off + n, :] * cw[j:j + 1]
    act = conv * _sigmoid(conv)

    row = lax.broadcasted_iota(jnp.int32, (n, n), 0)
    col = lax.broadcasted_iota(jnp.int32, (n, n), 1)
    same = (row // GDN_CHUNK) == (col // GDN_CHUNK)
    tril = jnp.logical_and(same, row >= col)
    strict = jnp.logical_and(same, row > col)
    eye = (row == col).astype(F32)

    br = br_ref[...]
    beta_all = _sigmoid(br)
    xg = br + dtb_ref[...]
    softplus = jnp.maximum(xg, 0.0) + jnp.log(1.0 + jnp.exp(-jnp.abs(xg)))
    g_all = -jnp.exp(alog_ref[...]) * softplus
    gc_all = jnp.dot(tril.astype(F32), g_all, preferred_element_type=F32, precision=HI)
    gc_t = gc_all.T
    first_chunk = lax.broadcasted_iota(jnp.int32, (n, 1), 0) < GDN_CHUNK
    dec_row = lax.broadcasted_iota(jnp.int32, (SUBLANES, LANES), 0)

    heads = range(C_HEADS)
    lanes = [slice(h * LANES, (h + 1) * LANES) for h in heads]
    gcols = [gc_all[:, C_HEADS + h:C_HEADS + h + 1] for h in heads]
    egs = [jnp.exp(g) for g in gcols]
    kbs, amats, invs = [], [], []
    for h in heads:
        grow = gc_t[C_HEADS + h:C_HEADS + h + 1, :]
        decay = jnp.where(tril, jnp.exp(jnp.where(tril, gcols[h] - grow, 0.0)), 0.0)
        q = act[:, h * C_DK:(h + 1) * C_DK]
        k = act[:, C_HEADS * C_DK + h * C_DK:C_HEADS * C_DK + (h + 1) * C_DK]
        q = q * lax.rsqrt(jnp.sum(q * q, axis=-1, keepdims=True) + EPS) * (C_DK ** -0.5)
        k = k * lax.rsqrt(jnp.sum(k * k, axis=-1, keepdims=True) + EPS)
        kb = k * beta_all[:, h:h + 1]
        k16 = k.astype(BF16)
        a = jnp.where(strict, lax.dot_general(kb.astype(BF16), k16, _NT, preferred_element_type=F32) * decay, 0.0)
        aqk = lax.dot_general(q.astype(BF16), k16, _NT, preferred_element_type=F32) * decay
        aqk_ref[:, lanes[h]] = aqk.astype(BF16)
        qg_ref[:, lanes[h]] = (q * egs[h]).astype(BF16)
        g_last = jnp.where(first_chunk, gcols[h][GDN_CHUNK - 1:GDN_CHUNK], gcols[h][n - 1:n])
        kdt_ref[:, lanes[h]] = (k * jnp.exp(g_last - gcols[h])).T.astype(BF16)
        dec_ref[0, :, lanes[h]] = jnp.where(dec_row < SUBLANES // 2, egs[h][GDN_CHUNK - 1:GDN_CHUNK],
                                             egs[h][n - 1:n])
        kbs.append(kb)
        amats.append(a)
        invs.append(eye - a)
    pows = []
    for h in heads:
        a16 = amats[h].astype(BF16)
        pows.append(jnp.dot(a16, a16, preferred_element_type=F32))
    for _ in range(4):
        for h in heads:
            p16 = pows[h].astype(BF16)
            both = jnp.dot(jnp.concatenate([invs[h].astype(BF16), p16], axis=0), p16, preferred_element_type=F32)
            invs[h] = invs[h] + both[0:n]
            pows[h] = both[n:]
    for h in heads:
        invs[h] = invs[h] + jnp.dot(invs[h].astype(BF16), pows[h].astype(BF16), preferred_element_type=F32)
    for h in heads:
        v = act[:, 2 * C_HEADS * C_DK + h * C_DV:2 * C_HEADS * C_DK + (h + 1) * C_DV]
        rhs = jnp.concatenate([v * beta_all[:, h:h + 1], kbs[h] * egs[h]], axis=1)
        uw = jnp.dot(invs[h].astype(BF16), rhs.astype(BF16), preferred_element_type=F32)
        u_ref[:, lanes[h]] = uw[:, 0:C_DV]
        w_ref[:, lanes[h]] = uw[:, C_DV:].astype(BF16)


def _gdn_rec_kernel(w_ref, u_ref, qg_ref, kdt_ref, aqk_ref, dec_ref, ng_ref, o_ref, s_scr):
    @pl.when(pl.program_id(1) == 0)
    def _():
        s_scr[...] = jnp.zeros_like(s_scr)

    half = SUBLANES // 2
    zeros = jnp.zeros((GDN_CHUNK, C_DV), BF16)
    heads = range(C_HEADS)
    lanes = [slice(h * LANES, (h + 1) * LANES) for h in heads]
    states = [s_scr[h] for h in heads]
    for j in range(GDN_STEP // GDN_CHUNK):
        rows = slice(j * GDN_CHUNK, (j + 1) * GDN_CHUNK)
        wss = []
        for h in heads:
            wq = jnp.concatenate([w_ref[rows, lanes[h]], qg_ref[rows, lanes[h]]], axis=0)
            wss.append(jnp.dot(wq, states[h].astype(BF16), preferred_element_type=F32))
        for h in heads:
            v_new = (u_ref[rows, lanes[h]] - wss[h][0:GDN_CHUNK]).astype(BF16)
            v_pad = jnp.concatenate([v_new, zeros] if j == 0 else [zeros, v_new], axis=0)
            o = wss[h][GDN_CHUNK:] + jnp.dot(aqk_ref[rows, lanes[h]], v_pad, preferred_element_type=F32)
            dec = dec_ref[0, j * half:j * half + 1, lanes[h]]
            states[h] = states[h] * dec + jnp.dot(kdt_ref[:, lanes[h]], v_pad, preferred_element_type=F32)
            ms = jnp.mean(o * o, axis=-1, keepdims=True)
            o_ref[rows, lanes[h]] = o * lax.rsqrt(ms + EPS) * ng_ref[...]
    for h in heads:
        s_scr[h] = states[h]


def _gdn_core(qkv_pre, br, bsz, seq, conv_w, a_log, dt_bias, norm_g):
    t, qkv_w = qkv_pre.shape
    hw = C_HEADS * C_DV
    steps_per_seq = seq // GDN_STEP
    halo_blocks = GDN_STEP // SUBLANES
    pad_row = lambda vec: jnp.zeros((1, LANES), F32).at[0, C_HEADS:2 * C_HEADS].set(vec.astype(F32))
    tok = lambda b, i: (b * steps_per_seq + i, 0)
    const = lambda b, i: (0, 0)
    tok_spec = pl.BlockSpec((GDN_STEP, hw), tok)
    dec_spec = pl.BlockSpec((1, SUBLANES, hw), lambda b, i: (b * steps_per_seq + i, 0, 0))
    w, u, qg, kdt, aqk, dec = pl.pallas_call(
        _gdn_prep_kernel,
        out_shape=[jax.ShapeDtypeStruct((t, hw), BF16), jax.ShapeDtypeStruct((t, hw), F32),
                   jax.ShapeDtypeStruct((t, hw), BF16), jax.ShapeDtypeStruct((t, hw), BF16),
                   jax.ShapeDtypeStruct((t, hw), BF16),
                   jax.ShapeDtypeStruct((t // GDN_STEP, SUBLANES, hw), F32)],
        grid=(bsz, steps_per_seq),
        in_specs=[pl.BlockSpec((GDN_STEP, qkv_w), tok),
                  pl.BlockSpec((SUBLANES, qkv_w),
                               lambda b, i: (jnp.maximum((b * steps_per_seq + i) * halo_blocks - 1, 0), 0)),
                  pl.BlockSpec((GDN_STEP, LANES), tok),
                  pl.BlockSpec((C_CONV, qkv_w), const),
                  pl.BlockSpec((1, LANES), const),
                  pl.BlockSpec((1, LANES), const)],
        out_specs=[tok_spec, tok_spec, tok_spec, tok_spec, tok_spec, dec_spec],
        scratch_shapes=[pltpu.VMEM((SUBLANES + GDN_STEP, qkv_w), F32)],
        compiler_params=_cparams(("arbitrary", "arbitrary")),
        name="gdn_prep",
    )(qkv_pre, qkv_pre, br, conv_w.astype(F32), pad_row(a_log), pad_row(dt_bias))
    return pl.pallas_call(
        _gdn_rec_kernel,
        out_shape=jax.ShapeDtypeStruct((t, hw), F32),
        grid=(bsz, steps_per_seq),
        in_specs=[tok_spec, tok_spec, tok_spec, tok_spec, tok_spec, dec_spec,
                  pl.BlockSpec((1, C_DV), const)],
        out_specs=tok_spec,
        scratch_shapes=[pltpu.VMEM((C_HEADS, C_DK, C_DV), F32)],
        compiler_params=_cparams(("arbitrary", "arbitrary")),
        name="gdn_recurrence",
    )(w, u, qg, kdt, aqk, dec, norm_g.reshape(1, C_DV).astype(F32))


def _ab_layer(x2, mod3, bsz, seq, pre_g, post_g, rel_bias, w_in, w_out, s5_params):
    b_width = s5_params[7].shape[-1]
    assert DILATED_CONFIGS[0][1] == 1 and all(SPAN_RES % dl == 0 for _, dl in DILATED_CONFIGS) and seq % SPAN == 0
    head_expand = jnp.asarray(np.arange(LANES)[:, None] == (np.arange(A_WIDTH)[None, :] // A_HEAD_DIM), BF16)
    unperm = jnp.asarray(_span_perm().T, BF16)
    splits = ((0, A_WIDTH, A_HEAD_DIM ** -0.5), (A_WIDTH, A_WIDTH, 1.0), (2 * A_WIDTH, A_WIDTH, 1.0),
              (3 * A_WIDTH, b_width, 1.0), (3 * A_WIDTH + b_width, A_WIDTH + b_width, 1.0))
    q, k, v, u, gate, qs, ks, vs = _in_proj(x2, mod3, pre_g, w_in.astype(BF16), splits,
                                            (BF16, BF16, BF16, F32, BF16), seq, n_span=3)
    outs, lses = [], []
    bias_tabs = _attn_bias_tables(rel_bias)
    for cfg, (_, dil) in enumerate(DILATED_CONFIGS):
        qkv = (q, k, v) if dil == 1 else (qs, ks, vs)
        o_c, lse_c = _dilated_attention_one(*qkv, bias_tabs[cfg], bsz, seq, dil)
        outs.append(o_c)
        lses.append(lse_c)
    o_b = _s5_layer(u, bsz, seq, *s5_params)
    row_inputs = [outs[0], lses[0], outs[1], outs[2], lses[1], lses[2], o_b, gate]
    return _out_proj(_out_ab_kernel, row_inputs, [head_expand, unperm], w_out.astype(BF16),
                     x2, mod3, post_g, seq, "out_proj_ab")


def _gdn_layer(x2, mod3, bsz, seq, pre_g, post_g, w_in, conv_w, a_log, dt_bias, norm_g, w_out):
    d = x2.shape[1]
    qkv_w = 2 * C_HEADS * C_DK + C_HEADS * C_DV
    gate_w = C_HEADS * C_DV
    w_pad = jnp.concatenate(
        [w_in, jnp.zeros((d, LANES - (w_in.shape[1] - qkv_w - gate_w)), w_in.dtype)], axis=1).astype(BF16)
    splits = ((0, qkv_w, 1.0), (qkv_w, gate_w, 1.0), (qkv_w + gate_w, LANES, 1.0))
    qkv_pre, gate, br = _in_proj(x2, mod3, pre_g, w_pad, splits, (F32, BF16, F32), seq)
    o = _gdn_core(qkv_pre, br, bsz, seq, conv_w, a_log, dt_bias, norm_g)
    return _out_proj(_out_c_kernel, [o, gate], [], w_out.astype(BF16), x2, mod3, post_g, seq, "out_proj_c")


def kernel(x, c, ada_w, ada_b, pre_g, post_g, rel_bias, ab_w_in, ab_w_out, s5_a_re, s5_a_im, s5_log_dt, s5_b_re, s5_b_im, s5_c_re, s5_c_im, s5_d, s5_glu_w, s5_glu_b, gdn_w_in, gdn_conv, gdn_a_log, gdn_dt_bias, gdn_norm_g, gdn_w_out):
    bsz, seq, d = x.shape
    depth = ada_w.shape[0]
    assert seq % ROW_TILE == 0 and seq % GDN_STEP == 0 and seq % (SUBLANES * S5_J) == 0
    x2 = x.reshape(bsz * seq, d)
    mod = _adaln_mod(c, ada_w, ada_b)
    for layer in range(depth):
        j = layer // 2
        mod3 = mod[layer].reshape(bsz, 1, 3 * d)
        if layer % 2 == 0:
            s5_params = (s5_a_re[j], s5_a_im[j], s5_log_dt[j], s5_b_re[j], s5_b_im[j], s5_c_re[j], s5_c_im[j],
                         s5_d[j], s5_glu_w[j], s5_glu_b[j])
            x2 = _ab_layer(x2, mod3, bsz, seq, pre_g[layer], post_g[layer], rel_bias, ab_w_in[j], ab_w_out[j],
                           s5_params)
        else:
            x2 = _gdn_layer(x2, mod3, bsz, seq, pre_g[layer], post_g[layer], gdn_w_in[j], gdn_conv[j],
                            gdn_a_log[j], gdn_dt_bias[j], gdn_norm_g[j], gdn_w_out[j])
    return x2.reshape(bsz, seq, d)
```

```python
import functools
import math

import numpy as np
import jax
import jax.numpy as jnp
from jax import lax
from jax.experimental import pallas as pl
from jax.experimental.pallas import tpu as pltpu

F32 = jnp.float32
BF16 = jnp.bfloat16
HI = lax.Precision.HIGHEST

EPS = 1e-6
A_HEADS = 8
A_HEAD_DIM = 64
A_WIDTH = A_HEADS * A_HEAD_DIM
A_BLOCK = 128
DILATED_CONFIGS = ((128, 1), (512, 4), (2048, 16))
B_GROUP = 16
B_STATE = 64
REL_BUCKETS = 32
REL_MAX_DIST = 2048
C_HEADS = 8
C_DK = 128
C_DV = 128
C_CONV = 4
MASK_NEG = -1e30

LANES = 128
SUBLANES = 8
VMEM_LIMIT = 48 * 1024 * 1024
ROW_TILE = 512
COL_CHUNK = 512
SPAN_RES = max(dl for _, dl in DILATED_CONFIGS)
SPAN = A_BLOCK * SPAN_RES
SPAN_RUN = ROW_TILE // SPAN_RES
ATTN_BLOCKS_PER_STEP = 4
S5_J = 32
S5_LW = 512
GDN_STEP = 128
GDN_CHUNK = 64
GDN_PREP_BLOCKS = 2
GDN_REC_BLOCKS = 4

_NT = (((1,), (1,)), ((), ()))


def _cparams(sem):
    return pltpu.CompilerParams(dimension_semantics=sem, vmem_limit_bytes=VMEM_LIMIT)


def _sigmoid(x):
    return 1.0 / (1.0 + jnp.exp(-x))


def _mod_kernel(c_ref, w_ref, b_ref, o_ref):
    c = c_ref[...]
    ca = c * _sigmoid(c)
    o_ref[0] = jnp.dot(ca, w_ref[0], preferred_element_type=F32, precision=HI) + b_ref[0]


def _adaln_mod(c, ada_w, ada_b):
    depth, d, d3 = ada_w.shape
    bsz = c.shape[0]
    return pl.pallas_call(
        _mod_kernel,
        out_shape=jax.ShapeDtypeStruct((depth, bsz, d3), F32),
        grid=(depth, d3 // d),
        in_specs=[pl.BlockSpec((bsz, d), lambda l, j: (0, 0)),
                  pl.BlockSpec((1, d, d), lambda l, j: (l, 0, j)),
                  pl.BlockSpec((1, 1, d), lambda l, j: (l, 0, j))],
        out_specs=pl.BlockSpec((1, bsz, d), lambda l, j: (l, 0, j)),
        compiler_params=_cparams(("arbitrary", "arbitrary")),
        name="adaln_mod",
    )(c, ada_w, ada_b.reshape(depth, 1, d3))


def _span_perm():
    rho = np.arange(ROW_TILE)
    nat = SPAN_RES * (rho % SPAN_RUN) + rho // SPAN_RUN
    return (nat[:, None] == np.arange(ROW_TILE)[None, :]).astype(np.float32)


def _modulated_norm(x_ref, mod_ref, g_ref, d_model):
    x = x_ref[...]
    ms = jnp.mean(x * x, axis=-1, keepdims=True)
    y = x * lax.rsqrt(ms + EPS) * g_ref[...]
    shift = mod_ref[0, :, 0:d_model]
    scale = mod_ref[0, :, d_model:2 * d_model]
    return (y * (1.0 + scale) + shift).astype(BF16)


def _in_proj_kernel(x_ref, mod_ref, g_ref, w_ref, *rest, splits, d_model, n_span):
    pm_ref = rest[0] if n_span else None
    out_refs = rest[1:] if n_span else rest
    span_refs = out_refs[len(splits):]
    h = _modulated_norm(x_ref, mod_ref, g_ref, d_model)
    for idx, ((c0, width, mult), o_ref) in enumerate(zip(splits, out_refs)):
        for cc in range(0, width, COL_CHUNK):
            cw = min(COL_CHUNK, width - cc)
            acc = jnp.dot(h, w_ref[:, c0 + cc:c0 + cc + cw], preferred_element_type=F32)
            if mult != 1.0:
                acc = acc * mult
            val = acc.astype(o_ref.dtype)
            o_ref[:, cc:cc + cw] = val
            if idx < n_span:
                moved = jnp.dot(pm_ref[...], val, preferred_element_type=F32).astype(BF16)
                span_refs[idx][:, :, cc:cc + cw] = moved.reshape(SPAN_RES, SPAN_RUN, cw)


def _in_proj(x2, mod3, gain, w_bf16, splits, out_dtypes, seq, n_span=0):
    t, d = x2.shape
    tiles_per_seq = seq // ROW_TILE
    tiles_per_span = SPAN // ROW_TILE
    n_w = w_bf16.shape[1]
    row = lambda wd: pl.BlockSpec((ROW_TILE, wd), lambda i: (i, 0))
    span_shape = lambda wd: jax.ShapeDtypeStruct((t // SPAN, SPAN_RES, A_BLOCK, wd), BF16)
    span_spec = lambda wd: pl.BlockSpec((None, SPAN_RES, SPAN_RUN, wd),
                                        lambda i: (i // tiles_per_span, 0, i % tiles_per_span, 0))
    perm_in = [jnp.asarray(_span_perm(), BF16)] if n_span else []
    perm_spec = [pl.BlockSpec((ROW_TILE, ROW_TILE), lambda i: (0, 0))] if n_span else []
    return pl.pallas_call(
        functools.partial(_in_proj_kernel, splits=splits, d_model=d, n_span=n_span),
        out_shape=([jax.ShapeDtypeStruct((t, wd), dt) for (_, wd, _), dt in zip(splits, out_dtypes)]
                   + [span_shape(wd) for (_, wd, _) in splits[:n_span]]),
        grid=(t // ROW_TILE,),
        in_specs=[row(d),
                  pl.BlockSpec((1, 1, 3 * d), lambda i: (i // tiles_per_seq, 0, 0)),
                  pl.BlockSpec((1, d), lambda i: (0, 0)),
                  pl.BlockSpec((d, n_w), lambda i: (0, 0))] + perm_spec,
        out_specs=[row(wd) for (_, wd, _) in splits] + [span_spec(wd) for (_, wd, _) in splits[:n_span]],
        compiler_params=_cparams(("arbitrary",)),
        name="in_proj",
    )(x2, mod3, gain.reshape(1, d), w_bf16, *perm_in)


def _t5_bucket(dist):
    dist = np.maximum(dist, 0)
    max_exact = REL_BUCKETS // 2
    large = max_exact + (np.log(np.maximum(dist, 1) / max_exact)
                         / math.log(REL_MAX_DIST / max_exact) * (REL_BUCKETS - max_exact)).astype(np.int32)
    large = np.minimum(large, REL_BUCKETS - 1)
    return np.where(dist < max_exact, dist, large).astype(np.int32)


def _bias_kernel(rb_ref, bucket_ref, mask_ref, o_ref):
    def body(r, carry):
        rows = pl.ds(pl.multiple_of(r * SUBLANES, SUBLANES), SUBLANES)
        bk = bucket_ref[0, rows, :]
        accs = [jnp.zeros(bk.shape, F32) for _ in range(A_HEADS)]
        for b in range(REL_BUCKETS):
            eq = bk == b
            accs = [jnp.where(eq, rb_ref[b, h], acc) for h, acc in enumerate(accs)]
        for f in range(2):
            keep = mask_ref[0, f, rows, :] != 0
            for h in range(A_HEADS):
                o_ref[0, f, h, rows, :] = jnp.where(keep, accs[h], MASK_NEG)
        return carry
    lax.fori_loop(0, A_BLOCK // SUBLANES, body, 0)


def _attn_bias_tables(rel_bias):
    qi = np.arange(A_BLOCK)[:, None]
    kj = np.arange(2 * A_BLOCK)[None, :]
    rel = qi + A_BLOCK - kj
    buckets, masks = [], []
    for window, dil in DILATED_CONFIGS:
        band = (rel >= 0) & (rel <= window // dil)
        bucket = _t5_bucket(rel * dil)
        mask = np.stack([band & (kj >= A_BLOCK), band]).astype(np.int32)
        if dil > 1:
            runs = SPAN_RES // dil
            run = A_BLOCK // runs
            rho = np.arange(A_BLOCK)
            sub = runs * (rho % run) + rho // run
            keys = np.concatenate([sub, A_BLOCK + sub])
            bucket = bucket[sub][:, keys]
            mask = mask[:, sub][:, :, keys]
        buckets.append(bucket)
        masks.append(mask)
    n_cfg = len(DILATED_CONFIGS)
    return pl.pallas_call(
        _bias_kernel,
        out_shape=jax.ShapeDtypeStruct((n_cfg, 2, A_HEADS, A_BLOCK, 2 * A_BLOCK), F32),
        grid=(n_cfg,),
        in_specs=[pl.BlockSpec(memory_space=pltpu.SMEM),
                  pl.BlockSpec((1, A_BLOCK, 2 * A_BLOCK), lambda c: (c, 0, 0)),
                  pl.BlockSpec((1, 2, A_BLOCK, 2 * A_BLOCK), lambda c: (c, 0, 0, 0))],
        out_specs=pl.BlockSpec((1, 2, A_HEADS, A_BLOCK, 2 * A_BLOCK), lambda c: (c, 0, 0, 0, 0)),
        compiler_params=_cparams(("arbitrary",)),
        name="attn_bias",
    )(rel_bias.astype(F32), jnp.asarray(np.stack(buckets)), jnp.asarray(np.stack(masks)))


def _attn_block(q, kcat, vcat, bias_ref):
    lane = lax.broadcasted_iota(jnp.int32, (A_BLOCK, LANES), 1)
    low = lane < A_HEAD_DIM
    lse_tile = jnp.zeros((A_BLOCK, LANES), F32)
    zero = jnp.zeros((A_BLOCK, LANES), BF16)
    heads = range(A_HEADS)
    pair = [slice((h // 2) * LANES, (h // 2 + 1) * LANES) for h in heads]
    scores = []
    for h in heads:
        qm = jnp.where(low if h % 2 == 0 else jnp.logical_not(low), q[:, pair[h]], zero)
        scores.append(lax.dot_general(qm, kcat[:, pair[h]], _NT, preferred_element_type=F32) + bias_ref[0, h])
    probs, inv_l = [], []
    for h in heads:
        m = jnp.max(scores[h], axis=-1, keepdims=True)
        p = jnp.exp(scores[h] - m)
        l = jnp.sum(p, axis=-1, keepdims=True)
        probs.append(p.astype(BF16))
        inv_l.append(1.0 / l)
        lse_tile = jnp.where(lane == h, m + jnp.log(l), lse_tile)
    outs = [jnp.dot(probs[h], vcat[:, pair[h]], preferred_element_type=F32) * inv_l[h] for h in heads]
    o = jnp.concatenate([jnp.where(low, outs[2 * hp], outs[2 * hp + 1]) for hp in range(A_HEADS // 2)], axis=1)
    return o, lse_tile


def _attn_kernel(q_ref, kp_ref, kc_ref, vp_ref, vc_ref, bias0_ref, bias_ref, o_ref, lse_ref, *, per_lead):
    def sub(ref, i):
        if ref.ndim == 2:
            return ref[i * A_BLOCK:(i + 1) * A_BLOCK, :]
        run = ref.shape[2] // per_lead
        return ref[i // per_lead, :, (i % per_lead) * run:(i % per_lead + 1) * run, :]

    def put(ref, i, val):
        if ref.ndim == 2:
            ref[i * A_BLOCK:(i + 1) * A_BLOCK, :] = val.astype(ref.dtype)
        else:
            run = ref.shape[2] // per_lead
            ref[i // per_lead, :, (i % per_lead) * run:(i % per_lead + 1) * run, :] = (
                val.astype(ref.dtype).reshape(ref.shape[1], run, ref.shape[3]))

    flat = lambda v: v.reshape(A_BLOCK, v.shape[-1])
    n_blocks = q_ref.shape[0] // A_BLOCK if q_ref.ndim == 2 else q_ref.shape[0] * per_lead
    k_prev, v_prev = flat(kp_ref[...]), flat(vp_ref[...])
    for i in range(n_blocks):
        k_cur, v_cur = flat(sub(kc_ref, i)), flat(sub(vc_ref, i))
        o, lse_tile = _attn_block(flat(sub(q_ref, i)), jnp.concatenate([k_prev, k_cur], axis=0),
                                  jnp.concatenate([v_prev, v_cur], axis=0), bias0_ref if i == 0 else bias_ref)
        put(o_ref, i, o)
        put(lse_ref, i, lse_tile)
        k_prev, v_prev = k_cur, v_cur


def _dilated_attention_one(q, k, v, bias_tab, bsz, seq, dil):
    nb = seq // dil // A_BLOCK
    step = min(ATTN_BLOCKS_PER_STEP, nb)
    w = A_WIDTH
    if dil == 1:
        per_lead = 1
        view = lambda a: a.reshape(bsz, seq, a.shape[-1])
        block = lambda wd: (None, step * A_BLOCK, wd)
        block_prev = lambda wd: (None, A_BLOCK, wd)
        cur = lambda b, r, m: (b, m, 0)
        prev = lambda b, r, m: (b, jnp.maximum(m * step - 1, 0), 0)
        out_shape = lambda wd, dt: jax.ShapeDtypeStruct((bsz, seq, wd), dt)
    else:
        runs = SPAN_RES // dil
        run = A_BLOCK // runs
        spans_per_seq = seq // SPAN
        per_lead = min(step, runs)
        lead = step // per_lead
        assert runs % per_lead == 0 and spans_per_seq % lead == 0
        view = lambda a: a.reshape(a.shape[0], runs, dil, A_BLOCK, a.shape[-1])
        block = lambda wd: (lead, runs, None, per_lead * run, wd)
        block_prev = lambda wd: (1, runs, None, run, wd)
        cur = lambda b, r, m: ((b * spans_per_seq + (m * step) // runs) // lead, 0, r, ((m * step) % runs) // per_lead, 0)

        def prev(b, r, m):
            n = jnp.maximum(m * step - 1, 0)
            return (b * spans_per_seq + n // runs, 0, r, n % runs, 0)

        out_shape = lambda wd, dt: jax.ShapeDtypeStruct((bsz * spans_per_seq, runs, dil, A_BLOCK, wd), dt)
    blk = pl.BlockSpec(block(w), cur)
    blk_prev = pl.BlockSpec(block_prev(w), prev)
    bias_block = (1, A_HEADS, A_BLOCK, 2 * A_BLOCK)
    o, lse = pl.pallas_call(
        functools.partial(_attn_kernel, per_lead=per_lead),
        out_shape=[out_shape(w, BF16), out_shape(LANES, F32)],
        grid=(bsz, dil, nb // step),
        in_specs=[blk, blk_prev, blk, blk_prev, blk,
                  pl.BlockSpec(bias_block, lambda b, r, m: (jnp.minimum(m, 1), 0, 0, 0)),
                  pl.BlockSpec(bias_block, lambda b, r, m: (1, 0, 0, 0))],
        out_specs=[pl.BlockSpec(block(w), cur), pl.BlockSpec(block(LANES), cur)],
        compiler_params=_cparams(("arbitrary", "arbitrary", "arbitrary")),
        name=f"dilated_attn_d{dil}",
    )(view(q), view(k), view(k), view(v), view(v), bias_tab, bias_tab)
    if dil == 1:
        return o.reshape(bsz * seq, w), lse.reshape(bsz * seq, LANES)
    return o.reshape(o.shape[0], SPAN_RES, A_BLOCK, w), lse.reshape(o.shape[0], SPAN_RES, A_BLOCK, LANES)


def _gelu_tanh(x):
    c = math.sqrt(2.0 / math.pi)
    return x * (0.5 * (1.0 + jnp.tanh(c * (x + 0.044715 * (x * x * x)))))


def _s5_kernel(u_ref, pm_ref, pmt_ref, bd_ref, pw_ref, cd_ref, dsk_ref, gw_ref, gb_ref, o_ref, x_scr, st_scr, *,
               n_state):
    n = n_state
    jn = S5_J

    @pl.when(pl.program_id(1) == 0)
    def _():
        st_scr[...] = jnp.zeros_like(st_scr)

    u = u_ref[...]
    u_perm = jnp.dot(pm_ref[...], u.astype(BF16), preferred_element_type=F32).astype(BF16)
    cw = u.shape[1] * S5_LW // n
    chunks = [(slice(c0, c0 + S5_LW), slice(n + c0, n + c0 + S5_LW)) for c0 in range(0, n, S5_LW)]

    def project_in(q):
        bu = jnp.dot(u_perm[:, q * cw:(q + 1) * cw], bd_ref[q], preferred_element_type=F32)
        x_scr[:, chunks[q][0]] = bu[:, 0:S5_LW]
        x_scr[:, chunks[q][1]] = bu[:, S5_LW:]

    project_in(0)
    end_r, end_i = [], []
    for q, (re, im) in enumerate(chunks):
        if q + 1 < len(chunks):
            project_in(q + 1)
        a1r, a1i = pw_ref[0:SUBLANES, re], pw_ref[0:SUBLANES, im]
        xr = jnp.zeros((SUBLANES, S5_LW), F32)
        xi = jnp.zeros((SUBLANES, S5_LW), F32)
        for j in range(jn):
            rows = slice(SUBLANES * j, SUBLANES * (j + 1))
            nr = a1r * xr - a1i * xi + x_scr[rows, re]
            ni = a1r * xi + a1i * xr + x_scr[rows, im]
            xr, xi = nr, ni
            x_scr[rows, re] = xr
            x_scr[rows, im] = xi
        end_r.append(xr)
        end_i.append(xi)
    er, ei = jnp.concatenate(end_r, axis=1), jnp.concatenate(end_i, axis=1)
    last = SUBLANES * (jn - 1)
    ajr, aji = pw_ref[last:last + 1, 0:n], pw_ref[last:last + 1, n:]
    pr, pi = st_scr[0:1, 0:n], st_scr[0:1, n:]
    cin_r, cin_i = [], []
    for s in range(SUBLANES):
        cin_r.append(pr)
        cin_i.append(pi)
        nr = er[s:s + 1] + ajr * pr - aji * pi
        ni = ei[s:s + 1] + ajr * pi + aji * pr
        pr, pi = nr, ni
    st_scr[0:1, 0:n] = pr
    st_scr[0:1, n:] = pi
    cr_all = jnp.concatenate(cin_r, axis=0)
    ci_all = jnp.concatenate(cin_i, axis=0)
    ys = []
    for q, (re, im) in enumerate(chunks):
        cr, ci = cr_all[:, re], ci_all[:, re]
        for j in range(jn):
            rows = slice(SUBLANES * j, SUBLANES * (j + 1))
            pjr, pji = pw_ref[rows, re], pw_ref[rows, im]
            x_scr[rows, re] = x_scr[rows, re] + (pjr * cr - pji * ci)
            x_scr[rows, im] = x_scr[rows, im] + (pjr * ci + pji * cr)
        xcat = jnp.concatenate([x_scr[:, re], x_scr[:, im]], axis=1).astype(BF16)
        ys.append(jnp.dot(xcat, cd_ref[q], preferred_element_type=F32))

    y_perm = jnp.concatenate(ys, axis=1)
    y_hi = y_perm.astype(BF16)
    y_lo = (y_perm - y_hi.astype(F32)).astype(BF16)
    y = (jnp.dot(pmt_ref[...], y_hi, preferred_element_type=F32)
         + jnp.dot(pmt_ref[...], y_lo, preferred_element_type=F32))
    y = _gelu_tanh(y + dsk_ref[...] * u)
    z = jnp.dot(y.astype(BF16), gw_ref[...], preferred_element_type=F32) + gb_ref[...]
    o_ref[...] = y * _sigmoid(z)


def _s5_tables(a_re, a_im, log_dt, b_re, b_im, c_re, c_im):
    g, p = a_re.shape
    dt = jnp.exp(log_dt.astype(F32))[:, None]
    ar, ai = a_re.astype(F32), a_im.astype(F32)
    mag = jnp.exp(dt * ar)
    abar_r, abar_i = mag * jnp.cos(dt * ai), mag * jnp.sin(dt * ai)
    den = ar * ar + ai * ai
    fr = ((abar_r - 1.0) * ar + abar_i * ai) / den
    fi = (abar_i * ar - (abar_r - 1.0) * ai) / den
    br, bi = b_re.astype(F32), b_im.astype(F32)
    bbar_r = fr[..., None] * br - fi[..., None] * bi
    bbar_i = fr[..., None] * bi + fi[..., None] * br
    m = br.shape[-1]
    gc = S5_LW // p
    nq = g // gc
    eye = jnp.eye(gc, dtype=F32)
    dense_b = lambda t: jnp.einsum('qgpm,gh->qgmhp', t.reshape(nq, gc, p, m), eye).reshape(nq, gc * m, gc * p)
    bd = jnp.concatenate([dense_b(bbar_r), dense_b(bbar_i)], axis=2)
    dense_c = lambda t: jnp.einsum('qgmp,gh->qgphm', t.reshape(nq, gc, m, p), eye).reshape(nq, gc * p, gc * m)
    cd = jnp.concatenate([dense_c(c_re.astype(F32)), -dense_c(c_im.astype(F32))], axis=1)
    kk = jnp.arange(1, S5_J + 1, dtype=F32)[:, None, None]
    pmag = jnp.exp(kk * (dt * ar)[None])
    pw_r = (pmag * jnp.cos(kk * (dt * ai)[None])).reshape(S5_J, g * p)
    pw_i = (pmag * jnp.sin(kk * (dt * ai)[None])).reshape(S5_J, g * p)
    pw = jnp.repeat(jnp.concatenate([pw_r, pw_i], axis=1), SUBLANES, axis=0)
    return bd.astype(BF16), pw, cd.astype(BF16)


def _s5_layer(u, bsz, seq, a_re, a_im, log_dt, b_re, b_im, c_re, c_im, d_skip, glu_w, glu_b):
    t, width = u.shape
    n = a_re.shape[0] * a_re.shape[1]
    bd, pw, cd = _s5_tables(a_re, a_im, log_dt, b_re, b_im, c_re, c_im)
    tile = SUBLANES * S5_J
    tiles_per_seq = seq // tile
    const = lambda b, i: (0, 0)
    src = (np.arange(tile) % SUBLANES) * S5_J + np.arange(tile) // SUBLANES
    perm = (src[:, None] == np.arange(tile)[None, :]).astype(np.float32)
    return pl.pallas_call(
        functools.partial(_s5_kernel, n_state=n),
        out_shape=jax.ShapeDtypeStruct((t, width), F32),
        grid=(bsz, tiles_per_seq),
        in_specs=[pl.BlockSpec((tile, width), lambda b, i: (b * tiles_per_seq + i, 0)),
                  pl.BlockSpec((tile, tile), const),
                  pl.BlockSpec((tile, tile), const),
                  pl.BlockSpec(bd.shape, lambda b, i: (0, 0, 0)),
                  pl.BlockSpec((tile, 2 * n), const),
                  pl.BlockSpec(cd.shape, lambda b, i: (0, 0, 0)),
                  pl.BlockSpec((1, width), const),
                  pl.BlockSpec((width, width), const),
                  pl.BlockSpec((1, width), const)],
        out_specs=pl.BlockSpec((tile, width), lambda b, i: (b * tiles_per_seq + i, 0)),
        scratch_shapes=[pltpu.VMEM((tile, 2 * n), F32), pltpu.VMEM((SUBLANES, 2 * n), F32)],
        compiler_params=_cparams(("arbitrary", "arbitrary")),
        name="s5_layer",
    )(u, jnp.asarray(perm, BF16), jnp.asarray(perm.T, BF16), bd, pw, cd,
      d_skip.reshape(1, width).astype(F32), glu_w.astype(BF16),
      glu_b.reshape(1, width).astype(F32))


def _finish(y, x_ref, mod_ref, pg_ref, out_ref, d_model):
    ms = jnp.mean(y * y, axis=-1, keepdims=True)
    yn = y * lax.rsqrt(ms + EPS) * pg_ref[...]
    gate_mod = mod_ref[0, :, 2 * d_model:3 * d_model]
    out_ref[...] = x_ref[...] + gate_mod * yn


def _split3(x):
    hi = x.astype(BF16)
    r1 = x - hi.astype(F32)
    mid = r1.astype(BF16)
    return hi, mid, (r1 - mid.astype(F32)).astype(BF16)


def _out_ab_kernel(o0_ref, l0_ref, o1_ref, o2_ref, l1_ref, l2_ref, ob_ref, gate_ref, e_ref, un_ref, w_ref,
                   x_ref, mod_ref, pg_ref, out_ref, *, d_model):
    un = un_ref[...]
    flat = lambda ref: ref[...].reshape(ROW_TILE, ref.shape[-1])
    unperm_bf16 = lambda ref: jnp.dot(un, flat(ref), preferred_element_type=F32)
    unperm_f32 = lambda ref: sum(jnp.dot(un, part, preferred_element_type=F32) for part in _split3(flat(ref)))
    o0, o1, o2 = o0_ref[...].astype(F32), unperm_bf16(o1_ref), unperm_bf16(o2_ref)
    l0, l1, l2 = l0_ref[...], unperm_f32(l1_ref), unperm_f32(l2_ref)
    mx = jnp.maximum(jnp.maximum(l0, l1), l2)
    e0, e1, e2 = jnp.exp(l0 - mx), jnp.exp(l1 - mx), jnp.exp(l2 - mx)
    inv_den = 1.0 / (e0 + e1 + e2)
    expand = lambda wgt: sum(jnp.dot(part, e_ref[...], preferred_element_type=F32) for part in _split3(wgt))
    o_a = expand(e0 * inv_den) * o0 + expand(e1 * inv_den) * o1 + expand(e2 * inv_den) * o2
    gate = gate_ref[...].astype(F32)
    sg = gate * _sigmoid(gate)
    aw = o_a.shape[-1]
    y = jnp.dot((o_a * sg[:, :aw]).astype(BF16), w_ref[0:aw, :], preferred_element_type=F32)
    y = y + jnp.dot((ob_ref[...] * sg[:, aw:]).astype(BF16), w_ref[aw:, :], preferred_element_type=F32)
    _finish(y, x_ref, mod_ref, pg_ref, out_ref, d_model)


def _out_c_kernel(o_ref, gate_ref, w_ref, x_ref, mod_ref, pg_ref, out_ref, *, d_model):
    gate = gate_ref[...].astype(F32)
    o = o_ref[...].astype(F32) * (gate * _sigmoid(gate))
    y = jnp.dot(o.astype(BF16), w_ref[...], preferred_element_type=F32)
    _finish(y, x_ref, mod_ref, pg_ref, out_ref, d_model)


def _out_proj(kern, row_inputs, const_inputs, w_bf16, x2, mod3, post_g, seq, name):
    t, d = x2.shape
    tiles_per_seq = seq // ROW_TILE
    tiles_per_span = SPAN // ROW_TILE

    def row_spec(a):
        if a.ndim == 2:
            return pl.BlockSpec((ROW_TILE, a.shape[1]), lambda i: (i, 0))
        return pl.BlockSpec((None, SPAN_RES, SPAN_RUN, a.shape[-1]),
                            lambda i: (i // tiles_per_span, 0, i % tiles_per_span, 0))

    const_spec = lambda a: pl.BlockSpec(a.shape, lambda i: (0, 0))
    return pl.pallas_call(
        functools.partial(kern, d_model=d),
        out_shape=jax.ShapeDtypeStruct((t, d), F32),
        grid=(t // ROW_TILE,),
        in_specs=([row_spec(a) for a in row_inputs] + [const_spec(a) for a in const_inputs]
                  + [const_spec(w_bf16), row_spec(x2),
                     pl.BlockSpec((1, 1, 3 * d), lambda i: (i // tiles_per_seq, 0, 0)),
                     pl.BlockSpec((1, d), lambda i: (0, 0))]),
        out_specs=pl.BlockSpec((ROW_TILE, d), lambda i: (i, 0)),
        compiler_params=_cparams(("arbitrary",)),
        name=name,
    )(*row_inputs, *const_inputs, w_bf16, x2, mod3, post_g.reshape(1, d))


def _gdn_prep_kernel(x_ref, halo_ref, br_ref, cw_ref, alog_ref, dtb_ref,
                     w_ref, u_ref, qg_ref, kdt_ref, aqk_ref, dec_ref, xs_scr):
    n_all = GDN_PREP_BLOCKS * GDN_STEP
    first = pl.program_id(1) == 0
    halo = halo_ref[...]
    xs_scr[0:SUBLANES, :] = jnp.where(first, jnp.zeros_like(halo), halo)
    xs_scr[SUBLANES:SUBLANES + n_all, :] = x_ref[...]
    cw = cw_ref[...]
    conv = xs_scr[SUBLANES:SUBLANES + n_all, :] * cw[C_CONV - 1:C_CONV]
    for j in range(C_CONV - 1):
        off = SUBLANES - (C_CONV - 1) + j
        conv = conv + xs_scr[off:off + n_all, :] * cw[j:j + 1]
    act = conv * _sigmoid(conv)
    for blk in range(GDN_PREP_BLOCKS):
        _gdn_prep_block(blk, act, br_ref, alog_ref, dtb_ref, w_ref, u_ref, qg_ref, kdt_ref, aqk_ref, dec_ref)


def _gdn_prep_block(blk, act, br_ref, alog_ref, dtb_ref, w_ref, u_ref, qg_ref, kdt_ref, aqk_ref, dec_ref):
    n = GDN_STEP
    rows = slice(blk * n, (blk + 1) * n)
    head_cols = lambda base, h: act[rows, base + h * C_DK:base + (h + 1) * C_DK]

    row = lax.broadcasted_iota(jnp.int32, (n, n), 0)
    col = lax.broadcasted_iota(jnp.int32, (n, n), 1)
    same = (row // GDN_CHUNK) == (col // GDN_CHUNK)
    tril = jnp.logical_and(same, row >= col)
    strict = jnp.logical_and(same, row > col)
    eye = (row == col).astype(F32)

    br = br_ref[rows, :]
    beta_all = _sigmoid(br)
    xg = br + dtb_ref[...]
    softplus = jnp.maximum(xg, 0.0) + jnp.log(1.0 + jnp.exp(-jnp.abs(xg)))
    g_all = -jnp.exp(alog_ref[...]) * softplus
    gc_all = jnp.dot(tril.astype(F32), g_all, preferred_element_type=F32, precision=HI)
    gc_t = gc_all.T
    first_chunk = lax.broadcasted_iota(jnp.int32, (n, 1), 0) < GDN_CHUNK
    dec_row = lax.broadcasted_iota(jnp.int32, (SUBLANES, LANES), 0)

    heads = range(C_HEADS)
    lanes = [slice(h * LANES, (h + 1) * LANES) for h in heads]
    gcols = [gc_all[:, C_HEADS + h:C_HEADS + h + 1] for h in heads]
    egs = [jnp.exp(g) for g in gcols]
    kbs, amats, invs = [], [], []
    for h in heads:
        grow = gc_t[C_HEADS + h:C_HEADS + h + 1, :]
        decay = jnp.where(tril, jnp.exp(jnp.where(tril, gcols[h] - grow, 0.0)), 0.0)
        q = head_cols(0, h)
        k = head_cols(C_HEADS * C_DK, h)
        q = q * lax.rsqrt(jnp.sum(q * q, axis=-1, keepdims=True) + EPS) * (C_DK ** -0.5)
        k = k * lax.rsqrt(jnp.sum(k * k, axis=-1, keepdims=True) + EPS)
        kb = k * beta_all[:, h:h + 1]
        k16 = k.astype(BF16)
        a = jnp.where(strict, lax.dot_general(kb.astype(BF16), k16, _NT, preferred_element_type=F32) * decay, 0.0)
        aqk = lax.dot_general(q.astype(BF16), k16, _NT, preferred_element_type=F32) * decay
        aqk_ref[rows, lanes[h]] = aqk.astype(BF16)
        qg_ref[rows, lanes[h]] = (q * egs[h]).astype(BF16)
        g_last = jnp.where(first_chunk, gcols[h][GDN_CHUNK - 1:GDN_CHUNK], gcols[h][n - 1:n])
        kdt_ref[rows, lanes[h]] = (k * jnp.exp(g_last - gcols[h])).T.astype(BF16)
        dec_ref[blk, :, lanes[h]] = jnp.where(dec_row < SUBLANES // 2, egs[h][GDN_CHUNK - 1:GDN_CHUNK],
                                             egs[h][n - 1:n])
        kbs.append(kb)
        amats.append(a)
        invs.append(eye - a)
    pows = []
    for h in heads:
        a16 = amats[h].astype(BF16)
        pows.append(jnp.dot(a16, a16, preferred_element_type=F32))
    for _ in range(4):
        for h in heads:
            p16 = pows[h].astype(BF16)
            both = jnp.dot(jnp.concatenate([invs[h].astype(BF16), p16], axis=0), p16, preferred_element_type=F32)
            invs[h] = invs[h] + both[0:n]
            pows[h] = both[n:]
    for h in heads:
        invs[h] = invs[h] + jnp.dot(invs[h].astype(BF16), pows[h].astype(BF16), preferred_element_type=F32)
    for h in heads:
        v = head_cols(2 * C_HEADS * C_DK, h)
        rhs = jnp.concatenate([v * beta_all[:, h:h + 1], kbs[h] * egs[h]], axis=1)
        uw = jnp.dot(invs[h].astype(BF16), rhs.astype(BF16), preferred_element_type=F32)
        u_ref[rows, lanes[h]] = uw[:, 0:C_DV]
        w_ref[rows, lanes[h]] = uw[:, C_DV:].astype(BF16)


def _gdn_rec_kernel(w_ref, u_ref, qg_ref, kdt_ref, aqk_ref, dec_ref, ng_ref, o_ref, s_scr):
    @pl.when(pl.program_id(1) == 0)
    def _():
        s_scr[...] = jnp.zeros_like(s_scr)

    half = SUBLANES // 2
    zeros = jnp.zeros((GDN_CHUNK, C_DV), BF16)
    heads = range(C_HEADS)
    lanes = [slice(h * LANES, (h + 1) * LANES) for h in heads]
    states = [s_scr[h] for h in heads]
    for blk in range(GDN_REC_BLOCKS):
        blk_rows = slice(blk * GDN_STEP, (blk + 1) * GDN_STEP)
        for j in range(GDN_STEP // GDN_CHUNK):
            rows = slice(blk * GDN_STEP + j * GDN_CHUNK, blk * GDN_STEP + (j + 1) * GDN_CHUNK)
            wss = []
            for h in heads:
                wq = jnp.concatenate([w_ref[rows, lanes[h]], qg_ref[rows, lanes[h]]], axis=0)
                wss.append(jnp.dot(wq, states[h].astype(BF16), preferred_element_type=F32))
            for h in heads:
                v_new = (u_ref[rows, lanes[h]] - wss[h][0:GDN_CHUNK]).astype(BF16)
                v_pad = jnp.concatenate([v_new, zeros] if j == 0 else [zeros, v_new], axis=0)
                o = wss[h][GDN_CHUNK:] + jnp.dot(aqk_ref[rows, lanes[h]], v_pad, preferred_element_type=F32)
                dec = dec_ref[blk, j * half:j * half + 1, lanes[h]]
                states[h] = states[h] * dec + jnp.dot(kdt_ref[blk_rows, lanes[h]], v_pad,
                                                      preferred_element_type=F32)
                ms = jnp.mean(o * o, axis=-1, keepdims=True)
                o_ref[rows, lanes[h]] = o * lax.rsqrt(ms + EPS) * ng_ref[...]
    for h in heads:
        s_scr[h] = states[h]


def _gdn_core(qkv_pre, br, bsz, seq, conv_w, a_log, dt_bias, norm_g):
    t, qkv_w = qkv_pre.shape
    hw = C_HEADS * C_DV
    steps_per_seq = seq // GDN_STEP
    prep_rows = GDN_PREP_BLOCKS * GDN_STEP
    prep_steps = steps_per_seq // GDN_PREP_BLOCKS
    halo_blocks = prep_rows // SUBLANES
    pad_row = lambda vec: jnp.zeros((1, LANES), F32).at[0, C_HEADS:2 * C_HEADS].set(vec.astype(F32))
    tok = lambda b, i: (b * prep_steps + i, 0)
    const = lambda b, i: (0, 0)
    tok_spec = pl.BlockSpec((prep_rows, hw), tok)
    dec_spec = pl.BlockSpec((GDN_PREP_BLOCKS, SUBLANES, hw), lambda b, i: (b * prep_steps + i, 0, 0))
    w, u, qg, kdt, aqk, dec = pl.pallas_call(
        _gdn_prep_kernel,
        out_shape=[jax.ShapeDtypeStruct((t, hw), BF16), jax.ShapeDtypeStruct((t, hw), F32),
                   jax.ShapeDtypeStruct((t, hw), BF16), jax.ShapeDtypeStruct((t, hw), BF16),
                   jax.ShapeDtypeStruct((t, hw), BF16),
                   jax.ShapeDtypeStruct((t // GDN_STEP, SUBLANES, hw), F32)],
        grid=(bsz, prep_steps),
        in_specs=[pl.BlockSpec((prep_rows, qkv_w), tok),
                  pl.BlockSpec((SUBLANES, qkv_w),
                               lambda b, i: (jnp.maximum((b * prep_steps + i) * halo_blocks - 1, 0), 0)),
                  pl.BlockSpec((prep_rows, LANES), tok),
                  pl.BlockSpec((C_CONV, qkv_w), const),
                  pl.BlockSpec((1, LANES), const),
                  pl.BlockSpec((1, LANES), const)],
        out_specs=[tok_spec, tok_spec, tok_spec, tok_spec, tok_spec, dec_spec],
        scratch_shapes=[pltpu.VMEM((SUBLANES + prep_rows, qkv_w), F32)],
        compiler_params=_cparams(("arbitrary", "arbitrary")),
        name="gdn_prep",
    )(qkv_pre, qkv_pre, br, conv_w.astype(F32), pad_row(a_log), pad_row(dt_bias))
    rec_steps = steps_per_seq // GDN_REC_BLOCKS
    rec_tok = pl.BlockSpec((GDN_REC_BLOCKS * GDN_STEP, hw), lambda b, i: (b * rec_steps + i, 0))
    rec_dec = pl.BlockSpec((GDN_REC_BLOCKS, SUBLANES, hw), lambda b, i: (b * rec_steps + i, 0, 0))
    return pl.pallas_call(
        _gdn_rec_kernel,
        out_shape=jax.ShapeDtypeStruct((t, hw), F32),
        grid=(bsz, rec_steps),
        in_specs=[rec_tok, rec_tok, rec_tok, rec_tok, rec_tok, rec_dec,
                  pl.BlockSpec((1, C_DV), const)],
        out_specs=rec_tok,
        scratch_shapes=[pltpu.VMEM((C_HEADS, C_DK, C_DV), F32)],
        compiler_params=_cparams(("arbitrary", "arbitrary")),
        name="gdn_recurrence",
    )(w, u, qg, kdt, aqk, dec, norm_g.reshape(1, C_DV).astype(F32))


def _ab_layer(x2, mod3, bsz, seq, pre_g, post_g, rel_bias, w_in, w_out, s5_params):
    b_width = s5_params[7].shape[-1]
    assert DILATED_CONFIGS[0][1] == 1 and all(SPAN_RES % dl == 0 for _, dl in DILATED_CONFIGS) and seq % SPAN == 0
    head_expand = jnp.asarray(np.arange(LANES)[:, None] == (np.arange(A_WIDTH)[None, :] // A_HEAD_DIM), BF16)
    unperm = jnp.asarray(_span_perm().T, BF16)
    splits = ((0, A_WIDTH, A_HEAD_DIM ** -0.5), (A_WIDTH, A_WIDTH, 1.0), (2 * A_WIDTH, A_WIDTH, 1.0),
              (3 * A_WIDTH, b_width, 1.0), (3 * A_WIDTH + b_width, A_WIDTH + b_width, 1.0))
    q, k, v, u, gate, qs, ks, vs = _in_proj(x2, mod3, pre_g, w_in.astype(BF16), splits,
                                            (BF16, BF16, BF16, F32, BF16), seq, n_span=3)
    outs, lses = [], []
    bias_tabs = _attn_bias_tables(rel_bias)
    for cfg, (_, dil) in enumerate(DILATED_CONFIGS):
        qkv = (q, k, v) if dil == 1 else (qs, ks, vs)
        o_c, lse_c = _dilated_attention_one(*qkv, bias_tabs[cfg], bsz, seq, dil)
        outs.append(o_c)
        lses.append(lse_c)
    o_b = _s5_layer(u, bsz, seq, *s5_params)
    row_inputs = [outs[0], lses[0], outs[1], outs[2], lses[1], lses[2], o_b, gate]
    return _out_proj(_out_ab_kernel, row_inputs, [head_expand, unperm], w_out.astype(BF16),
                     x2, mod3, post_g, seq, "out_proj_ab")


def _gdn_layer(x2, mod3, bsz, seq, pre_g, post_g, w_in, conv_w, a_log, dt_bias, norm_g, w_out):
    d = x2.shape[1]
    qkv_w = 2 * C_HEADS * C_DK + C_HEADS * C_DV
    gate_w = C_HEADS * C_DV
    w_pad = jnp.concatenate(
        [w_in, jnp.zeros((d, LANES - (w_in.shape[1] - qkv_w - gate_w)), w_in.dtype)], axis=1).astype(BF16)
    splits = ((0, qkv_w, 1.0), (qkv_w, gate_w, 1.0), (qkv_w + gate_w, LANES, 1.0))
    qkv_pre, gate, br = _in_proj(x2, mod3, pre_g, w_pad, splits, (F32, BF16, F32), seq)
    o = _gdn_core(qkv_pre, br, bsz, seq, conv_w, a_log, dt_bias, norm_g)
    return _out_proj(_out_c_kernel, [o, gate], [], w_out.astype(BF16), x2, mod3, post_g, seq, "out_proj_c")


def kernel(x, c, ada_w, ada_b, pre_g, post_g, rel_bias, ab_w_in, ab_w_out, s5_a_re, s5_a_im, s5_log_dt, s5_b_re, s5_b_im, s5_c_re, s5_c_im, s5_d, s5_glu_w, s5_glu_b, gdn_w_in, gdn_conv, gdn_a_log, gdn_dt_bias, gdn_norm_g, gdn_w_out):
    bsz, seq, d = x.shape
    depth = ada_w.shape[0]
    assert seq % ROW_TILE == 0 and seq % (SUBLANES * S5_J) == 0
    assert seq % (GDN_STEP * GDN_PREP_BLOCKS) == 0 and seq % (GDN_STEP * GDN_REC_BLOCKS) == 0
    x2 = x.reshape(bsz * seq, d)
    mod = _adaln_mod(c, ada_w, ada_b)
    for layer in range(depth):
        j = layer // 2
        mod3 = mod[layer].reshape(bsz, 1, 3 * d)
        if layer % 2 == 0:
            s5_params = (s5_a_re[j], s5_a_im[j], s5_log_dt[j], s5_b_re[j], s5_b_im[j], s5_c_re[j], s5_c_im[j],
                         s5_d[j], s5_glu_w[j], s5_glu_b[j])
            x2 = _ab_layer(x2, mod3, bsz, seq, pre_g[layer], post_g[layer], rel_bias, ab_w_in[j], ab_w_out[j],
                           s5_params)
        else:
            x2 = _gdn_layer(x2, mod3, bsz, seq, pre_g[layer], post_g[layer], gdn_w_in[j], gdn_conv[j],
                            gdn_a_log[j], gdn_dt_bias[j], gdn_norm_g[j], gdn_w_out[j])
    return x2.reshape(bsz, seq, d)
```

```python
import functools
import math

import numpy as np
import jax
import jax.numpy as jnp
from jax import lax
from jax.experimental import pallas as pl
from jax.experimental.pallas import tpu as pltpu

F32 = jnp.float32
BF16 = jnp.bfloat16
HI = lax.Precision.HIGHEST

EPS = 1e-6
A_HEADS = 8
A_HEAD_DIM = 64
A_WIDTH = A_HEADS * A_HEAD_DIM
A_BLOCK = 128
DILATED_CONFIGS = ((128, 1), (512, 4), (2048, 16))
B_GROUP = 16
B_STATE = 64
REL_BUCKETS = 32
REL_MAX_DIST = 2048
C_HEADS = 8
C_DK = 128
C_DV = 128
C_CONV = 4
MASK_NEG = -1e30

LANES = 128
SUBLANES = 8
VMEM_LIMIT = 48 * 1024 * 1024
ROW_TILE = 512
COL_CHUNK = 512
SPAN_RES = max(dl for _, dl in DILATED_CONFIGS)
SPAN = A_BLOCK * SPAN_RES
SPAN_RUN = ROW_TILE // SPAN_RES
ATTN_BLOCKS_PER_STEP = 4
S5_J = 32
S5_LW = 512
S5_TILES_PER_STEP = 2
GDN_STEP = 128
GDN_CHUNK = 64
GDN_PREP_BLOCKS = 2
GDN_REC_BLOCKS = 4

_NT = (((1,), (1,)), ((), ()))


def _cparams(sem):
    return pltpu.CompilerParams(dimension_semantics=sem, vmem_limit_bytes=VMEM_LIMIT)


def _sigmoid(x):
    return 1.0 / (1.0 + jnp.exp(-x))


def _mod_kernel(c_ref, w_ref, b_ref, o_ref):
    c = c_ref[...]
    ca = c * _sigmoid(c)
    o_ref[0] = jnp.dot(ca, w_ref[0], preferred_element_type=F32, precision=HI) + b_ref[0]


def _adaln_mod(c, ada_w, ada_b):
    depth, d, d3 = ada_w.shape
    bsz = c.shape[0]
    return pl.pallas_call(
        _mod_kernel,
        out_shape=jax.ShapeDtypeStruct((depth, bsz, d3), F32),
        grid=(depth, d3 // d),
        in_specs=[pl.BlockSpec((bsz, d), lambda l, j: (0, 0)),
                  pl.BlockSpec((1, d, d), lambda l, j: (l, 0, j)),
                  pl.BlockSpec((1, 1, d), lambda l, j: (l, 0, j))],
        out_specs=pl.BlockSpec((1, bsz, d), lambda l, j: (l, 0, j)),
        compiler_params=_cparams(("arbitrary", "arbitrary")),
        name="adaln_mod",
    )(c, ada_w, ada_b.reshape(depth, 1, d3))


def _span_perm():
    rho = np.arange(ROW_TILE)
    nat = SPAN_RES * (rho % SPAN_RUN) + rho // SPAN_RUN
    return (nat[:, None] == np.arange(ROW_TILE)[None, :]).astype(np.float32)


def _modulated_norm(x_ref, mod_ref, g_ref, d_model):
    x = x_ref[...]
    ms = jnp.mean(x * x, axis=-1, keepdims=True)
    y = x * lax.rsqrt(ms + EPS) * g_ref[...]
    shift = mod_ref[0, :, 0:d_model]
    scale = mod_ref[0, :, d_model:2 * d_model]
    return (y * (1.0 + scale) + shift).astype(BF16)


def _in_proj_kernel(x_ref, mod_ref, g_ref, w_ref, *rest, splits, d_model, n_span):
    pm_ref = rest[0] if n_span else None
    out_refs = rest[1:] if n_span else rest
    span_refs = out_refs[len(splits):]
    h = _modulated_norm(x_ref, mod_ref, g_ref, d_model)
    for idx, ((c0, width, mult), o_ref) in enumerate(zip(splits, out_refs)):
        for cc in range(0, width, COL_CHUNK):
            cw = min(COL_CHUNK, width - cc)
            acc = jnp.dot(h, w_ref[:, c0 + cc:c0 + cc + cw], preferred_element_type=F32)
            if mult != 1.0:
                acc = acc * mult
            val = acc.astype(o_ref.dtype)
            o_ref[:, cc:cc + cw] = val
            if idx < n_span:
                moved = jnp.dot(pm_ref[...], val, preferred_element_type=F32).astype(BF16)
                span_refs[idx][:, :, cc:cc + cw] = moved.reshape(SPAN_RES, SPAN_RUN, cw)


def _in_proj(x2, mod3, gain, w_bf16, splits, out_dtypes, seq, n_span=0):
    t, d = x2.shape
    tiles_per_seq = seq // ROW_TILE
    tiles_per_span = SPAN // ROW_TILE
    n_w = w_bf16.shape[1]
    row = lambda wd: pl.BlockSpec((ROW_TILE, wd), lambda i: (i, 0))
    span_shape = lambda wd: jax.ShapeDtypeStruct((t // SPAN, SPAN_RES, A_BLOCK, wd), BF16)
    span_spec = lambda wd: pl.BlockSpec((None, SPAN_RES, SPAN_RUN, wd),
                                        lambda i: (i // tiles_per_span, 0, i % tiles_per_span, 0))
    perm_in = [jnp.asarray(_span_perm(), BF16)] if n_span else []
    perm_spec = [pl.BlockSpec((ROW_TILE, ROW_TILE), lambda i: (0, 0))] if n_span else []
    return pl.pallas_call(
        functools.partial(_in_proj_kernel, splits=splits, d_model=d, n_span=n_span),
        out_shape=([jax.ShapeDtypeStruct((t, wd), dt) for (_, wd, _), dt in zip(splits, out_dtypes)]
                   + [span_shape(wd) for (_, wd, _) in splits[:n_span]]),
        grid=(t // ROW_TILE,),
        in_specs=[row(d),
                  pl.BlockSpec((1, 1, 3 * d), lambda i: (i // tiles_per_seq, 0, 0)),
                  pl.BlockSpec((1, d), lambda i: (0, 0)),
                  pl.BlockSpec((d, n_w), lambda i: (0, 0))] + perm_spec,
        out_specs=[row(wd) for (_, wd, _) in splits] + [span_spec(wd) for (_, wd, _) in splits[:n_span]],
        compiler_params=_cparams(("arbitrary",)),
        name="in_proj",
    )(x2, mod3, gain.reshape(1, d), w_bf16, *perm_in)


def _t5_bucket(dist):
    dist = np.maximum(dist, 0)
    max_exact = REL_BUCKETS // 2
    large = max_exact + (np.log(np.maximum(dist, 1) / max_exact)
                         / math.log(REL_MAX_DIST / max_exact) * (REL_BUCKETS - max_exact)).astype(np.int32)
    large = np.minimum(large, REL_BUCKETS - 1)
    return np.where(dist < max_exact, dist, large).astype(np.int32)


def _bias_kernel(rb_ref, bucket_ref, mask_ref, o_ref):
    def body(r, carry):
        rows = pl.ds(pl.multiple_of(r * SUBLANES, SUBLANES), SUBLANES)
        bk = bucket_ref[0, rows, :]
        accs = [jnp.zeros(bk.shape, F32) for _ in range(A_HEADS)]
        for b in range(REL_BUCKETS):
            eq = bk == b
            accs = [jnp.where(eq, rb_ref[b, h], acc) for h, acc in enumerate(accs)]
        for f in range(2):
            keep = mask_ref[0, f, rows, :] != 0
            for h in range(A_HEADS):
                o_ref[0, f, h, rows, :] = jnp.where(keep, accs[h], MASK_NEG)
        return carry
    lax.fori_loop(0, A_BLOCK // SUBLANES, body, 0)


def _attn_bias_tables(rel_bias):
    qi = np.arange(A_BLOCK)[:, None]
    kj = np.arange(2 * A_BLOCK)[None, :]
    rel = qi + A_BLOCK - kj
    buckets, masks = [], []
    for window, dil in DILATED_CONFIGS:
        band = (rel >= 0) & (rel <= window // dil)
        bucket = _t5_bucket(rel * dil)
        mask = np.stack([band & (kj >= A_BLOCK), band]).astype(np.int32)
        if dil > 1:
            runs = SPAN_RES // dil
            run = A_BLOCK // runs
            rho = np.arange(A_BLOCK)
            sub = runs * (rho % run) + rho // run
            keys = np.concatenate([sub, A_BLOCK + sub])
            bucket = bucket[sub][:, keys]
            mask = mask[:, sub][:, :, keys]
        buckets.append(bucket)
        masks.append(mask)
    n_cfg = len(DILATED_CONFIGS)
    return pl.pallas_call(
        _bias_kernel,
        out_shape=jax.ShapeDtypeStruct((n_cfg, 2, A_HEADS, A_BLOCK, 2 * A_BLOCK), F32),
        grid=(n_cfg,),
        in_specs=[pl.BlockSpec(memory_space=pltpu.SMEM),
                  pl.BlockSpec((1, A_BLOCK, 2 * A_BLOCK), lambda c: (c, 0, 0)),
                  pl.BlockSpec((1, 2, A_BLOCK, 2 * A_BLOCK), lambda c: (c, 0, 0, 0))],
        out_specs=pl.BlockSpec((1, 2, A_HEADS, A_BLOCK, 2 * A_BLOCK), lambda c: (c, 0, 0, 0, 0)),
        compiler_params=_cparams(("arbitrary",)),
        name="attn_bias",
    )(rel_bias.astype(F32), jnp.asarray(np.stack(buckets)), jnp.asarray(np.stack(masks)))


def _attn_block(q, kcat, vcat, bias_ref):
    lane = lax.broadcasted_iota(jnp.int32, (A_BLOCK, LANES), 1)
    low = lane < A_HEAD_DIM
    lse_tile = jnp.zeros((A_BLOCK, LANES), F32)
    zero = jnp.zeros((A_BLOCK, LANES), BF16)
    heads = range(A_HEADS)
    pair = [slice((h // 2) * LANES, (h // 2 + 1) * LANES) for h in heads]
    scores = []
    for h in heads:
        qm = jnp.where(low if h % 2 == 0 else jnp.logical_not(low), q[:, pair[h]], zero)
        scores.append(lax.dot_general(qm, kcat[:, pair[h]], _NT, preferred_element_type=F32) + bias_ref[0, h])
    probs, inv_l = [], []
    for h in heads:
        m = jnp.max(scores[h], axis=-1, keepdims=True)
        p = jnp.exp(scores[h] - m)
        l = jnp.sum(p, axis=-1, keepdims=True)
        probs.append(p.astype(BF16))
        inv_l.append(1.0 / l)
        lse_tile = jnp.where(lane == h, m + jnp.log(l), lse_tile)
    outs = [jnp.dot(probs[h], vcat[:, pair[h]], preferred_element_type=F32) * inv_l[h] for h in heads]
    o = jnp.concatenate([jnp.where(low, outs[2 * hp], outs[2 * hp + 1]) for hp in range(A_HEADS // 2)], axis=1)
    return o, lse_tile


def _attn_kernel(q_ref, kp_ref, kc_ref, vp_ref, vc_ref, bias0_ref, bias_ref, o_ref, lse_ref, *, per_lead):
    def sub(ref, i):
        if ref.ndim == 2:
            return ref[i * A_BLOCK:(i + 1) * A_BLOCK, :]
        run = ref.shape[2] // per_lead
        return ref[i // per_lead, :, (i % per_lead) * run:(i % per_lead + 1) * run, :]

    def put(ref, i, val):
        if ref.ndim == 2:
            ref[i * A_BLOCK:(i + 1) * A_BLOCK, :] = val.astype(ref.dtype)
        else:
            run = ref.shape[2] // per_lead
            ref[i // per_lead, :, (i % per_lead) * run:(i % per_lead + 1) * run, :] = (
                val.astype(ref.dtype).reshape(ref.shape[1], run, ref.shape[3]))

    flat = lambda v: v.reshape(A_BLOCK, v.shape[-1])
    n_blocks = q_ref.shape[0] // A_BLOCK if q_ref.ndim == 2 else q_ref.shape[0] * per_lead
    k_prev, v_prev = flat(kp_ref[...]), flat(vp_ref[...])
    for i in range(n_blocks):
        k_cur, v_cur = flat(sub(kc_ref, i)), flat(sub(vc_ref, i))
        o, lse_tile = _attn_block(flat(sub(q_ref, i)), jnp.concatenate([k_prev, k_cur], axis=0),
                                  jnp.concatenate([v_prev, v_cur], axis=0), bias0_ref if i == 0 else bias_ref)
        put(o_ref, i, o)
        put(lse_ref, i, lse_tile)
        k_prev, v_prev = k_cur, v_cur


def _dilated_attention_one(q, k, v, bias_tab, bsz, seq, dil):
    nb = seq // dil // A_BLOCK
    step = min(ATTN_BLOCKS_PER_STEP, nb)
    w = A_WIDTH
    if dil == 1:
        per_lead = 1
        view = lambda a: a.reshape(bsz, seq, a.shape[-1])
        block = lambda wd: (None, step * A_BLOCK, wd)
        block_prev = lambda wd: (None, A_BLOCK, wd)
        cur = lambda b, r, m: (b, m, 0)
        prev = lambda b, r, m: (b, jnp.maximum(m * step - 1, 0), 0)
        out_shape = lambda wd, dt: jax.ShapeDtypeStruct((bsz, seq, wd), dt)
    else:
        runs = SPAN_RES // dil
        run = A_BLOCK // runs
        spans_per_seq = seq // SPAN
        per_lead = min(step, runs)
        lead = step // per_lead
        assert runs % per_lead == 0 and spans_per_seq % lead == 0
        view = lambda a: a.reshape(a.shape[0], runs, dil, A_BLOCK, a.shape[-1])
        block = lambda wd: (lead, runs, None, per_lead * run, wd)
        block_prev = lambda wd: (1, runs, None, run, wd)
        cur = lambda b, r, m: ((b * spans_per_seq + (m * step) // runs) // lead, 0, r, ((m * step) % runs) // per_lead, 0)

        def prev(b, r, m):
            n = jnp.maximum(m * step - 1, 0)
            return (b * spans_per_seq + n // runs, 0, r, n % runs, 0)

        out_shape = lambda wd, dt: jax.ShapeDtypeStruct((bsz * spans_per_seq, runs, dil, A_BLOCK, wd), dt)
    blk = pl.BlockSpec(block(w), cur)
    blk_prev = pl.BlockSpec(block_prev(w), prev)
    bias_block = (1, A_HEADS, A_BLOCK, 2 * A_BLOCK)
    o, lse = pl.pallas_call(
        functools.partial(_attn_kernel, per_lead=per_lead),
        out_shape=[out_shape(w, BF16), out_shape(LANES, F32)],
        grid=(bsz, dil, nb // step),
        in_specs=[blk, blk_prev, blk, blk_prev, blk,
                  pl.BlockSpec(bias_block, lambda b, r, m: (jnp.minimum(m, 1), 0, 0, 0)),
                  pl.BlockSpec(bias_block, lambda b, r, m: (1, 0, 0, 0))],
        out_specs=[pl.BlockSpec(block(w), cur), pl.BlockSpec(block(LANES), cur)],
        compiler_params=_cparams(("arbitrary", "arbitrary", "arbitrary")),
        name=f"dilated_attn_d{dil}",
    )(view(q), view(k), view(k), view(v), view(v), bias_tab, bias_tab)
    if dil == 1:
        return o.reshape(bsz * seq, w), lse.reshape(bsz * seq, LANES)
    return o.reshape(o.shape[0], SPAN_RES, A_BLOCK, w), lse.reshape(o.shape[0], SPAN_RES, A_BLOCK, LANES)


def _gelu_tanh(x):
    c = math.sqrt(2.0 / math.pi)
    return x * (0.5 * (1.0 + jnp.tanh(c * (x + 0.044715 * (x * x * x)))))


def _s5_kernel(u_ref, pm_ref, pmt_ref, bd_ref, pw_ref, cd_ref, dsk_ref, gw_ref, gb_ref, o_ref, x_scr, st_scr, *,
               n_state):
    @pl.when(pl.program_id(1) == 0)
    def _():
        st_scr[...] = jnp.zeros_like(st_scr)

    for tile_idx in range(S5_TILES_PER_STEP):
        _s5_tile(tile_idx, u_ref, pm_ref, pmt_ref, bd_ref, pw_ref, cd_ref, dsk_ref, gw_ref, gb_ref, o_ref,
                 x_scr, st_scr, n_state)


def _s5_tile(tile_idx, u_ref, pm_ref, pmt_ref, bd_ref, pw_ref, cd_ref, dsk_ref, gw_ref, gb_ref, o_ref,
             x_scr, st_scr, n_state):
    n = n_state
    jn = S5_J
    tile_rows = slice(tile_idx * SUBLANES * jn, (tile_idx + 1) * SUBLANES * jn)

    u = u_ref[tile_rows, :]
    u_perm = jnp.dot(pm_ref[...], u.astype(BF16), preferred_element_type=F32).astype(BF16)
    cw = u.shape[1] * S5_LW // n
    chunks = [(slice(c0, c0 + S5_LW), slice(n + c0, n + c0 + S5_LW)) for c0 in range(0, n, S5_LW)]

    def project_in(q):
        bu = jnp.dot(u_perm[:, q * cw:(q + 1) * cw], bd_ref[q], preferred_element_type=F32)
        x_scr[:, chunks[q][0]] = bu[:, 0:S5_LW]
        x_scr[:, chunks[q][1]] = bu[:, S5_LW:]

    project_in(0)
    end_r, end_i = [], []
    for q, (re, im) in enumerate(chunks):
        if q + 1 < len(chunks):
            project_in(q + 1)
        a1r, a1i = pw_ref[0:SUBLANES, re], pw_ref[0:SUBLANES, im]
        xr = jnp.zeros((SUBLANES, S5_LW), F32)
        xi = jnp.zeros((SUBLANES, S5_LW), F32)
        for j in range(jn):
            rows = slice(SUBLANES * j, SUBLANES * (j + 1))
            nr = a1r * xr - a1i * xi + x_scr[rows, re]
            ni = a1r * xi + a1i * xr + x_scr[rows, im]
            xr, xi = nr, ni
            x_scr[rows, re] = xr
            x_scr[rows, im] = xi
        end_r.append(xr)
        end_i.append(xi)
    er, ei = jnp.concatenate(end_r, axis=1), jnp.concatenate(end_i, axis=1)
    last = SUBLANES * (jn - 1)
    ajr, aji = pw_ref[last:last + 1, 0:n], pw_ref[last:last + 1, n:]
    pr, pi = st_scr[0:1, 0:n], st_scr[0:1, n:]
    cin_r, cin_i = [], []
    for s in range(SUBLANES):
        cin_r.append(pr)
        cin_i.append(pi)
        nr = er[s:s + 1] + ajr * pr - aji * pi
        ni = ei[s:s + 1] + ajr * pi + aji * pr
        pr, pi = nr, ni
    st_scr[0:1, 0:n] = pr
    st_scr[0:1, n:] = pi
    cr_all = jnp.concatenate(cin_r, axis=0)
    ci_all = jnp.concatenate(cin_i, axis=0)
    ys = []
    for q, (re, im) in enumerate(chunks):
        cr, ci = cr_all[:, re], ci_all[:, re]
        for j in range(jn):
            rows = slice(SUBLANES * j, SUBLANES * (j + 1))
            pjr, pji = pw_ref[rows, re], pw_ref[rows, im]
            x_scr[rows, re] = x_scr[rows, re] + (pjr * cr - pji * ci)
            x_scr[rows, im] = x_scr[rows, im] + (pjr * ci + pji * cr)
        xcat = jnp.concatenate([x_scr[:, re], x_scr[:, im]], axis=1).astype(BF16)
        ys.append(jnp.dot(xcat, cd_ref[q], preferred_element_type=F32))

    y_perm = jnp.concatenate(ys, axis=1)
    y_hi = y_perm.astype(BF16)
    y_lo = (y_perm - y_hi.astype(F32)).astype(BF16)
    y = (jnp.dot(pmt_ref[...], y_hi, preferred_element_type=F32)
         + jnp.dot(pmt_ref[...], y_lo, preferred_element_type=F32))
    y = _gelu_tanh(y + dsk_ref[...] * u)
    z = jnp.dot(y.astype(BF16), gw_ref[...], preferred_element_type=F32) + gb_ref[...]
    o_ref[tile_rows, :] = (y * _sigmoid(z)).astype(o_ref.dtype)


def _s5_tables(a_re, a_im, log_dt, b_re, b_im, c_re, c_im):
    g, p = a_re.shape
    dt = jnp.exp(log_dt.astype(F32))[:, None]
    ar, ai = a_re.astype(F32), a_im.astype(F32)
    mag = jnp.exp(dt * ar)
    abar_r, abar_i = mag * jnp.cos(dt * ai), mag * jnp.sin(dt * ai)
    den = ar * ar + ai * ai
    fr = ((abar_r - 1.0) * ar + abar_i * ai) / den
    fi = (abar_i * ar - (abar_r - 1.0) * ai) / den
    br, bi = b_re.astype(F32), b_im.astype(F32)
    bbar_r = fr[..., None] * br - fi[..., None] * bi
    bbar_i = fr[..., None] * bi + fi[..., None] * br
    m = br.shape[-1]
    gc = S5_LW // p
    nq = g // gc
    eye = jnp.eye(gc, dtype=F32)
    dense_b = lambda t: jnp.einsum('qgpm,gh->qgmhp', t.reshape(nq, gc, p, m), eye).reshape(nq, gc * m, gc * p)
    bd = jnp.concatenate([dense_b(bbar_r), dense_b(bbar_i)], axis=2)
    dense_c = lambda t: jnp.einsum('qgmp,gh->qgphm', t.reshape(nq, gc, m, p), eye).reshape(nq, gc * p, gc * m)
    cd = jnp.concatenate([dense_c(c_re.astype(F32)), -dense_c(c_im.astype(F32))], axis=1)
    kk = jnp.arange(1, S5_J + 1, dtype=F32)[:, None, None]
    pmag = jnp.exp(kk * (dt * ar)[None])
    pw_r = (pmag * jnp.cos(kk * (dt * ai)[None])).reshape(S5_J, g * p)
    pw_i = (pmag * jnp.sin(kk * (dt * ai)[None])).reshape(S5_J, g * p)
    pw = jnp.repeat(jnp.concatenate([pw_r, pw_i], axis=1), SUBLANES, axis=0)
    return bd.astype(BF16), pw, cd.astype(BF16)


def _s5_layer(u, bsz, seq, a_re, a_im, log_dt, b_re, b_im, c_re, c_im, d_skip, glu_w, glu_b):
    t, width = u.shape
    n = a_re.shape[0] * a_re.shape[1]
    bd, pw, cd = _s5_tables(a_re, a_im, log_dt, b_re, b_im, c_re, c_im)
    tile = SUBLANES * S5_J
    step_rows = S5_TILES_PER_STEP * tile
    steps_per_seq = seq // step_rows
    const = lambda b, i: (0, 0)
    src = (np.arange(tile) % SUBLANES) * S5_J + np.arange(tile) // SUBLANES
    perm = (src[:, None] == np.arange(tile)[None, :]).astype(np.float32)
    return pl.pallas_call(
        functools.partial(_s5_kernel, n_state=n),
        out_shape=jax.ShapeDtypeStruct((t, width), BF16),
        grid=(bsz, steps_per_seq),
        in_specs=[pl.BlockSpec((step_rows, width), lambda b, i: (b * steps_per_seq + i, 0)),
                  pl.BlockSpec((tile, tile), const),
                  pl.BlockSpec((tile, tile), const),
                  pl.BlockSpec(bd.shape, lambda b, i: (0, 0, 0)),
                  pl.BlockSpec((tile, 2 * n), const),
                  pl.BlockSpec(cd.shape, lambda b, i: (0, 0, 0)),
                  pl.BlockSpec((1, width), const),
                  pl.BlockSpec((width, width), const),
                  pl.BlockSpec((1, width), const)],
        out_specs=pl.BlockSpec((step_rows, width), lambda b, i: (b * steps_per_seq + i, 0)),
        scratch_shapes=[pltpu.VMEM((tile, 2 * n), F32), pltpu.VMEM((SUBLANES, 2 * n), F32)],
        compiler_params=_cparams(("arbitrary", "arbitrary")),
        name="s5_layer",
    )(u, jnp.asarray(perm, BF16), jnp.asarray(perm.T, BF16), bd, pw, cd,
      d_skip.reshape(1, width).astype(F32), glu_w.astype(BF16),
      glu_b.reshape(1, width).astype(F32))


def _finish(y, x_ref, mod_ref, pg_ref, out_ref, d_model):
    ms = jnp.mean(y * y, axis=-1, keepdims=True)
    yn = y * lax.rsqrt(ms + EPS) * pg_ref[...]
    gate_mod = mod_ref[0, :, 2 * d_model:3 * d_model]
    out_ref[...] = x_ref[...] + gate_mod * yn


def _split2(x):
    hi = x.astype(BF16)
    return hi, (x - hi.astype(F32)).astype(BF16)


def _out_ab_kernel(o0_ref, l0_ref, o1_ref, o2_ref, l1_ref, l2_ref, ob_ref, gate_ref, e_ref, un_ref, w_ref,
                   x_ref, mod_ref, pg_ref, out_ref, *, d_model):
    flat = lambda ref: ref[...].reshape(ROW_TILE, ref.shape[-1])
    aw = o0_ref.shape[-1]

    def unperm(o_ref, l_ref):
        parts = jnp.concatenate([flat(o_ref)] + list(_split2(flat(l_ref))), axis=1)
        moved = jnp.dot(un_ref[...], parts, preferred_element_type=F32)
        return moved[:, 0:aw], moved[:, aw:aw + LANES] + moved[:, aw + LANES:]

    o0, l0 = o0_ref[...].astype(F32), l0_ref[...]
    o1, l1 = unperm(o1_ref, l1_ref)
    o2, l2 = unperm(o2_ref, l2_ref)
    mx = jnp.maximum(jnp.maximum(l0, l1), l2)
    e0, e1, e2 = jnp.exp(l0 - mx), jnp.exp(l1 - mx), jnp.exp(l2 - mx)
    inv_den = 1.0 / (e0 + e1 + e2)
    stacked = jnp.concatenate(list(_split2(e1 * inv_den)) + list(_split2(e2 * inv_den)), axis=0)
    wide = jnp.dot(stacked, e_ref[...], preferred_element_type=F32)
    term = lambda i: wide[i * ROW_TILE:(i + 1) * ROW_TILE]
    w1 = term(0) + term(1)
    w2 = term(2) + term(3)
    o_a = (1.0 - w1 - w2) * o0 + w1 * o1 + w2 * o2
    gate = gate_ref[...].astype(F32)
    sg = gate * _sigmoid(gate)
    aw = o_a.shape[-1]
    y = jnp.dot((o_a * sg[:, :aw]).astype(BF16), w_ref[0:aw, :], preferred_element_type=F32)
    y = y + jnp.dot((ob_ref[...].astype(F32) * sg[:, aw:]).astype(BF16), w_ref[aw:, :],
                    preferred_element_type=F32)
    _finish(y, x_ref, mod_ref, pg_ref, out_ref, d_model)


def _out_c_kernel(o_ref, gate_ref, w_ref, x_ref, mod_ref, pg_ref, out_ref, *, d_model):
    gate = gate_ref[...].astype(F32)
    o = o_ref[...].astype(F32) * (gate * _sigmoid(gate))
    y = jnp.dot(o.astype(BF16), w_ref[...], preferred_element_type=F32)
    _finish(y, x_ref, mod_ref, pg_ref, out_ref, d_model)


def _out_proj(kern, row_inputs, const_inputs, w_bf16, x2, mod3, post_g, seq, name):
    t, d = x2.shape
    tiles_per_seq = seq // ROW_TILE
    tiles_per_span = SPAN // ROW_TILE

    def row_spec(a):
        if a.ndim == 2:
            return pl.BlockSpec((ROW_TILE, a.shape[1]), lambda i: (i, 0))
        return pl.BlockSpec((None, SPAN_RES, SPAN_RUN, a.shape[-1]),
                            lambda i: (i // tiles_per_span, 0, i % tiles_per_span, 0))

    const_spec = lambda a: pl.BlockSpec(a.shape, lambda i: (0, 0))
    return pl.pallas_call(
        functools.partial(kern, d_model=d),
        out_shape=jax.ShapeDtypeStruct((t, d), F32),
        grid=(t // ROW_TILE,),
        in_specs=([row_spec(a) for a in row_inputs] + [const_spec(a) for a in const_inputs]
                  + [const_spec(w_bf16), row_spec(x2),
                     pl.BlockSpec((1, 1, 3 * d), lambda i: (i // tiles_per_seq, 0, 0)),
                     pl.BlockSpec((1, d), lambda i: (0, 0))]),
        out_specs=pl.BlockSpec((ROW_TILE, d), lambda i: (i, 0)),
        compiler_params=_cparams(("arbitrary",)),
        name=name,
    )(*row_inputs, *const_inputs, w_bf16, x2, mod3, post_g.reshape(1, d))


def _gdn_prep_kernel(x_ref, halo_ref, br_ref, cw_ref, alog_ref, dtb_ref,
                     w_ref, u_ref, qg_ref, kdt_ref, aqk_ref, dec_ref, xs_scr):
    n_all = GDN_PREP_BLOCKS * GDN_STEP
    first = pl.program_id(1) == 0
    halo = halo_ref[...]
    xs_scr[0:SUBLANES, :] = jnp.where(first, jnp.zeros_like(halo), halo)
    xs_scr[SUBLANES:SUBLANES + n_all, :] = x_ref[...]
    cw = cw_ref[...]
    conv = xs_scr[SUBLANES:SUBLANES + n_all, :] * cw[C_CONV - 1:C_CONV]
    for j in range(C_CONV - 1):
        off = SUBLANES - (C_CONV - 1) + j
        conv = conv + xs_scr[off:off + n_all, :] * cw[j:j + 1]
    act = conv * _sigmoid(conv)
    for blk in range(GDN_PREP_BLOCKS):
        _gdn_prep_block(blk, act, br_ref, alog_ref, dtb_ref, w_ref, u_ref, qg_ref, kdt_ref, aqk_ref, dec_ref)


def _gdn_prep_block(blk, act, br_ref, alog_ref, dtb_ref, w_ref, u_ref, qg_ref, kdt_ref, aqk_ref, dec_ref):
    n = GDN_STEP
    rows = slice(blk * n, (blk + 1) * n)
    head_cols = lambda base, h: act[rows, base + h * C_DK:base + (h + 1) * C_DK]

    row = lax.broadcasted_iota(jnp.int32, (n, n), 0)
    col = lax.broadcasted_iota(jnp.int32, (n, n), 1)
    same = (row // GDN_CHUNK) == (col // GDN_CHUNK)
    tril = jnp.logical_and(same, row >= col)
    strict = jnp.logical_and(same, row > col)
    eye = (row == col).astype(F32)

    br = br_ref[rows, :]
    beta_all = _sigmoid(br)
    xg = br + dtb_ref[...]
    softplus = jnp.maximum(xg, 0.0) + jnp.log(1.0 + jnp.exp(-jnp.abs(xg)))
    g_all = -jnp.exp(alog_ref[...]) * softplus
    gc_all = jnp.dot(tril.astype(F32), g_all, preferred_element_type=F32, precision=HI)
    gc_t = gc_all.T
    first_chunk = lax.broadcasted_iota(jnp.int32, (n, 1), 0) < GDN_CHUNK
    dec_row = lax.broadcasted_iota(jnp.int32, (SUBLANES, LANES), 0)

    heads = range(C_HEADS)
    lanes = [slice(h * LANES, (h + 1) * LANES) for h in heads]
    gcols = [gc_all[:, C_HEADS + h:C_HEADS + h + 1] for h in heads]
    egs = [jnp.exp(g) for g in gcols]
    kbs, amats, invs = [], [], []
    for h in heads:
        grow = gc_t[C_HEADS + h:C_HEADS + h + 1, :]
        decay = jnp.where(tril, jnp.exp(jnp.where(tril, gcols[h] - grow, 0.0)), 0.0)
        q = head_cols(0, h)
        k = head_cols(C_HEADS * C_DK, h)
        q = q * lax.rsqrt(jnp.sum(q * q, axis=-1, keepdims=True) + EPS) * (C_DK ** -0.5)
        k = k * lax.rsqrt(jnp.sum(k * k, axis=-1, keepdims=True) + EPS)
        kb = k * beta_all[:, h:h + 1]
        k16 = k.astype(BF16)
        a = jnp.where(strict, lax.dot_general(kb.astype(BF16), k16, _NT, preferred_element_type=F32) * decay, 0.0)
        aqk = lax.dot_general(q.astype(BF16), k16, _NT, preferred_element_type=F32) * decay
        aqk_ref[rows, lanes[h]] = aqk.astype(BF16)
        qg_ref[rows, lanes[h]] = (q * egs[h]).astype(BF16)
        g_last = jnp.where(first_chunk, gcols[h][GDN_CHUNK - 1:GDN_CHUNK], gcols[h][n - 1:n])
        kdt_ref[rows, lanes[h]] = (k * jnp.exp(g_last - gcols[h])).T.astype(BF16)
        dec_ref[blk, :, lanes[h]] = jnp.where(dec_row < SUBLANES // 2, egs[h][GDN_CHUNK - 1:GDN_CHUNK],
                                             egs[h][n - 1:n])
        kbs.append(kb)
        amats.append(a)
        invs.append(eye - a)
    pows = []
    for h in heads:
        a16 = amats[h].astype(BF16)
        pows.append(jnp.dot(a16, a16, preferred_element_type=F32))
    for _ in range(4):
        for h in heads:
            p16 = pows[h].astype(BF16)
            both = jnp.dot(jnp.concatenate([invs[h].astype(BF16), p16], axis=0), p16, preferred_element_type=F32)
            invs[h] = invs[h] + both[0:n]
            pows[h] = both[n:]
    for h in heads:
        invs[h] = invs[h] + jnp.dot(invs[h].astype(BF16), pows[h].astype(BF16), preferred_element_type=F32)
    for h in heads:
        v = head_cols(2 * C_HEADS * C_DK, h)
        rhs = jnp.concatenate([v * beta_all[:, h:h + 1], kbs[h] * egs[h]], axis=1)
        uw = jnp.dot(invs[h].astype(BF16), rhs.astype(BF16), preferred_element_type=F32)
        u_ref[rows, lanes[h]] = uw[:, 0:C_DV]
        w_ref[rows, lanes[h]] = uw[:, C_DV:].astype(BF16)


def _gdn_rec_kernel(w_ref, u_ref, qg_ref, kdt_ref, aqk_ref, dec_ref, ng_ref, o_ref, s_scr):
    @pl.when(pl.program_id(1) == 0)
    def _():
        s_scr[...] = jnp.zeros_like(s_scr)

    half = SUBLANES // 2
    zeros = jnp.zeros((GDN_CHUNK, C_DV), BF16)
    heads = range(C_HEADS)
    lanes = [slice(h * LANES, (h + 1) * LANES) for h in heads]
    states = [s_scr[h] for h in heads]
    for blk in range(GDN_REC_BLOCKS):
        blk_rows = slice(blk * GDN_STEP, (blk + 1) * GDN_STEP)
        for j in range(GDN_STEP // GDN_CHUNK):
            rows = slice(blk * GDN_STEP + j * GDN_CHUNK, blk * GDN_STEP + (j + 1) * GDN_CHUNK)
            wss = []
            for h in heads:
                wq = jnp.concatenate([w_ref[rows, lanes[h]], qg_ref[rows, lanes[h]]], axis=0)
                wss.append(jnp.dot(wq, states[h].astype(BF16), preferred_element_type=F32))
            for h in heads:
                v_new = (u_ref[rows, lanes[h]] - wss[h][0:GDN_CHUNK]).astype(BF16)
                v_pad = jnp.concatenate([v_new, zeros] if j == 0 else [zeros, v_new], axis=0)
                o = wss[h][GDN_CHUNK:] + jnp.dot(aqk_ref[rows, lanes[h]], v_pad, preferred_element_type=F32)
                dec = dec_ref[blk, j * half:j * half + 1, lanes[h]]
                states[h] = states[h] * dec + jnp.dot(kdt_ref[blk_rows, lanes[h]], v_pad,
                                                      preferred_element_type=F32)
                ms = jnp.mean(o * o, axis=-1, keepdims=True)
                o_ref[rows, lanes[h]] = (o * lax.rsqrt(ms + EPS) * ng_ref[...]).astype(o_ref.dtype)
    for h in heads:
        s_scr[h] = states[h]


def _gdn_core(qkv_pre, br, bsz, seq, conv_w, a_log, dt_bias, norm_g):
    t, qkv_w = qkv_pre.shape
    hw = C_HEADS * C_DV
    steps_per_seq = seq // GDN_STEP
    prep_rows = GDN_PREP_BLOCKS * GDN_STEP
    prep_steps = steps_per_seq // GDN_PREP_BLOCKS
    halo_blocks = prep_rows // SUBLANES
    pad_row = lambda vec: jnp.zeros((1, LANES), F32).at[0, C_HEADS:2 * C_HEADS].set(vec.astype(F32))
    tok = lambda b, i: (b * prep_steps + i, 0)
    const = lambda b, i: (0, 0)
    tok_spec = pl.BlockSpec((prep_rows, hw), tok)
    dec_spec = pl.BlockSpec((GDN_PREP_BLOCKS, SUBLANES, hw), lambda b, i: (b * prep_steps + i, 0, 0))
    w, u, qg, kdt, aqk, dec = pl.pallas_call(
        _gdn_prep_kernel,
        out_shape=[jax.ShapeDtypeStruct((t, hw), BF16), jax.ShapeDtypeStruct((t, hw), F32),
                   jax.ShapeDtypeStruct((t, hw), BF16), jax.ShapeDtypeStruct((t, hw), BF16),
                   jax.ShapeDtypeStruct((t, hw), BF16),
                   jax.ShapeDtypeStruct((t // GDN_STEP, SUBLANES, hw), F32)],
        grid=(bsz, prep_steps),
        in_specs=[pl.BlockSpec((prep_rows, qkv_w), tok),
                  pl.BlockSpec((SUBLANES, qkv_w),
                               lambda b, i: (jnp.maximum((b * prep_steps + i) * halo_blocks - 1, 0), 0)),
                  pl.BlockSpec((prep_rows, LANES), tok),
                  pl.BlockSpec((C_CONV, qkv_w), const),
                  pl.BlockSpec((1, LANES), const),
                  pl.BlockSpec((1, LANES), const)],
        out_specs=[tok_spec, tok_spec, tok_spec, tok_spec, tok_spec, dec_spec],
        scratch_shapes=[pltpu.VMEM((SUBLANES + prep_rows, qkv_w), F32)],
        compiler_params=_cparams(("arbitrary", "arbitrary")),
        name="gdn_prep",
    )(qkv_pre, qkv_pre, br, conv_w.astype(F32), pad_row(a_log), pad_row(dt_bias))
    rec_steps = steps_per_seq // GDN_REC_BLOCKS
    rec_tok = pl.BlockSpec((GDN_REC_BLOCKS * GDN_STEP, hw), lambda b, i: (b * rec_steps + i, 0))
    rec_dec = pl.BlockSpec((GDN_REC_BLOCKS, SUBLANES, hw), lambda b, i: (b * rec_steps + i, 0, 0))
    return pl.pallas_call(
        _gdn_rec_kernel,
        out_shape=jax.ShapeDtypeStruct((t, hw), BF16),
        grid=(bsz, rec_steps),
        in_specs=[rec_tok, rec_tok, rec_tok, rec_tok, rec_tok, rec_dec,
                  pl.BlockSpec((1, C_DV), const)],
        out_specs=rec_tok,
        scratch_shapes=[pltpu.VMEM((C_HEADS, C_DK, C_DV), F32)],
        compiler_params=_cparams(("arbitrary", "arbitrary")),
        name="gdn_recurrence",
    )(w, u, qg, kdt, aqk, dec, norm_g.reshape(1, C_DV).astype(F32))


def _ab_layer(x2, mod3, bsz, seq, pre_g, post_g, rel_bias, w_in, w_out, s5_params):
    b_width = s5_params[7].shape[-1]
    assert DILATED_CONFIGS[0][1] == 1 and all(SPAN_RES % dl == 0 for _, dl in DILATED_CONFIGS) and seq % SPAN == 0
    head_expand = jnp.asarray(np.arange(LANES)[:, None] == (np.arange(A_WIDTH)[None, :] // A_HEAD_DIM), BF16)
    unperm = jnp.asarray(_span_perm().T, BF16)
    splits = ((0, A_WIDTH, A_HEAD_DIM ** -0.5), (A_WIDTH, A_WIDTH, 1.0), (2 * A_WIDTH, A_WIDTH, 1.0),
              (3 * A_WIDTH, b_width, 1.0), (3 * A_WIDTH + b_width, A_WIDTH + b_width, 1.0))
    q, k, v, u, gate, qs, ks, vs = _in_proj(x2, mod3, pre_g, w_in.astype(BF16), splits,
                                            (BF16, BF16, BF16, F32, BF16), seq, n_span=3)
    outs, lses = [], []
    bias_tabs = _attn_bias_tables(rel_bias)
    for cfg, (_, dil) in enumerate(DILATED_CONFIGS):
        qkv = (q, k, v) if dil == 1 else (qs, ks, vs)
        o_c, lse_c = _dilated_attention_one(*qkv, bias_tabs[cfg], bsz, seq, dil)
        outs.append(o_c)
        lses.append(lse_c)
    o_b = _s5_layer(u, bsz, seq, *s5_params)
    row_inputs = [outs[0], lses[0], outs[1], outs[2], lses[1], lses[2], o_b, gate]
    return _out_proj(_out_ab_kernel, row_inputs, [head_expand, unperm], w_out.astype(BF16),
                     x2, mod3, post_g, seq, "out_proj_ab")


def _gdn_layer(x2, mod3, bsz, seq, pre_g, post_g, w_in, conv_w, a_log, dt_bias, norm_g, w_out):
    d = x2.shape[1]
    qkv_w = 2 * C_HEADS * C_DK + C_HEADS * C_DV
    gate_w = C_HEADS * C_DV
    w_pad = jnp.concatenate(
        [w_in, jnp.zeros((d, LANES - (w_in.shape[1] - qkv_w - gate_w)), w_in.dtype)], axis=1).astype(BF16)
    splits = ((0, qkv_w, 1.0), (qkv_w, gate_w, 1.0), (qkv_w + gate_w, LANES, 1.0))
    qkv_pre, gate, br = _in_proj(x2, mod3, pre_g, w_pad, splits, (F32, BF16, F32), seq)
    o = _gdn_core(qkv_pre, br, bsz, seq, conv_w, a_log, dt_bias, norm_g)
    return _out_proj(_out_c_kernel, [o, gate], [], w_out.astype(BF16), x2, mod3, post_g, seq, "out_proj_c")


def kernel(x, c, ada_w, ada_b, pre_g, post_g, rel_bias, ab_w_in, ab_w_out, s5_a_re, s5_a_im, s5_log_dt, s5_b_re, s5_b_im, s5_c_re, s5_c_im, s5_d, s5_glu_w, s5_glu_b, gdn_w_in, gdn_conv, gdn_a_log, gdn_dt_bias, gdn_norm_g, gdn_w_out):
    bsz, seq, d = x.shape
    depth = ada_w.shape[0]
    assert seq % ROW_TILE == 0 and seq % (SUBLANES * S5_J * S5_TILES_PER_STEP) == 0
    assert seq % (GDN_STEP * GDN_PREP_BLOCKS) == 0 and seq % (GDN_STEP * GDN_REC_BLOCKS) == 0
    x2 = x.reshape(bsz * seq, d)
    mod = _adaln_mod(c, ada_w, ada_b)
    for layer in range(depth):
        j = layer // 2
        mod3 = mod[layer].reshape(bsz, 1, 3 * d)
        if layer % 2 == 0:
            s5_params = (s5_a_re[j], s5_a_im[j], s5_log_dt[j], s5_b_re[j], s5_b_im[j], s5_c_re[j], s5_c_im[j],
                         s5_d[j], s5_glu_w[j], s5_glu_b[j])
            x2 = _ab_layer(x2, mod3, bsz, seq, pre_g[layer], post_g[layer], rel_bias, ab_w_in[j], ab_w_out[j],
                           s5_params)
        else:
            x2 = _gdn_layer(x2, mod3, bsz, seq, pre_g[layer], post_g[layer], gdn_w_in[j], gdn_conv[j],
                            gdn_a_log[j], gdn_dt_bias[j], gdn_norm_g[j], gdn_w_out[j])
    return x2.reshape(bsz, seq, d)
```

```python
import functools
import math

import numpy as np
import jax
import jax.numpy as jnp
from jax import lax
from jax.experimental import pallas as pl
from jax.experimental.pallas import tpu as pltpu

F32 = jnp.float32
BF16 = jnp.bfloat16
HI = lax.Precision.HIGHEST

EPS = 1e-6
A_HEADS = 8
A_HEAD_DIM = 64
A_WIDTH = A_HEADS * A_HEAD_DIM
A_BLOCK = 128
DILATED_CONFIGS = ((128, 1), (512, 4), (2048, 16))
B_GROUP = 16
B_STATE = 64
REL_BUCKETS = 32
REL_MAX_DIST = 2048
C_HEADS = 8
C_DK = 128
C_DV = 128
C_CONV = 4
MASK_NEG = -1e30
LOG2_E = math.log2(math.e)

LANES = 128
SUBLANES = 8
VMEM_LIMIT = 48 * 1024 * 1024
ROW_TILE = 512
COL_CHUNK = 512
SPAN_RES = max(dl for _, dl in DILATED_CONFIGS)
SPAN = A_BLOCK * SPAN_RES
SPAN_RUN = ROW_TILE // SPAN_RES
ATTN_BLOCKS_PER_STEP = 4
S5_J = 32
S5_LW = 512
S5_TILES_PER_STEP = 2
GDN_STEP = 128
GDN_CHUNK = 64
GDN_PREP_BLOCKS = 2
GDN_REC_BLOCKS = 4

_NT = (((1,), (1,)), ((), ()))


def _cparams(sem):
    return pltpu.CompilerParams(dimension_semantics=sem, vmem_limit_bytes=VMEM_LIMIT)


def _sigmoid(x):
    return 1.0 / (1.0 + jnp.exp(-x))


def _mod_kernel(c_ref, w_ref, b_ref, o_ref):
    c = c_ref[...]
    ca = c * _sigmoid(c)
    o_ref[0] = jnp.dot(ca, w_ref[0], preferred_element_type=F32, precision=HI) + b_ref[0]


def _adaln_mod(c, ada_w, ada_b):
    depth, d, d3 = ada_w.shape
    bsz = c.shape[0]
    return pl.pallas_call(
        _mod_kernel,
        out_shape=jax.ShapeDtypeStruct((depth, bsz, d3), F32),
        grid=(depth, d3 // d),
        in_specs=[pl.BlockSpec((bsz, d), lambda l, j: (0, 0)),
                  pl.BlockSpec((1, d, d), lambda l, j: (l, 0, j)),
                  pl.BlockSpec((1, 1, d), lambda l, j: (l, 0, j))],
        out_specs=pl.BlockSpec((1, bsz, d), lambda l, j: (l, 0, j)),
        compiler_params=_cparams(("arbitrary", "arbitrary")),
        name="adaln_mod",
    )(c, ada_w, ada_b.reshape(depth, 1, d3))


def _span_perm():
    rho = np.arange(ROW_TILE)
    nat = SPAN_RES * (rho % SPAN_RUN) + rho // SPAN_RUN
    return (nat[:, None] == np.arange(ROW_TILE)[None, :]).astype(np.float32)


def _modulated_norm(x_ref, mod_ref, g_ref, d_model):
    x = x_ref[...]
    ms = jnp.mean(x * x, axis=-1, keepdims=True)
    y = x * lax.rsqrt(ms + EPS) * g_ref[...]
    shift = mod_ref[0, :, 0:d_model]
    scale = mod_ref[0, :, d_model:2 * d_model]
    return (y * (1.0 + scale) + shift).astype(BF16)


def _in_proj_kernel(x_ref, mod_ref, g_ref, w_ref, *rest, splits, d_model, n_span):
    pm_ref = rest[0] if n_span else None
    out_refs = rest[1:] if n_span else rest
    span_refs = out_refs[len(splits):]
    h = _modulated_norm(x_ref, mod_ref, g_ref, d_model)
    for idx, ((c0, width, mult), o_ref) in enumerate(zip(splits, out_refs)):
        for cc in range(0, width, COL_CHUNK):
            cw = min(COL_CHUNK, width - cc)
            acc = jnp.dot(h, w_ref[:, c0 + cc:c0 + cc + cw], preferred_element_type=F32)
            if mult != 1.0:
                acc = acc * mult
            val = acc.astype(o_ref.dtype)
            o_ref[:, cc:cc + cw] = val
            if idx < n_span:
                moved = jnp.dot(pm_ref[...], val, preferred_element_type=F32).astype(BF16)
                span_refs[idx][:, :, cc:cc + cw] = moved.reshape(SPAN_RES, SPAN_RUN, cw)


def _in_proj(x2, mod3, gain, w_bf16, splits, out_dtypes, seq, n_span=0):
    t, d = x2.shape
    tiles_per_seq = seq // ROW_TILE
    tiles_per_span = SPAN // ROW_TILE
    n_w = w_bf16.shape[1]
    row = lambda wd: pl.BlockSpec((ROW_TILE, wd), lambda i: (i, 0))
    span_shape = lambda wd: jax.ShapeDtypeStruct((t // SPAN, SPAN_RES, A_BLOCK, wd), BF16)
    span_spec = lambda wd: pl.BlockSpec((None, SPAN_RES, SPAN_RUN, wd),
                                        lambda i: (i // tiles_per_span, 0, i % tiles_per_span, 0))
    perm_in = [jnp.asarray(_span_perm(), BF16)] if n_span else []
    perm_spec = [pl.BlockSpec((ROW_TILE, ROW_TILE), lambda i: (0, 0))] if n_span else []
    return pl.pallas_call(
        functools.partial(_in_proj_kernel, splits=splits, d_model=d, n_span=n_span),
        out_shape=([jax.ShapeDtypeStruct((t, wd), dt) for (_, wd, _), dt in zip(splits, out_dtypes)]
                   + [span_shape(wd) for (_, wd, _) in splits[:n_span]]),
        grid=(t // ROW_TILE,),
        in_specs=[row(d),
                  pl.BlockSpec((1, 1, 3 * d), lambda i: (i // tiles_per_seq, 0, 0)),
                  pl.BlockSpec((1, d), lambda i: (0, 0)),
                  pl.BlockSpec((d, n_w), lambda i: (0, 0))] + perm_spec,
        out_specs=[row(wd) for (_, wd, _) in splits] + [span_spec(wd) for (_, wd, _) in splits[:n_span]],
        compiler_params=_cparams(("arbitrary",)),
        name="in_proj",
    )(x2, mod3, gain.reshape(1, d), w_bf16, *perm_in)


def _t5_bucket(dist):
    dist = np.maximum(dist, 0)
    max_exact = REL_BUCKETS // 2
    large = max_exact + (np.log(np.maximum(dist, 1) / max_exact)
                         / math.log(REL_MAX_DIST / max_exact) * (REL_BUCKETS - max_exact)).astype(np.int32)
    large = np.minimum(large, REL_BUCKETS - 1)
    return np.where(dist < max_exact, dist, large).astype(np.int32)


def _bias_kernel(rb_ref, bucket_ref, mask_ref, o_ref):
    def body(r, carry):
        rows = pl.ds(pl.multiple_of(r * SUBLANES, SUBLANES), SUBLANES)
        bk = bucket_ref[0, rows, :]
        accs = [jnp.zeros(bk.shape, F32) for _ in range(A_HEADS)]
        for b in range(REL_BUCKETS):
            eq = bk == b
            accs = [jnp.where(eq, rb_ref[b, h], acc) for h, acc in enumerate(accs)]
        for f in range(2):
            keep = mask_ref[0, f, rows, :] != 0
            for h in range(A_HEADS):
                o_ref[0, f, h, rows, :] = jnp.where(keep, accs[h] * LOG2_E, MASK_NEG)
        return carry
    lax.fori_loop(0, A_BLOCK // SUBLANES, body, 0)


def _attn_bias_tables(rel_bias):
    qi = np.arange(A_BLOCK)[:, None]
    kj = np.arange(2 * A_BLOCK)[None, :]
    rel = qi + A_BLOCK - kj
    buckets, masks = [], []
    for window, dil in DILATED_CONFIGS:
        band = (rel >= 0) & (rel <= window // dil)
        bucket = _t5_bucket(rel * dil)
        mask = np.stack([band & (kj >= A_BLOCK), band]).astype(np.int32)
        if dil > 1:
            runs = SPAN_RES // dil
            run = A_BLOCK // runs
            rho = np.arange(A_BLOCK)
            sub = runs * (rho % run) + rho // run
            keys = np.concatenate([sub, A_BLOCK + sub])
            bucket = bucket[sub][:, keys]
            mask = mask[:, sub][:, :, keys]
        buckets.append(bucket)
        masks.append(mask)
    n_cfg = len(DILATED_CONFIGS)
    return pl.pallas_call(
        _bias_kernel,
        out_shape=jax.ShapeDtypeStruct((n_cfg, 2, A_HEADS, A_BLOCK, 2 * A_BLOCK), F32),
        grid=(n_cfg,),
        in_specs=[pl.BlockSpec(memory_space=pltpu.SMEM),
                  pl.BlockSpec((1, A_BLOCK, 2 * A_BLOCK), lambda c: (c, 0, 0)),
                  pl.BlockSpec((1, 2, A_BLOCK, 2 * A_BLOCK), lambda c: (c, 0, 0, 0))],
        out_specs=pl.BlockSpec((1, 2, A_HEADS, A_BLOCK, 2 * A_BLOCK), lambda c: (c, 0, 0, 0, 0)),
        compiler_params=_cparams(("arbitrary",)),
        name="attn_bias",
    )(rel_bias.astype(F32), jnp.asarray(np.stack(buckets)), jnp.asarray(np.stack(masks)))


def _attn_block(q, kcat, vcat, bias_ref):
    lane = lax.broadcasted_iota(jnp.int32, (A_BLOCK, LANES), 1)
    low = lane < A_HEAD_DIM
    lse_tile = jnp.zeros((A_BLOCK, LANES), F32)
    zero = jnp.zeros((A_BLOCK, LANES), BF16)
    heads = range(A_HEADS)
    pair = [slice((h // 2) * LANES, (h // 2 + 1) * LANES) for h in heads]
    scores = []
    for h in heads:
        qm = jnp.where(low if h % 2 == 0 else jnp.logical_not(low), q[:, pair[h]], zero)
        scores.append(lax.dot_general(qm, kcat[:, pair[h]], _NT, preferred_element_type=F32) + bias_ref[0, h])
    probs, inv_l = [], []
    for h in heads:
        m = jnp.max(scores[h], axis=-1, keepdims=True)
        p = jnp.exp2(scores[h] - m)
        l = jnp.sum(p, axis=-1, keepdims=True)
        probs.append(p.astype(BF16))
        inv_l.append(1.0 / l)
        lse_tile = jnp.where(lane == h, m + jnp.log2(l), lse_tile)
    outs = [jnp.dot(probs[h], vcat[:, pair[h]], preferred_element_type=F32) * inv_l[h] for h in heads]
    o = jnp.concatenate([jnp.where(low, outs[2 * hp], outs[2 * hp + 1]) for hp in range(A_HEADS // 2)], axis=1)
    return o, lse_tile


def _attn_kernel(q_ref, kp_ref, kc_ref, vp_ref, vc_ref, bias0_ref, bias_ref, o_ref, lse_ref, *, per_lead):
    def sub(ref, rr, i):
        if ref.ndim == 2:
            return ref[i * A_BLOCK:(i + 1) * A_BLOCK, :]
        run = ref.shape[3] // per_lead
        return ref[i // per_lead, :, rr, (i % per_lead) * run:(i % per_lead + 1) * run, :]

    def put(ref, rr, i, val):
        if ref.ndim == 2:
            ref[i * A_BLOCK:(i + 1) * A_BLOCK, :] = val.astype(ref.dtype)
        else:
            run = ref.shape[3] // per_lead
            ref[i // per_lead, :, rr, (i % per_lead) * run:(i % per_lead + 1) * run, :] = (
                val.astype(ref.dtype).reshape(ref.shape[1], run, ref.shape[4]))

    flat = lambda v: v.reshape(A_BLOCK, v.shape[-1])
    token_order = q_ref.ndim == 2
    n_blocks = q_ref.shape[0] // A_BLOCK if token_order else q_ref.shape[0] * per_lead
    for rr in range(1 if token_order else q_ref.shape[2]):
        k_prev = flat(kp_ref[...] if token_order else kp_ref[0, :, rr])
        v_prev = flat(vp_ref[...] if token_order else vp_ref[0, :, rr])
        for i in range(n_blocks):
            k_cur, v_cur = flat(sub(kc_ref, rr, i)), flat(sub(vc_ref, rr, i))
            o, lse_tile = _attn_block(flat(sub(q_ref, rr, i)), jnp.concatenate([k_prev, k_cur], axis=0),
                                      jnp.concatenate([v_prev, v_cur], axis=0),
                                      bias0_ref if i == 0 else bias_ref)
            put(o_ref, rr, i, o)
            put(lse_ref, rr, i, lse_tile)
            k_prev, v_prev = k_cur, v_cur


def _dilated_attention_one(q, k, v, bias_tab, bsz, seq, dil):
    nb = seq // dil // A_BLOCK
    step = min(ATTN_BLOCKS_PER_STEP, nb)
    w = A_WIDTH
    if dil == 1:
        per_lead = n_res = 1
        view = lambda a: a.reshape(bsz, seq, a.shape[-1])
        block = lambda wd: (None, step * A_BLOCK, wd)
        block_prev = lambda wd: (None, A_BLOCK, wd)
        cur = lambda b, r, m: (b, m, 0)
        prev = lambda b, r, m: (b, jnp.maximum(m * step - 1, 0), 0)
        out_shape = lambda wd, dt: jax.ShapeDtypeStruct((bsz, seq, wd), dt)
    else:
        runs = SPAN_RES // dil
        run = A_BLOCK // runs
        spans_per_seq = seq // SPAN
        per_lead = min(step, runs)
        lead = step // per_lead
        n_res = max(1, ATTN_BLOCKS_PER_STEP // step)
        assert runs % per_lead == 0 and spans_per_seq % lead == 0 and dil % n_res == 0
        view = lambda a: a.reshape(a.shape[0], runs, dil, A_BLOCK, a.shape[-1])
        block = lambda wd: (lead, runs, n_res, per_lead * run, wd)
        block_prev = lambda wd: (1, runs, n_res, run, wd)
        cur = lambda b, r, m: ((b * spans_per_seq + (m * step) // runs) // lead, 0, r, ((m * step) % runs) // per_lead, 0)

        def prev(b, r, m):
            n = jnp.maximum(m * step - 1, 0)
            return (b * spans_per_seq + n // runs, 0, r, n % runs, 0)

        out_shape = lambda wd, dt: jax.ShapeDtypeStruct((bsz * spans_per_seq, runs, dil, A_BLOCK, wd), dt)
    blk = pl.BlockSpec(block(w), cur)
    blk_prev = pl.BlockSpec(block_prev(w), prev)
    bias_block = (1, A_HEADS, A_BLOCK, 2 * A_BLOCK)
    o, lse = pl.pallas_call(
        functools.partial(_attn_kernel, per_lead=per_lead),
        out_shape=[out_shape(w, BF16), out_shape(LANES, F32)],
        grid=(bsz, dil // n_res, nb // step),
        in_specs=[blk, blk_prev, blk, blk_prev, blk,
                  pl.BlockSpec(bias_block, lambda b, r, m: (jnp.minimum(m, 1), 0, 0, 0)),
                  pl.BlockSpec(bias_block, lambda b, r, m: (1, 0, 0, 0))],
        out_specs=[pl.BlockSpec(block(w), cur), pl.BlockSpec(block(LANES), cur)],
        compiler_params=_cparams(("arbitrary", "arbitrary", "arbitrary")),
        name=f"dilated_attn_d{dil}",
    )(view(q), view(k), view(k), view(v), view(v), bias_tab, bias_tab)
    if dil == 1:
        return o.reshape(bsz * seq, w), lse.reshape(bsz * seq, LANES)
    return o.reshape(o.shape[0], SPAN_RES, A_BLOCK, w), lse.reshape(o.shape[0], SPAN_RES, A_BLOCK, LANES)


def _gelu_tanh(x):
    c = math.sqrt(2.0 / math.pi)
    return x * (0.5 * (1.0 + jnp.tanh(c * (x + 0.044715 * (x * x * x)))))


def _s5_kernel(u_ref, pm_ref, pmt_ref, bd_ref, pw_ref, pw16_ref, cd_ref, dsk_ref, gw_ref, gb_ref, o_ref,
               x_scr, st_scr, *, n_state):
    @pl.when(pl.program_id(1) == 0)
    def _():
        st_scr[...] = jnp.zeros_like(st_scr)

    for tile_idx in range(S5_TILES_PER_STEP):
        _s5_tile(tile_idx, u_ref, pm_ref, pmt_ref, bd_ref, pw_ref, pw16_ref, cd_ref, dsk_ref, gw_ref, gb_ref,
                 o_ref, x_scr, st_scr, n_state)


def _s5_tile(tile_idx, u_ref, pm_ref, pmt_ref, bd_ref, pw_ref, pw16_ref, cd_ref, dsk_ref, gw_ref, gb_ref,
             o_ref, x_scr, st_scr, n_state):
    n = n_state
    jn = S5_J
    tile_rows = slice(tile_idx * SUBLANES * jn, (tile_idx + 1) * SUBLANES * jn)

    u = u_ref[tile_rows, :]
    u_perm = jnp.dot(pm_ref[...], u.astype(BF16), preferred_element_type=F32).astype(BF16)
    cw = u.shape[1] * S5_LW // n
    chunks = [(slice(c0, c0 + S5_LW), slice(n + c0, n + c0 + S5_LW)) for c0 in range(0, n, S5_LW)]

    def project_in(q):
        bu = jnp.dot(u_perm[:, q * cw:(q + 1) * cw], bd_ref[q], preferred_element_type=F32)
        x_scr[:, chunks[q][0]] = bu[:, 0:S5_LW]
        x_scr[:, chunks[q][1]] = bu[:, S5_LW:]

    project_in(0)
    end_r, end_i = [], []
    for q, (re, im) in enumerate(chunks):
        if q + 1 < len(chunks):
            project_in(q + 1)
        a1r, a1i = pw_ref[0:SUBLANES, re], pw_ref[0:SUBLANES, im]
        xr = jnp.zeros((SUBLANES, S5_LW), F32)
        xi = jnp.zeros((SUBLANES, S5_LW), F32)
        for j in range(jn):
            rows = slice(SUBLANES * j, SUBLANES * (j + 1))
            nr = a1r * xr - a1i * xi + x_scr[rows, re]
            ni = a1r * xi + a1i * xr + x_scr[rows, im]
            xr, xi = nr, ni
            x_scr[rows, re] = xr
            x_scr[rows, im] = xi
        end_r.append(xr)
        end_i.append(xi)
    er, ei = jnp.concatenate(end_r, axis=1), jnp.concatenate(end_i, axis=1)
    last = SUBLANES * (jn - 1)
    ajr, aji = pw_ref[last:last + 1, 0:n], pw_ref[last:last + 1, n:]
    pr, pi = st_scr[0:1, 0:n], st_scr[0:1, n:]
    cin_r, cin_i = [], []
    for s in range(SUBLANES):
        cin_r.append(pr)
        cin_i.append(pi)
        nr = er[s:s + 1] + ajr * pr - aji * pi
        ni = ei[s:s + 1] + ajr * pi + aji * pr
        pr, pi = nr, ni
    st_scr[0:1, 0:n] = pr
    st_scr[0:1, n:] = pi
    cr_all = jnp.concatenate(cin_r, axis=0)
    ci_all = jnp.concatenate(cin_i, axis=0)
    pack = 2 * SUBLANES
    ys = []
    for q, (re, im) in enumerate(chunks):
        cr = jnp.concatenate([cr_all[:, re]] * 2, axis=0).astype(BF16)
        ci = jnp.concatenate([ci_all[:, re]] * 2, axis=0).astype(BF16)
        xr_parts, xi_parts = [], []
        for m in range(SUBLANES * jn // pack):
            rows = slice(pack * m, pack * (m + 1))
            pjr, pji = pw16_ref[rows, re], pw16_ref[rows, im]
            xr_parts.append(x_scr[rows, re].astype(BF16) + (pjr * cr - pji * ci))
            xi_parts.append(x_scr[rows, im].astype(BF16) + (pjr * ci + pji * cr))
        xcat = jnp.concatenate([jnp.concatenate(xr_parts, axis=0), jnp.concatenate(xi_parts, axis=0)], axis=1)
        ys.append(jnp.dot(xcat, cd_ref[q], preferred_element_type=F32))

    y_perm = jnp.concatenate(ys, axis=1)
    y_hi = y_perm.astype(BF16)
    y_lo = (y_perm - y_hi.astype(F32)).astype(BF16)
    y = (jnp.dot(pmt_ref[...], y_hi, preferred_element_type=F32)
         + jnp.dot(pmt_ref[...], y_lo, preferred_element_type=F32))
    y = _gelu_tanh(y + dsk_ref[...] * u)
    z = jnp.dot(y.astype(BF16), gw_ref[...], preferred_element_type=F32) + gb_ref[...]
    o_ref[tile_rows, :] = (y * _sigmoid(z)).astype(o_ref.dtype)


def _s5_tables(a_re, a_im, log_dt, b_re, b_im, c_re, c_im):
    g, p = a_re.shape
    dt = jnp.exp(log_dt.astype(F32))[:, None]
    ar, ai = a_re.astype(F32), a_im.astype(F32)
    mag = jnp.exp(dt * ar)
    abar_r, abar_i = mag * jnp.cos(dt * ai), mag * jnp.sin(dt * ai)
    den = ar * ar + ai * ai
    fr = ((abar_r - 1.0) * ar + abar_i * ai) / den
    fi = (abar_i * ar - (abar_r - 1.0) * ai) / den
    br, bi = b_re.astype(F32), b_im.astype(F32)
    bbar_r = fr[..., None] * br - fi[..., None] * bi
    bbar_i = fr[..., None] * bi + fi[..., None] * br
    m = br.shape[-1]
    gc = S5_LW // p
    nq = g // gc
    eye = jnp.eye(gc, dtype=F32)
    dense_b = lambda t: jnp.einsum('qgpm,gh->qgmhp', t.reshape(nq, gc, p, m), eye).reshape(nq, gc * m, gc * p)
    bd = jnp.concatenate([dense_b(bbar_r), dense_b(bbar_i)], axis=2)
    dense_c = lambda t: jnp.einsum('qgmp,gh->qgphm', t.reshape(nq, gc, m, p), eye).reshape(nq, gc * p, gc * m)
    cd = jnp.concatenate([dense_c(c_re.astype(F32)), -dense_c(c_im.astype(F32))], axis=1)
    kk = jnp.arange(1, S5_J + 1, dtype=F32)[:, None, None]
    pmag = jnp.exp(kk * (dt * ar)[None])
    pw_r = (pmag * jnp.cos(kk * (dt * ai)[None])).reshape(S5_J, g * p)
    pw_i = (pmag * jnp.sin(kk * (dt * ai)[None])).reshape(S5_J, g * p)
    pw = jnp.repeat(jnp.concatenate([pw_r, pw_i], axis=1), SUBLANES, axis=0)
    return bd.astype(BF16), pw, cd.astype(BF16)


def _s5_layer(u, bsz, seq, a_re, a_im, log_dt, b_re, b_im, c_re, c_im, d_skip, glu_w, glu_b):
    t, width = u.shape
    n = a_re.shape[0] * a_re.shape[1]
    bd, pw, cd = _s5_tables(a_re, a_im, log_dt, b_re, b_im, c_re, c_im)
    tile = SUBLANES * S5_J
    step_rows = S5_TILES_PER_STEP * tile
    steps_per_seq = seq // step_rows
    const = lambda b, i: (0, 0)
    src = (np.arange(tile) % SUBLANES) * S5_J + np.arange(tile) // SUBLANES
    perm = (src[:, None] == np.arange(tile)[None, :]).astype(np.float32)
    return pl.pallas_call(
        functools.partial(_s5_kernel, n_state=n),
        out_shape=jax.ShapeDtypeStruct((t, width), BF16),
        grid=(bsz, steps_per_seq),
        in_specs=[pl.BlockSpec((step_rows, width), lambda b, i: (b * steps_per_seq + i, 0)),
                  pl.BlockSpec((tile, tile), const),
                  pl.BlockSpec((tile, tile), const),
                  pl.BlockSpec(bd.shape, lambda b, i: (0, 0, 0)),
                  pl.BlockSpec((tile, 2 * n), const),
                  pl.BlockSpec((tile, 2 * n), const),
                  pl.BlockSpec(cd.shape, lambda b, i: (0, 0, 0)),
                  pl.BlockSpec((1, width), const),
                  pl.BlockSpec((width, width), const),
                  pl.BlockSpec((1, width), const)],
        out_specs=pl.BlockSpec((step_rows, width), lambda b, i: (b * steps_per_seq + i, 0)),
        scratch_shapes=[pltpu.VMEM((tile, 2 * n), F32), pltpu.VMEM((SUBLANES, 2 * n), F32)],
        compiler_params=_cparams(("arbitrary", "arbitrary")),
        name="s5_layer",
    )(u, jnp.asarray(perm, BF16), jnp.asarray(perm.T, BF16), bd, pw, pw.astype(BF16), cd,
      d_skip.reshape(1, width).astype(F32), glu_w.astype(BF16),
      glu_b.reshape(1, width).astype(F32))


def _finish(y, x_ref, mod_ref, pg_ref, out_ref, d_model):
    ms = jnp.mean(y * y, axis=-1, keepdims=True)
    yn = y * lax.rsqrt(ms + EPS) * pg_ref[...]
    gate_mod = mod_ref[0, :, 2 * d_model:3 * d_model]
    out_ref[...] = x_ref[...] + gate_mod * yn


def _split2(x):
    hi = x.astype(BF16)
    return hi, (x - hi.astype(F32)).astype(BF16)


def _out_ab_kernel(o0_ref, l0_ref, o1_ref, o2_ref, l1_ref, l2_ref, ob_ref, gate_ref, e_ref, un_ref, w_ref,
                   x_ref, mod_ref, pg_ref, out_ref, *, d_model):
    flat = lambda ref: ref[...].reshape(ROW_TILE, ref.shape[-1])
    aw = o0_ref.shape[-1]

    def unperm(o_ref, l_ref):
        parts = jnp.concatenate([flat(o_ref)] + list(_split2(flat(l_ref))), axis=1)
        moved = jnp.dot(un_ref[...], parts, preferred_element_type=F32)
        return moved[:, 0:aw], moved[:, aw:aw + LANES] + moved[:, aw + LANES:]

    o0, l0 = o0_ref[...].astype(F32), l0_ref[...]
    o1, l1 = unperm(o1_ref, l1_ref)
    o2, l2 = unperm(o2_ref, l2_ref)
    mx = jnp.maximum(jnp.maximum(l0, l1), l2)
    e0, e1, e2 = jnp.exp2(l0 - mx), jnp.exp2(l1 - mx), jnp.exp2(l2 - mx)
    inv_den = 1.0 / (e0 + e1 + e2)
    stacked = jnp.concatenate(list(_split2(e1 * inv_den)) + list(_split2(e2 * inv_den)), axis=0)
    wide = jnp.dot(stacked, e_ref[...], preferred_element_type=F32)
    term = lambda i: wide[i * ROW_TILE:(i + 1) * ROW_TILE]
    w1 = term(0) + term(1)
    w2 = term(2) + term(3)
    o_a = (1.0 - w1 - w2) * o0 + w1 * o1 + w2 * o2
    gate = gate_ref[...].astype(F32)
    sg = gate * _sigmoid(gate)
    aw = o_a.shape[-1]
    y = jnp.dot((o_a * sg[:, :aw]).astype(BF16), w_ref[0:aw, :], preferred_element_type=F32)
    y = y + jnp.dot((ob_ref[...].astype(F32) * sg[:, aw:]).astype(BF16), w_ref[aw:, :],
                    preferred_element_type=F32)
    _finish(y, x_ref, mod_ref, pg_ref, out_ref, d_model)


def _out_c_kernel(o_ref, gate_ref, w_ref, x_ref, mod_ref, pg_ref, out_ref, *, d_model):
    gate = gate_ref[...].astype(F32)
    o = o_ref[...].astype(F32) * (gate * _sigmoid(gate))
    y = jnp.dot(o.astype(BF16), w_ref[...], preferred_element_type=F32)
    _finish(y, x_ref, mod_ref, pg_ref, out_ref, d_model)


def _out_proj(kern, row_inputs, const_inputs, w_bf16, x2, mod3, post_g, seq, name):
    t, d = x2.shape
    tiles_per_seq = seq // ROW_TILE
    tiles_per_span = SPAN // ROW_TILE

    def row_spec(a):
        if a.ndim == 2:
            return pl.BlockSpec((ROW_TILE, a.shape[1]), lambda i: (i, 0))
        return pl.BlockSpec((None, SPAN_RES, SPAN_RUN, a.shape[-1]),
                            lambda i: (i // tiles_per_span, 0, i % tiles_per_span, 0))

    const_spec = lambda a: pl.BlockSpec(a.shape, lambda i: (0, 0))
    return pl.pallas_call(
        functools.partial(kern, d_model=d),
        out_shape=jax.ShapeDtypeStruct((t, d), F32),
        grid=(t // ROW_TILE,),
        in_specs=([row_spec(a) for a in row_inputs] + [const_spec(a) for a in const_inputs]
                  + [const_spec(w_bf16), row_spec(x2),
                     pl.BlockSpec((1, 1, 3 * d), lambda i: (i // tiles_per_seq, 0, 0)),
                     pl.BlockSpec((1, d), lambda i: (0, 0))]),
        out_specs=pl.BlockSpec((ROW_TILE, d), lambda i: (i, 0)),
        compiler_params=_cparams(("arbitrary",)),
        name=name,
    )(*row_inputs, *const_inputs, w_bf16, x2, mod3, post_g.reshape(1, d))


def _gdn_prep_kernel(x_ref, halo_ref, br_ref, cw_ref, alog_ref, dtb_ref,
                     w_ref, u_ref, qg_ref, kdt_ref, aqk_ref, dec_ref, xs_scr):
    n_all = GDN_PREP_BLOCKS * GDN_STEP
    first = pl.program_id(1) == 0
    halo = halo_ref[...]
    xs_scr[0:SUBLANES, :] = jnp.where(first, jnp.zeros_like(halo), halo)
    xs_scr[SUBLANES:SUBLANES + n_all, :] = x_ref[...]
    cw = cw_ref[...]
    conv = xs_scr[SUBLANES:SUBLANES + n_all, :] * cw[C_CONV - 1:C_CONV]
    for j in range(C_CONV - 1):
        off = SUBLANES - (C_CONV - 1) + j
        conv = conv + xs_scr[off:off + n_all, :] * cw[j:j + 1]
    act = conv * _sigmoid(conv)
    for blk in range(GDN_PREP_BLOCKS):
        _gdn_prep_block(blk, act, br_ref, alog_ref, dtb_ref, w_ref, u_ref, qg_ref, kdt_ref, aqk_ref, dec_ref)


def _gdn_prep_block(blk, act, br_ref, alog_ref, dtb_ref, w_ref, u_ref, qg_ref, kdt_ref, aqk_ref, dec_ref):
    n = GDN_STEP
    rows = slice(blk * n, (blk + 1) * n)
    head_cols = lambda base, h: act[rows, base + h * C_DK:base + (h + 1) * C_DK]

    row = lax.broadcasted_iota(jnp.int32, (n, n), 0)
    col = lax.broadcasted_iota(jnp.int32, (n, n), 1)
    same = (row // GDN_CHUNK) == (col // GDN_CHUNK)
    tril = jnp.logical_and(same, row >= col)
    strict = jnp.logical_and(same, row > col)
    eye = (row == col).astype(F32)

    br = br_ref[rows, :]
    beta_all = _sigmoid(br)
    xg = br + dtb_ref[...]
    softplus = jnp.maximum(xg, 0.0) + jnp.log(1.0 + jnp.exp(-jnp.abs(xg)))
    g_all = -jnp.exp(alog_ref[...]) * softplus
    gc_all = jnp.dot(tril.astype(F32), g_all, preferred_element_type=F32, precision=HI)
    gc_t = gc_all.T
    first_chunk = lax.broadcasted_iota(jnp.int32, (n, 1), 0) < GDN_CHUNK
    dec_row = lax.broadcasted_iota(jnp.int32, (SUBLANES, LANES), 0)

    heads = range(C_HEADS)
    lanes = [slice(h * LANES, (h + 1) * LANES) for h in heads]
    gcols = [gc_all[:, C_HEADS + h:C_HEADS + h + 1] for h in heads]
    egs = [jnp.exp(g) for g in gcols]
    kbs, amats, invs = [], [], []
    for h in heads:
        grow = gc_t[C_HEADS + h:C_HEADS + h + 1, :]
        decay = jnp.where(tril, jnp.exp(jnp.where(tril, gcols[h] - grow, 0.0)), 0.0)
        q = head_cols(0, h)
        k = head_cols(C_HEADS * C_DK, h)
        q = q * lax.rsqrt(jnp.sum(q * q, axis=-1, keepdims=True) + EPS) * (C_DK ** -0.5)
        k = k * lax.rsqrt(jnp.sum(k * k, axis=-1, keepdims=True) + EPS)
        kb = k * beta_all[:, h:h + 1]
        k16 = k.astype(BF16)
        a = jnp.where(strict, lax.dot_general(kb.astype(BF16), k16, _NT, preferred_element_type=F32) * decay, 0.0)
        aqk = lax.dot_general(q.astype(BF16), k16, _NT, preferred_element_type=F32) * decay
        aqk_ref[rows, lanes[h]] = aqk.astype(BF16)
        qg_ref[rows, lanes[h]] = (q * egs[h]).astype(BF16)
        g_last = jnp.where(first_chunk, gcols[h][GDN_CHUNK - 1:GDN_CHUNK], gcols[h][n - 1:n])
        kdt_ref[rows, lanes[h]] = (k * jnp.exp(g_last - gcols[h])).T.astype(BF16)
        dec_ref[blk, :, lanes[h]] = jnp.where(dec_row < SUBLANES // 2, egs[h][GDN_CHUNK - 1:GDN_CHUNK],
                                             egs[h][n - 1:n])
        kbs.append(kb)
        amats.append(a)
        invs.append(eye - a)
    pows = []
    for h in heads:
        a16 = amats[h].astype(BF16)
        pows.append(jnp.dot(a16, a16, preferred_element_type=F32))
    for _ in range(4):
        for h in heads:
            p16 = pows[h].astype(BF16)
            both = jnp.dot(jnp.concatenate([invs[h].astype(BF16), p16], axis=0), p16, preferred_element_type=F32)
            invs[h] = invs[h] + both[0:n]
            pows[h] = both[n:]
    for h in heads:
        invs[h] = invs[h] + jnp.dot(invs[h].astype(BF16), pows[h].astype(BF16), preferred_element_type=F32)
    for h in heads:
        v = head_cols(2 * C_HEADS * C_DK, h)
        rhs = jnp.concatenate([v * beta_all[:, h:h + 1], kbs[h] * egs[h]], axis=1)
        uw = jnp.dot(invs[h].astype(BF16), rhs.astype(BF16), preferred_element_type=F32)
        u_ref[rows, lanes[h]] = uw[:, 0:C_DV]
        w_ref[rows, lanes[h]] = uw[:, C_DV:].astype(BF16)


def _gdn_rec_kernel(w_ref, u_ref, qg_ref, kdt_ref, aqk_ref, dec_ref, ng_ref, o_ref, s_scr):
    @pl.when(pl.program_id(1) == 0)
    def _():
        s_scr[...] = jnp.zeros_like(s_scr)

    half = SUBLANES // 2
    zeros = jnp.zeros((GDN_CHUNK, C_DV), BF16)
    heads = range(C_HEADS)
    lanes = [slice(h * LANES, (h + 1) * LANES) for h in heads]
    states = [s_scr[h] for h in heads]
    for blk in range(GDN_REC_BLOCKS):
        blk_rows = slice(blk * GDN_STEP, (blk + 1) * GDN_STEP)
        for j in range(GDN_STEP // GDN_CHUNK):
            rows = slice(blk * GDN_STEP + j * GDN_CHUNK, blk * GDN_STEP + (j + 1) * GDN_CHUNK)
            wss = []
            for h in heads:
                wq = jnp.concatenate([w_ref[rows, lanes[h]], qg_ref[rows, lanes[h]]], axis=0)
                wss.append(jnp.dot(wq, states[h].astype(BF16), preferred_element_type=F32))
            for h in heads:
                v_new = (u_ref[rows, lanes[h]] - wss[h][0:GDN_CHUNK]).astype(BF16)
                v_pad = jnp.concatenate([v_new, zeros] if j == 0 else [zeros, v_new], axis=0)
                o = wss[h][GDN_CHUNK:] + jnp.dot(aqk_ref[rows, lanes[h]], v_pad, preferred_element_type=F32)
                dec = dec_ref[blk, j * half:j * half + 1, lanes[h]]
                states[h] = states[h] * dec + jnp.dot(kdt_ref[blk_rows, lanes[h]], v_pad,
                                                      preferred_element_type=F32)
                ms = jnp.mean(o * o, axis=-1, keepdims=True)
                o_ref[rows, lanes[h]] = (o * lax.rsqrt(ms + EPS) * ng_ref[...]).astype(o_ref.dtype)
    for h in heads:
        s_scr[h] = states[h]


def _gdn_core(qkv_pre, br, bsz, seq, conv_w, a_log, dt_bias, norm_g):
    t, qkv_w = qkv_pre.shape
    hw = C_HEADS * C_DV
    steps_per_seq = seq // GDN_STEP
    prep_rows = GDN_PREP_BLOCKS * GDN_STEP
    prep_steps = steps_per_seq // GDN_PREP_BLOCKS
    halo_blocks = prep_rows // SUBLANES
    pad_row = lambda vec: jnp.zeros((1, LANES), F32).at[0, C_HEADS:2 * C_HEADS].set(vec.astype(F32))
    tok = lambda b, i: (b * prep_steps + i, 0)
    const = lambda b, i: (0, 0)
    tok_spec = pl.BlockSpec((prep_rows, hw), tok)
    dec_spec = pl.BlockSpec((GDN_PREP_BLOCKS, SUBLANES, hw), lambda b, i: (b * prep_steps + i, 0, 0))
    w, u, qg, kdt, aqk, dec = pl.pallas_call(
        _gdn_prep_kernel,
        out_shape=[jax.ShapeDtypeStruct((t, hw), BF16), jax.ShapeDtypeStruct((t, hw), F32),
                   jax.ShapeDtypeStruct((t, hw), BF16), jax.ShapeDtypeStruct((t, hw), BF16),
                   jax.ShapeDtypeStruct((t, hw), BF16),
                   jax.ShapeDtypeStruct((t // GDN_STEP, SUBLANES, hw), F32)],
        grid=(bsz, prep_steps),
        in_specs=[pl.BlockSpec((prep_rows, qkv_w), tok),
                  pl.BlockSpec((SUBLANES, qkv_w),
                               lambda b, i: (jnp.maximum((b * prep_steps + i) * halo_blocks - 1, 0), 0)),
                  pl.BlockSpec((prep_rows, LANES), tok),
                  pl.BlockSpec((C_CONV, qkv_w), const),
                  pl.BlockSpec((1, LANES), const),
                  pl.BlockSpec((1, LANES), const)],
        out_specs=[tok_spec, tok_spec, tok_spec, tok_spec, tok_spec, dec_spec],
        scratch_shapes=[pltpu.VMEM((SUBLANES + prep_rows, qkv_w), F32)],
        compiler_params=_cparams(("arbitrary", "arbitrary")),
        name="gdn_prep",
    )(qkv_pre, qkv_pre, br, conv_w.astype(F32), pad_row(a_log), pad_row(dt_bias))
    rec_steps = steps_per_seq // GDN_REC_BLOCKS
    rec_tok = pl.BlockSpec((GDN_REC_BLOCKS * GDN_STEP, hw), lambda b, i: (b * rec_steps + i, 0))
    rec_dec = pl.BlockSpec((GDN_REC_BLOCKS, SUBLANES, hw), lambda b, i: (b * rec_steps + i, 0, 0))
    return pl.pallas_call(
        _gdn_rec_kernel,
        out_shape=jax.ShapeDtypeStruct((t, hw), BF16),
        grid=(bsz, rec_steps),
        in_specs=[rec_tok, rec_tok, rec_tok, rec_tok, rec_tok, rec_dec,
                  pl.BlockSpec((1, C_DV), const)],
        out_specs=rec_tok,
        scratch_shapes=[pltpu.VMEM((C_HEADS, C_DK, C_DV), F32)],
        compiler_params=_cparams(("arbitrary", "arbitrary")),
        name="gdn_recurrence",
    )(w, u, qg, kdt, aqk, dec, norm_g.reshape(1, C_DV).astype(F32))


def _ab_layer(x2, mod3, bsz, seq, pre_g, post_g, rel_bias, w_in, w_out, s5_params):
    b_width = s5_params[7].shape[-1]
    assert DILATED_CONFIGS[0][1] == 1 and all(SPAN_RES % dl == 0 for _, dl in DILATED_CONFIGS) and seq % SPAN == 0
    head_expand = jnp.asarray(np.arange(LANES)[:, None] == (np.arange(A_WIDTH)[None, :] // A_HEAD_DIM), BF16)
    unperm = jnp.asarray(_span_perm().T, BF16)
    splits = ((0, A_WIDTH, A_HEAD_DIM ** -0.5 * LOG2_E), (A_WIDTH, A_WIDTH, 1.0), (2 * A_WIDTH, A_WIDTH, 1.0),
              (3 * A_WIDTH, b_width, 1.0), (3 * A_WIDTH + b_width, A_WIDTH + b_width, 1.0))
    q, k, v, u, gate, qs, ks, vs = _in_proj(x2, mod3, pre_g, w_in.astype(BF16), splits,
                                            (BF16, BF16, BF16, F32, BF16), seq, n_span=3)
    outs, lses = [], []
    bias_tabs = _attn_bias_tables(rel_bias)
    for cfg, (_, dil) in enumerate(DILATED_CONFIGS):
        qkv = (q, k, v) if dil == 1 else (qs, ks, vs)
        o_c, lse_c = _dilated_attention_one(*qkv, bias_tabs[cfg], bsz, seq, dil)
        outs.append(o_c)
        lses.append(lse_c)
    o_b = _s5_layer(u, bsz, seq, *s5_params)
    row_inputs = [outs[0], lses[0], outs[1], outs[2], lses[1], lses[2], o_b, gate]
    return _out_proj(_out_ab_kernel, row_inputs, [head_expand, unperm], w_out.astype(BF16),
                     x2, mod3, post_g, seq, "out_proj_ab")


def _gdn_layer(x2, mod3, bsz, seq, pre_g, post_g, w_in, conv_w, a_log, dt_bias, norm_g, w_out):
    d = x2.shape[1]
    qkv_w = 2 * C_HEADS * C_DK + C_HEADS * C_DV
    gate_w = C_HEADS * C_DV
    w_pad = jnp.concatenate(
        [w_in, jnp.zeros((d, LANES - (w_in.shape[1] - qkv_w - gate_w)), w_in.dtype)], axis=1).astype(BF16)
    splits = ((0, qkv_w, 1.0), (qkv_w, gate_w, 1.0), (qkv_w + gate_w, LANES, 1.0))
    qkv_pre, gate, br = _in_proj(x2, mod3, pre_g, w_pad, splits, (F32, BF16, F32), seq)
    o = _gdn_core(qkv_pre, br, bsz, seq, conv_w, a_log, dt_bias, norm_g)
    return _out_proj(_out_c_kernel, [o, gate], [], w_out.astype(BF16), x2, mod3, post_g, seq, "out_proj_c")


def kernel(x, c, ada_w, ada_b, pre_g, post_g, rel_bias, ab_w_in, ab_w_out, s5_a_re, s5_a_im, s5_log_dt, s5_b_re, s5_b_im, s5_c_re, s5_c_im, s5_d, s5_glu_w, s5_glu_b, gdn_w_in, gdn_conv, gdn_a_log, gdn_dt_bias, gdn_norm_g, gdn_w_out):
    bsz, seq, d = x.shape
    depth = ada_w.shape[0]
    assert seq % ROW_TILE == 0 and seq % (SUBLANES * S5_J * S5_TILES_PER_STEP) == 0
    assert seq % (GDN_STEP * GDN_PREP_BLOCKS) == 0 and seq % (GDN_STEP * GDN_REC_BLOCKS) == 0
    x2 = x.reshape(bsz * seq, d)
    mod = _adaln_mod(c, ada_w, ada_b)
    for layer in range(depth):
        j = layer // 2
        mod3 = mod[layer].reshape(bsz, 1, 3 * d)
        if layer % 2 == 0:
            s5_params = (s5_a_re[j], s5_a_im[j], s5_log_dt[j], s5_b_re[j], s5_b_im[j], s5_c_re[j], s5_c_im[j],
                         s5_d[j], s5_glu_w[j], s5_glu_b[j])
            x2 = _ab_layer(x2, mod3, bsz, seq, pre_g[layer], post_g[layer], rel_bias, ab_w_in[j], ab_w_out[j],
                           s5_params)
        else:
            x2 = _gdn_layer(x2, mod3, bsz, seq, pre_g[layer], post_g[layer], gdn_w_in[j], gdn_conv[j],
                            gdn_a_log[j], gdn_dt_bias[j], gdn_norm_g[j], gdn_w_out[j])
    return x2.reshape(bsz, seq, d)
```

```python
import functools
import math

import numpy as np
import jax
import jax.numpy as jnp
from jax import lax
from jax.experimental import pallas as pl
from jax.experimental.pallas import tpu as pltpu

F32 = jnp.float32
BF16 = jnp.bfloat16
HI = lax.Precision.HIGHEST

EPS = 1e-6
A_HEADS = 8
A_HEAD_DIM = 64
A_WIDTH = A_HEADS * A_HEAD_DIM
A_BLOCK = 128
DILATED_CONFIGS = ((128, 1), (512, 4), (2048, 16))
B_GROUP = 16
B_STATE = 64
REL_BUCKETS = 32
REL_MAX_DIST = 2048
C_HEADS = 8
C_DK = 128
C_DV = 128
C_CONV = 4
MASK_NEG = -1e30
LOG2_E = math.log2(math.e)

LANES = 128
SUBLANES = 8
VMEM_LIMIT = 48 * 1024 * 1024
ROW_TILE = 512
COL_CHUNK = 512
SPAN_RES = max(dl for _, dl in DILATED_CONFIGS)
SPAN = A_BLOCK * SPAN_RES
SPAN_RUN = ROW_TILE // SPAN_RES
ATTN_BLOCKS_PER_STEP = 4
S5_J = 32
S5_LW = 512
S5_TILES_PER_STEP = 2
GDN_STEP = 128
GDN_CHUNK = 64
GDN_PREP_BLOCKS = 2
GDN_REC_BLOCKS = 2

_NT = (((1,), (1,)), ((), ()))


def _cparams(sem):
    return pltpu.CompilerParams(dimension_semantics=sem, vmem_limit_bytes=VMEM_LIMIT)


def _sigmoid(x):
    return 1.0 / (1.0 + jnp.exp(-x))


def _mod_kernel(c_ref, w_ref, b_ref, o_ref):
    c = c_ref[...]
    ca = c * _sigmoid(c)
    o_ref[0] = jnp.dot(ca, w_ref[0], preferred_element_type=F32, precision=HI) + b_ref[0]


def _adaln_mod(c, ada_w, ada_b):
    depth, d, d3 = ada_w.shape
    bsz = c.shape[0]
    return pl.pallas_call(
        _mod_kernel,
        out_shape=jax.ShapeDtypeStruct((depth, bsz, d3), F32),
        grid=(depth, d3 // d),
        in_specs=[pl.BlockSpec((bsz, d), lambda l, j: (0, 0)),
                  pl.BlockSpec((1, d, d), lambda l, j: (l, 0, j)),
                  pl.BlockSpec((1, 1, d), lambda l, j: (l, 0, j))],
        out_specs=pl.BlockSpec((1, bsz, d), lambda l, j: (l, 0, j)),
        compiler_params=_cparams(("arbitrary", "arbitrary")),
        name="adaln_mod",
    )(c, ada_w, ada_b.reshape(depth, 1, d3))


def _span_perm():
    rho = np.arange(ROW_TILE)
    nat = SPAN_RES * (rho % SPAN_RUN) + rho // SPAN_RUN
    return (nat[:, None] == np.arange(ROW_TILE)[None, :]).astype(np.float32)


def _modulated_norm(x_ref, mod_ref, g_ref, d_model):
    x = x_ref[...]
    ms = jnp.mean(x * x, axis=-1, keepdims=True)
    y = x * lax.rsqrt(ms + EPS) * g_ref[...]
    shift = mod_ref[0, :, 0:d_model]
    scale = mod_ref[0, :, d_model:2 * d_model]
    return (y * (1.0 + scale) + shift).astype(BF16)


def _in_proj_kernel(x_ref, mod_ref, g_ref, w_ref, *rest, splits, d_model, n_span):
    pm_ref = rest[0] if n_span else None
    out_refs = rest[1:] if n_span else rest
    span_refs = out_refs[len(splits):]
    h = _modulated_norm(x_ref, mod_ref, g_ref, d_model)
    for idx, ((c0, width, mult), o_ref) in enumerate(zip(splits, out_refs)):
        for cc in range(0, width, COL_CHUNK):
            cw = min(COL_CHUNK, width - cc)
            acc = jnp.dot(h, w_ref[:, c0 + cc:c0 + cc + cw], preferred_element_type=F32)
            if mult != 1.0:
                acc = acc * mult
            val = acc.astype(o_ref.dtype)
            o_ref[:, cc:cc + cw] = val
            if idx < n_span:
                moved = jnp.dot(pm_ref[...], val, preferred_element_type=F32).astype(BF16)
                span_refs[idx][:, :, cc:cc + cw] = moved.reshape(SPAN_RES, SPAN_RUN, cw)


def _in_proj(x2, mod3, gain, w_bf16, splits, out_dtypes, seq, n_span=0):
    t, d = x2.shape
    tiles_per_seq = seq // ROW_TILE
    tiles_per_span = SPAN // ROW_TILE
    n_w = w_bf16.shape[1]
    row = lambda wd: pl.BlockSpec((ROW_TILE, wd), lambda i: (i, 0))
    span_shape = lambda wd: jax.ShapeDtypeStruct((t // SPAN, SPAN_RES, A_BLOCK, wd), BF16)
    span_spec = lambda wd: pl.BlockSpec((None, SPAN_RES, SPAN_RUN, wd),
                                        lambda i: (i // tiles_per_span, 0, i % tiles_per_span, 0))
    perm_in = [jnp.asarray(_span_perm(), BF16)] if n_span else []
    perm_spec = [pl.BlockSpec((ROW_TILE, ROW_TILE), lambda i: (0, 0))] if n_span else []
    return pl.pallas_call(
        functools.partial(_in_proj_kernel, splits=splits, d_model=d, n_span=n_span),
        out_shape=([jax.ShapeDtypeStruct((t, wd), dt) for (_, wd, _), dt in zip(splits, out_dtypes)]
                   + [span_shape(wd) for (_, wd, _) in splits[:n_span]]),
        grid=(t // ROW_TILE,),
        in_specs=[row(d),
                  pl.BlockSpec((1, 1, 3 * d), lambda i: (i // tiles_per_seq, 0, 0)),
                  pl.BlockSpec((1, d), lambda i: (0, 0)),
                  pl.BlockSpec((d, n_w), lambda i: (0, 0))] + perm_spec,
        out_specs=[row(wd) for (_, wd, _) in splits] + [span_spec(wd) for (_, wd, _) in splits[:n_span]],
        compiler_params=_cparams(("arbitrary",)),
        name="in_proj",
    )(x2, mod3, gain.reshape(1, d), w_bf16, *perm_in)


def _t5_bucket(dist):
    dist = np.maximum(dist, 0)
    max_exact = REL_BUCKETS // 2
    large = max_exact + (np.log(np.maximum(dist, 1) / max_exact)
                         / math.log(REL_MAX_DIST / max_exact) * (REL_BUCKETS - max_exact)).astype(np.int32)
    large = np.minimum(large, REL_BUCKETS - 1)
    return np.where(dist < max_exact, dist, large).astype(np.int32)


def _bias_kernel(rb_ref, bucket_ref, mask_ref, o_ref):
    def body(r, carry):
        rows = pl.ds(pl.multiple_of(r * SUBLANES, SUBLANES), SUBLANES)
        bk = bucket_ref[0, rows, :]
        accs = [jnp.zeros(bk.shape, F32) for _ in range(A_HEADS)]
        for b in range(REL_BUCKETS):
            eq = bk == b
            accs = [jnp.where(eq, rb_ref[b, h], acc) for h, acc in enumerate(accs)]
        for f in range(2):
            keep = mask_ref[0, f, rows, :] != 0
            for h in range(A_HEADS):
                o_ref[0, f, h, rows, :] = jnp.where(keep, accs[h] * LOG2_E, MASK_NEG)
        return carry
    lax.fori_loop(0, A_BLOCK // SUBLANES, body, 0)


def _attn_bias_tables(rel_bias):
    qi = np.arange(A_BLOCK)[:, None]
    kj = np.arange(2 * A_BLOCK)[None, :]
    rel = qi + A_BLOCK - kj
    buckets, masks = [], []
    for window, dil in DILATED_CONFIGS:
        band = (rel >= 0) & (rel <= window // dil)
        bucket = _t5_bucket(rel * dil)
        mask = np.stack([band & (kj >= A_BLOCK), band]).astype(np.int32)
        if dil > 1:
            runs = SPAN_RES // dil
            run = A_BLOCK // runs
            rho = np.arange(A_BLOCK)
            sub = runs * (rho % run) + rho // run
            keys = np.concatenate([sub, A_BLOCK + sub])
            bucket = bucket[sub][:, keys]
            mask = mask[:, sub][:, :, keys]
        buckets.append(bucket)
        masks.append(mask)
    n_cfg = len(DILATED_CONFIGS)
    return pl.pallas_call(
        _bias_kernel,
        out_shape=jax.ShapeDtypeStruct((n_cfg, 2, A_HEADS, A_BLOCK, 2 * A_BLOCK), F32),
        grid=(n_cfg,),
        in_specs=[pl.BlockSpec(memory_space=pltpu.SMEM),
                  pl.BlockSpec((1, A_BLOCK, 2 * A_BLOCK), lambda c: (c, 0, 0)),
                  pl.BlockSpec((1, 2, A_BLOCK, 2 * A_BLOCK), lambda c: (c, 0, 0, 0))],
        out_specs=pl.BlockSpec((1, 2, A_HEADS, A_BLOCK, 2 * A_BLOCK), lambda c: (c, 0, 0, 0, 0)),
        compiler_params=_cparams(("arbitrary",)),
        name="attn_bias",
    )(rel_bias.astype(F32), jnp.asarray(np.stack(buckets)), jnp.asarray(np.stack(masks)))


def _attn_block(q, kcat, vcat, bias_ref):
    lane = lax.broadcasted_iota(jnp.int32, (A_BLOCK, LANES), 1)
    low = lane < A_HEAD_DIM
    lse_tile = jnp.zeros((A_BLOCK, LANES), F32)
    zero = jnp.zeros((A_BLOCK, LANES), BF16)
    heads = range(A_HEADS)
    pair = [slice((h // 2) * LANES, (h // 2 + 1) * LANES) for h in heads]
    scores = []
    for h in heads:
        qm = jnp.where(low if h % 2 == 0 else jnp.logical_not(low), q[:, pair[h]], zero)
        scores.append(lax.dot_general(qm, kcat[:, pair[h]], _NT, preferred_element_type=F32) + bias_ref[0, h])
    probs, inv_l = [], []
    for h in heads:
        m = jnp.max(scores[h], axis=-1, keepdims=True)
        p = jnp.exp2(scores[h] - m)
        l = jnp.sum(p, axis=-1, keepdims=True)
        probs.append(p.astype(BF16))
        inv_l.append(1.0 / l)
        lse_tile = jnp.where(lane == h, m + jnp.log2(l), lse_tile)
    outs = [jnp.dot(probs[h], vcat[:, pair[h]], preferred_element_type=F32) * inv_l[h] for h in heads]
    o = jnp.concatenate([jnp.where(low, outs[2 * hp], outs[2 * hp + 1]) for hp in range(A_HEADS // 2)], axis=1)
    return o, lse_tile


def _attn_kernel(q_ref, kp_ref, kc_ref, vp_ref, vc_ref, bias0_ref, bias_ref, o_ref, lse_ref, *, per_lead):
    def sub(ref, rr, i):
        if ref.ndim == 2:
            return ref[i * A_BLOCK:(i + 1) * A_BLOCK, :]
        run = ref.shape[3] // per_lead
        return ref[i // per_lead, :, rr, (i % per_lead) * run:(i % per_lead + 1) * run, :]

    def put(ref, rr, i, val):
        if ref.ndim == 2:
            ref[i * A_BLOCK:(i + 1) * A_BLOCK, :] = val.astype(ref.dtype)
        else:
            run = ref.shape[3] // per_lead
            ref[i // per_lead, :, rr, (i % per_lead) * run:(i % per_lead + 1) * run, :] = (
                val.astype(ref.dtype).reshape(ref.shape[1], run, ref.shape[4]))

    flat = lambda v: v.reshape(A_BLOCK, v.shape[-1])
    token_order = q_ref.ndim == 2
    n_blocks = q_ref.shape[0] // A_BLOCK if token_order else q_ref.shape[0] * per_lead
    for rr in range(1 if token_order else q_ref.shape[2]):
        k_prev = flat(kp_ref[...] if token_order else kp_ref[0, :, rr])
        v_prev = flat(vp_ref[...] if token_order else vp_ref[0, :, rr])
        for i in range(n_blocks):
            k_cur, v_cur = flat(sub(kc_ref, rr, i)), flat(sub(vc_ref, rr, i))
            o, lse_tile = _attn_block(flat(sub(q_ref, rr, i)), jnp.concatenate([k_prev, k_cur], axis=0),
                                      jnp.concatenate([v_prev, v_cur], axis=0),
                                      bias0_ref if i == 0 else bias_ref)
            put(o_ref, rr, i, o)
            put(lse_ref, rr, i, lse_tile)
            k_prev, v_prev = k_cur, v_cur


def _dilated_attention_one(q, k, v, bias_tab, bsz, seq, dil):
    nb = seq // dil // A_BLOCK
    step = min(ATTN_BLOCKS_PER_STEP, nb)
    w = A_WIDTH
    if dil == 1:
        per_lead = n_res = 1
        view = lambda a: a.reshape(bsz, seq, a.shape[-1])
        block = lambda wd: (None, step * A_BLOCK, wd)
        block_prev = lambda wd: (None, A_BLOCK, wd)
        cur = lambda b, r, m: (b, m, 0)
        prev = lambda b, r, m: (b, jnp.maximum(m * step - 1, 0), 0)
        out_shape = lambda wd, dt: jax.ShapeDtypeStruct((bsz, seq, wd), dt)
    else:
        runs = SPAN_RES // dil
        run = A_BLOCK // runs
        spans_per_seq = seq // SPAN
        per_lead = min(step, runs)
        lead = step // per_lead
        n_res = max(1, ATTN_BLOCKS_PER_STEP // step)
        assert runs % per_lead == 0 and spans_per_seq % lead == 0 and dil % n_res == 0
        view = lambda a: a.reshape(a.shape[0], runs, dil, A_BLOCK, a.shape[-1])
        block = lambda wd: (lead, runs, n_res, per_lead * run, wd)
        block_prev = lambda wd: (1, runs, n_res, run, wd)
        cur = lambda b, r, m: ((b * spans_per_seq + (m * step) // runs) // lead, 0, r, ((m * step) % runs) // per_lead, 0)

        def prev(b, r, m):
            n = jnp.maximum(m * step - 1, 0)
            return (b * spans_per_seq + n // runs, 0, r, n % runs, 0)

        out_shape = lambda wd, dt: jax.ShapeDtypeStruct((bsz * spans_per_seq, runs, dil, A_BLOCK, wd), dt)
    blk = pl.BlockSpec(block(w), cur)
    blk_prev = pl.BlockSpec(block_prev(w), prev)
    bias_block = (1, A_HEADS, A_BLOCK, 2 * A_BLOCK)
    o, lse = pl.pallas_call(
        functools.partial(_attn_kernel, per_lead=per_lead),
        out_shape=[out_shape(w, BF16), out_shape(LANES, F32)],
        grid=(bsz, dil // n_res, nb // step),
        in_specs=[blk, blk_prev, blk, blk_prev, blk,
                  pl.BlockSpec(bias_block, lambda b, r, m: (jnp.minimum(m, 1), 0, 0, 0)),
                  pl.BlockSpec(bias_block, lambda b, r, m: (1, 0, 0, 0))],
        out_specs=[pl.BlockSpec(block(w), cur), pl.BlockSpec(block(LANES), cur)],
        compiler_params=_cparams(("arbitrary", "arbitrary", "arbitrary")),
        name=f"dilated_attn_d{dil}",
    )(view(q), view(k), view(k), view(v), view(v), bias_tab, bias_tab)
    if dil == 1:
        return o.reshape(bsz * seq, w), lse.reshape(bsz * seq, LANES)
    return o.reshape(o.shape[0], SPAN_RES, A_BLOCK, w), lse.reshape(o.shape[0], SPAN_RES, A_BLOCK, LANES)


def _gelu_tanh(x):
    c = math.sqrt(2.0 / math.pi)
    return x * (0.5 * (1.0 + jnp.tanh(c * (x + 0.044715 * (x * x * x)))))


def _s5_kernel(u_ref, pm_ref, pmt_ref, bd_ref, pw_ref, pw16_ref, cd_ref, dsk_ref, gw_ref, gb_ref, o_ref,
               x_scr, st_scr, *, n_state):
    @pl.when(pl.program_id(1) == 0)
    def _():
        st_scr[...] = jnp.zeros_like(st_scr)

    for tile_idx in range(S5_TILES_PER_STEP):
        _s5_tile(tile_idx, u_ref, pm_ref, pmt_ref, bd_ref, pw_ref, pw16_ref, cd_ref, dsk_ref, gw_ref, gb_ref,
                 o_ref, x_scr, st_scr, n_state)


def _s5_tile(tile_idx, u_ref, pm_ref, pmt_ref, bd_ref, pw_ref, pw16_ref, cd_ref, dsk_ref, gw_ref, gb_ref,
             o_ref, x_scr, st_scr, n_state):
    n = n_state
    jn = S5_J
    tile_rows = slice(tile_idx * SUBLANES * jn, (tile_idx + 1) * SUBLANES * jn)

    u = u_ref[tile_rows, :]
    u_perm = jnp.dot(pm_ref[...], u.astype(BF16), preferred_element_type=F32).astype(BF16)
    cw = u.shape[1] * S5_LW // n
    chunks = [(slice(c0, c0 + S5_LW), slice(n + c0, n + c0 + S5_LW)) for c0 in range(0, n, S5_LW)]

    def project_in(q):
        bu = jnp.dot(u_perm[:, q * cw:(q + 1) * cw], bd_ref[q], preferred_element_type=F32)
        x_scr[:, chunks[q][0]] = bu[:, 0:S5_LW]
        x_scr[:, chunks[q][1]] = bu[:, S5_LW:]

    project_in(0)
    end_r, end_i = [], []
    for q, (re, im) in enumerate(chunks):
        if q + 1 < len(chunks):
            project_in(q + 1)
        a1r, a1i = pw_ref[0:SUBLANES, re], pw_ref[0:SUBLANES, im]
        xr = jnp.zeros((SUBLANES, S5_LW), F32)
        xi = jnp.zeros((SUBLANES, S5_LW), F32)
        for j in range(jn):
            rows = slice(SUBLANES * j, SUBLANES * (j + 1))
            nr = a1r * xr - a1i * xi + x_scr[rows, re]
            ni = a1r * xi + a1i * xr + x_scr[rows, im]
            xr, xi = nr, ni
            x_scr[rows, re] = xr
            x_scr[rows, im] = xi
        end_r.append(xr)
        end_i.append(xi)
    er, ei = jnp.concatenate(end_r, axis=1), jnp.concatenate(end_i, axis=1)
    last = SUBLANES * (jn - 1)
    ajr, aji = pw_ref[last:last + 1, 0:n], pw_ref[last:last + 1, n:]
    pr, pi = st_scr[0:1, 0:n], st_scr[0:1, n:]
    cin_r, cin_i = [], []
    for s in range(SUBLANES):
        cin_r.append(pr)
        cin_i.append(pi)
        nr = er[s:s + 1] + ajr * pr - aji * pi
        ni = ei[s:s + 1] + ajr * pi + aji * pr
        pr, pi = nr, ni
    st_scr[0:1, 0:n] = pr
    st_scr[0:1, n:] = pi
    cr_all = jnp.concatenate(cin_r, axis=0)
    ci_all = jnp.concatenate(cin_i, axis=0)
    pack = 2 * SUBLANES
    ys = []
    for q, (re, im) in enumerate(chunks):
        cr = jnp.concatenate([cr_all[:, re]] * 2, axis=0).astype(BF16)
        ci = jnp.concatenate([ci_all[:, re]] * 2, axis=0).astype(BF16)
        xr_parts, xi_parts = [], []
        for m in range(SUBLANES * jn // pack):
            rows = slice(pack * m, pack * (m + 1))
            pjr, pji = pw16_ref[rows, re], pw16_ref[rows, im]
            xr_parts.append(x_scr[rows, re].astype(BF16) + (pjr * cr - pji * ci))
            xi_parts.append(x_scr[rows, im].astype(BF16) + (pjr * ci + pji * cr))
        xcat = jnp.concatenate([jnp.concatenate(xr_parts, axis=0), jnp.concatenate(xi_parts, axis=0)], axis=1)
        ys.append(jnp.dot(xcat, cd_ref[q], preferred_element_type=F32))

    y_perm = jnp.concatenate(ys, axis=1)
    y_hi = y_perm.astype(BF16)
    y_lo = (y_perm - y_hi.astype(F32)).astype(BF16)
    y = (jnp.dot(pmt_ref[...], y_hi, preferred_element_type=F32)
         + jnp.dot(pmt_ref[...], y_lo, preferred_element_type=F32))
    y = _gelu_tanh(y + dsk_ref[...] * u)
    z = jnp.dot(y.astype(BF16), gw_ref[...], preferred_element_type=F32) + gb_ref[...]
    o_ref[tile_rows, :] = (y * _sigmoid(z)).astype(o_ref.dtype)


def _s5_tables(a_re, a_im, log_dt, b_re, b_im, c_re, c_im):
    g, p = a_re.shape
    dt = jnp.exp(log_dt.astype(F32))[:, None]
    ar, ai = a_re.astype(F32), a_im.astype(F32)
    mag = jnp.exp(dt * ar)
    abar_r, abar_i = mag * jnp.cos(dt * ai), mag * jnp.sin(dt * ai)
    den = ar * ar + ai * ai
    fr = ((abar_r - 1.0) * ar + abar_i * ai) / den
    fi = (abar_i * ar - (abar_r - 1.0) * ai) / den
    br, bi = b_re.astype(F32), b_im.astype(F32)
    bbar_r = fr[..., None] * br - fi[..., None] * bi
    bbar_i = fr[..., None] * bi + fi[..., None] * br
    m = br.shape[-1]
    gc = S5_LW // p
    nq = g // gc
    eye = jnp.eye(gc, dtype=F32)
    dense_b = lambda t: jnp.einsum('qgpm,gh->qgmhp', t.reshape(nq, gc, p, m), eye).reshape(nq, gc * m, gc * p)
    bd = jnp.concatenate([dense_b(bbar_r), dense_b(bbar_i)], axis=2)
    dense_c = lambda t: jnp.einsum('qgmp,gh->qgphm', t.reshape(nq, gc, m, p), eye).reshape(nq, gc * p, gc * m)
    cd = jnp.concatenate([dense_c(c_re.astype(F32)), -dense_c(c_im.astype(F32))], axis=1)
    kk = jnp.arange(1, S5_J + 1, dtype=F32)[:, None, None]
    pmag = jnp.exp(kk * (dt * ar)[None])
    pw_r = (pmag * jnp.cos(kk * (dt * ai)[None])).reshape(S5_J, g * p)
    pw_i = (pmag * jnp.sin(kk * (dt * ai)[None])).reshape(S5_J, g * p)
    pw = jnp.repeat(jnp.concatenate([pw_r, pw_i], axis=1), SUBLANES, axis=0)
    return bd.astype(BF16), pw, cd.astype(BF16)


def _s5_layer(u, bsz, seq, a_re, a_im, log_dt, b_re, b_im, c_re, c_im, d_skip, glu_w, glu_b):
    t, width = u.shape
    n = a_re.shape[0] * a_re.shape[1]
    bd, pw, cd = _s5_tables(a_re, a_im, log_dt, b_re, b_im, c_re, c_im)
    tile = SUBLANES * S5_J
    step_rows = S5_TILES_PER_STEP * tile
    steps_per_seq = seq // step_rows
    const = lambda b, i: (0, 0)
    src = (np.arange(tile) % SUBLANES) * S5_J + np.arange(tile) // SUBLANES
    perm = (src[:, None] == np.arange(tile)[None, :]).astype(np.float32)
    return pl.pallas_call(
        functools.partial(_s5_kernel, n_state=n),
        out_shape=jax.ShapeDtypeStruct((t, width), BF16),
        grid=(bsz, steps_per_seq),
        in_specs=[pl.BlockSpec((step_rows, width), lambda b, i: (b * steps_per_seq + i, 0)),
                  pl.BlockSpec((tile, tile), const),
                  pl.BlockSpec((tile, tile), const),
                  pl.BlockSpec(bd.shape, lambda b, i: (0, 0, 0)),
                  pl.BlockSpec((tile, 2 * n), const),
                  pl.BlockSpec((tile, 2 * n), const),
                  pl.BlockSpec(cd.shape, lambda b, i: (0, 0, 0)),
                  pl.BlockSpec((1, width), const),
                  pl.BlockSpec((width, width), const),
                  pl.BlockSpec((1, width), const)],
        out_specs=pl.BlockSpec((step_rows, width), lambda b, i: (b * steps_per_seq + i, 0)),
        scratch_shapes=[pltpu.VMEM((tile, 2 * n), F32), pltpu.VMEM((SUBLANES, 2 * n), F32)],
        compiler_params=_cparams(("arbitrary", "arbitrary")),
        name="s5_layer",
    )(u, jnp.asarray(perm, BF16), jnp.asarray(perm.T, BF16), bd, pw, pw.astype(BF16), cd,
      d_skip.reshape(1, width).astype(F32), glu_w.astype(BF16),
      glu_b.reshape(1, width).astype(F32))


def _finish(y, x_ref, mod_ref, pg_ref, out_ref, d_model):
    ms = jnp.mean(y * y, axis=-1, keepdims=True)
    yn = y * lax.rsqrt(ms + EPS) * pg_ref[...]
    gate_mod = mod_ref[0, :, 2 * d_model:3 * d_model]
    out_ref[...] = x_ref[...] + gate_mod * yn


def _split2(x):
    hi = x.astype(BF16)
    return hi, (x - hi.astype(F32)).astype(BF16)


def _out_ab_kernel(o0_ref, l0_ref, o1_ref, o2_ref, l1_ref, l2_ref, ob_ref, gate_ref, e_ref, un_ref, w_ref,
                   x_ref, mod_ref, pg_ref, out_ref, *, d_model):
    flat = lambda ref: ref[...].reshape(ROW_TILE, ref.shape[-1])
    aw = o0_ref.shape[-1]

    def unperm(o_ref, l_ref):
        parts = jnp.concatenate([flat(o_ref)] + list(_split2(flat(l_ref))), axis=1)
        moved = jnp.dot(un_ref[...], parts, preferred_element_type=F32)
        return moved[:, 0:aw], moved[:, aw:aw + LANES] + moved[:, aw + LANES:]

    o0, l0 = o0_ref[...].astype(F32), l0_ref[...]
    o1, l1 = unperm(o1_ref, l1_ref)
    o2, l2 = unperm(o2_ref, l2_ref)
    mx = jnp.maximum(jnp.maximum(l0, l1), l2)
    e0, e1, e2 = jnp.exp2(l0 - mx), jnp.exp2(l1 - mx), jnp.exp2(l2 - mx)
    inv_den = 1.0 / (e0 + e1 + e2)
    stacked = jnp.concatenate(list(_split2(e1 * inv_den)) + list(_split2(e2 * inv_den)), axis=0)
    wide = jnp.dot(stacked, e_ref[...], preferred_element_type=F32)
    term = lambda i: wide[i * ROW_TILE:(i + 1) * ROW_TILE]
    w1 = term(0) + term(1)
    w2 = term(2) + term(3)
    o_a = (1.0 - w1 - w2) * o0 + w1 * o1 + w2 * o2
    gate = gate_ref[...].astype(F32)
    sg = gate * _sigmoid(gate)
    aw = o_a.shape[-1]
    y = jnp.dot((o_a * sg[:, :aw]).astype(BF16), w_ref[0:aw, :], preferred_element_type=F32)
    y = y + jnp.dot((ob_ref[...].astype(F32) * sg[:, aw:]).astype(BF16), w_ref[aw:, :],
                    preferred_element_type=F32)
    _finish(y, x_ref, mod_ref, pg_ref, out_ref, d_model)


def _out_c_kernel(o_ref, gate_ref, w_ref, x_ref, mod_ref, pg_ref, out_ref, *, d_model):
    gate = gate_ref[...].astype(F32)
    o = o_ref[...].astype(F32) * (gate * _sigmoid(gate))
    y = jnp.dot(o.astype(BF16), w_ref[...], preferred_element_type=F32)
    _finish(y, x_ref, mod_ref, pg_ref, out_ref, d_model)


def _out_proj(kern, row_inputs, const_inputs, w_bf16, x2, mod3, post_g, seq, name):
    t, d = x2.shape
    tiles_per_seq = seq // ROW_TILE
    tiles_per_span = SPAN // ROW_TILE

    def row_spec(a):
        if a.ndim == 2:
            return pl.BlockSpec((ROW_TILE, a.shape[1]), lambda i: (i, 0))
        return pl.BlockSpec((None, SPAN_RES, SPAN_RUN, a.shape[-1]),
                            lambda i: (i // tiles_per_span, 0, i % tiles_per_span, 0))

    const_spec = lambda a: pl.BlockSpec(a.shape, lambda i: (0, 0))
    return pl.pallas_call(
        functools.partial(kern, d_model=d),
        out_shape=jax.ShapeDtypeStruct((t, d), F32),
        grid=(t // ROW_TILE,),
        in_specs=([row_spec(a) for a in row_inputs] + [const_spec(a) for a in const_inputs]
                  + [const_spec(w_bf16), row_spec(x2),
                     pl.BlockSpec((1, 1, 3 * d), lambda i: (i // tiles_per_seq, 0, 0)),
                     pl.BlockSpec((1, d), lambda i: (0, 0))]),
        out_specs=pl.BlockSpec((ROW_TILE, d), lambda i: (i, 0)),
        compiler_params=_cparams(("arbitrary",)),
        name=name,
    )(*row_inputs, *const_inputs, w_bf16, x2, mod3, post_g.reshape(1, d))


def _gdn_prep_kernel(x_ref, halo_ref, br_ref, cw_ref, alog_ref, dtb_ref,
                     w_ref, u_ref, qg_ref, kdt_ref, aqk_ref, dec_ref, xs_scr):
    n_all = GDN_PREP_BLOCKS * GDN_STEP
    first = pl.program_id(1) == 0
    halo = halo_ref[...]
    xs_scr[0:SUBLANES, :] = jnp.where(first, jnp.zeros_like(halo), halo)
    xs_scr[SUBLANES:SUBLANES + n_all, :] = x_ref[...]
    cw = cw_ref[...]
    conv = xs_scr[SUBLANES:SUBLANES + n_all, :] * cw[C_CONV - 1:C_CONV]
    for j in range(C_CONV - 1):
        off = SUBLANES - (C_CONV - 1) + j
        conv = conv + xs_scr[off:off + n_all, :] * cw[j:j + 1]
    act = conv * _sigmoid(conv)
    for blk in range(GDN_PREP_BLOCKS):
        _gdn_prep_block(blk, act, br_ref, alog_ref, dtb_ref, w_ref, u_ref, qg_ref, kdt_ref, aqk_ref, dec_ref)


def _gdn_prep_block(blk, act, br_ref, alog_ref, dtb_ref, w_ref, u_ref, qg_ref, kdt_ref, aqk_ref, dec_ref):
    n = GDN_STEP
    rows = slice(blk * n, (blk + 1) * n)
    head_cols = lambda base, h: act[rows, base + h * C_DK:base + (h + 1) * C_DK]

    row = lax.broadcasted_iota(jnp.int32, (n, n), 0)
    col = lax.broadcasted_iota(jnp.int32, (n, n), 1)
    same = (row // GDN_CHUNK) == (col // GDN_CHUNK)
    tril = jnp.logical_and(same, row >= col)
    strict = jnp.logical_and(same, row > col)
    eye = (row == col).astype(F32)

    br = br_ref[rows, :]
    beta_all = _sigmoid(br)
    xg = br + dtb_ref[...]
    softplus = jnp.maximum(xg, 0.0) + jnp.log(1.0 + jnp.exp(-jnp.abs(xg)))
    g_all = -jnp.exp(alog_ref[...]) * softplus
    gc_all = jnp.dot(tril.astype(F32), g_all, preferred_element_type=F32, precision=HI)
    gc_t = gc_all.T
    first_chunk = lax.broadcasted_iota(jnp.int32, (n, 1), 0) < GDN_CHUNK
    dec_row = lax.broadcasted_iota(jnp.int32, (SUBLANES, LANES), 0)

    heads = range(C_HEADS)
    lanes = [slice(h * LANES, (h + 1) * LANES) for h in heads]
    gcols = [gc_all[:, C_HEADS + h:C_HEADS + h + 1] for h in heads]
    egs = [jnp.exp(g) for g in gcols]
    kbs, amats, invs = [], [], []
    for h in heads:
        grow = gc_t[C_HEADS + h:C_HEADS + h + 1, :]
        decay = jnp.where(tril, jnp.exp(jnp.where(tril, gcols[h] - grow, 0.0)), 0.0)
        q = head_cols(0, h)
        k = head_cols(C_HEADS * C_DK, h)
        q = q * lax.rsqrt(jnp.sum(q * q, axis=-1, keepdims=True) + EPS) * (C_DK ** -0.5)
        k = k * lax.rsqrt(jnp.sum(k * k, axis=-1, keepdims=True) + EPS)
        kb = k * beta_all[:, h:h + 1]
        k16 = k.astype(BF16)
        a = jnp.where(strict, lax.dot_general(kb.astype(BF16), k16, _NT, preferred_element_type=F32) * decay, 0.0)
        aqk = lax.dot_general(q.astype(BF16), k16, _NT, preferred_element_type=F32) * decay
        aqk_ref[rows, lanes[h]] = aqk.astype(BF16)
        qg_ref[rows, lanes[h]] = (q * egs[h]).astype(BF16)
        g_last = jnp.where(first_chunk, gcols[h][GDN_CHUNK - 1:GDN_CHUNK], gcols[h][n - 1:n])
        kdt_ref[rows, lanes[h]] = (k * jnp.exp(g_last - gcols[h])).T.astype(BF16)
        dec_ref[blk, :, lanes[h]] = jnp.where(dec_row < SUBLANES // 2, egs[h][GDN_CHUNK - 1:GDN_CHUNK],
                                             egs[h][n - 1:n])
        kbs.append(kb)
        amats.append(a)
        invs.append(eye - a)
    pows = []
    for h in heads:
        a16 = amats[h].astype(BF16)
        pows.append(jnp.dot(a16, a16, preferred_element_type=F32))
    for _ in range(4):
        for h in heads:
            p16 = pows[h].astype(BF16)
            both = jnp.dot(jnp.concatenate([invs[h].astype(BF16), p16], axis=0), p16, preferred_element_type=F32)
            invs[h] = invs[h] + both[0:n]
            pows[h] = both[n:]
    for h in heads:
        invs[h] = invs[h] + jnp.dot(invs[h].astype(BF16), pows[h].astype(BF16), preferred_element_type=F32)
    for h in heads:
        v = head_cols(2 * C_HEADS * C_DK, h)
        rhs = jnp.concatenate([v * beta_all[:, h:h + 1], kbs[h] * egs[h]], axis=1)
        uw = jnp.dot(invs[h].astype(BF16), rhs.astype(BF16), preferred_element_type=F32)
        u_ref[rows, lanes[h]] = uw[:, 0:C_DV]
        w_ref[rows, lanes[h]] = uw[:, C_DV:].astype(BF16)


def _gdn_rec_kernel(w_ref, u_ref, qg_ref, kdt_ref, aqk_ref, dec_ref, ng_ref, o_ref, s_scr):
    @pl.when(pl.program_id(0) == 0)
    def _():
        s_scr[...] = jnp.zeros_like(s_scr)

    half = SUBLANES // 2
    zeros = jnp.zeros((GDN_CHUNK, C_DV), BF16)
    lanes = [slice(h * LANES, (h + 1) * LANES) for h in range(C_HEADS)]
    chains = [(b, h) for b in range(w_ref.shape[0]) for h in range(C_HEADS)]
    states = [s_scr[b * C_HEADS + h] for b, h in chains]
    for blk in range(GDN_REC_BLOCKS):
        blk_rows = slice(blk * GDN_STEP, (blk + 1) * GDN_STEP)
        for j in range(GDN_STEP // GDN_CHUNK):
            rows = slice(blk * GDN_STEP + j * GDN_CHUNK, blk * GDN_STEP + (j + 1) * GDN_CHUNK)
            wss = []
            for c, (b, h) in enumerate(chains):
                wq = jnp.concatenate([w_ref[b, rows, lanes[h]], qg_ref[b, rows, lanes[h]]], axis=0)
                wss.append(jnp.dot(wq, states[c].astype(BF16), preferred_element_type=F32))
            for c, (b, h) in enumerate(chains):
                v_new = (u_ref[b, rows, lanes[h]] - wss[c][0:GDN_CHUNK]).astype(BF16)
                v_pad = jnp.concatenate([v_new, zeros] if j == 0 else [zeros, v_new], axis=0)
                both = jnp.dot(jnp.concatenate([aqk_ref[b, rows, lanes[h]], kdt_ref[b, blk_rows, lanes[h]]], axis=0),
                               v_pad, preferred_element_type=F32)
                o = wss[c][GDN_CHUNK:] + both[0:GDN_CHUNK]
                dec = dec_ref[b, blk, j * half:j * half + 1, lanes[h]]
                states[c] = states[c] * dec + both[GDN_CHUNK:]
                ms = jnp.mean(o * o, axis=-1, keepdims=True)
                o_ref[b, rows, lanes[h]] = (o * lax.rsqrt(ms + EPS) * ng_ref[...]).astype(o_ref.dtype)
    for c, (b, h) in enumerate(chains):
        s_scr[b * C_HEADS + h] = states[c]


def _gdn_core(qkv_pre, br, bsz, seq, conv_w, a_log, dt_bias, norm_g):
    t, qkv_w = qkv_pre.shape
    hw = C_HEADS * C_DV
    steps_per_seq = seq // GDN_STEP
    prep_rows = GDN_PREP_BLOCKS * GDN_STEP
    prep_steps = steps_per_seq // GDN_PREP_BLOCKS
    halo_blocks = prep_rows // SUBLANES
    pad_row = lambda vec: jnp.zeros((1, LANES), F32).at[0, C_HEADS:2 * C_HEADS].set(vec.astype(F32))
    tok = lambda b, i: (b * prep_steps + i, 0)
    const = lambda b, i: (0, 0)
    tok_spec = pl.BlockSpec((prep_rows, hw), tok)
    dec_spec = pl.BlockSpec((GDN_PREP_BLOCKS, SUBLANES, hw), lambda b, i: (b * prep_steps + i, 0, 0))
    w, u, qg, kdt, aqk, dec = pl.pallas_call(
        _gdn_prep_kernel,
        out_shape=[jax.ShapeDtypeStruct((t, hw), BF16), jax.ShapeDtypeStruct((t, hw), F32),
                   jax.ShapeDtypeStruct((t, hw), BF16), jax.ShapeDtypeStruct((t, hw), BF16),
                   jax.ShapeDtypeStruct((t, hw), BF16),
                   jax.ShapeDtypeStruct((t // GDN_STEP, SUBLANES, hw), F32)],
        grid=(bsz, prep_steps),
        in_specs=[pl.BlockSpec((prep_rows, qkv_w), tok),
                  pl.BlockSpec((SUBLANES, qkv_w),
                               lambda b, i: (jnp.maximum((b * prep_steps + i) * halo_blocks - 1, 0), 0)),
                  pl.BlockSpec((prep_rows, LANES), tok),
                  pl.BlockSpec((C_CONV, qkv_w), const),
                  pl.BlockSpec((1, LANES), const),
                  pl.BlockSpec((1, LANES), const)],
        out_specs=[tok_spec, tok_spec, tok_spec, tok_spec, tok_spec, dec_spec],
        scratch_shapes=[pltpu.VMEM((SUBLANES + prep_rows, qkv_w), F32)],
        compiler_params=_cparams(("arbitrary", "arbitrary")),
        name="gdn_prep",
    )(qkv_pre, qkv_pre, br, conv_w.astype(F32), pad_row(a_log), pad_row(dt_bias))
    rec_steps = steps_per_seq // GDN_REC_BLOCKS
    seq_view = lambda a: a.reshape(bsz, seq, hw)
    rec_tok = pl.BlockSpec((bsz, GDN_REC_BLOCKS * GDN_STEP, hw), lambda i: (0, i, 0))
    rec_dec = pl.BlockSpec((bsz, GDN_REC_BLOCKS, SUBLANES, hw), lambda i: (0, i, 0, 0))
    o = pl.pallas_call(
        _gdn_rec_kernel,
        out_shape=jax.ShapeDtypeStruct((bsz, seq, hw), BF16),
        grid=(rec_steps,),
        in_specs=[rec_tok, rec_tok, rec_tok, rec_tok, rec_tok, rec_dec,
                  pl.BlockSpec((1, C_DV), lambda i: (0, 0))],
        out_specs=rec_tok,
        scratch_shapes=[pltpu.VMEM((bsz * C_HEADS, C_DK, C_DV), F32)],
        compiler_params=_cparams(("arbitrary",)),
        name="gdn_recurrence",
    )(seq_view(w), seq_view(u), seq_view(qg), seq_view(kdt), seq_view(aqk),
      dec.reshape(bsz, steps_per_seq, SUBLANES, hw), norm_g.reshape(1, C_DV).astype(F32))
    return o.reshape(t, hw)


def _ab_layer(x2, mod3, bsz, seq, pre_g, post_g, rel_bias, w_in, w_out, s5_params):
    b_width = s5_params[7].shape[-1]
    assert DILATED_CONFIGS[0][1] == 1 and all(SPAN_RES % dl == 0 for _, dl in DILATED_CONFIGS) and seq % SPAN == 0
    head_expand = jnp.asarray(np.arange(LANES)[:, None] == (np.arange(A_WIDTH)[None, :] // A_HEAD_DIM), BF16)
    unperm = jnp.asarray(_span_perm().T, BF16)
    splits = ((0, A_WIDTH, A_HEAD_DIM ** -0.5 * LOG2_E), (A_WIDTH, A_WIDTH, 1.0), (2 * A_WIDTH, A_WIDTH, 1.0),
              (3 * A_WIDTH, b_width, 1.0), (3 * A_WIDTH + b_width, A_WIDTH + b_width, 1.0))
    q, k, v, u, gate, qs, ks, vs = _in_proj(x2, mod3, pre_g, w_in.astype(BF16), splits,
                                            (BF16, BF16, BF16, F32, BF16), seq, n_span=3)
    outs, lses = [], []
    bias_tabs = _attn_bias_tables(rel_bias)
    for cfg, (_, dil) in enumerate(DILATED_CONFIGS):
        qkv = (q, k, v) if dil == 1 else (qs, ks, vs)
        o_c, lse_c = _dilated_attention_one(*qkv, bias_tabs[cfg], bsz, seq, dil)
        outs.append(o_c)
        lses.append(lse_c)
    o_b = _s5_layer(u, bsz, seq, *s5_params)
    row_inputs = [outs[0], lses[0], outs[1], outs[2], lses[1], lses[2], o_b, gate]
    return _out_proj(_out_ab_kernel, row_inputs, [head_expand, unperm], w_out.astype(BF16),
                     x2, mod3, post_g, seq, "out_proj_ab")


def _gdn_layer(x2, mod3, bsz, seq, pre_g, post_g, w_in, conv_w, a_log, dt_bias, norm_g, w_out):
    d = x2.shape[1]
    qkv_w = 2 * C_HEADS * C_DK + C_HEADS * C_DV
    gate_w = C_HEADS * C_DV
    w_pad = jnp.concatenate(
        [w_in, jnp.zeros((d, LANES - (w_in.shape[1] - qkv_w - gate_w)), w_in.dtype)], axis=1).astype(BF16)
    splits = ((0, qkv_w, 1.0), (qkv_w, gate_w, 1.0), (qkv_w + gate_w, LANES, 1.0))
    qkv_pre, gate, br = _in_proj(x2, mod3, pre_g, w_pad, splits, (F32, BF16, F32), seq)
    o = _gdn_core(qkv_pre, br, bsz, seq, conv_w, a_log, dt_bias, norm_g)
    return _out_proj(_out_c_kernel, [o, gate], [], w_out.astype(BF16), x2, mod3, post_g, seq, "out_proj_c")


def kernel(x, c, ada_w, ada_b, pre_g, post_g, rel_bias, ab_w_in, ab_w_out, s5_a_re, s5_a_im, s5_log_dt, s5_b_re, s5_b_im, s5_c_re, s5_c_im, s5_d, s5_glu_w, s5_glu_b, gdn_w_in, gdn_conv, gdn_a_log, gdn_dt_bias, gdn_norm_g, gdn_w_out):
    bsz, seq, d = x.shape
    depth = ada_w.shape[0]
    assert seq % ROW_TILE == 0 and seq % (SUBLANES * S5_J * S5_TILES_PER_STEP) == 0
    assert seq % (GDN_STEP * GDN_PREP_BLOCKS) == 0 and seq % (GDN_STEP * GDN_REC_BLOCKS) == 0
    x2 = x.reshape(bsz * seq, d)
    mod = _adaln_mod(c, ada_w, ada_b)
    for layer in range(depth):
        j = layer // 2
        mod3 = mod[layer].reshape(bsz, 1, 3 * d)
        if layer % 2 == 0:
            s5_params = (s5_a_re[j], s5_a_im[j], s5_log_dt[j], s5_b_re[j], s5_b_im[j], s5_c_re[j], s5_c_im[j],
                         s5_d[j], s5_glu_w[j], s5_glu_b[j])
            x2 = _ab_layer(x2, mod3, bsz, seq, pre_g[layer], post_g[layer], rel_bias, ab_w_in[j], ab_w_out[j],
                           s5_params)
        else:
            x2 = _gdn_layer(x2, mod3, bsz, seq, pre_g[layer], post_g[layer], gdn_w_in[j], gdn_conv[j],
                            gdn_a_log[j], gdn_dt_bias[j], gdn_norm_g[j], gdn_w_out[j])
    return x2.reshape(bsz, seq, d)
```

```python
import functools
import math

import numpy as np
import jax
import jax.numpy as jnp
from jax import lax
from jax.experimental import pallas as pl
from jax.experimental.pallas import tpu as pltpu

F32 = jnp.float32
BF16 = jnp.bfloat16
HI = lax.Precision.HIGHEST

EPS = 1e-6
A_HEADS = 8
A_HEAD_DIM = 64
A_WIDTH = A_HEADS * A_HEAD_DIM
A_BLOCK = 128
DILATED_CONFIGS = ((128, 1), (512, 4), (2048, 16))
B_GROUP = 16
B_STATE = 64
REL_BUCKETS = 32
REL_MAX_DIST = 2048
C_HEADS = 8
C_DK = 128
C_DV = 128
C_CONV = 4
MASK_NEG = -1e30
LOG2_E = math.log2(math.e)

LANES = 128
SUBLANES = 8
VMEM_LIMIT = 48 * 1024 * 1024
ROW_TILE = 512
COL_CHUNK = 512
SPAN_RES = max(dl for _, dl in DILATED_CONFIGS)
SPAN = A_BLOCK * SPAN_RES
SPAN_RUN = ROW_TILE // SPAN_RES
ATTN_BLOCKS_PER_STEP = 4
S5_J = 32
S5_LW = 512
S5_TILES_PER_STEP = 2
GDN_STEP = 128
GDN_CHUNK = 64
CONV_TILE = 256
GDN_PREP_BLOCKS = 2
GDN_REC_BLOCKS = 2

_NT = (((1,), (1,)), ((), ()))


def _cparams(sem):
    return pltpu.CompilerParams(dimension_semantics=sem, vmem_limit_bytes=VMEM_LIMIT)


def _sigmoid(x):
    return 1.0 / (1.0 + jnp.exp(-x))


def _mod_kernel(c_ref, w_ref, b_ref, o_ref):
    c = c_ref[...]
    ca = c * _sigmoid(c)
    o_ref[0] = jnp.dot(ca, w_ref[0], preferred_element_type=F32, precision=HI) + b_ref[0]


def _adaln_mod(c, ada_w, ada_b):
    depth, d, d3 = ada_w.shape
    bsz = c.shape[0]
    return pl.pallas_call(
        _mod_kernel,
        out_shape=jax.ShapeDtypeStruct((depth, bsz, d3), F32),
        grid=(depth, d3 // d),
        in_specs=[pl.BlockSpec((bsz, d), lambda l, j: (0, 0)),
                  pl.BlockSpec((1, d, d), lambda l, j: (l, 0, j)),
                  pl.BlockSpec((1, 1, d), lambda l, j: (l, 0, j))],
        out_specs=pl.BlockSpec((1, bsz, d), lambda l, j: (l, 0, j)),
        compiler_params=_cparams(("arbitrary", "arbitrary")),
        name="adaln_mod",
    )(c, ada_w, ada_b.reshape(depth, 1, d3))


def _span_perm():
    rho = np.arange(ROW_TILE)
    nat = SPAN_RES * (rho % SPAN_RUN) + rho // SPAN_RUN
    return (nat[:, None] == np.arange(ROW_TILE)[None, :]).astype(np.float32)


def _modulated_norm(x_ref, mod_ref, g_ref, d_model):
    x = x_ref[...]
    ms = jnp.mean(x * x, axis=-1, keepdims=True)
    y = x * lax.rsqrt(ms + EPS) * g_ref[...]
    shift = mod_ref[0, :, 0:d_model]
    scale = mod_ref[0, :, d_model:2 * d_model]
    return (y * (1.0 + scale) + shift).astype(BF16)


def _sublane_major_perm(tile):
    rho = np.arange(tile)
    src = (rho % SUBLANES) * (tile // SUBLANES) + rho // SUBLANES
    return (src[:, None] == np.arange(tile)[None, :]).astype(np.float32)


def _in_proj_gdn_kernel(x_ref, mod_ref, g_ref, w_ref, cw_ref, pm_ref, act_ref, gate_ref, br_ref, tail_scr, *,
                        d_model, qkv_w, gate_w, tiles_per_seq):
    @pl.when(pl.program_id(0) % tiles_per_seq == 0)
    def _():
        tail_scr[...] = jnp.zeros_like(tail_scr)

    h = _modulated_norm(x_ref, mod_ref, g_ref, d_model)
    groups = ROW_TILE // CONV_TILE
    jn = CONV_TILE // SUBLANES
    taps = C_CONV - 1
    h_perm = jnp.concatenate(
        [jnp.dot(pm_ref[...], h[g * CONV_TILE:(g + 1) * CONV_TILE], preferred_element_type=F32).astype(BF16)
         for g in range(groups)], axis=0)
    first_sublane = lax.broadcasted_iota(jnp.int32, (SUBLANES, COL_CHUNK), 0) == 0
    units = [(c0, g) for c0 in range(0, qkv_w, COL_CHUNK) for g in range(groups)]
    project = lambda c0, g: jnp.dot(h_perm[g * CONV_TILE:(g + 1) * CONV_TILE], w_ref[:, c0:c0 + COL_CHUNK],
                                    preferred_element_type=F32)
    z_next = project(*units[0])
    for idx, (c0, g) in enumerate(units):
        cols = slice(c0, c0 + COL_CHUNK)
        z = z_next
        if idx + 1 < len(units):
            z_next = project(*units[idx + 1])
        wt = [cw_ref[SUBLANES * j:SUBLANES * (j + 1), cols] for j in range(C_CONV)]
        prev_tail = [tail_scr[SUBLANES * m:SUBLANES * (m + 1), cols] for m in range(taps)]
        zj = [z[SUBLANES * j:SUBLANES * (j + 1)] for j in range(jn)]
        wrapped = [jnp.where(first_sublane, pltpu.roll(prev_tail[m], 1, axis=0),
                             pltpu.roll(zj[jn - taps + m], 1, axis=0)) for m in range(taps)]
        ext = wrapped + zj
        outs = []
        for j in range(jn):
            conv = ext[j + taps] * wt[taps]
            for k in range(taps):
                conv = conv + ext[j + k] * wt[k]
            outs.append(conv * _sigmoid(conv))
        for m in range(taps):
            tail_scr[SUBLANES * m:SUBLANES * (m + 1), cols] = zj[jn - taps + m]
        act_ref[g * CONV_TILE:(g + 1) * CONV_TILE, cols] = jnp.concatenate(outs, axis=0).astype(act_ref.dtype)
    for c0 in range(0, gate_w, COL_CHUNK):
        gate_ref[:, c0:c0 + COL_CHUNK] = jnp.dot(
            h, w_ref[:, qkv_w + c0:qkv_w + c0 + COL_CHUNK], preferred_element_type=F32).astype(gate_ref.dtype)
    br_ref[...] = jnp.dot(h, w_ref[:, qkv_w + gate_w:], preferred_element_type=F32)


def _in_proj_gdn(x2, mod3, gain, w_bf16, conv_w, seq, qkv_w, gate_w):
    t, d = x2.shape
    tiles_per_seq = seq // ROW_TILE
    row = lambda wd: pl.BlockSpec((ROW_TILE, wd), lambda i: (i, 0))
    const = lambda a: pl.BlockSpec(a.shape, lambda i: (0, 0))
    rest_w = w_bf16.shape[1] - qkv_w - gate_w
    cw8 = jnp.repeat(conv_w.astype(F32), SUBLANES, axis=0)
    perm = jnp.asarray(_sublane_major_perm(CONV_TILE), BF16)
    return pl.pallas_call(
        functools.partial(_in_proj_gdn_kernel, d_model=d, qkv_w=qkv_w, gate_w=gate_w, tiles_per_seq=tiles_per_seq),
        out_shape=[jax.ShapeDtypeStruct((t, qkv_w), BF16), jax.ShapeDtypeStruct((t, gate_w), BF16),
                   jax.ShapeDtypeStruct((t, rest_w), F32)],
        grid=(t // ROW_TILE,),
        in_specs=[row(d),
                  pl.BlockSpec((1, 1, 3 * d), lambda i: (i // tiles_per_seq, 0, 0)),
                  pl.BlockSpec((1, d), lambda i: (0, 0)),
                  const(w_bf16), const(cw8), const(perm)],
        out_specs=[row(qkv_w), row(gate_w), row(rest_w)],
        scratch_shapes=[pltpu.VMEM((SUBLANES * (C_CONV - 1), qkv_w), F32)],
        compiler_params=_cparams(("arbitrary",)),
        name="in_proj_gdn",
    )(x2, mod3, gain.reshape(1, d), w_bf16, cw8, perm)


def _in_proj_kernel(x_ref, mod_ref, g_ref, w_ref, *rest, splits, d_model, n_span):
    pm_ref = rest[0] if n_span else None
    out_refs = rest[1:] if n_span else rest
    span_refs = out_refs[len(splits):]
    h = _modulated_norm(x_ref, mod_ref, g_ref, d_model)
    for idx, ((c0, width, mult), o_ref) in enumerate(zip(splits, out_refs)):
        for cc in range(0, width, COL_CHUNK):
            cw = min(COL_CHUNK, width - cc)
            acc = jnp.dot(h, w_ref[:, c0 + cc:c0 + cc + cw], preferred_element_type=F32)
            if mult != 1.0:
                acc = acc * mult
            val = acc.astype(o_ref.dtype)
            o_ref[:, cc:cc + cw] = val
            if idx < n_span:
                moved = jnp.dot(pm_ref[...], val, preferred_element_type=F32).astype(BF16)
                span_refs[idx][:, :, cc:cc + cw] = moved.reshape(SPAN_RES, SPAN_RUN, cw)


def _in_proj(x2, mod3, gain, w_bf16, splits, out_dtypes, seq, n_span=0):
    t, d = x2.shape
    tiles_per_seq = seq // ROW_TILE
    tiles_per_span = SPAN // ROW_TILE
    n_w = w_bf16.shape[1]
    row = lambda wd: pl.BlockSpec((ROW_TILE, wd), lambda i: (i, 0))
    span_shape = lambda wd: jax.ShapeDtypeStruct((t // SPAN, SPAN_RES, A_BLOCK, wd), BF16)
    span_spec = lambda wd: pl.BlockSpec((None, SPAN_RES, SPAN_RUN, wd),
                                        lambda i: (i // tiles_per_span, 0, i % tiles_per_span, 0))
    perm_in = [jnp.asarray(_span_perm(), BF16)] if n_span else []
    perm_spec = [pl.BlockSpec((ROW_TILE, ROW_TILE), lambda i: (0, 0))] if n_span else []
    return pl.pallas_call(
        functools.partial(_in_proj_kernel, splits=splits, d_model=d, n_span=n_span),
        out_shape=([jax.ShapeDtypeStruct((t, wd), dt) for (_, wd, _), dt in zip(splits, out_dtypes)]
                   + [span_shape(wd) for (_, wd, _) in splits[:n_span]]),
        grid=(t // ROW_TILE,),
        in_specs=[row(d),
                  pl.BlockSpec((1, 1, 3 * d), lambda i: (i // tiles_per_seq, 0, 0)),
                  pl.BlockSpec((1, d), lambda i: (0, 0)),
                  pl.BlockSpec((d, n_w), lambda i: (0, 0))] + perm_spec,
        out_specs=[row(wd) for (_, wd, _) in splits] + [span_spec(wd) for (_, wd, _) in splits[:n_span]],
        compiler_params=_cparams(("arbitrary",)),
        name="in_proj",
    )(x2, mod3, gain.reshape(1, d), w_bf16, *perm_in)


def _t5_bucket(dist):
    dist = np.maximum(dist, 0)
    max_exact = REL_BUCKETS // 2
    large = max_exact + (np.log(np.maximum(dist, 1) / max_exact)
                         / math.log(REL_MAX_DIST / max_exact) * (REL_BUCKETS - max_exact)).astype(np.int32)
    large = np.minimum(large, REL_BUCKETS - 1)
    return np.where(dist < max_exact, dist, large).astype(np.int32)


def _bias_kernel(rb_ref, bucket_ref, mask_ref, o_ref):
    def body(r, carry):
        rows = pl.ds(pl.multiple_of(r * SUBLANES, SUBLANES), SUBLANES)
        bk = bucket_ref[0, rows, :]
        accs = [jnp.zeros(bk.shape, F32) for _ in range(A_HEADS)]
        for b in range(REL_BUCKETS):
            eq = bk == b
            accs = [jnp.where(eq, rb_ref[b, h], acc) for h, acc in enumerate(accs)]
        for f in range(2):
            keep = mask_ref[0, f, rows, :] != 0
            for h in range(A_HEADS):
                o_ref[0, f, h, rows, :] = jnp.where(keep, accs[h] * LOG2_E, MASK_NEG)
        return carry
    lax.fori_loop(0, A_BLOCK // SUBLANES, body, 0)


def _attn_bias_tables(rel_bias):
    qi = np.arange(A_BLOCK)[:, None]
    kj = np.arange(2 * A_BLOCK)[None, :]
    rel = qi + A_BLOCK - kj
    buckets, masks = [], []
    for window, dil in DILATED_CONFIGS:
        band = (rel >= 0) & (rel <= window // dil)
        bucket = _t5_bucket(rel * dil)
        mask = np.stack([band & (kj >= A_BLOCK), band]).astype(np.int32)
        if dil > 1:
            runs = SPAN_RES // dil
            run = A_BLOCK // runs
            rho = np.arange(A_BLOCK)
            sub = runs * (rho % run) + rho // run
            keys = np.concatenate([sub, A_BLOCK + sub])
            bucket = bucket[sub][:, keys]
            mask = mask[:, sub][:, :, keys]
        buckets.append(bucket)
        masks.append(mask)
    n_cfg = len(DILATED_CONFIGS)
    return pl.pallas_call(
        _bias_kernel,
        out_shape=jax.ShapeDtypeStruct((n_cfg, 2, A_HEADS, A_BLOCK, 2 * A_BLOCK), F32),
        grid=(n_cfg,),
        in_specs=[pl.BlockSpec(memory_space=pltpu.SMEM),
                  pl.BlockSpec((1, A_BLOCK, 2 * A_BLOCK), lambda c: (c, 0, 0)),
                  pl.BlockSpec((1, 2, A_BLOCK, 2 * A_BLOCK), lambda c: (c, 0, 0, 0))],
        out_specs=pl.BlockSpec((1, 2, A_HEADS, A_BLOCK, 2 * A_BLOCK), lambda c: (c, 0, 0, 0, 0)),
        compiler_params=_cparams(("arbitrary",)),
        name="attn_bias",
    )(rel_bias.astype(F32), jnp.asarray(np.stack(buckets)), jnp.asarray(np.stack(masks)))


def _attn_block(q, kcat, vcat, bias_ref):
    lane = lax.broadcasted_iota(jnp.int32, (A_BLOCK, LANES), 1)
    low = lane < A_HEAD_DIM
    lse_tile = jnp.zeros((A_BLOCK, LANES), F32)
    zero = jnp.zeros((A_BLOCK, LANES), BF16)
    heads = range(A_HEADS)
    pair = [slice((h // 2) * LANES, (h // 2 + 1) * LANES) for h in heads]
    scores = []
    for h in heads:
        qm = jnp.where(low if h % 2 == 0 else jnp.logical_not(low), q[:, pair[h]], zero)
        scores.append(lax.dot_general(qm, kcat[:, pair[h]], _NT, preferred_element_type=F32) + bias_ref[0, h])
    probs, inv_l = [], []
    for h in heads:
        m = jnp.max(scores[h], axis=-1, keepdims=True)
        p = jnp.exp2(scores[h] - m)
        l = jnp.sum(p, axis=-1, keepdims=True)
        probs.append(p.astype(BF16))
        inv_l.append(1.0 / l)
        lse_tile = jnp.where(lane == h, m + jnp.log2(l), lse_tile)
    outs = [jnp.dot(probs[h], vcat[:, pair[h]], preferred_element_type=F32) * inv_l[h] for h in heads]
    o = jnp.concatenate([jnp.where(low, outs[2 * hp], outs[2 * hp + 1]) for hp in range(A_HEADS // 2)], axis=1)
    return o, lse_tile


def _attn_kernel(q_ref, kp_ref, kc_ref, vp_ref, vc_ref, bias0_ref, bias_ref, o_ref, lse_ref, *, per_lead):
    def sub(ref, rr, i):
        if ref.ndim == 2:
            return ref[i * A_BLOCK:(i + 1) * A_BLOCK, :]
        run = ref.shape[3] // per_lead
        return ref[i // per_lead, :, rr, (i % per_lead) * run:(i % per_lead + 1) * run, :]

    def put(ref, rr, i, val):
        if ref.ndim == 2:
            ref[i * A_BLOCK:(i + 1) * A_BLOCK, :] = val.astype(ref.dtype)
        else:
            run = ref.shape[3] // per_lead
            ref[i // per_lead, :, rr, (i % per_lead) * run:(i % per_lead + 1) * run, :] = (
                val.astype(ref.dtype).reshape(ref.shape[1], run, ref.shape[4]))

    flat = lambda v: v.reshape(A_BLOCK, v.shape[-1])
    token_order = q_ref.ndim == 2
    n_blocks = q_ref.shape[0] // A_BLOCK if token_order else q_ref.shape[0] * per_lead
    for rr in range(1 if token_order else q_ref.shape[2]):
        k_prev = flat(kp_ref[...] if token_order else kp_ref[0, :, rr])
        v_prev = flat(vp_ref[...] if token_order else vp_ref[0, :, rr])
        for i in range(n_blocks):
            k_cur, v_cur = flat(sub(kc_ref, rr, i)), flat(sub(vc_ref, rr, i))
            o, lse_tile = _attn_block(flat(sub(q_ref, rr, i)), jnp.concatenate([k_prev, k_cur], axis=0),
                                      jnp.concatenate([v_prev, v_cur], axis=0),
                                      bias0_ref if i == 0 else bias_ref)
            put(o_ref, rr, i, o)
            put(lse_ref, rr, i, lse_tile)
            k_prev, v_prev = k_cur, v_cur


def _dilated_attention_one(q, k, v, bias_tab, bsz, seq, dil):
    nb = seq // dil // A_BLOCK
    step = min(ATTN_BLOCKS_PER_STEP, nb)
    w = A_WIDTH
    if dil == 1:
        per_lead = n_res = 1
        view = lambda a: a.reshape(bsz, seq, a.shape[-1])
        block = lambda wd: (None, step * A_BLOCK, wd)
        block_prev = lambda wd: (None, A_BLOCK, wd)
        cur = lambda b, r, m: (b, m, 0)
        prev = lambda b, r, m: (b, jnp.maximum(m * step - 1, 0), 0)
        out_shape = lambda wd, dt: jax.ShapeDtypeStruct((bsz, seq, wd), dt)
    else:
        runs = SPAN_RES // dil
        run = A_BLOCK // runs
        spans_per_seq = seq // SPAN
        per_lead = min(step, runs)
        lead = step // per_lead
        n_res = max(1, ATTN_BLOCKS_PER_STEP // step)
        assert runs % per_lead == 0 and spans_per_seq % lead == 0 and dil % n_res == 0
        view = lambda a: a.reshape(a.shape[0], runs, dil, A_BLOCK, a.shape[-1])
        block = lambda wd: (lead, runs, n_res, per_lead * run, wd)
        block_prev = lambda wd: (1, runs, n_res, run, wd)
        cur = lambda b, r, m: ((b * spans_per_seq + (m * step) // runs) // lead, 0, r, ((m * step) % runs) // per_lead, 0)

        def prev(b, r, m):
            n = jnp.maximum(m * step - 1, 0)
            return (b * spans_per_seq + n // runs, 0, r, n % runs, 0)

        out_shape = lambda wd, dt: jax.ShapeDtypeStruct((bsz * spans_per_seq, runs, dil, A_BLOCK, wd), dt)
    blk = pl.BlockSpec(block(w), cur)
    blk_prev = pl.BlockSpec(block_prev(w), prev)
    bias_block = (1, A_HEADS, A_BLOCK, 2 * A_BLOCK)
    o, lse = pl.pallas_call(
        functools.partial(_attn_kernel, per_lead=per_lead),
        out_shape=[out_shape(w, BF16), out_shape(LANES, F32)],
        grid=(bsz, dil // n_res, nb // step),
        in_specs=[blk, blk_prev, blk, blk_prev, blk,
                  pl.BlockSpec(bias_block, lambda b, r, m: (jnp.minimum(m, 1), 0, 0, 0)),
                  pl.BlockSpec(bias_block, lambda b, r, m: (1, 0, 0, 0))],
        out_specs=[pl.BlockSpec(block(w), cur), pl.BlockSpec(block(LANES), cur)],
        compiler_params=_cparams(("arbitrary", "arbitrary", "arbitrary")),
        name=f"dilated_attn_d{dil}",
    )(view(q), view(k), view(k), view(v), view(v), bias_tab, bias_tab)
    if dil == 1:
        return o.reshape(bsz * seq, w), lse.reshape(bsz * seq, LANES)
    return o.reshape(o.shape[0], SPAN_RES, A_BLOCK, w), lse.reshape(o.shape[0], SPAN_RES, A_BLOCK, LANES)


def _gelu_tanh(x):
    c = math.sqrt(2.0 / math.pi)
    return x * (0.5 * (1.0 + jnp.tanh(c * (x + 0.044715 * (x * x * x)))))


def _s5_kernel(u_ref, pm_ref, pmt_ref, bd_ref, pw_ref, pw16_ref, cd_ref, dsk_ref, gw_ref, gb_ref, o_ref,
               x_scr, st_scr, *, n_state):
    @pl.when(pl.program_id(1) == 0)
    def _():
        st_scr[...] = jnp.zeros_like(st_scr)

    for tile_idx in range(S5_TILES_PER_STEP):
        _s5_tile(tile_idx, u_ref, pm_ref, pmt_ref, bd_ref, pw_ref, pw16_ref, cd_ref, dsk_ref, gw_ref, gb_ref,
                 o_ref, x_scr, st_scr, n_state)


def _s5_tile(tile_idx, u_ref, pm_ref, pmt_ref, bd_ref, pw_ref, pw16_ref, cd_ref, dsk_ref, gw_ref, gb_ref,
             o_ref, x_scr, st_scr, n_state):
    n = n_state
    jn = S5_J
    tile_rows = slice(tile_idx * SUBLANES * jn, (tile_idx + 1) * SUBLANES * jn)

    u = u_ref[tile_rows, :]
    u_perm = jnp.dot(pm_ref[...], u.astype(BF16), preferred_element_type=F32).astype(BF16)
    cw = u.shape[1] * S5_LW // n
    chunks = [(slice(c0, c0 + S5_LW), slice(n + c0, n + c0 + S5_LW)) for c0 in range(0, n, S5_LW)]

    def project_in(q):
        bu = jnp.dot(u_perm[:, q * cw:(q + 1) * cw], bd_ref[q], preferred_element_type=F32)
        x_scr[:, chunks[q][0]] = bu[:, 0:S5_LW]
        x_scr[:, chunks[q][1]] = bu[:, S5_LW:]

    project_in(0)
    end_r, end_i = [], []
    for q, (re, im) in enumerate(chunks):
        if q + 1 < len(chunks):
            project_in(q + 1)
        a1r, a1i = pw_ref[0:SUBLANES, re], pw_ref[0:SUBLANES, im]
        xr = jnp.zeros((SUBLANES, S5_LW), F32)
        xi = jnp.zeros((SUBLANES, S5_LW), F32)
        for j in range(jn):
            rows = slice(SUBLANES * j, SUBLANES * (j + 1))
            nr = a1r * xr - a1i * xi + x_scr[rows, re]
            ni = a1r * xi + a1i * xr + x_scr[rows, im]
            xr, xi = nr, ni
            x_scr[rows, re] = xr
            x_scr[rows, im] = xi
        end_r.append(xr)
        end_i.append(xi)
    er, ei = jnp.concatenate(end_r, axis=1), jnp.concatenate(end_i, axis=1)
    last = SUBLANES * (jn - 1)
    ajr, aji = pw_ref[last:last + 1, 0:n], pw_ref[last:last + 1, n:]
    pr, pi = st_scr[0:1, 0:n], st_scr[0:1, n:]
    cin_r, cin_i = [], []
    for s in range(SUBLANES):
        cin_r.append(pr)
        cin_i.append(pi)
        nr = er[s:s + 1] + ajr * pr - aji * pi
        ni = ei[s:s + 1] + ajr * pi + aji * pr
        pr, pi = nr, ni
    st_scr[0:1, 0:n] = pr
    st_scr[0:1, n:] = pi
    cr_all = jnp.concatenate(cin_r, axis=0)
    ci_all = jnp.concatenate(cin_i, axis=0)
    pack = 2 * SUBLANES
    ys = []
    for q, (re, im) in enumerate(chunks):
        cr = jnp.concatenate([cr_all[:, re]] * 2, axis=0).astype(BF16)
        ci = jnp.concatenate([ci_all[:, re]] * 2, axis=0).astype(BF16)
        xr_parts, xi_parts = [], []
        for m in range(SUBLANES * jn // pack):
            rows = slice(pack * m, pack * (m + 1))
            pjr, pji = pw16_ref[rows, re], pw16_ref[rows, im]
            xr_parts.append(x_scr[rows, re].astype(BF16) + (pjr * cr - pji * ci))
            xi_parts.append(x_scr[rows, im].astype(BF16) + (pjr * ci + pji * cr))
        xcat = jnp.concatenate([jnp.concatenate(xr_parts, axis=0), jnp.concatenate(xi_parts, axis=0)], axis=1)
        ys.append(jnp.dot(xcat, cd_ref[q], preferred_element_type=F32))

    y_perm = jnp.concatenate(ys, axis=1)
    y_hi = y_perm.astype(BF16)
    y_lo = (y_perm - y_hi.astype(F32)).astype(BF16)
    y = (jnp.dot(pmt_ref[...], y_hi, preferred_element_type=F32)
         + jnp.dot(pmt_ref[...], y_lo, preferred_element_type=F32))
    y = _gelu_tanh(y + dsk_ref[...] * u)
    z = jnp.dot(y.astype(BF16), gw_ref[...], preferred_element_type=F32) + gb_ref[...]
    o_ref[tile_rows, :] = (y * _sigmoid(z)).astype(o_ref.dtype)


def _s5_tables(a_re, a_im, log_dt, b_re, b_im, c_re, c_im):
    g, p = a_re.shape
    dt = jnp.exp(log_dt.astype(F32))[:, None]
    ar, ai = a_re.astype(F32), a_im.astype(F32)
    mag = jnp.exp(dt * ar)
    abar_r, abar_i = mag * jnp.cos(dt * ai), mag * jnp.sin(dt * ai)
    den = ar * ar + ai * ai
    fr = ((abar_r - 1.0) * ar + abar_i * ai) / den
    fi = (abar_i * ar - (abar_r - 1.0) * ai) / den
    br, bi = b_re.astype(F32), b_im.astype(F32)
    bbar_r = fr[..., None] * br - fi[..., None] * bi
    bbar_i = fr[..., None] * bi + fi[..., None] * br
    m = br.shape[-1]
    gc = S5_LW // p
    nq = g // gc
    eye = jnp.eye(gc, dtype=F32)
    dense_b = lambda t: jnp.einsum('qgpm,gh->qgmhp', t.reshape(nq, gc, p, m), eye).reshape(nq, gc * m, gc * p)
    bd = jnp.concatenate([dense_b(bbar_r), dense_b(bbar_i)], axis=2)
    dense_c = lambda t: jnp.einsum('qgmp,gh->qgphm', t.reshape(nq, gc, m, p), eye).reshape(nq, gc * p, gc * m)
    cd = jnp.concatenate([dense_c(c_re.astype(F32)), -dense_c(c_im.astype(F32))], axis=1)
    kk = jnp.arange(1, S5_J + 1, dtype=F32)[:, None, None]
    pmag = jnp.exp(kk * (dt * ar)[None])
    pw_r = (pmag * jnp.cos(kk * (dt * ai)[None])).reshape(S5_J, g * p)
    pw_i = (pmag * jnp.sin(kk * (dt * ai)[None])).reshape(S5_J, g * p)
    pw = jnp.repeat(jnp.concatenate([pw_r, pw_i], axis=1), SUBLANES, axis=0)
    return bd.astype(BF16), pw, cd.astype(BF16)


def _s5_layer(u, bsz, seq, a_re, a_im, log_dt, b_re, b_im, c_re, c_im, d_skip, glu_w, glu_b):
    t, width = u.shape
    n = a_re.shape[0] * a_re.shape[1]
    bd, pw, cd = _s5_tables(a_re, a_im, log_dt, b_re, b_im, c_re, c_im)
    tile = SUBLANES * S5_J
    step_rows = S5_TILES_PER_STEP * tile
    steps_per_seq = seq // step_rows
    const = lambda b, i: (0, 0)
    perm = _sublane_major_perm(tile)
    return pl.pallas_call(
        functools.partial(_s5_kernel, n_state=n),
        out_shape=jax.ShapeDtypeStruct((t, width), BF16),
        grid=(bsz, steps_per_seq),
        in_specs=[pl.BlockSpec((step_rows, width), lambda b, i: (b * steps_per_seq + i, 0)),
                  pl.BlockSpec((tile, tile), const),
                  pl.BlockSpec((tile, tile), const),
                  pl.BlockSpec(bd.shape, lambda b, i: (0, 0, 0)),
                  pl.BlockSpec((tile, 2 * n), const),
                  pl.BlockSpec((tile, 2 * n), const),
                  pl.BlockSpec(cd.shape, lambda b, i: (0, 0, 0)),
                  pl.BlockSpec((1, width), const),
                  pl.BlockSpec((width, width), const),
                  pl.BlockSpec((1, width), const)],
        out_specs=pl.BlockSpec((step_rows, width), lambda b, i: (b * steps_per_seq + i, 0)),
        scratch_shapes=[pltpu.VMEM((tile, 2 * n), F32), pltpu.VMEM((SUBLANES, 2 * n), F32)],
        compiler_params=_cparams(("arbitrary", "arbitrary")),
        name="s5_layer",
    )(u, jnp.asarray(perm, BF16), jnp.asarray(perm.T, BF16), bd, pw, pw.astype(BF16), cd,
      d_skip.reshape(1, width).astype(F32), glu_w.astype(BF16),
      glu_b.reshape(1, width).astype(F32))


def _finish(y, x_ref, mod_ref, pg_ref, out_ref, d_model):
    ms = jnp.mean(y * y, axis=-1, keepdims=True)
    yn = y * lax.rsqrt(ms + EPS) * pg_ref[...]
    gate_mod = mod_ref[0, :, 2 * d_model:3 * d_model]
    out_ref[...] = x_ref[...] + gate_mod * yn


def _split2(x):
    hi = x.astype(BF16)
    return hi, (x - hi.astype(F32)).astype(BF16)


def _out_ab_kernel(o0_ref, l0_ref, o1_ref, o2_ref, l1_ref, l2_ref, ob_ref, gate_ref, e_ref, un_ref, w_ref,
                   x_ref, mod_ref, pg_ref, out_ref, *, d_model):
    flat = lambda ref: ref[...].reshape(ROW_TILE, ref.shape[-1])
    aw = o0_ref.shape[-1]

    def unperm(o_ref, l_ref):
        parts = jnp.concatenate([flat(o_ref)] + list(_split2(flat(l_ref))), axis=1)
        moved = jnp.dot(un_ref[...], parts, preferred_element_type=F32)
        return moved[:, 0:aw], moved[:, aw:aw + LANES] + moved[:, aw + LANES:]

    o0, l0 = o0_ref[...].astype(F32), l0_ref[...]
    o1, l1 = unperm(o1_ref, l1_ref)
    o2, l2 = unperm(o2_ref, l2_ref)
    mx = jnp.maximum(jnp.maximum(l0, l1), l2)
    e0, e1, e2 = jnp.exp2(l0 - mx), jnp.exp2(l1 - mx), jnp.exp2(l2 - mx)
    inv_den = 1.0 / (e0 + e1 + e2)
    stacked = jnp.concatenate(list(_split2(e1 * inv_den)) + list(_split2(e2 * inv_den)), axis=0)
    wide = jnp.dot(stacked, e_ref[...], preferred_element_type=F32)
    term = lambda i: wide[i * ROW_TILE:(i + 1) * ROW_TILE]
    w1 = term(0) + term(1)
    w2 = term(2) + term(3)
    o_a = (1.0 - w1 - w2) * o0 + w1 * o1 + w2 * o2
    gate = gate_ref[...].astype(F32)
    sg = gate * _sigmoid(gate)
    aw = o_a.shape[-1]
    y = jnp.dot((o_a * sg[:, :aw]).astype(BF16), w_ref[0:aw, :], preferred_element_type=F32)
    y = y + jnp.dot((ob_ref[...].astype(F32) * sg[:, aw:]).astype(BF16), w_ref[aw:, :],
                    preferred_element_type=F32)
    _finish(y, x_ref, mod_ref, pg_ref, out_ref, d_model)


def _out_c_kernel(o_ref, gate_ref, w_ref, x_ref, mod_ref, pg_ref, out_ref, *, d_model):
    gate = gate_ref[...].astype(F32)
    o = o_ref[...].astype(F32) * (gate * _sigmoid(gate))
    y = jnp.dot(o.astype(BF16), w_ref[...], preferred_element_type=F32)
    _finish(y, x_ref, mod_ref, pg_ref, out_ref, d_model)


def _out_proj(kern, row_inputs, const_inputs, w_bf16, x2, mod3, post_g, seq, name):
    t, d = x2.shape
    tiles_per_seq = seq // ROW_TILE
    tiles_per_span = SPAN // ROW_TILE

    def row_spec(a):
        if a.ndim == 2:
            return pl.BlockSpec((ROW_TILE, a.shape[1]), lambda i: (i, 0))
        return pl.BlockSpec((None, SPAN_RES, SPAN_RUN, a.shape[-1]),
                            lambda i: (i // tiles_per_span, 0, i % tiles_per_span, 0))

    const_spec = lambda a: pl.BlockSpec(a.shape, lambda i: (0, 0))
    return pl.pallas_call(
        functools.partial(kern, d_model=d),
        out_shape=jax.ShapeDtypeStruct((t, d), F32),
        grid=(t // ROW_TILE,),
        in_specs=([row_spec(a) for a in row_inputs] + [const_spec(a) for a in const_inputs]
                  + [const_spec(w_bf16), row_spec(x2),
                     pl.BlockSpec((1, 1, 3 * d), lambda i: (i // tiles_per_seq, 0, 0)),
                     pl.BlockSpec((1, d), lambda i: (0, 0))]),
        out_specs=pl.BlockSpec((ROW_TILE, d), lambda i: (i, 0)),
        compiler_params=_cparams(("arbitrary",)),
        name=name,
    )(*row_inputs, *const_inputs, w_bf16, x2, mod3, post_g.reshape(1, d))


def _gdn_prep_kernel(act_ref, unperm_ref, br_ref, alog_ref, dtb_ref, w_ref, u_ref, qg_ref, kdt_ref, aqk_ref,
                     dec_ref):
    act = jnp.dot(unperm_ref[...], act_ref[...], preferred_element_type=F32)
    for blk in range(GDN_PREP_BLOCKS):
        _gdn_prep_block(blk, act, br_ref, alog_ref, dtb_ref, w_ref, u_ref, qg_ref, kdt_ref, aqk_ref, dec_ref)


def _gdn_prep_block(blk, act, br_ref, alog_ref, dtb_ref, w_ref, u_ref, qg_ref, kdt_ref, aqk_ref, dec_ref):
    n = GDN_STEP
    rows = slice(blk * n, (blk + 1) * n)
    head_cols = lambda base, h: act[rows, base + h * C_DK:base + (h + 1) * C_DK]

    row = lax.broadcasted_iota(jnp.int32, (n, n), 0)
    col = lax.broadcasted_iota(jnp.int32, (n, n), 1)
    same = (row // GDN_CHUNK) == (col // GDN_CHUNK)
    tril = jnp.logical_and(same, row >= col)
    strict = jnp.logical_and(same, row > col)
    eye = (row == col).astype(F32)

    br = br_ref[rows, :]
    beta_all = _sigmoid(br)
    xg = br + dtb_ref[...]
    softplus = jnp.maximum(xg, 0.0) + jnp.log(1.0 + jnp.exp(-jnp.abs(xg)))
    g_all = -jnp.exp(alog_ref[...]) * softplus
    gc_all = jnp.dot(tril.astype(F32), g_all, preferred_element_type=F32, precision=HI)
    gc_t = gc_all.T
    first_chunk = lax.broadcasted_iota(jnp.int32, (n, 1), 0) < GDN_CHUNK
    dec_row = lax.broadcasted_iota(jnp.int32, (SUBLANES, LANES), 0)

    heads = range(C_HEADS)
    lanes = [slice(h * LANES, (h + 1) * LANES) for h in heads]
    gcols = [gc_all[:, C_HEADS + h:C_HEADS + h + 1] for h in heads]
    egs = [jnp.exp(g) for g in gcols]
    kbs, amats, invs = [], [], []
    for h in heads:
        grow = gc_t[C_HEADS + h:C_HEADS + h + 1, :]
        decay = jnp.where(tril, jnp.exp(jnp.where(tril, gcols[h] - grow, 0.0)), 0.0)
        q = head_cols(0, h)
        k = head_cols(C_HEADS * C_DK, h)
        q = q * lax.rsqrt(jnp.sum(q * q, axis=-1, keepdims=True) + EPS) * (C_DK ** -0.5)
        k = k * lax.rsqrt(jnp.sum(k * k, axis=-1, keepdims=True) + EPS)
        kb = k * beta_all[:, h:h + 1]
        k16 = k.astype(BF16)
        a = jnp.where(strict, lax.dot_general(kb.astype(BF16), k16, _NT, preferred_element_type=F32) * decay, 0.0)
        aqk = lax.dot_general(q.astype(BF16), k16, _NT, preferred_element_type=F32) * decay
        aqk_ref[rows, lanes[h]] = aqk.astype(BF16)
        qg_ref[rows, lanes[h]] = (q * egs[h]).astype(BF16)
        g_last = jnp.where(first_chunk, gcols[h][GDN_CHUNK - 1:GDN_CHUNK], gcols[h][n - 1:n])
        kdt_ref[rows, lanes[h]] = (k * jnp.exp(g_last - gcols[h])).T.astype(BF16)
        dec_ref[blk, :, lanes[h]] = jnp.where(dec_row < SUBLANES // 2, egs[h][GDN_CHUNK - 1:GDN_CHUNK],
                                             egs[h][n - 1:n])
        kbs.append(kb)
        amats.append(a)
        invs.append(eye - a)
    pows = []
    for h in heads:
        a16 = amats[h].astype(BF16)
        pows.append(jnp.dot(a16, a16, preferred_element_type=F32))
    for _ in range(4):
        for h in heads:
            p16 = pows[h].astype(BF16)
            both = jnp.dot(jnp.concatenate([invs[h].astype(BF16), p16], axis=0), p16, preferred_element_type=F32)
            invs[h] = invs[h] + both[0:n]
            pows[h] = both[n:]
    for h in heads:
        invs[h] = invs[h] + jnp.dot(invs[h].astype(BF16), pows[h].astype(BF16), preferred_element_type=F32)
    for h in heads:
        v = head_cols(2 * C_HEADS * C_DK, h)
        rhs = jnp.concatenate([v * beta_all[:, h:h + 1], kbs[h] * egs[h]], axis=1)
        uw = jnp.dot(invs[h].astype(BF16), rhs.astype(BF16), preferred_element_type=F32)
        u_ref[rows, lanes[h]] = uw[:, 0:C_DV].astype(u_ref.dtype)
        w_ref[rows, lanes[h]] = uw[:, C_DV:].astype(BF16)


def _gdn_rec_kernel(w_ref, u_ref, qg_ref, kdt_ref, aqk_ref, dec_ref, ng_ref, o_ref, s_scr):
    @pl.when(pl.program_id(0) == 0)
    def _():
        s_scr[...] = jnp.zeros_like(s_scr)

    half = SUBLANES // 2
    zeros = jnp.zeros((GDN_CHUNK, C_DV), BF16)
    lanes = [slice(h * LANES, (h + 1) * LANES) for h in range(C_HEADS)]
    chains = [(b, h) for b in range(w_ref.shape[0]) for h in range(C_HEADS)]
    states = [s_scr[b * C_HEADS + h] for b, h in chains]
    for blk in range(GDN_REC_BLOCKS):
        blk_rows = slice(blk * GDN_STEP, (blk + 1) * GDN_STEP)
        for j in range(GDN_STEP // GDN_CHUNK):
            rows = slice(blk * GDN_STEP + j * GDN_CHUNK, blk * GDN_STEP + (j + 1) * GDN_CHUNK)
            wss = []
            for c, (b, h) in enumerate(chains):
                wq = jnp.concatenate([w_ref[b, rows, lanes[h]], qg_ref[b, rows, lanes[h]]], axis=0)
                wss.append(jnp.dot(wq, states[c].astype(BF16), preferred_element_type=F32))
            for c, (b, h) in enumerate(chains):
                v_new = (u_ref[b, rows, lanes[h]].astype(F32) - wss[c][0:GDN_CHUNK]).astype(BF16)
                v_pad = jnp.concatenate([v_new, zeros] if j == 0 else [zeros, v_new], axis=0)
                both = jnp.dot(jnp.concatenate([aqk_ref[b, rows, lanes[h]], kdt_ref[b, blk_rows, lanes[h]]], axis=0),
                               v_pad, preferred_element_type=F32)
                o = wss[c][GDN_CHUNK:] + both[0:GDN_CHUNK]
                dec = dec_ref[b, blk, j * half:j * half + 1, lanes[h]]
                states[c] = states[c] * dec + both[GDN_CHUNK:]
                ms = jnp.mean(o * o, axis=-1, keepdims=True)
                o_ref[b, rows, lanes[h]] = (o * lax.rsqrt(ms + EPS) * ng_ref[...]).astype(o_ref.dtype)
    for c, (b, h) in enumerate(chains):
        s_scr[b * C_HEADS + h] = states[c]


def _gdn_core(act, br, bsz, seq, a_log, dt_bias, norm_g):
    t, qkv_w = act.shape
    hw = C_HEADS * C_DV
    steps_per_seq = seq // GDN_STEP
    prep_rows = GDN_PREP_BLOCKS * GDN_STEP
    assert prep_rows == CONV_TILE
    prep_steps = steps_per_seq // GDN_PREP_BLOCKS
    unperm = jnp.asarray(_sublane_major_perm(CONV_TILE).T, BF16)
    pad_row = lambda vec: jnp.zeros((1, LANES), F32).at[0, C_HEADS:2 * C_HEADS].set(vec.astype(F32))
    tok = lambda b, i: (b * prep_steps + i, 0)
    const = lambda b, i: (0, 0)
    tok_spec = pl.BlockSpec((prep_rows, hw), tok)
    dec_spec = pl.BlockSpec((GDN_PREP_BLOCKS, SUBLANES, hw), lambda b, i: (b * prep_steps + i, 0, 0))
    w, u, qg, kdt, aqk, dec = pl.pallas_call(
        _gdn_prep_kernel,
        out_shape=[jax.ShapeDtypeStruct((t, hw), BF16), jax.ShapeDtypeStruct((t, hw), BF16),
                   jax.ShapeDtypeStruct((t, hw), BF16), jax.ShapeDtypeStruct((t, hw), BF16),
                   jax.ShapeDtypeStruct((t, hw), BF16),
                   jax.ShapeDtypeStruct((t // GDN_STEP, SUBLANES, hw), F32)],
        grid=(bsz, prep_steps),
        in_specs=[pl.BlockSpec((prep_rows, qkv_w), tok),
                  pl.BlockSpec((prep_rows, prep_rows), const),
                  pl.BlockSpec((prep_rows, LANES), tok),
                  pl.BlockSpec((1, LANES), const),
                  pl.BlockSpec((1, LANES), const)],
        out_specs=[tok_spec, tok_spec, tok_spec, tok_spec, tok_spec, dec_spec],
        compiler_params=_cparams(("arbitrary", "arbitrary")),
        name="gdn_prep",
    )(act, unperm, br, pad_row(a_log), pad_row(dt_bias))
    rec_steps = steps_per_seq // GDN_REC_BLOCKS
    seq_view = lambda a: a.reshape(bsz, seq, hw)
    rec_tok = pl.BlockSpec((bsz, GDN_REC_BLOCKS * GDN_STEP, hw), lambda i: (0, i, 0))
    rec_dec = pl.BlockSpec((bsz, GDN_REC_BLOCKS, SUBLANES, hw), lambda i: (0, i, 0, 0))
    o = pl.pallas_call(
        _gdn_rec_kernel,
        out_shape=jax.ShapeDtypeStruct((bsz, seq, hw), BF16),
        grid=(rec_steps,),
        in_specs=[rec_tok, rec_tok, rec_tok, rec_tok, rec_tok, rec_dec,
                  pl.BlockSpec((1, C_DV), lambda i: (0, 0))],
        out_specs=rec_tok,
        scratch_shapes=[pltpu.VMEM((bsz * C_HEADS, C_DK, C_DV), F32)],
        compiler_params=_cparams(("arbitrary",)),
        name="gdn_recurrence",
    )(seq_view(w), seq_view(u), seq_view(qg), seq_view(kdt), seq_view(aqk),
      dec.reshape(bsz, steps_per_seq, SUBLANES, hw), norm_g.reshape(1, C_DV).astype(F32))
    return o.reshape(t, hw)


def _ab_layer(x2, mod3, bsz, seq, pre_g, post_g, rel_bias, w_in, w_out, s5_params):
    b_width = s5_params[7].shape[-1]
    assert DILATED_CONFIGS[0][1] == 1 and all(SPAN_RES % dl == 0 for _, dl in DILATED_CONFIGS) and seq % SPAN == 0
    head_expand = jnp.asarray(np.arange(LANES)[:, None] == (np.arange(A_WIDTH)[None, :] // A_HEAD_DIM), BF16)
    unperm = jnp.asarray(_span_perm().T, BF16)
    splits = ((0, A_WIDTH, A_HEAD_DIM ** -0.5 * LOG2_E), (A_WIDTH, A_WIDTH, 1.0), (2 * A_WIDTH, A_WIDTH, 1.0),
              (3 * A_WIDTH, b_width, 1.0), (3 * A_WIDTH + b_width, A_WIDTH + b_width, 1.0))
    q, k, v, u, gate, qs, ks, vs = _in_proj(x2, mod3, pre_g, w_in.astype(BF16), splits,
                                            (BF16, BF16, BF16, F32, BF16), seq, n_span=3)
    outs, lses = [], []
    bias_tabs = _attn_bias_tables(rel_bias)
    for cfg, (_, dil) in enumerate(DILATED_CONFIGS):
        qkv = (q, k, v) if dil == 1 else (qs, ks, vs)
        o_c, lse_c = _dilated_attention_one(*qkv, bias_tabs[cfg], bsz, seq, dil)
        outs.append(o_c)
        lses.append(lse_c)
    o_b = _s5_layer(u, bsz, seq, *s5_params)
    row_inputs = [outs[0], lses[0], outs[1], outs[2], lses[1], lses[2], o_b, gate]
    return _out_proj(_out_ab_kernel, row_inputs, [head_expand, unperm], w_out.astype(BF16),
                     x2, mod3, post_g, seq, "out_proj_ab")


def _gdn_layer(x2, mod3, bsz, seq, pre_g, post_g, w_in, conv_w, a_log, dt_bias, norm_g, w_out):
    d = x2.shape[1]
    qkv_w = 2 * C_HEADS * C_DK + C_HEADS * C_DV
    gate_w = C_HEADS * C_DV
    w_pad = jnp.concatenate(
        [w_in, jnp.zeros((d, LANES - (w_in.shape[1] - qkv_w - gate_w)), w_in.dtype)], axis=1).astype(BF16)
    act, gate, br = _in_proj_gdn(x2, mod3, pre_g, w_pad, conv_w, seq, qkv_w, gate_w)
    o = _gdn_core(act, br, bsz, seq, a_log, dt_bias, norm_g)
    return _out_proj(_out_c_kernel, [o, gate], [], w_out.astype(BF16), x2, mod3, post_g, seq, "out_proj_c")


def kernel(x, c, ada_w, ada_b, pre_g, post_g, rel_bias, ab_w_in, ab_w_out, s5_a_re, s5_a_im, s5_log_dt, s5_b_re, s5_b_im, s5_c_re, s5_c_im, s5_d, s5_glu_w, s5_glu_b, gdn_w_in, gdn_conv, gdn_a_log, gdn_dt_bias, gdn_norm_g, gdn_w_out):
    bsz, seq, d = x.shape
    depth = ada_w.shape[0]
    assert seq % ROW_TILE == 0 and seq % (SUBLANES * S5_J * S5_TILES_PER_STEP) == 0
    assert seq % (GDN_STEP * GDN_PREP_BLOCKS) == 0 and seq % (GDN_STEP * GDN_REC_BLOCKS) == 0
    x2 = x.reshape(bsz * seq, d)
    mod = _adaln_mod(c, ada_w, ada_b)
    for layer in range(depth):
        j = layer // 2
        mod3 = mod[layer].reshape(bsz, 1, 3 * d)
        if layer % 2 == 0:
            s5_params = (s5_a_re[j], s5_a_im[j], s5_log_dt[j], s5_b_re[j], s5_b_im[j], s5_c_re[j], s5_c_im[j],
                         s5_d[j], s5_glu_w[j], s5_glu_b[j])
            x2 = _ab_layer(x2, mod3, bsz, seq, pre_g[layer], post_g[layer], rel_bias, ab_w_in[j], ab_w_out[j],
                           s5_params)
        else:
            x2 = _gdn_layer(x2, mod3, bsz, seq, pre_g[layer], post_g[layer], gdn_w_in[j], gdn_conv[j],
                            gdn_a_log[j], gdn_dt_bias[j], gdn_norm_g[j], gdn_w_out[j])
    return x2.reshape(bsz, seq, d)
```

```python
import functools
import math

import numpy as np
import jax
import jax.numpy as jnp
from jax import lax
from jax.experimental import pallas as pl
from jax.experimental.pallas import tpu as pltpu

F32 = jnp.float32
BF16 = jnp.bfloat16
HI = lax.Precision.HIGHEST

EPS = 1e-6
A_HEADS = 8
A_HEAD_DIM = 64
A_WIDTH = A_HEADS * A_HEAD_DIM
A_BLOCK = 128
DILATED_CONFIGS = ((128, 1), (512, 4), (2048, 16))
B_GROUP = 16
B_STATE = 64
REL_BUCKETS = 32
REL_MAX_DIST = 2048
C_HEADS = 8
C_DK = 128
C_DV = 128
C_CONV = 4
MASK_NEG = -1e30
LOG2_E = math.log2(math.e)

LANES = 128
SUBLANES = 8
VMEM_LIMIT = 48 * 1024 * 1024
ROW_TILE = 512
OUT_C_TILE = 1024
COL_CHUNK = 512
SPAN_RES = max(dl for _, dl in DILATED_CONFIGS)
SPAN = A_BLOCK * SPAN_RES
SPAN_RUN = ROW_TILE // SPAN_RES
ATTN_BLOCKS_PER_STEP = 8
S5_J = 32
S5_LW = 512
S5_TILES_PER_STEP = 4
GDN_STEP = 128
GDN_CHUNK = 64
GDN_PREP_BLOCKS = 4
GDN_REC_BLOCKS = 2

_NT = (((1,), (1,)), ((), ()))


def _cparams(sem):
    return pltpu.CompilerParams(dimension_semantics=sem, vmem_limit_bytes=VMEM_LIMIT)


def _sigmoid(x):
    return 1.0 / (1.0 + jnp.exp(-x))


def _mod_kernel(c_ref, w_ref, b_ref, o_ref):
    c = c_ref[...]
    ca = c * _sigmoid(c)
    o_ref[0] = jnp.dot(ca.astype(BF16), w_ref[0].astype(BF16), preferred_element_type=F32) + b_ref[0]


def _adaln_mod(c, ada_w, ada_b):
    depth, d, d3 = ada_w.shape
    bsz = c.shape[0]
    return pl.pallas_call(
        _mod_kernel,
        out_shape=jax.ShapeDtypeStruct((depth, bsz, d3), F32),
        grid=(depth, d3 // d),
        in_specs=[pl.BlockSpec((bsz, d), lambda l, j: (0, 0)),
                  pl.BlockSpec((1, d, d), lambda l, j: (l, 0, j)),
                  pl.BlockSpec((1, 1, d), lambda l, j: (l, 0, j))],
        out_specs=pl.BlockSpec((1, bsz, d), lambda l, j: (l, 0, j)),
        compiler_params=_cparams(("arbitrary", "arbitrary")),
        name="adaln_mod",
    )(c, ada_w, ada_b.reshape(depth, 1, d3))


def _span_perm():
    rho = np.arange(ROW_TILE)
    nat = SPAN_RES * (rho % SPAN_RUN) + rho // SPAN_RUN
    return (nat[:, None] == np.arange(ROW_TILE)[None, :]).astype(np.float32)


def _modulated_norm(x_ref, mod_ref, g_ref, d_model):
    x = x_ref[...]
    ms = jnp.mean(x * x, axis=-1, keepdims=True)
    y = x * lax.rsqrt(ms + EPS) * g_ref[...]
    shift = mod_ref[0, :, 0:d_model]
    scale = mod_ref[0, :, d_model:2 * d_model]
    return (y * (1.0 + scale) + shift).astype(BF16)


def _sublane_major_perm(tile):
    rho = np.arange(tile)
    src = (rho % SUBLANES) * (tile // SUBLANES) + rho // SUBLANES
    return (src[:, None] == np.arange(tile)[None, :]).astype(np.float32)


def _in_proj_kernel(x_ref, mod_ref, g_ref, w_ref, *rest, splits, d_model, n_span):
    pm_ref = rest[0] if n_span else None
    out_refs = rest[1:] if n_span else rest
    span_refs = out_refs[len(splits):]
    h = _modulated_norm(x_ref, mod_ref, g_ref, d_model)
    for idx, ((c0, width, mult), o_ref) in enumerate(zip(splits, out_refs)):
        for cc in range(0, width, COL_CHUNK):
            cw = min(COL_CHUNK, width - cc)
            acc = jnp.dot(h, w_ref[:, c0 + cc:c0 + cc + cw], preferred_element_type=F32)
            if mult != 1.0:
                acc = acc * mult
            val = acc.astype(o_ref.dtype)
            o_ref[:, cc:cc + cw] = val
            if idx < n_span:
                moved = jnp.dot(pm_ref[...], val, preferred_element_type=F32).astype(BF16)
                span_refs[idx][:, :, cc:cc + cw] = moved.reshape(SPAN_RES, SPAN_RUN, cw)


def _in_proj(x2, mod3, gain, w_bf16, splits, out_dtypes, seq, n_span=0):
    t, d = x2.shape
    tiles_per_seq = seq // ROW_TILE
    tiles_per_span = SPAN // ROW_TILE
    n_w = w_bf16.shape[1]
    row = lambda wd: pl.BlockSpec((ROW_TILE, wd), lambda i: (i, 0))
    span_shape = lambda wd: jax.ShapeDtypeStruct((t // SPAN, SPAN_RES, A_BLOCK, wd), BF16)
    span_spec = lambda wd: pl.BlockSpec((None, SPAN_RES, SPAN_RUN, wd),
                                        lambda i: (i // tiles_per_span, 0, i % tiles_per_span, 0))
    perm_in = [jnp.asarray(_span_perm(), BF16)] if n_span else []
    perm_spec = [pl.BlockSpec((ROW_TILE, ROW_TILE), lambda i: (0, 0))] if n_span else []
    return pl.pallas_call(
        functools.partial(_in_proj_kernel, splits=splits, d_model=d, n_span=n_span),
        out_shape=([jax.ShapeDtypeStruct((t, wd), dt) for (_, wd, _), dt in zip(splits, out_dtypes)]
                   + [span_shape(wd) for (_, wd, _) in splits[:n_span]]),
        grid=(t // ROW_TILE,),
        in_specs=[row(d),
                  pl.BlockSpec((1, 1, 3 * d), lambda i: (i // tiles_per_seq, 0, 0)),
                  pl.BlockSpec((1, d), lambda i: (0, 0)),
                  pl.BlockSpec((d, n_w), lambda i: (0, 0))] + perm_spec,
        out_specs=[row(wd) for (_, wd, _) in splits] + [span_spec(wd) for (_, wd, _) in splits[:n_span]],
        compiler_params=_cparams(("arbitrary",)),
        name="in_proj",
    )(x2, mod3, gain.reshape(1, d), w_bf16, *perm_in)


def _t5_bucket(dist):
    dist = np.maximum(dist, 0)
    max_exact = REL_BUCKETS // 2
    large = max_exact + (np.log(np.maximum(dist, 1) / max_exact)
                         / math.log(REL_MAX_DIST / max_exact) * (REL_BUCKETS - max_exact)).astype(np.int32)
    large = np.minimum(large, REL_BUCKETS - 1)
    return np.where(dist < max_exact, dist, large).astype(np.int32)


def _bias_kernel(rb_ref, bucket_ref, mask_ref, o_ref):
    def body(r, carry):
        rows = pl.ds(pl.multiple_of(r * SUBLANES, SUBLANES), SUBLANES)
        bk = bucket_ref[0, rows, :]
        accs = [jnp.zeros(bk.shape, F32) for _ in range(A_HEADS)]
        for b in range(REL_BUCKETS):
            eq = bk == b
            accs = [jnp.where(eq, rb_ref[b, h], acc) for h, acc in enumerate(accs)]
        for f in range(2):
            keep = mask_ref[0, f, rows, :] != 0
            for h in range(A_HEADS):
                o_ref[0, f, h, rows, :] = jnp.where(keep, accs[h] * LOG2_E, MASK_NEG)
        return carry
    lax.fori_loop(0, A_BLOCK // SUBLANES, body, 0)


def _attn_bias_tables(rel_bias):
    qi = np.arange(A_BLOCK)[:, None]
    kj = np.arange(2 * A_BLOCK)[None, :]
    rel = qi + A_BLOCK - kj
    buckets, masks = [], []
    for window, dil in DILATED_CONFIGS:
        band = (rel >= 0) & (rel <= window // dil)
        bucket = _t5_bucket(rel * dil)
        mask = np.stack([band & (kj >= A_BLOCK), band]).astype(np.int32)
        if dil > 1:
            runs = SPAN_RES // dil
            run = A_BLOCK // runs
            rho = np.arange(A_BLOCK)
            sub = runs * (rho % run) + rho // run
            keys = np.concatenate([sub, A_BLOCK + sub])
            bucket = bucket[sub][:, keys]
            mask = mask[:, sub][:, :, keys]
        buckets.append(bucket)
        masks.append(mask)
    n_cfg = len(DILATED_CONFIGS)
    return pl.pallas_call(
        _bias_kernel,
        out_shape=jax.ShapeDtypeStruct((n_cfg, 2, A_HEADS, A_BLOCK, 2 * A_BLOCK), F32),
        grid=(n_cfg,),
        in_specs=[pl.BlockSpec(memory_space=pltpu.SMEM),
                  pl.BlockSpec((1, A_BLOCK, 2 * A_BLOCK), lambda c: (c, 0, 0)),
                  pl.BlockSpec((1, 2, A_BLOCK, 2 * A_BLOCK), lambda c: (c, 0, 0, 0))],
        out_specs=pl.BlockSpec((1, 2, A_HEADS, A_BLOCK, 2 * A_BLOCK), lambda c: (c, 0, 0, 0, 0)),
        compiler_params=_cparams(("arbitrary",)),
        name="attn_bias",
    )(rel_bias.astype(F32), jnp.asarray(np.stack(buckets)), jnp.asarray(np.stack(masks)))


def _attn_block(q, kcat, vcat, bias_ref):
    lane = lax.broadcasted_iota(jnp.int32, (A_BLOCK, LANES), 1)
    low = lane < A_HEAD_DIM
    lse_tile = jnp.zeros((A_BLOCK, LANES), F32)
    zero = jnp.zeros((A_BLOCK, LANES), BF16)
    heads = range(A_HEADS)
    pair = [slice((h // 2) * LANES, (h // 2 + 1) * LANES) for h in heads]
    scores = []
    for h in heads:
        qm = jnp.where(low if h % 2 == 0 else jnp.logical_not(low), q[:, pair[h]], zero)
        scores.append(lax.dot_general(qm, kcat[:, pair[h]], _NT, preferred_element_type=F32) + bias_ref[0, h])
    probs, inv_l = [], []
    for h in heads:
        m = jnp.max(scores[h], axis=-1, keepdims=True)
        p = jnp.exp2(scores[h] - m)
        l = jnp.sum(p, axis=-1, keepdims=True)
        probs.append(p.astype(BF16))
        inv_l.append(1.0 / l)
        lse_tile = jnp.where(lane == h, m + jnp.log2(l), lse_tile)
    outs = [jnp.dot(probs[h], vcat[:, pair[h]], preferred_element_type=F32) * inv_l[h] for h in heads]
    o = jnp.concatenate([jnp.where(low, outs[2 * hp], outs[2 * hp + 1]) for hp in range(A_HEADS // 2)], axis=1)
    return o, lse_tile


def _attn_kernel(q_ref, kp_ref, kc_ref, vp_ref, vc_ref, bias0_ref, bias_ref, o_ref, lse_ref, *, per_lead):
    def sub(ref, rr, i):
        if ref.ndim == 2:
            return ref[i * A_BLOCK:(i + 1) * A_BLOCK, :]
        run = ref.shape[3] // per_lead
        return ref[i // per_lead, :, rr, (i % per_lead) * run:(i % per_lead + 1) * run, :]

    def put(ref, rr, i, val):
        if ref.ndim == 2:
            ref[i * A_BLOCK:(i + 1) * A_BLOCK, :] = val.astype(ref.dtype)
        else:
            run = ref.shape[3] // per_lead
            ref[i // per_lead, :, rr, (i % per_lead) * run:(i % per_lead + 1) * run, :] = (
                val.astype(ref.dtype).reshape(ref.shape[1], run, ref.shape[4]))

    flat = lambda v: v.reshape(A_BLOCK, v.shape[-1])
    token_order = q_ref.ndim == 2
    n_blocks = q_ref.shape[0] // A_BLOCK if token_order else q_ref.shape[0] * per_lead
    for rr in range(1 if token_order else q_ref.shape[2]):
        k_prev = flat(kp_ref[...] if token_order else kp_ref[0, :, rr])
        v_prev = flat(vp_ref[...] if token_order else vp_ref[0, :, rr])
        for i in range(n_blocks):
            k_cur, v_cur = flat(sub(kc_ref, rr, i)), flat(sub(vc_ref, rr, i))
            o, lse_tile = _attn_block(flat(sub(q_ref, rr, i)), jnp.concatenate([k_prev, k_cur], axis=0),
                                      jnp.concatenate([v_prev, v_cur], axis=0),
                                      bias0_ref if i == 0 else bias_ref)
            put(o_ref, rr, i, o)
            put(lse_ref, rr, i, lse_tile)
            k_prev, v_prev = k_cur, v_cur


def _dilated_attention_one(q, k, v, bias_tab, bsz, seq, dil):
    nb = seq // dil // A_BLOCK
    step = min(ATTN_BLOCKS_PER_STEP, nb)
    w = A_WIDTH
    if dil == 1:
        per_lead = n_res = 1
        view = lambda a: a.reshape(bsz, seq, a.shape[-1])
        block = lambda wd: (None, step * A_BLOCK, wd)
        block_prev = lambda wd: (None, A_BLOCK, wd)
        cur = lambda b, r, m: (b, m, 0)
        prev = lambda b, r, m: (b, jnp.maximum(m * step - 1, 0), 0)
        out_shape = lambda wd, dt: jax.ShapeDtypeStruct((bsz, seq, wd), dt)
    else:
        runs = SPAN_RES // dil
        run = A_BLOCK // runs
        spans_per_seq = seq // SPAN
        per_lead = min(step, runs)
        lead = step // per_lead
        n_res = max(1, ATTN_BLOCKS_PER_STEP // step)
        assert runs % per_lead == 0 and spans_per_seq % lead == 0 and dil % n_res == 0
        view = lambda a: a.reshape(a.shape[0], runs, dil, A_BLOCK, a.shape[-1])
        block = lambda wd: (lead, runs, n_res, per_lead * run, wd)
        block_prev = lambda wd: (1, runs, n_res, run, wd)
        cur = lambda b, r, m: ((b * spans_per_seq + (m * step) // runs) // lead, 0, r, ((m * step) % runs) // per_lead, 0)

        def prev(b, r, m):
            n = jnp.maximum(m * step - 1, 0)
            return (b * spans_per_seq + n // runs, 0, r, n % runs, 0)

        out_shape = lambda wd, dt: jax.ShapeDtypeStruct((bsz * spans_per_seq, runs, dil, A_BLOCK, wd), dt)
    blk = pl.BlockSpec(block(w), cur)
    blk_prev = pl.BlockSpec(block_prev(w), prev)
    bias_block = (1, A_HEADS, A_BLOCK, 2 * A_BLOCK)
    o, lse = pl.pallas_call(
        functools.partial(_attn_kernel, per_lead=per_lead),
        out_shape=[out_shape(w, BF16), out_shape(LANES, F32)],
        grid=(bsz, dil // n_res, nb // step),
        in_specs=[blk, blk_prev, blk, blk_prev, blk,
                  pl.BlockSpec(bias_block, lambda b, r, m: (jnp.minimum(m, 1), 0, 0, 0)),
                  pl.BlockSpec(bias_block, lambda b, r, m: (1, 0, 0, 0))],
        out_specs=[pl.BlockSpec(block(w), cur), pl.BlockSpec(block(LANES), cur)],
        compiler_params=_cparams(("arbitrary", "arbitrary", "arbitrary")),
        name=f"dilated_attn_d{dil}",
    )(view(q), view(k), view(k), view(v), view(v), bias_tab, bias_tab)
    if dil == 1:
        return o.reshape(bsz * seq, w), lse.reshape(bsz * seq, LANES)
    return o.reshape(o.shape[0], SPAN_RES, A_BLOCK, w), lse.reshape(o.shape[0], SPAN_RES, A_BLOCK, LANES)


def _gelu_tanh(x):
    c = math.sqrt(2.0 / math.pi)
    return x * (0.5 * (1.0 + jnp.tanh(c * (x + 0.044715 * (x * x * x)))))


def _s5_kernel(u_ref, pm_ref, pmt_ref, bd_ref, pw_ref, pw16_ref, cd_ref, dsk_ref, gw_ref, gb_ref, o_ref,
               x_scr, st_scr, *, n_state):
    @pl.when(pl.program_id(1) == 0)
    def _():
        st_scr[...] = jnp.zeros_like(st_scr)

    for tile_idx in range(S5_TILES_PER_STEP):
        _s5_tile(tile_idx, u_ref, pm_ref, pmt_ref, bd_ref, pw_ref, pw16_ref, cd_ref, dsk_ref, gw_ref, gb_ref,
                 o_ref, x_scr, st_scr, n_state)


def _s5_tile(tile_idx, u_ref, pm_ref, pmt_ref, bd_ref, pw_ref, pw16_ref, cd_ref, dsk_ref, gw_ref, gb_ref,
             o_ref, x_scr, st_scr, n_state):
    n = n_state
    jn = S5_J
    tile_rows = slice(tile_idx * SUBLANES * jn, (tile_idx + 1) * SUBLANES * jn)

    u = u_ref[tile_rows, :]
    u_perm = jnp.dot(pm_ref[...], u.astype(BF16), preferred_element_type=F32).astype(BF16)
    cw = u.shape[1] * S5_LW // n
    chunks = [(slice(c0, c0 + S5_LW), slice(n + c0, n + c0 + S5_LW)) for c0 in range(0, n, S5_LW)]

    def project_in(q):
        bu = jnp.dot(u_perm[:, q * cw:(q + 1) * cw], bd_ref[q], preferred_element_type=F32)
        x_scr[:, chunks[q][0]] = bu[:, 0:S5_LW]
        x_scr[:, chunks[q][1]] = bu[:, S5_LW:]

    project_in(0)
    end_r, end_i = [], []
    for q, (re, im) in enumerate(chunks):
        if q + 1 < len(chunks):
            project_in(q + 1)
        a1r, a1i = pw_ref[0:SUBLANES, re], pw_ref[0:SUBLANES, im]
        xr = jnp.zeros((SUBLANES, S5_LW), F32)
        xi = jnp.zeros((SUBLANES, S5_LW), F32)
        for j in range(jn):
            rows = slice(SUBLANES * j, SUBLANES * (j + 1))
            nr = a1r * xr - a1i * xi + x_scr[rows, re]
            ni = a1r * xi + a1i * xr + x_scr[rows, im]
            xr, xi = nr, ni
            x_scr[rows, re] = xr
            x_scr[rows, im] = xi
        end_r.append(xr)
        end_i.append(xi)
    er, ei = jnp.concatenate(end_r, axis=1), jnp.concatenate(end_i, axis=1)
    last = SUBLANES * (jn - 1)
    ajr, aji = pw_ref[last:last + 1, 0:n], pw_ref[last:last + 1, n:]
    pr, pi = st_scr[0:1, 0:n], st_scr[0:1, n:]
    cin_r, cin_i = [], []
    for s in range(SUBLANES):
        cin_r.append(pr)
        cin_i.append(pi)
        nr = er[s:s + 1] + ajr * pr - aji * pi
        ni = ei[s:s + 1] + ajr * pi + aji * pr
        pr, pi = nr, ni
    st_scr[0:1, 0:n] = pr
    st_scr[0:1, n:] = pi
    cr_all = jnp.concatenate(cin_r, axis=0)
    ci_all = jnp.concatenate(cin_i, axis=0)
    pack = 2 * SUBLANES
    ys = []
    for q, (re, im) in enumerate(chunks):
        cr = jnp.concatenate([cr_all[:, re]] * 2, axis=0).astype(BF16)
        ci = jnp.concatenate([ci_all[:, re]] * 2, axis=0).astype(BF16)
        xr_parts, xi_parts = [], []
        for m in range(SUBLANES * jn // pack):
            rows = slice(pack * m, pack * (m + 1))
            pjr, pji = pw16_ref[rows, re], pw16_ref[rows, im]
            xr_parts.append(x_scr[rows, re].astype(BF16) + (pjr * cr - pji * ci))
            xi_parts.append(x_scr[rows, im].astype(BF16) + (pjr * ci + pji * cr))
        xcat = jnp.concatenate([jnp.concatenate(xr_parts, axis=0), jnp.concatenate(xi_parts, axis=0)], axis=1)
        ys.append(jnp.dot(xcat, cd_ref[q], preferred_element_type=F32))

    y_perm = jnp.concatenate(ys, axis=1)
    y_hi = y_perm.astype(BF16)
    y_lo = (y_perm - y_hi.astype(F32)).astype(BF16)
    y = (jnp.dot(pmt_ref[...], y_hi, preferred_element_type=F32)
         + jnp.dot(pmt_ref[...], y_lo, preferred_element_type=F32))
    y = _gelu_tanh(y + dsk_ref[...] * u)
    z = jnp.dot(y.astype(BF16), gw_ref[...], preferred_element_type=F32) + gb_ref[...]
    o_ref[tile_rows, :] = (y * _sigmoid(z)).astype(o_ref.dtype)


def _s5_tables(a_re, a_im, log_dt, b_re, b_im, c_re, c_im):
    g, p = a_re.shape
    dt = jnp.exp(log_dt.astype(F32))[:, None]
    ar, ai = a_re.astype(F32), a_im.astype(F32)
    mag = jnp.exp(dt * ar)
    abar_r, abar_i = mag * jnp.cos(dt * ai), mag * jnp.sin(dt * ai)
    den = ar * ar + ai * ai
    fr = ((abar_r - 1.0) * ar + abar_i * ai) / den
    fi = (abar_i * ar - (abar_r - 1.0) * ai) / den
    br, bi = b_re.astype(F32), b_im.astype(F32)
    bbar_r = fr[..., None] * br - fi[..., None] * bi
    bbar_i = fr[..., None] * bi + fi[..., None] * br
    m = br.shape[-1]
    gc = S5_LW // p
    nq = g // gc
    eye = jnp.eye(gc, dtype=F32)
    dense_b = lambda t: jnp.einsum('qgpm,gh->qgmhp', t.reshape(nq, gc, p, m), eye).reshape(nq, gc * m, gc * p)
    bd = jnp.concatenate([dense_b(bbar_r), dense_b(bbar_i)], axis=2)
    dense_c = lambda t: jnp.einsum('qgmp,gh->qgphm', t.reshape(nq, gc, m, p), eye).reshape(nq, gc * p, gc * m)
    cd = jnp.concatenate([dense_c(c_re.astype(F32)), -dense_c(c_im.astype(F32))], axis=1)
    kk = jnp.arange(1, S5_J + 1, dtype=F32)[:, None, None]
    pmag = jnp.exp(kk * (dt * ar)[None])
    pw_r = (pmag * jnp.cos(kk * (dt * ai)[None])).reshape(S5_J, g * p)
    pw_i = (pmag * jnp.sin(kk * (dt * ai)[None])).reshape(S5_J, g * p)
    pw = jnp.repeat(jnp.concatenate([pw_r, pw_i], axis=1), SUBLANES, axis=0)
    return bd.astype(BF16), pw, cd.astype(BF16)


def _s5_layer(u, bsz, seq, a_re, a_im, log_dt, b_re, b_im, c_re, c_im, d_skip, glu_w, glu_b):
    t, width = u.shape
    n = a_re.shape[0] * a_re.shape[1]
    bd, pw, cd = _s5_tables(a_re, a_im, log_dt, b_re, b_im, c_re, c_im)
    tile = SUBLANES * S5_J
    step_rows = S5_TILES_PER_STEP * tile
    steps_per_seq = seq // step_rows
    const = lambda b, i: (0, 0)
    perm = _sublane_major_perm(tile)
    return pl.pallas_call(
        functools.partial(_s5_kernel, n_state=n),
        out_shape=jax.ShapeDtypeStruct((t, width), BF16),
        grid=(bsz, steps_per_seq),
        in_specs=[pl.BlockSpec((step_rows, width), lambda b, i: (b * steps_per_seq + i, 0)),
                  pl.BlockSpec((tile, tile), const),
                  pl.BlockSpec((tile, tile), const),
                  pl.BlockSpec(bd.shape, lambda b, i: (0, 0, 0)),
                  pl.BlockSpec((tile, 2 * n), const),
                  pl.BlockSpec((tile, 2 * n), const),
                  pl.BlockSpec(cd.shape, lambda b, i: (0, 0, 0)),
                  pl.BlockSpec((1, width), const),
                  pl.BlockSpec((width, width), const),
                  pl.BlockSpec((1, width), const)],
        out_specs=pl.BlockSpec((step_rows, width), lambda b, i: (b * steps_per_seq + i, 0)),
        scratch_shapes=[pltpu.VMEM((tile, 2 * n), F32), pltpu.VMEM((SUBLANES, 2 * n), F32)],
        compiler_params=_cparams(("arbitrary", "arbitrary")),
        name="s5_layer",
    )(u, jnp.asarray(perm, BF16), jnp.asarray(perm.T, BF16), bd, pw, pw.astype(BF16), cd,
      d_skip.reshape(1, width).astype(F32), glu_w.astype(BF16),
      glu_b.reshape(1, width).astype(F32))


def _finish(y, x_ref, mod_ref, pg_ref, out_ref, d_model):
    ms = jnp.mean(y * y, axis=-1, keepdims=True)
    yn = y * lax.rsqrt(ms + EPS) * pg_ref[...]
    gate_mod = mod_ref[0, :, 2 * d_model:3 * d_model]
    out_ref[...] = x_ref[...] + gate_mod * yn


def _split2(x):
    hi = x.astype(BF16)
    return hi, (x - hi.astype(F32)).astype(BF16)


def _out_ab_kernel(o0_ref, l0_ref, o1_ref, o2_ref, l1_ref, l2_ref, ob_ref, gate_ref, e_ref, un_ref, w_ref,
                   x_ref, mod_ref, pg_ref, out_ref, *, d_model):
    flat = lambda ref: ref[...].reshape(ROW_TILE, ref.shape[-1])
    aw = o0_ref.shape[-1]

    def unperm(o_ref, l_ref):
        parts = jnp.concatenate([flat(o_ref)] + list(_split2(flat(l_ref))), axis=1)
        moved = jnp.dot(un_ref[...], parts, preferred_element_type=F32)
        return moved[:, 0:aw], moved[:, aw:aw + LANES] + moved[:, aw + LANES:]

    o0, l0 = o0_ref[...].astype(F32), l0_ref[...]
    o1, l1 = unperm(o1_ref, l1_ref)
    o2, l2 = unperm(o2_ref, l2_ref)
    mx = jnp.maximum(jnp.maximum(l0, l1), l2)
    e0, e1, e2 = jnp.exp2(l0 - mx), jnp.exp2(l1 - mx), jnp.exp2(l2 - mx)
    inv_den = 1.0 / (e0 + e1 + e2)
    stacked = jnp.concatenate(list(_split2(e1 * inv_den)) + list(_split2(e2 * inv_den)), axis=0)
    wide = jnp.dot(stacked, e_ref[...], preferred_element_type=F32)
    term = lambda i: wide[i * ROW_TILE:(i + 1) * ROW_TILE]
    w1 = term(0) + term(1)
    w2 = term(2) + term(3)
    o_a = (1.0 - w1 - w2) * o0 + w1 * o1 + w2 * o2
    gate = gate_ref[...].astype(F32)
    sg = gate * _sigmoid(gate)
    aw = o_a.shape[-1]
    y = jnp.dot((o_a * sg[:, :aw]).astype(BF16), w_ref[0:aw, :], preferred_element_type=F32)
    y = y + jnp.dot((ob_ref[...].astype(F32) * sg[:, aw:]).astype(BF16), w_ref[aw:, :],
                    preferred_element_type=F32)
    _finish(y, x_ref, mod_ref, pg_ref, out_ref, d_model)


def _out_c_kernel(o_ref, gate_ref, w_ref, x_ref, mod_ref, pg_ref, out_ref, *, d_model):
    gate = gate_ref[...].astype(F32)
    o = o_ref[...].astype(F32) * (gate * _sigmoid(gate))
    y = jnp.dot(o.astype(BF16), w_ref[...], preferred_element_type=F32)
    _finish(y, x_ref, mod_ref, pg_ref, out_ref, d_model)


def _out_proj(kern, row_inputs, const_inputs, w_bf16, x2, mod3, post_g, seq, name, tile=ROW_TILE):
    t, d = x2.shape
    tiles_per_seq = seq // tile
    tiles_per_span = SPAN // tile

    def row_spec(a):
        if a.ndim == 2:
            return pl.BlockSpec((tile, a.shape[1]), lambda i: (i, 0))
        assert tile == ROW_TILE
        return pl.BlockSpec((None, SPAN_RES, SPAN_RUN, a.shape[-1]),
                            lambda i: (i // tiles_per_span, 0, i % tiles_per_span, 0))

    const_spec = lambda a: pl.BlockSpec(a.shape, lambda i: (0, 0))
    return pl.pallas_call(
        functools.partial(kern, d_model=d),
        out_shape=jax.ShapeDtypeStruct((t, d), F32),
        grid=(t // tile,),
        in_specs=([row_spec(a) for a in row_inputs] + [const_spec(a) for a in const_inputs]
                  + [const_spec(w_bf16), row_spec(x2),
                     pl.BlockSpec((1, 1, 3 * d), lambda i: (i // tiles_per_seq, 0, 0)),
                     pl.BlockSpec((1, d), lambda i: (0, 0))]),
        out_specs=pl.BlockSpec((tile, d), lambda i: (i, 0)),
        compiler_params=_cparams(("arbitrary",)),
        name=name,
    )(*row_inputs, *const_inputs, w_bf16, x2, mod3, post_g.reshape(1, d))


def _gdn_prep_kernel(x_ref, halo_ref, br_ref, cw_ref, alog_ref, dtb_ref,
                     w_ref, u_ref, qg_ref, kdt_ref, aqk_ref, dec_ref, xs_scr):
    n_all = GDN_PREP_BLOCKS * GDN_STEP
    first = pl.program_id(1) == 0
    halo = halo_ref[...]
    xs_scr[0:SUBLANES, :] = jnp.where(first, jnp.zeros_like(halo), halo)
    xs_scr[SUBLANES:SUBLANES + n_all, :] = x_ref[...]
    cw = cw_ref[...]
    conv = xs_scr[SUBLANES:SUBLANES + n_all, :] * cw[C_CONV - 1:C_CONV]
    for j in range(C_CONV - 1):
        off = SUBLANES - (C_CONV - 1) + j
        conv = conv + xs_scr[off:off + n_all, :] * cw[j:j + 1]
    act = conv * _sigmoid(conv)
    for blk in range(GDN_PREP_BLOCKS):
        _gdn_prep_block(blk, act, br_ref, alog_ref, dtb_ref, w_ref, u_ref, qg_ref, kdt_ref, aqk_ref, dec_ref)


def _gdn_prep_block(blk, act, br_ref, alog_ref, dtb_ref, w_ref, u_ref, qg_ref, kdt_ref, aqk_ref, dec_ref):
    n = GDN_STEP
    rows = slice(blk * n, (blk + 1) * n)
    head_cols = lambda base, h: act[rows, base + h * C_DK:base + (h + 1) * C_DK]

    row = lax.broadcasted_iota(jnp.int32, (n, n), 0)
    col = lax.broadcasted_iota(jnp.int32, (n, n), 1)
    same = (row // GDN_CHUNK) == (col // GDN_CHUNK)
    tril = jnp.logical_and(same, row >= col)
    strict = jnp.logical_and(same, row > col)
    eye = (row == col).astype(F32)

    br = br_ref[rows, :]
    beta_all = _sigmoid(br)
    xg = br + dtb_ref[...]
    softplus = jnp.maximum(xg, 0.0) + jnp.log(1.0 + jnp.exp(-jnp.abs(xg)))
    g_all = -jnp.exp(alog_ref[...]) * softplus
    gc_all = jnp.dot(tril.astype(F32), g_all, preferred_element_type=F32, precision=HI)
    gc_t = gc_all.T
    first_chunk = lax.broadcasted_iota(jnp.int32, (n, 1), 0) < GDN_CHUNK
    dec_row = lax.broadcasted_iota(jnp.int32, (SUBLANES, LANES), 0)

    heads = range(C_HEADS)
    lanes = [slice(h * LANES, (h + 1) * LANES) for h in heads]
    gcols = [gc_all[:, C_HEADS + h:C_HEADS + h + 1] for h in heads]
    egs = [jnp.exp(g) for g in gcols]
    kbs, amats, invs = [], [], []
    for h in heads:
        grow = gc_t[C_HEADS + h:C_HEADS + h + 1, :]
        decay = jnp.where(tril, jnp.exp(jnp.where(tril, gcols[h] - grow, 0.0)), 0.0)
        q = head_cols(0, h)
        k = head_cols(C_HEADS * C_DK, h)
        q = q * lax.rsqrt(jnp.sum(q * q, axis=-1, keepdims=True) + EPS) * (C_DK ** -0.5)
        k = k * lax.rsqrt(jnp.sum(k * k, axis=-1, keepdims=True) + EPS)
        kb = k * beta_all[:, h:h + 1]
        k16 = k.astype(BF16)
        a = jnp.where(strict, lax.dot_general(kb.astype(BF16), k16, _NT, preferred_element_type=F32) * decay, 0.0)
        aqk = lax.dot_general(q.astype(BF16), k16, _NT, preferred_element_type=F32) * decay
        aqk_ref[rows, lanes[h]] = aqk.astype(BF16)
        qg_ref[rows, lanes[h]] = (q * egs[h]).astype(BF16)
        g_last = jnp.where(first_chunk, gcols[h][GDN_CHUNK - 1:GDN_CHUNK], gcols[h][n - 1:n])
        kdt_ref[rows, lanes[h]] = (k * jnp.exp(g_last - gcols[h])).T.astype(BF16)
        dec_ref[blk, :, lanes[h]] = jnp.where(dec_row < SUBLANES // 2, egs[h][GDN_CHUNK - 1:GDN_CHUNK],
                                             egs[h][n - 1:n])
        kbs.append(kb)
        amats.append(a)
        invs.append(eye - a)
    pows = []
    for h in heads:
        a16 = amats[h].astype(BF16)
        pows.append(jnp.dot(a16, a16, preferred_element_type=F32))
    for _ in range(4):
        for h in heads:
            p16 = pows[h].astype(BF16)
            both = jnp.dot(jnp.concatenate([invs[h].astype(BF16), p16], axis=0), p16, preferred_element_type=F32)
            invs[h] = invs[h] + both[0:n]
            pows[h] = both[n:]
    for h in heads:
        invs[h] = invs[h] + jnp.dot(invs[h].astype(BF16), pows[h].astype(BF16), preferred_element_type=F32)
    for h in heads:
        v = head_cols(2 * C_HEADS * C_DK, h)
        rhs = jnp.concatenate([v * beta_all[:, h:h + 1], kbs[h] * egs[h]], axis=1)
        uw = jnp.dot(invs[h].astype(BF16), rhs.astype(BF16), preferred_element_type=F32)
        u_ref[rows, lanes[h]] = uw[:, 0:C_DV].astype(u_ref.dtype)
        w_ref[rows, lanes[h]] = uw[:, C_DV:].astype(BF16)


def _gdn_rec_kernel(w_ref, u_ref, qg_ref, kdt_ref, aqk_ref, dec_ref, ng_ref, o_ref, s_scr):
    @pl.when(pl.program_id(0) == 0)
    def _():
        s_scr[...] = jnp.zeros_like(s_scr)

    half = SUBLANES // 2
    zeros = jnp.zeros((GDN_CHUNK, C_DV), BF16)
    lanes = [slice(h * LANES, (h + 1) * LANES) for h in range(C_HEADS)]
    chains = [(b, h) for b in range(w_ref.shape[0]) for h in range(C_HEADS)]
    states = [s_scr[b * C_HEADS + h] for b, h in chains]
    for blk in range(GDN_REC_BLOCKS):
        blk_rows = slice(blk * GDN_STEP, (blk + 1) * GDN_STEP)
        for j in range(GDN_STEP // GDN_CHUNK):
            rows = slice(blk * GDN_STEP + j * GDN_CHUNK, blk * GDN_STEP + (j + 1) * GDN_CHUNK)
            wss = []
            for c, (b, h) in enumerate(chains):
                wq = jnp.concatenate([w_ref[b, rows, lanes[h]], qg_ref[b, rows, lanes[h]]], axis=0)
                wss.append(jnp.dot(wq, states[c].astype(BF16), preferred_element_type=F32))
            for c, (b, h) in enumerate(chains):
                v_new = (u_ref[b, rows, lanes[h]].astype(F32) - wss[c][0:GDN_CHUNK]).astype(BF16)
                v_pad = jnp.concatenate([v_new, zeros] if j == 0 else [zeros, v_new], axis=0)
                both = jnp.dot(jnp.concatenate([aqk_ref[b, rows, lanes[h]], kdt_ref[b, blk_rows, lanes[h]]], axis=0),
                               v_pad, preferred_element_type=F32)
                o = wss[c][GDN_CHUNK:] + both[0:GDN_CHUNK]
                dec = dec_ref[b, blk, j * half:j * half + 1, lanes[h]]
                states[c] = states[c] * dec + both[GDN_CHUNK:]
                ms = jnp.mean(o * o, axis=-1, keepdims=True)
                o_ref[b, rows, lanes[h]] = (o * lax.rsqrt(ms + EPS) * ng_ref[...]).astype(o_ref.dtype)
    for c, (b, h) in enumerate(chains):
        s_scr[b * C_HEADS + h] = states[c]


def _gdn_core(qkv_pre, br, bsz, seq, conv_w, a_log, dt_bias, norm_g):
    t, qkv_w = qkv_pre.shape
    hw = C_HEADS * C_DV
    steps_per_seq = seq // GDN_STEP
    prep_rows = GDN_PREP_BLOCKS * GDN_STEP
    prep_steps = steps_per_seq // GDN_PREP_BLOCKS
    halo_blocks = prep_rows // SUBLANES
    pad_row = lambda vec: jnp.zeros((1, LANES), F32).at[0, C_HEADS:2 * C_HEADS].set(vec.astype(F32))
    tok = lambda b, i: (b * prep_steps + i, 0)
    const = lambda b, i: (0, 0)
    tok_spec = pl.BlockSpec((prep_rows, hw), tok)
    dec_spec = pl.BlockSpec((GDN_PREP_BLOCKS, SUBLANES, hw), lambda b, i: (b * prep_steps + i, 0, 0))
    w, u, qg, kdt, aqk, dec = pl.pallas_call(
        _gdn_prep_kernel,
        out_shape=[jax.ShapeDtypeStruct((t, hw), BF16), jax.ShapeDtypeStruct((t, hw), BF16),
                   jax.ShapeDtypeStruct((t, hw), BF16), jax.ShapeDtypeStruct((t, hw), BF16),
                   jax.ShapeDtypeStruct((t, hw), BF16),
                   jax.ShapeDtypeStruct((t // GDN_STEP, SUBLANES, hw), F32)],
        grid=(bsz, prep_steps),
        in_specs=[pl.BlockSpec((prep_rows, qkv_w), tok),
                  pl.BlockSpec((SUBLANES, qkv_w),
                               lambda b, i: (jnp.maximum((b * prep_steps + i) * halo_blocks - 1, 0), 0)),
                  pl.BlockSpec((prep_rows, LANES), tok),
                  pl.BlockSpec((C_CONV, qkv_w), const),
                  pl.BlockSpec((1, LANES), const),
                  pl.BlockSpec((1, LANES), const)],
        out_specs=[tok_spec, tok_spec, tok_spec, tok_spec, tok_spec, dec_spec],
        scratch_shapes=[pltpu.VMEM((SUBLANES + prep_rows, qkv_w), F32)],
        compiler_params=_cparams(("arbitrary", "arbitrary")),
        name="gdn_prep",
    )(qkv_pre, qkv_pre, br, conv_w.astype(F32), pad_row(a_log), pad_row(dt_bias))
    rec_steps = steps_per_seq // GDN_REC_BLOCKS
    seq_view = lambda a: a.reshape(bsz, seq, hw)
    rec_tok = pl.BlockSpec((bsz, GDN_REC_BLOCKS * GDN_STEP, hw), lambda i: (0, i, 0))
    rec_dec = pl.BlockSpec((bsz, GDN_REC_BLOCKS, SUBLANES, hw), lambda i: (0, i, 0, 0))
    o = pl.pallas_call(
        _gdn_rec_kernel,
        out_shape=jax.ShapeDtypeStruct((bsz, seq, hw), BF16),
        grid=(rec_steps,),
        in_specs=[rec_tok, rec_tok, rec_tok, rec_tok, rec_tok, rec_dec,
                  pl.BlockSpec((1, C_DV), lambda i: (0, 0))],
        out_specs=rec_tok,
        scratch_shapes=[pltpu.VMEM((bsz * C_HEADS, C_DK, C_DV), F32)],
        compiler_params=_cparams(("arbitrary",)),
        name="gdn_recurrence",
    )(seq_view(w), seq_view(u), seq_view(qg), seq_view(kdt), seq_view(aqk),
      dec.reshape(bsz, steps_per_seq, SUBLANES, hw), norm_g.reshape(1, C_DV).astype(F32))
    return o.reshape(t, hw)


def _ab_layer(x2, mod3, bsz, seq, pre_g, post_g, rel_bias, w_in, w_out, s5_params):
    b_width = s5_params[7].shape[-1]
    assert DILATED_CONFIGS[0][1] == 1 and all(SPAN_RES % dl == 0 for _, dl in DILATED_CONFIGS) and seq % SPAN == 0
    head_expand = jnp.asarray(np.arange(LANES)[:, None] == (np.arange(A_WIDTH)[None, :] // A_HEAD_DIM), BF16)
    unperm = jnp.asarray(_span_perm().T, BF16)
    splits = ((0, A_WIDTH, A_HEAD_DIM ** -0.5 * LOG2_E), (A_WIDTH, A_WIDTH, 1.0), (2 * A_WIDTH, A_WIDTH, 1.0),
              (3 * A_WIDTH, b_width, 1.0), (3 * A_WIDTH + b_width, A_WIDTH + b_width, 1.0))
    q, k, v, u, gate, qs, ks, vs = _in_proj(x2, mod3, pre_g, w_in.astype(BF16), splits,
                                            (BF16, BF16, BF16, F32, BF16), seq, n_span=3)
    outs, lses = [], []
    bias_tabs = _attn_bias_tables(rel_bias)
    for cfg, (_, dil) in enumerate(DILATED_CONFIGS):
        qkv = (q, k, v) if dil == 1 else (qs, ks, vs)
        o_c, lse_c = _dilated_attention_one(*qkv, bias_tabs[cfg], bsz, seq, dil)
        outs.append(o_c)
        lses.append(lse_c)
    o_b = _s5_layer(u, bsz, seq, *s5_params)
    row_inputs = [outs[0], lses[0], outs[1], outs[2], lses[1], lses[2], o_b, gate]
    return _out_proj(_out_ab_kernel, row_inputs, [head_expand, unperm], w_out.astype(BF16),
                     x2, mod3, post_g, seq, "out_proj_ab")


def _gdn_layer(x2, mod3, bsz, seq, pre_g, post_g, w_in, conv_w, a_log, dt_bias, norm_g, w_out):
    d = x2.shape[1]
    qkv_w = 2 * C_HEADS * C_DK + C_HEADS * C_DV
    gate_w = C_HEADS * C_DV
    w_pad = jnp.concatenate(
        [w_in, jnp.zeros((d, LANES - (w_in.shape[1] - qkv_w - gate_w)), w_in.dtype)], axis=1).astype(BF16)
    splits = ((0, qkv_w, 1.0), (qkv_w, gate_w, 1.0), (qkv_w + gate_w, LANES, 1.0))
    qkv_pre, gate, br = _in_proj(x2, mod3, pre_g, w_pad, splits, (F32, BF16, F32), seq)
    o = _gdn_core(qkv_pre, br, bsz, seq, conv_w, a_log, dt_bias, norm_g)
    return _out_proj(_out_c_kernel, [o, gate], [], w_out.astype(BF16), x2, mod3, post_g, seq, "out_proj_c",
                     tile=OUT_C_TILE)


def kernel(x, c, ada_w, ada_b, pre_g, post_g, rel_bias, ab_w_in, ab_w_out, s5_a_re, s5_a_im, s5_log_dt, s5_b_re, s5_b_im, s5_c_re, s5_c_im, s5_d, s5_glu_w, s5_glu_b, gdn_w_in, gdn_conv, gdn_a_log, gdn_dt_bias, gdn_norm_g, gdn_w_out):
    bsz, seq, d = x.shape
    depth = ada_w.shape[0]
    assert seq % ROW_TILE == 0 and seq % OUT_C_TILE == 0 and seq % (SUBLANES * S5_J * S5_TILES_PER_STEP) == 0
    assert seq % (GDN_STEP * GDN_PREP_BLOCKS) == 0 and seq % (GDN_STEP * GDN_REC_BLOCKS) == 0
    x2 = x.reshape(bsz * seq, d)
    mod = _adaln_mod(c, ada_w, ada_b)
    for layer in range(depth):
        j = layer // 2
        mod3 = mod[layer].reshape(bsz, 1, 3 * d)
        if layer % 2 == 0:
            s5_params = (s5_a_re[j], s5_a_im[j], s5_log_dt[j], s5_b_re[j], s5_b_im[j], s5_c_re[j], s5_c_im[j],
                         s5_d[j], s5_glu_w[j], s5_glu_b[j])
            x2 = _ab_layer(x2, mod3, bsz, seq, pre_g[layer], post_g[layer], rel_bias, ab_w_in[j], ab_w_out[j],
                           s5_params)
        else:
            x2 = _gdn_layer(x2, mod3, bsz, seq, pre_g[layer], post_g[layer], gdn_w_in[j], gdn_conv[j],
                            gdn_a_log[j], gdn_dt_bias[j], gdn_norm_g[j], gdn_w_out[j])
    return x2.reshape(bsz, seq, d)
```

```python
import functools
import math

import numpy as np
import jax
import jax.numpy as jnp
from jax import lax
from jax.experimental import pallas as pl
from jax.experimental.pallas import tpu as pltpu

F32 = jnp.float32
BF16 = jnp.bfloat16
HI = lax.Precision.HIGHEST

EPS = 1e-6
A_HEADS = 8
A_HEAD_DIM = 64
A_WIDTH = A_HEADS * A_HEAD_DIM
A_BLOCK = 128
DILATED_CONFIGS = ((128, 1), (512, 4), (2048, 16))
B_GROUP = 16
B_STATE = 64
REL_BUCKETS = 32
REL_MAX_DIST = 2048
C_HEADS = 8
C_DK = 128
C_DV = 128
C_CONV = 4
MASK_NEG = -1e30
LOG2_E = math.log2(math.e)

LANES = 128
SUBLANES = 8
VMEM_LIMIT = 48 * 1024 * 1024
ROW_TILE = 512
OUT_C_TILE = 1024
COL_CHUNK = 512
SPAN_RES = max(dl for _, dl in DILATED_CONFIGS)
SPAN = A_BLOCK * SPAN_RES
SPAN_RUN = ROW_TILE // SPAN_RES
ATTN_BLOCKS_PER_STEP = 8
S5_J = 32
S5_LW = 512
S5_TILES_PER_STEP = 4
GDN_STEP = 128
GDN_CHUNK = 64
GDN_PREP_BLOCKS = 4
GDN_REC_BLOCKS = 2

_NT = (((1,), (1,)), ((), ()))


def _cparams(sem):
    return pltpu.CompilerParams(dimension_semantics=sem, vmem_limit_bytes=VMEM_LIMIT)


def _sigmoid(x):
    return 1.0 / (1.0 + jnp.exp(-x))


def _mod_kernel(c_ref, w_ref, b_ref, o_ref):
    c = c_ref[...]
    ca = c * _sigmoid(c)
    o_ref[0] = jnp.dot(ca.astype(BF16), w_ref[0].astype(BF16), preferred_element_type=F32) + b_ref[0]


def _adaln_mod(c, ada_w, ada_b):
    depth, d, d3 = ada_w.shape
    bsz = c.shape[0]
    return pl.pallas_call(
        _mod_kernel,
        out_shape=jax.ShapeDtypeStruct((depth, bsz, d3), F32),
        grid=(depth, d3 // d),
        in_specs=[pl.BlockSpec((bsz, d), lambda l, j: (0, 0)),
                  pl.BlockSpec((1, d, d), lambda l, j: (l, 0, j)),
                  pl.BlockSpec((1, 1, d), lambda l, j: (l, 0, j))],
        out_specs=pl.BlockSpec((1, bsz, d), lambda l, j: (l, 0, j)),
        compiler_params=_cparams(("arbitrary", "arbitrary")),
        name="adaln_mod",
    )(c, ada_w, ada_b.reshape(depth, 1, d3))


def _span_perm():
    rho = np.arange(ROW_TILE)
    nat = SPAN_RES * (rho % SPAN_RUN) + rho // SPAN_RUN
    return (nat[:, None] == np.arange(ROW_TILE)[None, :]).astype(np.float32)


def _modulated_norm(x_ref, mod_ref, g_ref, d_model):
    x = x_ref[...]
    ms = jnp.mean(x * x, axis=-1, keepdims=True)
    y = x * lax.rsqrt(ms + EPS) * g_ref[...]
    shift = mod_ref[0, :, 0:d_model]
    scale = mod_ref[0, :, d_model:2 * d_model]
    return (y * (1.0 + scale) + shift).astype(BF16)


def _sublane_major_perm(tile):
    rho = np.arange(tile)
    src = (rho % SUBLANES) * (tile // SUBLANES) + rho // SUBLANES
    return (src[:, None] == np.arange(tile)[None, :]).astype(np.float32)


def _in_proj_kernel(x_ref, mod_ref, g_ref, w_ref, *rest, splits, d_model, n_span):
    pm_ref = rest[0] if n_span else None
    out_refs = rest[1:] if n_span else rest
    span_refs = out_refs[len(splits):]
    h = _modulated_norm(x_ref, mod_ref, g_ref, d_model)
    for idx, ((c0, width, mult), o_ref) in enumerate(zip(splits, out_refs)):
        for cc in range(0, width, COL_CHUNK):
            cw = min(COL_CHUNK, width - cc)
            acc = jnp.dot(h, w_ref[:, c0 + cc:c0 + cc + cw], preferred_element_type=F32)
            if mult != 1.0:
                acc = acc * mult
            val = acc.astype(o_ref.dtype)
            o_ref[:, cc:cc + cw] = val
            if idx < n_span:
                moved = jnp.dot(pm_ref[...], val, preferred_element_type=F32).astype(BF16)
                span_refs[idx][:, :, cc:cc + cw] = moved.reshape(SPAN_RES, SPAN_RUN, cw)


def _in_proj(x2, mod3, gain, w_bf16, splits, out_dtypes, seq, n_span=0):
    t, d = x2.shape
    tiles_per_seq = seq // ROW_TILE
    tiles_per_span = SPAN // ROW_TILE
    n_w = w_bf16.shape[1]
    row = lambda wd: pl.BlockSpec((ROW_TILE, wd), lambda i: (i, 0))
    span_shape = lambda wd: jax.ShapeDtypeStruct((t // SPAN, SPAN_RES, A_BLOCK, wd), BF16)
    span_spec = lambda wd: pl.BlockSpec((None, SPAN_RES, SPAN_RUN, wd),
                                        lambda i: (i // tiles_per_span, 0, i % tiles_per_span, 0))
    perm_in = [jnp.asarray(_span_perm(), BF16)] if n_span else []
    perm_spec = [pl.BlockSpec((ROW_TILE, ROW_TILE), lambda i: (0, 0))] if n_span else []
    return pl.pallas_call(
        functools.partial(_in_proj_kernel, splits=splits, d_model=d, n_span=n_span),
        out_shape=([jax.ShapeDtypeStruct((t, wd), dt) for (_, wd, _), dt in zip(splits, out_dtypes)]
                   + [span_shape(wd) for (_, wd, _) in splits[:n_span]]),
        grid=(t // ROW_TILE,),
        in_specs=[row(d),
                  pl.BlockSpec((1, 1, 3 * d), lambda i: (i // tiles_per_seq, 0, 0)),
                  pl.BlockSpec((1, d), lambda i: (0, 0)),
                  pl.BlockSpec((d, n_w), lambda i: (0, 0))] + perm_spec,
        out_specs=[row(wd) for (_, wd, _) in splits] + [span_spec(wd) for (_, wd, _) in splits[:n_span]],
        compiler_params=_cparams(("arbitrary",)),
        name="in_proj",
    )(x2, mod3, gain.reshape(1, d), w_bf16, *perm_in)


def _t5_bucket(dist):
    dist = np.maximum(dist, 0)
    max_exact = REL_BUCKETS // 2
    large = max_exact + (np.log(np.maximum(dist, 1) / max_exact)
                         / math.log(REL_MAX_DIST / max_exact) * (REL_BUCKETS - max_exact)).astype(np.int32)
    large = np.minimum(large, REL_BUCKETS - 1)
    return np.where(dist < max_exact, dist, large).astype(np.int32)


def _bias_kernel(rb_ref, bucket_ref, mask_ref, o_ref):
    def body(r, carry):
        rows = pl.ds(pl.multiple_of(r * SUBLANES, SUBLANES), SUBLANES)
        bk = bucket_ref[0, rows, :]
        accs = [jnp.zeros(bk.shape, F32) for _ in range(A_HEADS)]
        for b in range(REL_BUCKETS):
            eq = bk == b
            accs = [jnp.where(eq, rb_ref[b, h], acc) for h, acc in enumerate(accs)]
        for f in range(2):
            keep = mask_ref[0, f, rows, :] != 0
            for h in range(A_HEADS):
                o_ref[0, f, h, rows, :] = jnp.where(keep, accs[h] * LOG2_E, MASK_NEG)
        return carry
    lax.fori_loop(0, A_BLOCK // SUBLANES, body, 0)


def _attn_bias_tables(rel_bias):
    qi = np.arange(A_BLOCK)[:, None]
    kj = np.arange(2 * A_BLOCK)[None, :]
    rel = qi + A_BLOCK - kj
    buckets, masks = [], []
    for window, dil in DILATED_CONFIGS:
        band = (rel >= 0) & (rel <= window // dil)
        bucket = _t5_bucket(rel * dil)
        mask = np.stack([band & (kj >= A_BLOCK), band]).astype(np.int32)
        if dil > 1:
            runs = SPAN_RES // dil
            run = A_BLOCK // runs
            rho = np.arange(A_BLOCK)
            sub = runs * (rho % run) + rho // run
            keys = np.concatenate([sub, A_BLOCK + sub])
            bucket = bucket[sub][:, keys]
            mask = mask[:, sub][:, :, keys]
        buckets.append(bucket)
        masks.append(mask)
    n_cfg = len(DILATED_CONFIGS)
    return pl.pallas_call(
        _bias_kernel,
        out_shape=jax.ShapeDtypeStruct((n_cfg, 2, A_HEADS, A_BLOCK, 2 * A_BLOCK), F32),
        grid=(n_cfg,),
        in_specs=[pl.BlockSpec(memory_space=pltpu.SMEM),
                  pl.BlockSpec((1, A_BLOCK, 2 * A_BLOCK), lambda c: (c, 0, 0)),
                  pl.BlockSpec((1, 2, A_BLOCK, 2 * A_BLOCK), lambda c: (c, 0, 0, 0))],
        out_specs=pl.BlockSpec((1, 2, A_HEADS, A_BLOCK, 2 * A_BLOCK), lambda c: (c, 0, 0, 0, 0)),
        compiler_params=_cparams(("arbitrary",)),
        name="attn_bias",
    )(rel_bias.astype(F32), jnp.asarray(np.stack(buckets)), jnp.asarray(np.stack(masks)))


def _attn_block(q, kcat, vcat, bias_ref):
    lane = lax.broadcasted_iota(jnp.int32, (A_BLOCK, LANES), 1)
    low = lane < A_HEAD_DIM
    lse_tile = jnp.zeros((A_BLOCK, LANES), F32)
    zero = jnp.zeros((A_BLOCK, LANES), BF16)
    heads = range(A_HEADS)
    pair = [slice((h // 2) * LANES, (h // 2 + 1) * LANES) for h in heads]
    scores = []
    for h in heads:
        qm = jnp.where(low if h % 2 == 0 else jnp.logical_not(low), q[:, pair[h]], zero)
        scores.append(lax.dot_general(qm, kcat[:, pair[h]], _NT, preferred_element_type=F32) + bias_ref[0, h])
    probs, inv_l = [], []
    for h in heads:
        m = jnp.max(scores[h], axis=-1, keepdims=True)
        p = jnp.exp2(scores[h] - m)
        l = jnp.sum(p, axis=-1, keepdims=True)
        probs.append(p.astype(BF16))
        inv_l.append(1.0 / l)
        lse_tile = jnp.where(lane == h, m + jnp.log2(l), lse_tile)
    outs = [jnp.dot(probs[h], vcat[:, pair[h]], preferred_element_type=F32) * inv_l[h] for h in heads]
    o = jnp.concatenate([jnp.where(low, outs[2 * hp], outs[2 * hp + 1]) for hp in range(A_HEADS // 2)], axis=1)
    return o, lse_tile


def _attn_kernel(q_ref, kp_ref, kc_ref, vp_ref, vc_ref, bias0_ref, bias_ref, o_ref, lse_ref, *, per_lead):
    def sub(ref, rr, i):
        if ref.ndim == 2:
            return ref[i * A_BLOCK:(i + 1) * A_BLOCK, :]
        run = ref.shape[3] // per_lead
        return ref[i // per_lead, :, rr, (i % per_lead) * run:(i % per_lead + 1) * run, :]

    def put(ref, rr, i, val):
        if ref.ndim == 2:
            ref[i * A_BLOCK:(i + 1) * A_BLOCK, :] = val.astype(ref.dtype)
        else:
            run = ref.shape[3] // per_lead
            ref[i // per_lead, :, rr, (i % per_lead) * run:(i % per_lead + 1) * run, :] = (
                val.astype(ref.dtype).reshape(ref.shape[1], run, ref.shape[4]))

    flat = lambda v: v.reshape(A_BLOCK, v.shape[-1])
    token_order = q_ref.ndim == 2
    n_blocks = q_ref.shape[0] // A_BLOCK if token_order else q_ref.shape[0] * per_lead
    for rr in range(1 if token_order else q_ref.shape[2]):
        k_prev = flat(kp_ref[...] if token_order else kp_ref[0, :, rr])
        v_prev = flat(vp_ref[...] if token_order else vp_ref[0, :, rr])
        for i in range(n_blocks):
            k_cur, v_cur = flat(sub(kc_ref, rr, i)), flat(sub(vc_ref, rr, i))
            o, lse_tile = _attn_block(flat(sub(q_ref, rr, i)), jnp.concatenate([k_prev, k_cur], axis=0),
                                      jnp.concatenate([v_prev, v_cur], axis=0),
                                      bias0_ref if i == 0 else bias_ref)
            put(o_ref, rr, i, o)
            put(lse_ref, rr, i, lse_tile)
            k_prev, v_prev = k_cur, v_cur


def _dilated_attention_one(q, k, v, bias_tab, bsz, seq, dil):
    nb = seq // dil // A_BLOCK
    step = min(ATTN_BLOCKS_PER_STEP, nb)
    w = A_WIDTH
    if dil == 1:
        per_lead = n_res = 1
        view = lambda a: a.reshape(bsz, seq, a.shape[-1])
        block = lambda wd: (None, step * A_BLOCK, wd)
        block_prev = lambda wd: (None, A_BLOCK, wd)
        cur = lambda b, r, m: (b, m, 0)
        prev = lambda b, r, m: (b, jnp.maximum(m * step - 1, 0), 0)
        out_shape = lambda wd, dt: jax.ShapeDtypeStruct((bsz, seq, wd), dt)
    else:
        runs = SPAN_RES // dil
        run = A_BLOCK // runs
        spans_per_seq = seq // SPAN
        per_lead = min(step, runs)
        lead = step // per_lead
        n_res = max(1, ATTN_BLOCKS_PER_STEP // step)
        assert runs % per_lead == 0 and spans_per_seq % lead == 0 and dil % n_res == 0
        view = lambda a: a.reshape(a.shape[0], runs, dil, A_BLOCK, a.shape[-1])
        block = lambda wd: (lead, runs, n_res, per_lead * run, wd)
        block_prev = lambda wd: (1, runs, n_res, run, wd)
        cur = lambda b, r, m: ((b * spans_per_seq + (m * step) // runs) // lead, 0, r, ((m * step) % runs) // per_lead, 0)

        def prev(b, r, m):
            n = jnp.maximum(m * step - 1, 0)
            return (b * spans_per_seq + n // runs, 0, r, n % runs, 0)

        out_shape = lambda wd, dt: jax.ShapeDtypeStruct((bsz * spans_per_seq, runs, dil, A_BLOCK, wd), dt)
    blk = pl.BlockSpec(block(w), cur)
    blk_prev = pl.BlockSpec(block_prev(w), prev)
    bias_block = (1, A_HEADS, A_BLOCK, 2 * A_BLOCK)
    o, lse = pl.pallas_call(
        functools.partial(_attn_kernel, per_lead=per_lead),
        out_shape=[out_shape(w, BF16), out_shape(LANES, F32)],
        grid=(bsz, dil // n_res, nb // step),
        in_specs=[blk, blk_prev, blk, blk_prev, blk,
                  pl.BlockSpec(bias_block, lambda b, r, m: (jnp.minimum(m, 1), 0, 0, 0)),
                  pl.BlockSpec(bias_block, lambda b, r, m: (1, 0, 0, 0))],
        out_specs=[pl.BlockSpec(block(w), cur), pl.BlockSpec(block(LANES), cur)],
        compiler_params=_cparams(("arbitrary", "arbitrary", "arbitrary")),
        name=f"dilated_attn_d{dil}",
    )(view(q), view(k), view(k), view(v), view(v), bias_tab, bias_tab)
    if dil == 1:
        return o.reshape(bsz * seq, w), lse.reshape(bsz * seq, LANES)
    return o.reshape(o.shape[0], SPAN_RES, A_BLOCK, w), lse.reshape(o.shape[0], SPAN_RES, A_BLOCK, LANES)


def _gelu_tanh(x):
    c = math.sqrt(2.0 / math.pi)
    return x * (0.5 * (1.0 + jnp.tanh(c * (x + 0.044715 * (x * x * x)))))


def _s5_kernel(u_ref, pm_ref, pmt_ref, bd_ref, pw_ref, pw16_ref, cd_ref, dsk_ref, gw_ref, gb_ref, o_ref,
               x_scr, st_scr, *, n_state):
    @pl.when(pl.program_id(1) == 0)
    def _():
        st_scr[...] = jnp.zeros_like(st_scr)

    for tile_idx in range(S5_TILES_PER_STEP):
        _s5_tile(tile_idx, u_ref, pm_ref, pmt_ref, bd_ref, pw_ref, pw16_ref, cd_ref, dsk_ref, gw_ref, gb_ref,
                 o_ref, x_scr, st_scr, n_state)


def _s5_tile(tile_idx, u_ref, pm_ref, pmt_ref, bd_ref, pw_ref, pw16_ref, cd_ref, dsk_ref, gw_ref, gb_ref,
             o_ref, x_scr, st_scr, n_state):
    n = n_state
    jn = S5_J
    tile_rows = slice(tile_idx * SUBLANES * jn, (tile_idx + 1) * SUBLANES * jn)

    u = u_ref[tile_rows, :]
    u_perm = jnp.dot(pm_ref[...], u.astype(BF16), preferred_element_type=F32).astype(BF16)
    cw = u.shape[1] * S5_LW // n
    chunks = [(slice(c0, c0 + S5_LW), slice(n + c0, n + c0 + S5_LW)) for c0 in range(0, n, S5_LW)]

    def project_in(q):
        bu = jnp.dot(u_perm[:, q * cw:(q + 1) * cw], bd_ref[q], preferred_element_type=F32)
        x_scr[:, chunks[q][0]] = bu[:, 0:S5_LW]
        x_scr[:, chunks[q][1]] = bu[:, S5_LW:]

    project_in(0)
    end_r, end_i = [], []
    for q, (re, im) in enumerate(chunks):
        if q + 1 < len(chunks):
            project_in(q + 1)
        a1r, a1i = pw_ref[0:SUBLANES, re], pw_ref[0:SUBLANES, im]
        xr = jnp.zeros((SUBLANES, S5_LW), F32)
        xi = jnp.zeros((SUBLANES, S5_LW), F32)
        for j in range(jn):
            rows = slice(SUBLANES * j, SUBLANES * (j + 1))
            nr = a1r * xr - a1i * xi + x_scr[rows, re]
            ni = a1r * xi + a1i * xr + x_scr[rows, im]
            xr, xi = nr, ni
            x_scr[rows, re] = xr
            x_scr[rows, im] = xi
        end_r.append(xr)
        end_i.append(xi)
    er, ei = jnp.concatenate(end_r, axis=1), jnp.concatenate(end_i, axis=1)
    last = SUBLANES * (jn - 1)
    ajr, aji = pw_ref[last:last + 1, 0:n], pw_ref[last:last + 1, n:]
    pr, pi = st_scr[0:1, 0:n], st_scr[0:1, n:]
    cin_r, cin_i = [], []
    for s in range(SUBLANES):
        cin_r.append(pr)
        cin_i.append(pi)
        nr = er[s:s + 1] + ajr * pr - aji * pi
        ni = ei[s:s + 1] + ajr * pi + aji * pr
        pr, pi = nr, ni
    st_scr[0:1, 0:n] = pr
    st_scr[0:1, n:] = pi
    cr_all = jnp.concatenate(cin_r, axis=0)
    ci_all = jnp.concatenate(cin_i, axis=0)
    pack = 2 * SUBLANES
    ys = []
    for q, (re, im) in enumerate(chunks):
        cr = jnp.concatenate([cr_all[:, re]] * 2, axis=0).astype(BF16)
        ci = jnp.concatenate([ci_all[:, re]] * 2, axis=0).astype(BF16)
        xr_parts, xi_parts = [], []
        for m in range(SUBLANES * jn // pack):
            rows = slice(pack * m, pack * (m + 1))
            pjr, pji = pw16_ref[rows, re], pw16_ref[rows, im]
            xr_parts.append(x_scr[rows, re].astype(BF16) + (pjr * cr - pji * ci))
            xi_parts.append(x_scr[rows, im].astype(BF16) + (pjr * ci + pji * cr))
        xcat = jnp.concatenate([jnp.concatenate(xr_parts, axis=0), jnp.concatenate(xi_parts, axis=0)], axis=1)
        ys.append(jnp.dot(xcat, cd_ref[q], preferred_element_type=F32))

    y_perm = jnp.concatenate(ys, axis=1)
    y_hi = y_perm.astype(BF16)
    y_lo = (y_perm - y_hi.astype(F32)).astype(BF16)
    y = (jnp.dot(pmt_ref[...], y_hi, preferred_element_type=F32)
         + jnp.dot(pmt_ref[...], y_lo, preferred_element_type=F32))
    y = _gelu_tanh(y + dsk_ref[...] * u)
    z = jnp.dot(y.astype(BF16), gw_ref[...], preferred_element_type=F32) + gb_ref[...]
    o_ref[tile_rows, :] = (y * _sigmoid(z)).astype(o_ref.dtype)


def _s5_tables(a_re, a_im, log_dt, b_re, b_im, c_re, c_im):
    g, p = a_re.shape
    dt = jnp.exp(log_dt.astype(F32))[:, None]
    ar, ai = a_re.astype(F32), a_im.astype(F32)
    mag = jnp.exp(dt * ar)
    abar_r, abar_i = mag * jnp.cos(dt * ai), mag * jnp.sin(dt * ai)
    den = ar * ar + ai * ai
    fr = ((abar_r - 1.0) * ar + abar_i * ai) / den
    fi = (abar_i * ar - (abar_r - 1.0) * ai) / den
    br, bi = b_re.astype(F32), b_im.astype(F32)
    bbar_r = fr[..., None] * br - fi[..., None] * bi
    bbar_i = fr[..., None] * bi + fi[..., None] * br
    m = br.shape[-1]
    gc = S5_LW // p
    nq = g // gc
    eye = jnp.eye(gc, dtype=F32)
    dense_b = lambda t: jnp.einsum('qgpm,gh->qgmhp', t.reshape(nq, gc, p, m), eye).reshape(nq, gc * m, gc * p)
    bd = jnp.concatenate([dense_b(bbar_r), dense_b(bbar_i)], axis=2)
    dense_c = lambda t: jnp.einsum('qgmp,gh->qgphm', t.reshape(nq, gc, m, p), eye).reshape(nq, gc * p, gc * m)
    cd = jnp.concatenate([dense_c(c_re.astype(F32)), -dense_c(c_im.astype(F32))], axis=1)
    kk = jnp.arange(1, S5_J + 1, dtype=F32)[:, None, None]
    pmag = jnp.exp(kk * (dt * ar)[None])
    pw_r = (pmag * jnp.cos(kk * (dt * ai)[None])).reshape(S5_J, g * p)
    pw_i = (pmag * jnp.sin(kk * (dt * ai)[None])).reshape(S5_J, g * p)
    pw = jnp.repeat(jnp.concatenate([pw_r, pw_i], axis=1), SUBLANES, axis=0)
    return bd.astype(BF16), pw, cd.astype(BF16)


def _s5_layer(u, bsz, seq, a_re, a_im, log_dt, b_re, b_im, c_re, c_im, d_skip, glu_w, glu_b):
    t, width = u.shape
    n = a_re.shape[0] * a_re.shape[1]
    bd, pw, cd = _s5_tables(a_re, a_im, log_dt, b_re, b_im, c_re, c_im)
    tile = SUBLANES * S5_J
    step_rows = S5_TILES_PER_STEP * tile
    steps_per_seq = seq // step_rows
    const = lambda b, i: (0, 0)
    perm = _sublane_major_perm(tile)
    return pl.pallas_call(
        functools.partial(_s5_kernel, n_state=n),
        out_shape=jax.ShapeDtypeStruct((t, width), BF16),
        grid=(bsz, steps_per_seq),
        in_specs=[pl.BlockSpec((step_rows, width), lambda b, i: (b * steps_per_seq + i, 0)),
                  pl.BlockSpec((tile, tile), const),
                  pl.BlockSpec((tile, tile), const),
                  pl.BlockSpec(bd.shape, lambda b, i: (0, 0, 0)),
                  pl.BlockSpec((tile, 2 * n), const),
                  pl.BlockSpec((tile, 2 * n), const),
                  pl.BlockSpec(cd.shape, lambda b, i: (0, 0, 0)),
                  pl.BlockSpec((1, width), const),
                  pl.BlockSpec((width, width), const),
                  pl.BlockSpec((1, width), const)],
        out_specs=pl.BlockSpec((step_rows, width), lambda b, i: (b * steps_per_seq + i, 0)),
        scratch_shapes=[pltpu.VMEM((tile, 2 * n), F32), pltpu.VMEM((SUBLANES, 2 * n), F32)],
        compiler_params=_cparams(("arbitrary", "arbitrary")),
        name="s5_layer",
    )(u, jnp.asarray(perm, BF16), jnp.asarray(perm.T, BF16), bd, pw, pw.astype(BF16), cd,
      d_skip.reshape(1, width).astype(F32), glu_w.astype(BF16),
      glu_b.reshape(1, width).astype(F32))


def _finish(y, x_ref, mod_ref, pg_ref, out_ref, d_model):
    ms = jnp.mean(y * y, axis=-1, keepdims=True)
    yn = y * lax.rsqrt(ms + EPS) * pg_ref[...]
    gate_mod = mod_ref[0, :, 2 * d_model:3 * d_model]
    out_ref[...] = x_ref[...] + gate_mod * yn


def _split2(x):
    hi = x.astype(BF16)
    return hi, (x - hi.astype(F32)).astype(BF16)


def _out_ab_kernel(o0_ref, l0_ref, o1_ref, o2_ref, l1_ref, l2_ref, ob_ref, gate_ref, e_ref, un_ref, w_ref,
                   x_ref, mod_ref, pg_ref, out_ref, *, d_model):
    flat = lambda ref: ref[...].reshape(ROW_TILE, ref.shape[-1])
    aw = o0_ref.shape[-1]

    def unperm(o_ref, l_ref):
        parts = jnp.concatenate([flat(o_ref)] + list(_split2(flat(l_ref))), axis=1)
        moved = jnp.dot(un_ref[...], parts, preferred_element_type=F32)
        return moved[:, 0:aw], moved[:, aw:aw + LANES] + moved[:, aw + LANES:]

    o0, l0 = o0_ref[...].astype(F32), l0_ref[...]
    o1, l1 = unperm(o1_ref, l1_ref)
    o2, l2 = unperm(o2_ref, l2_ref)
    mx = jnp.maximum(jnp.maximum(l0, l1), l2)
    e0, e1, e2 = jnp.exp2(l0 - mx), jnp.exp2(l1 - mx), jnp.exp2(l2 - mx)
    inv_den = 1.0 / (e0 + e1 + e2)
    stacked = jnp.concatenate(list(_split2(e1 * inv_den)) + list(_split2(e2 * inv_den)), axis=0)
    wide = jnp.dot(stacked, e_ref[...], preferred_element_type=F32)
    term = lambda i: wide[i * ROW_TILE:(i + 1) * ROW_TILE]
    w1 = term(0) + term(1)
    w2 = term(2) + term(3)
    o_a = (1.0 - w1 - w2) * o0 + w1 * o1 + w2 * o2
    gate = gate_ref[...].astype(F32)
    sg = gate * _sigmoid(gate)
    aw = o_a.shape[-1]
    y = jnp.dot((o_a * sg[:, :aw]).astype(BF16), w_ref[0:aw, :], preferred_element_type=F32)
    y = y + jnp.dot((ob_ref[...].astype(F32) * sg[:, aw:]).astype(BF16), w_ref[aw:, :],
                    preferred_element_type=F32)
    _finish(y, x_ref, mod_ref, pg_ref, out_ref, d_model)


def _out_c_kernel(o_ref, gate_ref, w_ref, x_ref, mod_ref, pg_ref, out_ref, *, d_model):
    gate = gate_ref[...].astype(F32)
    o = o_ref[...].astype(F32) * (gate * _sigmoid(gate))
    y = jnp.dot(o.astype(BF16), w_ref[...], preferred_element_type=F32)
    _finish(y, x_ref, mod_ref, pg_ref, out_ref, d_model)


def _out_proj(kern, row_inputs, const_inputs, w_bf16, x2, mod3, post_g, seq, name, tile=ROW_TILE):
    t, d = x2.shape
    tiles_per_seq = seq // tile
    tiles_per_span = SPAN // tile

    def row_spec(a):
        if a.ndim == 2:
            return pl.BlockSpec((tile, a.shape[1]), lambda i: (i, 0))
        assert tile == ROW_TILE
        return pl.BlockSpec((None, SPAN_RES, SPAN_RUN, a.shape[-1]),
                            lambda i: (i // tiles_per_span, 0, i % tiles_per_span, 0))

    const_spec = lambda a: pl.BlockSpec(a.shape, lambda i: (0, 0))
    return pl.pallas_call(
        functools.partial(kern, d_model=d),
        out_shape=jax.ShapeDtypeStruct((t, d), F32),
        grid=(t // tile,),
        in_specs=([row_spec(a) for a in row_inputs] + [const_spec(a) for a in const_inputs]
                  + [const_spec(w_bf16), row_spec(x2),
                     pl.BlockSpec((1, 1, 3 * d), lambda i: (i // tiles_per_seq, 0, 0)),
                     pl.BlockSpec((1, d), lambda i: (0, 0))]),
        out_specs=pl.BlockSpec((tile, d), lambda i: (i, 0)),
        compiler_params=_cparams(("arbitrary",)),
        name=name,
    )(*row_inputs, *const_inputs, w_bf16, x2, mod3, post_g.reshape(1, d))


def _gdn_prep_kernel(x_ref, halo_ref, br_ref, cw_ref, alog_ref, dtb_ref,
                     w_ref, u_ref, qg_ref, kdt_ref, aqk_ref, dec_ref, xs_scr):
    n_all = GDN_PREP_BLOCKS * GDN_STEP
    first = pl.program_id(1) == 0
    halo = halo_ref[...]
    xs_scr[0:SUBLANES, :] = jnp.where(first, jnp.zeros_like(halo), halo)
    xs_scr[SUBLANES:SUBLANES + n_all, :] = x_ref[...]
    cw = cw_ref[...]
    conv = xs_scr[SUBLANES:SUBLANES + n_all, :] * cw[C_CONV - 1:C_CONV]
    for j in range(C_CONV - 1):
        off = SUBLANES - (C_CONV - 1) + j
        conv = conv + xs_scr[off:off + n_all, :] * cw[j:j + 1]
    act = conv * _sigmoid(conv)
    _gdn_prep_blocks(act, br_ref, alog_ref, dtb_ref, w_ref, u_ref, qg_ref, kdt_ref, aqk_ref, dec_ref)


def _gdn_prep_blocks(act, br_ref, alog_ref, dtb_ref, w_ref, u_ref, qg_ref, kdt_ref, aqk_ref, dec_ref):
    n = GDN_STEP
    row = lax.broadcasted_iota(jnp.int32, (n, n), 0)
    col = lax.broadcasted_iota(jnp.int32, (n, n), 1)
    same = (row // GDN_CHUNK) == (col // GDN_CHUNK)
    tril = jnp.logical_and(same, row >= col)
    strict = jnp.logical_and(same, row > col)
    eye = (row == col).astype(F32)
    first_chunk = lax.broadcasted_iota(jnp.int32, (n, 1), 0) < GDN_CHUNK
    dec_row = lax.broadcasted_iota(jnp.int32, (SUBLANES, LANES), 0)
    heads = range(C_HEADS)
    lanes = [slice(h * LANES, (h + 1) * LANES) for h in heads]

    def gates(blk):
        rows = slice(blk * n, (blk + 1) * n)
        br = br_ref[rows, :]
        xg = br + dtb_ref[...]
        softplus = jnp.maximum(xg, 0.0) + jnp.log(1.0 + jnp.exp(-jnp.abs(xg)))
        g_all = -jnp.exp(alog_ref[...]) * softplus
        gc_all = jnp.dot(tril.astype(F32), g_all, preferred_element_type=F32, precision=HI)
        return dict(rows=rows, beta=_sigmoid(br), gc=gc_all, gc_t=gc_all.T)

    def first_stage(ctx, blk, h):
        rows = ctx["rows"]
        gcol = ctx["gc"][:, C_HEADS + h:C_HEADS + h + 1]
        grow = ctx["gc_t"][C_HEADS + h:C_HEADS + h + 1, :]
        eg = jnp.exp(gcol)
        decay = jnp.where(tril, jnp.exp(jnp.where(tril, gcol - grow, 0.0)), 0.0)
        q = act[rows, h * C_DK:(h + 1) * C_DK]
        k = act[rows, (C_HEADS + h) * C_DK:(C_HEADS + h + 1) * C_DK]
        v = act[rows, 2 * C_HEADS * C_DK + h * C_DV:2 * C_HEADS * C_DK + (h + 1) * C_DV]
        q = q * lax.rsqrt(jnp.sum(q * q, axis=-1, keepdims=True) + EPS) * (C_DK ** -0.5)
        k = k * lax.rsqrt(jnp.sum(k * k, axis=-1, keepdims=True) + EPS)
        beta = ctx["beta"][:, h:h + 1]
        kb = k * beta
        k16 = k.astype(BF16)
        a = jnp.where(strict, lax.dot_general(kb.astype(BF16), k16, _NT, preferred_element_type=F32) * decay, 0.0)
        aqk = lax.dot_general(q.astype(BF16), k16, _NT, preferred_element_type=F32) * decay
        aqk_ref[rows, lanes[h]] = aqk.astype(BF16)
        qg_ref[rows, lanes[h]] = (q * eg).astype(BF16)
        g_last = jnp.where(first_chunk, gcol[GDN_CHUNK - 1:GDN_CHUNK], gcol[n - 1:n])
        kdt_ref[rows, lanes[h]] = (k * jnp.exp(g_last - gcol)).T.astype(BF16)
        dec_ref[blk, :, lanes[h]] = jnp.where(dec_row < SUBLANES // 2, eg[GDN_CHUNK - 1:GDN_CHUNK], eg[n - 1:n])
        rhs = jnp.concatenate([v * beta, kb * eg], axis=1).astype(BF16)
        return dict(a=a, inv=eye - a, rhs=rhs)

    n_blocks = act.shape[0] // n
    state = [first_stage(gates(0), 0, h) for h in heads]
    for blk in range(n_blocks):
        nxt_ctx = gates(blk + 1) if blk + 1 < n_blocks else None
        nxt = []

        def side_work(hs, blk=blk, nxt=nxt, nxt_ctx=nxt_ctx):
            if nxt_ctx is not None:
                nxt.extend(first_stage(nxt_ctx, blk + 1, h) for h in hs)

        invs = [st["inv"] for st in state]
        pows = []
        for st in state:
            a16 = st["a"].astype(BF16)
            pows.append(jnp.dot(a16, a16, preferred_element_type=F32))
        side_work(heads[0:2])
        for level in range(4):
            for h in heads:
                p16 = pows[h].astype(BF16)
                both = jnp.dot(jnp.concatenate([invs[h].astype(BF16), p16], axis=0), p16,
                               preferred_element_type=F32)
                invs[h] = invs[h] + both[0:n]
                pows[h] = both[n:]
            side_work(heads[2 + level:3 + level])
        for h in heads:
            invs[h] = invs[h] + jnp.dot(invs[h].astype(BF16), pows[h].astype(BF16), preferred_element_type=F32)
        side_work(heads[6:8])
        rows = slice(blk * n, (blk + 1) * n)
        for h in heads:
            uw = jnp.dot(invs[h].astype(BF16), state[h]["rhs"], preferred_element_type=F32)
            u_ref[rows, lanes[h]] = uw[:, 0:C_DV].astype(u_ref.dtype)
            w_ref[rows, lanes[h]] = uw[:, C_DV:].astype(BF16)
        state = nxt


def _gdn_rec_kernel(w_ref, u_ref, qg_ref, kdt_ref, aqk_ref, dec_ref, ng_ref, o_ref, s_scr):
    @pl.when(pl.program_id(0) == 0)
    def _():
        s_scr[...] = jnp.zeros_like(s_scr)

    half = SUBLANES // 2
    zeros = jnp.zeros((GDN_CHUNK, C_DV), BF16)
    lanes = [slice(h * LANES, (h + 1) * LANES) for h in range(C_HEADS)]
    chains = [(b, h) for b in range(w_ref.shape[0]) for h in range(C_HEADS)]
    states = [s_scr[b * C_HEADS + h] for b, h in chains]
    for blk in range(GDN_REC_BLOCKS):
        blk_rows = slice(blk * GDN_STEP, (blk + 1) * GDN_STEP)
        for j in range(GDN_STEP // GDN_CHUNK):
            rows = slice(blk * GDN_STEP + j * GDN_CHUNK, blk * GDN_STEP + (j + 1) * GDN_CHUNK)
            wss = []
            for c, (b, h) in enumerate(chains):
                wq = jnp.concatenate([w_ref[b, rows, lanes[h]], qg_ref[b, rows, lanes[h]]], axis=0)
                wss.append(jnp.dot(wq, states[c].astype(BF16), preferred_element_type=F32))
            for c, (b, h) in enumerate(chains):
                v_new = (u_ref[b, rows, lanes[h]].astype(F32) - wss[c][0:GDN_CHUNK]).astype(BF16)
                v_pad = jnp.concatenate([v_new, zeros] if j == 0 else [zeros, v_new], axis=0)
                both = jnp.dot(jnp.concatenate([aqk_ref[b, rows, lanes[h]], kdt_ref[b, blk_rows, lanes[h]]], axis=0),
                               v_pad, preferred_element_type=F32)
                o = wss[c][GDN_CHUNK:] + both[0:GDN_CHUNK]
                dec = dec_ref[b, blk, j * half:j * half + 1, lanes[h]]
                states[c] = states[c] * dec + both[GDN_CHUNK:]
                ms = jnp.mean(o * o, axis=-1, keepdims=True)
                o_ref[b, rows, lanes[h]] = (o * lax.rsqrt(ms + EPS) * ng_ref[...]).astype(o_ref.dtype)
    for c, (b, h) in enumerate(chains):
        s_scr[b * C_HEADS + h] = states[c]


def _gdn_core(qkv_pre, br, bsz, seq, conv_w, a_log, dt_bias, norm_g):
    t, qkv_w = qkv_pre.shape
    hw = C_HEADS * C_DV
    steps_per_seq = seq // GDN_STEP
    prep_rows = GDN_PREP_BLOCKS * GDN_STEP
    prep_steps = steps_per_seq // GDN_PREP_BLOCKS
    halo_blocks = prep_rows // SUBLANES
    pad_row = lambda vec: jnp.zeros((1, LANES), F32).at[0, C_HEADS:2 * C_HEADS].set(vec.astype(F32))
    tok = lambda b, i: (b * prep_steps + i, 0)
    const = lambda b, i: (0, 0)
    tok_spec = pl.BlockSpec((prep_rows, hw), tok)
    dec_spec = pl.BlockSpec((GDN_PREP_BLOCKS, SUBLANES, hw), lambda b, i: (b * prep_steps + i, 0, 0))
    w, u, qg, kdt, aqk, dec = pl.pallas_call(
        _gdn_prep_kernel,
        out_shape=[jax.ShapeDtypeStruct((t, hw), BF16), jax.ShapeDtypeStruct((t, hw), BF16),
                   jax.ShapeDtypeStruct((t, hw), BF16), jax.ShapeDtypeStruct((t, hw), BF16),
                   jax.ShapeDtypeStruct((t, hw), BF16),
                   jax.ShapeDtypeStruct((t // GDN_STEP, SUBLANES, hw), F32)],
        grid=(bsz, prep_steps),
        in_specs=[pl.BlockSpec((prep_rows, qkv_w), tok),
                  pl.BlockSpec((SUBLANES, qkv_w),
                               lambda b, i: (jnp.maximum((b * prep_steps + i) * halo_blocks - 1, 0), 0)),
                  pl.BlockSpec((prep_rows, LANES), tok),
                  pl.BlockSpec((C_CONV, qkv_w), const),
                  pl.BlockSpec((1, LANES), const),
                  pl.BlockSpec((1, LANES), const)],
        out_specs=[tok_spec, tok_spec, tok_spec, tok_spec, tok_spec, dec_spec],
        scratch_shapes=[pltpu.VMEM((SUBLANES + prep_rows, qkv_w), F32)],
        compiler_params=_cparams(("arbitrary", "arbitrary")),
        name="gdn_prep",
    )(qkv_pre, qkv_pre, br, conv_w.astype(F32), pad_row(a_log), pad_row(dt_bias))
    rec_steps = steps_per_seq // GDN_REC_BLOCKS
    seq_view = lambda a: a.reshape(bsz, seq, hw)
    rec_tok = pl.BlockSpec((bsz, GDN_REC_BLOCKS * GDN_STEP, hw), lambda i: (0, i, 0))
    rec_dec = pl.BlockSpec((bsz, GDN_REC_BLOCKS, SUBLANES, hw), lambda i: (0, i, 0, 0))
    o = pl.pallas_call(
        _gdn_rec_kernel,
        out_shape=jax.ShapeDtypeStruct((bsz, seq, hw), BF16),
        grid=(rec_steps,),
        in_specs=[rec_tok, rec_tok, rec_tok, rec_tok, rec_tok, rec_dec,
                  pl.BlockSpec((1, C_DV), lambda i: (0, 0))],
        out_specs=rec_tok,
        scratch_shapes=[pltpu.VMEM((bsz * C_HEADS, C_DK, C_DV), F32)],
        compiler_params=_cparams(("arbitrary",)),
        name="gdn_recurrence",
    )(seq_view(w), seq_view(u), seq_view(qg), seq_view(kdt), seq_view(aqk),
      dec.reshape(bsz, steps_per_seq, SUBLANES, hw), norm_g.reshape(1, C_DV).astype(F32))
    return o.reshape(t, hw)


def _ab_layer(x2, mod3, bsz, seq, pre_g, post_g, rel_bias, w_in, w_out, s5_params):
    b_width = s5_params[7].shape[-1]
    assert DILATED_CONFIGS[0][1] == 1 and all(SPAN_RES % dl == 0 for _, dl in DILATED_CONFIGS) and seq % SPAN == 0
    head_expand = jnp.asarray(np.arange(LANES)[:, None] == (np.arange(A_WIDTH)[None, :] // A_HEAD_DIM), BF16)
    unperm = jnp.asarray(_span_perm().T, BF16)
    splits = ((0, A_WIDTH, A_HEAD_DIM ** -0.5 * LOG2_E), (A_WIDTH, A_WIDTH, 1.0), (2 * A_WIDTH, A_WIDTH, 1.0),
              (3 * A_WIDTH, b_width, 1.0), (3 * A_WIDTH + b_width, A_WIDTH + b_width, 1.0))
    q, k, v, u, gate, qs, ks, vs = _in_proj(x2, mod3, pre_g, w_in.astype(BF16), splits,
                                            (BF16, BF16, BF16, F32, BF16), seq, n_span=3)
    outs, lses = [], []
    bias_tabs = _attn_bias_tables(rel_bias)
    for cfg, (_, dil) in enumerate(DILATED_CONFIGS):
        qkv = (q, k, v) if dil == 1 else (qs, ks, vs)
        o_c, lse_c = _dilated_attention_one(*qkv, bias_tabs[cfg], bsz, seq, dil)
        outs.append(o_c)
        lses.append(lse_c)
    o_b = _s5_layer(u, bsz, seq, *s5_params)
    row_inputs = [outs[0], lses[0], outs[1], outs[2], lses[1], lses[2], o_b, gate]
    return _out_proj(_out_ab_kernel, row_inputs, [head_expand, unperm], w_out.astype(BF16),
                     x2, mod3, post_g, seq, "out_proj_ab")


def _gdn_layer(x2, mod3, bsz, seq, pre_g, post_g, w_in, conv_w, a_log, dt_bias, norm_g, w_out):
    d = x2.shape[1]
    qkv_w = 2 * C_HEADS * C_DK + C_HEADS * C_DV
    gate_w = C_HEADS * C_DV
    w_pad = jnp.concatenate(
        [w_in, jnp.zeros((d, LANES - (w_in.shape[1] - qkv_w - gate_w)), w_in.dtype)], axis=1).astype(BF16)
    splits = ((0, qkv_w, 1.0), (qkv_w, gate_w, 1.0), (qkv_w + gate_w, LANES, 1.0))
    qkv_pre, gate, br = _in_proj(x2, mod3, pre_g, w_pad, splits, (F32, BF16, F32), seq)
    o = _gdn_core(qkv_pre, br, bsz, seq, conv_w, a_log, dt_bias, norm_g)
    return _out_proj(_out_c_kernel, [o, gate], [], w_out.astype(BF16), x2, mod3, post_g, seq, "out_proj_c",
                     tile=OUT_C_TILE)


def kernel(x, c, ada_w, ada_b, pre_g, post_g, rel_bias, ab_w_in, ab_w_out, s5_a_re, s5_a_im, s5_log_dt, s5_b_re, s5_b_im, s5_c_re, s5_c_im, s5_d, s5_glu_w, s5_glu_b, gdn_w_in, gdn_conv, gdn_a_log, gdn_dt_bias, gdn_norm_g, gdn_w_out):
    bsz, seq, d = x.shape
    depth = ada_w.shape[0]
    assert seq % ROW_TILE == 0 and seq % OUT_C_TILE == 0 and seq % (SUBLANES * S5_J * S5_TILES_PER_STEP) == 0
    assert seq % (GDN_STEP * GDN_PREP_BLOCKS) == 0 and seq % (GDN_STEP * GDN_REC_BLOCKS) == 0
    x2 = x.reshape(bsz * seq, d)
    mod = _adaln_mod(c, ada_w, ada_b)
    for layer in range(depth):
        j = layer // 2
        mod3 = mod[layer].reshape(bsz, 1, 3 * d)
        if layer % 2 == 0:
            s5_params = (s5_a_re[j], s5_a_im[j], s5_log_dt[j], s5_b_re[j], s5_b_im[j], s5_c_re[j], s5_c_im[j],
                         s5_d[j], s5_glu_w[j], s5_glu_b[j])
            x2 = _ab_layer(x2, mod3, bsz, seq, pre_g[layer], post_g[layer], rel_bias, ab_w_in[j], ab_w_out[j],
                           s5_params)
        else:
            x2 = _gdn_layer(x2, mod3, bsz, seq, pre_g[layer], post_g[layer], gdn_w_in[j], gdn_conv[j],
                            gdn_a_log[j], gdn_dt_bias[j], gdn_norm_g[j], gdn_w_out[j])
    return x2.reshape(bsz, seq, d)
```

```python
import functools
import math

import numpy as np
import jax
import jax.numpy as jnp
from jax import lax
from jax.experimental import pallas as pl
from jax.experimental.pallas import tpu as pltpu

F32 = jnp.float32
BF16 = jnp.bfloat16
HI = lax.Precision.HIGHEST

EPS = 1e-6
A_HEADS = 8
A_HEAD_DIM = 64
A_WIDTH = A_HEADS * A_HEAD_DIM
A_BLOCK = 128
DILATED_CONFIGS = ((128, 1), (512, 4), (2048, 16))
B_GROUP = 16
B_STATE = 64
REL_BUCKETS = 32
REL_MAX_DIST = 2048
C_HEADS = 8
C_DK = 128
C_DV = 128
C_CONV = 4
MASK_NEG = -1e30
LOG2_E = math.log2(math.e)

LANES = 128
SUBLANES = 8
VMEM_LIMIT = 48 * 1024 * 1024
ROW_TILE = 512
OUT_C_TILE = 1024
COL_CHUNK = 512
SPAN_RES = max(dl for _, dl in DILATED_CONFIGS)
SPAN = A_BLOCK * SPAN_RES
SPAN_RUN = ROW_TILE // SPAN_RES
ATTN_BLOCKS_PER_STEP = 8
S5_J = 32
S5_LW = 512
S5_TILES_PER_STEP = 4
GDN_STEP = 128
GDN_CHUNK = 64
GDN_PREP_BLOCKS = 4
GDN_REC_BLOCKS = 2

_NT = (((1,), (1,)), ((), ()))


def _cparams(sem):
    return pltpu.CompilerParams(dimension_semantics=sem, vmem_limit_bytes=VMEM_LIMIT)


def _sigmoid(x):
    return 1.0 / (1.0 + jnp.exp(-x))


def _mod_kernel(c_ref, w_ref, b_ref, o_ref):
    c = c_ref[...]
    ca = c * _sigmoid(c)
    o_ref[0] = jnp.dot(ca.astype(BF16), w_ref[0].astype(BF16), preferred_element_type=F32) + b_ref[0]


def _adaln_mod(c, ada_w, ada_b):
    depth, d, d3 = ada_w.shape
    bsz = c.shape[0]
    return pl.pallas_call(
        _mod_kernel,
        out_shape=jax.ShapeDtypeStruct((depth, bsz, d3), F32),
        grid=(depth, d3 // d),
        in_specs=[pl.BlockSpec((bsz, d), lambda l, j: (0, 0)),
                  pl.BlockSpec((1, d, d), lambda l, j: (l, 0, j)),
                  pl.BlockSpec((1, 1, d), lambda l, j: (l, 0, j))],
        out_specs=pl.BlockSpec((1, bsz, d), lambda l, j: (l, 0, j)),
        compiler_params=_cparams(("arbitrary", "arbitrary")),
        name="adaln_mod",
    )(c, ada_w, ada_b.reshape(depth, 1, d3))


def _span_perm():
    rho = np.arange(ROW_TILE)
    nat = SPAN_RES * (rho % SPAN_RUN) + rho // SPAN_RUN
    return (nat[:, None] == np.arange(ROW_TILE)[None, :]).astype(np.float32)


def _modulated_norm(x_ref, mod_ref, g_ref, d_model):
    x = x_ref[...]
    ms = jnp.mean(x * x, axis=-1, keepdims=True)
    y = x * lax.rsqrt(ms + EPS) * g_ref[...]
    shift = mod_ref[0, :, 0:d_model]
    scale = mod_ref[0, :, d_model:2 * d_model]
    return (y * (1.0 + scale) + shift).astype(BF16)


def _sublane_major_perm(tile):
    rho = np.arange(tile)
    src = (rho % SUBLANES) * (tile // SUBLANES) + rho // SUBLANES
    return (src[:, None] == np.arange(tile)[None, :]).astype(np.float32)


def _in_proj_kernel(x_ref, mod_ref, g_ref, w_ref, *rest, splits, d_model, n_span):
    pm_ref = rest[0] if n_span else None
    out_refs = rest[1:] if n_span else rest
    span_refs = out_refs[len(splits):]
    h = _modulated_norm(x_ref, mod_ref, g_ref, d_model)
    for idx, ((c0, width, mult), o_ref) in enumerate(zip(splits, out_refs)):
        for cc in range(0, width, COL_CHUNK):
            cw = min(COL_CHUNK, width - cc)
            acc = jnp.dot(h, w_ref[:, c0 + cc:c0 + cc + cw], preferred_element_type=F32)
            if mult != 1.0:
                acc = acc * mult
            val = acc.astype(o_ref.dtype)
            o_ref[:, cc:cc + cw] = val
            if idx < n_span:
                moved = jnp.dot(pm_ref[...], val, preferred_element_type=F32).astype(BF16)
                span_refs[idx][:, :, cc:cc + cw] = moved.reshape(SPAN_RES, SPAN_RUN, cw)


def _in_proj(x2, mod3, gain, w_bf16, splits, out_dtypes, seq, n_span=0):
    t, d = x2.shape
    tiles_per_seq = seq // ROW_TILE
    tiles_per_span = SPAN // ROW_TILE
    n_w = w_bf16.shape[1]
    row = lambda wd: pl.BlockSpec((ROW_TILE, wd), lambda i: (i, 0))
    span_shape = lambda wd: jax.ShapeDtypeStruct((t // SPAN, SPAN_RES, A_BLOCK, wd), BF16)
    span_spec = lambda wd: pl.BlockSpec((None, SPAN_RES, SPAN_RUN, wd),
                                        lambda i: (i // tiles_per_span, 0, i % tiles_per_span, 0))
    perm_in = [jnp.asarray(_span_perm(), BF16)] if n_span else []
    perm_spec = [pl.BlockSpec((ROW_TILE, ROW_TILE), lambda i: (0, 0))] if n_span else []
    return pl.pallas_call(
        functools.partial(_in_proj_kernel, splits=splits, d_model=d, n_span=n_span),
        out_shape=([jax.ShapeDtypeStruct((t, wd), dt) for (_, wd, _), dt in zip(splits, out_dtypes)]
                   + [span_shape(wd) for (_, wd, _) in splits[:n_span]]),
        grid=(t // ROW_TILE,),
        in_specs=[row(d),
                  pl.BlockSpec((1, 1, 3 * d), lambda i: (i // tiles_per_seq, 0, 0)),
                  pl.BlockSpec((1, d), lambda i: (0, 0)),
                  pl.BlockSpec((d, n_w), lambda i: (0, 0))] + perm_spec,
        out_specs=[row(wd) for (_, wd, _) in splits] + [span_spec(wd) for (_, wd, _) in splits[:n_span]],
        compiler_params=_cparams(("arbitrary",)),
        name="in_proj",
    )(x2, mod3, gain.reshape(1, d), w_bf16, *perm_in)


def _t5_bucket(dist):
    dist = np.maximum(dist, 0)
    max_exact = REL_BUCKETS // 2
    large = max_exact + (np.log(np.maximum(dist, 1) / max_exact)
                         / math.log(REL_MAX_DIST / max_exact) * (REL_BUCKETS - max_exact)).astype(np.int32)
    large = np.minimum(large, REL_BUCKETS - 1)
    return np.where(dist < max_exact, dist, large).astype(np.int32)


def _bias_kernel(rb_ref, bucket_ref, mask_ref, o_ref):
    def body(r, carry):
        rows = pl.ds(pl.multiple_of(r * SUBLANES, SUBLANES), SUBLANES)
        bk = bucket_ref[0, rows, :]
        accs = [jnp.zeros(bk.shape, F32) for _ in range(A_HEADS)]
        for b in range(REL_BUCKETS):
            eq = bk == b
            accs = [jnp.where(eq, rb_ref[b, h], acc) for h, acc in enumerate(accs)]
        for f in range(2):
            keep = mask_ref[0, f, rows, :] != 0
            for h in range(A_HEADS):
                o_ref[0, f, h, rows, :] = jnp.where(keep, accs[h] * LOG2_E, MASK_NEG)
        return carry
    lax.fori_loop(0, A_BLOCK // SUBLANES, body, 0)


def _attn_bias_tables(rel_bias):
    qi = np.arange(A_BLOCK)[:, None]
    kj = np.arange(2 * A_BLOCK)[None, :]
    rel = qi + A_BLOCK - kj
    buckets, masks = [], []
    for window, dil in DILATED_CONFIGS:
        band = (rel >= 0) & (rel <= window // dil)
        bucket = _t5_bucket(rel * dil)
        mask = np.stack([band & (kj >= A_BLOCK), band]).astype(np.int32)
        if dil > 1:
            runs = SPAN_RES // dil
            run = A_BLOCK // runs
            rho = np.arange(A_BLOCK)
            sub = runs * (rho % run) + rho // run
            keys = np.concatenate([sub, A_BLOCK + sub])
            bucket = bucket[sub][:, keys]
            mask = mask[:, sub][:, :, keys]
        buckets.append(bucket)
        masks.append(mask)
    n_cfg = len(DILATED_CONFIGS)
    return pl.pallas_call(
        _bias_kernel,
        out_shape=jax.ShapeDtypeStruct((n_cfg, 2, A_HEADS, A_BLOCK, 2 * A_BLOCK), F32),
        grid=(n_cfg,),
        in_specs=[pl.BlockSpec(memory_space=pltpu.SMEM),
                  pl.BlockSpec((1, A_BLOCK, 2 * A_BLOCK), lambda c: (c, 0, 0)),
                  pl.BlockSpec((1, 2, A_BLOCK, 2 * A_BLOCK), lambda c: (c, 0, 0, 0))],
        out_specs=pl.BlockSpec((1, 2, A_HEADS, A_BLOCK, 2 * A_BLOCK), lambda c: (c, 0, 0, 0, 0)),
        compiler_params=_cparams(("arbitrary",)),
        name="attn_bias",
    )(rel_bias.astype(F32), jnp.asarray(np.stack(buckets)), jnp.asarray(np.stack(masks)))


def _attn_block(q, kcat, vcat, bias_ref):
    lane = lax.broadcasted_iota(jnp.int32, (A_BLOCK, LANES), 1)
    low = lane < A_HEAD_DIM
    lse_tile = jnp.zeros((A_BLOCK, LANES), F32)
    zero = jnp.zeros((A_BLOCK, LANES), BF16)
    heads = range(A_HEADS)
    pair = [slice((h // 2) * LANES, (h // 2 + 1) * LANES) for h in heads]
    scores = []
    for h in heads:
        qm = jnp.where(low if h % 2 == 0 else jnp.logical_not(low), q[:, pair[h]], zero)
        scores.append(lax.dot_general(qm, kcat[:, pair[h]], _NT, preferred_element_type=F32) + bias_ref[0, h])
    probs, inv_l = [], []
    for h in heads:
        m = jnp.max(scores[h], axis=-1, keepdims=True)
        p = jnp.exp2(scores[h] - m)
        l = jnp.sum(p, axis=-1, keepdims=True)
        probs.append(p.astype(BF16))
        inv_l.append(1.0 / l)
        lse_tile = jnp.where(lane == h, m + jnp.log2(l), lse_tile)
    outs = [jnp.dot(probs[h], vcat[:, pair[h]], preferred_element_type=F32) * inv_l[h] for h in heads]
    o = jnp.concatenate([jnp.where(low, outs[2 * hp], outs[2 * hp + 1]) for hp in range(A_HEADS // 2)], axis=1)
    return o, lse_tile


def _attn_kernel(q_ref, kp_ref, kc_ref, vp_ref, vc_ref, bias0_ref, bias_ref, o_ref, lse_ref, *, per_lead):
    def sub(ref, rr, i):
        if ref.ndim == 2:
            return ref[i * A_BLOCK:(i + 1) * A_BLOCK, :]
        run = ref.shape[3] // per_lead
        return ref[i // per_lead, :, rr, (i % per_lead) * run:(i % per_lead + 1) * run, :]

    def put(ref, rr, i, val):
        if ref.ndim == 2:
            ref[i * A_BLOCK:(i + 1) * A_BLOCK, :] = val.astype(ref.dtype)
        else:
            run = ref.shape[3] // per_lead
            ref[i // per_lead, :, rr, (i % per_lead) * run:(i % per_lead + 1) * run, :] = (
                val.astype(ref.dtype).reshape(ref.shape[1], run, ref.shape[4]))

    flat = lambda v: v.reshape(A_BLOCK, v.shape[-1])
    token_order = q_ref.ndim == 2
    n_blocks = q_ref.shape[0] // A_BLOCK if token_order else q_ref.shape[0] * per_lead
    for rr in range(1 if token_order else q_ref.shape[2]):
        k_prev = flat(kp_ref[...] if token_order else kp_ref[0, :, rr])
        v_prev = flat(vp_ref[...] if token_order else vp_ref[0, :, rr])
        for i in range(n_blocks):
            k_cur, v_cur = flat(sub(kc_ref, rr, i)), flat(sub(vc_ref, rr, i))
            o, lse_tile = _attn_block(flat(sub(q_ref, rr, i)), jnp.concatenate([k_prev, k_cur], axis=0),
                                      jnp.concatenate([v_prev, v_cur], axis=0),
                                      bias0_ref if i == 0 else bias_ref)
            put(o_ref, rr, i, o)
            put(lse_ref, rr, i, lse_tile)
            k_prev, v_prev = k_cur, v_cur


def _dilated_attention_one(q, k, v, bias_tab, bsz, seq, dil):
    nb = seq // dil // A_BLOCK
    step = min(ATTN_BLOCKS_PER_STEP, nb)
    w = A_WIDTH
    if dil == 1:
        per_lead = n_res = 1
        view = lambda a: a.reshape(bsz, seq, a.shape[-1])
        block = lambda wd: (None, step * A_BLOCK, wd)
        block_prev = lambda wd: (None, A_BLOCK, wd)
        cur = lambda b, r, m: (b, m, 0)
        prev = lambda b, r, m: (b, jnp.maximum(m * step - 1, 0), 0)
        out_shape = lambda wd, dt: jax.ShapeDtypeStruct((bsz, seq, wd), dt)
    else:
        runs = SPAN_RES // dil
        run = A_BLOCK // runs
        spans_per_seq = seq // SPAN
        per_lead = min(step, runs)
        lead = step // per_lead
        n_res = max(1, ATTN_BLOCKS_PER_STEP // step)
        assert runs % per_lead == 0 and spans_per_seq % lead == 0 and dil % n_res == 0
        view = lambda a: a.reshape(a.shape[0], runs, dil, A_BLOCK, a.shape[-1])
        block = lambda wd: (lead, runs, n_res, per_lead * run, wd)
        block_prev = lambda wd: (1, runs, n_res, run, wd)
        cur = lambda b, r, m: ((b * spans_per_seq + (m * step) // runs) // lead, 0, r, ((m * step) % runs) // per_lead, 0)

        def prev(b, r, m):
            n = jnp.maximum(m * step - 1, 0)
            return (b * spans_per_seq + n // runs, 0, r, n % runs, 0)

        out_shape = lambda wd, dt: jax.ShapeDtypeStruct((bsz * spans_per_seq, runs, dil, A_BLOCK, wd), dt)
    blk = pl.BlockSpec(block(w), cur)
    blk_prev = pl.BlockSpec(block_prev(w), prev)
    bias_block = (1, A_HEADS, A_BLOCK, 2 * A_BLOCK)
    o, lse = pl.pallas_call(
        functools.partial(_attn_kernel, per_lead=per_lead),
        out_shape=[out_shape(w, BF16), out_shape(LANES, F32)],
        grid=(bsz, dil // n_res, nb // step),
        in_specs=[blk, blk_prev, blk, blk_prev, blk,
                  pl.BlockSpec(bias_block, lambda b, r, m: (jnp.minimum(m, 1), 0, 0, 0)),
                  pl.BlockSpec(bias_block, lambda b, r, m: (1, 0, 0, 0))],
        out_specs=[pl.BlockSpec(block(w), cur), pl.BlockSpec(block(LANES), cur)],
        compiler_params=_cparams(("arbitrary", "arbitrary", "arbitrary")),
        name=f"dilated_attn_d{dil}",
    )(view(q), view(k), view(k), view(v), view(v), bias_tab, bias_tab)
    if dil == 1:
        return o.reshape(bsz * seq, w), lse.reshape(bsz * seq, LANES)
    return o.reshape(o.shape[0], SPAN_RES, A_BLOCK, w), lse.reshape(o.shape[0], SPAN_RES, A_BLOCK, LANES)


def _gelu_tanh(x):
    c = math.sqrt(2.0 / math.pi)
    return x * (0.5 * (1.0 + jnp.tanh(c * (x + 0.044715 * (x * x * x)))))


def _s5_kernel(u_ref, pm_ref, pmt_ref, bd_ref, pw_ref, pw16_ref, cd_ref, dsk_ref, gw_ref, gb_ref, o_ref,
               x_scr, st_scr, *, n_state):
    @pl.when(pl.program_id(1) == 0)
    def _():
        st_scr[...] = jnp.zeros_like(st_scr)

    for tile_idx in range(S5_TILES_PER_STEP):
        _s5_tile(tile_idx, u_ref, pm_ref, pmt_ref, bd_ref, pw_ref, pw16_ref, cd_ref, dsk_ref, gw_ref, gb_ref,
                 o_ref, x_scr, st_scr, n_state)


def _s5_tile(tile_idx, u_ref, pm_ref, pmt_ref, bd_ref, pw_ref, pw16_ref, cd_ref, dsk_ref, gw_ref, gb_ref,
             o_ref, x_scr, st_scr, n_state):
    n = n_state
    jn = S5_J
    tile_rows = slice(tile_idx * SUBLANES * jn, (tile_idx + 1) * SUBLANES * jn)

    u = u_ref[tile_rows, :]
    u_perm = jnp.dot(pm_ref[...], u.astype(BF16), preferred_element_type=F32).astype(BF16)
    cw = u.shape[1] * S5_LW // n
    chunks = [(slice(c0, c0 + S5_LW), slice(n + c0, n + c0 + S5_LW)) for c0 in range(0, n, S5_LW)]

    def project_in(q):
        bu = jnp.dot(u_perm[:, q * cw:(q + 1) * cw], bd_ref[q], preferred_element_type=F32)
        x_scr[:, chunks[q][0]] = bu[:, 0:S5_LW]
        x_scr[:, chunks[q][1]] = bu[:, S5_LW:]

    project_in(0)
    end_r, end_i = [], []
    for q, (re, im) in enumerate(chunks):
        if q + 1 < len(chunks):
            project_in(q + 1)
        a1r, a1i = pw_ref[0:SUBLANES, re], pw_ref[0:SUBLANES, im]
        xr = jnp.zeros((SUBLANES, S5_LW), F32)
        xi = jnp.zeros((SUBLANES, S5_LW), F32)
        for j in range(jn):
            rows = slice(SUBLANES * j, SUBLANES * (j + 1))
            nr = a1r * xr - a1i * xi + x_scr[rows, re]
            ni = a1r * xi + a1i * xr + x_scr[rows, im]
            xr, xi = nr, ni
            x_scr[rows, re] = xr
            x_scr[rows, im] = xi
        end_r.append(xr)
        end_i.append(xi)
    er, ei = jnp.concatenate(end_r, axis=1), jnp.concatenate(end_i, axis=1)
    last = SUBLANES * (jn - 1)
    ajr, aji = pw_ref[last:last + 1, 0:n], pw_ref[last:last + 1, n:]
    pr, pi = st_scr[0:1, 0:n], st_scr[0:1, n:]
    cin_r, cin_i = [], []
    for s in range(SUBLANES):
        cin_r.append(pr)
        cin_i.append(pi)
        nr = er[s:s + 1] + ajr * pr - aji * pi
        ni = ei[s:s + 1] + ajr * pi + aji * pr
        pr, pi = nr, ni
    st_scr[0:1, 0:n] = pr
    st_scr[0:1, n:] = pi
    cr_all = jnp.concatenate(cin_r, axis=0)
    ci_all = jnp.concatenate(cin_i, axis=0)
    pack = 2 * SUBLANES
    ys = []
    for q, (re, im) in enumerate(chunks):
        cr = jnp.concatenate([cr_all[:, re]] * 2, axis=0).astype(BF16)
        ci = jnp.concatenate([ci_all[:, re]] * 2, axis=0).astype(BF16)
        xr_parts, xi_parts = [], []
        for m in range(SUBLANES * jn // pack):
            rows = slice(pack * m, pack * (m + 1))
            pjr, pji = pw16_ref[rows, re], pw16_ref[rows, im]
            xr_parts.append(x_scr[rows, re].astype(BF16) + (pjr * cr - pji * ci))
            xi_parts.append(x_scr[rows, im].astype(BF16) + (pjr * ci + pji * cr))
        xcat = jnp.concatenate([jnp.concatenate(xr_parts, axis=0), jnp.concatenate(xi_parts, axis=0)], axis=1)
        ys.append(jnp.dot(xcat, cd_ref[q], preferred_element_type=F32))

    y_perm = jnp.concatenate(ys, axis=1)
    y_hi = y_perm.astype(BF16)
    y_lo = (y_perm - y_hi.astype(F32)).astype(BF16)
    y = (jnp.dot(pmt_ref[...], y_hi, preferred_element_type=F32)
         + jnp.dot(pmt_ref[...], y_lo, preferred_element_type=F32))
    y = _gelu_tanh(y + dsk_ref[...] * u)
    z = jnp.dot(y.astype(BF16), gw_ref[...], preferred_element_type=F32) + gb_ref[...]
    o_ref[tile_rows, :] = (y * _sigmoid(z)).astype(o_ref.dtype)


def _s5_tables(a_re, a_im, log_dt, b_re, b_im, c_re, c_im):
    g, p = a_re.shape
    dt = jnp.exp(log_dt.astype(F32))[:, None]
    ar, ai = a_re.astype(F32), a_im.astype(F32)
    mag = jnp.exp(dt * ar)
    abar_r, abar_i = mag * jnp.cos(dt * ai), mag * jnp.sin(dt * ai)
    den = ar * ar + ai * ai
    fr = ((abar_r - 1.0) * ar + abar_i * ai) / den
    fi = (abar_i * ar - (abar_r - 1.0) * ai) / den
    br, bi = b_re.astype(F32), b_im.astype(F32)
    bbar_r = fr[..., None] * br - fi[..., None] * bi
    bbar_i = fr[..., None] * bi + fi[..., None] * br
    m = br.shape[-1]
    gc = S5_LW // p
    nq = g // gc
    eye = jnp.eye(gc, dtype=F32)
    dense_b = lambda t: jnp.einsum('qgpm,gh->qgmhp', t.reshape(nq, gc, p, m), eye).reshape(nq, gc * m, gc * p)
    bd = jnp.concatenate([dense_b(bbar_r), dense_b(bbar_i)], axis=2)
    dense_c = lambda t: jnp.einsum('qgmp,gh->qgphm', t.reshape(nq, gc, m, p), eye).reshape(nq, gc * p, gc * m)
    cd = jnp.concatenate([dense_c(c_re.astype(F32)), -dense_c(c_im.astype(F32))], axis=1)
    kk = jnp.arange(1, S5_J + 1, dtype=F32)[:, None, None]
    pmag = jnp.exp(kk * (dt * ar)[None])
    pw_r = (pmag * jnp.cos(kk * (dt * ai)[None])).reshape(S5_J, g * p)
    pw_i = (pmag * jnp.sin(kk * (dt * ai)[None])).reshape(S5_J, g * p)
    pw = jnp.repeat(jnp.concatenate([pw_r, pw_i], axis=1), SUBLANES, axis=0)
    return bd.astype(BF16), pw, cd.astype(BF16)


def _s5_layer(u, bsz, seq, a_re, a_im, log_dt, b_re, b_im, c_re, c_im, d_skip, glu_w, glu_b):
    t, width = u.shape
    n = a_re.shape[0] * a_re.shape[1]
    bd, pw, cd = _s5_tables(a_re, a_im, log_dt, b_re, b_im, c_re, c_im)
    tile = SUBLANES * S5_J
    step_rows = S5_TILES_PER_STEP * tile
    steps_per_seq = seq // step_rows
    const = lambda b, i: (0, 0)
    perm = _sublane_major_perm(tile)
    return pl.pallas_call(
        functools.partial(_s5_kernel, n_state=n),
        out_shape=jax.ShapeDtypeStruct((t, width), BF16),
        grid=(bsz, steps_per_seq),
        in_specs=[pl.BlockSpec((step_rows, width), lambda b, i: (b * steps_per_seq + i, 0)),
                  pl.BlockSpec((tile, tile), const),
                  pl.BlockSpec((tile, tile), const),
                  pl.BlockSpec(bd.shape, lambda b, i: (0, 0, 0)),
                  pl.BlockSpec((tile, 2 * n), const),
                  pl.BlockSpec((tile, 2 * n), const),
                  pl.BlockSpec(cd.shape, lambda b, i: (0, 0, 0)),
                  pl.BlockSpec((1, width), const),
                  pl.BlockSpec((width, width), const),
                  pl.BlockSpec((1, width), const)],
        out_specs=pl.BlockSpec((step_rows, width), lambda b, i: (b * steps_per_seq + i, 0)),
        scratch_shapes=[pltpu.VMEM((tile, 2 * n), F32), pltpu.VMEM((SUBLANES, 2 * n), F32)],
        compiler_params=_cparams(("arbitrary", "arbitrary")),
        name="s5_layer",
    )(u, jnp.asarray(perm, BF16), jnp.asarray(perm.T, BF16), bd, pw, pw.astype(BF16), cd,
      d_skip.reshape(1, width).astype(F32), glu_w.astype(BF16),
      glu_b.reshape(1, width).astype(F32))


def _finish(y, x_ref, mod_ref, pg_ref, out_ref, d_model):
    ms = jnp.mean(y * y, axis=-1, keepdims=True)
    yn = y * lax.rsqrt(ms + EPS) * pg_ref[...]
    gate_mod = mod_ref[0, :, 2 * d_model:3 * d_model]
    out_ref[...] = x_ref[...] + gate_mod * yn


def _split2(x):
    hi = x.astype(BF16)
    return hi, (x - hi.astype(F32)).astype(BF16)


def _out_ab_kernel(o0_ref, l0_ref, o1_ref, o2_ref, l1_ref, l2_ref, ob_ref, gate_ref, e_ref, un_ref, w_ref,
                   x_ref, mod_ref, pg_ref, out_ref, *, d_model):
    flat = lambda ref: ref[...].reshape(ROW_TILE, ref.shape[-1])
    aw = o0_ref.shape[-1]

    def unperm(o_ref, l_ref):
        parts = jnp.concatenate([flat(o_ref)] + list(_split2(flat(l_ref))), axis=1)
        moved = jnp.dot(un_ref[...], parts, preferred_element_type=F32)
        return moved[:, 0:aw], moved[:, aw:aw + LANES] + moved[:, aw + LANES:]

    o0, l0 = o0_ref[...].astype(F32), l0_ref[...]
    o1, l1 = unperm(o1_ref, l1_ref)
    o2, l2 = unperm(o2_ref, l2_ref)
    mx = jnp.maximum(jnp.maximum(l0, l1), l2)
    e0, e1, e2 = jnp.exp2(l0 - mx), jnp.exp2(l1 - mx), jnp.exp2(l2 - mx)
    inv_den = 1.0 / (e0 + e1 + e2)
    stacked = jnp.concatenate(list(_split2(e1 * inv_den)) + list(_split2(e2 * inv_den)), axis=0)
    wide = jnp.dot(stacked, e_ref[...], preferred_element_type=F32)
    term = lambda i: wide[i * ROW_TILE:(i + 1) * ROW_TILE]
    w1 = term(0) + term(1)
    w2 = term(2) + term(3)
    o_a = (1.0 - w1 - w2) * o0 + w1 * o1 + w2 * o2
    gate = gate_ref[...].astype(F32)
    sg = gate * _sigmoid(gate)
    aw = o_a.shape[-1]
    y = jnp.dot((o_a * sg[:, :aw]).astype(BF16), w_ref[0:aw, :], preferred_element_type=F32)
    y = y + jnp.dot((ob_ref[...].astype(F32) * sg[:, aw:]).astype(BF16), w_ref[aw:, :],
                    preferred_element_type=F32)
    _finish(y, x_ref, mod_ref, pg_ref, out_ref, d_model)


def _out_c_kernel(o_ref, gate_ref, w_ref, x_ref, mod_ref, pg_ref, out_ref, *, d_model):
    gate = gate_ref[...].astype(F32)
    o = o_ref[...].astype(F32) * (gate * _sigmoid(gate))
    y = jnp.dot(o.astype(BF16), w_ref[...], preferred_element_type=F32)
    _finish(y, x_ref, mod_ref, pg_ref, out_ref, d_model)


def _out_proj(kern, row_inputs, const_inputs, w_bf16, x2, mod3, post_g, seq, name, tile=ROW_TILE):
    t, d = x2.shape
    tiles_per_seq = seq // tile
    tiles_per_span = SPAN // tile

    def row_spec(a):
        if a.ndim == 2:
            return pl.BlockSpec((tile, a.shape[1]), lambda i: (i, 0))
        assert tile == ROW_TILE
        return pl.BlockSpec((None, SPAN_RES, SPAN_RUN, a.shape[-1]),
                            lambda i: (i // tiles_per_span, 0, i % tiles_per_span, 0))

    const_spec = lambda a: pl.BlockSpec(a.shape, lambda i: (0, 0))
    return pl.pallas_call(
        functools.partial(kern, d_model=d),
        out_shape=jax.ShapeDtypeStruct((t, d), F32),
        grid=(t // tile,),
        in_specs=([row_spec(a) for a in row_inputs] + [const_spec(a) for a in const_inputs]
                  + [const_spec(w_bf16), row_spec(x2),
                     pl.BlockSpec((1, 1, 3 * d), lambda i: (i // tiles_per_seq, 0, 0)),
                     pl.BlockSpec((1, d), lambda i: (0, 0))]),
        out_specs=pl.BlockSpec((tile, d), lambda i: (i, 0)),
        compiler_params=_cparams(("arbitrary",)),
        name=name,
    )(*row_inputs, *const_inputs, w_bf16, x2, mod3, post_g.reshape(1, d))


def _gdn_prep_kernel(x_ref, halo_ref, br_ref, cw_ref, alog_ref, dtb_ref,
                     w_ref, u_ref, qg_ref, kdt_ref, aqk_ref, dec_ref, xs_scr):
    n_all = GDN_PREP_BLOCKS * GDN_STEP
    first = pl.program_id(1) == 0
    halo = halo_ref[...]
    xs_scr[0:SUBLANES, :] = jnp.where(first, jnp.zeros_like(halo), halo)
    xs_scr[SUBLANES:SUBLANES + n_all, :] = x_ref[...]
    cw = cw_ref[...]
    conv = xs_scr[SUBLANES:SUBLANES + n_all, :] * cw[C_CONV - 1:C_CONV]
    for j in range(C_CONV - 1):
        off = SUBLANES - (C_CONV - 1) + j
        conv = conv + xs_scr[off:off + n_all, :] * cw[j:j + 1]
    act = conv * _sigmoid(conv)
    _gdn_prep_blocks(act, br_ref, alog_ref, dtb_ref, w_ref, u_ref, qg_ref, kdt_ref, aqk_ref, dec_ref)


def _gdn_prep_blocks(act, br_ref, alog_ref, dtb_ref, w_ref, u_ref, qg_ref, kdt_ref, aqk_ref, dec_ref):
    n = GDN_STEP
    row = lax.broadcasted_iota(jnp.int32, (n, n), 0)
    col = lax.broadcasted_iota(jnp.int32, (n, n), 1)
    same = (row // GDN_CHUNK) == (col // GDN_CHUNK)
    tril = jnp.logical_and(same, row >= col)
    strict = jnp.logical_and(same, row > col)
    eye = (row == col).astype(F32)
    first_chunk = lax.broadcasted_iota(jnp.int32, (n, 1), 0) < GDN_CHUNK
    dec_row = lax.broadcasted_iota(jnp.int32, (SUBLANES, LANES), 0)
    heads = range(C_HEADS)
    lanes = [slice(h * LANES, (h + 1) * LANES) for h in heads]

    def gates(blk):
        rows = slice(blk * n, (blk + 1) * n)
        br = br_ref[rows, :]
        xg = br + dtb_ref[...]
        softplus = jnp.maximum(xg, 0.0) + jnp.log(1.0 + jnp.exp(-jnp.abs(xg)))
        g_all = -jnp.exp(alog_ref[...]) * softplus
        gc_all = jnp.dot(tril.astype(F32), g_all, preferred_element_type=F32, precision=HI)
        return dict(rows=rows, beta=_sigmoid(br), gc=gc_all, gc_t=gc_all.T)

    def first_stage(ctx, blk, h):
        rows = ctx["rows"]
        gcol = ctx["gc"][:, C_HEADS + h:C_HEADS + h + 1]
        grow = ctx["gc_t"][C_HEADS + h:C_HEADS + h + 1, :]
        eg = jnp.exp(gcol)
        decay = jnp.where(tril, jnp.exp(jnp.where(tril, gcol - grow, 0.0)), 0.0)
        q = act[rows, h * C_DK:(h + 1) * C_DK]
        k = act[rows, (C_HEADS + h) * C_DK:(C_HEADS + h + 1) * C_DK]
        v = act[rows, 2 * C_HEADS * C_DK + h * C_DV:2 * C_HEADS * C_DK + (h + 1) * C_DV]
        q = q * lax.rsqrt(jnp.sum(q * q, axis=-1, keepdims=True) + EPS) * (C_DK ** -0.5)
        k = k * lax.rsqrt(jnp.sum(k * k, axis=-1, keepdims=True) + EPS)
        beta = ctx["beta"][:, h:h + 1]
        kb = k * beta
        k16 = k.astype(BF16)
        a = jnp.where(strict, lax.dot_general(kb.astype(BF16), k16, _NT, preferred_element_type=F32) * decay, 0.0)
        aqk = lax.dot_general(q.astype(BF16), k16, _NT, preferred_element_type=F32) * decay
        aqk_ref[rows, lanes[h]] = aqk.astype(BF16)
        qg_ref[rows, lanes[h]] = (q * eg).astype(BF16)
        g_last = jnp.where(first_chunk, gcol[GDN_CHUNK - 1:GDN_CHUNK], gcol[n - 1:n])
        kdt_ref[rows, lanes[h]] = (k * jnp.exp(g_last - gcol)).T.astype(BF16)
        dec_ref[blk, :, lanes[h]] = jnp.where(dec_row < SUBLANES // 2, eg[GDN_CHUNK - 1:GDN_CHUNK], eg[n - 1:n])
        rhs = jnp.concatenate([v * beta, kb * eg], axis=1).astype(BF16)
        return dict(a=a, inv=eye - a, rhs=rhs)

    n_blocks = act.shape[0] // n
    first_ctx = gates(0)
    state = [first_stage(first_ctx, 0, h) for h in heads]
    for blk in range(n_blocks):
        nxt_ctx = gates(blk + 1) if blk + 1 < n_blocks else None
        nxt = []

        def side_work(hs, blk=blk, nxt=nxt, nxt_ctx=nxt_ctx):
            if nxt_ctx is not None:
                nxt.extend(first_stage(nxt_ctx, blk + 1, h) for h in hs)

        invs = [st["inv"] for st in state]
        pows = []
        for st in state:
            a16 = st["a"].astype(BF16)
            pows.append(jnp.dot(a16, a16, preferred_element_type=F32))
        side_work(heads[0:2])
        for level in range(4):
            for h in heads:
                p16 = pows[h].astype(BF16)
                both = jnp.dot(jnp.concatenate([invs[h].astype(BF16), p16], axis=0), p16,
                               preferred_element_type=F32)
                invs[h] = invs[h] + both[0:n]
                pows[h] = both[n:]
            side_work(heads[2 + level:3 + level])
        for h in heads:
            invs[h] = invs[h] + jnp.dot(invs[h].astype(BF16), pows[h].astype(BF16), preferred_element_type=F32)
        side_work(heads[6:8])
        rows = slice(blk * n, (blk + 1) * n)
        for h in heads:
            uw = jnp.dot(invs[h].astype(BF16), state[h]["rhs"], preferred_element_type=F32)
            u_ref[rows, lanes[h]] = uw[:, 0:C_DV].astype(u_ref.dtype)
            w_ref[rows, lanes[h]] = uw[:, C_DV:].astype(BF16)
        state = nxt


def _gdn_rec_kernel(w_ref, u_ref, qg_ref, kdt_ref, aqk_ref, dec_ref, ng_ref, o_ref, s_scr):
    @pl.when(pl.program_id(0) == 0)
    def _():
        s_scr[...] = jnp.zeros_like(s_scr)

    half = SUBLANES // 2
    zeros = jnp.zeros((GDN_CHUNK, C_DV), BF16)
    lanes = [slice(h * LANES, (h + 1) * LANES) for h in range(C_HEADS)]
    chains = [(b, h) for b in range(w_ref.shape[0]) for h in range(C_HEADS)]
    states = [s_scr[b * C_HEADS + h] for b, h in chains]
    for blk in range(GDN_REC_BLOCKS):
        blk_rows = slice(blk * GDN_STEP, (blk + 1) * GDN_STEP)
        for j in range(GDN_STEP // GDN_CHUNK):
            rows = slice(blk * GDN_STEP + j * GDN_CHUNK, blk * GDN_STEP + (j + 1) * GDN_CHUNK)
            wss = []
            for c, (b, h) in enumerate(chains):
                wq = jnp.concatenate([w_ref[b, rows, lanes[h]], qg_ref[b, rows, lanes[h]]], axis=0)
                wss.append(jnp.dot(wq, states[c].astype(BF16), preferred_element_type=F32))
            for c, (b, h) in enumerate(chains):
                v_new = (u_ref[b, rows, lanes[h]].astype(F32) - wss[c][0:GDN_CHUNK]).astype(BF16)
                v_pad = jnp.concatenate([v_new, zeros] if j == 0 else [zeros, v_new], axis=0)
                both = jnp.dot(jnp.concatenate([aqk_ref[b, rows, lanes[h]], kdt_ref[b, blk_rows, lanes[h]]], axis=0),
                               v_pad, preferred_element_type=F32)
                o = wss[c][GDN_CHUNK:] + both[0:GDN_CHUNK]
                dec = dec_ref[b, blk, j * half:j * half + 1, lanes[h]]
                states[c] = states[c] * dec + both[GDN_CHUNK:]
                ms = jnp.mean(o * o, axis=-1, keepdims=True)
                o_ref[b, rows, lanes[h]] = (o * lax.rsqrt(ms + EPS) * ng_ref[...]).astype(o_ref.dtype)
    for c, (b, h) in enumerate(chains):
        s_scr[b * C_HEADS + h] = states[c]


def _gdn_core(qkv_pre, br, bsz, seq, conv_w, a_log, dt_bias, norm_g):
    t, qkv_w = qkv_pre.shape
    hw = C_HEADS * C_DV
    steps_per_seq = seq // GDN_STEP
    prep_rows = GDN_PREP_BLOCKS * GDN_STEP
    prep_steps = steps_per_seq // GDN_PREP_BLOCKS
    halo_blocks = prep_rows // SUBLANES
    pad_row = lambda vec: jnp.zeros((1, LANES), F32).at[0, C_HEADS:2 * C_HEADS].set(vec.astype(F32))
    tok = lambda b, i: (b * prep_steps + i, 0)
    const = lambda b, i: (0, 0)
    tok_spec = pl.BlockSpec((prep_rows, hw), tok)
    dec_spec = pl.BlockSpec((GDN_PREP_BLOCKS, SUBLANES, hw), lambda b, i: (b * prep_steps + i, 0, 0))
    w, u, qg, kdt, aqk, dec = pl.pallas_call(
        _gdn_prep_kernel,
        out_shape=[jax.ShapeDtypeStruct((t, hw), BF16), jax.ShapeDtypeStruct((t, hw), BF16),
                   jax.ShapeDtypeStruct((t, hw), BF16), jax.ShapeDtypeStruct((t, hw), BF16),
                   jax.ShapeDtypeStruct((t, hw), BF16),
                   jax.ShapeDtypeStruct((t // GDN_STEP, SUBLANES, hw), F32)],
        grid=(bsz, prep_steps),
        in_specs=[pl.BlockSpec((prep_rows, qkv_w), tok),
                  pl.BlockSpec((SUBLANES, qkv_w),
                               lambda b, i: (jnp.maximum((b * prep_steps + i) * halo_blocks - 1, 0), 0)),
                  pl.BlockSpec((prep_rows, LANES), tok),
                  pl.BlockSpec((C_CONV, qkv_w), const),
                  pl.BlockSpec((1, LANES), const),
                  pl.BlockSpec((1, LANES), const)],
        out_specs=[tok_spec, tok_spec, tok_spec, tok_spec, tok_spec, dec_spec],
        scratch_shapes=[pltpu.VMEM((SUBLANES + prep_rows, qkv_w), F32)],
        compiler_params=_cparams(("arbitrary", "arbitrary")),
        name="gdn_prep",
    )(qkv_pre, qkv_pre, br, conv_w.astype(F32), pad_row(a_log), pad_row(dt_bias))
    rec_steps = steps_per_seq // GDN_REC_BLOCKS
    seq_view = lambda a: a.reshape(bsz, seq, hw)
    rec_tok = pl.BlockSpec((bsz, GDN_REC_BLOCKS * GDN_STEP, hw), lambda i: (0, i, 0))
    rec_dec = pl.BlockSpec((bsz, GDN_REC_BLOCKS, SUBLANES, hw), lambda i: (0, i, 0, 0))
    o = pl.pallas_call(
        _gdn_rec_kernel,
        out_shape=jax.ShapeDtypeStruct((bsz, seq, hw), BF16),
        grid=(rec_steps,),
        in_specs=[rec_tok, rec_tok, rec_tok, rec_tok, rec_tok, rec_dec,
                  pl.BlockSpec((1, C_DV), lambda i: (0, 0))],
        out_specs=rec_tok,
        scratch_shapes=[pltpu.VMEM((bsz * C_HEADS, C_DK, C_DV), F32)],
        compiler_params=_cparams(("arbitrary",)),
        name="gdn_recurrence",
    )(seq_view(w), seq_view(u), seq_view(qg), seq_view(kdt), seq_view(aqk),
      dec.reshape(bsz, steps_per_seq, SUBLANES, hw), norm_g.reshape(1, C_DV).astype(F32))
    return o.reshape(t, hw)


def _ab_layer(x2, mod3, bsz, seq, pre_g, post_g, rel_bias, w_in, w_out, s5_params):
    b_width = s5_params[7].shape[-1]
    assert DILATED_CONFIGS[0][1] == 1 and all(SPAN_RES % dl == 0 for _, dl in DILATED_CONFIGS) and seq % SPAN == 0
    head_expand = jnp.asarray(np.arange(LANES)[:, None] == (np.arange(A_WIDTH)[None, :] // A_HEAD_DIM), BF16)
    unperm = jnp.asarray(_span_perm().T, BF16)
    splits = ((0, A_WIDTH, A_HEAD_DIM ** -0.5 * LOG2_E), (A_WIDTH, A_WIDTH, 1.0), (2 * A_WIDTH, A_WIDTH, 1.0),
              (3 * A_WIDTH, b_width, 1.0), (3 * A_WIDTH + b_width, A_WIDTH + b_width, 1.0))
    q, k, v, u, gate, qs, ks, vs = _in_proj(x2, mod3, pre_g, w_in.astype(BF16), splits,
                                            (BF16, BF16, BF16, F32, BF16), seq, n_span=3)
    outs, lses = [], []
    bias_tabs = _attn_bias_tables(rel_bias)
    for cfg, (_, dil) in enumerate(DILATED_CONFIGS):
        qkv = (q, k, v) if dil == 1 else (qs, ks, vs)
        o_c, lse_c = _dilated_attention_one(*qkv, bias_tabs[cfg], bsz, seq, dil)
        outs.append(o_c)
        lses.append(lse_c)
    o_b = _s5_layer(u, bsz, seq, *s5_params)
    row_inputs = [outs[0], lses[0], outs[1], outs[2], lses[1], lses[2], o_b, gate]
    return _out_proj(_out_ab_kernel, row_inputs, [head_expand, unperm], w_out.astype(BF16),
                     x2, mod3, post_g, seq, "out_proj_ab")


def _gdn_layer(x2, mod3, bsz, seq, pre_g, post_g, w_in, conv_w, a_log, dt_bias, norm_g, w_out):
    d = x2.shape[1]
    qkv_w = 2 * C_HEADS * C_DK + C_HEADS * C_DV
    gate_w = C_HEADS * C_DV
    w_pad = jnp.concatenate(
        [w_in, jnp.zeros((d, LANES - (w_in.shape[1] - qkv_w - gate_w)), w_in.dtype)], axis=1).astype(BF16)
    splits = ((0, qkv_w, 1.0), (qkv_w, gate_w, 1.0), (qkv_w + gate_w, LANES, 1.0))
    qkv_pre, gate, br = _in_proj(x2, mod3, pre_g, w_pad, splits, (F32, BF16, F32), seq)
    o = _gdn_core(qkv_pre, br, bsz, seq, conv_w, a_log, dt_bias, norm_g)
    return _out_proj(_out_c_kernel, [o, gate], [], w_out.astype(BF16), x2, mod3, post_g, seq, "out_proj_c",
                     tile=OUT_C_TILE)


def kernel(x, c, ada_w, ada_b, pre_g, post_g, rel_bias, ab_w_in, ab_w_out, s5_a_re, s5_a_im, s5_log_dt, s5_b_re, s5_b_im, s5_c_re, s5_c_im, s5_d, s5_glu_w, s5_glu_b, gdn_w_in, gdn_conv, gdn_a_log, gdn_dt_bias, gdn_norm_g, gdn_w_out):
    bsz, seq, d = x.shape
    depth = ada_w.shape[0]
    assert seq % ROW_TILE == 0 and seq % OUT_C_TILE == 0 and seq % (SUBLANES * S5_J * S5_TILES_PER_STEP) == 0
    assert seq % (GDN_STEP * GDN_PREP_BLOCKS) == 0 and seq % (GDN_STEP * GDN_REC_BLOCKS) == 0
    x2 = x.reshape(bsz * seq, d)
    mod = _adaln_mod(c, ada_w, ada_b)
    for layer in range(depth):
        j = layer // 2
        mod3 = mod[layer].reshape(bsz, 1, 3 * d)
        if layer % 2 == 0:
            s5_params = (s5_a_re[j], s5_a_im[j], s5_log_dt[j], s5_b_re[j], s5_b_im[j], s5_c_re[j], s5_c_im[j],
                         s5_d[j], s5_glu_w[j], s5_glu_b[j])
            x2 = _ab_layer(x2, mod3, bsz, seq, pre_g[layer], post_g[layer], rel_bias, ab_w_in[j], ab_w_out[j],
                           s5_params)
        else:
            x2 = _gdn_layer(x2, mod3, bsz, seq, pre_g[layer], post_g[layer], gdn_w_in[j], gdn_conv[j],
                            gdn_a_log[j], gdn_dt_bias[j], gdn_norm_g[j], gdn_w_out[j])
    return x2.reshape(bsz, seq, d)
```

```python
import functools
import math

import numpy as np
import jax
import jax.numpy as jnp
from jax import lax
from jax.experimental import pallas as pl
from jax.experimental.pallas import tpu as pltpu

F32 = jnp.float32
BF16 = jnp.bfloat16
HI = lax.Precision.HIGHEST

EPS = 1e-6
A_HEADS = 8
A_HEAD_DIM = 64
A_WIDTH = A_HEADS * A_HEAD_DIM
A_BLOCK = 128
DILATED_CONFIGS = ((128, 1), (512, 4), (2048, 16))
B_GROUP = 16
B_STATE = 64
REL_BUCKETS = 32
REL_MAX_DIST = 2048
C_HEADS = 8
C_DK = 128
C_DV = 128
C_CONV = 4
MASK_NEG = -1e30
LOG2_E = math.log2(math.e)

LANES = 128
SUBLANES = 8
VMEM_LIMIT = 48 * 1024 * 1024
ROW_TILE = 512
OUT_C_TILE = 1024
COL_CHUNK = 512
SPAN_RES = max(dl for _, dl in DILATED_CONFIGS)
SPAN = A_BLOCK * SPAN_RES
SPAN_RUN = ROW_TILE // SPAN_RES
ATTN_BLOCKS_PER_STEP = 8
S5_J = 32
S5_LW = 512
S5_TILES_PER_STEP = 4
GDN_STEP = 128
GDN_CHUNK = 64
GDN_PREP_BLOCKS = 4
GDN_REC_BLOCKS = 2

_NT = (((1,), (1,)), ((), ()))


def _cparams(sem):
    return pltpu.CompilerParams(dimension_semantics=sem, vmem_limit_bytes=VMEM_LIMIT)


def _sigmoid(x):
    return 1.0 / (1.0 + jnp.exp(-x))


def _mod_kernel(c_ref, w_ref, b_ref, o_ref):
    c = c_ref[...]
    ca = c * _sigmoid(c)
    o_ref[0] = jnp.dot(ca.astype(BF16), w_ref[0].astype(BF16), preferred_element_type=F32) + b_ref[0]


def _adaln_mod(c, ada_w, ada_b):
    depth, d, d3 = ada_w.shape
    bsz = c.shape[0]
    return pl.pallas_call(
        _mod_kernel,
        out_shape=jax.ShapeDtypeStruct((depth, bsz, d3), F32),
        grid=(depth, d3 // d),
        in_specs=[pl.BlockSpec((bsz, d), lambda l, j: (0, 0)),
                  pl.BlockSpec((1, d, d), lambda l, j: (l, 0, j)),
                  pl.BlockSpec((1, 1, d), lambda l, j: (l, 0, j))],
        out_specs=pl.BlockSpec((1, bsz, d), lambda l, j: (l, 0, j)),
        compiler_params=_cparams(("arbitrary", "arbitrary")),
        name="adaln_mod",
    )(c, ada_w, ada_b.reshape(depth, 1, d3))


def _span_perm():
    rho = np.arange(ROW_TILE)
    nat = SPAN_RES * (rho % SPAN_RUN) + rho // SPAN_RUN
    return (nat[:, None] == np.arange(ROW_TILE)[None, :]).astype(np.float32)


def _modulated_norm(x_ref, mod_ref, g_ref, d_model):
    x = x_ref[...]
    ms = jnp.mean(x * x, axis=-1, keepdims=True)
    y = x * lax.rsqrt(ms + EPS) * g_ref[...]
    shift = mod_ref[0, :, 0:d_model]
    scale = mod_ref[0, :, d_model:2 * d_model]
    return (y * (1.0 + scale) + shift).astype(BF16)


def _sublane_major_perm(tile):
    rho = np.arange(tile)
    src = (rho % SUBLANES) * (tile // SUBLANES) + rho // SUBLANES
    return (src[:, None] == np.arange(tile)[None, :]).astype(np.float32)


def _in_proj_kernel(x_ref, mod_ref, g_ref, w_ref, *rest, splits, d_model, n_span):
    pm_ref = rest[0] if n_span else None
    out_refs = rest[1:] if n_span else rest
    span_refs = out_refs[len(splits):]
    h = _modulated_norm(x_ref, mod_ref, g_ref, d_model)
    for idx, ((c0, width, mult), o_ref) in enumerate(zip(splits, out_refs)):
        for cc in range(0, width, COL_CHUNK):
            cw = min(COL_CHUNK, width - cc)
            acc = jnp.dot(h, w_ref[:, c0 + cc:c0 + cc + cw], preferred_element_type=F32)
            if mult != 1.0:
                acc = acc * mult
            val = acc.astype(o_ref.dtype)
            o_ref[:, cc:cc + cw] = val
            if idx < n_span:
                moved = jnp.dot(pm_ref[...], val, preferred_element_type=F32).astype(BF16)
                span_refs[idx][:, :, cc:cc + cw] = moved.reshape(SPAN_RES, SPAN_RUN, cw)


def _in_proj(x2, mod3, gain, w_bf16, splits, out_dtypes, seq, n_span=0):
    t, d = x2.shape
    tiles_per_seq = seq // ROW_TILE
    tiles_per_span = SPAN // ROW_TILE
    n_w = w_bf16.shape[1]
    row = lambda wd: pl.BlockSpec((ROW_TILE, wd), lambda i: (i, 0))
    span_shape = lambda wd: jax.ShapeDtypeStruct((t // SPAN, SPAN_RES, A_BLOCK, wd), BF16)
    span_spec = lambda wd: pl.BlockSpec((None, SPAN_RES, SPAN_RUN, wd),
                                        lambda i: (i // tiles_per_span, 0, i % tiles_per_span, 0))
    perm_in = [jnp.asarray(_span_perm(), BF16)] if n_span else []
    perm_spec = [pl.BlockSpec((ROW_TILE, ROW_TILE), lambda i: (0, 0))] if n_span else []
    return pl.pallas_call(
        functools.partial(_in_proj_kernel, splits=splits, d_model=d, n_span=n_span),
        out_shape=([jax.ShapeDtypeStruct((t, wd), dt) for (_, wd, _), dt in zip(splits, out_dtypes)]
                   + [span_shape(wd) for (_, wd, _) in splits[:n_span]]),
        grid=(t // ROW_TILE,),
        in_specs=[row(d),
                  pl.BlockSpec((1, 1, 3 * d), lambda i: (i // tiles_per_seq, 0, 0)),
                  pl.BlockSpec((1, d), lambda i: (0, 0)),
                  pl.BlockSpec((d, n_w), lambda i: (0, 0))] + perm_spec,
        out_specs=[row(wd) for (_, wd, _) in splits] + [span_spec(wd) for (_, wd, _) in splits[:n_span]],
        compiler_params=_cparams(("arbitrary",)),
        name="in_proj",
    )(x2, mod3, gain.reshape(1, d), w_bf16, *perm_in)


def _t5_bucket(dist):
    dist = np.maximum(dist, 0)
    max_exact = REL_BUCKETS // 2
    large = max_exact + (np.log(np.maximum(dist, 1) / max_exact)
                         / math.log(REL_MAX_DIST / max_exact) * (REL_BUCKETS - max_exact)).astype(np.int32)
    large = np.minimum(large, REL_BUCKETS - 1)
    return np.where(dist < max_exact, dist, large).astype(np.int32)


def _bias_kernel(rb_ref, bucket_ref, mask_ref, o_ref):
    def body(r, carry):
        rows = pl.ds(pl.multiple_of(r * SUBLANES, SUBLANES), SUBLANES)
        bk = bucket_ref[0, rows, :]
        accs = [jnp.zeros(bk.shape, F32) for _ in range(A_HEADS)]
        for b in range(REL_BUCKETS):
            eq = bk == b
            accs = [jnp.where(eq, rb_ref[b, h], acc) for h, acc in enumerate(accs)]
        for f in range(2):
            keep = mask_ref[0, f, rows, :] != 0
            for h in range(A_HEADS):
                o_ref[0, f, h, rows, :] = jnp.where(keep, accs[h] * LOG2_E, MASK_NEG)
        return carry
    lax.fori_loop(0, A_BLOCK // SUBLANES, body, 0)


def _attn_bias_tables(rel_bias):
    qi = np.arange(A_BLOCK)[:, None]
    kj = np.arange(2 * A_BLOCK)[None, :]
    rel = qi + A_BLOCK - kj
    buckets, masks = [], []
    for window, dil in DILATED_CONFIGS:
        band = (rel >= 0) & (rel <= window // dil)
        bucket = _t5_bucket(rel * dil)
        mask = np.stack([band & (kj >= A_BLOCK), band]).astype(np.int32)
        if dil > 1:
            runs = SPAN_RES // dil
            run = A_BLOCK // runs
            rho = np.arange(A_BLOCK)
            sub = runs * (rho % run) + rho // run
            keys = np.concatenate([sub, A_BLOCK + sub])
            bucket = bucket[sub][:, keys]
            mask = mask[:, sub][:, :, keys]
        buckets.append(bucket)
        masks.append(mask)
    n_cfg = len(DILATED_CONFIGS)
    return pl.pallas_call(
        _bias_kernel,
        out_shape=jax.ShapeDtypeStruct((n_cfg, 2, A_HEADS, A_BLOCK, 2 * A_BLOCK), F32),
        grid=(n_cfg,),
        in_specs=[pl.BlockSpec(memory_space=pltpu.SMEM),
                  pl.BlockSpec((1, A_BLOCK, 2 * A_BLOCK), lambda c: (c, 0, 0)),
                  pl.BlockSpec((1, 2, A_BLOCK, 2 * A_BLOCK), lambda c: (c, 0, 0, 0))],
        out_specs=pl.BlockSpec((1, 2, A_HEADS, A_BLOCK, 2 * A_BLOCK), lambda c: (c, 0, 0, 0, 0)),
        compiler_params=_cparams(("arbitrary",)),
        name="attn_bias",
    )(rel_bias.astype(F32), jnp.asarray(np.stack(buckets)), jnp.asarray(np.stack(masks)))


def _attn_block(q, kcat, vcat, bias_ref):
    lane = lax.broadcasted_iota(jnp.int32, (A_BLOCK, LANES), 1)
    low = lane < A_HEAD_DIM
    lse_tile = jnp.zeros((A_BLOCK, LANES), F32)
    zero = jnp.zeros((A_BLOCK, LANES), BF16)
    heads = range(A_HEADS)
    pair = [slice((h // 2) * LANES, (h // 2 + 1) * LANES) for h in heads]
    scores = []
    for h in heads:
        qm = jnp.where(low if h % 2 == 0 else jnp.logical_not(low), q[:, pair[h]], zero)
        scores.append(lax.dot_general(qm, kcat[:, pair[h]], _NT, preferred_element_type=F32) + bias_ref[0, h])
    probs, inv_l = [], []
    for h in heads:
        m = jnp.max(scores[h], axis=-1, keepdims=True)
        p = jnp.exp2(scores[h] - m)
        l = jnp.sum(p, axis=-1, keepdims=True)
        probs.append(p.astype(BF16))
        inv_l.append(1.0 / l)
        lse_tile = jnp.where(lane == h, m + jnp.log2(l), lse_tile)
    outs = [jnp.dot(probs[h], vcat[:, pair[h]], preferred_element_type=F32) * inv_l[h] for h in heads]
    o = jnp.concatenate([jnp.where(low, outs[2 * hp], outs[2 * hp + 1]) for hp in range(A_HEADS // 2)], axis=1)
    return o, lse_tile


def _attn_kernel(q_ref, kp_ref, kc_ref, vp_ref, vc_ref, bias0_ref, bias_ref, o_ref, lse_ref, *, per_lead):
    def sub(ref, rr, i):
        if ref.ndim == 2:
            return ref[i * A_BLOCK:(i + 1) * A_BLOCK, :]
        run = ref.shape[3] // per_lead
        return ref[i // per_lead, :, rr, (i % per_lead) * run:(i % per_lead + 1) * run, :]

    def put(ref, rr, i, val):
        if ref.ndim == 2:
            ref[i * A_BLOCK:(i + 1) * A_BLOCK, :] = val.astype(ref.dtype)
        else:
            run = ref.shape[3] // per_lead
            ref[i // per_lead, :, rr, (i % per_lead) * run:(i % per_lead + 1) * run, :] = (
                val.astype(ref.dtype).reshape(ref.shape[1], run, ref.shape[4]))

    flat = lambda v: v.reshape(A_BLOCK, v.shape[-1])
    token_order = q_ref.ndim == 2
    n_blocks = q_ref.shape[0] // A_BLOCK if token_order else q_ref.shape[0] * per_lead
    for rr in range(1 if token_order else q_ref.shape[2]):
        k_prev = flat(kp_ref[...] if token_order else kp_ref[0, :, rr])
        v_prev = flat(vp_ref[...] if token_order else vp_ref[0, :, rr])
        for i in range(n_blocks):
            k_cur, v_cur = flat(sub(kc_ref, rr, i)), flat(sub(vc_ref, rr, i))
            o, lse_tile = _attn_block(flat(sub(q_ref, rr, i)), jnp.concatenate([k_prev, k_cur], axis=0),
                                      jnp.concatenate([v_prev, v_cur], axis=0),
                                      bias0_ref if i == 0 else bias_ref)
            put(o_ref, rr, i, o)
            put(lse_ref, rr, i, lse_tile)
            k_prev, v_prev = k_cur, v_cur


def _dilated_attention_one(q, k, v, bias_tab, bsz, seq, dil):
    nb = seq // dil // A_BLOCK
    step = min(ATTN_BLOCKS_PER_STEP, nb)
    w = A_WIDTH
    if dil == 1:
        per_lead = n_res = 1
        view = lambda a: a.reshape(bsz, seq, a.shape[-1])
        block = lambda wd: (None, step * A_BLOCK, wd)
        block_prev = lambda wd: (None, A_BLOCK, wd)
        cur = lambda b, r, m: (b, m, 0)
        prev = lambda b, r, m: (b, jnp.maximum(m * step - 1, 0), 0)
        out_shape = lambda wd, dt: jax.ShapeDtypeStruct((bsz, seq, wd), dt)
    else:
        runs = SPAN_RES // dil
        run = A_BLOCK // runs
        spans_per_seq = seq // SPAN
        per_lead = min(step, runs)
        lead = step // per_lead
        n_res = max(1, ATTN_BLOCKS_PER_STEP // step)
        assert runs % per_lead == 0 and spans_per_seq % lead == 0 and dil % n_res == 0
        view = lambda a: a.reshape(a.shape[0], runs, dil, A_BLOCK, a.shape[-1])
        block = lambda wd: (lead, runs, n_res, per_lead * run, wd)
        block_prev = lambda wd: (1, runs, n_res, run, wd)
        cur = lambda b, r, m: ((b * spans_per_seq + (m * step) // runs) // lead, 0, r, ((m * step) % runs) // per_lead, 0)

        def prev(b, r, m):
            n = jnp.maximum(m * step - 1, 0)
            return (b * spans_per_seq + n // runs, 0, r, n % runs, 0)

        out_shape = lambda wd, dt: jax.ShapeDtypeStruct((bsz * spans_per_seq, runs, dil, A_BLOCK, wd), dt)
    blk = pl.BlockSpec(block(w), cur)
    blk_prev = pl.BlockSpec(block_prev(w), prev)
    bias_block = (1, A_HEADS, A_BLOCK, 2 * A_BLOCK)
    o, lse = pl.pallas_call(
        functools.partial(_attn_kernel, per_lead=per_lead),
        out_shape=[out_shape(w, BF16), out_shape(LANES, F32)],
        grid=(bsz, dil // n_res, nb // step),
        in_specs=[blk, blk_prev, blk, blk_prev, blk,
                  pl.BlockSpec(bias_block, lambda b, r, m: (jnp.minimum(m, 1), 0, 0, 0)),
                  pl.BlockSpec(bias_block, lambda b, r, m: (1, 0, 0, 0))],
        out_specs=[pl.BlockSpec(block(w), cur), pl.BlockSpec(block(LANES), cur)],
        compiler_params=_cparams(("arbitrary", "arbitrary", "arbitrary")),
        name=f"dilated_attn_d{dil}",
    )(view(q), view(k), view(k), view(v), view(v), bias_tab, bias_tab)
    if dil == 1:
        return o.reshape(bsz * seq, w), lse.reshape(bsz * seq, LANES)
    return o.reshape(o.shape[0], SPAN_RES, A_BLOCK, w), lse.reshape(o.shape[0], SPAN_RES, A_BLOCK, LANES)


def _gelu_tanh(x):
    c = math.sqrt(2.0 / math.pi)
    return x * (0.5 * (1.0 + jnp.tanh(c * (x + 0.044715 * (x * x * x)))))


def _s5_kernel(u_ref, pm_ref, pmt_ref, bd_ref, pw_ref, pw16_ref, cd_ref, dsk_ref, gw_ref, gb_ref, o_ref,
               x_scr, st_scr, *, n_state):
    @pl.when(pl.program_id(1) == 0)
    def _():
        st_scr[...] = jnp.zeros_like(st_scr)

    for tile_idx in range(S5_TILES_PER_STEP):
        _s5_tile(tile_idx, u_ref, pm_ref, pmt_ref, bd_ref, pw_ref, pw16_ref, cd_ref, dsk_ref, gw_ref, gb_ref,
                 o_ref, x_scr, st_scr, n_state)


def _s5_tile(tile_idx, u_ref, pm_ref, pmt_ref, bd_ref, pw_ref, pw16_ref, cd_ref, dsk_ref, gw_ref, gb_ref,
             o_ref, x_scr, st_scr, n_state):
    n = n_state
    jn = S5_J
    tile_rows = slice(tile_idx * SUBLANES * jn, (tile_idx + 1) * SUBLANES * jn)

    u = u_ref[tile_rows, :]
    u_perm = jnp.dot(pm_ref[...], u.astype(BF16), preferred_element_type=F32).astype(BF16)
    cw = u.shape[1] * S5_LW // n
    chunks = [(slice(c0, c0 + S5_LW), slice(n + c0, n + c0 + S5_LW)) for c0 in range(0, n, S5_LW)]

    def project_in(q):
        bu = jnp.dot(u_perm[:, q * cw:(q + 1) * cw], bd_ref[q], preferred_element_type=F32)
        x_scr[:, chunks[q][0]] = bu[:, 0:S5_LW]
        x_scr[:, chunks[q][1]] = bu[:, S5_LW:]

    project_in(0)
    end_r, end_i = [], []
    for q, (re, im) in enumerate(chunks):
        if q + 1 < len(chunks):
            project_in(q + 1)
        a1r, a1i = pw_ref[0:SUBLANES, re], pw_ref[0:SUBLANES, im]
        xr = jnp.zeros((SUBLANES, S5_LW), F32)
        xi = jnp.zeros((SUBLANES, S5_LW), F32)
        for j in range(jn):
            rows = slice(SUBLANES * j, SUBLANES * (j + 1))
            nr = a1r * xr - a1i * xi + x_scr[rows, re]
            ni = a1r * xi + a1i * xr + x_scr[rows, im]
            xr, xi = nr, ni
            x_scr[rows, re] = xr
            x_scr[rows, im] = xi
        end_r.append(xr)
        end_i.append(xi)
    er, ei = jnp.concatenate(end_r, axis=1), jnp.concatenate(end_i, axis=1)
    last = SUBLANES * (jn - 1)
    ajr, aji = pw_ref[last:last + 1, 0:n], pw_ref[last:last + 1, n:]
    pr, pi = st_scr[0:1, 0:n], st_scr[0:1, n:]
    cin_r, cin_i = [], []
    for s in range(SUBLANES):
        cin_r.append(pr)
        cin_i.append(pi)
        nr = er[s:s + 1] + ajr * pr - aji * pi
        ni = ei[s:s + 1] + ajr * pi + aji * pr
        pr, pi = nr, ni
    st_scr[0:1, 0:n] = pr
    st_scr[0:1, n:] = pi
    cr_all = jnp.concatenate(cin_r, axis=0)
    ci_all = jnp.concatenate(cin_i, axis=0)
    pack = 2 * SUBLANES
    ys = []
    for q, (re, im) in enumerate(chunks):
        cr = jnp.concatenate([cr_all[:, re]] * 2, axis=0).astype(BF16)
        ci = jnp.concatenate([ci_all[:, re]] * 2, axis=0).astype(BF16)
        xr_parts, xi_parts = [], []
        for m in range(SUBLANES * jn // pack):
            rows = slice(pack * m, pack * (m + 1))
            pjr, pji = pw16_ref[rows, re], pw16_ref[rows, im]
            xr_parts.append(x_scr[rows, re].astype(BF16) + (pjr * cr - pji * ci))
            xi_parts.append(x_scr[rows, im].astype(BF16) + (pjr * ci + pji * cr))
        xcat = jnp.concatenate([jnp.concatenate(xr_parts, axis=0), jnp.concatenate(xi_parts, axis=0)], axis=1)
        ys.append(jnp.dot(xcat, cd_ref[q], preferred_element_type=F32))

    y_perm = jnp.concatenate(ys, axis=1)
    y_hi = y_perm.astype(BF16)
    y_lo = (y_perm - y_hi.astype(F32)).astype(BF16)
    y = (jnp.dot(pmt_ref[...], y_hi, preferred_element_type=F32)
         + jnp.dot(pmt_ref[...], y_lo, preferred_element_type=F32))
    y = _gelu_tanh(y + dsk_ref[...] * u)
    z = jnp.dot(y.astype(BF16), gw_ref[...], preferred_element_type=F32) + gb_ref[...]
    o_ref[tile_rows, :] = (y * _sigmoid(z)).astype(o_ref.dtype)


def _s5_tables(a_re, a_im, log_dt, b_re, b_im, c_re, c_im):
    g, p = a_re.shape
    dt = jnp.exp(log_dt.astype(F32))[:, None]
    ar, ai = a_re.astype(F32), a_im.astype(F32)
    mag = jnp.exp(dt * ar)
    abar_r, abar_i = mag * jnp.cos(dt * ai), mag * jnp.sin(dt * ai)
    den = ar * ar + ai * ai
    fr = ((abar_r - 1.0) * ar + abar_i * ai) / den
    fi = (abar_i * ar - (abar_r - 1.0) * ai) / den
    br, bi = b_re.astype(F32), b_im.astype(F32)
    bbar_r = fr[..., None] * br - fi[..., None] * bi
    bbar_i = fr[..., None] * bi + fi[..., None] * br
    m = br.shape[-1]
    gc = S5_LW // p
    nq = g // gc
    eye = jnp.eye(gc, dtype=F32)
    dense_b = lambda t: jnp.einsum('qgpm,gh->qgmhp', t.reshape(nq, gc, p, m), eye).reshape(nq, gc * m, gc * p)
    bd = jnp.concatenate([dense_b(bbar_r), dense_b(bbar_i)], axis=2)
    dense_c = lambda t: jnp.einsum('qgmp,gh->qgphm', t.reshape(nq, gc, m, p), eye).reshape(nq, gc * p, gc * m)
    cd = jnp.concatenate([dense_c(c_re.astype(F32)), -dense_c(c_im.astype(F32))], axis=1)
    kk = jnp.arange(1, S5_J + 1, dtype=F32)[:, None, None]
    pmag = jnp.exp(kk * (dt * ar)[None])
    pw_r = (pmag * jnp.cos(kk * (dt * ai)[None])).reshape(S5_J, g * p)
    pw_i = (pmag * jnp.sin(kk * (dt * ai)[None])).reshape(S5_J, g * p)
    pw = jnp.repeat(jnp.concatenate([pw_r, pw_i], axis=1), SUBLANES, axis=0)
    return bd.astype(BF16), pw, cd.astype(BF16)


def _s5_layer(u, bsz, seq, a_re, a_im, log_dt, b_re, b_im, c_re, c_im, d_skip, glu_w, glu_b):
    t, width = u.shape
    n = a_re.shape[0] * a_re.shape[1]
    bd, pw, cd = _s5_tables(a_re, a_im, log_dt, b_re, b_im, c_re, c_im)
    tile = SUBLANES * S5_J
    step_rows = S5_TILES_PER_STEP * tile
    steps_per_seq = seq // step_rows
    const = lambda b, i: (0, 0)
    perm = _sublane_major_perm(tile)
    return pl.pallas_call(
        functools.partial(_s5_kernel, n_state=n),
        out_shape=jax.ShapeDtypeStruct((t, width), BF16),
        grid=(bsz, steps_per_seq),
        in_specs=[pl.BlockSpec((step_rows, width), lambda b, i: (b * steps_per_seq + i, 0)),
                  pl.BlockSpec((tile, tile), const),
                  pl.BlockSpec((tile, tile), const),
                  pl.BlockSpec(bd.shape, lambda b, i: (0, 0, 0)),
                  pl.BlockSpec((tile, 2 * n), const),
                  pl.BlockSpec((tile, 2 * n), const),
                  pl.BlockSpec(cd.shape, lambda b, i: (0, 0, 0)),
                  pl.BlockSpec((1, width), const),
                  pl.BlockSpec((width, width), const),
                  pl.BlockSpec((1, width), const)],
        out_specs=pl.BlockSpec((step_rows, width), lambda b, i: (b * steps_per_seq + i, 0)),
        scratch_shapes=[pltpu.VMEM((tile, 2 * n), F32), pltpu.VMEM((SUBLANES, 2 * n), F32)],
        compiler_params=_cparams(("arbitrary", "arbitrary")),
        name="s5_layer",
    )(u, jnp.asarray(perm, BF16), jnp.asarray(perm.T, BF16), bd, pw, pw.astype(BF16), cd,
      d_skip.reshape(1, width).astype(F32), glu_w.astype(BF16),
      glu_b.reshape(1, width).astype(F32))


def _finish(y, x_ref, mod_ref, pg_ref, out_ref, d_model):
    ms = jnp.mean(y * y, axis=-1, keepdims=True)
    yn = y * lax.rsqrt(ms + EPS) * pg_ref[...]
    gate_mod = mod_ref[0, :, 2 * d_model:3 * d_model]
    out_ref[...] = x_ref[...] + gate_mod * yn


def _split2(x):
    hi = x.astype(BF16)
    return hi, (x - hi.astype(F32)).astype(BF16)


def _out_ab_kernel(o0_ref, l0_ref, o1_ref, o2_ref, l1_ref, l2_ref, ob_ref, gate_ref, e_ref, un_ref, w_ref,
                   x_ref, mod_ref, pg_ref, out_ref, *, d_model):
    flat = lambda ref: ref[...].reshape(ROW_TILE, ref.shape[-1])
    aw = o0_ref.shape[-1]

    def unperm(o_ref, l_ref):
        parts = jnp.concatenate([flat(o_ref)] + list(_split2(flat(l_ref))), axis=1)
        moved = jnp.dot(un_ref[...], parts, preferred_element_type=F32)
        return moved[:, 0:aw], moved[:, aw:aw + LANES] + moved[:, aw + LANES:]

    o0, l0 = o0_ref[...].astype(F32), l0_ref[...]
    o1, l1 = unperm(o1_ref, l1_ref)
    o2, l2 = unperm(o2_ref, l2_ref)
    mx = jnp.maximum(jnp.maximum(l0, l1), l2)
    e0, e1, e2 = jnp.exp2(l0 - mx), jnp.exp2(l1 - mx), jnp.exp2(l2 - mx)
    inv_den = 1.0 / (e0 + e1 + e2)
    stacked = jnp.concatenate(list(_split2(e1 * inv_den)) + list(_split2(e2 * inv_den)), axis=0)
    wide = jnp.dot(stacked, e_ref[...], preferred_element_type=F32)
    term = lambda i: wide[i * ROW_TILE:(i + 1) * ROW_TILE]
    w1 = term(0) + term(1)
    w2 = term(2) + term(3)
    o_a = (1.0 - w1 - w2) * o0 + w1 * o1 + w2 * o2
    gate = gate_ref[...].astype(F32)
    sg = gate * _sigmoid(gate)
    aw = o_a.shape[-1]
    y = jnp.dot((o_a * sg[:, :aw]).astype(BF16), w_ref[0:aw, :], preferred_element_type=F32)
    y = y + jnp.dot((ob_ref[...].astype(F32) * sg[:, aw:]).astype(BF16), w_ref[aw:, :],
                    preferred_element_type=F32)
    _finish(y, x_ref, mod_ref, pg_ref, out_ref, d_model)


def _out_c_kernel(o_ref, gate_ref, w_ref, x_ref, mod_ref, pg_ref, out_ref, *, d_model):
    gate = gate_ref[...].astype(F32)
    o = o_ref[...].astype(F32) * (gate * _sigmoid(gate))
    y = jnp.dot(o.astype(BF16), w_ref[...], preferred_element_type=F32)
    _finish(y, x_ref, mod_ref, pg_ref, out_ref, d_model)


def _out_proj(kern, row_inputs, const_inputs, w_bf16, x2, mod3, post_g, seq, name, tile=ROW_TILE):
    t, d = x2.shape
    tiles_per_seq = seq // tile
    tiles_per_span = SPAN // tile

    def row_spec(a):
        if a.ndim == 2:
            return pl.BlockSpec((tile, a.shape[1]), lambda i: (i, 0))
        assert tile == ROW_TILE
        return pl.BlockSpec((None, SPAN_RES, SPAN_RUN, a.shape[-1]),
                            lambda i: (i // tiles_per_span, 0, i % tiles_per_span, 0))

    const_spec = lambda a: pl.BlockSpec(a.shape, lambda i: (0, 0))
    return pl.pallas_call(
        functools.partial(kern, d_model=d),
        out_shape=jax.ShapeDtypeStruct((t, d), F32),
        grid=(t // tile,),
        in_specs=([row_spec(a) for a in row_inputs] + [const_spec(a) for a in const_inputs]
                  + [const_spec(w_bf16), row_spec(x2),
                     pl.BlockSpec((1, 1, 3 * d), lambda i: (i // tiles_per_seq, 0, 0)),
                     pl.BlockSpec((1, d), lambda i: (0, 0))]),
        out_specs=pl.BlockSpec((tile, d), lambda i: (i, 0)),
        compiler_params=_cparams(("arbitrary",)),
        name=name,
    )(*row_inputs, *const_inputs, w_bf16, x2, mod3, post_g.reshape(1, d))


def _gdn_prep_kernel(x_ref, halo_ref, br_ref, cw_ref, alog_ref, dtb_ref,
                     w_ref, u_ref, qg_ref, kdt_ref, aqk_ref, dec_ref, xs_scr):
    n_all = GDN_PREP_BLOCKS * GDN_STEP
    first = pl.program_id(1) == 0
    halo = halo_ref[...]
    xs_scr[0:SUBLANES, :] = jnp.where(first, jnp.zeros_like(halo), halo)
    xs_scr[SUBLANES:SUBLANES + n_all, :] = x_ref[...]
    cw = cw_ref[...]
    conv = xs_scr[SUBLANES:SUBLANES + n_all, :] * cw[C_CONV - 1:C_CONV]
    for j in range(C_CONV - 1):
        off = SUBLANES - (C_CONV - 1) + j
        conv = conv + xs_scr[off:off + n_all, :] * cw[j:j + 1]
    act = conv * _sigmoid(conv)
    _gdn_prep_blocks(act, br_ref, alog_ref, dtb_ref, w_ref, u_ref, qg_ref, kdt_ref, aqk_ref, dec_ref)


def _gdn_prep_blocks(act, br_ref, alog_ref, dtb_ref, w_ref, u_ref, qg_ref, kdt_ref, aqk_ref, dec_ref):
    n = GDN_STEP
    c = GDN_CHUNK
    row = lax.broadcasted_iota(jnp.int32, (n, n), 0)
    col = lax.broadcasted_iota(jnp.int32, (n, n), 1)
    cum = jnp.logical_and((row // c) == (col // c), row >= col).astype(F32)
    prow = lax.broadcasted_iota(jnp.int32, (c, n), 0)
    plane = lax.broadcasted_iota(jnp.int32, (c, n), 1)
    left = plane < c
    pcol = jnp.where(left, plane, plane - c)
    tril = prow >= pcol
    strict = prow > pcol
    eye = (prow == pcol).astype(F32)
    pair = lambda full: jnp.where(left, full[0:c], full[c:])

    def blockdiag(p):
        zero = jnp.zeros_like(p)
        return jnp.concatenate([jnp.where(left, p, zero), jnp.where(left, zero, p)], axis=0)

    first_chunk = lax.broadcasted_iota(jnp.int32, (n, 1), 0) < GDN_CHUNK
    dec_row = lax.broadcasted_iota(jnp.int32, (SUBLANES, LANES), 0)
    heads = range(C_HEADS)
    lanes = [slice(h * LANES, (h + 1) * LANES) for h in heads]

    def gates(blk):
        rows = slice(blk * n, (blk + 1) * n)
        br = br_ref[rows, :]
        xg = br + dtb_ref[...]
        softplus = jnp.maximum(xg, 0.0) + jnp.log(1.0 + jnp.exp(-jnp.abs(xg)))
        g_all = -jnp.exp(alog_ref[...]) * softplus
        gc_all = jnp.dot(cum, g_all, preferred_element_type=F32, precision=HI)
        return dict(rows=rows, beta=_sigmoid(br), gc=gc_all, gc_t=gc_all.T)

    def first_stage(ctx, blk, h):
        rows = ctx["rows"]
        gcol = ctx["gc"][:, C_HEADS + h:C_HEADS + h + 1]
        grow = ctx["gc_t"][C_HEADS + h:C_HEADS + h + 1, :]
        eg = jnp.exp(gcol)
        gcol_p = jnp.where(left, gcol[0:c], gcol[c:])
        decay = jnp.where(tril, jnp.exp(jnp.where(tril, gcol_p - grow, 0.0)), 0.0)
        q = act[rows, h * C_DK:(h + 1) * C_DK]
        k = act[rows, (C_HEADS + h) * C_DK:(C_HEADS + h + 1) * C_DK]
        v = act[rows, 2 * C_HEADS * C_DK + h * C_DV:2 * C_HEADS * C_DK + (h + 1) * C_DV]
        q = q * lax.rsqrt(jnp.sum(q * q, axis=-1, keepdims=True) + EPS) * (C_DK ** -0.5)
        k = k * lax.rsqrt(jnp.sum(k * k, axis=-1, keepdims=True) + EPS)
        beta = ctx["beta"][:, h:h + 1]
        kb = k * beta
        k16 = k.astype(BF16)
        a = jnp.where(strict, pair(lax.dot_general(kb.astype(BF16), k16, _NT, preferred_element_type=F32)) * decay,
                      0.0)
        aqk = pair(lax.dot_general(q.astype(BF16), k16, _NT, preferred_element_type=F32)) * decay
        aqk_ref[rows, lanes[h]] = blockdiag(aqk).astype(BF16)
        qg_ref[rows, lanes[h]] = (q * eg).astype(BF16)
        g_last = jnp.where(first_chunk, gcol[GDN_CHUNK - 1:GDN_CHUNK], gcol[n - 1:n])
        kdt_ref[rows, lanes[h]] = (k * jnp.exp(g_last - gcol)).T.astype(BF16)
        dec_ref[blk, :, lanes[h]] = jnp.where(dec_row < SUBLANES // 2, eg[GDN_CHUNK - 1:GDN_CHUNK], eg[n - 1:n])
        rhs = jnp.concatenate([v * beta, kb * eg], axis=1).astype(BF16)
        return dict(a=a, inv=eye - a, rhs=rhs)

    n_blocks = act.shape[0] // n
    first_ctx = gates(0)
    state = [first_stage(first_ctx, 0, h) for h in heads]
    for blk in range(n_blocks):
        nxt_ctx = gates(blk + 1) if blk + 1 < n_blocks else None
        nxt = []

        def side_work(hs, blk=blk, nxt=nxt, nxt_ctx=nxt_ctx):
            if nxt_ctx is not None:
                nxt.extend(first_stage(nxt_ctx, blk + 1, h) for h in hs)

        invs = [st["inv"] for st in state]
        pows = []
        for st in state:
            pows.append(jnp.dot(st["a"].astype(BF16), blockdiag(st["a"]).astype(BF16), preferred_element_type=F32))
        side_work(heads[0:2])
        for level in range(4):
            for h in heads:
                both = jnp.dot(jnp.concatenate([invs[h], pows[h]], axis=0).astype(BF16),
                               blockdiag(pows[h]).astype(BF16), preferred_element_type=F32)
                invs[h] = invs[h] + both[0:c]
                pows[h] = both[c:]
            side_work(heads[2 + level:3 + level])
        for h in heads:
            invs[h] = invs[h] + jnp.dot(invs[h].astype(BF16), blockdiag(pows[h]).astype(BF16),
                                        preferred_element_type=F32)
        side_work(heads[6:8])
        rows = slice(blk * n, (blk + 1) * n)
        for h in heads:
            uw = jnp.dot(blockdiag(invs[h]).astype(BF16), state[h]["rhs"], preferred_element_type=F32)
            u_ref[rows, lanes[h]] = uw[:, 0:C_DV].astype(u_ref.dtype)
            w_ref[rows, lanes[h]] = uw[:, C_DV:].astype(BF16)
        state = nxt


def _gdn_rec_kernel(w_ref, u_ref, qg_ref, kdt_ref, aqk_ref, dec_ref, ng_ref, o_ref, s_scr):
    @pl.when(pl.program_id(0) == 0)
    def _():
        s_scr[...] = jnp.zeros_like(s_scr)

    half = SUBLANES // 2
    zeros = jnp.zeros((GDN_CHUNK, C_DV), BF16)
    lanes = [slice(h * LANES, (h + 1) * LANES) for h in range(C_HEADS)]
    chains = [(b, h) for b in range(w_ref.shape[0]) for h in range(C_HEADS)]
    states = [s_scr[b * C_HEADS + h] for b, h in chains]
    for blk in range(GDN_REC_BLOCKS):
        blk_rows = slice(blk * GDN_STEP, (blk + 1) * GDN_STEP)
        for j in range(GDN_STEP // GDN_CHUNK):
            rows = slice(blk * GDN_STEP + j * GDN_CHUNK, blk * GDN_STEP + (j + 1) * GDN_CHUNK)
            wss = []
            for c, (b, h) in enumerate(chains):
                wq = jnp.concatenate([w_ref[b, rows, lanes[h]], qg_ref[b, rows, lanes[h]]], axis=0)
                wss.append(jnp.dot(wq, states[c].astype(BF16), preferred_element_type=F32))
            for c, (b, h) in enumerate(chains):
                v_new = (u_ref[b, rows, lanes[h]].astype(F32) - wss[c][0:GDN_CHUNK]).astype(BF16)
                v_pad = jnp.concatenate([v_new, zeros] if j == 0 else [zeros, v_new], axis=0)
                both = jnp.dot(jnp.concatenate([aqk_ref[b, rows, lanes[h]], kdt_ref[b, blk_rows, lanes[h]]], axis=0),
                               v_pad, preferred_element_type=F32)
                o = wss[c][GDN_CHUNK:] + both[0:GDN_CHUNK]
                dec = dec_ref[b, blk, j * half:j * half + 1, lanes[h]]
                states[c] = states[c] * dec + both[GDN_CHUNK:]
                ms = jnp.mean(o * o, axis=-1, keepdims=True)
                o_ref[b, rows, lanes[h]] = (o * lax.rsqrt(ms + EPS) * ng_ref[...]).astype(o_ref.dtype)
    for c, (b, h) in enumerate(chains):
        s_scr[b * C_HEADS + h] = states[c]


def _gdn_core(qkv_pre, br, bsz, seq, conv_w, a_log, dt_bias, norm_g):
    t, qkv_w = qkv_pre.shape
    hw = C_HEADS * C_DV
    steps_per_seq = seq // GDN_STEP
    prep_rows = GDN_PREP_BLOCKS * GDN_STEP
    prep_steps = steps_per_seq // GDN_PREP_BLOCKS
    halo_blocks = prep_rows // SUBLANES
    pad_row = lambda vec: jnp.zeros((1, LANES), F32).at[0, C_HEADS:2 * C_HEADS].set(vec.astype(F32))
    tok = lambda b, i: (b * prep_steps + i, 0)
    const = lambda b, i: (0, 0)
    tok_spec = pl.BlockSpec((prep_rows, hw), tok)
    dec_spec = pl.BlockSpec((GDN_PREP_BLOCKS, SUBLANES, hw), lambda b, i: (b * prep_steps + i, 0, 0))
    w, u, qg, kdt, aqk, dec = pl.pallas_call(
        _gdn_prep_kernel,
        out_shape=[jax.ShapeDtypeStruct((t, hw), BF16), jax.ShapeDtypeStruct((t, hw), BF16),
                   jax.ShapeDtypeStruct((t, hw), BF16), jax.ShapeDtypeStruct((t, hw), BF16),
                   jax.ShapeDtypeStruct((t, hw), BF16),
                   jax.ShapeDtypeStruct((t // GDN_STEP, SUBLANES, hw), F32)],
        grid=(bsz, prep_steps),
        in_specs=[pl.BlockSpec((prep_rows, qkv_w), tok),
                  pl.BlockSpec((SUBLANES, qkv_w),
                               lambda b, i: (jnp.maximum((b * prep_steps + i) * halo_blocks - 1, 0), 0)),
                  pl.BlockSpec((prep_rows, LANES), tok),
                  pl.BlockSpec((C_CONV, qkv_w), const),
                  pl.BlockSpec((1, LANES), const),
                  pl.BlockSpec((1, LANES), const)],
        out_specs=[tok_spec, tok_spec, tok_spec, tok_spec, tok_spec, dec_spec],
        scratch_shapes=[pltpu.VMEM((SUBLANES + prep_rows, qkv_w), F32)],
        compiler_params=_cparams(("arbitrary", "arbitrary")),
        name="gdn_prep",
    )(qkv_pre, qkv_pre, br, conv_w.astype(F32), pad_row(a_log), pad_row(dt_bias))
    rec_steps = steps_per_seq // GDN_REC_BLOCKS
    seq_view = lambda a: a.reshape(bsz, seq, hw)
    rec_tok = pl.BlockSpec((bsz, GDN_REC_BLOCKS * GDN_STEP, hw), lambda i: (0, i, 0))
    rec_dec = pl.BlockSpec((bsz, GDN_REC_BLOCKS, SUBLANES, hw), lambda i: (0, i, 0, 0))
    o = pl.pallas_call(
        _gdn_rec_kernel,
        out_shape=jax.ShapeDtypeStruct((bsz, seq, hw), BF16),
        grid=(rec_steps,),
        in_specs=[rec_tok, rec_tok, rec_tok, rec_tok, rec_tok, rec_dec,
                  pl.BlockSpec((1, C_DV), lambda i: (0, 0))],
        out_specs=rec_tok,
        scratch_shapes=[pltpu.VMEM((bsz * C_HEADS, C_DK, C_DV), F32)],
        compiler_params=_cparams(("arbitrary",)),
        name="gdn_recurrence",
    )(seq_view(w), seq_view(u), seq_view(qg), seq_view(kdt), seq_view(aqk),
      dec.reshape(bsz, steps_per_seq, SUBLANES, hw), norm_g.reshape(1, C_DV).astype(F32))
    return o.reshape(t, hw)


def _ab_layer(x2, mod3, bsz, seq, pre_g, post_g, rel_bias, w_in, w_out, s5_params):
    b_width = s5_params[7].shape[-1]
    assert DILATED_CONFIGS[0][1] == 1 and all(SPAN_RES % dl == 0 for _, dl in DILATED_CONFIGS) and seq % SPAN == 0
    head_expand = jnp.asarray(np.arange(LANES)[:, None] == (np.arange(A_WIDTH)[None, :] // A_HEAD_DIM), BF16)
    unperm = jnp.asarray(_span_perm().T, BF16)
    splits = ((0, A_WIDTH, A_HEAD_DIM ** -0.5 * LOG2_E), (A_WIDTH, A_WIDTH, 1.0), (2 * A_WIDTH, A_WIDTH, 1.0),
              (3 * A_WIDTH, b_width, 1.0), (3 * A_WIDTH + b_width, A_WIDTH + b_width, 1.0))
    q, k, v, u, gate, qs, ks, vs = _in_proj(x2, mod3, pre_g, w_in.astype(BF16), splits,
                                            (BF16, BF16, BF16, F32, BF16), seq, n_span=3)
    outs, lses = [], []
    bias_tabs = _attn_bias_tables(rel_bias)
    for cfg, (_, dil) in enumerate(DILATED_CONFIGS):
        qkv = (q, k, v) if dil == 1 else (qs, ks, vs)
        o_c, lse_c = _dilated_attention_one(*qkv, bias_tabs[cfg], bsz, seq, dil)
        outs.append(o_c)
        lses.append(lse_c)
    o_b = _s5_layer(u, bsz, seq, *s5_params)
    row_inputs = [outs[0], lses[0], outs[1], outs[2], lses[1], lses[2], o_b, gate]
    return _out_proj(_out_ab_kernel, row_inputs, [head_expand, unperm], w_out.astype(BF16),
                     x2, mod3, post_g, seq, "out_proj_ab")


def _gdn_layer(x2, mod3, bsz, seq, pre_g, post_g, w_in, conv_w, a_log, dt_bias, norm_g, w_out):
    d = x2.shape[1]
    qkv_w = 2 * C_HEADS * C_DK + C_HEADS * C_DV
    gate_w = C_HEADS * C_DV
    w_pad = jnp.concatenate(
        [w_in, jnp.zeros((d, LANES - (w_in.shape[1] - qkv_w - gate_w)), w_in.dtype)], axis=1).astype(BF16)
    splits = ((0, qkv_w, 1.0), (qkv_w, gate_w, 1.0), (qkv_w + gate_w, LANES, 1.0))
    qkv_pre, gate, br = _in_proj(x2, mod3, pre_g, w_pad, splits, (F32, BF16, F32), seq)
    o = _gdn_core(qkv_pre, br, bsz, seq, conv_w, a_log, dt_bias, norm_g)
    return _out_proj(_out_c_kernel, [o, gate], [], w_out.astype(BF16), x2, mod3, post_g, seq, "out_proj_c",
                     tile=OUT_C_TILE)


def kernel(x, c, ada_w, ada_b, pre_g, post_g, rel_bias, ab_w_in, ab_w_out, s5_a_re, s5_a_im, s5_log_dt, s5_b_re, s5_b_im, s5_c_re, s5_c_im, s5_d, s5_glu_w, s5_glu_b, gdn_w_in, gdn_conv, gdn_a_log, gdn_dt_bias, gdn_norm_g, gdn_w_out):
    bsz, seq, d = x.shape
    depth = ada_w.shape[0]
    assert seq % ROW_TILE == 0 and seq % OUT_C_TILE == 0 and seq % (SUBLANES * S5_J * S5_TILES_PER_STEP) == 0
    assert seq % (GDN_STEP * GDN_PREP_BLOCKS) == 0 and seq % (GDN_STEP * GDN_REC_BLOCKS) == 0
    x2 = x.reshape(bsz * seq, d)
    mod = _adaln_mod(c, ada_w, ada_b)
    for layer in range(depth):
        j = layer // 2
        mod3 = mod[layer].reshape(bsz, 1, 3 * d)
        if layer % 2 == 0:
            s5_params = (s5_a_re[j], s5_a_im[j], s5_log_dt[j], s5_b_re[j], s5_b_im[j], s5_c_re[j], s5_c_im[j],
                         s5_d[j], s5_glu_w[j], s5_glu_b[j])
            x2 = _ab_layer(x2, mod3, bsz, seq, pre_g[layer], post_g[layer], rel_bias, ab_w_in[j], ab_w_out[j],
                           s5_params)
        else:
            x2 = _gdn_layer(x2, mod3, bsz, seq, pre_g[layer], post_g[layer], gdn_w_in[j], gdn_conv[j],
                            gdn_a_log[j], gdn_dt_bias[j], gdn_norm_g[j], gdn_w_out[j])
    return x2.reshape(bsz, seq, d)
```

```python
import functools
import math

import numpy as np
import jax
import jax.numpy as jnp
from jax import lax
from jax.experimental import pallas as pl
from jax.experimental.pallas import tpu as pltpu

F32 = jnp.float32
BF16 = jnp.bfloat16
HI = lax.Precision.HIGHEST

EPS = 1e-6
A_HEADS = 8
A_HEAD_DIM = 64
A_WIDTH = A_HEADS * A_HEAD_DIM
A_BLOCK = 128
DILATED_CONFIGS = ((128, 1), (512, 4), (2048, 16))
B_GROUP = 16
B_STATE = 64
REL_BUCKETS = 32
REL_MAX_DIST = 2048
C_HEADS = 8
C_DK = 128
C_DV = 128
C_CONV = 4
MASK_NEG = -1e30
LOG2_E = math.log2(math.e)

LANES = 128
SUBLANES = 8
VMEM_LIMIT = 48 * 1024 * 1024
ROW_TILE = 512
OUT_C_TILE = 1024
COL_CHUNK = 512
SPAN_RES = max(dl for _, dl in DILATED_CONFIGS)
SPAN = A_BLOCK * SPAN_RES
SPAN_RUN = ROW_TILE // SPAN_RES
ATTN_BLOCKS_PER_STEP = 8
S5_J = 32
S5_LW = 512
S5_TILES_PER_STEP = 4
GDN_STEP = 128
GDN_CHUNK = 64
GDN_PREP_BLOCKS = 4
GDN_REC_BLOCKS = 2

_NT = (((1,), (1,)), ((), ()))


def _cparams(sem):
    return pltpu.CompilerParams(dimension_semantics=sem, vmem_limit_bytes=VMEM_LIMIT)


def _sigmoid(x):
    return 1.0 / (1.0 + jnp.exp(-x))


def _mod_kernel(c_ref, w_ref, b_ref, o_ref):
    c = c_ref[...]
    ca = c * _sigmoid(c)
    o_ref[0] = jnp.dot(ca.astype(BF16), w_ref[0].astype(BF16), preferred_element_type=F32) + b_ref[0]


def _adaln_mod(c, ada_w, ada_b):
    depth, d, d3 = ada_w.shape
    bsz = c.shape[0]
    return pl.pallas_call(
        _mod_kernel,
        out_shape=jax.ShapeDtypeStruct((depth, bsz, d3), F32),
        grid=(depth, d3 // d),
        in_specs=[pl.BlockSpec((bsz, d), lambda l, j: (0, 0)),
                  pl.BlockSpec((1, d, d), lambda l, j: (l, 0, j)),
                  pl.BlockSpec((1, 1, d), lambda l, j: (l, 0, j))],
        out_specs=pl.BlockSpec((1, bsz, d), lambda l, j: (l, 0, j)),
        compiler_params=_cparams(("arbitrary", "arbitrary")),
        name="adaln_mod",
    )(c, ada_w, ada_b.reshape(depth, 1, d3))


def _span_perm():
    rho = np.arange(ROW_TILE)
    nat = SPAN_RES * (rho % SPAN_RUN) + rho // SPAN_RUN
    return (nat[:, None] == np.arange(ROW_TILE)[None, :]).astype(np.float32)


def _modulated_norm(x_ref, mod_ref, g_ref, d_model):
    x = x_ref[...]
    ms = jnp.mean(x * x, axis=-1, keepdims=True)
    y = x * lax.rsqrt(ms + EPS) * g_ref[...]
    shift = mod_ref[0, :, 0:d_model]
    scale = mod_ref[0, :, d_model:2 * d_model]
    return (y * (1.0 + scale) + shift).astype(BF16)


def _sublane_major_perm(tile):
    rho = np.arange(tile)
    src = (rho % SUBLANES) * (tile // SUBLANES) + rho // SUBLANES
    return (src[:, None] == np.arange(tile)[None, :]).astype(np.float32)


def _in_proj_kernel(x_ref, mod_ref, g_ref, w_ref, *rest, splits, d_model, n_span):
    pm_ref = rest[0] if n_span else None
    out_refs = rest[1:] if n_span else rest
    span_refs = out_refs[len(splits):]
    h = _modulated_norm(x_ref, mod_ref, g_ref, d_model)
    for idx, ((c0, width, mult), o_ref) in enumerate(zip(splits, out_refs)):
        for cc in range(0, width, COL_CHUNK):
            cw = min(COL_CHUNK, width - cc)
            acc = jnp.dot(h, w_ref[:, c0 + cc:c0 + cc + cw], preferred_element_type=F32)
            if mult != 1.0:
                acc = acc * mult
            val = acc.astype(o_ref.dtype)
            o_ref[:, cc:cc + cw] = val
            if idx < n_span:
                moved = jnp.dot(pm_ref[...], val, preferred_element_type=F32).astype(BF16)
                span_refs[idx][:, :, cc:cc + cw] = moved.reshape(SPAN_RES, SPAN_RUN, cw)


def _in_proj(x2, mod3, gain, w_bf16, splits, out_dtypes, seq, n_span=0):
    t, d = x2.shape
    tiles_per_seq = seq // ROW_TILE
    tiles_per_span = SPAN // ROW_TILE
    n_w = w_bf16.shape[1]
    row = lambda wd: pl.BlockSpec((ROW_TILE, wd), lambda i: (i, 0))
    span_shape = lambda wd: jax.ShapeDtypeStruct((t // SPAN, SPAN_RES, A_BLOCK, wd), BF16)
    span_spec = lambda wd: pl.BlockSpec((None, SPAN_RES, SPAN_RUN, wd),
                                        lambda i: (i // tiles_per_span, 0, i % tiles_per_span, 0))
    perm_in = [jnp.asarray(_span_perm(), BF16)] if n_span else []
    perm_spec = [pl.BlockSpec((ROW_TILE, ROW_TILE), lambda i: (0, 0))] if n_span else []
    return pl.pallas_call(
        functools.partial(_in_proj_kernel, splits=splits, d_model=d, n_span=n_span),
        out_shape=([jax.ShapeDtypeStruct((t, wd), dt) for (_, wd, _), dt in zip(splits, out_dtypes)]
                   + [span_shape(wd) for (_, wd, _) in splits[:n_span]]),
        grid=(t // ROW_TILE,),
        in_specs=[row(d),
                  pl.BlockSpec((1, 1, 3 * d), lambda i: (i // tiles_per_seq, 0, 0)),
                  pl.BlockSpec((1, d), lambda i: (0, 0)),
                  pl.BlockSpec((d, n_w), lambda i: (0, 0))] + perm_spec,
        out_specs=[row(wd) for (_, wd, _) in splits] + [span_spec(wd) for (_, wd, _) in splits[:n_span]],
        compiler_params=_cparams(("arbitrary",)),
        name="in_proj",
    )(x2, mod3, gain.reshape(1, d), w_bf16, *perm_in)


def _t5_bucket(dist):
    dist = np.maximum(dist, 0)
    max_exact = REL_BUCKETS // 2
    large = max_exact + (np.log(np.maximum(dist, 1) / max_exact)
                         / math.log(REL_MAX_DIST / max_exact) * (REL_BUCKETS - max_exact)).astype(np.int32)
    large = np.minimum(large, REL_BUCKETS - 1)
    return np.where(dist < max_exact, dist, large).astype(np.int32)


def _bias_kernel(rb_ref, bucket_ref, mask_ref, o_ref):
    def body(r, carry):
        rows = pl.ds(pl.multiple_of(r * SUBLANES, SUBLANES), SUBLANES)
        bk = bucket_ref[0, rows, :]
        accs = [jnp.zeros(bk.shape, F32) for _ in range(A_HEADS)]
        for b in range(REL_BUCKETS):
            eq = bk == b
            accs = [jnp.where(eq, rb_ref[b, h], acc) for h, acc in enumerate(accs)]
        for f in range(2):
            keep = mask_ref[0, f, rows, :] != 0
            for h in range(A_HEADS):
                o_ref[0, f, h, rows, :] = jnp.where(keep, accs[h] * LOG2_E, MASK_NEG)
        return carry
    lax.fori_loop(0, A_BLOCK // SUBLANES, body, 0)


def _attn_bias_tables(rel_bias):
    qi = np.arange(A_BLOCK)[:, None]
    kj = np.arange(2 * A_BLOCK)[None, :]
    rel = qi + A_BLOCK - kj
    buckets, masks = [], []
    for window, dil in DILATED_CONFIGS:
        band = (rel >= 0) & (rel <= window // dil)
        bucket = _t5_bucket(rel * dil)
        mask = np.stack([band & (kj >= A_BLOCK), band]).astype(np.int32)
        if dil > 1:
            runs = SPAN_RES // dil
            run = A_BLOCK // runs
            rho = np.arange(A_BLOCK)
            sub = runs * (rho % run) + rho // run
            keys = np.concatenate([sub, A_BLOCK + sub])
            bucket = bucket[sub][:, keys]
            mask = mask[:, sub][:, :, keys]
        buckets.append(bucket)
        masks.append(mask)
    n_cfg = len(DILATED_CONFIGS)
    return pl.pallas_call(
        _bias_kernel,
        out_shape=jax.ShapeDtypeStruct((n_cfg, 2, A_HEADS, A_BLOCK, 2 * A_BLOCK), F32),
        grid=(n_cfg,),
        in_specs=[pl.BlockSpec(memory_space=pltpu.SMEM),
                  pl.BlockSpec((1, A_BLOCK, 2 * A_BLOCK), lambda c: (c, 0, 0)),
                  pl.BlockSpec((1, 2, A_BLOCK, 2 * A_BLOCK), lambda c: (c, 0, 0, 0))],
        out_specs=pl.BlockSpec((1, 2, A_HEADS, A_BLOCK, 2 * A_BLOCK), lambda c: (c, 0, 0, 0, 0)),
        compiler_params=_cparams(("arbitrary",)),
        name="attn_bias",
    )(rel_bias.astype(F32), jnp.asarray(np.stack(buckets)), jnp.asarray(np.stack(masks)))


def _attn_block(q, kcat, vcat, bias_ref):
    lane = lax.broadcasted_iota(jnp.int32, (A_BLOCK, LANES), 1)
    low = lane < A_HEAD_DIM
    lse_tile = jnp.zeros((A_BLOCK, LANES), F32)
    zero = jnp.zeros((A_BLOCK, LANES), BF16)
    heads = range(A_HEADS)
    pair = [slice((h // 2) * LANES, (h // 2 + 1) * LANES) for h in heads]
    scores = []
    for h in heads:
        qm = jnp.where(low if h % 2 == 0 else jnp.logical_not(low), q[:, pair[h]], zero)
        scores.append(lax.dot_general(qm, kcat[:, pair[h]], _NT, preferred_element_type=F32) + bias_ref[0, h])
    probs, inv_l = [], []
    for h in heads:
        m = jnp.max(scores[h], axis=-1, keepdims=True)
        p = jnp.exp2(scores[h] - m)
        l = jnp.sum(p, axis=-1, keepdims=True)
        probs.append(p.astype(BF16))
        inv_l.append(1.0 / l)
        lse_tile = jnp.where(lane == h, m + jnp.log2(l), lse_tile)
    outs = [jnp.dot(probs[h], vcat[:, pair[h]], preferred_element_type=F32) * inv_l[h] for h in heads]
    o = jnp.concatenate([jnp.where(low, outs[2 * hp], outs[2 * hp + 1]) for hp in range(A_HEADS // 2)], axis=1)
    return o, lse_tile


def _attn_kernel(q_ref, kp_ref, kc_ref, vp_ref, vc_ref, bias0_ref, bias_ref, o_ref, lse_ref, *, per_lead):
    def sub(ref, rr, i):
        if ref.ndim == 2:
            return ref[i * A_BLOCK:(i + 1) * A_BLOCK, :]
        run = ref.shape[3] // per_lead
        return ref[i // per_lead, :, rr, (i % per_lead) * run:(i % per_lead + 1) * run, :]

    def put(ref, rr, i, val):
        if ref.ndim == 2:
            ref[i * A_BLOCK:(i + 1) * A_BLOCK, :] = val.astype(ref.dtype)
        else:
            run = ref.shape[3] // per_lead
            ref[i // per_lead, :, rr, (i % per_lead) * run:(i % per_lead + 1) * run, :] = (
                val.astype(ref.dtype).reshape(ref.shape[1], run, ref.shape[4]))

    flat = lambda v: v.reshape(A_BLOCK, v.shape[-1])
    token_order = q_ref.ndim == 2
    n_blocks = q_ref.shape[0] // A_BLOCK if token_order else q_ref.shape[0] * per_lead
    for rr in range(1 if token_order else q_ref.shape[2]):
        k_prev = flat(kp_ref[...] if token_order else kp_ref[0, :, rr])
        v_prev = flat(vp_ref[...] if token_order else vp_ref[0, :, rr])
        for i in range(n_blocks):
            k_cur, v_cur = flat(sub(kc_ref, rr, i)), flat(sub(vc_ref, rr, i))
            o, lse_tile = _attn_block(flat(sub(q_ref, rr, i)), jnp.concatenate([k_prev, k_cur], axis=0),
                                      jnp.concatenate([v_prev, v_cur], axis=0),
                                      bias0_ref if i == 0 else bias_ref)
            put(o_ref, rr, i, o)
            put(lse_ref, rr, i, lse_tile)
            k_prev, v_prev = k_cur, v_cur


def _dilated_attention_one(q, k, v, bias_tab, bsz, seq, dil):
    nb = seq // dil // A_BLOCK
    step = min(ATTN_BLOCKS_PER_STEP, nb)
    w = A_WIDTH
    if dil == 1:
        per_lead = n_res = 1
        view = lambda a: a.reshape(bsz, seq, a.shape[-1])
        block = lambda wd: (None, step * A_BLOCK, wd)
        block_prev = lambda wd: (None, A_BLOCK, wd)
        cur = lambda b, r, m: (b, m, 0)
        prev = lambda b, r, m: (b, jnp.maximum(m * step - 1, 0), 0)
        out_shape = lambda wd, dt: jax.ShapeDtypeStruct((bsz, seq, wd), dt)
    else:
        runs = SPAN_RES // dil
        run = A_BLOCK // runs
        spans_per_seq = seq // SPAN
        per_lead = min(step, runs)
        lead = step // per_lead
        n_res = max(1, ATTN_BLOCKS_PER_STEP // step)
        assert runs % per_lead == 0 and spans_per_seq % lead == 0 and dil % n_res == 0
        view = lambda a: a.reshape(a.shape[0], runs, dil, A_BLOCK, a.shape[-1])
        block = lambda wd: (lead, runs, n_res, per_lead * run, wd)
        block_prev = lambda wd: (1, runs, n_res, run, wd)
        cur = lambda b, r, m: ((b * spans_per_seq + (m * step) // runs) // lead, 0, r, ((m * step) % runs) // per_lead, 0)

        def prev(b, r, m):
            n = jnp.maximum(m * step - 1, 0)
            return (b * spans_per_seq + n // runs, 0, r, n % runs, 0)

        out_shape = lambda wd, dt: jax.ShapeDtypeStruct((bsz * spans_per_seq, runs, dil, A_BLOCK, wd), dt)
    blk = pl.BlockSpec(block(w), cur)
    blk_prev = pl.BlockSpec(block_prev(w), prev)
    bias_block = (1, A_HEADS, A_BLOCK, 2 * A_BLOCK)
    o, lse = pl.pallas_call(
        functools.partial(_attn_kernel, per_lead=per_lead),
        out_shape=[out_shape(w, BF16), out_shape(LANES, F32)],
        grid=(bsz, dil // n_res, nb // step),
        in_specs=[blk, blk_prev, blk, blk_prev, blk,
                  pl.BlockSpec(bias_block, lambda b, r, m: (jnp.minimum(m, 1), 0, 0, 0)),
                  pl.BlockSpec(bias_block, lambda b, r, m: (1, 0, 0, 0))],
        out_specs=[pl.BlockSpec(block(w), cur), pl.BlockSpec(block(LANES), cur)],
        compiler_params=_cparams(("arbitrary", "arbitrary", "arbitrary")),
        name=f"dilated_attn_d{dil}",
    )(view(q), view(k), view(k), view(v), view(v), bias_tab, bias_tab)
    if dil == 1:
        return o.reshape(bsz * seq, w), lse.reshape(bsz * seq, LANES)
    return o.reshape(o.shape[0], SPAN_RES, A_BLOCK, w), lse.reshape(o.shape[0], SPAN_RES, A_BLOCK, LANES)


def _gelu_tanh(x):
    c = math.sqrt(2.0 / math.pi)
    return x * (0.5 * (1.0 + jnp.tanh(c * (x + 0.044715 * (x * x * x)))))


def _s5_kernel(u_ref, pm_ref, pmt_ref, bd_ref, pw_ref, pw16_ref, cd_ref, dsk_ref, gw_ref, gb_ref, o_ref,
               x_scr, st_scr, *, n_state):
    @pl.when(pl.program_id(1) == 0)
    def _():
        st_scr[...] = jnp.zeros_like(st_scr)

    for tile_idx in range(S5_TILES_PER_STEP):
        _s5_tile(tile_idx, u_ref, pm_ref, pmt_ref, bd_ref, pw_ref, pw16_ref, cd_ref, dsk_ref, gw_ref, gb_ref,
                 o_ref, x_scr, st_scr, n_state)


def _s5_tile(tile_idx, u_ref, pm_ref, pmt_ref, bd_ref, pw_ref, pw16_ref, cd_ref, dsk_ref, gw_ref, gb_ref,
             o_ref, x_scr, st_scr, n_state):
    n = n_state
    jn = S5_J
    tile_rows = slice(tile_idx * SUBLANES * jn, (tile_idx + 1) * SUBLANES * jn)

    u = u_ref[tile_rows, :]
    u_perm = jnp.dot(pm_ref[...], u.astype(BF16), preferred_element_type=F32).astype(BF16)
    cw = u.shape[1] * S5_LW // n
    chunks = [(slice(c0, c0 + S5_LW), slice(n + c0, n + c0 + S5_LW)) for c0 in range(0, n, S5_LW)]

    def project_in(q):
        bu = jnp.dot(u_perm[:, q * cw:(q + 1) * cw], bd_ref[q], preferred_element_type=F32)
        x_scr[:, chunks[q][0]] = bu[:, 0:S5_LW]
        x_scr[:, chunks[q][1]] = bu[:, S5_LW:]

    project_in(0)
    end_r, end_i = [], []
    for q, (re, im) in enumerate(chunks):
        if q + 1 < len(chunks):
            project_in(q + 1)
        a1r, a1i = pw_ref[0:SUBLANES, re], pw_ref[0:SUBLANES, im]
        xr = jnp.zeros((SUBLANES, S5_LW), F32)
        xi = jnp.zeros((SUBLANES, S5_LW), F32)
        for j in range(jn):
            rows = slice(SUBLANES * j, SUBLANES * (j + 1))
            nr = a1r * xr - a1i * xi + x_scr[rows, re]
            ni = a1r * xi + a1i * xr + x_scr[rows, im]
            xr, xi = nr, ni
            x_scr[rows, re] = xr
            x_scr[rows, im] = xi
        end_r.append(xr)
        end_i.append(xi)
    er, ei = jnp.concatenate(end_r, axis=1), jnp.concatenate(end_i, axis=1)
    last = SUBLANES * (jn - 1)
    ajr, aji = pw_ref[last:last + 1, 0:n], pw_ref[last:last + 1, n:]
    pr, pi = st_scr[0:1, 0:n], st_scr[0:1, n:]
    cin_r, cin_i = [], []
    for s in range(SUBLANES):
        cin_r.append(pr)
        cin_i.append(pi)
        nr = er[s:s + 1] + ajr * pr - aji * pi
        ni = ei[s:s + 1] + ajr * pi + aji * pr
        pr, pi = nr, ni
    st_scr[0:1, 0:n] = pr
    st_scr[0:1, n:] = pi
    cr_all = jnp.concatenate(cin_r, axis=0)
    ci_all = jnp.concatenate(cin_i, axis=0)
    pack = 2 * SUBLANES
    ys = []
    for q, (re, im) in enumerate(chunks):
        cr = jnp.concatenate([cr_all[:, re]] * 2, axis=0).astype(BF16)
        ci = jnp.concatenate([ci_all[:, re]] * 2, axis=0).astype(BF16)
        xr_parts, xi_parts = [], []
        for m in range(SUBLANES * jn // pack):
            rows = slice(pack * m, pack * (m + 1))
            pjr, pji = pw16_ref[rows, re], pw16_ref[rows, im]
            xr_parts.append(x_scr[rows, re].astype(BF16) + (pjr * cr - pji * ci))
            xi_parts.append(x_scr[rows, im].astype(BF16) + (pjr * ci + pji * cr))
        xcat = jnp.concatenate([jnp.concatenate(xr_parts, axis=0), jnp.concatenate(xi_parts, axis=0)], axis=1)
        ys.append(jnp.dot(xcat, cd_ref[q], preferred_element_type=F32))

    y_perm = jnp.concatenate(ys, axis=1)
    y_hi = y_perm.astype(BF16)
    y_lo = (y_perm - y_hi.astype(F32)).astype(BF16)
    y = (jnp.dot(pmt_ref[...], y_hi, preferred_element_type=F32)
         + jnp.dot(pmt_ref[...], y_lo, preferred_element_type=F32))
    y = _gelu_tanh(y + dsk_ref[...] * u)
    z = jnp.dot(y.astype(BF16), gw_ref[...], preferred_element_type=F32) + gb_ref[...]
    o_ref[tile_rows, :] = (y * _sigmoid(z)).astype(o_ref.dtype)


def _s5_tables(a_re, a_im, log_dt, b_re, b_im, c_re, c_im):
    g, p = a_re.shape
    dt = jnp.exp(log_dt.astype(F32))[:, None]
    ar, ai = a_re.astype(F32), a_im.astype(F32)
    mag = jnp.exp(dt * ar)
    abar_r, abar_i = mag * jnp.cos(dt * ai), mag * jnp.sin(dt * ai)
    den = ar * ar + ai * ai
    fr = ((abar_r - 1.0) * ar + abar_i * ai) / den
    fi = (abar_i * ar - (abar_r - 1.0) * ai) / den
    br, bi = b_re.astype(F32), b_im.astype(F32)
    bbar_r = fr[..., None] * br - fi[..., None] * bi
    bbar_i = fr[..., None] * bi + fi[..., None] * br
    m = br.shape[-1]
    gc = S5_LW // p
    nq = g // gc
    eye = jnp.eye(gc, dtype=F32)
    dense_b = lambda t: jnp.einsum('qgpm,gh->qgmhp', t.reshape(nq, gc, p, m), eye).reshape(nq, gc * m, gc * p)
    bd = jnp.concatenate([dense_b(bbar_r), dense_b(bbar_i)], axis=2)
    dense_c = lambda t: jnp.einsum('qgmp,gh->qgphm', t.reshape(nq, gc, m, p), eye).reshape(nq, gc * p, gc * m)
    cd = jnp.concatenate([dense_c(c_re.astype(F32)), -dense_c(c_im.astype(F32))], axis=1)
    kk = jnp.arange(1, S5_J + 1, dtype=F32)[:, None, None]
    pmag = jnp.exp(kk * (dt * ar)[None])
    pw_r = (pmag * jnp.cos(kk * (dt * ai)[None])).reshape(S5_J, g * p)
    pw_i = (pmag * jnp.sin(kk * (dt * ai)[None])).reshape(S5_J, g * p)
    pw = jnp.repeat(jnp.concatenate([pw_r, pw_i], axis=1), SUBLANES, axis=0)
    return bd.astype(BF16), pw, cd.astype(BF16)


def _s5_layer(u, bsz, seq, a_re, a_im, log_dt, b_re, b_im, c_re, c_im, d_skip, glu_w, glu_b):
    t, width = u.shape
    n = a_re.shape[0] * a_re.shape[1]
    bd, pw, cd = _s5_tables(a_re, a_im, log_dt, b_re, b_im, c_re, c_im)
    tile = SUBLANES * S5_J
    step_rows = S5_TILES_PER_STEP * tile
    steps_per_seq = seq // step_rows
    const = lambda b, i: (0, 0)
    perm = _sublane_major_perm(tile)
    return pl.pallas_call(
        functools.partial(_s5_kernel, n_state=n),
        out_shape=jax.ShapeDtypeStruct((t, width), BF16),
        grid=(bsz, steps_per_seq),
        in_specs=[pl.BlockSpec((step_rows, width), lambda b, i: (b * steps_per_seq + i, 0)),
                  pl.BlockSpec((tile, tile), const),
                  pl.BlockSpec((tile, tile), const),
                  pl.BlockSpec(bd.shape, lambda b, i: (0, 0, 0)),
                  pl.BlockSpec((tile, 2 * n), const),
                  pl.BlockSpec((tile, 2 * n), const),
                  pl.BlockSpec(cd.shape, lambda b, i: (0, 0, 0)),
                  pl.BlockSpec((1, width), const),
                  pl.BlockSpec((width, width), const),
                  pl.BlockSpec((1, width), const)],
        out_specs=pl.BlockSpec((step_rows, width), lambda b, i: (b * steps_per_seq + i, 0)),
        scratch_shapes=[pltpu.VMEM((tile, 2 * n), F32), pltpu.VMEM((SUBLANES, 2 * n), F32)],
        compiler_params=_cparams(("arbitrary", "arbitrary")),
        name="s5_layer",
    )(u, jnp.asarray(perm, BF16), jnp.asarray(perm.T, BF16), bd, pw, pw.astype(BF16), cd,
      d_skip.reshape(1, width).astype(F32), glu_w.astype(BF16),
      glu_b.reshape(1, width).astype(F32))


def _finish(y, x_ref, mod_ref, pg_ref, out_ref, d_model):
    ms = jnp.mean(y * y, axis=-1, keepdims=True)
    yn = y * lax.rsqrt(ms + EPS) * pg_ref[...]
    gate_mod = mod_ref[0, :, 2 * d_model:3 * d_model]
    out_ref[...] = x_ref[...] + gate_mod * yn


def _split2(x):
    hi = x.astype(BF16)
    return hi, (x - hi.astype(F32)).astype(BF16)


def _out_ab_kernel(o0_ref, l0_ref, o1_ref, o2_ref, l1_ref, l2_ref, ob_ref, gate_ref, e_ref, un_ref, w_ref,
                   x_ref, mod_ref, pg_ref, out_ref, *, d_model):
    flat = lambda ref: ref[...].reshape(ROW_TILE, ref.shape[-1])
    aw = o0_ref.shape[-1]

    def unperm(o_ref, l_ref):
        parts = jnp.concatenate([flat(o_ref)] + list(_split2(flat(l_ref))), axis=1)
        moved = jnp.dot(un_ref[...], parts, preferred_element_type=F32)
        return moved[:, 0:aw], moved[:, aw:aw + LANES] + moved[:, aw + LANES:]

    o0, l0 = o0_ref[...].astype(F32), l0_ref[...]
    o1, l1 = unperm(o1_ref, l1_ref)
    o2, l2 = unperm(o2_ref, l2_ref)
    mx = jnp.maximum(jnp.maximum(l0, l1), l2)
    e0, e1, e2 = jnp.exp2(l0 - mx), jnp.exp2(l1 - mx), jnp.exp2(l2 - mx)
    inv_den = 1.0 / (e0 + e1 + e2)
    stacked = jnp.concatenate(list(_split2(e1 * inv_den)) + list(_split2(e2 * inv_den)), axis=0)
    wide = jnp.dot(stacked, e_ref[...], preferred_element_type=F32)
    term = lambda i: wide[i * ROW_TILE:(i + 1) * ROW_TILE]
    w1 = term(0) + term(1)
    w2 = term(2) + term(3)
    o_a = (1.0 - w1 - w2) * o0 + w1 * o1 + w2 * o2
    gate = gate_ref[...].astype(F32)
    sg = gate * _sigmoid(gate)
    aw = o_a.shape[-1]
    y = jnp.dot((o_a * sg[:, :aw]).astype(BF16), w_ref[0:aw, :], preferred_element_type=F32)
    y = y + jnp.dot((ob_ref[...].astype(F32) * sg[:, aw:]).astype(BF16), w_ref[aw:, :],
                    preferred_element_type=F32)
    _finish(y, x_ref, mod_ref, pg_ref, out_ref, d_model)


def _out_c_kernel(o_ref, gate_ref, w_ref, x_ref, mod_ref, pg_ref, out_ref, *, d_model):
    gate = gate_ref[...].astype(F32)
    o = o_ref[...].astype(F32) * (gate * _sigmoid(gate))
    y = jnp.dot(o.astype(BF16), w_ref[...], preferred_element_type=F32)
    _finish(y, x_ref, mod_ref, pg_ref, out_ref, d_model)


def _out_proj(kern, row_inputs, const_inputs, w_bf16, x2, mod3, post_g, seq, name, tile=ROW_TILE):
    t, d = x2.shape
    tiles_per_seq = seq // tile
    tiles_per_span = SPAN // tile

    def row_spec(a):
        if a.ndim == 2:
            return pl.BlockSpec((tile, a.shape[1]), lambda i: (i, 0))
        assert tile == ROW_TILE
        return pl.BlockSpec((None, SPAN_RES, SPAN_RUN, a.shape[-1]),
                            lambda i: (i // tiles_per_span, 0, i % tiles_per_span, 0))

    const_spec = lambda a: pl.BlockSpec(a.shape, lambda i: (0, 0))
    return pl.pallas_call(
        functools.partial(kern, d_model=d),
        out_shape=jax.ShapeDtypeStruct((t, d), F32),
        grid=(t // tile,),
        in_specs=([row_spec(a) for a in row_inputs] + [const_spec(a) for a in const_inputs]
                  + [const_spec(w_bf16), row_spec(x2),
                     pl.BlockSpec((1, 1, 3 * d), lambda i: (i // tiles_per_seq, 0, 0)),
                     pl.BlockSpec((1, d), lambda i: (0, 0))]),
        out_specs=pl.BlockSpec((tile, d), lambda i: (i, 0)),
        compiler_params=_cparams(("arbitrary",)),
        name=name,
    )(*row_inputs, *const_inputs, w_bf16, x2, mod3, post_g.reshape(1, d))


def _gdn_prep_kernel(x_ref, halo_ref, br_ref, cw_ref, alog_ref, dtb_ref,
                     w_ref, u_ref, qg_ref, kdt_ref, aqk_ref, dec_ref, xs_scr):
    n_all = GDN_PREP_BLOCKS * GDN_STEP
    first = pl.program_id(1) == 0
    halo = halo_ref[...]
    xs_scr[0:SUBLANES, :] = jnp.where(first, jnp.zeros_like(halo), halo)
    xs_scr[SUBLANES:SUBLANES + n_all, :] = x_ref[...]
    cw = cw_ref[...]
    ext = xs_scr[...].reshape(n_all // SUBLANES + 1, SUBLANES, x_ref.shape[1])
    first_sublane = lax.broadcasted_iota(jnp.int32, ext.shape, 1) == 0

    def delay(y):
        rot = pltpu.roll(y, 1, axis=1)
        prev = jnp.concatenate([rot[:1], rot[:-1]], axis=0)
        return jnp.where(first_sublane, prev, rot)

    acc = ext * cw[0:1]
    for j in range(1, C_CONV):
        acc = delay(acc) + ext * cw[j:j + 1]
    conv = acc[1:].reshape(n_all, x_ref.shape[1])
    act = conv * _sigmoid(conv)
    _gdn_prep_blocks(act, br_ref, alog_ref, dtb_ref, w_ref, u_ref, qg_ref, kdt_ref, aqk_ref, dec_ref)


def _gdn_prep_blocks(act, br_ref, alog_ref, dtb_ref, w_ref, u_ref, qg_ref, kdt_ref, aqk_ref, dec_ref):
    n = GDN_STEP
    c = GDN_CHUNK
    row = lax.broadcasted_iota(jnp.int32, (n, n), 0)
    col = lax.broadcasted_iota(jnp.int32, (n, n), 1)
    cum = jnp.logical_and((row // c) == (col // c), row >= col).astype(F32)
    prow = lax.broadcasted_iota(jnp.int32, (c, n), 0)
    plane = lax.broadcasted_iota(jnp.int32, (c, n), 1)
    left = plane < c
    pcol = jnp.where(left, plane, plane - c)
    tril = prow >= pcol
    strict = prow > pcol
    eye = (prow == pcol).astype(F32)
    pair = lambda full: jnp.where(left, full[0:c], full[c:])

    def blockdiag(p):
        zero = jnp.zeros_like(p)
        return jnp.concatenate([jnp.where(left, p, zero), jnp.where(left, zero, p)], axis=0)

    first_chunk = lax.broadcasted_iota(jnp.int32, (n, 1), 0) < GDN_CHUNK
    dec_row = lax.broadcasted_iota(jnp.int32, (SUBLANES, LANES), 0)
    heads = range(C_HEADS)
    lanes = [slice(h * LANES, (h + 1) * LANES) for h in heads]

    def gates(blk):
        rows = slice(blk * n, (blk + 1) * n)
        br = br_ref[rows, :]
        xg = br + dtb_ref[...]
        softplus = jnp.maximum(xg, 0.0) + jnp.log(1.0 + jnp.exp(-jnp.abs(xg)))
        g_all = -jnp.exp(alog_ref[...]) * softplus
        gc_all = jnp.dot(cum, g_all, preferred_element_type=F32, precision=HI)
        return dict(rows=rows, beta=_sigmoid(br), gc=gc_all, gc_t=gc_all.T)

    def first_stage(ctx, blk, h):
        rows = ctx["rows"]
        gcol = ctx["gc"][:, C_HEADS + h:C_HEADS + h + 1]
        grow = ctx["gc_t"][C_HEADS + h:C_HEADS + h + 1, :]
        eg = jnp.exp(gcol)
        gcol_p = jnp.where(left, gcol[0:c], gcol[c:])
        decay = jnp.where(tril, jnp.exp(jnp.where(tril, gcol_p - grow, 0.0)), 0.0)
        q = act[rows, h * C_DK:(h + 1) * C_DK]
        k = act[rows, (C_HEADS + h) * C_DK:(C_HEADS + h + 1) * C_DK]
        v = act[rows, 2 * C_HEADS * C_DK + h * C_DV:2 * C_HEADS * C_DK + (h + 1) * C_DV]
        q = q * lax.rsqrt(jnp.sum(q * q, axis=-1, keepdims=True) + EPS) * (C_DK ** -0.5)
        k = k * lax.rsqrt(jnp.sum(k * k, axis=-1, keepdims=True) + EPS)
        beta = ctx["beta"][:, h:h + 1]
        kb = k * beta
        k16 = k.astype(BF16)
        a = jnp.where(strict, pair(lax.dot_general(kb.astype(BF16), k16, _NT, preferred_element_type=F32)) * decay,
                      0.0)
        aqk = pair(lax.dot_general(q.astype(BF16), k16, _NT, preferred_element_type=F32)) * decay
        aqk_ref[rows, lanes[h]] = blockdiag(aqk).astype(BF16)
        qg_ref[rows, lanes[h]] = (q * eg).astype(BF16)
        g_last = jnp.where(first_chunk, gcol[GDN_CHUNK - 1:GDN_CHUNK], gcol[n - 1:n])
        kdt_ref[rows, lanes[h]] = (k * jnp.exp(g_last - gcol)).T.astype(BF16)
        dec_ref[blk, :, lanes[h]] = jnp.where(dec_row < SUBLANES // 2, eg[GDN_CHUNK - 1:GDN_CHUNK], eg[n - 1:n])
        rhs = jnp.concatenate([v * beta, kb * eg], axis=1).astype(BF16)
        return dict(a=a, inv=eye - a, rhs=rhs)

    n_blocks = act.shape[0] // n
    first_ctx = gates(0)
    state = [first_stage(first_ctx, 0, h) for h in heads]
    for blk in range(n_blocks):
        nxt_ctx = gates(blk + 1) if blk + 1 < n_blocks else None
        nxt = []

        def side_work(hs, blk=blk, nxt=nxt, nxt_ctx=nxt_ctx):
            if nxt_ctx is not None:
                nxt.extend(first_stage(nxt_ctx, blk + 1, h) for h in hs)

        invs = [st["inv"] for st in state]
        pows = []
        for st in state:
            pows.append(jnp.dot(st["a"].astype(BF16), blockdiag(st["a"]).astype(BF16), preferred_element_type=F32))
        side_work(heads[0:2])
        for level in range(4):
            for h in heads:
                both = jnp.dot(jnp.concatenate([invs[h], pows[h]], axis=0).astype(BF16),
                               blockdiag(pows[h]).astype(BF16), preferred_element_type=F32)
                invs[h] = invs[h] + both[0:c]
                pows[h] = both[c:]
            side_work(heads[2 + level:3 + level])
        for h in heads:
            invs[h] = invs[h] + jnp.dot(invs[h].astype(BF16), blockdiag(pows[h]).astype(BF16),
                                        preferred_element_type=F32)
        side_work(heads[6:8])
        rows = slice(blk * n, (blk + 1) * n)
        for h in heads:
            uw = jnp.dot(blockdiag(invs[h]).astype(BF16), state[h]["rhs"], preferred_element_type=F32)
            u_ref[rows, lanes[h]] = uw[:, 0:C_DV].astype(u_ref.dtype)
            w_ref[rows, lanes[h]] = uw[:, C_DV:].astype(BF16)
        state = nxt


def _gdn_rec_kernel(w_ref, u_ref, qg_ref, kdt_ref, aqk_ref, dec_ref, ng_ref, o_ref, s_scr):
    @pl.when(pl.program_id(0) == 0)
    def _():
        s_scr[...] = jnp.zeros_like(s_scr)

    half = SUBLANES // 2
    zeros = jnp.zeros((GDN_CHUNK, C_DV), BF16)
    lanes = [slice(h * LANES, (h + 1) * LANES) for h in range(C_HEADS)]
    chains = [(b, h) for b in range(w_ref.shape[0]) for h in range(C_HEADS)]
    states = [s_scr[b * C_HEADS + h] for b, h in chains]
    for blk in range(GDN_REC_BLOCKS):
        blk_rows = slice(blk * GDN_STEP, (blk + 1) * GDN_STEP)
        for j in range(GDN_STEP // GDN_CHUNK):
            rows = slice(blk * GDN_STEP + j * GDN_CHUNK, blk * GDN_STEP + (j + 1) * GDN_CHUNK)
            wss = []
            for c, (b, h) in enumerate(chains):
                wq = jnp.concatenate([w_ref[b, rows, lanes[h]], qg_ref[b, rows, lanes[h]]], axis=0)
                wss.append(jnp.dot(wq, states[c].astype(BF16), preferred_element_type=F32))
            for c, (b, h) in enumerate(chains):
                v_new = (u_ref[b, rows, lanes[h]].astype(F32) - wss[c][0:GDN_CHUNK]).astype(BF16)
                v_pad = jnp.concatenate([v_new, zeros] if j == 0 else [zeros, v_new], axis=0)
                both = jnp.dot(jnp.concatenate([aqk_ref[b, rows, lanes[h]], kdt_ref[b, blk_rows, lanes[h]]], axis=0),
                               v_pad, preferred_element_type=F32)
                o = wss[c][GDN_CHUNK:] + both[0:GDN_CHUNK]
                dec = dec_ref[b, blk, j * half:j * half + 1, lanes[h]]
                states[c] = states[c] * dec + both[GDN_CHUNK:]
                ms = jnp.mean(o * o, axis=-1, keepdims=True)
                o_ref[b, rows, lanes[h]] = (o * lax.rsqrt(ms + EPS) * ng_ref[...]).astype(o_ref.dtype)
    for c, (b, h) in enumerate(chains):
        s_scr[b * C_HEADS + h] = states[c]


def _gdn_core(qkv_pre, br, bsz, seq, conv_w, a_log, dt_bias, norm_g):
    t, qkv_w = qkv_pre.shape
    hw = C_HEADS * C_DV
    steps_per_seq = seq // GDN_STEP
    prep_rows = GDN_PREP_BLOCKS * GDN_STEP
    prep_steps = steps_per_seq // GDN_PREP_BLOCKS
    halo_blocks = prep_rows // SUBLANES
    pad_row = lambda vec: jnp.zeros((1, LANES), F32).at[0, C_HEADS:2 * C_HEADS].set(vec.astype(F32))
    tok = lambda b, i: (b * prep_steps + i, 0)
    const = lambda b, i: (0, 0)
    tok_spec = pl.BlockSpec((prep_rows, hw), tok)
    dec_spec = pl.BlockSpec((GDN_PREP_BLOCKS, SUBLANES, hw), lambda b, i: (b * prep_steps + i, 0, 0))
    w, u, qg, kdt, aqk, dec = pl.pallas_call(
        _gdn_prep_kernel,
        out_shape=[jax.ShapeDtypeStruct((t, hw), BF16), jax.ShapeDtypeStruct((t, hw), BF16),
                   jax.ShapeDtypeStruct((t, hw), BF16), jax.ShapeDtypeStruct((t, hw), BF16),
                   jax.ShapeDtypeStruct((t, hw), BF16),
                   jax.ShapeDtypeStruct((t // GDN_STEP, SUBLANES, hw), F32)],
        grid=(bsz, prep_steps),
        in_specs=[pl.BlockSpec((prep_rows, qkv_w), tok),
                  pl.BlockSpec((SUBLANES, qkv_w),
                               lambda b, i: (jnp.maximum((b * prep_steps + i) * halo_blocks - 1, 0), 0)),
                  pl.BlockSpec((prep_rows, LANES), tok),
                  pl.BlockSpec((C_CONV, qkv_w), const),
                  pl.BlockSpec((1, LANES), const),
                  pl.BlockSpec((1, LANES), const)],
        out_specs=[tok_spec, tok_spec, tok_spec, tok_spec, tok_spec, dec_spec],
        scratch_shapes=[pltpu.VMEM((SUBLANES + prep_rows, qkv_w), F32)],
        compiler_params=_cparams(("arbitrary", "arbitrary")),
        name="gdn_prep",
    )(qkv_pre, qkv_pre, br, conv_w.astype(F32), pad_row(a_log), pad_row(dt_bias))
    rec_steps = steps_per_seq // GDN_REC_BLOCKS
    seq_view = lambda a: a.reshape(bsz, seq, hw)
    rec_tok = pl.BlockSpec((bsz, GDN_REC_BLOCKS * GDN_STEP, hw), lambda i: (0, i, 0))
    rec_dec = pl.BlockSpec((bsz, GDN_REC_BLOCKS, SUBLANES, hw), lambda i: (0, i, 0, 0))
    o = pl.pallas_call(
        _gdn_rec_kernel,
        out_shape=jax.ShapeDtypeStruct((bsz, seq, hw), BF16),
        grid=(rec_steps,),
        in_specs=[rec_tok, rec_tok, rec_tok, rec_tok, rec_tok, rec_dec,
                  pl.BlockSpec((1, C_DV), lambda i: (0, 0))],
        out_specs=rec_tok,
        scratch_shapes=[pltpu.VMEM((bsz * C_HEADS, C_DK, C_DV), F32)],
        compiler_params=_cparams(("arbitrary",)),
        name="gdn_recurrence",
    )(seq_view(w), seq_view(u), seq_view(qg), seq_view(kdt), seq_view(aqk),
      dec.reshape(bsz, steps_per_seq, SUBLANES, hw), norm_g.reshape(1, C_DV).astype(F32))
    return o.reshape(t, hw)


def _ab_layer(x2, mod3, bsz, seq, pre_g, post_g, rel_bias, w_in, w_out, s5_params):
    b_width = s5_params[7].shape[-1]
    assert DILATED_CONFIGS[0][1] == 1 and all(SPAN_RES % dl == 0 for _, dl in DILATED_CONFIGS) and seq % SPAN == 0
    head_expand = jnp.asarray(np.arange(LANES)[:, None] == (np.arange(A_WIDTH)[None, :] // A_HEAD_DIM), BF16)
    unperm = jnp.asarray(_span_perm().T, BF16)
    splits = ((0, A_WIDTH, A_HEAD_DIM ** -0.5 * LOG2_E), (A_WIDTH, A_WIDTH, 1.0), (2 * A_WIDTH, A_WIDTH, 1.0),
              (3 * A_WIDTH, b_width, 1.0), (3 * A_WIDTH + b_width, A_WIDTH + b_width, 1.0))
    q, k, v, u, gate, qs, ks, vs = _in_proj(x2, mod3, pre_g, w_in.astype(BF16), splits,
                                            (BF16, BF16, BF16, F32, BF16), seq, n_span=3)
    outs, lses = [], []
    bias_tabs = _attn_bias_tables(rel_bias)
    for cfg, (_, dil) in enumerate(DILATED_CONFIGS):
        qkv = (q, k, v) if dil == 1 else (qs, ks, vs)
        o_c, lse_c = _dilated_attention_one(*qkv, bias_tabs[cfg], bsz, seq, dil)
        outs.append(o_c)
        lses.append(lse_c)
    o_b = _s5_layer(u, bsz, seq, *s5_params)
    row_inputs = [outs[0], lses[0], outs[1], outs[2], lses[1], lses[2], o_b, gate]
    return _out_proj(_out_ab_kernel, row_inputs, [head_expand, unperm], w_out.astype(BF16),
                     x2, mod3, post_g, seq, "out_proj_ab")


def _gdn_layer(x2, mod3, bsz, seq, pre_g, post_g, w_in, conv_w, a_log, dt_bias, norm_g, w_out):
    d = x2.shape[1]
    qkv_w = 2 * C_HEADS * C_DK + C_HEADS * C_DV
    gate_w = C_HEADS * C_DV
    w_pad = jnp.concatenate(
        [w_in, jnp.zeros((d, LANES - (w_in.shape[1] - qkv_w - gate_w)), w_in.dtype)], axis=1).astype(BF16)
    splits = ((0, qkv_w, 1.0), (qkv_w, gate_w, 1.0), (qkv_w + gate_w, LANES, 1.0))
    qkv_pre, gate, br = _in_proj(x2, mod3, pre_g, w_pad, splits, (F32, BF16, F32), seq)
    o = _gdn_core(qkv_pre, br, bsz, seq, conv_w, a_log, dt_bias, norm_g)
    return _out_proj(_out_c_kernel, [o, gate], [], w_out.astype(BF16), x2, mod3, post_g, seq, "out_proj_c",
                     tile=OUT_C_TILE)


def kernel(x, c, ada_w, ada_b, pre_g, post_g, rel_bias, ab_w_in, ab_w_out, s5_a_re, s5_a_im, s5_log_dt, s5_b_re, s5_b_im, s5_c_re, s5_c_im, s5_d, s5_glu_w, s5_glu_b, gdn_w_in, gdn_conv, gdn_a_log, gdn_dt_bias, gdn_norm_g, gdn_w_out):
    bsz, seq, d = x.shape
    depth = ada_w.shape[0]
    assert seq % ROW_TILE == 0 and seq % OUT_C_TILE == 0 and seq % (SUBLANES * S5_J * S5_TILES_PER_STEP) == 0
    assert seq % (GDN_STEP * GDN_PREP_BLOCKS) == 0 and seq % (GDN_STEP * GDN_REC_BLOCKS) == 0
    x2 = x.reshape(bsz * seq, d)
    mod = _adaln_mod(c, ada_w, ada_b)
    for layer in range(depth):
        j = layer // 2
        mod3 = mod[layer].reshape(bsz, 1, 3 * d)
        if layer % 2 == 0:
            s5_params = (s5_a_re[j], s5_a_im[j], s5_log_dt[j], s5_b_re[j], s5_b_im[j], s5_c_re[j], s5_c_im[j],
                         s5_d[j], s5_glu_w[j], s5_glu_b[j])
            x2 = _ab_layer(x2, mod3, bsz, seq, pre_g[layer], post_g[layer], rel_bias, ab_w_in[j], ab_w_out[j],
                           s5_params)
        else:
            x2 = _gdn_layer(x2, mod3, bsz, seq, pre_g[layer], post_g[layer], gdn_w_in[j], gdn_conv[j],
                            gdn_a_log[j], gdn_dt_bias[j], gdn_norm_g[j], gdn_w_out[j])
    return x2.reshape(bsz, seq, d)
```

```python
import functools
import math

import numpy as np
import jax
import jax.numpy as jnp
from jax import lax
from jax.experimental import pallas as pl
from jax.experimental.pallas import tpu as pltpu

F32 = jnp.float32
BF16 = jnp.bfloat16
HI = lax.Precision.HIGHEST

EPS = 1e-6
A_HEADS = 8
A_HEAD_DIM = 64
A_WIDTH = A_HEADS * A_HEAD_DIM
A_BLOCK = 128
DILATED_CONFIGS = ((128, 1), (512, 4), (2048, 16))
B_GROUP = 16
B_STATE = 64
REL_BUCKETS = 32
REL_MAX_DIST = 2048
C_HEADS = 8
C_DK = 128
C_DV = 128
C_CONV = 4
MASK_NEG = -1e30
LOG2_E = math.log2(math.e)

LANES = 128
SUBLANES = 8
VMEM_LIMIT = 48 * 1024 * 1024
ROW_TILE = 512
OUT_C_TILE = 1024
COL_CHUNK = 512
SPAN_RES = max(dl for _, dl in DILATED_CONFIGS)
SPAN = A_BLOCK * SPAN_RES
SPAN_RUN = ROW_TILE // SPAN_RES
PERM_ROWS = 256
PERM_RUN = PERM_ROWS // SPAN_RES
ATTN_BLOCKS_PER_STEP = 8
S5_J = 32
S5_LW = 512
S5_TILES_PER_STEP = 4
GDN_STEP = 128
GDN_CHUNK = 64
GDN_PREP_BLOCKS = 4
GDN_REC_BLOCKS = 2

_NT = (((1,), (1,)), ((), ()))


def _cparams(sem):
    return pltpu.CompilerParams(dimension_semantics=sem, vmem_limit_bytes=VMEM_LIMIT)


def _sigmoid(x):
    return 1.0 / (1.0 + jnp.exp(-x))


def _mod_kernel(c_ref, w_ref, b_ref, o_ref):
    c = c_ref[...]
    ca = c * _sigmoid(c)
    o_ref[0] = jnp.dot(ca.astype(BF16), w_ref[0].astype(BF16), preferred_element_type=F32) + b_ref[0]


def _adaln_mod(c, ada_w, ada_b):
    depth, d, d3 = ada_w.shape
    bsz = c.shape[0]
    return pl.pallas_call(
        _mod_kernel,
        out_shape=jax.ShapeDtypeStruct((depth, bsz, d3), F32),
        grid=(depth, d3 // d),
        in_specs=[pl.BlockSpec((bsz, d), lambda l, j: (0, 0)),
                  pl.BlockSpec((1, d, d), lambda l, j: (l, 0, j)),
                  pl.BlockSpec((1, 1, d), lambda l, j: (l, 0, j))],
        out_specs=pl.BlockSpec((1, bsz, d), lambda l, j: (l, 0, j)),
        compiler_params=_cparams(("arbitrary", "arbitrary")),
        name="adaln_mod",
    )(c, ada_w, ada_b.reshape(depth, 1, d3))


def _span_perm():
    rho = np.arange(PERM_ROWS)
    nat = SPAN_RES * (rho % PERM_RUN) + rho // PERM_RUN
    return (nat[:, None] == np.arange(PERM_ROWS)[None, :]).astype(np.float32)


def _modulated_norm(x_ref, mod_ref, g_ref, d_model):
    x = x_ref[...]
    ms = jnp.mean(x * x, axis=-1, keepdims=True)
    y = x * lax.rsqrt(ms + EPS) * g_ref[...]
    shift = mod_ref[0, :, 0:d_model]
    scale = mod_ref[0, :, d_model:2 * d_model]
    return (y * (1.0 + scale) + shift).astype(BF16)


def _sublane_major_perm(tile):
    rho = np.arange(tile)
    src = (rho % SUBLANES) * (tile // SUBLANES) + rho // SUBLANES
    return (src[:, None] == np.arange(tile)[None, :]).astype(np.float32)


def _in_proj_kernel(x_ref, mod_ref, g_ref, w_ref, *rest, splits, d_model, n_span):
    pm_ref = rest[0] if n_span else None
    out_refs = rest[1:] if n_span else rest
    span_refs = out_refs[len(splits):]
    h = _modulated_norm(x_ref, mod_ref, g_ref, d_model)
    for idx, ((c0, width, mult), o_ref) in enumerate(zip(splits, out_refs)):
        for cc in range(0, width, COL_CHUNK):
            cw = min(COL_CHUNK, width - cc)
            acc = jnp.dot(h, w_ref[:, c0 + cc:c0 + cc + cw], preferred_element_type=F32)
            if mult != 1.0:
                acc = acc * mult
            val = acc.astype(o_ref.dtype)
            o_ref[:, cc:cc + cw] = val
            if idx < n_span:
                for piece in range(ROW_TILE // PERM_ROWS):
                    moved = jnp.dot(pm_ref[...], val[piece * PERM_ROWS:(piece + 1) * PERM_ROWS],
                                    preferred_element_type=F32).astype(BF16)
                    span_refs[idx][:, piece * PERM_RUN:(piece + 1) * PERM_RUN, cc:cc + cw] = (
                        moved.reshape(SPAN_RES, PERM_RUN, cw))


def _in_proj(x2, mod3, gain, w_bf16, splits, out_dtypes, seq, n_span=0):
    t, d = x2.shape
    tiles_per_seq = seq // ROW_TILE
    tiles_per_span = SPAN // ROW_TILE
    n_w = w_bf16.shape[1]
    row = lambda wd: pl.BlockSpec((ROW_TILE, wd), lambda i: (i, 0))
    span_shape = lambda wd: jax.ShapeDtypeStruct((t // SPAN, SPAN_RES, A_BLOCK, wd), BF16)
    span_spec = lambda wd: pl.BlockSpec((None, SPAN_RES, SPAN_RUN, wd),
                                        lambda i: (i // tiles_per_span, 0, i % tiles_per_span, 0))
    perm_in = [jnp.asarray(_span_perm(), BF16)] if n_span else []
    perm_spec = [pl.BlockSpec((PERM_ROWS, PERM_ROWS), lambda i: (0, 0))] if n_span else []
    return pl.pallas_call(
        functools.partial(_in_proj_kernel, splits=splits, d_model=d, n_span=n_span),
        out_shape=([jax.ShapeDtypeStruct((t, wd), dt) for (_, wd, _), dt in zip(splits, out_dtypes)]
                   + [span_shape(wd) for (_, wd, _) in splits[:n_span]]),
        grid=(t // ROW_TILE,),
        in_specs=[row(d),
                  pl.BlockSpec((1, 1, 3 * d), lambda i: (i // tiles_per_seq, 0, 0)),
                  pl.BlockSpec((1, d), lambda i: (0, 0)),
                  pl.BlockSpec((d, n_w), lambda i: (0, 0))] + perm_spec,
        out_specs=[row(wd) for (_, wd, _) in splits] + [span_spec(wd) for (_, wd, _) in splits[:n_span]],
        compiler_params=_cparams(("arbitrary",)),
        name="in_proj",
    )(x2, mod3, gain.reshape(1, d), w_bf16, *perm_in)


def _t5_bucket(dist):
    dist = np.maximum(dist, 0)
    max_exact = REL_BUCKETS // 2
    large = max_exact + (np.log(np.maximum(dist, 1) / max_exact)
                         / math.log(REL_MAX_DIST / max_exact) * (REL_BUCKETS - max_exact)).astype(np.int32)
    large = np.minimum(large, REL_BUCKETS - 1)
    return np.where(dist < max_exact, dist, large).astype(np.int32)


def _bias_kernel(rb_ref, bucket_ref, mask_ref, o_ref):
    def body(r, carry):
        rows = pl.ds(pl.multiple_of(r * SUBLANES, SUBLANES), SUBLANES)
        bk = bucket_ref[0, rows, :]
        accs = [jnp.zeros(bk.shape, F32) for _ in range(A_HEADS)]
        for b in range(REL_BUCKETS):
            eq = bk == b
            accs = [jnp.where(eq, rb_ref[b, h], acc) for h, acc in enumerate(accs)]
        for f in range(2):
            keep = mask_ref[0, f, rows, :] != 0
            for h in range(A_HEADS):
                o_ref[0, f, h, rows, :] = jnp.where(keep, accs[h] * LOG2_E, MASK_NEG)
        return carry
    lax.fori_loop(0, A_BLOCK // SUBLANES, body, 0)


def _attn_bias_tables(rel_bias):
    qi = np.arange(A_BLOCK)[:, None]
    kj = np.arange(2 * A_BLOCK)[None, :]
    rel = qi + A_BLOCK - kj
    buckets, masks = [], []
    for window, dil in DILATED_CONFIGS:
        band = (rel >= 0) & (rel <= window // dil)
        bucket = _t5_bucket(rel * dil)
        mask = np.stack([band & (kj >= A_BLOCK), band]).astype(np.int32)
        if dil > 1:
            runs = SPAN_RES // dil
            run = A_BLOCK // runs
            rho = np.arange(A_BLOCK)
            sub = runs * (rho % run) + rho // run
            keys = np.concatenate([sub, A_BLOCK + sub])
            bucket = bucket[sub][:, keys]
            mask = mask[:, sub][:, :, keys]
        buckets.append(bucket)
        masks.append(mask)
    n_cfg = len(DILATED_CONFIGS)
    return pl.pallas_call(
        _bias_kernel,
        out_shape=jax.ShapeDtypeStruct((n_cfg, 2, A_HEADS, A_BLOCK, 2 * A_BLOCK), F32),
        grid=(n_cfg,),
        in_specs=[pl.BlockSpec(memory_space=pltpu.SMEM),
                  pl.BlockSpec((1, A_BLOCK, 2 * A_BLOCK), lambda c: (c, 0, 0)),
                  pl.BlockSpec((1, 2, A_BLOCK, 2 * A_BLOCK), lambda c: (c, 0, 0, 0))],
        out_specs=pl.BlockSpec((1, 2, A_HEADS, A_BLOCK, 2 * A_BLOCK), lambda c: (c, 0, 0, 0, 0)),
        compiler_params=_cparams(("arbitrary",)),
        name="attn_bias",
    )(rel_bias.astype(F32), jnp.asarray(np.stack(buckets)), jnp.asarray(np.stack(masks)))


def _attn_block(q, kcat, vcat, bias_ref):
    lane = lax.broadcasted_iota(jnp.int32, (A_BLOCK, LANES), 1)
    low = lane < A_HEAD_DIM
    lse_tile = jnp.zeros((A_BLOCK, LANES), F32)
    zero = jnp.zeros((A_BLOCK, LANES), BF16)
    heads = range(A_HEADS)
    pair = [slice((h // 2) * LANES, (h // 2 + 1) * LANES) for h in heads]
    scores = []
    for h in heads:
        qm = jnp.where(low if h % 2 == 0 else jnp.logical_not(low), q[:, pair[h]], zero)
        scores.append(lax.dot_general(qm, kcat[:, pair[h]], _NT, preferred_element_type=F32) + bias_ref[0, h])
    probs, inv_l = [], []
    for h in heads:
        m = jnp.max(scores[h], axis=-1, keepdims=True)
        p = jnp.exp2(scores[h] - m)
        l = jnp.sum(p, axis=-1, keepdims=True)
        probs.append(p.astype(BF16))
        inv_l.append(1.0 / l)
        lse_tile = jnp.where(lane == h, m + jnp.log2(l), lse_tile)
    outs = [jnp.dot(probs[h], vcat[:, pair[h]], preferred_element_type=F32) * inv_l[h] for h in heads]
    o = jnp.concatenate([jnp.where(low, outs[2 * hp], outs[2 * hp + 1]) for hp in range(A_HEADS // 2)], axis=1)
    return o, lse_tile


def _attn_kernel(q_ref, kp_ref, kc_ref, vp_ref, vc_ref, bias0_ref, bias_ref, o_ref, lse_ref, *, per_lead):
    def sub(ref, rr, i):
        if ref.ndim == 2:
            return ref[i * A_BLOCK:(i + 1) * A_BLOCK, :]
        run = ref.shape[3] // per_lead
        return ref[i // per_lead, :, rr, (i % per_lead) * run:(i % per_lead + 1) * run, :]

    def put(ref, rr, i, val):
        if ref.ndim == 2:
            ref[i * A_BLOCK:(i + 1) * A_BLOCK, :] = val.astype(ref.dtype)
        else:
            run = ref.shape[3] // per_lead
            ref[i // per_lead, :, rr, (i % per_lead) * run:(i % per_lead + 1) * run, :] = (
                val.astype(ref.dtype).reshape(ref.shape[1], run, ref.shape[4]))

    flat = lambda v: v.reshape(A_BLOCK, v.shape[-1])
    token_order = q_ref.ndim == 2
    n_blocks = q_ref.shape[0] // A_BLOCK if token_order else q_ref.shape[0] * per_lead
    for rr in range(1 if token_order else q_ref.shape[2]):
        k_prev = flat(kp_ref[...] if token_order else kp_ref[0, :, rr])
        v_prev = flat(vp_ref[...] if token_order else vp_ref[0, :, rr])
        for i in range(n_blocks):
            k_cur, v_cur = flat(sub(kc_ref, rr, i)), flat(sub(vc_ref, rr, i))
            o, lse_tile = _attn_block(flat(sub(q_ref, rr, i)), jnp.concatenate([k_prev, k_cur], axis=0),
                                      jnp.concatenate([v_prev, v_cur], axis=0),
                                      bias0_ref if i == 0 else bias_ref)
            put(o_ref, rr, i, o)
            put(lse_ref, rr, i, lse_tile)
            k_prev, v_prev = k_cur, v_cur


def _dilated_attention_one(q, k, v, bias_tab, bsz, seq, dil):
    nb = seq // dil // A_BLOCK
    step = min(ATTN_BLOCKS_PER_STEP, nb)
    w = A_WIDTH
    if dil == 1:
        per_lead = n_res = 1
        view = lambda a: a.reshape(bsz, seq, a.shape[-1])
        block = lambda wd: (None, step * A_BLOCK, wd)
        block_prev = lambda wd: (None, A_BLOCK, wd)
        cur = lambda b, r, m: (b, m, 0)
        prev = lambda b, r, m: (b, jnp.maximum(m * step - 1, 0), 0)
        out_shape = lambda wd, dt: jax.ShapeDtypeStruct((bsz, seq, wd), dt)
    else:
        runs = SPAN_RES // dil
        run = A_BLOCK // runs
        spans_per_seq = seq // SPAN
        per_lead = min(step, runs)
        lead = step // per_lead
        n_res = max(1, ATTN_BLOCKS_PER_STEP // step)
        assert runs % per_lead == 0 and spans_per_seq % lead == 0 and dil % n_res == 0
        view = lambda a: a.reshape(a.shape[0], runs, dil, A_BLOCK, a.shape[-1])
        block = lambda wd: (lead, runs, n_res, per_lead * run, wd)
        block_prev = lambda wd: (1, runs, n_res, run, wd)
        cur = lambda b, r, m: ((b * spans_per_seq + (m * step) // runs) // lead, 0, r, ((m * step) % runs) // per_lead, 0)

        def prev(b, r, m):
            n = jnp.maximum(m * step - 1, 0)
            return (b * spans_per_seq + n // runs, 0, r, n % runs, 0)

        out_shape = lambda wd, dt: jax.ShapeDtypeStruct((bsz * spans_per_seq, runs, dil, A_BLOCK, wd), dt)
    blk = pl.BlockSpec(block(w), cur)
    blk_prev = pl.BlockSpec(block_prev(w), prev)
    bias_block = (1, A_HEADS, A_BLOCK, 2 * A_BLOCK)
    o, lse = pl.pallas_call(
        functools.partial(_attn_kernel, per_lead=per_lead),
        out_shape=[out_shape(w, BF16), out_shape(LANES, F32)],
        grid=(bsz, dil // n_res, nb // step),
        in_specs=[blk, blk_prev, blk, blk_prev, blk,
                  pl.BlockSpec(bias_block, lambda b, r, m: (jnp.minimum(m, 1), 0, 0, 0)),
                  pl.BlockSpec(bias_block, lambda b, r, m: (1, 0, 0, 0))],
        out_specs=[pl.BlockSpec(block(w), cur), pl.BlockSpec(block(LANES), cur)],
        compiler_params=_cparams(("arbitrary", "arbitrary", "arbitrary")),
        name=f"dilated_attn_d{dil}",
    )(view(q), view(k), view(k), view(v), view(v), bias_tab, bias_tab)
    if dil == 1:
        return o.reshape(bsz * seq, w), lse.reshape(bsz * seq, LANES)
    return o.reshape(o.shape[0], SPAN_RES, A_BLOCK, w), lse.reshape(o.shape[0], SPAN_RES, A_BLOCK, LANES)


def _gelu_tanh(x):
    c = math.sqrt(2.0 / math.pi)
    return x * (0.5 * (1.0 + jnp.tanh(c * (x + 0.044715 * (x * x * x)))))


def _s5_kernel(u_ref, pm_ref, pmt_ref, bd_ref, pw_ref, pw16_ref, cd_ref, dsk_ref, gw_ref, gb_ref, o_ref,
               x_scr, st_scr, *, n_state):
    @pl.when(pl.program_id(1) == 0)
    def _():
        st_scr[...] = jnp.zeros_like(st_scr)

    for tile_idx in range(S5_TILES_PER_STEP):
        _s5_tile(tile_idx, u_ref, pm_ref, pmt_ref, bd_ref, pw_ref, pw16_ref, cd_ref, dsk_ref, gw_ref, gb_ref,
                 o_ref, x_scr, st_scr, n_state)


def _s5_tile(tile_idx, u_ref, pm_ref, pmt_ref, bd_ref, pw_ref, pw16_ref, cd_ref, dsk_ref, gw_ref, gb_ref,
             o_ref, x_scr, st_scr, n_state):
    n = n_state
    jn = S5_J
    tile_rows = slice(tile_idx * SUBLANES * jn, (tile_idx + 1) * SUBLANES * jn)

    u = u_ref[tile_rows, :]
    u_perm = jnp.dot(pm_ref[...], u.astype(BF16), preferred_element_type=F32).astype(BF16)
    cw = u.shape[1] * S5_LW // n
    chunks = [(slice(c0, c0 + S5_LW), slice(n + c0, n + c0 + S5_LW)) for c0 in range(0, n, S5_LW)]

    def project_in(q):
        bu = jnp.dot(u_perm[:, q * cw:(q + 1) * cw], bd_ref[q], preferred_element_type=F32)
        x_scr[:, chunks[q][0]] = bu[:, 0:S5_LW]
        x_scr[:, chunks[q][1]] = bu[:, S5_LW:]

    project_in(0)
    end_r, end_i = [], []
    for q, (re, im) in enumerate(chunks):
        if q + 1 < len(chunks):
            project_in(q + 1)
        a1r, a1i = pw_ref[0:SUBLANES, re], pw_ref[0:SUBLANES, im]
        xr = jnp.zeros((SUBLANES, S5_LW), F32)
        xi = jnp.zeros((SUBLANES, S5_LW), F32)
        for j in range(jn):
            rows = slice(SUBLANES * j, SUBLANES * (j + 1))
            nr = a1r * xr - a1i * xi + x_scr[rows, re]
            ni = a1r * xi + a1i * xr + x_scr[rows, im]
            xr, xi = nr, ni
            x_scr[rows, re] = xr
            x_scr[rows, im] = xi
        end_r.append(xr)
        end_i.append(xi)
    er, ei = jnp.concatenate(end_r, axis=1), jnp.concatenate(end_i, axis=1)
    last = SUBLANES * (jn - 1)
    ajr, aji = pw_ref[last:last + 1, 0:n], pw_ref[last:last + 1, n:]
    pr, pi = st_scr[0:1, 0:n], st_scr[0:1, n:]
    cin_r, cin_i = [], []
    for s in range(SUBLANES):
        cin_r.append(pr)
        cin_i.append(pi)
        nr = er[s:s + 1] + ajr * pr - aji * pi
        ni = ei[s:s + 1] + ajr * pi + aji * pr
        pr, pi = nr, ni
    st_scr[0:1, 0:n] = pr
    st_scr[0:1, n:] = pi
    cr_all = jnp.concatenate(cin_r, axis=0)
    ci_all = jnp.concatenate(cin_i, axis=0)
    pack = 2 * SUBLANES
    ys = []
    for q, (re, im) in enumerate(chunks):
        cr = jnp.concatenate([cr_all[:, re]] * 2, axis=0).astype(BF16)
        ci = jnp.concatenate([ci_all[:, re]] * 2, axis=0).astype(BF16)
        xr_parts, xi_parts = [], []
        for m in range(SUBLANES * jn // pack):
            rows = slice(pack * m, pack * (m + 1))
            pjr, pji = pw16_ref[rows, re], pw16_ref[rows, im]
            xr_parts.append(x_scr[rows, re].astype(BF16) + (pjr * cr - pji * ci))
            xi_parts.append(x_scr[rows, im].astype(BF16) + (pjr * ci + pji * cr))
        xcat = jnp.concatenate([jnp.concatenate(xr_parts, axis=0), jnp.concatenate(xi_parts, axis=0)], axis=1)
        ys.append(jnp.dot(xcat, cd_ref[q], preferred_element_type=F32))

    y_perm = jnp.concatenate(ys, axis=1)
    y_hi = y_perm.astype(BF16)
    y_lo = (y_perm - y_hi.astype(F32)).astype(BF16)
    y = (jnp.dot(pmt_ref[...], y_hi, preferred_element_type=F32)
         + jnp.dot(pmt_ref[...], y_lo, preferred_element_type=F32))
    y = _gelu_tanh(y + dsk_ref[...] * u)
    z = jnp.dot(y.astype(BF16), gw_ref[...], preferred_element_type=F32) + gb_ref[...]
    o_ref[tile_rows, :] = (y * _sigmoid(z)).astype(o_ref.dtype)


def _s5_tables(a_re, a_im, log_dt, b_re, b_im, c_re, c_im):
    g, p = a_re.shape
    dt = jnp.exp(log_dt.astype(F32))[:, None]
    ar, ai = a_re.astype(F32), a_im.astype(F32)
    mag = jnp.exp(dt * ar)
    abar_r, abar_i = mag * jnp.cos(dt * ai), mag * jnp.sin(dt * ai)
    den = ar * ar + ai * ai
    fr = ((abar_r - 1.0) * ar + abar_i * ai) / den
    fi = (abar_i * ar - (abar_r - 1.0) * ai) / den
    br, bi = b_re.astype(F32), b_im.astype(F32)
    bbar_r = fr[..., None] * br - fi[..., None] * bi
    bbar_i = fr[..., None] * bi + fi[..., None] * br
    m = br.shape[-1]
    gc = S5_LW // p
    nq = g // gc
    eye = jnp.eye(gc, dtype=F32)
    dense_b = lambda t: jnp.einsum('qgpm,gh->qgmhp', t.reshape(nq, gc, p, m), eye).reshape(nq, gc * m, gc * p)
    bd = jnp.concatenate([dense_b(bbar_r), dense_b(bbar_i)], axis=2)
    dense_c = lambda t: jnp.einsum('qgmp,gh->qgphm', t.reshape(nq, gc, m, p), eye).reshape(nq, gc * p, gc * m)
    cd = jnp.concatenate([dense_c(c_re.astype(F32)), -dense_c(c_im.astype(F32))], axis=1)
    kk = jnp.arange(1, S5_J + 1, dtype=F32)[:, None, None]
    pmag = jnp.exp(kk * (dt * ar)[None])
    pw_r = (pmag * jnp.cos(kk * (dt * ai)[None])).reshape(S5_J, g * p)
    pw_i = (pmag * jnp.sin(kk * (dt * ai)[None])).reshape(S5_J, g * p)
    pw = jnp.repeat(jnp.concatenate([pw_r, pw_i], axis=1), SUBLANES, axis=0)
    return bd.astype(BF16), pw, cd.astype(BF16)


def _s5_layer(u, bsz, seq, a_re, a_im, log_dt, b_re, b_im, c_re, c_im, d_skip, glu_w, glu_b):
    t, width = u.shape
    n = a_re.shape[0] * a_re.shape[1]
    bd, pw, cd = _s5_tables(a_re, a_im, log_dt, b_re, b_im, c_re, c_im)
    tile = SUBLANES * S5_J
    step_rows = S5_TILES_PER_STEP * tile
    steps_per_seq = seq // step_rows
    const = lambda b, i: (0, 0)
    perm = _sublane_major_perm(tile)
    return pl.pallas_call(
        functools.partial(_s5_kernel, n_state=n),
        out_shape=jax.ShapeDtypeStruct((t, width), BF16),
        grid=(bsz, steps_per_seq),
        in_specs=[pl.BlockSpec((step_rows, width), lambda b, i: (b * steps_per_seq + i, 0)),
                  pl.BlockSpec((tile, tile), const),
                  pl.BlockSpec((tile, tile), const),
                  pl.BlockSpec(bd.shape, lambda b, i: (0, 0, 0)),
                  pl.BlockSpec((tile, 2 * n), const),
                  pl.BlockSpec((tile, 2 * n), const),
                  pl.BlockSpec(cd.shape, lambda b, i: (0, 0, 0)),
                  pl.BlockSpec((1, width), const),
                  pl.BlockSpec((width, width), const),
                  pl.BlockSpec((1, width), const)],
        out_specs=pl.BlockSpec((step_rows, width), lambda b, i: (b * steps_per_seq + i, 0)),
        scratch_shapes=[pltpu.VMEM((tile, 2 * n), F32), pltpu.VMEM((SUBLANES, 2 * n), F32)],
        compiler_params=_cparams(("arbitrary", "arbitrary")),
        name="s5_layer",
    )(u, jnp.asarray(perm, BF16), jnp.asarray(perm.T, BF16), bd, pw, pw.astype(BF16), cd,
      d_skip.reshape(1, width).astype(F32), glu_w.astype(BF16),
      glu_b.reshape(1, width).astype(F32))


def _finish(y, x_ref, mod_ref, pg_ref, out_ref, d_model):
    ms = jnp.mean(y * y, axis=-1, keepdims=True)
    yn = y * lax.rsqrt(ms + EPS) * pg_ref[...]
    gate_mod = mod_ref[0, :, 2 * d_model:3 * d_model]
    out_ref[...] = x_ref[...] + gate_mod * yn


def _split2(x):
    hi = x.astype(BF16)
    return hi, (x - hi.astype(F32)).astype(BF16)


def _out_ab_kernel(o0_ref, l0_ref, o1_ref, o2_ref, l1_ref, l2_ref, ob_ref, gate_ref, e_ref, un_ref, w_ref,
                   x_ref, mod_ref, pg_ref, out_ref, *, d_model):
    aw = o0_ref.shape[-1]

    def unperm(o_ref, l_ref):
        pieces = []
        for piece in range(ROW_TILE // PERM_ROWS):
            runs = slice(piece * PERM_RUN, (piece + 1) * PERM_RUN)
            o_p = o_ref[:, runs, :].reshape(PERM_ROWS, aw)
            l_p = l_ref[:, runs, :].reshape(PERM_ROWS, LANES)
            parts = jnp.concatenate([o_p] + list(_split2(l_p)), axis=1)
            pieces.append(jnp.dot(un_ref[...], parts, preferred_element_type=F32))
        moved = jnp.concatenate(pieces, axis=0)
        return moved[:, 0:aw], moved[:, aw:aw + LANES] + moved[:, aw + LANES:]

    o0, l0 = o0_ref[...].astype(F32), l0_ref[...]
    o1, l1 = unperm(o1_ref, l1_ref)
    o2, l2 = unperm(o2_ref, l2_ref)
    mx = jnp.maximum(jnp.maximum(l0, l1), l2)
    e0, e1, e2 = jnp.exp2(l0 - mx), jnp.exp2(l1 - mx), jnp.exp2(l2 - mx)
    inv_den = 1.0 / (e0 + e1 + e2)
    stacked = jnp.concatenate(list(_split2(e1 * inv_den)) + list(_split2(e2 * inv_den)), axis=0)
    wide = jnp.dot(stacked, e_ref[...], preferred_element_type=F32)
    term = lambda i: wide[i * ROW_TILE:(i + 1) * ROW_TILE]
    w1 = term(0) + term(1)
    w2 = term(2) + term(3)
    o_a = (1.0 - w1 - w2) * o0 + w1 * o1 + w2 * o2
    gate = gate_ref[...].astype(F32)
    sg = gate * _sigmoid(gate)
    aw = o_a.shape[-1]
    y = jnp.dot((o_a * sg[:, :aw]).astype(BF16), w_ref[0:aw, :], preferred_element_type=F32)
    y = y + jnp.dot((ob_ref[...].astype(F32) * sg[:, aw:]).astype(BF16), w_ref[aw:, :],
                    preferred_element_type=F32)
    _finish(y, x_ref, mod_ref, pg_ref, out_ref, d_model)


def _out_c_kernel(o_ref, gate_ref, w_ref, x_ref, mod_ref, pg_ref, out_ref, *, d_model):
    gate = gate_ref[...].astype(F32)
    o = o_ref[...].astype(F32) * (gate * _sigmoid(gate))
    y = jnp.dot(o.astype(BF16), w_ref[...], preferred_element_type=F32)
    _finish(y, x_ref, mod_ref, pg_ref, out_ref, d_model)


def _out_proj(kern, row_inputs, const_inputs, w_bf16, x2, mod3, post_g, seq, name, tile=ROW_TILE):
    t, d = x2.shape
    tiles_per_seq = seq // tile
    tiles_per_span = SPAN // tile

    def row_spec(a):
        if a.ndim == 2:
            return pl.BlockSpec((tile, a.shape[1]), lambda i: (i, 0))
        assert tile == ROW_TILE
        return pl.BlockSpec((None, SPAN_RES, SPAN_RUN, a.shape[-1]),
                            lambda i: (i // tiles_per_span, 0, i % tiles_per_span, 0))

    const_spec = lambda a: pl.BlockSpec(a.shape, lambda i: (0, 0))
    return pl.pallas_call(
        functools.partial(kern, d_model=d),
        out_shape=jax.ShapeDtypeStruct((t, d), F32),
        grid=(t // tile,),
        in_specs=([row_spec(a) for a in row_inputs] + [const_spec(a) for a in const_inputs]
                  + [const_spec(w_bf16), row_spec(x2),
                     pl.BlockSpec((1, 1, 3 * d), lambda i: (i // tiles_per_seq, 0, 0)),
                     pl.BlockSpec((1, d), lambda i: (0, 0))]),
        out_specs=pl.BlockSpec((tile, d), lambda i: (i, 0)),
        compiler_params=_cparams(("arbitrary",)),
        name=name,
    )(*row_inputs, *const_inputs, w_bf16, x2, mod3, post_g.reshape(1, d))


def _gdn_prep_kernel(x_ref, halo_ref, br_ref, cw_ref, alog_ref, dtb_ref,
                     w_ref, u_ref, qg_ref, kdt_ref, aqk_ref, dec_ref, xs_scr):
    n_all = GDN_PREP_BLOCKS * GDN_STEP
    first = pl.program_id(1) == 0
    halo = halo_ref[...]
    xs_scr[0:SUBLANES, :] = jnp.where(first, jnp.zeros_like(halo), halo)
    xs_scr[SUBLANES:SUBLANES + n_all, :] = x_ref[...]
    cw = cw_ref[...]
    ext = xs_scr[...].reshape(n_all // SUBLANES + 1, SUBLANES, x_ref.shape[1])
    first_sublane = lax.broadcasted_iota(jnp.int32, ext.shape, 1) == 0

    def delay(y):
        rot = pltpu.roll(y, 1, axis=1)
        prev = jnp.concatenate([rot[:1], rot[:-1]], axis=0)
        return jnp.where(first_sublane, prev, rot)

    acc = ext * cw[0:1]
    for j in range(1, C_CONV):
        acc = delay(acc) + ext * cw[j:j + 1]
    conv = acc[1:].reshape(n_all, x_ref.shape[1])
    act = conv * _sigmoid(conv)
    _gdn_prep_blocks(act, br_ref, alog_ref, dtb_ref, w_ref, u_ref, qg_ref, kdt_ref, aqk_ref, dec_ref)


def _gdn_prep_blocks(act, br_ref, alog_ref, dtb_ref, w_ref, u_ref, qg_ref, kdt_ref, aqk_ref, dec_ref):
    n = GDN_STEP
    c = GDN_CHUNK
    row = lax.broadcasted_iota(jnp.int32, (n, n), 0)
    col = lax.broadcasted_iota(jnp.int32, (n, n), 1)
    cum = jnp.logical_and((row // c) == (col // c), row >= col).astype(F32)
    prow = lax.broadcasted_iota(jnp.int32, (c, n), 0)
    plane = lax.broadcasted_iota(jnp.int32, (c, n), 1)
    left = plane < c
    pcol = jnp.where(left, plane, plane - c)
    tril = prow >= pcol
    strict = prow > pcol
    eye = (prow == pcol).astype(F32)
    pair = lambda full: jnp.where(left, full[0:c], full[c:])

    def blockdiag(p):
        zero = jnp.zeros_like(p)
        return jnp.concatenate([jnp.where(left, p, zero), jnp.where(left, zero, p)], axis=0)

    first_chunk = lax.broadcasted_iota(jnp.int32, (n, 1), 0) < GDN_CHUNK
    dec_row = lax.broadcasted_iota(jnp.int32, (SUBLANES, LANES), 0)
    heads = range(C_HEADS)
    lanes = [slice(h * LANES, (h + 1) * LANES) for h in heads]

    def gates(blk):
        rows = slice(blk * n, (blk + 1) * n)
        br = br_ref[rows, :]
        xg = br + dtb_ref[...]
        softplus = jnp.maximum(xg, 0.0) + jnp.log(1.0 + jnp.exp(-jnp.abs(xg)))
        g_all = -jnp.exp(alog_ref[...]) * softplus
        gc_all = jnp.dot(cum, g_all, preferred_element_type=F32, precision=HI)
        return dict(rows=rows, beta=_sigmoid(br), gc=gc_all, gc_t=gc_all.T)

    def first_stage(ctx, blk, h):
        rows = ctx["rows"]
        gcol = ctx["gc"][:, C_HEADS + h:C_HEADS + h + 1]
        grow = ctx["gc_t"][C_HEADS + h:C_HEADS + h + 1, :]
        eg = jnp.exp(gcol)
        gcol_p = jnp.where(left, gcol[0:c], gcol[c:])
        decay = jnp.where(tril, jnp.exp(jnp.where(tril, gcol_p - grow, 0.0)), 0.0)
        q = act[rows, h * C_DK:(h + 1) * C_DK]
        k = act[rows, (C_HEADS + h) * C_DK:(C_HEADS + h + 1) * C_DK]
        v = act[rows, 2 * C_HEADS * C_DK + h * C_DV:2 * C_HEADS * C_DK + (h + 1) * C_DV]
        q = q * lax.rsqrt(jnp.sum(q * q, axis=-1, keepdims=True) + EPS) * (C_DK ** -0.5)
        k = k * lax.rsqrt(jnp.sum(k * k, axis=-1, keepdims=True) + EPS)
        beta = ctx["beta"][:, h:h + 1]
        kb = k * beta
        k16 = k.astype(BF16)
        a = jnp.where(strict, pair(lax.dot_general(kb.astype(BF16), k16, _NT, preferred_element_type=F32)) * decay,
                      0.0)
        aqk = pair(lax.dot_general(q.astype(BF16), k16, _NT, preferred_element_type=F32)) * decay
        aqk_ref[rows, lanes[h]] = blockdiag(aqk).astype(BF16)
        qg_ref[rows, lanes[h]] = (q * eg).astype(BF16)
        g_last = jnp.where(first_chunk, gcol[GDN_CHUNK - 1:GDN_CHUNK], gcol[n - 1:n])
        kdt_ref[rows, lanes[h]] = (k * jnp.exp(g_last - gcol)).T.astype(BF16)
        dec_ref[blk, :, lanes[h]] = jnp.where(dec_row < SUBLANES // 2, eg[GDN_CHUNK - 1:GDN_CHUNK], eg[n - 1:n])
        rhs = jnp.concatenate([v * beta, kb * eg], axis=1).astype(BF16)
        return dict(a=a, inv=eye - a, rhs=rhs)

    n_blocks = act.shape[0] // n
    first_ctx = gates(0)
    state = [first_stage(first_ctx, 0, h) for h in heads]
    for blk in range(n_blocks):
        nxt_ctx = gates(blk + 1) if blk + 1 < n_blocks else None
        nxt = []

        def side_work(hs, blk=blk, nxt=nxt, nxt_ctx=nxt_ctx):
            if nxt_ctx is not None:
                nxt.extend(first_stage(nxt_ctx, blk + 1, h) for h in hs)

        invs = [st["inv"] for st in state]
        pows = []
        for st in state:
            pows.append(jnp.dot(st["a"].astype(BF16), blockdiag(st["a"]).astype(BF16), preferred_element_type=F32))
        side_work(heads[0:2])
        for level in range(4):
            for h in heads:
                both = jnp.dot(jnp.concatenate([invs[h], pows[h]], axis=0).astype(BF16),
                               blockdiag(pows[h]).astype(BF16), preferred_element_type=F32)
                invs[h] = invs[h] + both[0:c]
                pows[h] = both[c:]
            side_work(heads[2 + level:3 + level])
        for h in heads:
            invs[h] = invs[h] + jnp.dot(invs[h].astype(BF16), blockdiag(pows[h]).astype(BF16),
                                        preferred_element_type=F32)
        side_work(heads[6:8])
        rows = slice(blk * n, (blk + 1) * n)
        for h in heads:
            uw = jnp.dot(blockdiag(invs[h]).astype(BF16), state[h]["rhs"], preferred_element_type=F32)
            u_ref[rows, lanes[h]] = uw[:, 0:C_DV].astype(u_ref.dtype)
            w_ref[rows, lanes[h]] = uw[:, C_DV:].astype(BF16)
        state = nxt


def _gdn_rec_kernel(w_ref, u_ref, qg_ref, kdt_ref, aqk_ref, dec_ref, ng_ref, o_ref, s_scr):
    @pl.when(pl.program_id(0) == 0)
    def _():
        s_scr[...] = jnp.zeros_like(s_scr)

    half = SUBLANES // 2
    zeros = jnp.zeros((GDN_CHUNK, C_DV), BF16)
    lanes = [slice(h * LANES, (h + 1) * LANES) for h in range(C_HEADS)]
    chains = [(b, h) for b in range(w_ref.shape[0]) for h in range(C_HEADS)]
    states = [s_scr[b * C_HEADS + h] for b, h in chains]
    for blk in range(GDN_REC_BLOCKS):
        blk_rows = slice(blk * GDN_STEP, (blk + 1) * GDN_STEP)
        for j in range(GDN_STEP // GDN_CHUNK):
            rows = slice(blk * GDN_STEP + j * GDN_CHUNK, blk * GDN_STEP + (j + 1) * GDN_CHUNK)
            wss = []
            for c, (b, h) in enumerate(chains):
                wq = jnp.concatenate([w_ref[b, rows, lanes[h]], qg_ref[b, rows, lanes[h]]], axis=0)
                wss.append(jnp.dot(wq, states[c].astype(BF16), preferred_element_type=F32))
            for c, (b, h) in enumerate(chains):
                v_new = (u_ref[b, rows, lanes[h]].astype(F32) - wss[c][0:GDN_CHUNK]).astype(BF16)
                v_pad = jnp.concatenate([v_new, zeros] if j == 0 else [zeros, v_new], axis=0)
                both = jnp.dot(jnp.concatenate([aqk_ref[b, rows, lanes[h]], kdt_ref[b, blk_rows, lanes[h]]], axis=0),
                               v_pad, preferred_element_type=F32)
                o = wss[c][GDN_CHUNK:] + both[0:GDN_CHUNK]
                dec = dec_ref[b, blk, j * half:j * half + 1, lanes[h]]
                states[c] = states[c] * dec + both[GDN_CHUNK:]
                ms = jnp.mean(o * o, axis=-1, keepdims=True)
                o_ref[b, rows, lanes[h]] = (o * lax.rsqrt(ms + EPS) * ng_ref[...]).astype(o_ref.dtype)
    for c, (b, h) in enumerate(chains):
        s_scr[b * C_HEADS + h] = states[c]


def _gdn_core(qkv_pre, br, bsz, seq, conv_w, a_log, dt_bias, norm_g):
    t, qkv_w = qkv_pre.shape
    hw = C_HEADS * C_DV
    steps_per_seq = seq // GDN_STEP
    prep_rows = GDN_PREP_BLOCKS * GDN_STEP
    prep_steps = steps_per_seq // GDN_PREP_BLOCKS
    halo_blocks = prep_rows // SUBLANES
    pad_row = lambda vec: jnp.zeros((1, LANES), F32).at[0, C_HEADS:2 * C_HEADS].set(vec.astype(F32))
    tok = lambda b, i: (b * prep_steps + i, 0)
    const = lambda b, i: (0, 0)
    tok_spec = pl.BlockSpec((prep_rows, hw), tok)
    dec_spec = pl.BlockSpec((GDN_PREP_BLOCKS, SUBLANES, hw), lambda b, i: (b * prep_steps + i, 0, 0))
    w, u, qg, kdt, aqk, dec = pl.pallas_call(
        _gdn_prep_kernel,
        out_shape=[jax.ShapeDtypeStruct((t, hw), BF16), jax.ShapeDtypeStruct((t, hw), BF16),
                   jax.ShapeDtypeStruct((t, hw), BF16), jax.ShapeDtypeStruct((t, hw), BF16),
                   jax.ShapeDtypeStruct((t, hw), BF16),
                   jax.ShapeDtypeStruct((t // GDN_STEP, SUBLANES, hw), F32)],
        grid=(bsz, prep_steps),
        in_specs=[pl.BlockSpec((prep_rows, qkv_w), tok),
                  pl.BlockSpec((SUBLANES, qkv_w),
                               lambda b, i: (jnp.maximum((b * prep_steps + i) * halo_blocks - 1, 0), 0)),
                  pl.BlockSpec((prep_rows, LANES), tok),
                  pl.BlockSpec((C_CONV, qkv_w), const),
                  pl.BlockSpec((1, LANES), const),
                  pl.BlockSpec((1, LANES), const)],
        out_specs=[tok_spec, tok_spec, tok_spec, tok_spec, tok_spec, dec_spec],
        scratch_shapes=[pltpu.VMEM((SUBLANES + prep_rows, qkv_w), F32)],
        compiler_params=_cparams(("arbitrary", "arbitrary")),
        name="gdn_prep",
    )(qkv_pre, qkv_pre, br, conv_w.astype(F32), pad_row(a_log), pad_row(dt_bias))
    rec_steps = steps_per_seq // GDN_REC_BLOCKS
    seq_view = lambda a: a.reshape(bsz, seq, hw)
    rec_tok = pl.BlockSpec((bsz, GDN_REC_BLOCKS * GDN_STEP, hw), lambda i: (0, i, 0))
    rec_dec = pl.BlockSpec((bsz, GDN_REC_BLOCKS, SUBLANES, hw), lambda i: (0, i, 0, 0))
    o = pl.pallas_call(
        _gdn_rec_kernel,
        out_shape=jax.ShapeDtypeStruct((bsz, seq, hw), BF16),
        grid=(rec_steps,),
        in_specs=[rec_tok, rec_tok, rec_tok, rec_tok, rec_tok, rec_dec,
                  pl.BlockSpec((1, C_DV), lambda i: (0, 0))],
        out_specs=rec_tok,
        scratch_shapes=[pltpu.VMEM((bsz * C_HEADS, C_DK, C_DV), F32)],
        compiler_params=_cparams(("arbitrary",)),
        name="gdn_recurrence",
    )(seq_view(w), seq_view(u), seq_view(qg), seq_view(kdt), seq_view(aqk),
      dec.reshape(bsz, steps_per_seq, SUBLANES, hw), norm_g.reshape(1, C_DV).astype(F32))
    return o.reshape(t, hw)


def _ab_layer(x2, mod3, bsz, seq, pre_g, post_g, rel_bias, w_in, w_out, s5_params):
    b_width = s5_params[7].shape[-1]
    assert DILATED_CONFIGS[0][1] == 1 and all(SPAN_RES % dl == 0 for _, dl in DILATED_CONFIGS) and seq % SPAN == 0
    head_expand = jnp.asarray(np.arange(LANES)[:, None] == (np.arange(A_WIDTH)[None, :] // A_HEAD_DIM), BF16)
    unperm = jnp.asarray(_span_perm().T, BF16)
    splits = ((0, A_WIDTH, A_HEAD_DIM ** -0.5 * LOG2_E), (A_WIDTH, A_WIDTH, 1.0), (2 * A_WIDTH, A_WIDTH, 1.0),
              (3 * A_WIDTH, b_width, 1.0), (3 * A_WIDTH + b_width, A_WIDTH + b_width, 1.0))
    q, k, v, u, gate, qs, ks, vs = _in_proj(x2, mod3, pre_g, w_in.astype(BF16), splits,
                                            (BF16, BF16, BF16, F32, BF16), seq, n_span=3)
    outs, lses = [], []
    bias_tabs = _attn_bias_tables(rel_bias)
    for cfg, (_, dil) in enumerate(DILATED_CONFIGS):
        qkv = (q, k, v) if dil == 1 else (qs, ks, vs)
        o_c, lse_c = _dilated_attention_one(*qkv, bias_tabs[cfg], bsz, seq, dil)
        outs.append(o_c)
        lses.append(lse_c)
    o_b = _s5_layer(u, bsz, seq, *s5_params)
    row_inputs = [outs[0], lses[0], outs[1], outs[2], lses[1], lses[2], o_b, gate]
    return _out_proj(_out_ab_kernel, row_inputs, [head_expand, unperm], w_out.astype(BF16),
                     x2, mod3, post_g, seq, "out_proj_ab")


def _gdn_layer(x2, mod3, bsz, seq, pre_g, post_g, w_in, conv_w, a_log, dt_bias, norm_g, w_out):
    d = x2.shape[1]
    qkv_w = 2 * C_HEADS * C_DK + C_HEADS * C_DV
    gate_w = C_HEADS * C_DV
    w_pad = jnp.concatenate(
        [w_in, jnp.zeros((d, LANES - (w_in.shape[1] - qkv_w - gate_w)), w_in.dtype)], axis=1).astype(BF16)
    splits = ((0, qkv_w, 1.0), (qkv_w, gate_w, 1.0), (qkv_w + gate_w, LANES, 1.0))
    qkv_pre, gate, br = _in_proj(x2, mod3, pre_g, w_pad, splits, (F32, BF16, F32), seq)
    o = _gdn_core(qkv_pre, br, bsz, seq, conv_w, a_log, dt_bias, norm_g)
    return _out_proj(_out_c_kernel, [o, gate], [], w_out.astype(BF16), x2, mod3, post_g, seq, "out_proj_c",
                     tile=OUT_C_TILE)


def kernel(x, c, ada_w, ada_b, pre_g, post_g, rel_bias, ab_w_in, ab_w_out, s5_a_re, s5_a_im, s5_log_dt, s5_b_re, s5_b_im, s5_c_re, s5_c_im, s5_d, s5_glu_w, s5_glu_b, gdn_w_in, gdn_conv, gdn_a_log, gdn_dt_bias, gdn_norm_g, gdn_w_out):
    bsz, seq, d = x.shape
    depth = ada_w.shape[0]
    assert seq % ROW_TILE == 0 and seq % OUT_C_TILE == 0 and seq % (SUBLANES * S5_J * S5_TILES_PER_STEP) == 0
    assert seq % (GDN_STEP * GDN_PREP_BLOCKS) == 0 and seq % (GDN_STEP * GDN_REC_BLOCKS) == 0
    x2 = x.reshape(bsz * seq, d)
    mod = _adaln_mod(c, ada_w, ada_b)
    for layer in range(depth):
        j = layer // 2
        mod3 = mod[layer].reshape(bsz, 1, 3 * d)
        if layer % 2 == 0:
            s5_params = (s5_a_re[j], s5_a_im[j], s5_log_dt[j], s5_b_re[j], s5_b_im[j], s5_c_re[j], s5_c_im[j],
                         s5_d[j], s5_glu_w[j], s5_glu_b[j])
            x2 = _ab_layer(x2, mod3, bsz, seq, pre_g[layer], post_g[layer], rel_bias, ab_w_in[j], ab_w_out[j],
                           s5_params)
        else:
            x2 = _gdn_layer(x2, mod3, bsz, seq, pre_g[layer], post_g[layer], gdn_w_in[j], gdn_conv[j],
                            gdn_a_log[j], gdn_dt_bias[j], gdn_norm_g[j], gdn_w_out[j])
    return x2.reshape(bsz, seq, d)
```

```python
import functools
import math

import numpy as np
import jax
import jax.numpy as jnp
from jax import lax
from jax.experimental import pallas as pl
from jax.experimental.pallas import tpu as pltpu

F32 = jnp.float32
BF16 = jnp.bfloat16
HI = lax.Precision.HIGHEST

EPS = 1e-6
A_HEADS = 8
A_HEAD_DIM = 64
A_WIDTH = A_HEADS * A_HEAD_DIM
A_BLOCK = 128
DILATED_CONFIGS = ((128, 1), (512, 4), (2048, 16))
REL_BUCKETS = 32
REL_MAX_DIST = 2048
C_HEADS = 8
C_DK = 128
C_DV = 128
C_CONV = 4
MASK_NEG = -1e30
LOG2_E = math.log2(math.e)

LANES = 128
SUBLANES = 8
VMEM_LIMIT = 56 * 1024 * 1024
ROW_TILE = 512
AB_TILE = 1024
OUT_C_TILE = 1024
COL_CHUNK = 512
SPAN_RES = max(dl for _, dl in DILATED_CONFIGS)
SPAN = A_BLOCK * SPAN_RES
PERM_ROWS = 256
PERM_RUN = PERM_ROWS // SPAN_RES
ATTN_BLOCKS_PER_STEP = 8
S5_J = 32
S5_LW = 512
S5_TILES_PER_STEP = 4
GDN_STEP = 128
GDN_CHUNK = 64
GDN_PREP_BLOCKS = 4
GDN_REC_BLOCKS = 2

_NT = (((1,), (1,)), ((), ()))


def _cparams(sem):
    return pltpu.CompilerParams(dimension_semantics=sem, vmem_limit_bytes=VMEM_LIMIT)


def _sigmoid(x):
    return 1.0 / (1.0 + jnp.exp(-x))


def _mod_kernel(c_ref, w_ref, b_ref, o_ref):
    c = c_ref[...]
    ca = c * _sigmoid(c)
    o_ref[0] = jnp.dot(ca.astype(BF16), w_ref[0].astype(BF16), preferred_element_type=F32) + b_ref[0]


def _adaln_mod(c, ada_w, ada_b):
    depth, d, d3 = ada_w.shape
    bsz = c.shape[0]
    return pl.pallas_call(
        _mod_kernel,
        out_shape=jax.ShapeDtypeStruct((depth, bsz, d3), F32),
        grid=(depth, d3 // d),
        in_specs=[pl.BlockSpec((bsz, d), lambda l, j: (0, 0)),
                  pl.BlockSpec((1, d, d), lambda l, j: (l, 0, j)),
                  pl.BlockSpec((1, 1, d), lambda l, j: (l, 0, j))],
        out_specs=pl.BlockSpec((1, bsz, d), lambda l, j: (l, 0, j)),
        compiler_params=_cparams(("arbitrary", "arbitrary")),
        name="adaln_mod",
    )(c, ada_w, ada_b.reshape(depth, 1, d3))


def _span_perm():
    rho = np.arange(PERM_ROWS)
    nat = SPAN_RES * (rho % PERM_RUN) + rho // PERM_RUN
    return (nat[:, None] == np.arange(PERM_ROWS)[None, :]).astype(np.float32)


def _modulated_norm(x_ref, mod_ref, g_ref, d_model):
    x = x_ref[...]
    ms = jnp.mean(x * x, axis=-1, keepdims=True)
    y = x * lax.rsqrt(ms + EPS) * g_ref[...]
    shift = mod_ref[0, :, 0:d_model]
    scale = mod_ref[0, :, d_model:2 * d_model]
    return (y * (1.0 + scale) + shift).astype(BF16)


def _sublane_major_perm(tile):
    rho = np.arange(tile)
    src = (rho % SUBLANES) * (tile // SUBLANES) + rho // SUBLANES
    return (src[:, None] == np.arange(tile)[None, :]).astype(np.float32)


def _in_proj_kernel(x_ref, mod_ref, g_ref, w_ref, *rest, splits, d_model, n_span):
    pm_ref = rest[0] if n_span else None
    out_refs = rest[1:] if n_span else rest
    span_refs = out_refs[len(splits):]
    h = _modulated_norm(x_ref, mod_ref, g_ref, d_model)
    for idx, ((c0, width, mult), o_ref) in enumerate(zip(splits, out_refs)):
        for cc in range(0, width, COL_CHUNK):
            cw = min(COL_CHUNK, width - cc)
            acc = jnp.dot(h, w_ref[:, c0 + cc:c0 + cc + cw], preferred_element_type=F32)
            if mult != 1.0:
                acc = acc * mult
            val = acc.astype(o_ref.dtype)
            o_ref[:, cc:cc + cw] = val
            if idx < n_span:
                for piece in range(x_ref.shape[0] // PERM_ROWS):
                    moved = jnp.dot(pm_ref[...], val[piece * PERM_ROWS:(piece + 1) * PERM_ROWS],
                                    preferred_element_type=F32).astype(BF16)
                    span_refs[idx][:, piece * PERM_RUN:(piece + 1) * PERM_RUN, cc:cc + cw] = (
                        moved.reshape(SPAN_RES, PERM_RUN, cw))


def _in_proj(x2, mod3, gain, w_bf16, splits, out_dtypes, seq, n_span=0, tile=ROW_TILE):
    t, d = x2.shape
    tiles_per_seq = seq // tile
    tiles_per_span = SPAN // tile
    n_w = w_bf16.shape[1]
    row = lambda wd: pl.BlockSpec((tile, wd), lambda i: (i, 0))
    span_shape = lambda wd: jax.ShapeDtypeStruct((t // SPAN, SPAN_RES, A_BLOCK, wd), BF16)
    span_spec = lambda wd: pl.BlockSpec((None, SPAN_RES, tile // SPAN_RES, wd),
                                        lambda i: (i // tiles_per_span, 0, i % tiles_per_span, 0))
    perm_in = [jnp.asarray(_span_perm(), BF16)] if n_span else []
    perm_spec = [pl.BlockSpec((PERM_ROWS, PERM_ROWS), lambda i: (0, 0))] if n_span else []
    return pl.pallas_call(
        functools.partial(_in_proj_kernel, splits=splits, d_model=d, n_span=n_span),
        out_shape=([jax.ShapeDtypeStruct((t, wd), dt) for (_, wd, _), dt in zip(splits, out_dtypes)]
                   + [span_shape(wd) for (_, wd, _) in splits[:n_span]]),
        grid=(t // tile,),
        in_specs=[row(d),
                  pl.BlockSpec((1, 1, 3 * d), lambda i: (i // tiles_per_seq, 0, 0)),
                  pl.BlockSpec((1, d), lambda i: (0, 0)),
                  pl.BlockSpec((d, n_w), lambda i: (0, 0))] + perm_spec,
        out_specs=[row(wd) for (_, wd, _) in splits] + [span_spec(wd) for (_, wd, _) in splits[:n_span]],
        compiler_params=_cparams(("arbitrary",)),
        name="in_proj",
    )(x2, mod3, gain.reshape(1, d), w_bf16, *perm_in)


def _t5_bucket(dist):
    dist = np.maximum(dist, 0)
    max_exact = REL_BUCKETS // 2
    large = max_exact + (np.log(np.maximum(dist, 1) / max_exact)
                         / math.log(REL_MAX_DIST / max_exact) * (REL_BUCKETS - max_exact)).astype(np.int32)
    large = np.minimum(large, REL_BUCKETS - 1)
    return np.where(dist < max_exact, dist, large).astype(np.int32)


def _bias_kernel(rb_ref, bucket_ref, mask_ref, o_ref):
    def body(r, carry):
        rows = pl.ds(pl.multiple_of(r * SUBLANES, SUBLANES), SUBLANES)
        bk = bucket_ref[0, rows, :]
        accs = [jnp.zeros(bk.shape, F32) for _ in range(A_HEADS)]
        for b in range(REL_BUCKETS):
            eq = bk == b
            accs = [jnp.where(eq, rb_ref[b, h], acc) for h, acc in enumerate(accs)]
        for f in range(2):
            keep = mask_ref[0, f, rows, :] != 0
            for h in range(A_HEADS):
                o_ref[0, f, h, rows, :] = jnp.where(keep, accs[h] * LOG2_E, MASK_NEG)
        return carry
    lax.fori_loop(0, A_BLOCK // SUBLANES, body, 0)


def _attn_bias_tables(rel_bias):
    qi = np.arange(A_BLOCK)[:, None]
    kj = np.arange(2 * A_BLOCK)[None, :]
    rel = qi + A_BLOCK - kj
    buckets, masks = [], []
    for window, dil in DILATED_CONFIGS:
        band = (rel >= 0) & (rel <= window // dil)
        bucket = _t5_bucket(rel * dil)
        mask = np.stack([band & (kj >= A_BLOCK), band]).astype(np.int32)
        if dil > 1:
            runs = SPAN_RES // dil
            run = A_BLOCK // runs
            rho = np.arange(A_BLOCK)
            sub = runs * (rho % run) + rho // run
            keys = np.concatenate([sub, A_BLOCK + sub])
            bucket = bucket[sub][:, keys]
            mask = mask[:, sub][:, :, keys]
        buckets.append(bucket)
        masks.append(mask)
    n_cfg = len(DILATED_CONFIGS)
    return pl.pallas_call(
        _bias_kernel,
        out_shape=jax.ShapeDtypeStruct((n_cfg, 2, A_HEADS, A_BLOCK, 2 * A_BLOCK), F32),
        grid=(n_cfg,),
        in_specs=[pl.BlockSpec(memory_space=pltpu.SMEM),
                  pl.BlockSpec((1, A_BLOCK, 2 * A_BLOCK), lambda c: (c, 0, 0)),
                  pl.BlockSpec((1, 2, A_BLOCK, 2 * A_BLOCK), lambda c: (c, 0, 0, 0))],
        out_specs=pl.BlockSpec((1, 2, A_HEADS, A_BLOCK, 2 * A_BLOCK), lambda c: (c, 0, 0, 0, 0)),
        compiler_params=_cparams(("arbitrary",)),
        name="attn_bias",
    )(rel_bias.astype(F32), jnp.asarray(np.stack(buckets)), jnp.asarray(np.stack(masks)))


def _attn_block(q, kcat, vcat, bias_ref):
    lane = lax.broadcasted_iota(jnp.int32, (A_BLOCK, LANES), 1)
    low = lane < A_HEAD_DIM
    lse_tile = jnp.zeros((A_BLOCK, LANES), F32)
    zero = jnp.zeros((A_BLOCK, LANES), BF16)
    heads = range(A_HEADS)
    pair = [slice((h // 2) * LANES, (h // 2 + 1) * LANES) for h in heads]
    scores = []
    for h in heads:
        qm = jnp.where(low if h % 2 == 0 else jnp.logical_not(low), q[:, pair[h]], zero)
        scores.append(lax.dot_general(qm, kcat[:, pair[h]], _NT, preferred_element_type=F32) + bias_ref[0, h])
    probs, inv_l = [], []
    for h in heads:
        m = jnp.max(scores[h], axis=-1, keepdims=True)
        p = jnp.exp2(scores[h] - m)
        l = jnp.sum(p, axis=-1, keepdims=True)
        probs.append(p.astype(BF16))
        inv_l.append(1.0 / l)
        lse_tile = jnp.where(lane == h, m + jnp.log2(l), lse_tile)
    outs = [jnp.dot(probs[h], vcat[:, pair[h]], preferred_element_type=F32) * inv_l[h] for h in heads]
    o = jnp.concatenate([jnp.where(low, outs[2 * hp], outs[2 * hp + 1]) for hp in range(A_HEADS // 2)], axis=1)
    return o, lse_tile


def _attn_kernel(q_ref, kp_ref, kc_ref, vp_ref, vc_ref, bias0_ref, bias_ref, o_ref, lse_ref, *, per_lead):
    def sub(ref, rr, i):
        if ref.ndim == 2:
            return ref[i * A_BLOCK:(i + 1) * A_BLOCK, :]
        run = ref.shape[3] // per_lead
        return ref[i // per_lead, :, rr, (i % per_lead) * run:(i % per_lead + 1) * run, :]

    def put(ref, rr, i, val):
        if ref.ndim == 2:
            ref[i * A_BLOCK:(i + 1) * A_BLOCK, :] = val.astype(ref.dtype)
        else:
            run = ref.shape[3] // per_lead
            ref[i // per_lead, :, rr, (i % per_lead) * run:(i % per_lead + 1) * run, :] = (
                val.astype(ref.dtype).reshape(ref.shape[1], run, ref.shape[4]))

    flat = lambda v: v.reshape(A_BLOCK, v.shape[-1])
    token_order = q_ref.ndim == 2
    n_blocks = q_ref.shape[0] // A_BLOCK if token_order else q_ref.shape[0] * per_lead
    for rr in range(1 if token_order else q_ref.shape[2]):
        k_prev = flat(kp_ref[...] if token_order else kp_ref[0, :, rr])
        v_prev = flat(vp_ref[...] if token_order else vp_ref[0, :, rr])
        for i in range(n_blocks):
            k_cur, v_cur = flat(sub(kc_ref, rr, i)), flat(sub(vc_ref, rr, i))
            o, lse_tile = _attn_block(flat(sub(q_ref, rr, i)), jnp.concatenate([k_prev, k_cur], axis=0),
                                      jnp.concatenate([v_prev, v_cur], axis=0),
                                      bias0_ref if i == 0 else bias_ref)
            put(o_ref, rr, i, o)
            put(lse_ref, rr, i, lse_tile)
            k_prev, v_prev = k_cur, v_cur


def _dilated_attention_one(q, k, v, bias_tab, bsz, seq, dil):
    nb = seq // dil // A_BLOCK
    step = min(ATTN_BLOCKS_PER_STEP, nb)
    w = A_WIDTH
    if dil == 1:
        per_lead = n_res = 1
        view = lambda a: a.reshape(bsz, seq, a.shape[-1])
        block = lambda wd: (None, step * A_BLOCK, wd)
        block_prev = lambda wd: (None, A_BLOCK, wd)
        cur = lambda b, r, m: (b, m, 0)
        prev = lambda b, r, m: (b, jnp.maximum(m * step - 1, 0), 0)
        out_shape = lambda wd, dt: jax.ShapeDtypeStruct((bsz, seq, wd), dt)
    else:
        runs = SPAN_RES // dil
        run = A_BLOCK // runs
        spans_per_seq = seq // SPAN
        per_lead = min(step, runs)
        lead = step // per_lead
        n_res = max(1, ATTN_BLOCKS_PER_STEP // step)
        assert runs % per_lead == 0 and spans_per_seq % lead == 0 and dil % n_res == 0
        view = lambda a: a.reshape(a.shape[0], runs, dil, A_BLOCK, a.shape[-1])
        block = lambda wd: (lead, runs, n_res, per_lead * run, wd)
        block_prev = lambda wd: (1, runs, n_res, run, wd)
        cur = lambda b, r, m: ((b * spans_per_seq + (m * step) // runs) // lead, 0, r, ((m * step) % runs) // per_lead, 0)

        def prev(b, r, m):
            n = jnp.maximum(m * step - 1, 0)
            return (b * spans_per_seq + n // runs, 0, r, n % runs, 0)

        out_shape = lambda wd, dt: jax.ShapeDtypeStruct((bsz * spans_per_seq, runs, dil, A_BLOCK, wd), dt)
    blk = pl.BlockSpec(block(w), cur)
    blk_prev = pl.BlockSpec(block_prev(w), prev)
    bias_block = (1, A_HEADS, A_BLOCK, 2 * A_BLOCK)
    o, lse = pl.pallas_call(
        functools.partial(_attn_kernel, per_lead=per_lead),
        out_shape=[out_shape(w, BF16), out_shape(LANES, F32)],
        grid=(bsz, dil // n_res, nb // step),
        in_specs=[blk, blk_prev, blk, blk_prev, blk,
                  pl.BlockSpec(bias_block, lambda b, r, m: (jnp.minimum(m, 1), 0, 0, 0)),
                  pl.BlockSpec(bias_block, lambda b, r, m: (1, 0, 0, 0))],
        out_specs=[pl.BlockSpec(block(w), cur), pl.BlockSpec(block(LANES), cur)],
        compiler_params=_cparams(("arbitrary", "arbitrary", "arbitrary")),
        name=f"dilated_attn_d{dil}",
    )(view(q), view(k), view(k), view(v), view(v), bias_tab, bias_tab)
    if dil == 1:
        return o.reshape(bsz * seq, w), lse.reshape(bsz * seq, LANES)
    return o.reshape(o.shape[0], SPAN_RES, A_BLOCK, w), lse.reshape(o.shape[0], SPAN_RES, A_BLOCK, LANES)


def _gelu_tanh(x):
    c = math.sqrt(2.0 / math.pi)
    return x * (0.5 * (1.0 + jnp.tanh(c * (x + 0.044715 * (x * x * x)))))


def _s5_kernel(u_ref, pm_ref, pmt_ref, bd_ref, pw_ref, pw16_ref, cd_ref, dsk_ref, gw_ref, gb_ref, o_ref,
               x_scr, st_scr, *, n_state):
    @pl.when(pl.program_id(1) == 0)
    def _():
        st_scr[...] = jnp.zeros_like(st_scr)

    for tile_idx in range(S5_TILES_PER_STEP):
        _s5_tile(tile_idx, u_ref, pm_ref, pmt_ref, bd_ref, pw_ref, pw16_ref, cd_ref, dsk_ref, gw_ref, gb_ref,
                 o_ref, x_scr, st_scr, n_state)


def _s5_tile(tile_idx, u_ref, pm_ref, pmt_ref, bd_ref, pw_ref, pw16_ref, cd_ref, dsk_ref, gw_ref, gb_ref,
             o_ref, x_scr, st_scr, n_state):
    n = n_state
    jn = S5_J
    tile_rows = slice(tile_idx * SUBLANES * jn, (tile_idx + 1) * SUBLANES * jn)

    u = u_ref[tile_rows, :]
    u_perm = jnp.dot(pm_ref[...], u.astype(BF16), preferred_element_type=F32).astype(BF16)
    cw = u.shape[1] * S5_LW // n
    chunks = [(slice(c0, c0 + S5_LW), slice(n + c0, n + c0 + S5_LW)) for c0 in range(0, n, S5_LW)]

    def project_in(q):
        bu = jnp.dot(u_perm[:, q * cw:(q + 1) * cw], bd_ref[q], preferred_element_type=F32)
        x_scr[:, chunks[q][0]] = bu[:, 0:S5_LW]
        x_scr[:, chunks[q][1]] = bu[:, S5_LW:]

    project_in(0)
    end_r, end_i = [], []
    for q, (re, im) in enumerate(chunks):
        if q + 1 < len(chunks):
            project_in(q + 1)
        a1r, a1i = pw_ref[0:SUBLANES, re], pw_ref[0:SUBLANES, im]
        xr = jnp.zeros((SUBLANES, S5_LW), F32)
        xi = jnp.zeros((SUBLANES, S5_LW), F32)
        for j in range(jn):
            rows = slice(SUBLANES * j, SUBLANES * (j + 1))
            nr = a1r * xr - a1i * xi + x_scr[rows, re]
            ni = a1r * xi + a1i * xr + x_scr[rows, im]
            xr, xi = nr, ni
            x_scr[rows, re] = xr
            x_scr[rows, im] = xi
        end_r.append(xr)
        end_i.append(xi)
    er, ei = jnp.concatenate(end_r, axis=1), jnp.concatenate(end_i, axis=1)
    last = SUBLANES * (jn - 1)
    ajr, aji = pw_ref[last:last + 1, 0:n], pw_ref[last:last + 1, n:]
    pr, pi = st_scr[0:1, 0:n], st_scr[0:1, n:]
    cin_r, cin_i = [], []
    for s in range(SUBLANES):
        cin_r.append(pr)
        cin_i.append(pi)
        nr = er[s:s + 1] + ajr * pr - aji * pi
        ni = ei[s:s + 1] + ajr * pi + aji * pr
        pr, pi = nr, ni
    st_scr[0:1, 0:n] = pr
    st_scr[0:1, n:] = pi
    cr_all = jnp.concatenate(cin_r, axis=0)
    ci_all = jnp.concatenate(cin_i, axis=0)
    pack = 2 * SUBLANES
    ys = []
    for q, (re, im) in enumerate(chunks):
        cr = jnp.concatenate([cr_all[:, re]] * 2, axis=0).astype(BF16)
        ci = jnp.concatenate([ci_all[:, re]] * 2, axis=0).astype(BF16)
        xr_parts, xi_parts = [], []
        for m in range(SUBLANES * jn // pack):
            rows = slice(pack * m, pack * (m + 1))
            pjr, pji = pw16_ref[rows, re], pw16_ref[rows, im]
            xr_parts.append(x_scr[rows, re].astype(BF16) + (pjr * cr - pji * ci))
            xi_parts.append(x_scr[rows, im].astype(BF16) + (pjr * ci + pji * cr))
        xcat = jnp.concatenate([jnp.concatenate(xr_parts, axis=0), jnp.concatenate(xi_parts, axis=0)], axis=1)
        ys.append(jnp.dot(xcat, cd_ref[q], preferred_element_type=F32))

    y_perm = jnp.concatenate(ys, axis=1)
    y_hi = y_perm.astype(BF16)
    y_lo = (y_perm - y_hi.astype(F32)).astype(BF16)
    y = (jnp.dot(pmt_ref[...], y_hi, preferred_element_type=F32)
         + jnp.dot(pmt_ref[...], y_lo, preferred_element_type=F32))
    y = _gelu_tanh(y + dsk_ref[...] * u)
    z = jnp.dot(y.astype(BF16), gw_ref[...], preferred_element_type=F32) + gb_ref[...]
    o_ref[tile_rows, :] = (y * _sigmoid(z)).astype(o_ref.dtype)


def _s5_tables(a_re, a_im, log_dt, b_re, b_im, c_re, c_im):
    g, p = a_re.shape
    dt = jnp.exp(log_dt.astype(F32))[:, None]
    ar, ai = a_re.astype(F32), a_im.astype(F32)
    mag = jnp.exp(dt * ar)
    abar_r, abar_i = mag * jnp.cos(dt * ai), mag * jnp.sin(dt * ai)
    den = ar * ar + ai * ai
    fr = ((abar_r - 1.0) * ar + abar_i * ai) / den
    fi = (abar_i * ar - (abar_r - 1.0) * ai) / den
    br, bi = b_re.astype(F32), b_im.astype(F32)
    bbar_r = fr[..., None] * br - fi[..., None] * bi
    bbar_i = fr[..., None] * bi + fi[..., None] * br
    m = br.shape[-1]
    gc = S5_LW // p
    nq = g // gc
    eye = jnp.eye(gc, dtype=F32)
    dense_b = lambda t: jnp.einsum('qgpm,gh->qgmhp', t.reshape(nq, gc, p, m), eye).reshape(nq, gc * m, gc * p)
    bd = jnp.concatenate([dense_b(bbar_r), dense_b(bbar_i)], axis=2)
    dense_c = lambda t: jnp.einsum('qgmp,gh->qgphm', t.reshape(nq, gc, m, p), eye).reshape(nq, gc * p, gc * m)
    cd = jnp.concatenate([dense_c(c_re.astype(F32)), -dense_c(c_im.astype(F32))], axis=1)
    kk = jnp.arange(1, S5_J + 1, dtype=F32)[:, None, None]
    pmag = jnp.exp(kk * (dt * ar)[None])
    pw_r = (pmag * jnp.cos(kk * (dt * ai)[None])).reshape(S5_J, g * p)
    pw_i = (pmag * jnp.sin(kk * (dt * ai)[None])).reshape(S5_J, g * p)
    pw = jnp.repeat(jnp.concatenate([pw_r, pw_i], axis=1), SUBLANES, axis=0)
    return bd.astype(BF16), pw, cd.astype(BF16)


def _s5_layer(u, bsz, seq, a_re, a_im, log_dt, b_re, b_im, c_re, c_im, d_skip, glu_w, glu_b):
    t, width = u.shape
    n = a_re.shape[0] * a_re.shape[1]
    bd, pw, cd = _s5_tables(a_re, a_im, log_dt, b_re, b_im, c_re, c_im)
    tile = SUBLANES * S5_J
    step_rows = S5_TILES_PER_STEP * tile
    steps_per_seq = seq // step_rows
    const = lambda b, i: (0, 0)
    perm = _sublane_major_perm(tile)
    return pl.pallas_call(
        functools.partial(_s5_kernel, n_state=n),
        out_shape=jax.ShapeDtypeStruct((t, width), BF16),
        grid=(bsz, steps_per_seq),
        in_specs=[pl.BlockSpec((step_rows, width), lambda b, i: (b * steps_per_seq + i, 0)),
                  pl.BlockSpec((tile, tile), const),
                  pl.BlockSpec((tile, tile), const),
                  pl.BlockSpec(bd.shape, lambda b, i: (0, 0, 0)),
                  pl.BlockSpec((tile, 2 * n), const),
                  pl.BlockSpec((tile, 2 * n), const),
                  pl.BlockSpec(cd.shape, lambda b, i: (0, 0, 0)),
                  pl.BlockSpec((1, width), const),
                  pl.BlockSpec((width, width), const),
                  pl.BlockSpec((1, width), const)],
        out_specs=pl.BlockSpec((step_rows, width), lambda b, i: (b * steps_per_seq + i, 0)),
        scratch_shapes=[pltpu.VMEM((tile, 2 * n), F32), pltpu.VMEM((SUBLANES, 2 * n), F32)],
        compiler_params=_cparams(("arbitrary", "arbitrary")),
        name="s5_layer",
    )(u, jnp.asarray(perm, BF16), jnp.asarray(perm.T, BF16), bd, pw, pw.astype(BF16), cd,
      d_skip.reshape(1, width).astype(F32), glu_w.astype(BF16),
      glu_b.reshape(1, width).astype(F32))


def _finish(y, x_ref, mod_ref, pg_ref, out_ref, d_model):
    ms = jnp.mean(y * y, axis=-1, keepdims=True)
    yn = y * lax.rsqrt(ms + EPS) * pg_ref[...]
    gate_mod = mod_ref[0, :, 2 * d_model:3 * d_model]
    out_ref[...] = x_ref[...] + gate_mod * yn


def _split2(x):
    hi = x.astype(BF16)
    return hi, (x - hi.astype(F32)).astype(BF16)


def _out_ab_kernel(o0_ref, l0_ref, o1_ref, o2_ref, l1_ref, l2_ref, ob_ref, gate_ref, e_ref, un_ref, w_ref,
                   x_ref, mod_ref, pg_ref, out_ref, *, d_model):
    rows, aw = o0_ref.shape

    def unperm(o_ref, l_ref):
        pieces = []
        for piece in range(rows // PERM_ROWS):
            runs = slice(piece * PERM_RUN, (piece + 1) * PERM_RUN)
            o_p = o_ref[:, runs, :].reshape(PERM_ROWS, aw)
            l_p = l_ref[:, runs, :].reshape(PERM_ROWS, LANES)
            parts = jnp.concatenate([o_p] + list(_split2(l_p)), axis=1)
            pieces.append(jnp.dot(un_ref[...], parts, preferred_element_type=F32))
        moved = jnp.concatenate(pieces, axis=0)
        return moved[:, 0:aw], moved[:, aw:aw + LANES] + moved[:, aw + LANES:]

    o0, l0 = o0_ref[...].astype(F32), l0_ref[...]
    o1, l1 = unperm(o1_ref, l1_ref)
    o2, l2 = unperm(o2_ref, l2_ref)
    mx = jnp.maximum(jnp.maximum(l0, l1), l2)
    e0, e1, e2 = jnp.exp2(l0 - mx), jnp.exp2(l1 - mx), jnp.exp2(l2 - mx)
    inv_den = 1.0 / (e0 + e1 + e2)
    stacked = jnp.concatenate(list(_split2(e1 * inv_den)) + list(_split2(e2 * inv_den)), axis=0)
    wide = jnp.dot(stacked, e_ref[...], preferred_element_type=F32)
    term = lambda i: wide[i * rows:(i + 1) * rows]
    w1 = term(0) + term(1)
    w2 = term(2) + term(3)
    o_a = (1.0 - w1 - w2) * o0 + w1 * o1 + w2 * o2
    gate = gate_ref[...].astype(F32)
    sg = gate * _sigmoid(gate)
    aw = o_a.shape[-1]
    y = jnp.dot((o_a * sg[:, :aw]).astype(BF16), w_ref[0:aw, :], preferred_element_type=F32)
    y = y + jnp.dot((ob_ref[...].astype(F32) * sg[:, aw:]).astype(BF16), w_ref[aw:, :],
                    preferred_element_type=F32)
    _finish(y, x_ref, mod_ref, pg_ref, out_ref, d_model)


def _out_c_kernel(o_ref, gate_ref, w_ref, x_ref, mod_ref, pg_ref, out_ref, *, d_model):
    gate = gate_ref[...].astype(F32)
    o = o_ref[...].astype(F32) * (gate * _sigmoid(gate))
    y = jnp.dot(o.astype(BF16), w_ref[...], preferred_element_type=F32)
    _finish(y, x_ref, mod_ref, pg_ref, out_ref, d_model)


def _out_proj(kern, row_inputs, const_inputs, w_bf16, x2, mod3, post_g, seq, name, tile=ROW_TILE):
    t, d = x2.shape
    tiles_per_seq = seq // tile
    tiles_per_span = SPAN // tile

    def row_spec(a):
        if a.ndim == 2:
            return pl.BlockSpec((tile, a.shape[1]), lambda i: (i, 0))
        return pl.BlockSpec((None, SPAN_RES, tile // SPAN_RES, a.shape[-1]),
                            lambda i: (i // tiles_per_span, 0, i % tiles_per_span, 0))

    const_spec = lambda a: pl.BlockSpec(a.shape, lambda i: (0, 0))
    return pl.pallas_call(
        functools.partial(kern, d_model=d),
        out_shape=jax.ShapeDtypeStruct((t, d), F32),
        grid=(t // tile,),
        in_specs=([row_spec(a) for a in row_inputs] + [const_spec(a) for a in const_inputs]
                  + [const_spec(w_bf16), row_spec(x2),
                     pl.BlockSpec((1, 1, 3 * d), lambda i: (i // tiles_per_seq, 0, 0)),
                     pl.BlockSpec((1, d), lambda i: (0, 0))]),
        out_specs=pl.BlockSpec((tile, d), lambda i: (i, 0)),
        compiler_params=_cparams(("arbitrary",)),
        name=name,
    )(*row_inputs, *const_inputs, w_bf16, x2, mod3, post_g.reshape(1, d))


def _gdn_prep_kernel(x_ref, halo_ref, br_ref, cw_ref, alog_ref, dtb_ref,
                     w_ref, u_ref, qg_ref, kdt_ref, aqk_ref, dec_ref, xs_scr):
    n_all = GDN_PREP_BLOCKS * GDN_STEP
    first = pl.program_id(1) == 0
    halo = halo_ref[...]
    xs_scr[0:SUBLANES, :] = jnp.where(first, jnp.zeros_like(halo), halo)
    xs_scr[SUBLANES:SUBLANES + n_all, :] = x_ref[...]
    cw = cw_ref[...]
    ext = xs_scr[...].reshape(n_all // SUBLANES + 1, SUBLANES, x_ref.shape[1])
    first_sublane = lax.broadcasted_iota(jnp.int32, ext.shape, 1) == 0

    def delay(y):
        rot = pltpu.roll(y, 1, axis=1)
        prev = jnp.concatenate([rot[:1], rot[:-1]], axis=0)
        return jnp.where(first_sublane, prev, rot)

    acc = ext * cw[0:1]
    for j in range(1, C_CONV):
        acc = delay(acc) + ext * cw[j:j + 1]
    conv = acc[1:].reshape(n_all, x_ref.shape[1])
    act = conv * _sigmoid(conv)
    _gdn_prep_blocks(act, br_ref, alog_ref, dtb_ref, w_ref, u_ref, qg_ref, kdt_ref, aqk_ref, dec_ref)


def _gdn_prep_blocks(act, br_ref, alog_ref, dtb_ref, w_ref, u_ref, qg_ref, kdt_ref, aqk_ref, dec_ref):
    n = GDN_STEP
    c = GDN_CHUNK
    row = lax.broadcasted_iota(jnp.int32, (n, n), 0)
    col = lax.broadcasted_iota(jnp.int32, (n, n), 1)
    cum = jnp.logical_and((row // c) == (col // c), row >= col).astype(F32)
    prow = lax.broadcasted_iota(jnp.int32, (c, n), 0)
    plane = lax.broadcasted_iota(jnp.int32, (c, n), 1)
    left = plane < c
    pcol = jnp.where(left, plane, plane - c)
    tril = prow >= pcol
    strict = prow > pcol
    eye = (prow == pcol).astype(F32)
    pair = lambda full: jnp.where(left, full[0:c], full[c:])

    def blockdiag(p):
        zero = jnp.zeros_like(p)
        return jnp.concatenate([jnp.where(left, p, zero), jnp.where(left, zero, p)], axis=0)

    first_chunk = lax.broadcasted_iota(jnp.int32, (n, 1), 0) < GDN_CHUNK
    dec_row = lax.broadcasted_iota(jnp.int32, (SUBLANES, LANES), 0)
    heads = range(C_HEADS)
    lanes = [slice(h * LANES, (h + 1) * LANES) for h in heads]

    def gates(blk):
        rows = slice(blk * n, (blk + 1) * n)
        br = br_ref[rows, :]
        xg = br + dtb_ref[...]
        softplus = jnp.maximum(xg, 0.0) + jnp.log(1.0 + jnp.exp(-jnp.abs(xg)))
        g_all = -jnp.exp(alog_ref[...]) * softplus
        gc_all = jnp.dot(cum, g_all, preferred_element_type=F32, precision=HI)
        g_last = jnp.where(first_chunk, gc_all[c - 1:c], gc_all[n - 1:n])
        return dict(rows=rows, beta=_sigmoid(br), gc=gc_all, gc_t=gc_all.T, eg=jnp.exp(gc_all),
                    tail=jnp.exp(g_last - gc_all))

    def first_stage(ctx, blk, h):
        rows = ctx["rows"]
        gcol = ctx["gc"][:, C_HEADS + h:C_HEADS + h + 1]
        grow = ctx["gc_t"][C_HEADS + h:C_HEADS + h + 1, :]
        eg = ctx["eg"][:, C_HEADS + h:C_HEADS + h + 1]
        gcol_p = jnp.where(left, gcol[0:c], gcol[c:])
        decay = jnp.where(tril, jnp.exp(jnp.where(tril, gcol_p - grow, 0.0)), 0.0)
        q = act[rows, h * C_DK:(h + 1) * C_DK]
        k = act[rows, (C_HEADS + h) * C_DK:(C_HEADS + h + 1) * C_DK]
        v = act[rows, 2 * C_HEADS * C_DK + h * C_DV:2 * C_HEADS * C_DK + (h + 1) * C_DV]
        q = q * (lax.rsqrt(jnp.sum(q * q, axis=-1, keepdims=True) + EPS) * (C_DK ** -0.5))
        k = k * lax.rsqrt(jnp.sum(k * k, axis=-1, keepdims=True) + EPS)
        beta = ctx["beta"][:, h:h + 1]
        kb = k * beta
        k16 = k.astype(BF16)
        a = jnp.where(strict, pair(lax.dot_general(kb.astype(BF16), k16, _NT, preferred_element_type=F32)) * decay,
                      0.0)
        aqk = pair(lax.dot_general(q.astype(BF16), k16, _NT, preferred_element_type=F32)) * decay
        aqk_ref[rows, lanes[h]] = blockdiag(aqk).astype(BF16)
        qg_ref[rows, lanes[h]] = (q * eg).astype(BF16)
        kdt_ref[rows, lanes[h]] = (k * ctx["tail"][:, C_HEADS + h:C_HEADS + h + 1]).T.astype(BF16)
        dec_ref[blk, :, lanes[h]] = jnp.where(dec_row < SUBLANES // 2, eg[GDN_CHUNK - 1:GDN_CHUNK], eg[n - 1:n])
        rhs = jnp.concatenate([v * beta, kb * eg], axis=1).astype(BF16)
        return dict(a=a, inv=eye - a, rhs=rhs)

    n_blocks = act.shape[0] // n
    first_ctx = gates(0)
    state = [first_stage(first_ctx, 0, h) for h in heads]
    for blk in range(n_blocks):
        nxt_ctx = gates(blk + 1) if blk + 1 < n_blocks else None
        nxt = []

        def side_work(hs, blk=blk, nxt=nxt, nxt_ctx=nxt_ctx):
            if nxt_ctx is not None:
                nxt.extend(first_stage(nxt_ctx, blk + 1, h) for h in hs)

        invs = [st["inv"] for st in state]
        pows = []
        for st in state:
            pows.append(jnp.dot(st["a"].astype(BF16), blockdiag(st["a"]).astype(BF16), preferred_element_type=F32))
        side_work(heads[0:2])
        for level in range(4):
            for h in heads:
                both = jnp.dot(jnp.concatenate([invs[h], pows[h]], axis=0).astype(BF16),
                               blockdiag(pows[h]).astype(BF16), preferred_element_type=F32)
                invs[h] = invs[h] + both[0:c]
                pows[h] = both[c:]
            side_work(heads[2 + level:3 + level])
        for h in heads:
            invs[h] = invs[h] + jnp.dot(invs[h].astype(BF16), blockdiag(pows[h]).astype(BF16),
                                        preferred_element_type=F32)
        side_work(heads[6:8])
        rows = slice(blk * n, (blk + 1) * n)
        for h in heads:
            uw = jnp.dot(blockdiag(invs[h]).astype(BF16), state[h]["rhs"], preferred_element_type=F32)
            u_ref[rows, lanes[h]] = uw[:, 0:C_DV].astype(u_ref.dtype)
            w_ref[rows, lanes[h]] = uw[:, C_DV:].astype(BF16)
        state = nxt


def _gdn_rec_kernel(w_ref, u_ref, qg_ref, kdt_ref, aqk_ref, dec_ref, ng_ref, o_ref, s_scr):
    @pl.when(pl.program_id(0) == 0)
    def _():
        s_scr[...] = jnp.zeros_like(s_scr)

    half = SUBLANES // 2
    zeros = jnp.zeros((GDN_CHUNK, C_DV), BF16)
    lanes = [slice(h * LANES, (h + 1) * LANES) for h in range(C_HEADS)]
    chains = [(b, h) for b in range(w_ref.shape[0]) for h in range(C_HEADS)]
    states = [s_scr[b * C_HEADS + h] for b, h in chains]
    for blk in range(GDN_REC_BLOCKS):
        blk_rows = slice(blk * GDN_STEP, (blk + 1) * GDN_STEP)
        for j in range(GDN_STEP // GDN_CHUNK):
            rows = slice(blk * GDN_STEP + j * GDN_CHUNK, blk * GDN_STEP + (j + 1) * GDN_CHUNK)
            wss = []
            for c, (b, h) in enumerate(chains):
                wq = jnp.concatenate([w_ref[b, rows, lanes[h]], qg_ref[b, rows, lanes[h]]], axis=0)
                wss.append(jnp.dot(wq, states[c].astype(BF16), preferred_element_type=F32))
            for c, (b, h) in enumerate(chains):
                v_new = (u_ref[b, rows, lanes[h]].astype(F32) - wss[c][0:GDN_CHUNK]).astype(BF16)
                v_pad = jnp.concatenate([v_new, zeros] if j == 0 else [zeros, v_new], axis=0)
                both = jnp.dot(jnp.concatenate([aqk_ref[b, rows, lanes[h]], kdt_ref[b, blk_rows, lanes[h]]], axis=0),
                               v_pad, preferred_element_type=F32)
                o = wss[c][GDN_CHUNK:] + both[0:GDN_CHUNK]
                dec = dec_ref[b, blk, j * half:j * half + 1, lanes[h]]
                states[c] = states[c] * dec + both[GDN_CHUNK:]
                ms = jnp.mean(o * o, axis=-1, keepdims=True)
                o_ref[b, rows, lanes[h]] = (o * lax.rsqrt(ms + EPS) * ng_ref[...]).astype(o_ref.dtype)
    for c, (b, h) in enumerate(chains):
        s_scr[b * C_HEADS + h] = states[c]


def _gdn_core(qkv_pre, br, bsz, seq, conv_w, a_log, dt_bias, norm_g):
    t, qkv_w = qkv_pre.shape
    hw = C_HEADS * C_DV
    steps_per_seq = seq // GDN_STEP
    prep_rows = GDN_PREP_BLOCKS * GDN_STEP
    prep_steps = steps_per_seq // GDN_PREP_BLOCKS
    halo_blocks = prep_rows // SUBLANES
    pad_row = lambda vec: jnp.zeros((1, LANES), F32).at[0, C_HEADS:2 * C_HEADS].set(vec.astype(F32))
    tok = lambda b, i: (b * prep_steps + i, 0)
    const = lambda b, i: (0, 0)
    tok_spec = pl.BlockSpec((prep_rows, hw), tok)
    dec_spec = pl.BlockSpec((GDN_PREP_BLOCKS, SUBLANES, hw), lambda b, i: (b * prep_steps + i, 0, 0))
    w, u, qg, kdt, aqk, dec = pl.pallas_call(
        _gdn_prep_kernel,
        out_shape=[jax.ShapeDtypeStruct((t, hw), BF16), jax.ShapeDtypeStruct((t, hw), BF16),
                   jax.ShapeDtypeStruct((t, hw), BF16), jax.ShapeDtypeStruct((t, hw), BF16),
                   jax.ShapeDtypeStruct((t, hw), BF16),
                   jax.ShapeDtypeStruct((t // GDN_STEP, SUBLANES, hw), F32)],
        grid=(bsz, prep_steps),
        in_specs=[pl.BlockSpec((prep_rows, qkv_w), tok),
                  pl.BlockSpec((SUBLANES, qkv_w),
                               lambda b, i: (jnp.maximum((b * prep_steps + i) * halo_blocks - 1, 0), 0)),
                  pl.BlockSpec((prep_rows, LANES), tok),
                  pl.BlockSpec((C_CONV, qkv_w), const),
                  pl.BlockSpec((1, LANES), const),
                  pl.BlockSpec((1, LANES), const)],
        out_specs=[tok_spec, tok_spec, tok_spec, tok_spec, tok_spec, dec_spec],
        scratch_shapes=[pltpu.VMEM((SUBLANES + prep_rows, qkv_w), F32)],
        compiler_params=_cparams(("arbitrary", "arbitrary")),
        name="gdn_prep",
    )(qkv_pre, qkv_pre, br, conv_w.astype(F32), pad_row(a_log), pad_row(dt_bias))
    rec_steps = steps_per_seq // GDN_REC_BLOCKS
    seq_view = lambda a: a.reshape(bsz, seq, hw)
    rec_tok = pl.BlockSpec((bsz, GDN_REC_BLOCKS * GDN_STEP, hw), lambda i: (0, i, 0))
    rec_dec = pl.BlockSpec((bsz, GDN_REC_BLOCKS, SUBLANES, hw), lambda i: (0, i, 0, 0))
    o = pl.pallas_call(
        _gdn_rec_kernel,
        out_shape=jax.ShapeDtypeStruct((bsz, seq, hw), BF16),
        grid=(rec_steps,),
        in_specs=[rec_tok, rec_tok, rec_tok, rec_tok, rec_tok, rec_dec,
                  pl.BlockSpec((1, C_DV), lambda i: (0, 0))],
        out_specs=rec_tok,
        scratch_shapes=[pltpu.VMEM((bsz * C_HEADS, C_DK, C_DV), F32)],
        compiler_params=_cparams(("arbitrary",)),
        name="gdn_recurrence",
    )(seq_view(w), seq_view(u), seq_view(qg), seq_view(kdt), seq_view(aqk),
      dec.reshape(bsz, steps_per_seq, SUBLANES, hw), norm_g.reshape(1, C_DV).astype(F32))
    return o.reshape(t, hw)


def _ab_layer(x2, mod3, bsz, seq, pre_g, post_g, rel_bias, w_in, w_out, s5_params):
    b_width = s5_params[7].shape[-1]
    assert DILATED_CONFIGS[0][1] == 1 and all(SPAN_RES % dl == 0 for _, dl in DILATED_CONFIGS) and seq % SPAN == 0
    head_expand = jnp.asarray(np.arange(LANES)[:, None] == (np.arange(A_WIDTH)[None, :] // A_HEAD_DIM), BF16)
    unperm = jnp.asarray(_span_perm().T, BF16)
    splits = ((0, A_WIDTH, A_HEAD_DIM ** -0.5 * LOG2_E), (A_WIDTH, A_WIDTH, 1.0), (2 * A_WIDTH, A_WIDTH, 1.0),
              (3 * A_WIDTH, b_width, 1.0), (3 * A_WIDTH + b_width, A_WIDTH + b_width, 1.0))
    q, k, v, u, gate, qs, ks, vs = _in_proj(x2, mod3, pre_g, w_in.astype(BF16), splits,
                                            (BF16, BF16, BF16, F32, BF16), seq, n_span=3, tile=AB_TILE)
    outs, lses = [], []
    bias_tabs = _attn_bias_tables(rel_bias)
    for cfg, (_, dil) in enumerate(DILATED_CONFIGS):
        qkv = (q, k, v) if dil == 1 else (qs, ks, vs)
        o_c, lse_c = _dilated_attention_one(*qkv, bias_tabs[cfg], bsz, seq, dil)
        outs.append(o_c)
        lses.append(lse_c)
    o_b = _s5_layer(u, bsz, seq, *s5_params)
    row_inputs = [outs[0], lses[0], outs[1], outs[2], lses[1], lses[2], o_b, gate]
    return _out_proj(_out_ab_kernel, row_inputs, [head_expand, unperm], w_out.astype(BF16),
                     x2, mod3, post_g, seq, "out_proj_ab", tile=AB_TILE)


def _gdn_layer(x2, mod3, bsz, seq, pre_g, post_g, w_in, conv_w, a_log, dt_bias, norm_g, w_out):
    d = x2.shape[1]
    qkv_w = 2 * C_HEADS * C_DK + C_HEADS * C_DV
    gate_w = C_HEADS * C_DV
    w_pad = jnp.concatenate(
        [w_in, jnp.zeros((d, LANES - (w_in.shape[1] - qkv_w - gate_w)), w_in.dtype)], axis=1).astype(BF16)
    splits = ((0, qkv_w, 1.0), (qkv_w, gate_w, 1.0), (qkv_w + gate_w, LANES, 1.0))
    qkv_pre, gate, br = _in_proj(x2, mod3, pre_g, w_pad, splits, (F32, BF16, F32), seq)
    o = _gdn_core(qkv_pre, br, bsz, seq, conv_w, a_log, dt_bias, norm_g)
    return _out_proj(_out_c_kernel, [o, gate], [], w_out.astype(BF16), x2, mod3, post_g, seq, "out_proj_c",
                     tile=OUT_C_TILE)


def kernel(x, c, ada_w, ada_b, pre_g, post_g, rel_bias, ab_w_in, ab_w_out, s5_a_re, s5_a_im, s5_log_dt, s5_b_re, s5_b_im, s5_c_re, s5_c_im, s5_d, s5_glu_w, s5_glu_b, gdn_w_in, gdn_conv, gdn_a_log, gdn_dt_bias, gdn_norm_g, gdn_w_out):
    bsz, seq, d = x.shape
    depth = ada_w.shape[0]
    assert seq % ROW_TILE == 0 and seq % AB_TILE == 0 and seq % OUT_C_TILE == 0
    assert seq % (SUBLANES * S5_J * S5_TILES_PER_STEP) == 0 and AB_TILE % PERM_ROWS == 0 and SPAN % AB_TILE == 0
    assert seq % (GDN_STEP * GDN_PREP_BLOCKS) == 0 and seq % (GDN_STEP * GDN_REC_BLOCKS) == 0
    x2 = x.reshape(bsz * seq, d)
    mod = _adaln_mod(c, ada_w, ada_b)
    for layer in range(depth):
        j = layer // 2
        mod3 = mod[layer].reshape(bsz, 1, 3 * d)
        if layer % 2 == 0:
            s5_params = (s5_a_re[j], s5_a_im[j], s5_log_dt[j], s5_b_re[j], s5_b_im[j], s5_c_re[j], s5_c_im[j],
                         s5_d[j], s5_glu_w[j], s5_glu_b[j])
            x2 = _ab_layer(x2, mod3, bsz, seq, pre_g[layer], post_g[layer], rel_bias, ab_w_in[j], ab_w_out[j],
                           s5_params)
        else:
            x2 = _gdn_layer(x2, mod3, bsz, seq, pre_g[layer], post_g[layer], gdn_w_in[j], gdn_conv[j],
                            gdn_a_log[j], gdn_dt_bias[j], gdn_norm_g[j], gdn_w_out[j])
    return x2.reshape(bsz, seq, d)
```

```python
import functools
import math

import numpy as np
import jax
import jax.numpy as jnp
from jax import lax
from jax.experimental import pallas as pl
from jax.experimental.pallas import tpu as pltpu

F32 = jnp.float32
BF16 = jnp.bfloat16
HI = lax.Precision.HIGHEST

EPS = 1e-6
A_HEADS = 8
A_HEAD_DIM = 64
A_WIDTH = A_HEADS * A_HEAD_DIM
A_BLOCK = 128
DILATED_CONFIGS = ((128, 1), (512, 4), (2048, 16))
REL_BUCKETS = 32
REL_MAX_DIST = 2048
C_HEADS = 8
C_DK = 128
C_DV = 128
C_CONV = 4
MASK_NEG = -1e30
LOG2_E = math.log2(math.e)

LANES = 128
SUBLANES = 8
VMEM_LIMIT = 56 * 1024 * 1024
ROW_TILE = 512
AB_TILE = 1024
OUT_C_TILE = 1024
COL_CHUNK = 512
SPAN_RES = max(dl for _, dl in DILATED_CONFIGS)
SPAN = A_BLOCK * SPAN_RES
PERM_ROWS = 256
PERM_RUN = PERM_ROWS // SPAN_RES
ATTN_BLOCKS_PER_STEP = 8
S5_J = 32
S5_LW = 512
S5_TILES_PER_STEP = 4
GDN_STEP = 128
GDN_CHUNK = 64
GDN_PREP_BLOCKS = 4
GDN_REC_BLOCKS = 2

_NT = (((1,), (1,)), ((), ()))


def _cparams(sem):
    return pltpu.CompilerParams(dimension_semantics=sem, vmem_limit_bytes=VMEM_LIMIT)


def _sigmoid(x):
    return 1.0 / (1.0 + jnp.exp(-x))


def _mod_kernel(c_ref, w_ref, b_ref, o_ref):
    c = c_ref[...]
    ca = c * _sigmoid(c)
    o_ref[0] = jnp.dot(ca.astype(BF16), w_ref[0].astype(BF16), preferred_element_type=F32) + b_ref[0]


def _adaln_mod(c, ada_w, ada_b):
    depth, d, d3 = ada_w.shape
    bsz = c.shape[0]
    return pl.pallas_call(
        _mod_kernel,
        out_shape=jax.ShapeDtypeStruct((depth, bsz, d3), F32),
        grid=(depth, d3 // d),
        in_specs=[pl.BlockSpec((bsz, d), lambda l, j: (0, 0)),
                  pl.BlockSpec((1, d, d), lambda l, j: (l, 0, j)),
                  pl.BlockSpec((1, 1, d), lambda l, j: (l, 0, j))],
        out_specs=pl.BlockSpec((1, bsz, d), lambda l, j: (l, 0, j)),
        compiler_params=_cparams(("arbitrary", "arbitrary")),
        name="adaln_mod",
    )(c, ada_w, ada_b.reshape(depth, 1, d3))


def _span_perm():
    rho = np.arange(PERM_ROWS)
    nat = SPAN_RES * (rho % PERM_RUN) + rho // PERM_RUN
    return (nat[:, None] == np.arange(PERM_ROWS)[None, :]).astype(np.float32)


def _modulated_norm(x_ref, mod_ref, g_ref, d_model):
    x = x_ref[...]
    ms = jnp.mean(x * x, axis=-1, keepdims=True)
    y = x * lax.rsqrt(ms + EPS) * g_ref[...]
    shift = mod_ref[0, :, 0:d_model]
    scale = mod_ref[0, :, d_model:2 * d_model]
    return (y * (1.0 + scale) + shift).astype(BF16)


def _sublane_major_perm(tile):
    rho = np.arange(tile)
    src = (rho % SUBLANES) * (tile // SUBLANES) + rho // SUBLANES
    return (src[:, None] == np.arange(tile)[None, :]).astype(np.float32)


def _in_proj_kernel(x_ref, mod_ref, g_ref, *rest, splits, d_model, n_span, n_weights):
    w_refs = rest[:n_weights]
    pm_ref = rest[n_weights] if n_span else None
    out_refs = rest[n_weights + (1 if n_span else 0):]
    span_refs = out_refs[len(splits):]
    h = _modulated_norm(x_ref, mod_ref, g_ref, d_model)
    for idx, ((wi, c0, width, mult), o_ref) in enumerate(zip(splits, out_refs)):
        for cc in range(0, width, COL_CHUNK):
            cw = min(COL_CHUNK, width - cc)
            acc = jnp.dot(h, w_refs[wi][:, c0 + cc:c0 + cc + cw], preferred_element_type=F32)
            if mult != 1.0:
                acc = acc * mult
            val = acc.astype(o_ref.dtype)
            o_ref[:, cc:cc + cw] = val
            if idx < n_span:
                for piece in range(x_ref.shape[0] // PERM_ROWS):
                    moved = jnp.dot(pm_ref[...], val[piece * PERM_ROWS:(piece + 1) * PERM_ROWS],
                                    preferred_element_type=F32).astype(BF16)
                    span_refs[idx][:, piece * PERM_RUN:(piece + 1) * PERM_RUN, cc:cc + cw] = (
                        moved.reshape(SPAN_RES, PERM_RUN, cw))


def _in_proj(x2, mod3, gain, weights, splits, out_dtypes, seq, n_span=0, tile=ROW_TILE):
    t, d = x2.shape
    tiles_per_seq = seq // tile
    tiles_per_span = SPAN // tile
    row = lambda wd: pl.BlockSpec((tile, wd), lambda i: (i, 0))
    span_shape = lambda wd: jax.ShapeDtypeStruct((t // SPAN, SPAN_RES, A_BLOCK, wd), BF16)
    span_spec = lambda wd: pl.BlockSpec((None, SPAN_RES, tile // SPAN_RES, wd),
                                        lambda i: (i // tiles_per_span, 0, i % tiles_per_span, 0))
    perm_in = [jnp.asarray(_span_perm(), BF16)] if n_span else []
    perm_spec = [pl.BlockSpec((PERM_ROWS, PERM_ROWS), lambda i: (0, 0))] if n_span else []
    return pl.pallas_call(
        functools.partial(_in_proj_kernel, splits=splits, d_model=d, n_span=n_span, n_weights=len(weights)),
        out_shape=([jax.ShapeDtypeStruct((t, wd), dt) for (_, _, wd, _), dt in zip(splits, out_dtypes)]
                   + [span_shape(wd) for (_, _, wd, _) in splits[:n_span]]),
        grid=(t // tile,),
        in_specs=[row(d),
                  pl.BlockSpec((1, 1, 3 * d), lambda i: (i // tiles_per_seq, 0, 0)),
                  pl.BlockSpec((1, d), lambda i: (0, 0))]
                 + [pl.BlockSpec(wt.shape, lambda i: (0, 0)) for wt in weights] + perm_spec,
        out_specs=[row(wd) for (_, _, wd, _) in splits] + [span_spec(wd) for (_, _, wd, _) in splits[:n_span]],
        compiler_params=_cparams(("arbitrary",)),
        name="in_proj",
    )(x2, mod3, gain.reshape(1, d), *weights, *perm_in)


def _t5_bucket(dist):
    dist = np.maximum(dist, 0)
    max_exact = REL_BUCKETS // 2
    large = max_exact + (np.log(np.maximum(dist, 1) / max_exact)
                         / math.log(REL_MAX_DIST / max_exact) * (REL_BUCKETS - max_exact)).astype(np.int32)
    large = np.minimum(large, REL_BUCKETS - 1)
    return np.where(dist < max_exact, dist, large).astype(np.int32)


def _bias_kernel(rb_ref, bucket_ref, mask_ref, o_ref):
    def body(r, carry):
        rows = pl.ds(pl.multiple_of(r * SUBLANES, SUBLANES), SUBLANES)
        bk = bucket_ref[0, rows, :]
        accs = [jnp.zeros(bk.shape, F32) for _ in range(A_HEADS)]
        for b in range(REL_BUCKETS):
            eq = bk == b
            accs = [jnp.where(eq, rb_ref[b, h], acc) for h, acc in enumerate(accs)]
        for f in range(2):
            keep = mask_ref[0, f, rows, :] != 0
            for h in range(A_HEADS):
                o_ref[0, f, h, rows, :] = jnp.where(keep, accs[h] * LOG2_E, MASK_NEG)
        return carry
    lax.fori_loop(0, A_BLOCK // SUBLANES, body, 0)


def _attn_bias_tables(rel_bias):
    qi = np.arange(A_BLOCK)[:, None]
    kj = np.arange(2 * A_BLOCK)[None, :]
    rel = qi + A_BLOCK - kj
    buckets, masks = [], []
    for window, dil in DILATED_CONFIGS:
        band = (rel >= 0) & (rel <= window // dil)
        bucket = _t5_bucket(rel * dil)
        mask = np.stack([band & (kj >= A_BLOCK), band]).astype(np.int32)
        if dil > 1:
            runs = SPAN_RES // dil
            run = A_BLOCK // runs
            rho = np.arange(A_BLOCK)
            sub = runs * (rho % run) + rho // run
            keys = np.concatenate([sub, A_BLOCK + sub])
            bucket = bucket[sub][:, keys]
            mask = mask[:, sub][:, :, keys]
        buckets.append(bucket)
        masks.append(mask)
    n_cfg = len(DILATED_CONFIGS)
    return pl.pallas_call(
        _bias_kernel,
        out_shape=jax.ShapeDtypeStruct((n_cfg, 2, A_HEADS, A_BLOCK, 2 * A_BLOCK), F32),
        grid=(n_cfg,),
        in_specs=[pl.BlockSpec(memory_space=pltpu.SMEM),
                  pl.BlockSpec((1, A_BLOCK, 2 * A_BLOCK), lambda c: (c, 0, 0)),
                  pl.BlockSpec((1, 2, A_BLOCK, 2 * A_BLOCK), lambda c: (c, 0, 0, 0))],
        out_specs=pl.BlockSpec((1, 2, A_HEADS, A_BLOCK, 2 * A_BLOCK), lambda c: (c, 0, 0, 0, 0)),
        compiler_params=_cparams(("arbitrary",)),
        name="attn_bias",
    )(rel_bias.astype(F32), jnp.asarray(np.stack(buckets)), jnp.asarray(np.stack(masks)))


def _attn_block(q, kcat, vcat, bias_ref):
    lane = lax.broadcasted_iota(jnp.int32, (A_BLOCK, LANES), 1)
    low = lane < A_HEAD_DIM
    lse_tile = jnp.zeros((A_BLOCK, LANES), F32)
    zero = jnp.zeros((A_BLOCK, LANES), BF16)
    heads = range(A_HEADS)
    pair = [slice((h // 2) * LANES, (h // 2 + 1) * LANES) for h in heads]
    scores = []
    for h in heads:
        qm = jnp.where(low if h % 2 == 0 else jnp.logical_not(low), q[:, pair[h]], zero)
        scores.append(lax.dot_general(qm, kcat[:, pair[h]], _NT, preferred_element_type=F32) + bias_ref[0, h])
    probs, inv_l = [], []
    for h in heads:
        m = jnp.max(scores[h], axis=-1, keepdims=True)
        p = jnp.exp2(scores[h] - m)
        l = jnp.sum(p, axis=-1, keepdims=True)
        probs.append(p.astype(BF16))
        inv_l.append(1.0 / l)
        lse_tile = jnp.where(lane == h, m + jnp.log2(l), lse_tile)
    outs = [jnp.dot(probs[h], vcat[:, pair[h]], preferred_element_type=F32) * inv_l[h] for h in heads]
    o = jnp.concatenate([jnp.where(low, outs[2 * hp], outs[2 * hp + 1]) for hp in range(A_HEADS // 2)], axis=1)
    return o, lse_tile


def _attn_kernel(q_ref, kp_ref, kc_ref, vp_ref, vc_ref, bias0_ref, bias_ref, o_ref, lse_ref, *, per_lead):
    def sub(ref, rr, i):
        if ref.ndim == 2:
            return ref[i * A_BLOCK:(i + 1) * A_BLOCK, :]
        run = ref.shape[3] // per_lead
        return ref[i // per_lead, :, rr, (i % per_lead) * run:(i % per_lead + 1) * run, :]

    def put(ref, rr, i, val):
        if ref.ndim == 2:
            ref[i * A_BLOCK:(i + 1) * A_BLOCK, :] = val.astype(ref.dtype)
        else:
            run = ref.shape[3] // per_lead
            ref[i // per_lead, :, rr, (i % per_lead) * run:(i % per_lead + 1) * run, :] = (
                val.astype(ref.dtype).reshape(ref.shape[1], run, ref.shape[4]))

    flat = lambda v: v.reshape(A_BLOCK, v.shape[-1])
    token_order = q_ref.ndim == 2
    n_blocks = q_ref.shape[0] // A_BLOCK if token_order else q_ref.shape[0] * per_lead
    for rr in range(1 if token_order else q_ref.shape[2]):
        k_prev = flat(kp_ref[...] if token_order else kp_ref[0, :, rr])
        v_prev = flat(vp_ref[...] if token_order else vp_ref[0, :, rr])
        for i in range(n_blocks):
            k_cur, v_cur = flat(sub(kc_ref, rr, i)), flat(sub(vc_ref, rr, i))
            o, lse_tile = _attn_block(flat(sub(q_ref, rr, i)), jnp.concatenate([k_prev, k_cur], axis=0),
                                      jnp.concatenate([v_prev, v_cur], axis=0),
                                      bias0_ref if i == 0 else bias_ref)
            put(o_ref, rr, i, o)
            put(lse_ref, rr, i, lse_tile)
            k_prev, v_prev = k_cur, v_cur


def _dilated_attention_one(q, k, v, bias_tabs, cfg, bsz, seq, dil):
    nb = seq // dil // A_BLOCK
    step = min(ATTN_BLOCKS_PER_STEP, nb)
    w = A_WIDTH
    if dil == 1:
        per_lead = n_res = 1
        view = lambda a: a.reshape(bsz, seq, a.shape[-1])
        block = lambda wd: (None, step * A_BLOCK, wd)
        block_prev = lambda wd: (None, A_BLOCK, wd)
        cur = lambda b, r, m: (b, m, 0)
        prev = lambda b, r, m: (b, jnp.maximum(m * step - 1, 0), 0)
        out_shape = lambda wd, dt: jax.ShapeDtypeStruct((bsz, seq, wd), dt)
    else:
        runs = SPAN_RES // dil
        run = A_BLOCK // runs
        spans_per_seq = seq // SPAN
        per_lead = min(step, runs)
        lead = step // per_lead
        n_res = max(1, ATTN_BLOCKS_PER_STEP // step)
        assert runs % per_lead == 0 and spans_per_seq % lead == 0 and dil % n_res == 0
        view = lambda a: a.reshape(a.shape[0], runs, dil, A_BLOCK, a.shape[-1])
        block = lambda wd: (lead, runs, n_res, per_lead * run, wd)
        block_prev = lambda wd: (1, runs, n_res, run, wd)
        cur = lambda b, r, m: ((b * spans_per_seq + (m * step) // runs) // lead, 0, r, ((m * step) % runs) // per_lead, 0)

        def prev(b, r, m):
            n = jnp.maximum(m * step - 1, 0)
            return (b * spans_per_seq + n // runs, 0, r, n % runs, 0)

        out_shape = lambda wd, dt: jax.ShapeDtypeStruct((bsz * spans_per_seq, runs, dil, A_BLOCK, wd), dt)
    blk = pl.BlockSpec(block(w), cur)
    blk_prev = pl.BlockSpec(block_prev(w), prev)
    bias_block = (None, 1, A_HEADS, A_BLOCK, 2 * A_BLOCK)
    o, lse = pl.pallas_call(
        functools.partial(_attn_kernel, per_lead=per_lead),
        out_shape=[out_shape(w, BF16), out_shape(LANES, F32)],
        grid=(bsz, dil // n_res, nb // step),
        in_specs=[blk, blk_prev, blk, blk_prev, blk,
                  pl.BlockSpec(bias_block, lambda b, r, m: (cfg, jnp.minimum(m, 1), 0, 0, 0)),
                  pl.BlockSpec(bias_block, lambda b, r, m: (cfg, 1, 0, 0, 0))],
        out_specs=[pl.BlockSpec(block(w), cur), pl.BlockSpec(block(LANES), cur)],
        compiler_params=_cparams(("arbitrary", "arbitrary", "arbitrary")),
        name=f"dilated_attn_d{dil}",
    )(view(q), view(k), view(k), view(v), view(v), bias_tabs, bias_tabs)
    if dil == 1:
        return o.reshape(bsz * seq, w), lse.reshape(bsz * seq, LANES)
    return o.reshape(o.shape[0], SPAN_RES, A_BLOCK, w), lse.reshape(o.shape[0], SPAN_RES, A_BLOCK, LANES)


def _gelu_tanh(x):
    c = math.sqrt(2.0 / math.pi)
    return x * (0.5 * (1.0 + jnp.tanh(c * (x + 0.044715 * (x * x * x)))))


def _s5_kernel(u_ref, pm_ref, pmt_ref, bd_ref, pw_ref, pw16_ref, cd_ref, dsk_ref, gw_ref, gb_ref, o_ref,
               x_scr, st_scr, *, n_state):
    @pl.when(pl.program_id(1) == 0)
    def _():
        st_scr[...] = jnp.zeros_like(st_scr)

    for tile_idx in range(S5_TILES_PER_STEP):
        _s5_tile(tile_idx, u_ref, pm_ref, pmt_ref, bd_ref, pw_ref, pw16_ref, cd_ref, dsk_ref, gw_ref, gb_ref,
                 o_ref, x_scr, st_scr, n_state)


def _s5_tile(tile_idx, u_ref, pm_ref, pmt_ref, bd_ref, pw_ref, pw16_ref, cd_ref, dsk_ref, gw_ref, gb_ref,
             o_ref, x_scr, st_scr, n_state):
    n = n_state
    jn = S5_J
    tile_rows = slice(tile_idx * SUBLANES * jn, (tile_idx + 1) * SUBLANES * jn)

    u = u_ref[tile_rows, :]
    u_perm = jnp.dot(pm_ref[...], u.astype(BF16), preferred_element_type=F32).astype(BF16)
    cw = u.shape[1] * S5_LW // n
    chunks = [(slice(c0, c0 + S5_LW), slice(n + c0, n + c0 + S5_LW)) for c0 in range(0, n, S5_LW)]

    def project_in(q):
        bu = jnp.dot(u_perm[:, q * cw:(q + 1) * cw], bd_ref[q], preferred_element_type=F32)
        x_scr[:, chunks[q][0]] = bu[:, 0:S5_LW]
        x_scr[:, chunks[q][1]] = bu[:, S5_LW:]

    project_in(0)
    end_r, end_i = [], []
    for q, (re, im) in enumerate(chunks):
        if q + 1 < len(chunks):
            project_in(q + 1)
        a1r, a1i = pw_ref[0:SUBLANES, re], pw_ref[0:SUBLANES, im]
        xr = jnp.zeros((SUBLANES, S5_LW), F32)
        xi = jnp.zeros((SUBLANES, S5_LW), F32)
        for j in range(jn):
            rows = slice(SUBLANES * j, SUBLANES * (j + 1))
            nr = a1r * xr - a1i * xi + x_scr[rows, re]
            ni = a1r * xi + a1i * xr + x_scr[rows, im]
            xr, xi = nr, ni
            x_scr[rows, re] = xr
            x_scr[rows, im] = xi
        end_r.append(xr)
        end_i.append(xi)
    er, ei = jnp.concatenate(end_r, axis=1), jnp.concatenate(end_i, axis=1)
    last = SUBLANES * (jn - 1)
    ajr, aji = pw_ref[last:last + 1, 0:n], pw_ref[last:last + 1, n:]
    pr, pi = st_scr[0:1, 0:n], st_scr[0:1, n:]
    cin_r, cin_i = [], []
    for s in range(SUBLANES):
        cin_r.append(pr)
        cin_i.append(pi)
        nr = er[s:s + 1] + ajr * pr - aji * pi
        ni = ei[s:s + 1] + ajr * pi + aji * pr
        pr, pi = nr, ni
    st_scr[0:1, 0:n] = pr
    st_scr[0:1, n:] = pi
    cr_all = jnp.concatenate(cin_r, axis=0)
    ci_all = jnp.concatenate(cin_i, axis=0)
    pack = 2 * SUBLANES
    ys = []
    for q, (re, im) in enumerate(chunks):
        cr = jnp.concatenate([cr_all[:, re]] * 2, axis=0).astype(BF16)
        ci = jnp.concatenate([ci_all[:, re]] * 2, axis=0).astype(BF16)
        xr_parts, xi_parts = [], []
        for m in range(SUBLANES * jn // pack):
            rows = slice(pack * m, pack * (m + 1))
            pjr, pji = pw16_ref[rows, re], pw16_ref[rows, im]
            xr_parts.append(x_scr[rows, re].astype(BF16) + (pjr * cr - pji * ci))
            xi_parts.append(x_scr[rows, im].astype(BF16) + (pjr * ci + pji * cr))
        xcat = jnp.concatenate([jnp.concatenate(xr_parts, axis=0), jnp.concatenate(xi_parts, axis=0)], axis=1)
        ys.append(jnp.dot(xcat, cd_ref[q], preferred_element_type=F32))

    y_perm = jnp.concatenate(ys, axis=1)
    y_hi = y_perm.astype(BF16)
    y_lo = (y_perm - y_hi.astype(F32)).astype(BF16)
    y = (jnp.dot(pmt_ref[...], y_hi, preferred_element_type=F32)
         + jnp.dot(pmt_ref[...], y_lo, preferred_element_type=F32))
    y = _gelu_tanh(y + dsk_ref[...] * u)
    z = jnp.dot(y.astype(BF16), gw_ref[...], preferred_element_type=F32) + gb_ref[...]
    o_ref[tile_rows, :] = (y * _sigmoid(z)).astype(o_ref.dtype)


def _s5_tables(a_re, a_im, log_dt, b_re, b_im, c_re, c_im):
    g, p = a_re.shape
    dt = jnp.exp(log_dt.astype(F32))[:, None]
    ar, ai = a_re.astype(F32), a_im.astype(F32)
    mag = jnp.exp(dt * ar)
    abar_r, abar_i = mag * jnp.cos(dt * ai), mag * jnp.sin(dt * ai)
    den = ar * ar + ai * ai
    fr = ((abar_r - 1.0) * ar + abar_i * ai) / den
    fi = (abar_i * ar - (abar_r - 1.0) * ai) / den
    br, bi = b_re.astype(F32), b_im.astype(F32)
    bbar_r = fr[..., None] * br - fi[..., None] * bi
    bbar_i = fr[..., None] * bi + fi[..., None] * br
    m = br.shape[-1]
    gc = S5_LW // p
    nq = g // gc
    eye = jnp.eye(gc, dtype=F32)
    dense_b = lambda t: jnp.einsum('qgpm,gh->qgmhp', t.reshape(nq, gc, p, m), eye).reshape(nq, gc * m, gc * p)
    bd = jnp.concatenate([dense_b(bbar_r), dense_b(bbar_i)], axis=2)
    dense_c = lambda t: jnp.einsum('qgmp,gh->qgphm', t.reshape(nq, gc, m, p), eye).reshape(nq, gc * p, gc * m)
    cd = jnp.concatenate([dense_c(c_re.astype(F32)), -dense_c(c_im.astype(F32))], axis=1)
    kk = jnp.arange(1, S5_J + 1, dtype=F32)[:, None, None]
    pmag = jnp.exp(kk * (dt * ar)[None])
    pw_r = (pmag * jnp.cos(kk * (dt * ai)[None])).reshape(S5_J, g * p)
    pw_i = (pmag * jnp.sin(kk * (dt * ai)[None])).reshape(S5_J, g * p)
    pw = jnp.repeat(jnp.concatenate([pw_r, pw_i], axis=1), SUBLANES, axis=0)
    return bd.astype(BF16), pw, cd.astype(BF16)


def _s5_layer(u, bsz, seq, a_re, a_im, log_dt, b_re, b_im, c_re, c_im, d_skip, glu_w, glu_b):
    t, width = u.shape
    n = a_re.shape[0] * a_re.shape[1]
    bd, pw, cd = _s5_tables(a_re, a_im, log_dt, b_re, b_im, c_re, c_im)
    tile = SUBLANES * S5_J
    step_rows = S5_TILES_PER_STEP * tile
    steps_per_seq = seq // step_rows
    const = lambda b, i: (0, 0)
    perm = _sublane_major_perm(tile)
    return pl.pallas_call(
        functools.partial(_s5_kernel, n_state=n),
        out_shape=jax.ShapeDtypeStruct((t, width), BF16),
        grid=(bsz, steps_per_seq),
        in_specs=[pl.BlockSpec((step_rows, width), lambda b, i: (b * steps_per_seq + i, 0)),
                  pl.BlockSpec((tile, tile), const),
                  pl.BlockSpec((tile, tile), const),
                  pl.BlockSpec(bd.shape, lambda b, i: (0, 0, 0)),
                  pl.BlockSpec((tile, 2 * n), const),
                  pl.BlockSpec((tile, 2 * n), const),
                  pl.BlockSpec(cd.shape, lambda b, i: (0, 0, 0)),
                  pl.BlockSpec((1, width), const),
                  pl.BlockSpec((width, width), const),
                  pl.BlockSpec((1, width), const)],
        out_specs=pl.BlockSpec((step_rows, width), lambda b, i: (b * steps_per_seq + i, 0)),
        scratch_shapes=[pltpu.VMEM((tile, 2 * n), F32), pltpu.VMEM((SUBLANES, 2 * n), F32)],
        compiler_params=_cparams(("arbitrary", "arbitrary")),
        name="s5_layer",
    )(u, jnp.asarray(perm, BF16), jnp.asarray(perm.T, BF16), bd, pw, pw.astype(BF16), cd,
      d_skip.reshape(1, width).astype(F32), glu_w.astype(BF16),
      glu_b.reshape(1, width).astype(F32))


def _finish(y, x_ref, mod_ref, pg_ref, out_ref, d_model):
    ms = jnp.mean(y * y, axis=-1, keepdims=True)
    yn = y * lax.rsqrt(ms + EPS) * pg_ref[...]
    gate_mod = mod_ref[0, :, 2 * d_model:3 * d_model]
    out_ref[...] = x_ref[...] + gate_mod * yn


def _split2(x):
    hi = x.astype(BF16)
    return hi, (x - hi.astype(F32)).astype(BF16)


def _out_ab_kernel(o0_ref, l0_ref, o1_ref, o2_ref, l1_ref, l2_ref, ob_ref, gate_ref, e_ref, un_ref, w_ref,
                   x_ref, mod_ref, pg_ref, out_ref, *, d_model):
    rows, aw = o0_ref.shape

    def unperm(o_ref, l_ref):
        pieces = []
        for piece in range(rows // PERM_ROWS):
            runs = slice(piece * PERM_RUN, (piece + 1) * PERM_RUN)
            o_p = o_ref[:, runs, :].reshape(PERM_ROWS, aw)
            l_p = l_ref[:, runs, :].reshape(PERM_ROWS, LANES)
            parts = jnp.concatenate([o_p] + list(_split2(l_p)), axis=1)
            pieces.append(jnp.dot(un_ref[...], parts, preferred_element_type=F32))
        moved = jnp.concatenate(pieces, axis=0)
        return moved[:, 0:aw], moved[:, aw:aw + LANES] + moved[:, aw + LANES:]

    o0, l0 = o0_ref[...].astype(F32), l0_ref[...]
    o1, l1 = unperm(o1_ref, l1_ref)
    o2, l2 = unperm(o2_ref, l2_ref)
    mx = jnp.maximum(jnp.maximum(l0, l1), l2)
    e0, e1, e2 = jnp.exp2(l0 - mx), jnp.exp2(l1 - mx), jnp.exp2(l2 - mx)
    inv_den = 1.0 / (e0 + e1 + e2)
    stacked = jnp.concatenate(list(_split2(e1 * inv_den)) + list(_split2(e2 * inv_den)), axis=0)
    wide = jnp.dot(stacked, e_ref[...], preferred_element_type=F32)
    term = lambda i: wide[i * rows:(i + 1) * rows]
    w1 = term(0) + term(1)
    w2 = term(2) + term(3)
    o_a = (1.0 - w1 - w2) * o0 + w1 * o1 + w2 * o2
    gate = gate_ref[...].astype(F32)
    sg = gate * _sigmoid(gate)
    aw = o_a.shape[-1]
    y = jnp.dot((o_a * sg[:, :aw]).astype(BF16), w_ref[0:aw, :], preferred_element_type=F32)
    y = y + jnp.dot((ob_ref[...].astype(F32) * sg[:, aw:]).astype(BF16), w_ref[aw:, :],
                    preferred_element_type=F32)
    _finish(y, x_ref, mod_ref, pg_ref, out_ref, d_model)


def _out_c_kernel(o_ref, gate_ref, w_ref, x_ref, mod_ref, pg_ref, out_ref, *, d_model):
    gate = gate_ref[...].astype(F32)
    o = o_ref[...].astype(F32) * (gate * _sigmoid(gate))
    y = jnp.dot(o.astype(BF16), w_ref[...], preferred_element_type=F32)
    _finish(y, x_ref, mod_ref, pg_ref, out_ref, d_model)


def _out_proj(kern, row_inputs, const_inputs, w_bf16, x2, mod3, post_g, seq, name, tile=ROW_TILE):
    t, d = x2.shape
    tiles_per_seq = seq // tile
    tiles_per_span = SPAN // tile

    def row_spec(a):
        if a.ndim == 2:
            return pl.BlockSpec((tile, a.shape[1]), lambda i: (i, 0))
        return pl.BlockSpec((None, SPAN_RES, tile // SPAN_RES, a.shape[-1]),
                            lambda i: (i // tiles_per_span, 0, i % tiles_per_span, 0))

    const_spec = lambda a: pl.BlockSpec(a.shape, lambda i: (0, 0))
    return pl.pallas_call(
        functools.partial(kern, d_model=d),
        out_shape=jax.ShapeDtypeStruct((t, d), F32),
        grid=(t // tile,),
        in_specs=([row_spec(a) for a in row_inputs] + [const_spec(a) for a in const_inputs]
                  + [const_spec(w_bf16), row_spec(x2),
                     pl.BlockSpec((1, 1, 3 * d), lambda i: (i // tiles_per_seq, 0, 0)),
                     pl.BlockSpec((1, d), lambda i: (0, 0))]),
        out_specs=pl.BlockSpec((tile, d), lambda i: (i, 0)),
        compiler_params=_cparams(("arbitrary",)),
        name=name,
    )(*row_inputs, *const_inputs, w_bf16, x2, mod3, post_g.reshape(1, d))


def _gdn_prep_kernel(x_ref, halo_ref, br_ref, cw_ref, alog_ref, dtb_ref,
                     w_ref, u_ref, qg_ref, kdt_ref, aqk_ref, dec_ref, xs_scr):
    n_all = GDN_PREP_BLOCKS * GDN_STEP
    first = pl.program_id(1) == 0
    halo = halo_ref[...]
    xs_scr[0:SUBLANES, :] = jnp.where(first, jnp.zeros_like(halo), halo)
    xs_scr[SUBLANES:SUBLANES + n_all, :] = x_ref[...]
    cw = cw_ref[...]
    ext = xs_scr[...].reshape(n_all // SUBLANES + 1, SUBLANES, x_ref.shape[1])
    first_sublane = lax.broadcasted_iota(jnp.int32, ext.shape, 1) == 0

    def delay(y):
        rot = pltpu.roll(y, 1, axis=1)
        prev = jnp.concatenate([rot[:1], rot[:-1]], axis=0)
        return jnp.where(first_sublane, prev, rot)

    acc = ext * cw[0:1]
    for j in range(1, C_CONV):
        acc = delay(acc) + ext * cw[j:j + 1]
    conv = acc[1:].reshape(n_all, x_ref.shape[1])
    act = conv * _sigmoid(conv)
    _gdn_prep_blocks(act, br_ref, alog_ref, dtb_ref, w_ref, u_ref, qg_ref, kdt_ref, aqk_ref, dec_ref)


def _gdn_prep_blocks(act, br_ref, alog_ref, dtb_ref, w_ref, u_ref, qg_ref, kdt_ref, aqk_ref, dec_ref):
    n = GDN_STEP
    c = GDN_CHUNK
    row = lax.broadcasted_iota(jnp.int32, (n, n), 0)
    col = lax.broadcasted_iota(jnp.int32, (n, n), 1)
    cum = jnp.logical_and((row // c) == (col // c), row >= col).astype(F32)
    prow = lax.broadcasted_iota(jnp.int32, (c, n), 0)
    plane = lax.broadcasted_iota(jnp.int32, (c, n), 1)
    left = plane < c
    pcol = jnp.where(left, plane, plane - c)
    tril = prow >= pcol
    strict = prow > pcol
    eye = (prow == pcol).astype(F32)
    pair = lambda full: jnp.where(left, full[0:c], full[c:])

    def blockdiag(p):
        zero = jnp.zeros_like(p)
        return jnp.concatenate([jnp.where(left, p, zero), jnp.where(left, zero, p)], axis=0)

    first_chunk = lax.broadcasted_iota(jnp.int32, (n, 1), 0) < GDN_CHUNK
    dec_row = lax.broadcasted_iota(jnp.int32, (SUBLANES, LANES), 0)
    heads = range(C_HEADS)
    lanes = [slice(h * LANES, (h + 1) * LANES) for h in heads]

    def gates(blk):
        rows = slice(blk * n, (blk + 1) * n)
        br = br_ref[rows, :]
        xg = br + dtb_ref[...]
        softplus = jnp.maximum(xg, 0.0) + jnp.log(1.0 + jnp.exp(-jnp.abs(xg)))
        g_all = -jnp.exp(alog_ref[...]) * softplus
        gc_all = jnp.dot(cum, g_all, preferred_element_type=F32, precision=HI)
        g_last = jnp.where(first_chunk, gc_all[c - 1:c], gc_all[n - 1:n])
        return dict(rows=rows, beta=_sigmoid(br), gc=gc_all, gc_t=gc_all.T, eg=jnp.exp(gc_all),
                    tail=jnp.exp(g_last - gc_all))

    def first_stage(ctx, blk, h):
        rows = ctx["rows"]
        gcol = ctx["gc"][:, C_HEADS + h:C_HEADS + h + 1]
        grow = ctx["gc_t"][C_HEADS + h:C_HEADS + h + 1, :]
        eg = ctx["eg"][:, C_HEADS + h:C_HEADS + h + 1]
        gcol_p = jnp.where(left, gcol[0:c], gcol[c:])
        decay = jnp.where(tril, jnp.exp(jnp.where(tril, gcol_p - grow, 0.0)), 0.0)
        q = act[rows, h * C_DK:(h + 1) * C_DK]
        k = act[rows, (C_HEADS + h) * C_DK:(C_HEADS + h + 1) * C_DK]
        v = act[rows, 2 * C_HEADS * C_DK + h * C_DV:2 * C_HEADS * C_DK + (h + 1) * C_DV]
        q = q * (lax.rsqrt(jnp.sum(q * q, axis=-1, keepdims=True) + EPS) * (C_DK ** -0.5))
        k = k * lax.rsqrt(jnp.sum(k * k, axis=-1, keepdims=True) + EPS)
        beta = ctx["beta"][:, h:h + 1]
        kb = k * beta
        k16 = k.astype(BF16)
        a = jnp.where(strict, pair(lax.dot_general(kb.astype(BF16), k16, _NT, preferred_element_type=F32)) * decay,
                      0.0)
        aqk = pair(lax.dot_general(q.astype(BF16), k16, _NT, preferred_element_type=F32)) * decay
        aqk_ref[rows, lanes[h]] = blockdiag(aqk).astype(BF16)
        qg_ref[rows, lanes[h]] = (q * eg).astype(BF16)
        kdt_ref[rows, lanes[h]] = (k * ctx["tail"][:, C_HEADS + h:C_HEADS + h + 1]).T.astype(BF16)
        dec_ref[blk, :, lanes[h]] = jnp.where(dec_row < SUBLANES // 2, eg[GDN_CHUNK - 1:GDN_CHUNK], eg[n - 1:n])
        rhs = jnp.concatenate([v * beta, kb * eg], axis=1).astype(BF16)
        return dict(a=a, inv=eye - a, rhs=rhs)

    n_blocks = act.shape[0] // n
    first_ctx = gates(0)
    state = [first_stage(first_ctx, 0, h) for h in heads]
    for blk in range(n_blocks):
        nxt_ctx = gates(blk + 1) if blk + 1 < n_blocks else None
        nxt = []

        def side_work(hs, blk=blk, nxt=nxt, nxt_ctx=nxt_ctx):
            if nxt_ctx is not None:
                nxt.extend(first_stage(nxt_ctx, blk + 1, h) for h in hs)

        invs = [st["inv"] for st in state]
        pows = []
        for st in state:
            pows.append(jnp.dot(st["a"].astype(BF16), blockdiag(st["a"]).astype(BF16), preferred_element_type=F32))
        side_work(heads[0:2])
        for level in range(4):
            for h in heads:
                both = jnp.dot(jnp.concatenate([invs[h], pows[h]], axis=0).astype(BF16),
                               blockdiag(pows[h]).astype(BF16), preferred_element_type=F32)
                invs[h] = invs[h] + both[0:c]
                pows[h] = both[c:]
            side_work(heads[2 + level:3 + level])
        for h in heads:
            invs[h] = invs[h] + jnp.dot(invs[h].astype(BF16), blockdiag(pows[h]).astype(BF16),
                                        preferred_element_type=F32)
        side_work(heads[6:8])
        rows = slice(blk * n, (blk + 1) * n)
        for h in heads:
            uw = jnp.dot(blockdiag(invs[h]).astype(BF16), state[h]["rhs"], preferred_element_type=F32)
            u_ref[rows, lanes[h]] = uw[:, 0:C_DV].astype(u_ref.dtype)
            w_ref[rows, lanes[h]] = uw[:, C_DV:].astype(BF16)
        state = nxt


def _gdn_rec_kernel(w_ref, u_ref, qg_ref, kdt_ref, aqk_ref, dec_ref, ng_ref, o_ref, s_scr):
    @pl.when(pl.program_id(0) == 0)
    def _():
        s_scr[...] = jnp.zeros_like(s_scr)

    half = SUBLANES // 2
    zeros = jnp.zeros((GDN_CHUNK, C_DV), BF16)
    lanes = [slice(h * LANES, (h + 1) * LANES) for h in range(C_HEADS)]
    chains = [(b, h) for b in range(w_ref.shape[0]) for h in range(C_HEADS)]
    states = [s_scr[b * C_HEADS + h] for b, h in chains]
    for blk in range(GDN_REC_BLOCKS):
        blk_rows = slice(blk * GDN_STEP, (blk + 1) * GDN_STEP)
        for j in range(GDN_STEP // GDN_CHUNK):
            rows = slice(blk * GDN_STEP + j * GDN_CHUNK, blk * GDN_STEP + (j + 1) * GDN_CHUNK)
            wss = []
            for c, (b, h) in enumerate(chains):
                wq = jnp.concatenate([w_ref[b, rows, lanes[h]], qg_ref[b, rows, lanes[h]]], axis=0)
                wss.append(jnp.dot(wq, states[c].astype(BF16), preferred_element_type=F32))
            for c, (b, h) in enumerate(chains):
                v_new = (u_ref[b, rows, lanes[h]].astype(F32) - wss[c][0:GDN_CHUNK]).astype(BF16)
                v_pad = jnp.concatenate([v_new, zeros] if j == 0 else [zeros, v_new], axis=0)
                both = jnp.dot(jnp.concatenate([aqk_ref[b, rows, lanes[h]], kdt_ref[b, blk_rows, lanes[h]]], axis=0),
                               v_pad, preferred_element_type=F32)
                o = wss[c][GDN_CHUNK:] + both[0:GDN_CHUNK]
                dec = dec_ref[b, blk, j * half:j * half + 1, lanes[h]]
                states[c] = states[c] * dec + both[GDN_CHUNK:]
                ms = jnp.mean(o * o, axis=-1, keepdims=True)
                o_ref[b, rows, lanes[h]] = (o * lax.rsqrt(ms + EPS) * ng_ref[...]).astype(o_ref.dtype)
    for c, (b, h) in enumerate(chains):
        s_scr[b * C_HEADS + h] = states[c]


def _gdn_core(qkv_pre, br, bsz, seq, conv_w, a_log, dt_bias, norm_g):
    t, qkv_w = qkv_pre.shape
    hw = C_HEADS * C_DV
    steps_per_seq = seq // GDN_STEP
    prep_rows = GDN_PREP_BLOCKS * GDN_STEP
    prep_steps = steps_per_seq // GDN_PREP_BLOCKS
    halo_blocks = prep_rows // SUBLANES
    pad_row = lambda vec: jnp.zeros((1, LANES), F32).at[0, C_HEADS:2 * C_HEADS].set(vec.astype(F32))
    tok = lambda b, i: (b * prep_steps + i, 0)
    const = lambda b, i: (0, 0)
    tok_spec = pl.BlockSpec((prep_rows, hw), tok)
    dec_spec = pl.BlockSpec((GDN_PREP_BLOCKS, SUBLANES, hw), lambda b, i: (b * prep_steps + i, 0, 0))
    w, u, qg, kdt, aqk, dec = pl.pallas_call(
        _gdn_prep_kernel,
        out_shape=[jax.ShapeDtypeStruct((t, hw), BF16), jax.ShapeDtypeStruct((t, hw), BF16),
                   jax.ShapeDtypeStruct((t, hw), BF16), jax.ShapeDtypeStruct((t, hw), BF16),
                   jax.ShapeDtypeStruct((t, hw), BF16),
                   jax.ShapeDtypeStruct((t // GDN_STEP, SUBLANES, hw), F32)],
        grid=(bsz, prep_steps),
        in_specs=[pl.BlockSpec((prep_rows, qkv_w), tok),
                  pl.BlockSpec((SUBLANES, qkv_w),
                               lambda b, i: (jnp.maximum((b * prep_steps + i) * halo_blocks - 1, 0), 0)),
                  pl.BlockSpec((prep_rows, LANES), tok),
                  pl.BlockSpec((C_CONV, qkv_w), const),
                  pl.BlockSpec((1, LANES), const),
                  pl.BlockSpec((1, LANES), const)],
        out_specs=[tok_spec, tok_spec, tok_spec, tok_spec, tok_spec, dec_spec],
        scratch_shapes=[pltpu.VMEM((SUBLANES + prep_rows, qkv_w), F32)],
        compiler_params=_cparams(("arbitrary", "arbitrary")),
        name="gdn_prep",
    )(qkv_pre, qkv_pre, br, conv_w.astype(F32), pad_row(a_log), pad_row(dt_bias))
    rec_steps = steps_per_seq // GDN_REC_BLOCKS
    seq_view = lambda a: a.reshape(bsz, seq, hw)
    rec_tok = pl.BlockSpec((bsz, GDN_REC_BLOCKS * GDN_STEP, hw), lambda i: (0, i, 0))
    rec_dec = pl.BlockSpec((bsz, GDN_REC_BLOCKS, SUBLANES, hw), lambda i: (0, i, 0, 0))
    o = pl.pallas_call(
        _gdn_rec_kernel,
        out_shape=jax.ShapeDtypeStruct((bsz, seq, hw), BF16),
        grid=(rec_steps,),
        in_specs=[rec_tok, rec_tok, rec_tok, rec_tok, rec_tok, rec_dec,
                  pl.BlockSpec((1, C_DV), lambda i: (0, 0))],
        out_specs=rec_tok,
        scratch_shapes=[pltpu.VMEM((bsz * C_HEADS, C_DK, C_DV), F32)],
        compiler_params=_cparams(("arbitrary",)),
        name="gdn_recurrence",
    )(seq_view(w), seq_view(u), seq_view(qg), seq_view(kdt), seq_view(aqk),
      dec.reshape(bsz, steps_per_seq, SUBLANES, hw), norm_g.reshape(1, C_DV).astype(F32))
    return o.reshape(t, hw)


def _ab_layer(x2, mod3, bsz, seq, pre_g, post_g, rel_bias, w_in, w_out, s5_params):
    b_width = s5_params[7].shape[-1]
    assert DILATED_CONFIGS[0][1] == 1 and all(SPAN_RES % dl == 0 for _, dl in DILATED_CONFIGS) and seq % SPAN == 0
    head_expand = jnp.asarray(np.arange(LANES)[:, None] == (np.arange(A_WIDTH)[None, :] // A_HEAD_DIM), BF16)
    unperm = jnp.asarray(_span_perm().T, BF16)
    splits = ((0, 0, A_WIDTH, A_HEAD_DIM ** -0.5 * LOG2_E), (0, A_WIDTH, A_WIDTH, 1.0),
              (0, 2 * A_WIDTH, A_WIDTH, 1.0), (0, 3 * A_WIDTH, b_width, 1.0),
              (0, 3 * A_WIDTH + b_width, A_WIDTH + b_width, 1.0))
    q, k, v, u, gate, qs, ks, vs = _in_proj(x2, mod3, pre_g, [w_in.astype(BF16)], splits,
                                            (BF16, BF16, BF16, F32, BF16), seq, n_span=3, tile=AB_TILE)
    outs, lses = [], []
    bias_tabs = _attn_bias_tables(rel_bias)
    for cfg, (_, dil) in enumerate(DILATED_CONFIGS):
        qkv = (q, k, v) if dil == 1 else (qs, ks, vs)
        o_c, lse_c = _dilated_attention_one(*qkv, bias_tabs, cfg, bsz, seq, dil)
        outs.append(o_c)
        lses.append(lse_c)
    o_b = _s5_layer(u, bsz, seq, *s5_params)
    row_inputs = [outs[0], lses[0], outs[1], outs[2], lses[1], lses[2], o_b, gate]
    return _out_proj(_out_ab_kernel, row_inputs, [head_expand, unperm], w_out.astype(BF16),
                     x2, mod3, post_g, seq, "out_proj_ab", tile=AB_TILE)


def _gdn_layer(x2, mod3, bsz, seq, pre_g, post_g, w_in, conv_w, a_log, dt_bias, norm_g, w_out):
    d = x2.shape[1]
    qkv_w = 2 * C_HEADS * C_DK + C_HEADS * C_DV
    gate_w = C_HEADS * C_DV
    w_main = w_in[:, 0:qkv_w + gate_w].astype(BF16)
    w_tail = jnp.concatenate([w_in[:, qkv_w + gate_w:],
                              jnp.zeros((d, LANES - (w_in.shape[1] - qkv_w - gate_w)), w_in.dtype)], axis=1).astype(BF16)
    splits = ((0, 0, qkv_w, 1.0), (0, qkv_w, gate_w, 1.0), (1, 0, LANES, 1.0))
    qkv_pre, gate, br = _in_proj(x2, mod3, pre_g, [w_main, w_tail], splits, (F32, BF16, F32), seq)
    o = _gdn_core(qkv_pre, br, bsz, seq, conv_w, a_log, dt_bias, norm_g)
    return _out_proj(_out_c_kernel, [o, gate], [], w_out.astype(BF16), x2, mod3, post_g, seq, "out_proj_c",
                     tile=OUT_C_TILE)


def kernel(x, c, ada_w, ada_b, pre_g, post_g, rel_bias, ab_w_in, ab_w_out, s5_a_re, s5_a_im, s5_log_dt, s5_b_re, s5_b_im, s5_c_re, s5_c_im, s5_d, s5_glu_w, s5_glu_b, gdn_w_in, gdn_conv, gdn_a_log, gdn_dt_bias, gdn_norm_g, gdn_w_out):
    bsz, seq, d = x.shape
    depth = ada_w.shape[0]
    assert seq % ROW_TILE == 0 and seq % AB_TILE == 0 and seq % OUT_C_TILE == 0
    assert seq % (SUBLANES * S5_J * S5_TILES_PER_STEP) == 0 and AB_TILE % PERM_ROWS == 0 and SPAN % AB_TILE == 0
    assert seq % (GDN_STEP * GDN_PREP_BLOCKS) == 0 and seq % (GDN_STEP * GDN_REC_BLOCKS) == 0
    x2 = x.reshape(bsz * seq, d)
    mod = _adaln_mod(c, ada_w, ada_b)
    for layer in range(depth):
        j = layer // 2
        mod3 = mod[layer].reshape(bsz, 1, 3 * d)
        if layer % 2 == 0:
            s5_params = (s5_a_re[j], s5_a_im[j], s5_log_dt[j], s5_b_re[j], s5_b_im[j], s5_c_re[j], s5_c_im[j],
                         s5_d[j], s5_glu_w[j], s5_glu_b[j])
            x2 = _ab_layer(x2, mod3, bsz, seq, pre_g[layer], post_g[layer], rel_bias, ab_w_in[j], ab_w_out[j],
                           s5_params)
        else:
            x2 = _gdn_layer(x2, mod3, bsz, seq, pre_g[layer], post_g[layer], gdn_w_in[j], gdn_conv[j],
                            gdn_a_log[j], gdn_dt_bias[j], gdn_norm_g[j], gdn_w_out[j])
    return x2.reshape(bsz, seq, d)
```

```python
import functools
import math

import numpy as np
import jax
import jax.numpy as jnp
from jax import lax
from jax.experimental import pallas as pl
from jax.experimental.pallas import tpu as pltpu

F32 = jnp.float32
BF16 = jnp.bfloat16
HI = lax.Precision.HIGHEST

EPS = 1e-6
A_HEADS = 8
A_HEAD_DIM = 64
A_WIDTH = A_HEADS * A_HEAD_DIM
A_BLOCK = 128
DILATED_CONFIGS = ((128, 1), (512, 4), (2048, 16))
REL_BUCKETS = 32
REL_MAX_DIST = 2048
C_HEADS = 8
C_DK = 128
C_DV = 128
C_CONV = 4
MASK_NEG = -1e30
LOG2_E = math.log2(math.e)

LANES = 128
SUBLANES = 8
VMEM_LIMIT = 56 * 1024 * 1024
ROW_TILE = 512
AB_TILE = 1024
OUT_C_TILE = 1024
COL_CHUNK = 512
SPAN_RES = max(dl for _, dl in DILATED_CONFIGS)
SPAN = A_BLOCK * SPAN_RES
PERM_ROWS = 256
PERM_RUN = PERM_ROWS // SPAN_RES
ATTN_BLOCKS_PER_STEP = 8
S5_J = 32
S5_LW = 512
S5_TILES_PER_STEP = 4
GDN_STEP = 128
GDN_CHUNK = 64
GDN_PREP_BLOCKS = 4
GDN_REC_BLOCKS = 2

_NT = (((1,), (1,)), ((), ()))


def _cparams(sem):
    return pltpu.CompilerParams(dimension_semantics=sem, vmem_limit_bytes=VMEM_LIMIT)


def _sigmoid(x):
    return 1.0 / (1.0 + jnp.exp(-x))


def _mod_kernel(c_ref, w_ref, b_ref, o_ref):
    c = c_ref[...]
    ca = c * _sigmoid(c)
    o_ref[0] = jnp.dot(ca.astype(BF16), w_ref[0].astype(BF16), preferred_element_type=F32) + b_ref[0]


def _adaln_mod(c, ada_w, ada_b):
    depth, d, d3 = ada_w.shape
    bsz = c.shape[0]
    return pl.pallas_call(
        _mod_kernel,
        out_shape=jax.ShapeDtypeStruct((depth, bsz, d3), F32),
        grid=(depth, d3 // d),
        in_specs=[pl.BlockSpec((bsz, d), lambda l, j: (0, 0)),
                  pl.BlockSpec((1, d, d), lambda l, j: (l, 0, j)),
                  pl.BlockSpec((1, 1, d), lambda l, j: (l, 0, j))],
        out_specs=pl.BlockSpec((1, bsz, d), lambda l, j: (l, 0, j)),
        compiler_params=_cparams(("arbitrary", "arbitrary")),
        name="adaln_mod",
    )(c, ada_w, ada_b.reshape(depth, 1, d3))


def _span_perm():
    rho = np.arange(PERM_ROWS)
    nat = SPAN_RES * (rho % PERM_RUN) + rho // PERM_RUN
    return (nat[:, None] == np.arange(PERM_ROWS)[None, :]).astype(np.float32)


def _modulated_norm(x_ref, mod_ref, g_ref, d_model):
    x = x_ref[...]
    ms = jnp.mean(x * x, axis=-1, keepdims=True)
    y = x * lax.rsqrt(ms + EPS) * g_ref[...]
    shift = mod_ref[0, :, 0:d_model]
    scale = mod_ref[0, :, d_model:2 * d_model]
    return (y * (1.0 + scale) + shift).astype(BF16)


def _sublane_major_perm(tile):
    rho = np.arange(tile)
    src = (rho % SUBLANES) * (tile // SUBLANES) + rho // SUBLANES
    return (src[:, None] == np.arange(tile)[None, :]).astype(np.float32)


def _in_proj_kernel(x_ref, mod_ref, g_ref, w_ref, *rest, splits, d_model, n_span):
    pm_ref = rest[0] if n_span else None
    out_refs = rest[1:] if n_span else rest
    span_refs = out_refs[len(splits):]
    h = _modulated_norm(x_ref, mod_ref, g_ref, d_model)
    for idx, ((c0, width, mult), o_ref) in enumerate(zip(splits, out_refs)):
        for cc in range(0, width, COL_CHUNK):
            cw = min(COL_CHUNK, width - cc)
            acc = jnp.dot(h, w_ref[:, c0 + cc:c0 + cc + cw], preferred_element_type=F32)
            if mult != 1.0:
                acc = acc * mult
            val = acc.astype(o_ref.dtype)
            o_ref[:, cc:cc + cw] = val
            if idx < n_span:
                for piece in range(x_ref.shape[0] // PERM_ROWS):
                    moved = jnp.dot(pm_ref[...], val[piece * PERM_ROWS:(piece + 1) * PERM_ROWS],
                                    preferred_element_type=F32).astype(BF16)
                    span_refs[idx][:, piece * PERM_RUN:(piece + 1) * PERM_RUN, cc:cc + cw] = (
                        moved.reshape(SPAN_RES, PERM_RUN, cw))


def _in_proj(x2, mod3, gain, w_bf16, splits, out_dtypes, seq, n_span=0, tile=ROW_TILE):
    t, d = x2.shape
    tiles_per_seq = seq // tile
    tiles_per_span = SPAN // tile
    n_w = w_bf16.shape[1]
    row = lambda wd: pl.BlockSpec((tile, wd), lambda i: (i, 0))
    span_shape = lambda wd: jax.ShapeDtypeStruct((t // SPAN, SPAN_RES, A_BLOCK, wd), BF16)
    span_spec = lambda wd: pl.BlockSpec((None, SPAN_RES, tile // SPAN_RES, wd),
                                        lambda i: (i // tiles_per_span, 0, i % tiles_per_span, 0))
    perm_in = [jnp.asarray(_span_perm(), BF16)] if n_span else []
    perm_spec = [pl.BlockSpec((PERM_ROWS, PERM_ROWS), lambda i: (0, 0))] if n_span else []
    return pl.pallas_call(
        functools.partial(_in_proj_kernel, splits=splits, d_model=d, n_span=n_span),
        out_shape=([jax.ShapeDtypeStruct((t, wd), dt) for (_, wd, _), dt in zip(splits, out_dtypes)]
                   + [span_shape(wd) for (_, wd, _) in splits[:n_span]]),
        grid=(t // tile,),
        in_specs=[row(d),
                  pl.BlockSpec((1, 1, 3 * d), lambda i: (i // tiles_per_seq, 0, 0)),
                  pl.BlockSpec((1, d), lambda i: (0, 0)),
                  pl.BlockSpec((d, n_w), lambda i: (0, 0))] + perm_spec,
        out_specs=[row(wd) for (_, wd, _) in splits] + [span_spec(wd) for (_, wd, _) in splits[:n_span]],
        compiler_params=_cparams(("arbitrary",)),
        name="in_proj",
    )(x2, mod3, gain.reshape(1, d), w_bf16, *perm_in)


def _t5_bucket(dist):
    dist = np.maximum(dist, 0)
    max_exact = REL_BUCKETS // 2
    large = max_exact + (np.log(np.maximum(dist, 1) / max_exact)
                         / math.log(REL_MAX_DIST / max_exact) * (REL_BUCKETS - max_exact)).astype(np.int32)
    large = np.minimum(large, REL_BUCKETS - 1)
    return np.where(dist < max_exact, dist, large).astype(np.int32)


def _bias_kernel(rb_ref, bucket_ref, mask_ref, o_ref):
    def body(r, carry):
        rows = pl.ds(pl.multiple_of(r * SUBLANES, SUBLANES), SUBLANES)
        bk = bucket_ref[0, rows, :]
        accs = [jnp.zeros(bk.shape, F32) for _ in range(A_HEADS)]
        for b in range(REL_BUCKETS):
            eq = bk == b
            accs = [jnp.where(eq, rb_ref[b, h], acc) for h, acc in enumerate(accs)]
        for f in range(2):
            keep = mask_ref[0, f, rows, :] != 0
            for h in range(A_HEADS):
                o_ref[0, f, h, rows, :] = jnp.where(keep, accs[h] * LOG2_E, MASK_NEG)
        return carry
    lax.fori_loop(0, A_BLOCK // SUBLANES, body, 0)


def _attn_bias_tables(rel_bias):
    qi = np.arange(A_BLOCK)[:, None]
    kj = np.arange(2 * A_BLOCK)[None, :]
    rel = qi + A_BLOCK - kj
    buckets, masks = [], []
    for window, dil in DILATED_CONFIGS:
        band = (rel >= 0) & (rel <= window // dil)
        bucket = _t5_bucket(rel * dil)
        mask = np.stack([band & (kj >= A_BLOCK), band]).astype(np.int32)
        if dil > 1:
            runs = SPAN_RES // dil
            run = A_BLOCK // runs
            rho = np.arange(A_BLOCK)
            sub = runs * (rho % run) + rho // run
            keys = np.concatenate([sub, A_BLOCK + sub])
            bucket = bucket[sub][:, keys]
            mask = mask[:, sub][:, :, keys]
        buckets.append(bucket)
        masks.append(mask)
    n_cfg = len(DILATED_CONFIGS)
    return pl.pallas_call(
        _bias_kernel,
        out_shape=jax.ShapeDtypeStruct((n_cfg, 2, A_HEADS, A_BLOCK, 2 * A_BLOCK), F32),
        grid=(n_cfg,),
        in_specs=[pl.BlockSpec(memory_space=pltpu.SMEM),
                  pl.BlockSpec((1, A_BLOCK, 2 * A_BLOCK), lambda c: (c, 0, 0)),
                  pl.BlockSpec((1, 2, A_BLOCK, 2 * A_BLOCK), lambda c: (c, 0, 0, 0))],
        out_specs=pl.BlockSpec((1, 2, A_HEADS, A_BLOCK, 2 * A_BLOCK), lambda c: (c, 0, 0, 0, 0)),
        compiler_params=_cparams(("arbitrary",)),
        name="attn_bias",
    )(rel_bias.astype(F32), jnp.asarray(np.stack(buckets)), jnp.asarray(np.stack(masks)))


def _attn_block(q, kcat, vcat, bias_ref):
    lane = lax.broadcasted_iota(jnp.int32, (A_BLOCK, LANES), 1)
    low = lane < A_HEAD_DIM
    lse_tile = jnp.zeros((A_BLOCK, LANES), F32)
    zero = jnp.zeros((A_BLOCK, LANES), BF16)
    heads = range(A_HEADS)
    pair = [slice((h // 2) * LANES, (h // 2 + 1) * LANES) for h in heads]
    scores = []
    for h in heads:
        qm = jnp.where(low if h % 2 == 0 else jnp.logical_not(low), q[:, pair[h]], zero)
        scores.append(lax.dot_general(qm, kcat[:, pair[h]], _NT, preferred_element_type=F32) + bias_ref[0, h])
    probs, inv_l = [], []
    for h in heads:
        m = jnp.max(scores[h], axis=-1, keepdims=True)
        p = jnp.exp2(scores[h] - m)
        l = jnp.sum(p, axis=-1, keepdims=True)
        probs.append(p.astype(BF16))
        inv_l.append(1.0 / l)
        lse_tile = jnp.where(lane == h, m + jnp.log2(l), lse_tile)
    outs = [jnp.dot(probs[h], vcat[:, pair[h]], preferred_element_type=F32) * inv_l[h] for h in heads]
    o = jnp.concatenate([jnp.where(low, outs[2 * hp], outs[2 * hp + 1]) for hp in range(A_HEADS // 2)], axis=1)
    return o, lse_tile


def _attn_kernel(q_ref, kp_ref, kc_ref, vp_ref, vc_ref, bias0_ref, bias_ref, o_ref, lse_ref, *, per_lead):
    def sub(ref, rr, i):
        if ref.ndim == 2:
            return ref[i * A_BLOCK:(i + 1) * A_BLOCK, :]
        run = ref.shape[3] // per_lead
        return ref[i // per_lead, :, rr, (i % per_lead) * run:(i % per_lead + 1) * run, :]

    def put(ref, rr, i, val):
        if ref.ndim == 2:
            ref[i * A_BLOCK:(i + 1) * A_BLOCK, :] = val.astype(ref.dtype)
        else:
            run = ref.shape[3] // per_lead
            ref[i // per_lead, :, rr, (i % per_lead) * run:(i % per_lead + 1) * run, :] = (
                val.astype(ref.dtype).reshape(ref.shape[1], run, ref.shape[4]))

    flat = lambda v: v.reshape(A_BLOCK, v.shape[-1])
    token_order = q_ref.ndim == 2
    n_blocks = q_ref.shape[0] // A_BLOCK if token_order else q_ref.shape[0] * per_lead
    for rr in range(1 if token_order else q_ref.shape[2]):
        k_prev = flat(kp_ref[...] if token_order else kp_ref[0, :, rr])
        v_prev = flat(vp_ref[...] if token_order else vp_ref[0, :, rr])
        for i in range(n_blocks):
            k_cur, v_cur = flat(sub(kc_ref, rr, i)), flat(sub(vc_ref, rr, i))
            o, lse_tile = _attn_block(flat(sub(q_ref, rr, i)), jnp.concatenate([k_prev, k_cur], axis=0),
                                      jnp.concatenate([v_prev, v_cur], axis=0),
                                      bias0_ref if i == 0 else bias_ref)
            put(o_ref, rr, i, o)
            put(lse_ref, rr, i, lse_tile)
            k_prev, v_prev = k_cur, v_cur


def _dilated_attention_one(q, k, v, bias_tabs, cfg, bsz, seq, dil):
    nb = seq // dil // A_BLOCK
    step = min(ATTN_BLOCKS_PER_STEP, nb)
    w = A_WIDTH
    if dil == 1:
        per_lead = n_res = 1
        view = lambda a: a.reshape(bsz, seq, a.shape[-1])
        block = lambda wd: (None, step * A_BLOCK, wd)
        block_prev = lambda wd: (None, A_BLOCK, wd)
        cur = lambda b, r, m: (b, m, 0)
        prev = lambda b, r, m: (b, jnp.maximum(m * step - 1, 0), 0)
        out_shape = lambda wd, dt: jax.ShapeDtypeStruct((bsz, seq, wd), dt)
    else:
        runs = SPAN_RES // dil
        run = A_BLOCK // runs
        spans_per_seq = seq // SPAN
        per_lead = min(step, runs)
        lead = step // per_lead
        n_res = max(1, ATTN_BLOCKS_PER_STEP // step)
        assert runs % per_lead == 0 and spans_per_seq % lead == 0 and dil % n_res == 0
        view = lambda a: a.reshape(a.shape[0], runs, dil, A_BLOCK, a.shape[-1])
        block = lambda wd: (lead, runs, n_res, per_lead * run, wd)
        block_prev = lambda wd: (1, runs, n_res, run, wd)
        cur = lambda b, r, m: ((b * spans_per_seq + (m * step) // runs) // lead, 0, r, ((m * step) % runs) // per_lead, 0)

        def prev(b, r, m):
            n = jnp.maximum(m * step - 1, 0)
            return (b * spans_per_seq + n // runs, 0, r, n % runs, 0)

        out_shape = lambda wd, dt: jax.ShapeDtypeStruct((bsz * spans_per_seq, runs, dil, A_BLOCK, wd), dt)
    blk = pl.BlockSpec(block(w), cur)
    blk_prev = pl.BlockSpec(block_prev(w), prev)
    bias_block = (None, 1, A_HEADS, A_BLOCK, 2 * A_BLOCK)
    o, lse = pl.pallas_call(
        functools.partial(_attn_kernel, per_lead=per_lead),
        out_shape=[out_shape(w, BF16), out_shape(LANES, F32)],
        grid=(bsz, dil // n_res, nb // step),
        in_specs=[blk, blk_prev, blk, blk_prev, blk,
                  pl.BlockSpec(bias_block, lambda b, r, m: (cfg, jnp.minimum(m, 1), 0, 0, 0)),
                  pl.BlockSpec(bias_block, lambda b, r, m: (cfg, 1, 0, 0, 0))],
        out_specs=[pl.BlockSpec(block(w), cur), pl.BlockSpec(block(LANES), cur)],
        compiler_params=_cparams(("arbitrary", "arbitrary", "arbitrary")),
        name=f"dilated_attn_d{dil}",
    )(view(q), view(k), view(k), view(v), view(v), bias_tabs, bias_tabs)
    if dil == 1:
        return o.reshape(bsz * seq, w), lse.reshape(bsz * seq, LANES)
    return o.reshape(o.shape[0], SPAN_RES, A_BLOCK, w), lse.reshape(o.shape[0], SPAN_RES, A_BLOCK, LANES)


def _gelu_tanh(x):
    c = math.sqrt(2.0 / math.pi)
    return x * (0.5 * (1.0 + jnp.tanh(c * (x + 0.044715 * (x * x * x)))))


def _s5_kernel(u_ref, pm_ref, pmt_ref, bd_ref, pw_ref, pw16_ref, cd_ref, dsk_ref, gw_ref, gb_ref, o_ref,
               x_scr, st_scr, *, n_state):
    @pl.when(pl.program_id(1) == 0)
    def _():
        st_scr[...] = jnp.zeros_like(st_scr)

    for tile_idx in range(S5_TILES_PER_STEP):
        _s5_tile(tile_idx, u_ref, pm_ref, pmt_ref, bd_ref, pw_ref, pw16_ref, cd_ref, dsk_ref, gw_ref, gb_ref,
                 o_ref, x_scr, st_scr, n_state)


def _s5_tile(tile_idx, u_ref, pm_ref, pmt_ref, bd_ref, pw_ref, pw16_ref, cd_ref, dsk_ref, gw_ref, gb_ref,
             o_ref, x_scr, st_scr, n_state):
    n = n_state
    jn = S5_J
    tile_rows = slice(tile_idx * SUBLANES * jn, (tile_idx + 1) * SUBLANES * jn)

    u = u_ref[tile_rows, :]
    u_perm = jnp.dot(pm_ref[...], u.astype(BF16), preferred_element_type=F32).astype(BF16)
    cw = u.shape[1] * S5_LW // n
    chunks = [(slice(c0, c0 + S5_LW), slice(n + c0, n + c0 + S5_LW)) for c0 in range(0, n, S5_LW)]

    def project_in(q):
        bu = jnp.dot(u_perm[:, q * cw:(q + 1) * cw], bd_ref[q], preferred_element_type=F32)
        x_scr[:, chunks[q][0]] = bu[:, 0:S5_LW]
        x_scr[:, chunks[q][1]] = bu[:, S5_LW:]

    project_in(0)
    end_r, end_i = [], []
    for q, (re, im) in enumerate(chunks):
        if q + 1 < len(chunks):
            project_in(q + 1)
        a1r, a1i = pw_ref[0:SUBLANES, re], pw_ref[0:SUBLANES, im]
        xr = jnp.zeros((SUBLANES, S5_LW), F32)
        xi = jnp.zeros((SUBLANES, S5_LW), F32)
        for j in range(jn):
            rows = slice(SUBLANES * j, SUBLANES * (j + 1))
            nr = a1r * xr - a1i * xi + x_scr[rows, re]
            ni = a1r * xi + a1i * xr + x_scr[rows, im]
            xr, xi = nr, ni
            x_scr[rows, re] = xr
            x_scr[rows, im] = xi
        end_r.append(xr)
        end_i.append(xi)
    er, ei = jnp.concatenate(end_r, axis=1), jnp.concatenate(end_i, axis=1)
    last = SUBLANES * (jn - 1)
    ajr, aji = pw_ref[last:last + 1, 0:n], pw_ref[last:last + 1, n:]
    pr, pi = st_scr[0:1, 0:n], st_scr[0:1, n:]
    cin_r, cin_i = [], []
    for s in range(SUBLANES):
        cin_r.append(pr)
        cin_i.append(pi)
        nr = er[s:s + 1] + ajr * pr - aji * pi
        ni = ei[s:s + 1] + ajr * pi + aji * pr
        pr, pi = nr, ni
    st_scr[0:1, 0:n] = pr
    st_scr[0:1, n:] = pi
    cr_all = jnp.concatenate(cin_r, axis=0)
    ci_all = jnp.concatenate(cin_i, axis=0)
    pack = 2 * SUBLANES
    ys = []
    for q, (re, im) in enumerate(chunks):
        cr = jnp.concatenate([cr_all[:, re]] * 2, axis=0).astype(BF16)
        ci = jnp.concatenate([ci_all[:, re]] * 2, axis=0).astype(BF16)
        xr_parts, xi_parts = [], []
        for m in range(SUBLANES * jn // pack):
            rows = slice(pack * m, pack * (m + 1))
            pjr, pji = pw16_ref[rows, re], pw16_ref[rows, im]
            xr_parts.append(x_scr[rows, re].astype(BF16) + (pjr * cr - pji * ci))
            xi_parts.append(x_scr[rows, im].astype(BF16) + (pjr * ci + pji * cr))
        xcat = jnp.concatenate([jnp.concatenate(xr_parts, axis=0), jnp.concatenate(xi_parts, axis=0)], axis=1)
        ys.append(jnp.dot(xcat, cd_ref[q], preferred_element_type=F32))

    y_perm = jnp.concatenate(ys, axis=1)
    y_hi = y_perm.astype(BF16)
    y_lo = (y_perm - y_hi.astype(F32)).astype(BF16)
    y = (jnp.dot(pmt_ref[...], y_hi, preferred_element_type=F32)
         + jnp.dot(pmt_ref[...], y_lo, preferred_element_type=F32))
    y = _gelu_tanh(y + dsk_ref[...] * u)
    z = jnp.dot(y.astype(BF16), gw_ref[...], preferred_element_type=F32) + gb_ref[...]
    o_ref[tile_rows, :] = (y * _sigmoid(z)).astype(o_ref.dtype)


def _s5_tables(a_re, a_im, log_dt, b_re, b_im, c_re, c_im):
    g, p = a_re.shape
    dt = jnp.exp(log_dt.astype(F32))[:, None]
    ar, ai = a_re.astype(F32), a_im.astype(F32)
    mag = jnp.exp(dt * ar)
    abar_r, abar_i = mag * jnp.cos(dt * ai), mag * jnp.sin(dt * ai)
    den = ar * ar + ai * ai
    fr = ((abar_r - 1.0) * ar + abar_i * ai) / den
    fi = (abar_i * ar - (abar_r - 1.0) * ai) / den
    br, bi = b_re.astype(F32), b_im.astype(F32)
    bbar_r = fr[..., None] * br - fi[..., None] * bi
    bbar_i = fr[..., None] * bi + fi[..., None] * br
    m = br.shape[-1]
    gc = S5_LW // p
    nq = g // gc
    eye = jnp.eye(gc, dtype=F32)
    dense_b = lambda t: jnp.einsum('qgpm,gh->qgmhp', t.reshape(nq, gc, p, m), eye).reshape(nq, gc * m, gc * p)
    bd = jnp.concatenate([dense_b(bbar_r), dense_b(bbar_i)], axis=2)
    dense_c = lambda t: jnp.einsum('qgmp,gh->qgphm', t.reshape(nq, gc, m, p), eye).reshape(nq, gc * p, gc * m)
    cd = jnp.concatenate([dense_c(c_re.astype(F32)), -dense_c(c_im.astype(F32))], axis=1)
    kk = jnp.arange(1, S5_J + 1, dtype=F32)[:, None, None]
    pmag = jnp.exp(kk * (dt * ar)[None])
    pw_r = (pmag * jnp.cos(kk * (dt * ai)[None])).reshape(S5_J, g * p)
    pw_i = (pmag * jnp.sin(kk * (dt * ai)[None])).reshape(S5_J, g * p)
    pw = jnp.repeat(jnp.concatenate([pw_r, pw_i], axis=1), SUBLANES, axis=0)
    return bd.astype(BF16), pw, cd.astype(BF16)


def _s5_layer(u, bsz, seq, a_re, a_im, log_dt, b_re, b_im, c_re, c_im, d_skip, glu_w, glu_b):
    t, width = u.shape
    n = a_re.shape[0] * a_re.shape[1]
    bd, pw, cd = _s5_tables(a_re, a_im, log_dt, b_re, b_im, c_re, c_im)
    tile = SUBLANES * S5_J
    step_rows = S5_TILES_PER_STEP * tile
    steps_per_seq = seq // step_rows
    const = lambda b, i: (0, 0)
    perm = _sublane_major_perm(tile)
    return pl.pallas_call(
        functools.partial(_s5_kernel, n_state=n),
        out_shape=jax.ShapeDtypeStruct((t, width), BF16),
        grid=(bsz, steps_per_seq),
        in_specs=[pl.BlockSpec((step_rows, width), lambda b, i: (b * steps_per_seq + i, 0)),
                  pl.BlockSpec((tile, tile), const),
                  pl.BlockSpec((tile, tile), const),
                  pl.BlockSpec(bd.shape, lambda b, i: (0, 0, 0)),
                  pl.BlockSpec((tile, 2 * n), const),
                  pl.BlockSpec((tile, 2 * n), const),
                  pl.BlockSpec(cd.shape, lambda b, i: (0, 0, 0)),
                  pl.BlockSpec((1, width), const),
                  pl.BlockSpec((width, width), const),
                  pl.BlockSpec((1, width), const)],
        out_specs=pl.BlockSpec((step_rows, width), lambda b, i: (b * steps_per_seq + i, 0)),
        scratch_shapes=[pltpu.VMEM((tile, 2 * n), F32), pltpu.VMEM((SUBLANES, 2 * n), F32)],
        compiler_params=_cparams(("arbitrary", "arbitrary")),
        name="s5_layer",
    )(u, jnp.asarray(perm, BF16), jnp.asarray(perm.T, BF16), bd, pw, pw.astype(BF16), cd,
      d_skip.reshape(1, width).astype(F32), glu_w.astype(BF16),
      glu_b.reshape(1, width).astype(F32))


def _finish(y, x_ref, mod_ref, pg_ref, out_ref, d_model):
    ms = jnp.mean(y * y, axis=-1, keepdims=True)
    yn = y * lax.rsqrt(ms + EPS) * pg_ref[...]
    gate_mod = mod_ref[0, :, 2 * d_model:3 * d_model]
    out_ref[...] = x_ref[...] + gate_mod * yn


def _split2(x):
    hi = x.astype(BF16)
    return hi, (x - hi.astype(F32)).astype(BF16)


def _out_ab_kernel(o0_ref, l0_ref, o1_ref, o2_ref, l1_ref, l2_ref, ob_ref, gate_ref, e_ref, un_ref, w_ref,
                   x_ref, mod_ref, pg_ref, out_ref, *, d_model):
    rows, aw = o0_ref.shape

    def unperm(o_ref, l_ref):
        pieces = []
        for piece in range(rows // PERM_ROWS):
            runs = slice(piece * PERM_RUN, (piece + 1) * PERM_RUN)
            o_p = o_ref[:, runs, :].reshape(PERM_ROWS, aw)
            l_p = l_ref[:, runs, :].reshape(PERM_ROWS, LANES)
            parts = jnp.concatenate([o_p] + list(_split2(l_p)), axis=1)
            pieces.append(jnp.dot(un_ref[...], parts, preferred_element_type=F32))
        moved = jnp.concatenate(pieces, axis=0)
        return moved[:, 0:aw], moved[:, aw:aw + LANES] + moved[:, aw + LANES:]

    o0, l0 = o0_ref[...].astype(F32), l0_ref[...]
    o1, l1 = unperm(o1_ref, l1_ref)
    o2, l2 = unperm(o2_ref, l2_ref)
    mx = jnp.maximum(jnp.maximum(l0, l1), l2)
    e0, e1, e2 = jnp.exp2(l0 - mx), jnp.exp2(l1 - mx), jnp.exp2(l2 - mx)
    inv_den = 1.0 / (e0 + e1 + e2)
    stacked = jnp.concatenate(list(_split2(e1 * inv_den)) + list(_split2(e2 * inv_den)), axis=0)
    wide = jnp.dot(stacked, e_ref[...], preferred_element_type=F32)
    term = lambda i: wide[i * rows:(i + 1) * rows]
    w1 = term(0) + term(1)
    w2 = term(2) + term(3)
    o_a = (1.0 - w1 - w2) * o0 + w1 * o1 + w2 * o2
    gate = gate_ref[...].astype(F32)
    sg = gate * _sigmoid(gate)
    aw = o_a.shape[-1]
    y = jnp.dot((o_a * sg[:, :aw]).astype(BF16), w_ref[0:aw, :], preferred_element_type=F32)
    y = y + jnp.dot((ob_ref[...].astype(F32) * sg[:, aw:]).astype(BF16), w_ref[aw:, :],
                    preferred_element_type=F32)
    _finish(y, x_ref, mod_ref, pg_ref, out_ref, d_model)


def _out_c_kernel(o_ref, gate_ref, w_ref, x_ref, mod_ref, pg_ref, out_ref, *, d_model):
    gate = gate_ref[...].astype(F32)
    o = o_ref[...].astype(F32) * (gate * _sigmoid(gate))
    y = jnp.dot(o.astype(BF16), w_ref[...], preferred_element_type=F32)
    _finish(y, x_ref, mod_ref, pg_ref, out_ref, d_model)


def _out_proj(kern, row_inputs, const_inputs, w_bf16, x2, mod3, post_g, seq, name, tile=ROW_TILE):
    t, d = x2.shape
    tiles_per_seq = seq // tile
    tiles_per_span = SPAN // tile

    def row_spec(a):
        if a.ndim == 2:
            return pl.BlockSpec((tile, a.shape[1]), lambda i: (i, 0))
        return pl.BlockSpec((None, SPAN_RES, tile // SPAN_RES, a.shape[-1]),
                            lambda i: (i // tiles_per_span, 0, i % tiles_per_span, 0))

    const_spec = lambda a: pl.BlockSpec(a.shape, lambda i: (0, 0))
    return pl.pallas_call(
        functools.partial(kern, d_model=d),
        out_shape=jax.ShapeDtypeStruct((t, d), F32),
        grid=(t // tile,),
        in_specs=([row_spec(a) for a in row_inputs] + [const_spec(a) for a in const_inputs]
                  + [const_spec(w_bf16), row_spec(x2),
                     pl.BlockSpec((1, 1, 3 * d), lambda i: (i // tiles_per_seq, 0, 0)),
                     pl.BlockSpec((1, d), lambda i: (0, 0))]),
        out_specs=pl.BlockSpec((tile, d), lambda i: (i, 0)),
        compiler_params=_cparams(("arbitrary",)),
        name=name,
    )(*row_inputs, *const_inputs, w_bf16, x2, mod3, post_g.reshape(1, d))


def _gdn_prep_kernel(x_ref, halo_ref, br_ref, cw_ref, alog_ref, dtb_ref,
                     w_ref, u_ref, qg_ref, kdt_ref, aqk_ref, dec_ref, xs_scr):
    n_all = GDN_PREP_BLOCKS * GDN_STEP
    first = pl.program_id(1) == 0
    halo = halo_ref[...]
    xs_scr[0:SUBLANES, :] = jnp.where(first, jnp.zeros_like(halo), halo)
    xs_scr[SUBLANES:SUBLANES + n_all, :] = x_ref[...]
    cw = cw_ref[...]
    ext = xs_scr[...].reshape(n_all // SUBLANES + 1, SUBLANES, x_ref.shape[1])
    first_sublane = lax.broadcasted_iota(jnp.int32, ext.shape, 1) == 0

    def delay(y):
        rot = pltpu.roll(y, 1, axis=1)
        prev = jnp.concatenate([rot[:1], rot[:-1]], axis=0)
        return jnp.where(first_sublane, prev, rot)

    acc = ext * cw[0:1]
    for j in range(1, C_CONV):
        acc = delay(acc) + ext * cw[j:j + 1]
    conv = acc[1:].reshape(n_all, x_ref.shape[1])
    act = conv * _sigmoid(conv)
    _gdn_prep_blocks(act, br_ref, alog_ref, dtb_ref, w_ref, u_ref, qg_ref, kdt_ref, aqk_ref, dec_ref)


def _gdn_prep_blocks(act, br_ref, alog_ref, dtb_ref, w_ref, u_ref, qg_ref, kdt_ref, aqk_ref, dec_ref):
    n = GDN_STEP
    c = GDN_CHUNK
    row = lax.broadcasted_iota(jnp.int32, (n, n), 0)
    col = lax.broadcasted_iota(jnp.int32, (n, n), 1)
    cum = jnp.logical_and((row // c) == (col // c), row >= col).astype(F32)
    prow = lax.broadcasted_iota(jnp.int32, (c, n), 0)
    plane = lax.broadcasted_iota(jnp.int32, (c, n), 1)
    left = plane < c
    pcol = jnp.where(left, plane, plane - c)
    tril = prow >= pcol
    strict = prow > pcol
    eye = (prow == pcol).astype(F32)
    pair = lambda full: jnp.where(left, full[0:c], full[c:])

    def blockdiag(p):
        zero = jnp.zeros_like(p)
        return jnp.concatenate([jnp.where(left, p, zero), jnp.where(left, zero, p)], axis=0)

    first_chunk = lax.broadcasted_iota(jnp.int32, (n, 1), 0) < GDN_CHUNK
    dec_row = lax.broadcasted_iota(jnp.int32, (SUBLANES, LANES), 0)
    heads = range(C_HEADS)
    lanes = [slice(h * LANES, (h + 1) * LANES) for h in heads]

    def gates(blk):
        rows = slice(blk * n, (blk + 1) * n)
        br = br_ref[rows, :]
        xg = br + dtb_ref[...]
        softplus = jnp.maximum(xg, 0.0) + jnp.log(1.0 + jnp.exp(-jnp.abs(xg)))
        g_all = -jnp.exp(alog_ref[...]) * softplus
        gc_all = jnp.dot(cum, g_all, preferred_element_type=F32, precision=HI)
        g_last = jnp.where(first_chunk, gc_all[c - 1:c], gc_all[n - 1:n])
        return dict(rows=rows, beta=_sigmoid(br), gc=gc_all, gc_t=gc_all.T, eg=jnp.exp(gc_all),
                    tail=jnp.exp(g_last - gc_all))

    def first_stage(ctx, blk, h):
        rows = ctx["rows"]
        gcol = ctx["gc"][:, C_HEADS + h:C_HEADS + h + 1]
        grow = ctx["gc_t"][C_HEADS + h:C_HEADS + h + 1, :]
        eg = ctx["eg"][:, C_HEADS + h:C_HEADS + h + 1]
        gcol_p = jnp.where(left, gcol[0:c], gcol[c:])
        decay = jnp.where(tril, jnp.exp(jnp.where(tril, gcol_p - grow, 0.0)), 0.0)
        q = act[rows, h * C_DK:(h + 1) * C_DK]
        k = act[rows, (C_HEADS + h) * C_DK:(C_HEADS + h + 1) * C_DK]
        v = act[rows, 2 * C_HEADS * C_DK + h * C_DV:2 * C_HEADS * C_DK + (h + 1) * C_DV]
        q = q * (lax.rsqrt(jnp.sum(q * q, axis=-1, keepdims=True) + EPS) * (C_DK ** -0.5))
        k = k * lax.rsqrt(jnp.sum(k * k, axis=-1, keepdims=True) + EPS)
        beta = ctx["beta"][:, h:h + 1]
        kb = k * beta
        k16 = k.astype(BF16)
        a = jnp.where(strict, pair(lax.dot_general(kb.astype(BF16), k16, _NT, preferred_element_type=F32)) * decay,
                      0.0)
        aqk = pair(lax.dot_general(q.astype(BF16), k16, _NT, preferred_element_type=F32)) * decay
        aqk_ref[rows, lanes[h]] = blockdiag(aqk).astype(BF16)
        qg_ref[rows, lanes[h]] = (q * eg).astype(BF16)
        kdt_ref[rows, lanes[h]] = (k * ctx["tail"][:, C_HEADS + h:C_HEADS + h + 1]).T.astype(BF16)
        dec_ref[blk, :, lanes[h]] = jnp.where(dec_row < SUBLANES // 2, eg[GDN_CHUNK - 1:GDN_CHUNK], eg[n - 1:n])
        rhs = jnp.concatenate([v * beta, kb * eg], axis=1).astype(BF16)
        return dict(a=a, inv=eye - a, rhs=rhs)

    n_blocks = act.shape[0] // n
    first_ctx = gates(0)
    state = [first_stage(first_ctx, 0, h) for h in heads]
    for blk in range(n_blocks):
        nxt_ctx = gates(blk + 1) if blk + 1 < n_blocks else None
        nxt = []

        def side_work(hs, blk=blk, nxt=nxt, nxt_ctx=nxt_ctx):
            if nxt_ctx is not None:
                nxt.extend(first_stage(nxt_ctx, blk + 1, h) for h in hs)

        invs = [st["inv"] for st in state]
        pows = []
        for st in state:
            pows.append(jnp.dot(st["a"].astype(BF16), blockdiag(st["a"]).astype(BF16), preferred_element_type=F32))
        side_work(heads[0:2])
        for level in range(4):
            for h in heads:
                both = jnp.dot(jnp.concatenate([invs[h], pows[h]], axis=0).astype(BF16),
                               blockdiag(pows[h]).astype(BF16), preferred_element_type=F32)
                invs[h] = invs[h] + both[0:c]
                pows[h] = both[c:]
            side_work(heads[2 + level:3 + level])
        for h in heads:
            invs[h] = invs[h] + jnp.dot(invs[h].astype(BF16), blockdiag(pows[h]).astype(BF16),
                                        preferred_element_type=F32)
        side_work(heads[6:8])
        rows = slice(blk * n, (blk + 1) * n)
        for h in heads:
            uw = jnp.dot(blockdiag(invs[h]).astype(BF16), state[h]["rhs"], preferred_element_type=F32)
            u_ref[rows, lanes[h]] = uw[:, 0:C_DV].astype(u_ref.dtype)
            w_ref[rows, lanes[h]] = uw[:, C_DV:].astype(BF16)
        state = nxt


def _gdn_rec_kernel(w_ref, u_ref, qg_ref, kdt_ref, aqk_ref, dec_ref, ng_ref, o_ref, s_scr):
    @pl.when(pl.program_id(0) == 0)
    def _():
        s_scr[...] = jnp.zeros_like(s_scr)

    half = SUBLANES // 2
    zeros = jnp.zeros((GDN_CHUNK, C_DV), BF16)
    lanes = [slice(h * LANES, (h + 1) * LANES) for h in range(C_HEADS)]
    chains = [(b, h) for b in range(w_ref.shape[0]) for h in range(C_HEADS)]
    states = [s_scr[b * C_HEADS + h] for b, h in chains]
    for blk in range(GDN_REC_BLOCKS):
        blk_rows = slice(blk * GDN_STEP, (blk + 1) * GDN_STEP)
        for j in range(GDN_STEP // GDN_CHUNK):
            rows = slice(blk * GDN_STEP + j * GDN_CHUNK, blk * GDN_STEP + (j + 1) * GDN_CHUNK)
            wss = []
            for c, (b, h) in enumerate(chains):
                wq = jnp.concatenate([w_ref[b, rows, lanes[h]], qg_ref[b, rows, lanes[h]]], axis=0)
                wss.append(jnp.dot(wq, states[c].astype(BF16), preferred_element_type=F32))
            for c, (b, h) in enumerate(chains):
                v_new = (u_ref[b, rows, lanes[h]].astype(F32) - wss[c][0:GDN_CHUNK]).astype(BF16)
                v_pad = jnp.concatenate([v_new, zeros] if j == 0 else [zeros, v_new], axis=0)
                both = jnp.dot(jnp.concatenate([aqk_ref[b, rows, lanes[h]], kdt_ref[b, blk_rows, lanes[h]]], axis=0),
                               v_pad, preferred_element_type=F32)
                o = wss[c][GDN_CHUNK:] + both[0:GDN_CHUNK]
                dec = dec_ref[b, blk, j * half:j * half + 1, lanes[h]]
                states[c] = states[c] * dec + both[GDN_CHUNK:]
                ms = jnp.mean(o * o, axis=-1, keepdims=True)
                o_ref[b, rows, lanes[h]] = (o * lax.rsqrt(ms + EPS) * ng_ref[...]).astype(o_ref.dtype)
    for c, (b, h) in enumerate(chains):
        s_scr[b * C_HEADS + h] = states[c]


def _gdn_core(qkv_pre, br, bsz, seq, conv_w, a_log, dt_bias, norm_g):
    t, qkv_w = qkv_pre.shape
    hw = C_HEADS * C_DV
    steps_per_seq = seq // GDN_STEP
    prep_rows = GDN_PREP_BLOCKS * GDN_STEP
    prep_steps = steps_per_seq // GDN_PREP_BLOCKS
    halo_blocks = prep_rows // SUBLANES
    pad_row = lambda vec: jnp.zeros((1, LANES), F32).at[0, C_HEADS:2 * C_HEADS].set(vec.astype(F32))
    tok = lambda b, i: (b * prep_steps + i, 0)
    const = lambda b, i: (0, 0)
    tok_spec = pl.BlockSpec((prep_rows, hw), tok)
    dec_spec = pl.BlockSpec((GDN_PREP_BLOCKS, SUBLANES, hw), lambda b, i: (b * prep_steps + i, 0, 0))
    w, u, qg, kdt, aqk, dec = pl.pallas_call(
        _gdn_prep_kernel,
        out_shape=[jax.ShapeDtypeStruct((t, hw), BF16), jax.ShapeDtypeStruct((t, hw), BF16),
                   jax.ShapeDtypeStruct((t, hw), BF16), jax.ShapeDtypeStruct((t, hw), BF16),
                   jax.ShapeDtypeStruct((t, hw), BF16),
                   jax.ShapeDtypeStruct((t // GDN_STEP, SUBLANES, hw), F32)],
        grid=(bsz, prep_steps),
        in_specs=[pl.BlockSpec((prep_rows, qkv_w), tok),
                  pl.BlockSpec((SUBLANES, qkv_w),
                               lambda b, i: (jnp.maximum((b * prep_steps + i) * halo_blocks - 1, 0), 0)),
                  pl.BlockSpec((prep_rows, LANES), tok),
                  pl.BlockSpec((C_CONV, qkv_w), const),
                  pl.BlockSpec((1, LANES), const),
                  pl.BlockSpec((1, LANES), const)],
        out_specs=[tok_spec, tok_spec, tok_spec, tok_spec, tok_spec, dec_spec],
        scratch_shapes=[pltpu.VMEM((SUBLANES + prep_rows, qkv_w), F32)],
        compiler_params=_cparams(("arbitrary", "arbitrary")),
        name="gdn_prep",
    )(qkv_pre, qkv_pre, br, conv_w.astype(F32), pad_row(a_log), pad_row(dt_bias))
    rec_steps = steps_per_seq // GDN_REC_BLOCKS
    seq_view = lambda a: a.reshape(bsz, seq, hw)
    rec_tok = pl.BlockSpec((bsz, GDN_REC_BLOCKS * GDN_STEP, hw), lambda i: (0, i, 0))
    rec_dec = pl.BlockSpec((bsz, GDN_REC_BLOCKS, SUBLANES, hw), lambda i: (0, i, 0, 0))
    o = pl.pallas_call(
        _gdn_rec_kernel,
        out_shape=jax.ShapeDtypeStruct((bsz, seq, hw), BF16),
        grid=(rec_steps,),
        in_specs=[rec_tok, rec_tok, rec_tok, rec_tok, rec_tok, rec_dec,
                  pl.BlockSpec((1, C_DV), lambda i: (0, 0))],
        out_specs=rec_tok,
        scratch_shapes=[pltpu.VMEM((bsz * C_HEADS, C_DK, C_DV), F32)],
        compiler_params=_cparams(("arbitrary",)),
        name="gdn_recurrence",
    )(seq_view(w), seq_view(u), seq_view(qg), seq_view(kdt), seq_view(aqk),
      dec.reshape(bsz, steps_per_seq, SUBLANES, hw), norm_g.reshape(1, C_DV).astype(F32))
    return o.reshape(t, hw)


def _ab_layer(x2, mod3, bsz, seq, pre_g, post_g, rel_bias, w_in, w_out, s5_params):
    b_width = s5_params[7].shape[-1]
    assert DILATED_CONFIGS[0][1] == 1 and all(SPAN_RES % dl == 0 for _, dl in DILATED_CONFIGS) and seq % SPAN == 0
    head_expand = jnp.asarray(np.arange(LANES)[:, None] == (np.arange(A_WIDTH)[None, :] // A_HEAD_DIM), BF16)
    unperm = jnp.asarray(_span_perm().T, BF16)
    splits = ((0, A_WIDTH, A_HEAD_DIM ** -0.5 * LOG2_E), (A_WIDTH, A_WIDTH, 1.0), (2 * A_WIDTH, A_WIDTH, 1.0),
              (3 * A_WIDTH, b_width, 1.0), (3 * A_WIDTH + b_width, A_WIDTH + b_width, 1.0))
    q, k, v, u, gate, qs, ks, vs = _in_proj(x2, mod3, pre_g, w_in.astype(BF16), splits,
                                            (BF16, BF16, BF16, F32, BF16), seq, n_span=3, tile=AB_TILE)
    outs, lses = [], []
    bias_tabs = _attn_bias_tables(rel_bias)
    for cfg, (_, dil) in enumerate(DILATED_CONFIGS):
        qkv = (q, k, v) if dil == 1 else (qs, ks, vs)
        o_c, lse_c = _dilated_attention_one(*qkv, bias_tabs, cfg, bsz, seq, dil)
        outs.append(o_c)
        lses.append(lse_c)
    o_b = _s5_layer(u, bsz, seq, *s5_params)
    row_inputs = [outs[0], lses[0], outs[1], outs[2], lses[1], lses[2], o_b, gate]
    return _out_proj(_out_ab_kernel, row_inputs, [head_expand, unperm], w_out.astype(BF16),
                     x2, mod3, post_g, seq, "out_proj_ab", tile=AB_TILE)


def _gdn_layer(x2, mod3, bsz, seq, pre_g, post_g, w_in, conv_w, a_log, dt_bias, norm_g, w_out):
    d = x2.shape[1]
    qkv_w = 2 * C_HEADS * C_DK + C_HEADS * C_DV
    gate_w = C_HEADS * C_DV
    w_pad = jnp.concatenate(
        [w_in, jnp.zeros((d, LANES - (w_in.shape[1] - qkv_w - gate_w)), w_in.dtype)], axis=1).astype(BF16)
    splits = ((0, qkv_w, 1.0), (qkv_w, gate_w, 1.0), (qkv_w + gate_w, LANES, 1.0))
    qkv_pre, gate, br = _in_proj(x2, mod3, pre_g, w_pad, splits, (F32, BF16, F32), seq)
    o = _gdn_core(qkv_pre, br, bsz, seq, conv_w, a_log, dt_bias, norm_g)
    return _out_proj(_out_c_kernel, [o, gate], [], w_out.astype(BF16), x2, mod3, post_g, seq, "out_proj_c",
                     tile=OUT_C_TILE)


def kernel(x, c, ada_w, ada_b, pre_g, post_g, rel_bias, ab_w_in, ab_w_out, s5_a_re, s5_a_im, s5_log_dt, s5_b_re, s5_b_im, s5_c_re, s5_c_im, s5_d, s5_glu_w, s5_glu_b, gdn_w_in, gdn_conv, gdn_a_log, gdn_dt_bias, gdn_norm_g, gdn_w_out):
    bsz, seq, d = x.shape
    depth = ada_w.shape[0]
    assert seq % ROW_TILE == 0 and seq % AB_TILE == 0 and seq % OUT_C_TILE == 0
    assert seq % (SUBLANES * S5_J * S5_TILES_PER_STEP) == 0 and AB_TILE % PERM_ROWS == 0 and SPAN % AB_TILE == 0
    assert seq % (GDN_STEP * GDN_PREP_BLOCKS) == 0 and seq % (GDN_STEP * GDN_REC_BLOCKS) == 0
    x2 = x.reshape(bsz * seq, d)
    mod = _adaln_mod(c, ada_w, ada_b)
    for layer in range(depth):
        j = layer // 2
        mod3 = mod[layer].reshape(bsz, 1, 3 * d)
        if layer % 2 == 0:
            s5_params = (s5_a_re[j], s5_a_im[j], s5_log_dt[j], s5_b_re[j], s5_b_im[j], s5_c_re[j], s5_c_im[j],
                         s5_d[j], s5_glu_w[j], s5_glu_b[j])
            x2 = _ab_layer(x2, mod3, bsz, seq, pre_g[layer], post_g[layer], rel_bias, ab_w_in[j], ab_w_out[j],
                           s5_params)
        else:
            x2 = _gdn_layer(x2, mod3, bsz, seq, pre_g[layer], post_g[layer], gdn_w_in[j], gdn_conv[j],
                            gdn_a_log[j], gdn_dt_bias[j], gdn_norm_g[j], gdn_w_out[j])
    return x2.reshape(bsz, seq, d)
```

```python
import functools
import math

import numpy as np
import jax
import jax.numpy as jnp
from jax import lax
from jax.experimental import pallas as pl
from jax.experimental.pallas import tpu as pltpu

F32 = jnp.float32
BF16 = jnp.bfloat16
HI = lax.Precision.HIGHEST

EPS = 1e-6
A_HEADS = 8
A_HEAD_DIM = 64
A_WIDTH = A_HEADS * A_HEAD_DIM
A_BLOCK = 128
DILATED_CONFIGS = ((128, 1), (512, 4), (2048, 16))
REL_BUCKETS = 32
REL_MAX_DIST = 2048
C_HEADS = 8
C_DK = 128
C_DV = 128
C_CONV = 4
MASK_NEG = -1e30
LOG2_E = math.log2(math.e)

LANES = 128
SUBLANES = 8
VMEM_LIMIT = 56 * 1024 * 1024
PROJ_TILE = 1024
COL_CHUNK = 512
SPAN_RES = max(dl for _, dl in DILATED_CONFIGS)
SPAN = A_BLOCK * SPAN_RES
PERM_ROWS = 256
PERM_RUN = PERM_ROWS // SPAN_RES
ATTN_BLOCKS_PER_STEP = 8
S5_J = 32
S5_LW = 512
S5_TILES_PER_STEP = 4
GDN_STEP = 128
GDN_CHUNK = 64
GDN_PREP_BLOCKS = 4
GDN_REC_BLOCKS = 2

_NT = (((1,), (1,)), ((), ()))


def _cparams(sem):
    return pltpu.CompilerParams(dimension_semantics=sem, vmem_limit_bytes=VMEM_LIMIT)


def _sigmoid(x):
    return 1.0 / (1.0 + jnp.exp(-x))


def _mod_kernel(c_ref, w_ref, b_ref, o_ref):
    c = c_ref[...]
    ca = c * _sigmoid(c)
    o_ref[0] = jnp.dot(ca.astype(BF16), w_ref[0].astype(BF16), preferred_element_type=F32) + b_ref[0]


def _adaln_mod(c, ada_w, ada_b):
    depth, d, d3 = ada_w.shape
    bsz = c.shape[0]
    return pl.pallas_call(
        _mod_kernel,
        out_shape=jax.ShapeDtypeStruct((depth, bsz, d3), F32),
        grid=(depth, d3 // d),
        in_specs=[pl.BlockSpec((bsz, d), lambda l, j: (0, 0)),
                  pl.BlockSpec((1, d, d), lambda l, j: (l, 0, j)),
                  pl.BlockSpec((1, 1, d), lambda l, j: (l, 0, j))],
        out_specs=pl.BlockSpec((1, bsz, d), lambda l, j: (l, 0, j)),
        compiler_params=_cparams(("arbitrary", "arbitrary")),
        name="adaln_mod",
    )(c, ada_w, ada_b.reshape(depth, 1, d3))


def _span_perm():
    rho = np.arange(PERM_ROWS)
    nat = SPAN_RES * (rho % PERM_RUN) + rho // PERM_RUN
    return (nat[:, None] == np.arange(PERM_ROWS)[None, :]).astype(np.float32)


def _modulated_norm(x_ref, mod_ref, g_ref, d_model):
    x = x_ref[...]
    ms = jnp.mean(x * x, axis=-1, keepdims=True)
    y = x * lax.rsqrt(ms + EPS) * g_ref[...]
    shift = mod_ref[0, :, 0:d_model]
    scale = mod_ref[0, :, d_model:2 * d_model]
    return (y * (1.0 + scale) + shift).astype(BF16)


def _sublane_major_perm(tile):
    rho = np.arange(tile)
    src = (rho % SUBLANES) * (tile // SUBLANES) + rho // SUBLANES
    return (src[:, None] == np.arange(tile)[None, :]).astype(np.float32)


def _in_proj_kernel(x_ref, mod_ref, g_ref, w_ref, *rest, splits, d_model, n_span):
    pm_ref = rest[0] if n_span else None
    out_refs = rest[1:] if n_span else rest
    span_refs = out_refs[len(splits):]
    h = _modulated_norm(x_ref, mod_ref, g_ref, d_model)
    for idx, ((c0, width, mult), o_ref) in enumerate(zip(splits, out_refs)):
        for cc in range(0, width, COL_CHUNK):
            cw = min(COL_CHUNK, width - cc)
            acc = jnp.dot(h, w_ref[:, c0 + cc:c0 + cc + cw], preferred_element_type=F32)
            if mult != 1.0:
                acc = acc * mult
            val = acc.astype(o_ref.dtype)
            o_ref[:, cc:cc + cw] = val
            if idx < n_span:
                for piece in range(x_ref.shape[0] // PERM_ROWS):
                    moved = jnp.dot(pm_ref[...], val[piece * PERM_ROWS:(piece + 1) * PERM_ROWS],
                                    preferred_element_type=F32).astype(BF16)
                    span_refs[idx][:, piece * PERM_RUN:(piece + 1) * PERM_RUN, cc:cc + cw] = (
                        moved.reshape(SPAN_RES, PERM_RUN, cw))


def _in_proj(x2, mod3, gain, w_bf16, splits, out_dtypes, seq, n_span=0):
    t, d = x2.shape
    tile = PROJ_TILE
    tiles_per_seq = seq // tile
    tiles_per_span = SPAN // tile
    n_w = w_bf16.shape[1]
    row = lambda wd: pl.BlockSpec((tile, wd), lambda i: (i, 0))
    span_shape = lambda wd: jax.ShapeDtypeStruct((t // SPAN, SPAN_RES, A_BLOCK, wd), BF16)
    span_spec = lambda wd: pl.BlockSpec((None, SPAN_RES, tile // SPAN_RES, wd),
                                        lambda i: (i // tiles_per_span, 0, i % tiles_per_span, 0))
    perm_in = [jnp.asarray(_span_perm(), BF16)] if n_span else []
    perm_spec = [pl.BlockSpec((PERM_ROWS, PERM_ROWS), lambda i: (0, 0))] if n_span else []
    return pl.pallas_call(
        functools.partial(_in_proj_kernel, splits=splits, d_model=d, n_span=n_span),
        out_shape=([jax.ShapeDtypeStruct((t, wd), dt) for (_, wd, _), dt in zip(splits, out_dtypes)]
                   + [span_shape(wd) for (_, wd, _) in splits[:n_span]]),
        grid=(t // tile,),
        in_specs=[row(d),
                  pl.BlockSpec((1, 1, 3 * d), lambda i: (i // tiles_per_seq, 0, 0)),
                  pl.BlockSpec((1, d), lambda i: (0, 0)),
                  pl.BlockSpec((d, n_w), lambda i: (0, 0), pipeline_mode=pl.Buffered(1))] + perm_spec,
        out_specs=[row(wd) for (_, wd, _) in splits] + [span_spec(wd) for (_, wd, _) in splits[:n_span]],
        compiler_params=_cparams(("arbitrary",)),
        name="in_proj",
    )(x2, mod3, gain.reshape(1, d), w_bf16, *perm_in)


def _t5_bucket(dist):
    dist = np.maximum(dist, 0)
    max_exact = REL_BUCKETS // 2
    large = max_exact + (np.log(np.maximum(dist, 1) / max_exact)
                         / math.log(REL_MAX_DIST / max_exact) * (REL_BUCKETS - max_exact)).astype(np.int32)
    large = np.minimum(large, REL_BUCKETS - 1)
    return np.where(dist < max_exact, dist, large).astype(np.int32)


def _bias_kernel(rb_ref, bucket_ref, mask_ref, o_ref):
    def body(r, carry):
        rows = pl.ds(pl.multiple_of(r * SUBLANES, SUBLANES), SUBLANES)
        bk = bucket_ref[0, rows, :]
        accs = [jnp.zeros(bk.shape, F32) for _ in range(A_HEADS)]
        for b in range(REL_BUCKETS):
            eq = bk == b
            accs = [jnp.where(eq, rb_ref[b, h], acc) for h, acc in enumerate(accs)]
        for f in range(2):
            keep = mask_ref[0, f, rows, :] != 0
            for h in range(A_HEADS):
                o_ref[0, f, h, rows, :] = jnp.where(keep, accs[h] * LOG2_E, MASK_NEG)
        return carry
    lax.fori_loop(0, A_BLOCK // SUBLANES, body, 0)


def _attn_bias_tables(rel_bias):
    qi = np.arange(A_BLOCK)[:, None]
    kj = np.arange(2 * A_BLOCK)[None, :]
    rel = qi + A_BLOCK - kj
    buckets, masks = [], []
    for window, dil in DILATED_CONFIGS:
        band = (rel >= 0) & (rel <= window // dil)
        bucket = _t5_bucket(rel * dil)
        mask = np.stack([band & (kj >= A_BLOCK), band]).astype(np.int32)
        if dil > 1:
            runs = SPAN_RES // dil
            run = A_BLOCK // runs
            rho = np.arange(A_BLOCK)
            sub = runs * (rho % run) + rho // run
            keys = np.concatenate([sub, A_BLOCK + sub])
            bucket = bucket[sub][:, keys]
            mask = mask[:, sub][:, :, keys]
        buckets.append(bucket)
        masks.append(mask)
    n_cfg = len(DILATED_CONFIGS)
    return pl.pallas_call(
        _bias_kernel,
        out_shape=jax.ShapeDtypeStruct((n_cfg, 2, A_HEADS, A_BLOCK, 2 * A_BLOCK), F32),
        grid=(n_cfg,),
        in_specs=[pl.BlockSpec(memory_space=pltpu.SMEM),
                  pl.BlockSpec((1, A_BLOCK, 2 * A_BLOCK), lambda c: (c, 0, 0)),
                  pl.BlockSpec((1, 2, A_BLOCK, 2 * A_BLOCK), lambda c: (c, 0, 0, 0))],
        out_specs=pl.BlockSpec((1, 2, A_HEADS, A_BLOCK, 2 * A_BLOCK), lambda c: (c, 0, 0, 0, 0)),
        compiler_params=_cparams(("arbitrary",)),
        name="attn_bias",
    )(rel_bias.astype(F32), jnp.asarray(np.stack(buckets)), jnp.asarray(np.stack(masks)))


def _attn_block(q, kcat, vcat, bias_ref):
    lane = lax.broadcasted_iota(jnp.int32, (A_BLOCK, LANES), 1)
    low = lane < A_HEAD_DIM
    lse_tile = jnp.zeros((A_BLOCK, LANES), F32)
    zero = jnp.zeros((A_BLOCK, LANES), BF16)
    heads = range(A_HEADS)
    pair = [slice((h // 2) * LANES, (h // 2 + 1) * LANES) for h in heads]
    scores = []
    for h in heads:
        qm = jnp.where(low if h % 2 == 0 else jnp.logical_not(low), q[:, pair[h]], zero)
        scores.append(lax.dot_general(qm, kcat[:, pair[h]], _NT, preferred_element_type=F32) + bias_ref[0, h])
    probs, inv_l = [], []
    for h in heads:
        m = jnp.max(scores[h], axis=-1, keepdims=True)
        p = jnp.exp2(scores[h] - m)
        l = jnp.sum(p, axis=-1, keepdims=True)
        probs.append(p.astype(BF16))
        inv_l.append(1.0 / l)
        lse_tile = jnp.where(lane == h, m + jnp.log2(l), lse_tile)
    outs = [jnp.dot(probs[h], vcat[:, pair[h]], preferred_element_type=F32) * inv_l[h] for h in heads]
    o = jnp.concatenate([jnp.where(low, outs[2 * hp], outs[2 * hp + 1]) for hp in range(A_HEADS // 2)], axis=1)
    return o, lse_tile


def _attn_kernel(q_ref, kp_ref, kc_ref, vp_ref, vc_ref, bias0_ref, bias_ref, o_ref, lse_ref, *, per_lead):
    def sub(ref, rr, i):
        if ref.ndim == 2:
            return ref[i * A_BLOCK:(i + 1) * A_BLOCK, :]
        run = ref.shape[3] // per_lead
        return ref[i // per_lead, :, rr, (i % per_lead) * run:(i % per_lead + 1) * run, :]

    def put(ref, rr, i, val):
        if ref.ndim == 2:
            ref[i * A_BLOCK:(i + 1) * A_BLOCK, :] = val.astype(ref.dtype)
        else:
            run = ref.shape[3] // per_lead
            ref[i // per_lead, :, rr, (i % per_lead) * run:(i % per_lead + 1) * run, :] = (
                val.astype(ref.dtype).reshape(ref.shape[1], run, ref.shape[4]))

    flat = lambda v: v.reshape(A_BLOCK, v.shape[-1])
    token_order = q_ref.ndim == 2
    n_blocks = q_ref.shape[0] // A_BLOCK if token_order else q_ref.shape[0] * per_lead
    for rr in range(1 if token_order else q_ref.shape[2]):
        k_prev = flat(kp_ref[...] if token_order else kp_ref[0, :, rr])
        v_prev = flat(vp_ref[...] if token_order else vp_ref[0, :, rr])
        for i in range(n_blocks):
            k_cur, v_cur = flat(sub(kc_ref, rr, i)), flat(sub(vc_ref, rr, i))
            o, lse_tile = _attn_block(flat(sub(q_ref, rr, i)), jnp.concatenate([k_prev, k_cur], axis=0),
                                      jnp.concatenate([v_prev, v_cur], axis=0),
                                      bias0_ref if i == 0 else bias_ref)
            put(o_ref, rr, i, o)
            put(lse_ref, rr, i, lse_tile)
            k_prev, v_prev = k_cur, v_cur


def _dilated_attention_one(q, k, v, bias_tabs, cfg, bsz, seq, dil):
    nb = seq // dil // A_BLOCK
    step = min(ATTN_BLOCKS_PER_STEP, nb)
    w = A_WIDTH
    if dil == 1:
        per_lead = n_res = 1
        view = lambda a: a.reshape(bsz, seq, a.shape[-1])
        block = lambda wd: (None, step * A_BLOCK, wd)
        block_prev = lambda wd: (None, A_BLOCK, wd)
        cur = lambda b, r, m: (b, m, 0)
        prev = lambda b, r, m: (b, jnp.maximum(m * step - 1, 0), 0)
        out_shape = lambda wd, dt: jax.ShapeDtypeStruct((bsz, seq, wd), dt)
    else:
        runs = SPAN_RES // dil
        run = A_BLOCK // runs
        spans_per_seq = seq // SPAN
        per_lead = min(step, runs)
        lead = step // per_lead
        n_res = max(1, ATTN_BLOCKS_PER_STEP // step)
        assert runs % per_lead == 0 and spans_per_seq % lead == 0 and dil % n_res == 0
        view = lambda a: a.reshape(a.shape[0], runs, dil, A_BLOCK, a.shape[-1])
        block = lambda wd: (lead, runs, n_res, per_lead * run, wd)
        block_prev = lambda wd: (1, runs, n_res, run, wd)
        cur = lambda b, r, m: ((b * spans_per_seq + (m * step) // runs) // lead, 0, r, ((m * step) % runs) // per_lead, 0)

        def prev(b, r, m):
            n = jnp.maximum(m * step - 1, 0)
            return (b * spans_per_seq + n // runs, 0, r, n % runs, 0)

        out_shape = lambda wd, dt: jax.ShapeDtypeStruct((bsz * spans_per_seq, runs, dil, A_BLOCK, wd), dt)
    blk = pl.BlockSpec(block(w), cur)
    blk_prev = pl.BlockSpec(block_prev(w), prev)
    bias_block = (None, 1, A_HEADS, A_BLOCK, 2 * A_BLOCK)
    o, lse = pl.pallas_call(
        functools.partial(_attn_kernel, per_lead=per_lead),
        out_shape=[out_shape(w, BF16), out_shape(LANES, F32)],
        grid=(bsz, dil // n_res, nb // step),
        in_specs=[blk, blk_prev, blk, blk_prev, blk,
                  pl.BlockSpec(bias_block, lambda b, r, m: (cfg, jnp.minimum(m, 1), 0, 0, 0)),
                  pl.BlockSpec(bias_block, lambda b, r, m: (cfg, 1, 0, 0, 0))],
        out_specs=[pl.BlockSpec(block(w), cur), pl.BlockSpec(block(LANES), cur)],
        compiler_params=_cparams(("arbitrary", "arbitrary", "arbitrary")),
        name=f"dilated_attn_d{dil}",
    )(view(q), view(k), view(k), view(v), view(v), bias_tabs, bias_tabs)
    if dil == 1:
        return o.reshape(bsz * seq, w), lse.reshape(bsz * seq, LANES)
    return o.reshape(o.shape[0], SPAN_RES, A_BLOCK, w), lse.reshape(o.shape[0], SPAN_RES, A_BLOCK, LANES)


def _gelu_tanh(x):
    c = math.sqrt(2.0 / math.pi)
    return x * (0.5 * (1.0 + jnp.tanh(c * (x + 0.044715 * (x * x * x)))))


def _s5_kernel(u_ref, pm_ref, pmt_ref, bd_ref, pw_ref, pw16_ref, cd_ref, dsk_ref, gw_ref, gb_ref, o_ref,
               x_scr, st_scr, *, n_state):
    @pl.when(pl.program_id(1) == 0)
    def _():
        st_scr[...] = jnp.zeros_like(st_scr)

    for tile_idx in range(S5_TILES_PER_STEP):
        _s5_tile(tile_idx, u_ref, pm_ref, pmt_ref, bd_ref, pw_ref, pw16_ref, cd_ref, dsk_ref, gw_ref, gb_ref,
                 o_ref, x_scr, st_scr, n_state)


def _s5_tile(tile_idx, u_ref, pm_ref, pmt_ref, bd_ref, pw_ref, pw16_ref, cd_ref, dsk_ref, gw_ref, gb_ref,
             o_ref, x_scr, st_scr, n_state):
    n = n_state
    jn = S5_J
    tile_rows = slice(tile_idx * SUBLANES * jn, (tile_idx + 1) * SUBLANES * jn)

    u = u_ref[tile_rows, :]
    u_perm = jnp.dot(pm_ref[...], u.astype(BF16), preferred_element_type=F32).astype(BF16)
    cw = u.shape[1] * S5_LW // n
    chunks = [(slice(c0, c0 + S5_LW), slice(n + c0, n + c0 + S5_LW)) for c0 in range(0, n, S5_LW)]

    def project_in(q):
        bu = jnp.dot(u_perm[:, q * cw:(q + 1) * cw], bd_ref[q], preferred_element_type=F32)
        x_scr[:, chunks[q][0]] = bu[:, 0:S5_LW]
        x_scr[:, chunks[q][1]] = bu[:, S5_LW:]

    project_in(0)
    end_r, end_i = [], []
    for q, (re, im) in enumerate(chunks):
        if q + 1 < len(chunks):
            project_in(q + 1)
        a1r, a1i = pw_ref[0:SUBLANES, re], pw_ref[0:SUBLANES, im]
        xr = jnp.zeros((SUBLANES, S5_LW), F32)
        xi = jnp.zeros((SUBLANES, S5_LW), F32)
        for j in range(jn):
            rows = slice(SUBLANES * j, SUBLANES * (j + 1))
            nr = a1r * xr - a1i * xi + x_scr[rows, re]
            ni = a1r * xi + a1i * xr + x_scr[rows, im]
            xr, xi = nr, ni
            x_scr[rows, re] = xr
            x_scr[rows, im] = xi
        end_r.append(xr)
        end_i.append(xi)
    er, ei = jnp.concatenate(end_r, axis=1), jnp.concatenate(end_i, axis=1)
    last = SUBLANES * (jn - 1)
    ajr, aji = pw_ref[last:last + 1, 0:n], pw_ref[last:last + 1, n:]
    pr, pi = st_scr[0:1, 0:n], st_scr[0:1, n:]
    cin_r, cin_i = [], []
    for s in range(SUBLANES):
        cin_r.append(pr)
        cin_i.append(pi)
        nr = er[s:s + 1] + ajr * pr - aji * pi
        ni = ei[s:s + 1] + ajr * pi + aji * pr
        pr, pi = nr, ni
    st_scr[0:1, 0:n] = pr
    st_scr[0:1, n:] = pi
    cr_all = jnp.concatenate(cin_r, axis=0)
    ci_all = jnp.concatenate(cin_i, axis=0)
    pack = 2 * SUBLANES
    ys = []
    for q, (re, im) in enumerate(chunks):
        cr = jnp.concatenate([cr_all[:, re]] * 2, axis=0).astype(BF16)
        ci = jnp.concatenate([ci_all[:, re]] * 2, axis=0).astype(BF16)
        xr_parts, xi_parts = [], []
        for m in range(SUBLANES * jn // pack):
            rows = slice(pack * m, pack * (m + 1))
            pjr, pji = pw16_ref[rows, re], pw16_ref[rows, im]
            xr_parts.append(x_scr[rows, re].astype(BF16) + (pjr * cr - pji * ci))
            xi_parts.append(x_scr[rows, im].astype(BF16) + (pjr * ci + pji * cr))
        xcat = jnp.concatenate([jnp.concatenate(xr_parts, axis=0), jnp.concatenate(xi_parts, axis=0)], axis=1)
        ys.append(jnp.dot(xcat, cd_ref[q], preferred_element_type=F32))

    y_perm = jnp.concatenate(ys, axis=1)
    y_hi = y_perm.astype(BF16)
    y_lo = (y_perm - y_hi.astype(F32)).astype(BF16)
    y = (jnp.dot(pmt_ref[...], y_hi, preferred_element_type=F32)
         + jnp.dot(pmt_ref[...], y_lo, preferred_element_type=F32))
    y = _gelu_tanh(y + dsk_ref[...] * u)
    z = jnp.dot(y.astype(BF16), gw_ref[...], preferred_element_type=F32) + gb_ref[...]
    o_ref[tile_rows, :] = (y * _sigmoid(z)).astype(o_ref.dtype)


def _s5_tables(a_re, a_im, log_dt, b_re, b_im, c_re, c_im):
    g, p = a_re.shape
    dt = jnp.exp(log_dt.astype(F32))[:, None]
    ar, ai = a_re.astype(F32), a_im.astype(F32)
    mag = jnp.exp(dt * ar)
    abar_r, abar_i = mag * jnp.cos(dt * ai), mag * jnp.sin(dt * ai)
    den = ar * ar + ai * ai
    fr = ((abar_r - 1.0) * ar + abar_i * ai) / den
    fi = (abar_i * ar - (abar_r - 1.0) * ai) / den
    br, bi = b_re.astype(F32), b_im.astype(F32)
    bbar_r = fr[..., None] * br - fi[..., None] * bi
    bbar_i = fr[..., None] * bi + fi[..., None] * br
    m = br.shape[-1]
    gc = S5_LW // p
    nq = g // gc
    eye = jnp.eye(gc, dtype=F32)
    dense_b = lambda t: jnp.einsum('qgpm,gh->qgmhp', t.reshape(nq, gc, p, m), eye).reshape(nq, gc * m, gc * p)
    bd = jnp.concatenate([dense_b(bbar_r), dense_b(bbar_i)], axis=2)
    dense_c = lambda t: jnp.einsum('qgmp,gh->qgphm', t.reshape(nq, gc, m, p), eye).reshape(nq, gc * p, gc * m)
    cd = jnp.concatenate([dense_c(c_re.astype(F32)), -dense_c(c_im.astype(F32))], axis=1)
    kk = jnp.arange(1, S5_J + 1, dtype=F32)[:, None, None]
    pmag = jnp.exp(kk * (dt * ar)[None])
    pw_r = (pmag * jnp.cos(kk * (dt * ai)[None])).reshape(S5_J, g * p)
    pw_i = (pmag * jnp.sin(kk * (dt * ai)[None])).reshape(S5_J, g * p)
    pw = jnp.repeat(jnp.concatenate([pw_r, pw_i], axis=1), SUBLANES, axis=0)
    return bd.astype(BF16), pw, cd.astype(BF16)


def _s5_layer(u, bsz, seq, a_re, a_im, log_dt, b_re, b_im, c_re, c_im, d_skip, glu_w, glu_b):
    t, width = u.shape
    n = a_re.shape[0] * a_re.shape[1]
    bd, pw, cd = _s5_tables(a_re, a_im, log_dt, b_re, b_im, c_re, c_im)
    tile = SUBLANES * S5_J
    step_rows = S5_TILES_PER_STEP * tile
    steps_per_seq = seq // step_rows
    const = lambda b, i: (0, 0)
    perm = _sublane_major_perm(tile)
    return pl.pallas_call(
        functools.partial(_s5_kernel, n_state=n),
        out_shape=jax.ShapeDtypeStruct((t, width), BF16),
        grid=(bsz, steps_per_seq),
        in_specs=[pl.BlockSpec((step_rows, width), lambda b, i: (b * steps_per_seq + i, 0)),
                  pl.BlockSpec((tile, tile), const),
                  pl.BlockSpec((tile, tile), const),
                  pl.BlockSpec(bd.shape, lambda b, i: (0, 0, 0)),
                  pl.BlockSpec((tile, 2 * n), const),
                  pl.BlockSpec((tile, 2 * n), const),
                  pl.BlockSpec(cd.shape, lambda b, i: (0, 0, 0)),
                  pl.BlockSpec((1, width), const),
                  pl.BlockSpec((width, width), const),
                  pl.BlockSpec((1, width), const)],
        out_specs=pl.BlockSpec((step_rows, width), lambda b, i: (b * steps_per_seq + i, 0)),
        scratch_shapes=[pltpu.VMEM((tile, 2 * n), F32), pltpu.VMEM((SUBLANES, 2 * n), F32)],
        compiler_params=_cparams(("arbitrary", "arbitrary")),
        name="s5_layer",
    )(u, jnp.asarray(perm, BF16), jnp.asarray(perm.T, BF16), bd, pw, pw.astype(BF16), cd,
      d_skip.reshape(1, width).astype(F32), glu_w.astype(BF16),
      glu_b.reshape(1, width).astype(F32))


def _finish(y, x_ref, mod_ref, pg_ref, out_ref, d_model):
    ms = jnp.mean(y * y, axis=-1, keepdims=True)
    yn = y * lax.rsqrt(ms + EPS) * pg_ref[...]
    gate_mod = mod_ref[0, :, 2 * d_model:3 * d_model]
    out_ref[...] = x_ref[...] + gate_mod * yn


def _split2(x):
    hi = x.astype(BF16)
    return hi, (x - hi.astype(F32)).astype(BF16)


def _out_ab_kernel(o0_ref, l0_ref, o1_ref, o2_ref, l1_ref, l2_ref, ob_ref, gate_ref, e_ref, un_ref, w_ref,
                   x_ref, mod_ref, pg_ref, out_ref, *, d_model):
    rows, aw = o0_ref.shape

    def unperm(o_ref, l_ref):
        pieces = []
        for piece in range(rows // PERM_ROWS):
            runs = slice(piece * PERM_RUN, (piece + 1) * PERM_RUN)
            o_p = o_ref[:, runs, :].reshape(PERM_ROWS, aw)
            l_p = l_ref[:, runs, :].reshape(PERM_ROWS, LANES)
            parts = jnp.concatenate([o_p] + list(_split2(l_p)), axis=1)
            pieces.append(jnp.dot(un_ref[...], parts, preferred_element_type=F32))
        moved = jnp.concatenate(pieces, axis=0)
        return moved[:, 0:aw], moved[:, aw:aw + LANES] + moved[:, aw + LANES:]

    o0, l0 = o0_ref[...].astype(F32), l0_ref[...]
    o1, l1 = unperm(o1_ref, l1_ref)
    o2, l2 = unperm(o2_ref, l2_ref)
    mx = jnp.maximum(jnp.maximum(l0, l1), l2)
    e0, e1, e2 = jnp.exp2(l0 - mx), jnp.exp2(l1 - mx), jnp.exp2(l2 - mx)
    inv_den = 1.0 / (e0 + e1 + e2)
    stacked = jnp.concatenate(list(_split2(e1 * inv_den)) + list(_split2(e2 * inv_den)), axis=0)
    wide = jnp.dot(stacked, e_ref[...], preferred_element_type=F32)
    term = lambda i: wide[i * rows:(i + 1) * rows]
    w1 = term(0) + term(1)
    w2 = term(2) + term(3)
    o_a = (1.0 - w1 - w2) * o0 + w1 * o1 + w2 * o2
    gate = gate_ref[...].astype(F32)
    sg = gate * _sigmoid(gate)
    aw = o_a.shape[-1]
    y = jnp.dot((o_a * sg[:, :aw]).astype(BF16), w_ref[0:aw, :], preferred_element_type=F32)
    y = y + jnp.dot((ob_ref[...].astype(F32) * sg[:, aw:]).astype(BF16), w_ref[aw:, :],
                    preferred_element_type=F32)
    _finish(y, x_ref, mod_ref, pg_ref, out_ref, d_model)


def _out_c_kernel(o_ref, gate_ref, w_ref, x_ref, mod_ref, pg_ref, out_ref, *, d_model):
    gate = gate_ref[...].astype(F32)
    o = o_ref[...].astype(F32) * (gate * _sigmoid(gate))
    y = jnp.dot(o.astype(BF16), w_ref[...], preferred_element_type=F32)
    _finish(y, x_ref, mod_ref, pg_ref, out_ref, d_model)


def _out_proj(kern, row_inputs, const_inputs, w_bf16, x2, mod3, post_g, seq, name):
    t, d = x2.shape
    tile = PROJ_TILE
    tiles_per_seq = seq // tile
    tiles_per_span = SPAN // tile

    def row_spec(a):
        if a.ndim == 2:
            return pl.BlockSpec((tile, a.shape[1]), lambda i: (i, 0))
        return pl.BlockSpec((None, SPAN_RES, tile // SPAN_RES, a.shape[-1]),
                            lambda i: (i // tiles_per_span, 0, i % tiles_per_span, 0))

    const_spec = lambda a: pl.BlockSpec(a.shape, lambda i: (0, 0))
    return pl.pallas_call(
        functools.partial(kern, d_model=d),
        out_shape=jax.ShapeDtypeStruct((t, d), F32),
        grid=(t // tile,),
        in_specs=([row_spec(a) for a in row_inputs] + [const_spec(a) for a in const_inputs]
                  + [const_spec(w_bf16), row_spec(x2),
                     pl.BlockSpec((1, 1, 3 * d), lambda i: (i // tiles_per_seq, 0, 0)),
                     pl.BlockSpec((1, d), lambda i: (0, 0))]),
        out_specs=pl.BlockSpec((tile, d), lambda i: (i, 0)),
        compiler_params=_cparams(("arbitrary",)),
        name=name,
    )(*row_inputs, *const_inputs, w_bf16, x2, mod3, post_g.reshape(1, d))


def _gdn_prep_kernel(x_ref, halo_ref, br_ref, cw_ref, alog_ref, dtb_ref,
                     w_ref, u_ref, qg_ref, kdt_ref, aqk_ref, dec_ref, xs_scr):
    n_all = GDN_PREP_BLOCKS * GDN_STEP
    first = pl.program_id(1) == 0
    halo = halo_ref[...]
    xs_scr[0:SUBLANES, :] = jnp.where(first, jnp.zeros_like(halo), halo)
    xs_scr[SUBLANES:SUBLANES + n_all, :] = x_ref[...]
    cw = cw_ref[...]
    ext = xs_scr[...].reshape(n_all // SUBLANES + 1, SUBLANES, x_ref.shape[1])
    first_sublane = lax.broadcasted_iota(jnp.int32, ext.shape, 1) == 0

    def delay(y):
        rot = pltpu.roll(y, 1, axis=1)
        prev = jnp.concatenate([rot[:1], rot[:-1]], axis=0)
        return jnp.where(first_sublane, prev, rot)

    acc = ext * cw[0:1]
    for j in range(1, C_CONV):
        acc = delay(acc) + ext * cw[j:j + 1]
    conv = acc[1:].reshape(n_all, x_ref.shape[1])
    act = conv * _sigmoid(conv)
    _gdn_prep_blocks(act, br_ref, alog_ref, dtb_ref, w_ref, u_ref, qg_ref, kdt_ref, aqk_ref, dec_ref)


def _gdn_prep_blocks(act, br_ref, alog_ref, dtb_ref, w_ref, u_ref, qg_ref, kdt_ref, aqk_ref, dec_ref):
    n = GDN_STEP
    c = GDN_CHUNK
    row = lax.broadcasted_iota(jnp.int32, (n, n), 0)
    col = lax.broadcasted_iota(jnp.int32, (n, n), 1)
    cum = jnp.logical_and((row // c) == (col // c), row >= col).astype(F32)
    prow = lax.broadcasted_iota(jnp.int32, (c, n), 0)
    plane = lax.broadcasted_iota(jnp.int32, (c, n), 1)
    left = plane < c
    pcol = jnp.where(left, plane, plane - c)
    tril = prow >= pcol
    strict = prow > pcol
    eye = (prow == pcol).astype(F32)
    pair = lambda full: jnp.where(left, full[0:c], full[c:])

    def blockdiag(p):
        zero = jnp.zeros_like(p)
        return jnp.concatenate([jnp.where(left, p, zero), jnp.where(left, zero, p)], axis=0)

    first_chunk = lax.broadcasted_iota(jnp.int32, (n, 1), 0) < GDN_CHUNK
    dec_row = lax.broadcasted_iota(jnp.int32, (SUBLANES, LANES), 0)
    heads = range(C_HEADS)
    lanes = [slice(h * LANES, (h + 1) * LANES) for h in heads]

    def gates(blk):
        rows = slice(blk * n, (blk + 1) * n)
        br = br_ref[rows, :]
        xg = br + dtb_ref[...]
        softplus = jnp.maximum(xg, 0.0) + jnp.log(1.0 + jnp.exp(-jnp.abs(xg)))
        g_all = -jnp.exp(alog_ref[...]) * softplus
        gc_all = jnp.dot(cum, g_all, preferred_element_type=F32, precision=HI)
        g_last = jnp.where(first_chunk, gc_all[c - 1:c], gc_all[n - 1:n])
        return dict(rows=rows, beta=_sigmoid(br), gc=gc_all, gc_t=gc_all.T, eg=jnp.exp(gc_all),
                    tail=jnp.exp(g_last - gc_all))

    def first_stage(ctx, blk, h):
        rows = ctx["rows"]
        gcol = ctx["gc"][:, C_HEADS + h:C_HEADS + h + 1]
        grow = ctx["gc_t"][C_HEADS + h:C_HEADS + h + 1, :]
        eg = ctx["eg"][:, C_HEADS + h:C_HEADS + h + 1]
        gcol_p = jnp.where(left, gcol[0:c], gcol[c:])
        decay = jnp.where(tril, jnp.exp(jnp.where(tril, gcol_p - grow, 0.0)), 0.0)
        q = act[rows, h * C_DK:(h + 1) * C_DK]
        k = act[rows, (C_HEADS + h) * C_DK:(C_HEADS + h + 1) * C_DK]
        v = act[rows, 2 * C_HEADS * C_DK + h * C_DV:2 * C_HEADS * C_DK + (h + 1) * C_DV]
        q = q * (lax.rsqrt(jnp.sum(q * q, axis=-1, keepdims=True) + EPS) * (C_DK ** -0.5))
        k = k * lax.rsqrt(jnp.sum(k * k, axis=-1, keepdims=True) + EPS)
        beta = ctx["beta"][:, h:h + 1]
        kb = k * beta
        k16 = k.astype(BF16)
        a = jnp.where(strict, pair(lax.dot_general(kb.astype(BF16), k16, _NT, preferred_element_type=F32)) * decay,
                      0.0)
        aqk = pair(lax.dot_general(q.astype(BF16), k16, _NT, preferred_element_type=F32)) * decay
        aqk_ref[rows, lanes[h]] = blockdiag(aqk).astype(BF16)
        qg_ref[rows, lanes[h]] = (q * eg).astype(BF16)
        kdt_ref[rows, lanes[h]] = (k * ctx["tail"][:, C_HEADS + h:C_HEADS + h + 1]).T.astype(BF16)
        dec_ref[blk, :, lanes[h]] = jnp.where(dec_row < SUBLANES // 2, eg[GDN_CHUNK - 1:GDN_CHUNK], eg[n - 1:n])
        rhs = jnp.concatenate([v * beta, kb * eg], axis=1).astype(BF16)
        return dict(a=a, inv=eye - a, rhs=rhs)

    n_blocks = act.shape[0] // n
    first_ctx = gates(0)
    state = [first_stage(first_ctx, 0, h) for h in heads]
    for blk in range(n_blocks):
        nxt_ctx = gates(blk + 1) if blk + 1 < n_blocks else None
        nxt = []

        def side_work(hs, blk=blk, nxt=nxt, nxt_ctx=nxt_ctx):
            if nxt_ctx is not None:
                nxt.extend(first_stage(nxt_ctx, blk + 1, h) for h in hs)

        invs = [st["inv"] for st in state]
        pows = []
        for st in state:
            pows.append(jnp.dot(st["a"].astype(BF16), blockdiag(st["a"]).astype(BF16), preferred_element_type=F32))
        side_work(heads[0:2])
        for level in range(4):
            for h in heads:
                both = jnp.dot(jnp.concatenate([invs[h], pows[h]], axis=0).astype(BF16),
                               blockdiag(pows[h]).astype(BF16), preferred_element_type=F32)
                invs[h] = invs[h] + both[0:c]
                pows[h] = both[c:]
            side_work(heads[2 + level:3 + level])
        for h in heads:
            invs[h] = invs[h] + jnp.dot(invs[h].astype(BF16), blockdiag(pows[h]).astype(BF16),
                                        preferred_element_type=F32)
        side_work(heads[6:8])
        rows = slice(blk * n, (blk + 1) * n)
        for h in heads:
            uw = jnp.dot(blockdiag(invs[h]).astype(BF16), state[h]["rhs"], preferred_element_type=F32)
            u_ref[rows, lanes[h]] = uw[:, 0:C_DV].astype(u_ref.dtype)
            w_ref[rows, lanes[h]] = uw[:, C_DV:].astype(BF16)
        state = nxt


def _gdn_rec_kernel(w_ref, u_ref, qg_ref, kdt_ref, aqk_ref, dec_ref, ng_ref, o_ref, s_scr):
    @pl.when(pl.program_id(0) == 0)
    def _():
        s_scr[...] = jnp.zeros_like(s_scr)

    half = SUBLANES // 2
    zeros = jnp.zeros((GDN_CHUNK, C_DV), BF16)
    lanes = [slice(h * LANES, (h + 1) * LANES) for h in range(C_HEADS)]
    chains = [(b, h) for b in range(w_ref.shape[0]) for h in range(C_HEADS)]
    states = [s_scr[b * C_HEADS + h] for b, h in chains]
    for blk in range(GDN_REC_BLOCKS):
        blk_rows = slice(blk * GDN_STEP, (blk + 1) * GDN_STEP)
        for j in range(GDN_STEP // GDN_CHUNK):
            rows = slice(blk * GDN_STEP + j * GDN_CHUNK, blk * GDN_STEP + (j + 1) * GDN_CHUNK)
            wss = []
            for c, (b, h) in enumerate(chains):
                wq = jnp.concatenate([w_ref[b, rows, lanes[h]], qg_ref[b, rows, lanes[h]]], axis=0)
                wss.append(jnp.dot(wq, states[c].astype(BF16), preferred_element_type=F32))
            for c, (b, h) in enumerate(chains):
                v_new = (u_ref[b, rows, lanes[h]].astype(F32) - wss[c][0:GDN_CHUNK]).astype(BF16)
                v_pad = jnp.concatenate([v_new, zeros] if j == 0 else [zeros, v_new], axis=0)
                both = jnp.dot(jnp.concatenate([aqk_ref[b, rows, lanes[h]], kdt_ref[b, blk_rows, lanes[h]]], axis=0),
                               v_pad, preferred_element_type=F32)
                o = wss[c][GDN_CHUNK:] + both[0:GDN_CHUNK]
                dec = dec_ref[b, blk, j * half:j * half + 1, lanes[h]]
                states[c] = states[c] * dec + both[GDN_CHUNK:]
                ms = jnp.mean(o * o, axis=-1, keepdims=True)
                o_ref[b, rows, lanes[h]] = (o * lax.rsqrt(ms + EPS) * ng_ref[...]).astype(o_ref.dtype)
    for c, (b, h) in enumerate(chains):
        s_scr[b * C_HEADS + h] = states[c]


def _gdn_core(qkv_pre, br, bsz, seq, conv_w, a_log, dt_bias, norm_g):
    t, qkv_w = qkv_pre.shape
    hw = C_HEADS * C_DV
    steps_per_seq = seq // GDN_STEP
    prep_rows = GDN_PREP_BLOCKS * GDN_STEP
    prep_steps = steps_per_seq // GDN_PREP_BLOCKS
    halo_blocks = prep_rows // SUBLANES
    pad_row = lambda vec: jnp.zeros((1, LANES), F32).at[0, C_HEADS:2 * C_HEADS].set(vec.astype(F32))
    tok = lambda b, i: (b * prep_steps + i, 0)
    const = lambda b, i: (0, 0)
    tok_spec = pl.BlockSpec((prep_rows, hw), tok)
    dec_spec = pl.BlockSpec((GDN_PREP_BLOCKS, SUBLANES, hw), lambda b, i: (b * prep_steps + i, 0, 0))
    w, u, qg, kdt, aqk, dec = pl.pallas_call(
        _gdn_prep_kernel,
        out_shape=[jax.ShapeDtypeStruct((t, hw), BF16), jax.ShapeDtypeStruct((t, hw), BF16),
                   jax.ShapeDtypeStruct((t, hw), BF16), jax.ShapeDtypeStruct((t, hw), BF16),
                   jax.ShapeDtypeStruct((t, hw), BF16),
                   jax.ShapeDtypeStruct((t // GDN_STEP, SUBLANES, hw), F32)],
        grid=(bsz, prep_steps),
        in_specs=[pl.BlockSpec((prep_rows, qkv_w), tok),
                  pl.BlockSpec((SUBLANES, qkv_w),
                               lambda b, i: (jnp.maximum((b * prep_steps + i) * halo_blocks - 1, 0), 0)),
                  pl.BlockSpec((prep_rows, LANES), tok),
                  pl.BlockSpec((C_CONV, qkv_w), const),
                  pl.BlockSpec((1, LANES), const),
                  pl.BlockSpec((1, LANES), const)],
        out_specs=[tok_spec, tok_spec, tok_spec, tok_spec, tok_spec, dec_spec],
        scratch_shapes=[pltpu.VMEM((SUBLANES + prep_rows, qkv_w), F32)],
        compiler_params=_cparams(("arbitrary", "arbitrary")),
        name="gdn_prep",
    )(qkv_pre, qkv_pre, br, conv_w.astype(F32), pad_row(a_log), pad_row(dt_bias))
    rec_steps = steps_per_seq // GDN_REC_BLOCKS
    seq_view = lambda a: a.reshape(bsz, seq, hw)
    rec_tok = pl.BlockSpec((bsz, GDN_REC_BLOCKS * GDN_STEP, hw), lambda i: (0, i, 0))
    rec_dec = pl.BlockSpec((bsz, GDN_REC_BLOCKS, SUBLANES, hw), lambda i: (0, i, 0, 0))
    o = pl.pallas_call(
        _gdn_rec_kernel,
        out_shape=jax.ShapeDtypeStruct((bsz, seq, hw), BF16),
        grid=(rec_steps,),
        in_specs=[rec_tok, rec_tok, rec_tok, rec_tok, rec_tok, rec_dec,
                  pl.BlockSpec((1, C_DV), lambda i: (0, 0))],
        out_specs=rec_tok,
        scratch_shapes=[pltpu.VMEM((bsz * C_HEADS, C_DK, C_DV), F32)],
        compiler_params=_cparams(("arbitrary",)),
        name="gdn_recurrence",
    )(seq_view(w), seq_view(u), seq_view(qg), seq_view(kdt), seq_view(aqk),
      dec.reshape(bsz, steps_per_seq, SUBLANES, hw), norm_g.reshape(1, C_DV).astype(F32))
    return o.reshape(t, hw)


def _ab_layer(x2, mod3, bsz, seq, pre_g, post_g, rel_bias, w_in, w_out, s5_params):
    b_width = s5_params[7].shape[-1]
    assert DILATED_CONFIGS[0][1] == 1 and all(SPAN_RES % dl == 0 for _, dl in DILATED_CONFIGS) and seq % SPAN == 0
    head_expand = jnp.asarray(np.arange(LANES)[:, None] == (np.arange(A_WIDTH)[None, :] // A_HEAD_DIM), BF16)
    unperm = jnp.asarray(_span_perm().T, BF16)
    splits = ((0, A_WIDTH, A_HEAD_DIM ** -0.5 * LOG2_E), (A_WIDTH, A_WIDTH, 1.0), (2 * A_WIDTH, A_WIDTH, 1.0),
              (3 * A_WIDTH, b_width, 1.0), (3 * A_WIDTH + b_width, A_WIDTH + b_width, 1.0))
    q, k, v, u, gate, qs, ks, vs = _in_proj(x2, mod3, pre_g, w_in.astype(BF16), splits,
                                            (BF16, BF16, BF16, F32, BF16), seq, n_span=3)
    outs, lses = [], []
    bias_tabs = _attn_bias_tables(rel_bias)
    for cfg, (_, dil) in enumerate(DILATED_CONFIGS):
        qkv = (q, k, v) if dil == 1 else (qs, ks, vs)
        o_c, lse_c = _dilated_attention_one(*qkv, bias_tabs, cfg, bsz, seq, dil)
        outs.append(o_c)
        lses.append(lse_c)
    o_b = _s5_layer(u, bsz, seq, *s5_params)
    row_inputs = [outs[0], lses[0], outs[1], outs[2], lses[1], lses[2], o_b, gate]
    return _out_proj(_out_ab_kernel, row_inputs, [head_expand, unperm], w_out.astype(BF16),
                     x2, mod3, post_g, seq, "out_proj_ab")


def _gdn_layer(x2, mod3, bsz, seq, pre_g, post_g, w_in, conv_w, a_log, dt_bias, norm_g, w_out):
    d = x2.shape[1]
    qkv_w = 2 * C_HEADS * C_DK + C_HEADS * C_DV
    gate_w = C_HEADS * C_DV
    w_pad = jnp.concatenate(
        [w_in, jnp.zeros((d, LANES - (w_in.shape[1] - qkv_w - gate_w)), w_in.dtype)], axis=1).astype(BF16)
    splits = ((0, qkv_w, 1.0), (qkv_w, gate_w, 1.0), (qkv_w + gate_w, LANES, 1.0))
    qkv_pre, gate, br = _in_proj(x2, mod3, pre_g, w_pad, splits, (F32, BF16, F32), seq)
    o = _gdn_core(qkv_pre, br, bsz, seq, conv_w, a_log, dt_bias, norm_g)
    return _out_proj(_out_c_kernel, [o, gate], [], w_out.astype(BF16), x2, mod3, post_g, seq, "out_proj_c")


def kernel(x, c, ada_w, ada_b, pre_g, post_g, rel_bias, ab_w_in, ab_w_out, s5_a_re, s5_a_im, s5_log_dt, s5_b_re, s5_b_im, s5_c_re, s5_c_im, s5_d, s5_glu_w, s5_glu_b, gdn_w_in, gdn_conv, gdn_a_log, gdn_dt_bias, gdn_norm_g, gdn_w_out):
    bsz, seq, d = x.shape
    depth = ada_w.shape[0]
    assert seq % PROJ_TILE == 0 and PROJ_TILE % PERM_ROWS == 0 and SPAN % PROJ_TILE == 0
    assert seq % (SUBLANES * S5_J * S5_TILES_PER_STEP) == 0
    assert seq % (GDN_STEP * GDN_PREP_BLOCKS) == 0 and seq % (GDN_STEP * GDN_REC_BLOCKS) == 0
    x2 = x.reshape(bsz * seq, d)
    mod = _adaln_mod(c, ada_w, ada_b)
    for layer in range(depth):
        j = layer // 2
        mod3 = mod[layer].reshape(bsz, 1, 3 * d)
        if layer % 2 == 0:
            s5_params = (s5_a_re[j], s5_a_im[j], s5_log_dt[j], s5_b_re[j], s5_b_im[j], s5_c_re[j], s5_c_im[j],
                         s5_d[j], s5_glu_w[j], s5_glu_b[j])
            x2 = _ab_layer(x2, mod3, bsz, seq, pre_g[layer], post_g[layer], rel_bias, ab_w_in[j], ab_w_out[j],
                           s5_params)
        else:
            x2 = _gdn_layer(x2, mod3, bsz, seq, pre_g[layer], post_g[layer], gdn_w_in[j], gdn_conv[j],
                            gdn_a_log[j], gdn_dt_bias[j], gdn_norm_g[j], gdn_w_out[j])
    return x2.reshape(bsz, seq, d)
```

```python
import functools
import math

import numpy as np
import jax
import jax.numpy as jnp
from jax import lax
from jax.experimental import pallas as pl
from jax.experimental.pallas import tpu as pltpu

F32 = jnp.float32
BF16 = jnp.bfloat16
HI = lax.Precision.HIGHEST

EPS = 1e-6
A_HEADS = 8
A_HEAD_DIM = 64
A_WIDTH = A_HEADS * A_HEAD_DIM
A_BLOCK = 128
DILATED_CONFIGS = ((128, 1), (512, 4), (2048, 16))
REL_BUCKETS = 32
REL_MAX_DIST = 2048
C_HEADS = 8
C_DK = 128
C_DV = 128
C_CONV = 4
MASK_NEG = -1e30
LOG2_E = math.log2(math.e)

LANES = 128
SUBLANES = 8
VMEM_LIMIT = 56 * 1024 * 1024
ROW_TILE = 512
AB_TILE = 1024
OUT_C_TILE = 1024
COL_CHUNK = 512
SPAN_RES = max(dl for _, dl in DILATED_CONFIGS)
SPAN = A_BLOCK * SPAN_RES
PERM_ROWS = 256
PERM_RUN = PERM_ROWS // SPAN_RES
ATTN_BLOCKS_PER_STEP = 8
S5_J = 32
S5_LW = 512
S5_TILES_PER_STEP = 4
GDN_STEP = 128
GDN_CHUNK = 64
GDN_PREP_BLOCKS = 4
GDN_REC_BLOCKS = 2

_NT = (((1,), (1,)), ((), ()))


def _cparams(sem):
    return pltpu.CompilerParams(dimension_semantics=sem, vmem_limit_bytes=VMEM_LIMIT)


def _sigmoid(x):
    return 1.0 / (1.0 + jnp.exp(-x))


def _mod_kernel(c_ref, w_ref, b_ref, o_ref):
    c = c_ref[...]
    ca = c * _sigmoid(c)
    o_ref[0] = jnp.dot(ca.astype(BF16), w_ref[0].astype(BF16), preferred_element_type=F32) + b_ref[0]


def _adaln_mod(c, ada_w, ada_b):
    depth, d, d3 = ada_w.shape
    bsz = c.shape[0]
    return pl.pallas_call(
        _mod_kernel,
        out_shape=jax.ShapeDtypeStruct((depth, bsz, d3), F32),
        grid=(depth, d3 // d),
        in_specs=[pl.BlockSpec((bsz, d), lambda l, j: (0, 0)),
                  pl.BlockSpec((1, d, d), lambda l, j: (l, 0, j)),
                  pl.BlockSpec((1, 1, d), lambda l, j: (l, 0, j))],
        out_specs=pl.BlockSpec((1, bsz, d), lambda l, j: (l, 0, j)),
        compiler_params=_cparams(("arbitrary", "arbitrary")),
        name="adaln_mod",
    )(c, ada_w, ada_b.reshape(depth, 1, d3))


def _span_perm():
    rho = np.arange(PERM_ROWS)
    nat = SPAN_RES * (rho % PERM_RUN) + rho // PERM_RUN
    return (nat[:, None] == np.arange(PERM_ROWS)[None, :]).astype(np.float32)


def _modulated_norm(x_ref, mod_ref, g_ref, d_model):
    x = x_ref[...]
    ms = jnp.mean(x * x, axis=-1, keepdims=True)
    y = x * lax.rsqrt(ms + EPS) * g_ref[...]
    shift = mod_ref[0, :, 0:d_model]
    scale = mod_ref[0, :, d_model:2 * d_model]
    return (y * (1.0 + scale) + shift).astype(BF16)


def _sublane_major_perm(tile):
    rho = np.arange(tile)
    src = (rho % SUBLANES) * (tile // SUBLANES) + rho // SUBLANES
    return (src[:, None] == np.arange(tile)[None, :]).astype(np.float32)


def _in_proj_kernel(x_ref, mod_ref, g_ref, w_ref, *rest, splits, d_model, n_span):
    pm_ref = rest[0] if n_span else None
    out_refs = rest[1:] if n_span else rest
    span_refs = out_refs[len(splits):]
    h = _modulated_norm(x_ref, mod_ref, g_ref, d_model)
    for idx, ((c0, width, mult), o_ref) in enumerate(zip(splits, out_refs)):
        for cc in range(0, width, COL_CHUNK):
            cw = min(COL_CHUNK, width - cc)
            acc = jnp.dot(h, w_ref[:, c0 + cc:c0 + cc + cw], preferred_element_type=F32)
            if mult != 1.0:
                acc = acc * mult
            val = acc.astype(o_ref.dtype)
            o_ref[:, cc:cc + cw] = val
            if idx < n_span:
                for piece in range(x_ref.shape[0] // PERM_ROWS):
                    moved = jnp.dot(pm_ref[...], val[piece * PERM_ROWS:(piece + 1) * PERM_ROWS],
                                    preferred_element_type=F32).astype(BF16)
                    span_refs[idx][:, piece * PERM_RUN:(piece + 1) * PERM_RUN, cc:cc + cw] = (
                        moved.reshape(SPAN_RES, PERM_RUN, cw))


def _in_proj(x2, mod3, gain, w_bf16, splits, out_dtypes, seq, n_span=0, tile=ROW_TILE):
    t, d = x2.shape
    tiles_per_seq = seq // tile
    tiles_per_span = SPAN // tile
    n_w = w_bf16.shape[1]
    row = lambda wd: pl.BlockSpec((tile, wd), lambda i: (i, 0))
    span_shape = lambda wd: jax.ShapeDtypeStruct((t // SPAN, SPAN_RES, A_BLOCK, wd), BF16)
    span_spec = lambda wd: pl.BlockSpec((None, SPAN_RES, tile // SPAN_RES, wd),
                                        lambda i: (i // tiles_per_span, 0, i % tiles_per_span, 0))
    perm_in = [jnp.asarray(_span_perm(), BF16)] if n_span else []
    perm_spec = [pl.BlockSpec((PERM_ROWS, PERM_ROWS), lambda i: (0, 0))] if n_span else []
    return pl.pallas_call(
        functools.partial(_in_proj_kernel, splits=splits, d_model=d, n_span=n_span),
        out_shape=([jax.ShapeDtypeStruct((t, wd), dt) for (_, wd, _), dt in zip(splits, out_dtypes)]
                   + [span_shape(wd) for (_, wd, _) in splits[:n_span]]),
        grid=(t // tile,),
        in_specs=[row(d),
                  pl.BlockSpec((1, 1, 3 * d), lambda i: (i // tiles_per_seq, 0, 0)),
                  pl.BlockSpec((1, d), lambda i: (0, 0)),
                  pl.BlockSpec((d, n_w), lambda i: (0, 0))] + perm_spec,
        out_specs=[row(wd) for (_, wd, _) in splits] + [span_spec(wd) for (_, wd, _) in splits[:n_span]],
        compiler_params=_cparams(("arbitrary",)),
        name="in_proj",
    )(x2, mod3, gain.reshape(1, d), w_bf16, *perm_in)


def _t5_bucket(dist):
    dist = np.maximum(dist, 0)
    max_exact = REL_BUCKETS // 2
    large = max_exact + (np.log(np.maximum(dist, 1) / max_exact)
                         / math.log(REL_MAX_DIST / max_exact) * (REL_BUCKETS - max_exact)).astype(np.int32)
    large = np.minimum(large, REL_BUCKETS - 1)
    return np.where(dist < max_exact, dist, large).astype(np.int32)


def _bias_kernel(rb_ref, bucket_ref, mask_ref, o_ref):
    def body(r, carry):
        rows = pl.ds(pl.multiple_of(r * SUBLANES, SUBLANES), SUBLANES)
        bk = bucket_ref[0, rows, :]
        accs = [jnp.zeros(bk.shape, F32) for _ in range(A_HEADS)]
        for b in range(REL_BUCKETS):
            eq = bk == b
            accs = [jnp.where(eq, rb_ref[b, h], acc) for h, acc in enumerate(accs)]
        for f in range(2):
            keep = mask_ref[0, f, rows, :] != 0
            for h in range(A_HEADS):
                o_ref[0, f, h, rows, :] = jnp.where(keep, accs[h] * LOG2_E, MASK_NEG)
        return carry
    lax.fori_loop(0, A_BLOCK // SUBLANES, body, 0)


def _attn_bias_tables(rel_bias):
    qi = np.arange(A_BLOCK)[:, None]
    kj = np.arange(2 * A_BLOCK)[None, :]
    rel = qi + A_BLOCK - kj
    buckets, masks = [], []
    for window, dil in DILATED_CONFIGS:
        band = (rel >= 0) & (rel <= window // dil)
        bucket = _t5_bucket(rel * dil)
        mask = np.stack([band & (kj >= A_BLOCK), band]).astype(np.int32)
        if dil > 1:
            runs = SPAN_RES // dil
            run = A_BLOCK // runs
            rho = np.arange(A_BLOCK)
            sub = runs * (rho % run) + rho // run
            keys = np.concatenate([sub, A_BLOCK + sub])
            bucket = bucket[sub][:, keys]
            mask = mask[:, sub][:, :, keys]
        buckets.append(bucket)
        masks.append(mask)
    n_cfg = len(DILATED_CONFIGS)
    return pl.pallas_call(
        _bias_kernel,
        out_shape=jax.ShapeDtypeStruct((n_cfg, 2, A_HEADS, A_BLOCK, 2 * A_BLOCK), F32),
        grid=(n_cfg,),
        in_specs=[pl.BlockSpec(memory_space=pltpu.SMEM),
                  pl.BlockSpec((1, A_BLOCK, 2 * A_BLOCK), lambda c: (c, 0, 0)),
                  pl.BlockSpec((1, 2, A_BLOCK, 2 * A_BLOCK), lambda c: (c, 0, 0, 0))],
        out_specs=pl.BlockSpec((1, 2, A_HEADS, A_BLOCK, 2 * A_BLOCK), lambda c: (c, 0, 0, 0, 0)),
        compiler_params=_cparams(("arbitrary",)),
        name="attn_bias",
    )(rel_bias.astype(F32), jnp.asarray(np.stack(buckets)), jnp.asarray(np.stack(masks)))


def _attn_block(q, kcat, vcat, bias_ref):
    lane = lax.broadcasted_iota(jnp.int32, (A_BLOCK, LANES), 1)
    low = lane < A_HEAD_DIM
    lse_tile = jnp.zeros((A_BLOCK, LANES), F32)
    zero = jnp.zeros((A_BLOCK, LANES), BF16)
    heads = range(A_HEADS)
    pair = [slice((h // 2) * LANES, (h // 2 + 1) * LANES) for h in heads]
    scores = []
    for h in heads:
        qm = jnp.where(low if h % 2 == 0 else jnp.logical_not(low), q[:, pair[h]], zero)
        scores.append(lax.dot_general(qm, kcat[:, pair[h]], _NT, preferred_element_type=F32) + bias_ref[0, h])
    probs, inv_l = [], []
    for h in heads:
        m = jnp.max(scores[h], axis=-1, keepdims=True)
        p = jnp.exp2(scores[h] - m)
        l = jnp.sum(p, axis=-1, keepdims=True)
        probs.append(p.astype(BF16))
        inv_l.append(1.0 / l)
        lse_tile = jnp.where(lane == h, m + jnp.log2(l), lse_tile)
    outs = [jnp.dot(probs[h], vcat[:, pair[h]], preferred_element_type=F32) * inv_l[h] for h in heads]
    o = jnp.concatenate([jnp.where(low, outs[2 * hp], outs[2 * hp + 1]) for hp in range(A_HEADS // 2)], axis=1)
    return o, lse_tile


def _attn_kernel(q_ref, kp_ref, kc_ref, vp_ref, vc_ref, bias0_ref, bias_ref, o_ref, lse_ref, *, per_lead):
    def sub(ref, rr, i):
        if ref.ndim == 2:
            return ref[i * A_BLOCK:(i + 1) * A_BLOCK, :]
        run = ref.shape[3] // per_lead
        return ref[i // per_lead, :, rr, (i % per_lead) * run:(i % per_lead + 1) * run, :]

    def put(ref, rr, i, val):
        if ref.ndim == 2:
            ref[i * A_BLOCK:(i + 1) * A_BLOCK, :] = val.astype(ref.dtype)
        else:
            run = ref.shape[3] // per_lead
            ref[i // per_lead, :, rr, (i % per_lead) * run:(i % per_lead + 1) * run, :] = (
                val.astype(ref.dtype).reshape(ref.shape[1], run, ref.shape[4]))

    flat = lambda v: v.reshape(A_BLOCK, v.shape[-1])
    token_order = q_ref.ndim == 2
    n_blocks = q_ref.shape[0] // A_BLOCK if token_order else q_ref.shape[0] * per_lead
    for rr in range(1 if token_order else q_ref.shape[2]):
        k_prev = flat(kp_ref[...] if token_order else kp_ref[0, :, rr])
        v_prev = flat(vp_ref[...] if token_order else vp_ref[0, :, rr])
        for i in range(n_blocks):
            k_cur, v_cur = flat(sub(kc_ref, rr, i)), flat(sub(vc_ref, rr, i))
            o, lse_tile = _attn_block(flat(sub(q_ref, rr, i)), jnp.concatenate([k_prev, k_cur], axis=0),
                                      jnp.concatenate([v_prev, v_cur], axis=0),
                                      bias0_ref if i == 0 else bias_ref)
            put(o_ref, rr, i, o)
            put(lse_ref, rr, i, lse_tile)
            k_prev, v_prev = k_cur, v_cur


def _dilated_attention_one(q, k, v, bias_tabs, cfg, bsz, seq, dil):
    nb = seq // dil // A_BLOCK
    step = min(ATTN_BLOCKS_PER_STEP, nb)
    w = A_WIDTH
    if dil == 1:
        per_lead = n_res = 1
        view = lambda a: a.reshape(bsz, seq, a.shape[-1])
        block = lambda wd: (None, step * A_BLOCK, wd)
        block_prev = lambda wd: (None, A_BLOCK, wd)
        cur = lambda b, r, m: (b, m, 0)
        prev = lambda b, r, m: (b, jnp.maximum(m * step - 1, 0), 0)
        out_shape = lambda wd, dt: jax.ShapeDtypeStruct((bsz, seq, wd), dt)
    else:
        runs = SPAN_RES // dil
        run = A_BLOCK // runs
        spans_per_seq = seq // SPAN
        per_lead = min(step, runs)
        lead = step // per_lead
        n_res = max(1, ATTN_BLOCKS_PER_STEP // step)
        assert runs % per_lead == 0 and spans_per_seq % lead == 0 and dil % n_res == 0
        view = lambda a: a.reshape(a.shape[0], runs, dil, A_BLOCK, a.shape[-1])
        block = lambda wd: (lead, runs, n_res, per_lead * run, wd)
        block_prev = lambda wd: (1, runs, n_res, run, wd)
        cur = lambda b, r, m: ((b * spans_per_seq + (m * step) // runs) // lead, 0, r, ((m * step) % runs) // per_lead, 0)

        def prev(b, r, m):
            n = jnp.maximum(m * step - 1, 0)
            return (b * spans_per_seq + n // runs, 0, r, n % runs, 0)

        out_shape = lambda wd, dt: jax.ShapeDtypeStruct((bsz * spans_per_seq, runs, dil, A_BLOCK, wd), dt)
    blk = pl.BlockSpec(block(w), cur)
    blk_prev = pl.BlockSpec(block_prev(w), prev)
    bias_block = (None, 1, A_HEADS, A_BLOCK, 2 * A_BLOCK)
    o, lse = pl.pallas_call(
        functools.partial(_attn_kernel, per_lead=per_lead),
        out_shape=[out_shape(w, BF16), out_shape(LANES, F32)],
        grid=(bsz, dil // n_res, nb // step),
        in_specs=[blk, blk_prev, blk, blk_prev, blk,
                  pl.BlockSpec(bias_block, lambda b, r, m: (cfg, jnp.minimum(m, 1), 0, 0, 0)),
                  pl.BlockSpec(bias_block, lambda b, r, m: (cfg, 1, 0, 0, 0))],
        out_specs=[pl.BlockSpec(block(w), cur), pl.BlockSpec(block(LANES), cur)],
        compiler_params=_cparams(("arbitrary", "arbitrary", "arbitrary")),
        name=f"dilated_attn_d{dil}",
    )(view(q), view(k), view(k), view(v), view(v), bias_tabs, bias_tabs)
    if dil == 1:
        return o.reshape(bsz * seq, w), lse.reshape(bsz * seq, LANES)
    return o.reshape(o.shape[0], SPAN_RES, A_BLOCK, w), lse.reshape(o.shape[0], SPAN_RES, A_BLOCK, LANES)


def _gelu_tanh(x):
    c = math.sqrt(2.0 / math.pi)
    return x * (0.5 * (1.0 + jnp.tanh(c * (x + 0.044715 * (x * x * x)))))


def _s5_kernel(u_ref, pm_ref, pmt_ref, bd_ref, pw_ref, pw16_ref, cd_ref, dsk_ref, gw_ref, gb_ref, o_ref,
               x_scr, st_scr, *, n_state):
    @pl.when(pl.program_id(1) == 0)
    def _():
        st_scr[...] = jnp.zeros_like(st_scr)

    for tile_idx in range(S5_TILES_PER_STEP):
        _s5_tile(tile_idx, u_ref, pm_ref, pmt_ref, bd_ref, pw_ref, pw16_ref, cd_ref, dsk_ref, gw_ref, gb_ref,
                 o_ref, x_scr, st_scr, n_state)


def _s5_tile(tile_idx, u_ref, pm_ref, pmt_ref, bd_ref, pw_ref, pw16_ref, cd_ref, dsk_ref, gw_ref, gb_ref,
             o_ref, x_scr, st_scr, n_state):
    n = n_state
    jn = S5_J
    tile_rows = slice(tile_idx * SUBLANES * jn, (tile_idx + 1) * SUBLANES * jn)

    u = u_ref[tile_rows, :]
    u_perm = jnp.dot(pm_ref[...], u.astype(BF16), preferred_element_type=F32).astype(BF16)
    cw = u.shape[1] * S5_LW // n
    chunks = [(slice(c0, c0 + S5_LW), slice(n + c0, n + c0 + S5_LW)) for c0 in range(0, n, S5_LW)]

    def project_in(q):
        bu = jnp.dot(u_perm[:, q * cw:(q + 1) * cw], bd_ref[q], preferred_element_type=F32)
        x_scr[:, chunks[q][0]] = bu[:, 0:S5_LW]
        x_scr[:, chunks[q][1]] = bu[:, S5_LW:]

    project_in(0)
    end_r, end_i = [], []
    for q, (re, im) in enumerate(chunks):
        if q + 1 < len(chunks):
            project_in(q + 1)
        a1r, a1i = pw_ref[0:SUBLANES, re], pw_ref[0:SUBLANES, im]
        xr = jnp.zeros((SUBLANES, S5_LW), F32)
        xi = jnp.zeros((SUBLANES, S5_LW), F32)
        for j in range(jn):
            rows = slice(SUBLANES * j, SUBLANES * (j + 1))
            nr = a1r * xr - a1i * xi + x_scr[rows, re]
            ni = a1r * xi + a1i * xr + x_scr[rows, im]
            xr, xi = nr, ni
            x_scr[rows, re] = xr
            x_scr[rows, im] = xi
        end_r.append(xr)
        end_i.append(xi)
    er, ei = jnp.concatenate(end_r, axis=1), jnp.concatenate(end_i, axis=1)
    last = SUBLANES * (jn - 1)
    ajr, aji = pw_ref[last:last + 1, 0:n], pw_ref[last:last + 1, n:]
    pr, pi = st_scr[0:1, 0:n], st_scr[0:1, n:]
    cin_r, cin_i = [], []
    for s in range(SUBLANES):
        cin_r.append(pr)
        cin_i.append(pi)
        nr = er[s:s + 1] + ajr * pr - aji * pi
        ni = ei[s:s + 1] + ajr * pi + aji * pr
        pr, pi = nr, ni
    st_scr[0:1, 0:n] = pr
    st_scr[0:1, n:] = pi
    cr_all = jnp.concatenate(cin_r, axis=0)
    ci_all = jnp.concatenate(cin_i, axis=0)
    pack = 2 * SUBLANES
    ys = []
    for q, (re, im) in enumerate(chunks):
        cr = jnp.concatenate([cr_all[:, re]] * 2, axis=0).astype(BF16)
        ci = jnp.concatenate([ci_all[:, re]] * 2, axis=0).astype(BF16)
        xr_parts, xi_parts = [], []
        for m in range(SUBLANES * jn // pack):
            rows = slice(pack * m, pack * (m + 1))
            pjr, pji = pw16_ref[rows, re], pw16_ref[rows, im]
            xr_parts.append(x_scr[rows, re].astype(BF16) + (pjr * cr - pji * ci))
            xi_parts.append(x_scr[rows, im].astype(BF16) + (pjr * ci + pji * cr))
        xcat = jnp.concatenate([jnp.concatenate(xr_parts, axis=0), jnp.concatenate(xi_parts, axis=0)], axis=1)
        ys.append(jnp.dot(xcat, cd_ref[q], preferred_element_type=F32))

    y_perm = jnp.concatenate(ys, axis=1)
    y_hi = y_perm.astype(BF16)
    y_lo = (y_perm - y_hi.astype(F32)).astype(BF16)
    y = (jnp.dot(pmt_ref[...], y_hi, preferred_element_type=F32)
         + jnp.dot(pmt_ref[...], y_lo, preferred_element_type=F32))
    y = _gelu_tanh(y + dsk_ref[...] * u)
    z = jnp.dot(y.astype(BF16), gw_ref[...], preferred_element_type=F32) + gb_ref[...]
    o_ref[tile_rows, :] = (y * _sigmoid(z)).astype(o_ref.dtype)


def _s5_tables(a_re, a_im, log_dt, b_re, b_im, c_re, c_im):
    g, p = a_re.shape
    dt = jnp.exp(log_dt.astype(F32))[:, None]
    ar, ai = a_re.astype(F32), a_im.astype(F32)
    mag = jnp.exp(dt * ar)
    abar_r, abar_i = mag * jnp.cos(dt * ai), mag * jnp.sin(dt * ai)
    den = ar * ar + ai * ai
    fr = ((abar_r - 1.0) * ar + abar_i * ai) / den
    fi = (abar_i * ar - (abar_r - 1.0) * ai) / den
    br, bi = b_re.astype(F32), b_im.astype(F32)
    bbar_r = fr[..., None] * br - fi[..., None] * bi
    bbar_i = fr[..., None] * bi + fi[..., None] * br
    m = br.shape[-1]
    gc = S5_LW // p
    nq = g // gc
    eye = jnp.eye(gc, dtype=F32)
    dense_b = lambda t: jnp.einsum('qgpm,gh->qgmhp', t.reshape(nq, gc, p, m), eye).reshape(nq, gc * m, gc * p)
    bd = jnp.concatenate([dense_b(bbar_r), dense_b(bbar_i)], axis=2)
    dense_c = lambda t: jnp.einsum('qgmp,gh->qgphm', t.reshape(nq, gc, m, p), eye).reshape(nq, gc * p, gc * m)
    cd = jnp.concatenate([dense_c(c_re.astype(F32)), -dense_c(c_im.astype(F32))], axis=1)
    kk = jnp.arange(1, S5_J + 1, dtype=F32)[:, None, None]
    pmag = jnp.exp(kk * (dt * ar)[None])
    pw_r = (pmag * jnp.cos(kk * (dt * ai)[None])).reshape(S5_J, g * p)
    pw_i = (pmag * jnp.sin(kk * (dt * ai)[None])).reshape(S5_J, g * p)
    pw = jnp.repeat(jnp.concatenate([pw_r, pw_i], axis=1), SUBLANES, axis=0)
    return bd.astype(BF16), pw, cd.astype(BF16)


def _s5_layer(u, bsz, seq, a_re, a_im, log_dt, b_re, b_im, c_re, c_im, d_skip, glu_w, glu_b):
    t, width = u.shape
    n = a_re.shape[0] * a_re.shape[1]
    bd, pw, cd = _s5_tables(a_re, a_im, log_dt, b_re, b_im, c_re, c_im)
    tile = SUBLANES * S5_J
    step_rows = S5_TILES_PER_STEP * tile
    steps_per_seq = seq // step_rows
    const = lambda b, i: (0, 0)
    perm = _sublane_major_perm(tile)
    return pl.pallas_call(
        functools.partial(_s5_kernel, n_state=n),
        out_shape=jax.ShapeDtypeStruct((t, width), BF16),
        grid=(bsz, steps_per_seq),
        in_specs=[pl.BlockSpec((step_rows, width), lambda b, i: (b * steps_per_seq + i, 0)),
                  pl.BlockSpec((tile, tile), const),
                  pl.BlockSpec((tile, tile), const),
                  pl.BlockSpec(bd.shape, lambda b, i: (0, 0, 0)),
                  pl.BlockSpec((tile, 2 * n), const),
                  pl.BlockSpec((tile, 2 * n), const),
                  pl.BlockSpec(cd.shape, lambda b, i: (0, 0, 0)),
                  pl.BlockSpec((1, width), const),
                  pl.BlockSpec((width, width), const),
                  pl.BlockSpec((1, width), const)],
        out_specs=pl.BlockSpec((step_rows, width), lambda b, i: (b * steps_per_seq + i, 0)),
        scratch_shapes=[pltpu.VMEM((tile, 2 * n), F32), pltpu.VMEM((SUBLANES, 2 * n), F32)],
        compiler_params=_cparams(("arbitrary", "arbitrary")),
        name="s5_layer",
    )(u, jnp.asarray(perm, BF16), jnp.asarray(perm.T, BF16), bd, pw, pw.astype(BF16), cd,
      d_skip.reshape(1, width).astype(F32), glu_w.astype(BF16),
      glu_b.reshape(1, width).astype(F32))


def _finish(y, x_ref, mod_ref, pg_ref, out_ref, d_model):
    ms = jnp.mean(y * y, axis=-1, keepdims=True)
    yn = y * lax.rsqrt(ms + EPS) * pg_ref[...]
    gate_mod = mod_ref[0, :, 2 * d_model:3 * d_model]
    out_ref[...] = x_ref[...] + gate_mod * yn


def _split2(x):
    hi = x.astype(BF16)
    return hi, (x - hi.astype(F32)).astype(BF16)


def _out_ab_kernel(o0_ref, l0_ref, o1_ref, o2_ref, l1_ref, l2_ref, ob_ref, gate_ref, e_ref, un_ref, w_ref,
                   x_ref, mod_ref, pg_ref, out_ref, *, d_model):
    rows, aw = o0_ref.shape

    def unperm(o_ref, l_ref):
        pieces = []
        for piece in range(rows // PERM_ROWS):
            runs = slice(piece * PERM_RUN, (piece + 1) * PERM_RUN)
            o_p = o_ref[:, runs, :].reshape(PERM_ROWS, aw)
            l_p = l_ref[:, runs, :].reshape(PERM_ROWS, LANES)
            parts = jnp.concatenate([o_p] + list(_split2(l_p)), axis=1)
            pieces.append(jnp.dot(un_ref[...], parts, preferred_element_type=F32))
        moved = jnp.concatenate(pieces, axis=0)
        return moved[:, 0:aw], moved[:, aw:aw + LANES] + moved[:, aw + LANES:]

    o0, l0 = o0_ref[...].astype(F32), l0_ref[...]
    o1, l1 = unperm(o1_ref, l1_ref)
    o2, l2 = unperm(o2_ref, l2_ref)
    mx = jnp.maximum(jnp.maximum(l0, l1), l2)
    e0, e1, e2 = jnp.exp2(l0 - mx), jnp.exp2(l1 - mx), jnp.exp2(l2 - mx)
    inv_den = 1.0 / (e0 + e1 + e2)
    stacked = jnp.concatenate(list(_split2(e1 * inv_den)) + list(_split2(e2 * inv_den)), axis=0)
    wide = jnp.dot(stacked, e_ref[...], preferred_element_type=F32)
    term = lambda i: wide[i * rows:(i + 1) * rows]
    w1 = term(0) + term(1)
    w2 = term(2) + term(3)
    o_a = (1.0 - w1 - w2) * o0 + w1 * o1 + w2 * o2
    gate = gate_ref[...].astype(F32)
    sg = gate * _sigmoid(gate)
    aw = o_a.shape[-1]
    y = jnp.dot((o_a * sg[:, :aw]).astype(BF16), w_ref[0:aw, :], preferred_element_type=F32)
    y = y + jnp.dot((ob_ref[...].astype(F32) * sg[:, aw:]).astype(BF16), w_ref[aw:, :],
                    preferred_element_type=F32)
    _finish(y, x_ref, mod_ref, pg_ref, out_ref, d_model)


def _out_c_kernel(o_ref, gate_ref, w_ref, x_ref, mod_ref, pg_ref, out_ref, *, d_model):
    gate = gate_ref[...].astype(F32)
    o = o_ref[...].astype(F32) * (gate * _sigmoid(gate))
    y = jnp.dot(o.astype(BF16), w_ref[...], preferred_element_type=F32)
    _finish(y, x_ref, mod_ref, pg_ref, out_ref, d_model)


def _out_proj(kern, row_inputs, const_inputs, w_bf16, x2, mod3, post_g, seq, name, tile=ROW_TILE):
    t, d = x2.shape
    tiles_per_seq = seq // tile
    tiles_per_span = SPAN // tile

    def row_spec(a):
        if a.ndim == 2:
            return pl.BlockSpec((tile, a.shape[1]), lambda i: (i, 0))
        return pl.BlockSpec((None, SPAN_RES, tile // SPAN_RES, a.shape[-1]),
                            lambda i: (i // tiles_per_span, 0, i % tiles_per_span, 0))

    const_spec = lambda a: pl.BlockSpec(a.shape, lambda i: (0, 0))
    return pl.pallas_call(
        functools.partial(kern, d_model=d),
        out_shape=jax.ShapeDtypeStruct((t, d), F32),
        grid=(t // tile,),
        in_specs=([row_spec(a) for a in row_inputs] + [const_spec(a) for a in const_inputs]
                  + [const_spec(w_bf16), row_spec(x2),
                     pl.BlockSpec((1, 1, 3 * d), lambda i: (i // tiles_per_seq, 0, 0)),
                     pl.BlockSpec((1, d), lambda i: (0, 0))]),
        out_specs=pl.BlockSpec((tile, d), lambda i: (i, 0)),
        compiler_params=_cparams(("arbitrary",)),
        name=name,
    )(*row_inputs, *const_inputs, w_bf16, x2, mod3, post_g.reshape(1, d))


def _gdn_prep_kernel(x_ref, halo_ref, br_ref, cw_ref, alog_ref, dtb_ref,
                     w_ref, u_ref, qg_ref, kdt_ref, aqk_ref, dec_ref, xs_scr):
    n_all = GDN_PREP_BLOCKS * GDN_STEP
    first = pl.program_id(1) == 0
    halo = halo_ref[...]
    xs_scr[0:SUBLANES, :] = jnp.where(first, jnp.zeros_like(halo), halo)
    xs_scr[SUBLANES:SUBLANES + n_all, :] = x_ref[...]
    cw = cw_ref[...]
    ext = xs_scr[...].reshape(n_all // SUBLANES + 1, SUBLANES, x_ref.shape[1])
    sublane = lax.broadcasted_iota(jnp.int32, ext.shape, 1)

    def delay(y, k):
        rot = pltpu.roll(y, k, axis=1)
        prev = jnp.concatenate([rot[:1], rot[:-1]], axis=0)
        return jnp.where(sublane < k, prev, rot)

    assert C_CONV == 4
    ext_d = delay(ext, 1)
    acc = ext * cw[3:4] + ext_d * cw[2:3] + delay(ext * cw[1:2] + ext_d * cw[0:1], 2)
    conv = acc[1:].reshape(n_all, x_ref.shape[1])
    act = conv * _sigmoid(conv)
    _gdn_prep_blocks(act, br_ref, alog_ref, dtb_ref, w_ref, u_ref, qg_ref, kdt_ref, aqk_ref, dec_ref)


def _gdn_prep_blocks(act, br_ref, alog_ref, dtb_ref, w_ref, u_ref, qg_ref, kdt_ref, aqk_ref, dec_ref):
    n = GDN_STEP
    c = GDN_CHUNK
    row = lax.broadcasted_iota(jnp.int32, (n, n), 0)
    col = lax.broadcasted_iota(jnp.int32, (n, n), 1)
    cum = jnp.logical_and((row // c) == (col // c), row >= col).astype(F32)
    prow = lax.broadcasted_iota(jnp.int32, (c, n), 0)
    plane = lax.broadcasted_iota(jnp.int32, (c, n), 1)
    left = plane < c
    pcol = jnp.where(left, plane, plane - c)
    tril = prow >= pcol
    strict = prow > pcol
    eye = (prow == pcol).astype(F32)
    pair = lambda full: jnp.where(left, full[0:c], full[c:])

    def blockdiag(p):
        zero = jnp.zeros_like(p)
        return jnp.concatenate([jnp.where(left, p, zero), jnp.where(left, zero, p)], axis=0)

    first_chunk = lax.broadcasted_iota(jnp.int32, (n, 1), 0) < GDN_CHUNK
    dec_row = lax.broadcasted_iota(jnp.int32, (SUBLANES, LANES), 0)
    heads = range(C_HEADS)
    lanes = [slice(h * LANES, (h + 1) * LANES) for h in heads]

    def gates(blk):
        rows = slice(blk * n, (blk + 1) * n)
        br = br_ref[rows, :]
        xg = br + dtb_ref[...]
        softplus = jnp.maximum(xg, 0.0) + jnp.log(1.0 + jnp.exp(-jnp.abs(xg)))
        g_all = -jnp.exp(alog_ref[...]) * softplus
        gc_all = jnp.dot(cum, g_all, preferred_element_type=F32, precision=HI)
        g_last = jnp.where(first_chunk, gc_all[c - 1:c], gc_all[n - 1:n])
        return dict(rows=rows, beta=_sigmoid(br), gc=gc_all, gc_t=gc_all.T, eg=jnp.exp(gc_all),
                    tail=jnp.exp(g_last - gc_all))

    def first_stage(ctx, blk, h):
        rows = ctx["rows"]
        gcol = ctx["gc"][:, C_HEADS + h:C_HEADS + h + 1]
        grow = ctx["gc_t"][C_HEADS + h:C_HEADS + h + 1, :]
        eg = ctx["eg"][:, C_HEADS + h:C_HEADS + h + 1]
        gcol_p = jnp.where(left, gcol[0:c], gcol[c:])
        decay = jnp.where(tril, jnp.exp(jnp.where(tril, gcol_p - grow, 0.0)), 0.0)
        q = act[rows, h * C_DK:(h + 1) * C_DK]
        k = act[rows, (C_HEADS + h) * C_DK:(C_HEADS + h + 1) * C_DK]
        v = act[rows, 2 * C_HEADS * C_DK + h * C_DV:2 * C_HEADS * C_DK + (h + 1) * C_DV]
        q = q * (lax.rsqrt(jnp.sum(q * q, axis=-1, keepdims=True) + EPS) * (C_DK ** -0.5))
        k = k * lax.rsqrt(jnp.sum(k * k, axis=-1, keepdims=True) + EPS)
        beta = ctx["beta"][:, h:h + 1]
        kb = k * beta
        k16 = k.astype(BF16)
        a = jnp.where(strict, pair(lax.dot_general(kb.astype(BF16), k16, _NT, preferred_element_type=F32)) * decay,
                      0.0)
        aqk = pair(lax.dot_general(q.astype(BF16), k16, _NT, preferred_element_type=F32)) * decay
        aqk_ref[rows, lanes[h]] = blockdiag(aqk).astype(BF16)
        qg_ref[rows, lanes[h]] = (q * eg).astype(BF16)
        kdt_ref[rows, lanes[h]] = (k * ctx["tail"][:, C_HEADS + h:C_HEADS + h + 1]).T.astype(BF16)
        dec_ref[blk, :, lanes[h]] = jnp.where(dec_row < SUBLANES // 2, eg[GDN_CHUNK - 1:GDN_CHUNK], eg[n - 1:n])
        rhs = jnp.concatenate([v * beta, kb * eg], axis=1).astype(BF16)
        return dict(a=a, inv=eye - a, rhs=rhs)

    n_blocks = act.shape[0] // n
    first_ctx = gates(0)
    state = [first_stage(first_ctx, 0, h) for h in heads]
    for blk in range(n_blocks):
        nxt_ctx = gates(blk + 1) if blk + 1 < n_blocks else None
        nxt = []

        def side_work(hs, blk=blk, nxt=nxt, nxt_ctx=nxt_ctx):
            if nxt_ctx is not None:
                nxt.extend(first_stage(nxt_ctx, blk + 1, h) for h in hs)

        invs = [st["inv"] for st in state]
        pows = []
        for st in state:
            pows.append(jnp.dot(st["a"].astype(BF16), blockdiag(st["a"]).astype(BF16), preferred_element_type=F32))
        side_work(heads[0:2])
        for level in range(4):
            for h in heads:
                both = jnp.dot(jnp.concatenate([invs[h], pows[h]], axis=0).astype(BF16),
                               blockdiag(pows[h]).astype(BF16), preferred_element_type=F32)
                invs[h] = invs[h] + both[0:c]
                pows[h] = both[c:]
            side_work(heads[2 + level:3 + level])
        for h in heads:
            invs[h] = invs[h] + jnp.dot(invs[h].astype(BF16), blockdiag(pows[h]).astype(BF16),
                                        preferred_element_type=F32)
        side_work(heads[6:8])
        rows = slice(blk * n, (blk + 1) * n)
        for h in heads:
            uw = jnp.dot(blockdiag(invs[h]).astype(BF16), state[h]["rhs"], preferred_element_type=F32)
            u_ref[rows, lanes[h]] = uw[:, 0:C_DV].astype(u_ref.dtype)
            w_ref[rows, lanes[h]] = uw[:, C_DV:].astype(BF16)
        state = nxt


def _gdn_rec_kernel(w_ref, u_ref, qg_ref, kdt_ref, aqk_ref, dec_ref, ng_ref, o_ref, s_scr):
    @pl.when(pl.program_id(0) == 0)
    def _():
        s_scr[...] = jnp.zeros_like(s_scr)

    half = SUBLANES // 2
    zeros = jnp.zeros((GDN_CHUNK, C_DV), BF16)
    lanes = [slice(h * LANES, (h + 1) * LANES) for h in range(C_HEADS)]
    chains = [(b, h) for b in range(w_ref.shape[0]) for h in range(C_HEADS)]
    states = [s_scr[b * C_HEADS + h] for b, h in chains]
    for blk in range(GDN_REC_BLOCKS):
        blk_rows = slice(blk * GDN_STEP, (blk + 1) * GDN_STEP)
        for j in range(GDN_STEP // GDN_CHUNK):
            rows = slice(blk * GDN_STEP + j * GDN_CHUNK, blk * GDN_STEP + (j + 1) * GDN_CHUNK)
            wss = []
            for c, (b, h) in enumerate(chains):
                wq = jnp.concatenate([w_ref[b, rows, lanes[h]], qg_ref[b, rows, lanes[h]]], axis=0)
                wss.append(jnp.dot(wq, states[c].astype(BF16), preferred_element_type=F32))
            for c, (b, h) in enumerate(chains):
                v_new = (u_ref[b, rows, lanes[h]].astype(F32) - wss[c][0:GDN_CHUNK]).astype(BF16)
                v_pad = jnp.concatenate([v_new, zeros] if j == 0 else [zeros, v_new], axis=0)
                both = jnp.dot(jnp.concatenate([aqk_ref[b, rows, lanes[h]], kdt_ref[b, blk_rows, lanes[h]]], axis=0),
                               v_pad, preferred_element_type=F32)
                o = wss[c][GDN_CHUNK:] + both[0:GDN_CHUNK]
                dec = dec_ref[b, blk, j * half:j * half + 1, lanes[h]]
                states[c] = states[c] * dec + both[GDN_CHUNK:]
                ms = jnp.mean(o * o, axis=-1, keepdims=True)
                o_ref[b, rows, lanes[h]] = (o * lax.rsqrt(ms + EPS) * ng_ref[...]).astype(o_ref.dtype)
    for c, (b, h) in enumerate(chains):
        s_scr[b * C_HEADS + h] = states[c]


def _gdn_core(qkv_pre, br, bsz, seq, conv_w, a_log, dt_bias, norm_g):
    t, qkv_w = qkv_pre.shape
    hw = C_HEADS * C_DV
    steps_per_seq = seq // GDN_STEP
    prep_rows = GDN_PREP_BLOCKS * GDN_STEP
    prep_steps = steps_per_seq // GDN_PREP_BLOCKS
    halo_blocks = prep_rows // SUBLANES
    pad_row = lambda vec: jnp.zeros((1, LANES), F32).at[0, C_HEADS:2 * C_HEADS].set(vec.astype(F32))
    tok = lambda b, i: (b * prep_steps + i, 0)
    const = lambda b, i: (0, 0)
    tok_spec = pl.BlockSpec((prep_rows, hw), tok)
    dec_spec = pl.BlockSpec((GDN_PREP_BLOCKS, SUBLANES, hw), lambda b, i: (b * prep_steps + i, 0, 0))
    w, u, qg, kdt, aqk, dec = pl.pallas_call(
        _gdn_prep_kernel,
        out_shape=[jax.ShapeDtypeStruct((t, hw), BF16), jax.ShapeDtypeStruct((t, hw), BF16),
                   jax.ShapeDtypeStruct((t, hw), BF16), jax.ShapeDtypeStruct((t, hw), BF16),
                   jax.ShapeDtypeStruct((t, hw), BF16),
                   jax.ShapeDtypeStruct((t // GDN_STEP, SUBLANES, hw), F32)],
        grid=(bsz, prep_steps),
        in_specs=[pl.BlockSpec((prep_rows, qkv_w), tok),
                  pl.BlockSpec((SUBLANES, qkv_w),
                               lambda b, i: (jnp.maximum((b * prep_steps + i) * halo_blocks - 1, 0), 0)),
                  pl.BlockSpec((prep_rows, LANES), tok),
                  pl.BlockSpec((C_CONV, qkv_w), const),
                  pl.BlockSpec((1, LANES), const),
                  pl.BlockSpec((1, LANES), const)],
        out_specs=[tok_spec, tok_spec, tok_spec, tok_spec, tok_spec, dec_spec],
        scratch_shapes=[pltpu.VMEM((SUBLANES + prep_rows, qkv_w), F32)],
        compiler_params=_cparams(("arbitrary", "arbitrary")),
        name="gdn_prep",
    )(qkv_pre, qkv_pre, br, conv_w.astype(F32), pad_row(a_log), pad_row(dt_bias))
    rec_steps = steps_per_seq // GDN_REC_BLOCKS
    seq_view = lambda a: a.reshape(bsz, seq, hw)
    rec_tok = pl.BlockSpec((bsz, GDN_REC_BLOCKS * GDN_STEP, hw), lambda i: (0, i, 0))
    rec_dec = pl.BlockSpec((bsz, GDN_REC_BLOCKS, SUBLANES, hw), lambda i: (0, i, 0, 0))
    o = pl.pallas_call(
        _gdn_rec_kernel,
        out_shape=jax.ShapeDtypeStruct((bsz, seq, hw), BF16),
        grid=(rec_steps,),
        in_specs=[rec_tok, rec_tok, rec_tok, rec_tok, rec_tok, rec_dec,
                  pl.BlockSpec((1, C_DV), lambda i: (0, 0))],
        out_specs=rec_tok,
        scratch_shapes=[pltpu.VMEM((bsz * C_HEADS, C_DK, C_DV), F32)],
        compiler_params=_cparams(("arbitrary",)),
        name="gdn_recurrence",
    )(seq_view(w), seq_view(u), seq_view(qg), seq_view(kdt), seq_view(aqk),
      dec.reshape(bsz, steps_per_seq, SUBLANES, hw), norm_g.reshape(1, C_DV).astype(F32))
    return o.reshape(t, hw)


def _ab_layer(x2, mod3, bsz, seq, pre_g, post_g, rel_bias, w_in, w_out, s5_params):
    b_width = s5_params[7].shape[-1]
    assert DILATED_CONFIGS[0][1] == 1 and all(SPAN_RES % dl == 0 for _, dl in DILATED_CONFIGS) and seq % SPAN == 0
    head_expand = jnp.asarray(np.arange(LANES)[:, None] == (np.arange(A_WIDTH)[None, :] // A_HEAD_DIM), BF16)
    unperm = jnp.asarray(_span_perm().T, BF16)
    splits = ((0, A_WIDTH, A_HEAD_DIM ** -0.5 * LOG2_E), (A_WIDTH, A_WIDTH, 1.0), (2 * A_WIDTH, A_WIDTH, 1.0),
              (3 * A_WIDTH, b_width, 1.0), (3 * A_WIDTH + b_width, A_WIDTH + b_width, 1.0))
    q, k, v, u, gate, qs, ks, vs = _in_proj(x2, mod3, pre_g, w_in.astype(BF16), splits,
                                            (BF16, BF16, BF16, F32, BF16), seq, n_span=3, tile=AB_TILE)
    outs, lses = [], []
    bias_tabs = _attn_bias_tables(rel_bias)
    for cfg, (_, dil) in enumerate(DILATED_CONFIGS):
        qkv = (q, k, v) if dil == 1 else (qs, ks, vs)
        o_c, lse_c = _dilated_attention_one(*qkv, bias_tabs, cfg, bsz, seq, dil)
        outs.append(o_c)
        lses.append(lse_c)
    o_b = _s5_layer(u, bsz, seq, *s5_params)
    row_inputs = [outs[0], lses[0], outs[1], outs[2], lses[1], lses[2], o_b, gate]
    return _out_proj(_out_ab_kernel, row_inputs, [head_expand, unperm], w_out.astype(BF16),
                     x2, mod3, post_g, seq, "out_proj_ab", tile=AB_TILE)


def _gdn_layer(x2, mod3, bsz, seq, pre_g, post_g, w_in, conv_w, a_log, dt_bias, norm_g, w_out):
    d = x2.shape[1]
    qkv_w = 2 * C_HEADS * C_DK + C_HEADS * C_DV
    gate_w = C_HEADS * C_DV
    w_pad = jnp.concatenate(
        [w_in, jnp.zeros((d, LANES - (w_in.shape[1] - qkv_w - gate_w)), w_in.dtype)], axis=1).astype(BF16)
    splits = ((0, qkv_w, 1.0), (qkv_w, gate_w, 1.0), (qkv_w + gate_w, LANES, 1.0))
    qkv_pre, gate, br = _in_proj(x2, mod3, pre_g, w_pad, splits, (F32, BF16, F32), seq)
    o = _gdn_core(qkv_pre, br, bsz, seq, conv_w, a_log, dt_bias, norm_g)
    return _out_proj(_out_c_kernel, [o, gate], [], w_out.astype(BF16), x2, mod3, post_g, seq, "out_proj_c",
                     tile=OUT_C_TILE)


def kernel(x, c, ada_w, ada_b, pre_g, post_g, rel_bias, ab_w_in, ab_w_out, s5_a_re, s5_a_im, s5_log_dt, s5_b_re, s5_b_im, s5_c_re, s5_c_im, s5_d, s5_glu_w, s5_glu_b, gdn_w_in, gdn_conv, gdn_a_log, gdn_dt_bias, gdn_norm_g, gdn_w_out):
    bsz, seq, d = x.shape
    depth = ada_w.shape[0]
    assert seq % ROW_TILE == 0 and seq % AB_TILE == 0 and seq % OUT_C_TILE == 0
    assert seq % (SUBLANES * S5_J * S5_TILES_PER_STEP) == 0 and AB_TILE % PERM_ROWS == 0 and SPAN % AB_TILE == 0
    assert seq % (GDN_STEP * GDN_PREP_BLOCKS) == 0 and seq % (GDN_STEP * GDN_REC_BLOCKS) == 0
    x2 = x.reshape(bsz * seq, d)
    mod = _adaln_mod(c, ada_w, ada_b)
    for layer in range(depth):
        j = layer // 2
        mod3 = mod[layer].reshape(bsz, 1, 3 * d)
        if layer % 2 == 0:
            s5_params = (s5_a_re[j], s5_a_im[j], s5_log_dt[j], s5_b_re[j], s5_b_im[j], s5_c_re[j], s5_c_im[j],
                         s5_d[j], s5_glu_w[j], s5_glu_b[j])
            x2 = _ab_layer(x2, mod3, bsz, seq, pre_g[layer], post_g[layer], rel_bias, ab_w_in[j], ab_w_out[j],
                           s5_params)
        else:
            x2 = _gdn_layer(x2, mod3, bsz, seq, pre_g[layer], post_g[layer], gdn_w_in[j], gdn_conv[j],
                            gdn_a_log[j], gdn_dt_bias[j], gdn_norm_g[j], gdn_w_out[j])
    return x2.reshape(bsz, seq, d)
```
